```python
import math
import jax, jax.numpy as jnp
from jax import lax
import numpy as np

D_MODEL = 1024
BATCH = 8
SEQ = 4096
DEPTH = 1

SSD_EXPAND = 2
D_INNER = SSD_EXPAND * D_MODEL
SSD_HEAD_DIM = 64
SSD_HEADS = D_INNER // SSD_HEAD_DIM
SSD_GROUPS = 8
D_STATE = 128
CONV_K = 4
CONV_DIM = D_INNER + 2 * SSD_GROUPS * D_STATE
SSD_CHUNK = 128
SB_HEADS = 16
SB_HEAD_DIM = 64
SB_WIDTH = SB_HEADS * SB_HEAD_DIM
SB_BLOCK = 128
D_FF = 4 * D_MODEL
PLE_DIM = 256
N_BRANCHES = 2
RMS_EPS = 1e-6
IN_PROJ_DIM = D_INNER + CONV_DIM + SSD_HEADS + 3 * SB_WIDTH
SPLITS = [D_INNER, D_INNER + CONV_DIM, D_INNER + CONV_DIM + SSD_HEADS,
          D_INNER + CONV_DIM + SSD_HEADS + SB_WIDTH,
          D_INNER + CONV_DIM + SSD_HEADS + 2 * SB_WIDTH]

kernel_name = "hybrid_ssd_stickbreak_gated_block"


def rms_norm(x, w):
    xf = x.astype(jnp.float32)
    y = xf * lax.rsqrt(jnp.mean(xf * xf, axis=-1, keepdims=True) + RMS_EPS)
    return (y * w.astype(jnp.float32)).astype(x.dtype)


def causal_depthwise_conv(u, w, b):
    k = w.shape[0]
    c = u.shape[-1]
    y = lax.conv_general_dilated(u, w[:, None, :].astype(u.dtype), window_strides=(1,),
                                 padding=[(k - 1, 0)],
                                 dimension_numbers=("NWC", "WIO", "NWC"),
                                 feature_group_count=c)
    return y + b.astype(u.dtype)


def ssd_chunked_scan(xs, dt, a, bmat, cmat):
    bsz, seqlen, nh, hd = xs.shape
    ng, ns = bmat.shape[2], bmat.shape[3]
    r = nh // ng
    nc = seqlen // SSD_CHUNK
    shp = (bsz, nc, SSD_CHUNK, ng, r)
    xd = (xs * dt[..., None]).reshape(shp + (hd,))
    a_cs = jnp.cumsum((dt * a).reshape(shp), axis=2)
    bc = bmat.reshape(bsz, nc, SSD_CHUNK, ng, ns)
    cc = cmat.reshape(bsz, nc, SSD_CHUNK, ng, ns)
    seg = a_cs[:, :, :, None] - a_cs[:, :, None, :]
    causal = jnp.tril(jnp.ones((SSD_CHUNK, SSD_CHUNK), dtype=bool))[None, None, :, :, None, None]
    decay = jnp.exp(jnp.where(causal, seg, -jnp.inf))
    cb = jnp.einsum("bclgn,bcsgn->bclsg", cc, bc)
    y_diag = jnp.einsum("bclsgr,bcsgrp->bclgrp", cb[..., None] * decay, xd)
    decay_to_end = jnp.exp(a_cs[:, :, -1:] - a_cs)
    states = jnp.einsum("bcsgn,bcsgrp->bcgrpn", bc, xd * decay_to_end[..., None])
    chunk_decay = jnp.exp(a_cs[:, :, -1])

    def step(carry, inp):
        st, dec = inp
        return carry * dec[..., None, None] + st, carry

    init = jnp.zeros((bsz, ng, r, hd, ns), jnp.float32)
    _, prev = lax.scan(step, init, (jnp.moveaxis(states, 1, 0), jnp.moveaxis(chunk_decay, 1, 0)))
    prev = jnp.moveaxis(prev, 0, 1)
    y_off = jnp.einsum("bclgn,bcgrpn->bclgrp", cc, prev) * jnp.exp(a_cs)[..., None]
    return (y_diag + y_off).reshape(bsz, seqlen, nh, hd)


def ssd_mixer(z, xbc, dt_raw, conv_w, conv_b, dt_bias, a_log, d_skip, norm_w):
    bsz, seqlen, _ = z.shape
    xbc = jax.nn.silu(causal_depthwise_conv(xbc, conv_w, conv_b))
    xs, bmat, cmat = jnp.split(xbc, [D_INNER, D_INNER + SSD_GROUPS * D_STATE], axis=-1)
    xs = xs.astype(jnp.float32).reshape(bsz, seqlen, SSD_HEADS, SSD_HEAD_DIM)
    bmat = bmat.astype(jnp.float32).reshape(bsz, seqlen, SSD_GROUPS, D_STATE)
    cmat = cmat.astype(jnp.float32).reshape(bsz, seqlen, SSD_GROUPS, D_STATE)
    dt = jax.nn.softplus(dt_raw.astype(jnp.float32) + dt_bias.astype(jnp.float32))
    a = -jnp.exp(a_log.astype(jnp.float32))
    y = ssd_chunked_scan(xs, dt, a, bmat, cmat) + xs * d_skip.astype(jnp.float32)[:, None]
    y = y.reshape(bsz, seqlen, D_INNER) * jax.nn.silu(z.astype(jnp.float32))
    yg = y.reshape(bsz, seqlen, SSD_GROUPS, D_INNER // SSD_GROUPS)
    yg = yg * lax.rsqrt(jnp.mean(yg * yg, axis=-1, keepdims=True) + RMS_EPS)
    y = yg.reshape(bsz, seqlen, D_INNER) * norm_w.astype(jnp.float32)
    return y.astype(z.dtype)


def stick_breaking_attention(q, k, v):
    seqlen, hd = q.shape[2], q.shape[3]
    scale = hd ** -0.5
    outs = []
    for blk in range(seqlen // SB_BLOCK):
        t0 = blk * SB_BLOCK
        t1 = t0 + SB_BLOCK
        logits = jnp.einsum("bhtd,bhsd->bhts", q[:, :, t0:t1], k[:, :, :t1]).astype(jnp.float32) * scale
        mask = jnp.arange(t1)[None, :] < jnp.arange(t0, t1)[:, None]
        log_keep = jnp.where(mask, jax.nn.log_sigmoid(-logits), 0.0)
        later = jnp.flip(jnp.cumsum(jnp.flip(log_keep, -1), axis=-1), -1) - log_keep
        weights = jnp.where(mask, jnp.exp(jax.nn.log_sigmoid(logits) + later), 0.0)
        outs.append(jnp.einsum("bhts,bhsd->bhtd", weights, v[:, :, :t1].astype(jnp.float32)))
    return jnp.concatenate(outs, axis=2)


def _fwd_setup_inputs(seed: int = 0) -> dict:
    key = jax.random.key(seed)
    ks = jax.random.split(key, 26)
    f32 = jnp.float32

    def nrm(k, shape, fan_in):
        return jax.random.normal(k, shape, f32) * (fan_in ** -0.5)

    def gain(k, shape):
        return 1.0 + 0.05 * jax.random.normal(k, shape, f32)

    dt0 = jnp.exp(jax.random.uniform(ks[6], (DEPTH, SSD_HEADS), f32)
                  * (math.log(0.1) - math.log(0.001)) + math.log(0.001))
    dt_bias = dt0 + jnp.log(-jnp.expm1(-dt0))
    a_log = jnp.log(jax.random.uniform(ks[7], (DEPTH, SSD_HEADS), f32, 1.0, 16.0))
    return {
        "x": jax.random.normal(ks[0], (BATCH, SEQ, D_MODEL), f32),
        "p": jax.random.normal(ks[1], (DEPTH, BATCH, SEQ, PLE_DIM), f32),
        "norm_mix_pre": gain(ks[2], (DEPTH, D_MODEL)),
        "w_in": nrm(ks[3], (DEPTH, D_MODEL, IN_PROJ_DIM), D_MODEL),
        "conv_w": nrm(ks[4], (DEPTH, CONV_K, CONV_DIM), CONV_K),
        "conv_b": 0.01 * jax.random.normal(ks[5], (DEPTH, CONV_DIM), f32),
        "dt_bias": dt_bias,
        "a_log": a_log,
        "d_skip": 1.0 + 0.1 * jax.random.normal(ks[8], (DEPTH, SSD_HEADS), f32),
        "ssd_norm": gain(ks[9], (DEPTH, D_INNER)),
        "w_ssd_branch": nrm(ks[10], (DEPTH, D_INNER, D_MODEL), D_INNER),
        "w_sb_branch": nrm(ks[11], (DEPTH, SB_WIDTH, D_MODEL), SB_WIDTH),
        "w_gate": nrm(ks[12], (DEPTH, D_MODEL, N_BRANCHES * D_MODEL), D_MODEL),
        "b_gate": 0.01 * jax.random.normal(ks[13], (DEPTH, N_BRANCHES * D_MODEL), f32),
        "w_out": nrm(ks[14], (DEPTH, D_MODEL, D_MODEL), D_MODEL),
        "norm_mix_post": gain(ks[15], (DEPTH, D_MODEL)),
        "norm_ffn_pre": gain(ks[16], (DEPTH, D_MODEL)),
        "w_ff1": nrm(ks[17], (DEPTH, D_MODEL, D_FF), D_MODEL),
        "w_ff2": nrm(ks[18], (DEPTH, D_FF, D_MODEL), D_FF),
        "norm_ffn_post": gain(ks[19], (DEPTH, D_MODEL)),
        "w_ple": nrm(ks[20], (DEPTH, PLE_DIM, D_MODEL), PLE_DIM),
        "w_ple_gate": nrm(ks[21], (DEPTH, D_MODEL, D_MODEL), D_MODEL),
        "norm_ple_post": gain(ks[22], (DEPTH, D_MODEL)),
    }


def _fwd_reference(x, p, norm_mix_pre, w_in, conv_w, conv_b, dt_bias, a_log, d_skip, ssd_norm,
              w_ssd_branch, w_sb_branch, w_gate, b_gate, w_out, norm_mix_post,
              norm_ffn_pre, w_ff1, w_ff2, norm_ffn_post, w_ple, w_ple_gate, norm_ple_post):
    h = x
    bsz, seqlen, _ = x.shape
    for i in range(DEPTH):
        n1 = rms_norm(h, norm_mix_pre[i])
        proj = n1 @ w_in[i]
        z, xbc, dt_raw, q, k, v = jnp.split(proj, SPLITS, axis=-1)
        y_ssd = ssd_mixer(z, xbc, dt_raw, conv_w[i], conv_b[i], dt_bias[i], a_log[i],
                          d_skip[i], ssd_norm[i])
        heads = lambda t: t.reshape(bsz, seqlen, SB_HEADS, SB_HEAD_DIM).transpose(0, 2, 1, 3)
        y_sb = stick_breaking_attention(heads(q), heads(k), heads(v))
        y_sb = y_sb.transpose(0, 2, 1, 3).reshape(bsz, seqlen, SB_WIDTH).astype(h.dtype)
        gates = jax.nn.sigmoid((n1 @ w_gate[i] + b_gate[i]).astype(jnp.float32)).astype(h.dtype)
        g_ssd, g_sb = jnp.split(gates, N_BRANCHES, axis=-1)
        merged = g_ssd * (y_ssd @ w_ssd_branch[i]) + g_sb * (y_sb @ w_sb_branch[i])
        h = h + rms_norm(merged @ w_out[i], norm_mix_post[i])
        n2 = rms_norm(h, norm_ffn_pre[i])
        ff = jnp.square(jax.nn.relu(n2 @ w_ff1[i])) @ w_ff2[i]
        h = h + rms_norm(ff, norm_ffn_post[i])
        ple_gate = jax.nn.sigmoid((h @ w_ple_gate[i]).astype(jnp.float32)).astype(h.dtype)
        h = h + rms_norm(ple_gate * (p[i].astype(h.dtype) @ w_ple[i]), norm_ple_post[i])
    return h


import jax as _jax
import jax.numpy as _jnp

TWIN_FORMAT = 'train_step'
FWD_PARAMS = ['x', 'p', 'norm_mix_pre', 'w_in', 'conv_w', 'conv_b', 'dt_bias', 'a_log', 'd_skip', 'ssd_norm', 'w_ssd_branch', 'w_sb_branch', 'w_gate', 'b_gate', 'w_out', 'norm_mix_post', 'norm_ffn_pre', 'w_ff1', 'w_ff2', 'norm_ffn_post', 'w_ple', 'w_ple_gate', 'norm_ple_post']
TWIN_WEIGHTS = ['norm_mix_pre', 'w_in', 'conv_w', 'conv_b', 'dt_bias', 'a_log', 'd_skip', 'ssd_norm', 'w_ssd_branch', 'w_sb_branch', 'w_gate', 'b_gate', 'w_out', 'norm_mix_post', 'norm_ffn_pre', 'w_ff1', 'w_ff2', 'norm_ffn_post', 'w_ple', 'w_ple_gate', 'norm_ple_post']
TWIN_DIFF_INPUT = 'x'
TWIN_INPUTS = ['x', 'p', 'norm_mix_pre', 'w_in', 'conv_w', 'conv_b', 'dt_bias', 'a_log', 'd_skip', 'ssd_norm', 'w_ssd_branch', 'w_sb_branch', 'w_gate', 'b_gate', 'w_out', 'norm_mix_post', 'norm_ffn_pre', 'w_ff1', 'w_ff2', 'norm_ffn_post', 'w_ple', 'w_ple_gate', 'norm_ple_post', 'loss_target', 'm_norm_mix_pre', 'm_w_in', 'm_conv_w', 'm_conv_b', 'm_dt_bias', 'm_a_log', 'm_d_skip', 'm_ssd_norm', 'm_w_ssd_branch', 'm_w_sb_branch', 'm_w_gate', 'm_b_gate', 'm_w_out', 'm_norm_mix_post', 'm_norm_ffn_pre', 'm_w_ff1', 'm_w_ff2', 'm_norm_ffn_post', 'm_w_ple', 'm_w_ple_gate', 'm_norm_ple_post', 'v_norm_mix_pre', 'v_w_in', 'v_conv_w', 'v_conv_b', 'v_dt_bias', 'v_a_log', 'v_d_skip', 'v_ssd_norm', 'v_w_ssd_branch', 'v_w_sb_branch', 'v_w_gate', 'v_b_gate', 'v_w_out', 'v_norm_mix_post', 'v_norm_ffn_pre', 'v_w_ff1', 'v_w_ff2', 'v_norm_ffn_post', 'v_w_ple', 'v_w_ple_gate', 'v_norm_ple_post']
TWIN_OUTPUTS = ['loss', 'grad_x', 'grad_norm_mix_pre', 'grad_w_in', 'grad_conv_w', 'grad_conv_b', 'grad_dt_bias', 'grad_a_log', 'grad_d_skip', 'grad_ssd_norm', 'grad_w_ssd_branch', 'grad_w_sb_branch', 'grad_w_gate', 'grad_b_gate', 'grad_w_out', 'grad_norm_mix_post', 'grad_norm_ffn_pre', 'grad_w_ff1', 'grad_w_ff2', 'grad_norm_ffn_post', 'grad_w_ple', 'grad_w_ple_gate', 'grad_norm_ple_post', 'delta_norm_mix_pre', 'delta_w_in', 'delta_conv_w', 'delta_conv_b', 'delta_dt_bias', 'delta_a_log', 'delta_d_skip', 'delta_ssd_norm', 'delta_w_ssd_branch', 'delta_w_sb_branch', 'delta_w_gate', 'delta_b_gate', 'delta_w_out', 'delta_norm_mix_post', 'delta_norm_ffn_pre', 'delta_w_ff1', 'delta_w_ff2', 'delta_norm_ffn_post', 'delta_w_ple', 'delta_w_ple_gate', 'delta_norm_ple_post', 'new_m_norm_mix_pre', 'new_m_w_in', 'new_m_conv_w', 'new_m_conv_b', 'new_m_dt_bias', 'new_m_a_log', 'new_m_d_skip', 'new_m_ssd_norm', 'new_m_w_ssd_branch', 'new_m_w_sb_branch', 'new_m_w_gate', 'new_m_b_gate', 'new_m_w_out', 'new_m_norm_mix_post', 'new_m_norm_ffn_pre', 'new_m_w_ff1', 'new_m_w_ff2', 'new_m_norm_ffn_post', 'new_m_w_ple', 'new_m_w_ple_gate', 'new_m_norm_ple_post', 'new_v_norm_mix_pre', 'new_v_w_in', 'new_v_conv_w', 'new_v_conv_b', 'new_v_dt_bias', 'new_v_a_log', 'new_v_d_skip', 'new_v_ssd_norm', 'new_v_w_ssd_branch', 'new_v_w_sb_branch', 'new_v_w_gate', 'new_v_b_gate', 'new_v_w_out', 'new_v_norm_mix_post', 'new_v_norm_ffn_pre', 'new_v_w_ff1', 'new_v_w_ff2', 'new_v_norm_ffn_post', 'new_v_w_ple', 'new_v_w_ple_gate', 'new_v_norm_ple_post']
TWIN_LEAF_KINDS = {'loss': 'loss', 'grad_x': 'grad_x', 'grad_norm_mix_pre': 'grad_w', 'grad_w_in': 'grad_w', 'grad_conv_w': 'grad_w', 'grad_conv_b': 'grad_w', 'grad_dt_bias': 'grad_w', 'grad_a_log': 'grad_w', 'grad_d_skip': 'grad_w', 'grad_ssd_norm': 'grad_w', 'grad_w_ssd_branch': 'grad_w', 'grad_w_sb_branch': 'grad_w', 'grad_w_gate': 'grad_w', 'grad_b_gate': 'grad_w', 'grad_w_out': 'grad_w', 'grad_norm_mix_post': 'grad_w', 'grad_norm_ffn_pre': 'grad_w', 'grad_w_ff1': 'grad_w', 'grad_w_ff2': 'grad_w', 'grad_norm_ffn_post': 'grad_w', 'grad_w_ple': 'grad_w', 'grad_w_ple_gate': 'grad_w', 'grad_norm_ple_post': 'grad_w', 'delta_norm_mix_pre': 'delta_w', 'delta_w_in': 'delta_w', 'delta_conv_w': 'delta_w', 'delta_conv_b': 'delta_w', 'delta_dt_bias': 'delta_w', 'delta_a_log': 'delta_w', 'delta_d_skip': 'delta_w', 'delta_ssd_norm': 'delta_w', 'delta_w_ssd_branch': 'delta_w', 'delta_w_sb_branch': 'delta_w', 'delta_w_gate': 'delta_w', 'delta_b_gate': 'delta_w', 'delta_w_out': 'delta_w', 'delta_norm_mix_post': 'delta_w', 'delta_norm_ffn_pre': 'delta_w', 'delta_w_ff1': 'delta_w', 'delta_w_ff2': 'delta_w', 'delta_norm_ffn_post': 'delta_w', 'delta_w_ple': 'delta_w', 'delta_w_ple_gate': 'delta_w', 'delta_norm_ple_post': 'delta_w', 'new_m_norm_mix_pre': 'new_m', 'new_m_w_in': 'new_m', 'new_m_conv_w': 'new_m', 'new_m_conv_b': 'new_m', 'new_m_dt_bias': 'new_m', 'new_m_a_log': 'new_m', 'new_m_d_skip': 'new_m', 'new_m_ssd_norm': 'new_m', 'new_m_w_ssd_branch': 'new_m', 'new_m_w_sb_branch': 'new_m', 'new_m_w_gate': 'new_m', 'new_m_b_gate': 'new_m', 'new_m_w_out': 'new_m', 'new_m_norm_mix_post': 'new_m', 'new_m_norm_ffn_pre': 'new_m', 'new_m_w_ff1': 'new_m', 'new_m_w_ff2': 'new_m', 'new_m_norm_ffn_post': 'new_m', 'new_m_w_ple': 'new_m', 'new_m_w_ple_gate': 'new_m', 'new_m_norm_ple_post': 'new_m', 'new_v_norm_mix_pre': 'new_v', 'new_v_w_in': 'new_v', 'new_v_conv_w': 'new_v', 'new_v_conv_b': 'new_v', 'new_v_dt_bias': 'new_v', 'new_v_a_log': 'new_v', 'new_v_d_skip': 'new_v', 'new_v_ssd_norm': 'new_v', 'new_v_w_ssd_branch': 'new_v', 'new_v_w_sb_branch': 'new_v', 'new_v_w_gate': 'new_v', 'new_v_b_gate': 'new_v', 'new_v_w_out': 'new_v', 'new_v_norm_mix_post': 'new_v', 'new_v_norm_ffn_pre': 'new_v', 'new_v_w_ff1': 'new_v', 'new_v_w_ff2': 'new_v', 'new_v_norm_ffn_post': 'new_v', 'new_v_w_ple': 'new_v', 'new_v_w_ple_gate': 'new_v', 'new_v_norm_ple_post': 'new_v'}


def _forward(args):
    return _fwd_reference(*[args[k] for k in FWD_PARAMS])


def _output_shape():
    def fwd():
        inp = _fwd_setup_inputs(0)
        return _fwd_reference(*[inp[k] for k in FWD_PARAMS])
    out = _jax.eval_shape(fwd)
    return out.shape, out.dtype

N_MICROBATCH = 1
ADAM_LR = 0.001
ADAM_B1 = 0.9
ADAM_B2 = 0.999
ADAM_EPS = 1e-08
ADAM_WD = 0.01
ADAM_STEP = 10
PER_EXAMPLE_BATCH_AXIS = {'x': 0, 'p': 1, 'loss_target': 0}
SHARED_INPUTS = []
_WEIGHT_DTYPES = {'norm_mix_pre': _jnp.float32, 'w_in': _jnp.float32, 'conv_w': _jnp.float32, 'conv_b': _jnp.float32, 'dt_bias': _jnp.float32, 'a_log': _jnp.float32, 'd_skip': _jnp.float32, 'ssd_norm': _jnp.float32, 'w_ssd_branch': _jnp.float32, 'w_sb_branch': _jnp.float32, 'w_gate': _jnp.float32, 'b_gate': _jnp.float32, 'w_out': _jnp.float32, 'norm_mix_post': _jnp.float32, 'norm_ffn_pre': _jnp.float32, 'w_ff1': _jnp.float32, 'w_ff2': _jnp.float32, 'norm_ffn_post': _jnp.float32, 'w_ple': _jnp.float32, 'w_ple_gate': _jnp.float32, 'norm_ple_post': _jnp.float32}
MOMENT_SCALE = {'norm_mix_pre': 8.895748e-01, 'w_in': 2.566886e-01, 'conv_w': 6.931537e-01, 'conv_b': 2.441954e+00, 'dt_bias': 6.672395e-01, 'a_log': 5.118390e+00, 'd_skip': 2.607091e+00, 'ssd_norm': 1.610384e+00, 'w_ssd_branch': 2.117052e+00, 'w_sb_branch': 3.040038e-01, 'w_gate': 1.416118e-01, 'b_gate': 6.650294e-01, 'w_out': 2.086303e+00, 'norm_mix_post': 3.224975e+01, 'norm_ffn_pre': 1.024823e+00, 'w_ff1': 5.205212e-01, 'w_ff2': 2.287181e+00, 'norm_ffn_post': 3.322742e+01, 'w_ple': 3.820217e-01, 'w_ple_gate': 3.232389e-01, 'norm_ple_post': 3.293269e+01}


def _to_microbatches(a, axis):
    t = _jnp.moveaxis(a, axis, 0)
    t = t.reshape((N_MICROBATCH, t.shape[0] // N_MICROBATCH) + t.shape[1:])
    return _jnp.moveaxis(t, 1, axis + 1)


def setup_inputs(seed: int = 0) -> dict:
    inp = _fwd_setup_inputs(seed)
    key = _jax.random.fold_in(_jax.random.key(seed), 7919)
    shape, _ = _output_shape()
    out = dict(inp)
    out["loss_target"] = _jax.random.normal(_jax.random.fold_in(key, 0), shape, _jnp.float32)
    for i, name in enumerate(TWIN_WEIGHTS):
        w = inp[name].astype(_jnp.float32)
        if MOMENT_SCALE is None:
            s = _jnp.sqrt(_jnp.mean(_jnp.square(w)) + 1e-30)
        else:
            s = MOMENT_SCALE[name]
        km, kv = _jax.random.split(_jax.random.fold_in(key, i + 1))
        out[name] = w
        out["m_" + name] = s * _jax.random.normal(km, w.shape, _jnp.float32)
        out["v_" + name] = (s * s) * _jax.random.uniform(kv, w.shape, _jnp.float32, 0.5, 1.5)
    if N_MICROBATCH > 1:
        for name, axis in PER_EXAMPLE_BATCH_AXIS.items():
            out[name] = _to_microbatches(out[name], axis)
    return {'x': out['x'], 'p': out['p'], 'norm_mix_pre': out['norm_mix_pre'], 'w_in': out['w_in'], 'conv_w': out['conv_w'], 'conv_b': out['conv_b'], 'dt_bias': out['dt_bias'], 'a_log': out['a_log'], 'd_skip': out['d_skip'], 'ssd_norm': out['ssd_norm'], 'w_ssd_branch': out['w_ssd_branch'], 'w_sb_branch': out['w_sb_branch'], 'w_gate': out['w_gate'], 'b_gate': out['b_gate'], 'w_out': out['w_out'], 'norm_mix_post': out['norm_mix_post'], 'norm_ffn_pre': out['norm_ffn_pre'], 'w_ff1': out['w_ff1'], 'w_ff2': out['w_ff2'], 'norm_ffn_post': out['norm_ffn_post'], 'w_ple': out['w_ple'], 'w_ple_gate': out['w_ple_gate'], 'norm_ple_post': out['norm_ple_post'], 'loss_target': out['loss_target'], 'm_norm_mix_pre': out['m_norm_mix_pre'], 'm_w_in': out['m_w_in'], 'm_conv_w': out['m_conv_w'], 'm_conv_b': out['m_conv_b'], 'm_dt_bias': out['m_dt_bias'], 'm_a_log': out['m_a_log'], 'm_d_skip': out['m_d_skip'], 'm_ssd_norm': out['m_ssd_norm'], 'm_w_ssd_branch': out['m_w_ssd_branch'], 'm_w_sb_branch': out['m_w_sb_branch'], 'm_w_gate': out['m_w_gate'], 'm_b_gate': out['m_b_gate'], 'm_w_out': out['m_w_out'], 'm_norm_mix_post': out['m_norm_mix_post'], 'm_norm_ffn_pre': out['m_norm_ffn_pre'], 'm_w_ff1': out['m_w_ff1'], 'm_w_ff2': out['m_w_ff2'], 'm_norm_ffn_post': out['m_norm_ffn_post'], 'm_w_ple': out['m_w_ple'], 'm_w_ple_gate': out['m_w_ple_gate'], 'm_norm_ple_post': out['m_norm_ple_post'], 'v_norm_mix_pre': out['v_norm_mix_pre'], 'v_w_in': out['v_w_in'], 'v_conv_w': out['v_conv_w'], 'v_conv_b': out['v_conv_b'], 'v_dt_bias': out['v_dt_bias'], 'v_a_log': out['v_a_log'], 'v_d_skip': out['v_d_skip'], 'v_ssd_norm': out['v_ssd_norm'], 'v_w_ssd_branch': out['v_w_ssd_branch'], 'v_w_sb_branch': out['v_w_sb_branch'], 'v_w_gate': out['v_w_gate'], 'v_b_gate': out['v_b_gate'], 'v_w_out': out['v_w_out'], 'v_norm_mix_post': out['v_norm_mix_post'], 'v_norm_ffn_pre': out['v_norm_ffn_pre'], 'v_w_ff1': out['v_w_ff1'], 'v_w_ff2': out['v_w_ff2'], 'v_norm_ffn_post': out['v_norm_ffn_post'], 'v_w_ple': out['v_w_ple'], 'v_w_ple_gate': out['v_w_ple_gate'], 'v_norm_ple_post': out['v_norm_ple_post']}


def _loss(weights, diff, rest, loss_target):
    with _jax.named_scope("forward"):
        args = {**rest, TWIN_DIFF_INPUT: diff, **{k: w.astype(_WEIGHT_DTYPES[k]) for k, w in weights.items()}}
        y = _forward(args)
    with _jax.named_scope("loss_head"):
        err = _jnp.square(y.astype(_jnp.float32) - loss_target)
        return 0.5 * _jnp.sum(_jnp.mean(err, axis=-1)) if err.ndim else 0.5 * err


def _adamw(w, g, m, v):
    m = ADAM_B1 * m + (1.0 - ADAM_B1) * g
    v = ADAM_B2 * v + (1.0 - ADAM_B2) * _jnp.square(g)
    m_hat = m / (1.0 - ADAM_B1 ** ADAM_STEP)
    v_hat = v / (1.0 - ADAM_B2 ** ADAM_STEP)
    delta = -ADAM_LR * (m_hat / (_jnp.sqrt(v_hat) + ADAM_EPS) + ADAM_WD * w)
    return delta, m, v


def reference(x, p, norm_mix_pre, w_in, conv_w, conv_b, dt_bias, a_log, d_skip, ssd_norm, w_ssd_branch, w_sb_branch, w_gate, b_gate, w_out, norm_mix_post, norm_ffn_pre, w_ff1, w_ff2, norm_ffn_post, w_ple, w_ple_gate, norm_ple_post, loss_target, m_norm_mix_pre, m_w_in, m_conv_w, m_conv_b, m_dt_bias, m_a_log, m_d_skip, m_ssd_norm, m_w_ssd_branch, m_w_sb_branch, m_w_gate, m_b_gate, m_w_out, m_norm_mix_post, m_norm_ffn_pre, m_w_ff1, m_w_ff2, m_norm_ffn_post, m_w_ple, m_w_ple_gate, m_norm_ple_post, v_norm_mix_pre, v_w_in, v_conv_w, v_conv_b, v_dt_bias, v_a_log, v_d_skip, v_ssd_norm, v_w_ssd_branch, v_w_sb_branch, v_w_gate, v_b_gate, v_w_out, v_norm_mix_post, v_norm_ffn_pre, v_w_ff1, v_w_ff2, v_norm_ffn_post, v_w_ple, v_w_ple_gate, v_norm_ple_post):
    given = dict(x=x, p=p, norm_mix_pre=norm_mix_pre, w_in=w_in, conv_w=conv_w, conv_b=conv_b, dt_bias=dt_bias, a_log=a_log, d_skip=d_skip, ssd_norm=ssd_norm, w_ssd_branch=w_ssd_branch, w_sb_branch=w_sb_branch, w_gate=w_gate, b_gate=b_gate, w_out=w_out, norm_mix_post=norm_mix_post, norm_ffn_pre=norm_ffn_pre, w_ff1=w_ff1, w_ff2=w_ff2, norm_ffn_post=norm_ffn_post, w_ple=w_ple, w_ple_gate=w_ple_gate, norm_ple_post=norm_ple_post, loss_target=loss_target, m_norm_mix_pre=m_norm_mix_pre, m_w_in=m_w_in, m_conv_w=m_conv_w, m_conv_b=m_conv_b, m_dt_bias=m_dt_bias, m_a_log=m_a_log, m_d_skip=m_d_skip, m_ssd_norm=m_ssd_norm, m_w_ssd_branch=m_w_ssd_branch, m_w_sb_branch=m_w_sb_branch, m_w_gate=m_w_gate, m_b_gate=m_b_gate, m_w_out=m_w_out, m_norm_mix_post=m_norm_mix_post, m_norm_ffn_pre=m_norm_ffn_pre, m_w_ff1=m_w_ff1, m_w_ff2=m_w_ff2, m_norm_ffn_post=m_norm_ffn_post, m_w_ple=m_w_ple, m_w_ple_gate=m_w_ple_gate, m_norm_ple_post=m_norm_ple_post, v_norm_mix_pre=v_norm_mix_pre, v_w_in=v_w_in, v_conv_w=v_conv_w, v_conv_b=v_conv_b, v_dt_bias=v_dt_bias, v_a_log=v_a_log, v_d_skip=v_d_skip, v_ssd_norm=v_ssd_norm, v_w_ssd_branch=v_w_ssd_branch, v_w_sb_branch=v_w_sb_branch, v_w_gate=v_w_gate, v_b_gate=v_b_gate, v_w_out=v_w_out, v_norm_mix_post=v_norm_mix_post, v_norm_ffn_pre=v_norm_ffn_pre, v_w_ff1=v_w_ff1, v_w_ff2=v_w_ff2, v_norm_ffn_post=v_norm_ffn_post, v_w_ple=v_w_ple, v_w_ple_gate=v_w_ple_gate, v_norm_ple_post=v_norm_ple_post)
    weights = {n: given[n] for n in TWIN_WEIGHTS}
    shared = {n: given[n] for n in SHARED_INPUTS}
    per_example = {n: given[n] for n in ['x', 'p']}
    grad_fn = _jax.value_and_grad(_loss, argnums=(0, 1))

    def one_microbatch(ex, loss_target):
        ex = dict(ex)
        diff = ex.pop(TWIN_DIFF_INPUT)
        return grad_fn(weights, diff, {**shared, **ex}, loss_target)

    if N_MICROBATCH == 1:
        loss, (grad_w, grad_x) = one_microbatch(per_example, given["loss_target"])
    else:
        def body(carry, xs):
            loss_sum, grad_sum = carry
            l_k, (gw_k, gx_k) = one_microbatch(xs[0], xs[1])
            with _jax.named_scope("update"):
                return (loss_sum + l_k, _jax.tree.map(_jnp.add, grad_sum, gw_k)), gx_k

        init = (_jnp.zeros((), _jnp.float32), _jax.tree.map(_jnp.zeros_like, weights))
        (loss, grad_w), grad_x = _jax.lax.scan(body, init, (per_example, given["loss_target"]))
    with _jax.named_scope("update"):
        delta_w, new_m, new_v = {}, {}, {}
        for n in TWIN_WEIGHTS:
            delta_w[n], new_m[n], new_v[n] = _adamw(weights[n], grad_w[n], given["m_" + n], given["v_" + n])
    return (loss, grad_x, *[grad_w[n] for n in TWIN_WEIGHTS], *[delta_w[n] for n in TWIN_WEIGHTS],
            *[new_m[n] for n in TWIN_WEIGHTS], *[new_v[n] for n in TWIN_WEIGHTS])
```

```python
import functools
from typing import NamedTuple

import jax
import jax.numpy as jnp
from jax import lax
from jax.experimental import pallas as pl
from jax.experimental.pallas import tpu as pltpu

F32 = jnp.float32
BF16 = jnp.bfloat16
SDS = jax.ShapeDtypeStruct

HEAD_DIM = 64
GROUP_HEADS = 4
D_STATE = 128
CHUNK = 128
ATT_TILE = 128
CONV_K = 4
CONV_COLS = 128
RMS_EPS = 1e-6
LANES = 128
DT_PAD = 512
N_CHIPS = 4
N_DEV = 8
SMALL_ROWS = 16
VMEM_LIMIT = 48 * 1024 * 1024

ADAM_LR = 0.001
ADAM_B1 = 0.9
ADAM_B2 = 0.999
ADAM_EPS = 1e-08
ADAM_WD = 0.01
ADAM_STEP = 10

MESH_ID = pl.DeviceIdType.MESH
HBM_SPEC = pl.BlockSpec(memory_space=pltpu.HBM)

BIG_WEIGHTS = ("w_in", "conv_w", "w_ssd_branch", "w_sb_branch", "w_gate", "w_out", "w_ff1", "w_ff2", "w_ple", "w_ple_gate")
SHARD_AXIS = {"w_in": 1, "conv_w": 1, "w_ssd_branch": 0, "w_sb_branch": 0, "w_gate": 1, "w_out": 0, "w_ff1": 1,
              "w_ff2": 0, "w_ple": 1, "w_ple_gate": 0}
SMALL_WEIGHTS = ("norm_mix_pre", "conv_b", "dt_bias", "a_log", "d_skip", "ssd_norm", "b_gate", "norm_mix_post",
                 "norm_ffn_pre", "norm_ffn_post", "norm_ple_post")
ALL_WEIGHTS = ("norm_mix_pre", "w_in", "conv_w", "conv_b", "dt_bias", "a_log", "d_skip", "ssd_norm", "w_ssd_branch",
               "w_sb_branch", "w_gate", "b_gate", "w_out", "norm_mix_post", "norm_ffn_pre", "w_ff1", "w_ff2",
               "norm_ffn_post", "w_ple", "w_ple_gate", "norm_ple_post")


class Dims(NamedTuple):
    S: int
    D: int
    DI: int
    H: int
    G: int
    CD: int
    SBW: int
    DFF: int
    PLE: int

    @property
    def NA(self):
        return self.DI + self.CD + 3 * self.SBW + 2 * self.D + DT_PAD

    @property
    def off(self):
        o = {}
        o["z"] = 0
        o["xbc"] = self.DI
        o["q"] = self.DI + self.CD
        o["k"] = o["q"] + self.SBW
        o["v"] = o["k"] + self.SBW
        o["gate"] = o["v"] + self.SBW
        o["dt"] = o["gate"] + 2 * self.D
        return o


def _cparams(sem):
    return pltpu.CompilerParams(dimension_semantics=sem, vmem_limit_bytes=VMEM_LIMIT)


def _pick(n, cands):
    for c in cands:
        if n % c == 0:
            return c
    raise ValueError(f"no tile for {n}")


def matmul(a, b, *, ta=False, tb=False, out_dtype=F32, name):
    m, k = (a.shape[1], a.shape[0]) if ta else a.shape
    n, kb = b.shape if tb else (b.shape[1], b.shape[0])
    assert k == kb, (a.shape, b.shape, ta, tb)
    tm = _pick(m, (1024, 512, 256, 128))
    tn = _pick(n, (512, 256, 128))
    tk = _pick(k, (1024, 512, 256, 128))
    nk = k // tk
    dims = (((0 if ta else 1,), (1 if tb else 0,)), ((), ()))

    def body(a_ref, b_ref, o_ref, acc_ref):
        part = lax.dot_general(a_ref[...].astype(BF16), b_ref[...].astype(BF16), dims, preferred_element_type=F32)
        if nk == 1:
            o_ref[...] = part.astype(o_ref.dtype)
        else:
            kk = pl.program_id(2)

            @pl.when(kk == 0)
            def _():
                acc_ref[...] = part

            @pl.when(kk > 0)
            def _():
                acc_ref[...] += part

            @pl.when(kk == nk - 1)
            def _():
                o_ref[...] = acc_ref[...].astype(o_ref.dtype)

    a_spec = pl.BlockSpec((tk, tm), lambda i, j, kk: (kk, i)) if ta else pl.BlockSpec((tm, tk), lambda i, j, kk: (i, kk))
    b_spec = pl.BlockSpec((tn, tk), lambda i, j, kk: (j, kk)) if tb else pl.BlockSpec((tk, tn), lambda i, j, kk: (kk, j))
    return pl.pallas_call(
        body, grid=(m // tm, n // tn, nk), in_specs=[a_spec, b_spec],
        out_specs=pl.BlockSpec((tm, tn), lambda i, j, kk: (i, j)),
        out_shape=SDS((m, n), out_dtype), scratch_shapes=[pltpu.VMEM((tm, tn), F32)],
        compiler_params=_cparams(("parallel", "parallel", "arbitrary")), name=name)(a, b)


def _row_spec(entry, tile):
    arr, width, cb = entry if isinstance(entry, tuple) else (entry, entry.shape[1], 0)
    return arr, pl.BlockSpec((tile, width), lambda i, cb=cb: (i, cb))


def _par_spec(p):
    return pl.BlockSpec(p.shape, lambda i: (0, 0))


def row_fwd(name, fn, rows, params, outs, tile=256):
    arrs, specs = zip(*[_row_spec(e, tile) for e in rows])
    s = arrs[0].shape[0]
    nr, npar = len(rows), len(params)

    def body(*refs):
        r = [x[...].astype(F32) for x in refs[:nr]]
        p = [x[...] for x in refs[nr:nr + npar]]
        res = fn(*r, *p)
        for o_ref, val in zip(refs[nr + npar:], res):
            o_ref[...] = val.astype(o_ref.dtype)

    return pl.pallas_call(
        body, grid=(s // tile,), in_specs=list(specs) + [_par_spec(p) for p in params],
        out_specs=[pl.BlockSpec((tile, w), lambda i: (i, 0)) for w, _ in outs],
        out_shape=[SDS((s, w), dt) for w, dt in outs],
        compiler_params=_cparams(("parallel",)), name=name)(*arrs, *params)


def row_bwd(name, fn, rows, params, cots, row_grads, tile=256, primal_sum=False):
    arrs, specs = zip(*[_row_spec(e, tile) for e in rows])
    s = arrs[0].shape[0]
    nr, npar = len(rows), len(params)
    cot_entries = [e for c in cots if c is not None for e in c]
    carrs, cspecs = zip(*[_row_spec(e, tile) for e in cot_entries]) if cot_entries else ((), ())
    nc = len(cot_entries)
    want = [i for i, d in enumerate(row_grads) if d is not None]

    def body(*refs):
        r = [x[...].astype(F32) for x in refs[:nr]]
        p = [x[...] for x in refs[nr:nr + npar]]
        cvals = [x[...].astype(F32) for x in refs[nr + npar:nr + npar + nc]]
        outs = refs[nr + npar + nc:]
        prim, vjp = jax.vjp(fn, *r, *p)
        ct, pos = [], 0
        for c, pr in zip(cots, prim):
            if c is None:
                ct.append(jnp.ones_like(pr))
            else:
                acc = cvals[pos]
                for extra in cvals[pos + 1:pos + len(c)]:
                    acc = acc + extra
                pos += len(c)
                ct.append(acc)
        grads = vjp(tuple(ct))
        for o_ref, i in zip(outs[:len(want)], want):
            o_ref[...] = grads[i].astype(o_ref.dtype)
        acc_refs = outs[len(want):]
        vals = [grads[nr + j] for j in range(npar)]
        if primal_sum:
            vals.append(jnp.sum(prim[0], axis=0, keepdims=True))
        first = pl.program_id(0) == 0

        @pl.when(first)
        def _():
            for a_ref, v in zip(acc_refs, vals):
                a_ref[...] = v

        @pl.when(jnp.logical_not(first))
        def _():
            for a_ref, v in zip(acc_refs, vals):
                a_ref[...] += v

    widths = [(e[1] if isinstance(e, tuple) else e.shape[1]) for e in rows]
    out_specs = [pl.BlockSpec((tile, widths[i]), lambda i_: (i_, 0)) for i in want]
    out_shape = [SDS((s, widths[i]), row_grads[i]) for i in want]
    pshapes = [p.shape for p in params]
    if primal_sum:
        pshapes.append((1, widths[0]))
    out_specs += [pl.BlockSpec(sh, lambda i_: (0, 0)) for sh in pshapes]
    out_shape += [SDS(sh, F32) for sh in pshapes]
    res = pl.pallas_call(
        body, grid=(s // tile,), in_specs=list(specs) + [_par_spec(p) for p in params] + list(cspecs),
        out_specs=out_specs, out_shape=out_shape,
        compiler_params=_cparams(("arbitrary",)), name=name)(*arrs, *params, *carrs)
    return res[:len(want)], res[len(want):]


def _rms(x, w):
    return x * lax.rsqrt(jnp.mean(x * x, axis=-1, keepdims=True) + RMS_EPS) * w


def _sigmoid(x):
    return jax.nn.sigmoid(x)


def _softplus(x):
    return jnp.maximum(x, 0.0) + jnp.log1p(jnp.exp(-jnp.abs(x)))


def f_norm1(x, w):
    return (_rms(x, w),)


def f_merge(gp_ssd, gp_sb, yb_ssd, yb_sb, b_ssd, b_sb):
    return (_sigmoid(gp_ssd + b_ssd) * yb_ssd + _sigmoid(gp_sb + b_sb) * yb_sb,)


def f_mix_out(x, mo, w_post, w_pre):
    h1 = x + _rms(mo, w_post)
    return h1, _rms(h1, w_pre)


def f_relu2(a1):
    return (jnp.square(jnp.maximum(a1, 0.0)),)


def f_ffn_out(h1, ff, w):
    return (h1 + _rms(ff, w),)


def f_ple_loss(h2, pg, pe, tgt, w):
    h3 = h2 + _rms(_sigmoid(pg) * pe, w)
    return (0.5 * jnp.square(h3 - tgt) * (1.0 / h2.shape[-1]),)


def _shift_down(u, d, rows):
    return u if d == 0 else jnp.where(rows >= d, pltpu.roll(u, d, 0), 0.0)


def _shift_up(u, d, rows):
    s = u.shape[0]
    return u if d == 0 else jnp.where(rows < s - d, pltpu.roll(u, s - d, 0), 0.0)


def conv_fwd(proj, col0, cd, conv_w, conv_b, name):
    s = proj.shape[0]
    cb0 = col0 // CONV_COLS

    def body(u_ref, w_ref, b_ref, o_ref):
        u = u_ref[...]
        rows = lax.broadcasted_iota(jnp.int32, u.shape, 0)
        y = jnp.broadcast_to(b_ref[...], u.shape)
        for k in range(CONV_K):
            y = y + w_ref[k:k + 1, :] * _shift_down(u, CONV_K - 1 - k, rows)
        o_ref[...] = y * _sigmoid(y)

    return pl.pallas_call(
        body, grid=(cd // CONV_COLS,),
        in_specs=[pl.BlockSpec((s, CONV_COLS), lambda i: (0, cb0 + i)),
                  pl.BlockSpec((CONV_K, CONV_COLS), lambda i: (0, i)),
                  pl.BlockSpec((1, CONV_COLS), lambda i: (0, i))],
        out_specs=pl.BlockSpec((s, CONV_COLS), lambda i: (0, i)),
        out_shape=SDS((s, cd), F32), compiler_params=_cparams(("parallel",)), name=name)(proj, conv_w, conv_b)


def conv_bwd(proj, col0, dout, ch0, conv_w, conv_b, name):
    s = proj.shape[0]
    ncb = dout.shape[1] // CONV_COLS
    cb0 = (col0 + ch0) // CONV_COLS
    wb0 = ch0 // CONV_COLS

    def body(u_ref, g_ref, w_ref, b_ref, du_ref, dw_ref, db_ref):
        u = u_ref[...]
        rows = lax.broadcasted_iota(jnp.int32, u.shape, 0)
        y = jnp.broadcast_to(b_ref[...], u.shape)
        for k in range(CONV_K):
            y = y + w_ref[k:k + 1, :] * _shift_down(u, CONV_K - 1 - k, rows)
        sg = _sigmoid(y)
        dy = g_ref[...] * (sg * (1.0 + y * (1.0 - sg)))
        du = jnp.zeros_like(u)
        for k in range(CONV_K):
            d = CONV_K - 1 - k
            du = du + w_ref[k:k + 1, :] * _shift_up(dy, d, rows)
            dw_ref[k:k + 1, :] = jnp.sum(dy * _shift_down(u, d, rows), axis=0, keepdims=True)
        du_ref[...] = du.astype(du_ref.dtype)
        db_ref[...] = jnp.sum(dy, axis=0, keepdims=True)

    return pl.pallas_call(
        body, grid=(ncb,),
        in_specs=[pl.BlockSpec((s, CONV_COLS), lambda i: (0, cb0 + i)),
                  pl.BlockSpec((s, CONV_COLS), lambda i: (0, i)),
                  pl.BlockSpec((CONV_K, CONV_COLS), lambda i: (0, wb0 + i)),
                  pl.BlockSpec((1, CONV_COLS), lambda i: (0, wb0 + i))],
        out_specs=[pl.BlockSpec((s, CONV_COLS), lambda i: (0, i)),
                   pl.BlockSpec((CONV_K, CONV_COLS), lambda i: (0, i)),
                   pl.BlockSpec((1, CONV_COLS), lambda i: (0, i))],
        out_shape=[SDS((s, ncb * CONV_COLS), BF16), SDS((CONV_K, ncb * CONV_COLS), F32), SDS((1, ncb * CONV_COLS), F32)],
        compiler_params=_cparams(("parallel",)), name=name)(proj, dout, conv_w, conv_b)


def _dot(a, b, dims):
    return lax.dot_general(a.astype(BF16), b.astype(BF16), (dims, ((), ())), preferred_element_type=F32)


NN = ((1,), (0,))
NT = ((1,), (1,))
TN = ((0,), (0,))


def ssd_chunk(xs, zs, nws, dtc, dtw, dtbs, alogs, dsks, bm, cm, prevs):
    ln = bm.shape[0]
    row = lax.broadcasted_iota(jnp.int32, (ln, ln), 0)
    col = lax.broadcasted_iota(jnp.int32, (ln, ln), 1)
    causal = row >= col
    tri = causal.astype(F32)
    cb = _dot(cm, bm, NT)
    ygs, news = [], []
    for r in range(GROUP_HEADS):
        a = -jnp.exp(alogs[r])
        dt_c = _softplus(dtc[r] + dtbs[r])
        da_c = dt_c * a
        da_w = _softplus(dtw[r] + dtbs[r]) * a
        acs_c = jnp.sum(tri * da_w, axis=1, keepdims=True)
        acs_w = jnp.sum((row <= col).astype(F32) * da_c, axis=0, keepdims=True)
        alast = jnp.sum(da_c, axis=0, keepdims=True)
        xd = xs[r] * dt_c
        decay = jnp.exp(jnp.where(causal, acs_c - acs_w, -jnp.inf))
        y = _dot(cb * decay, xd, NN)
        y = y + _dot(cm, prevs[r], NT) * jnp.exp(acs_c)
        y = y + xs[r] * dsks[r]
        st = _dot(xd * jnp.exp(alast - acs_c), bm, TN)
        news.append(prevs[r] * jnp.exp(alast) + st)
        ygs.append(y * (zs[r] * _sigmoid(zs[r])))
    ss = sum(jnp.sum(v * v, axis=-1, keepdims=True) for v in ygs)
    rstd = lax.rsqrt(ss * (1.0 / (GROUP_HEADS * HEAD_DIM)) + RMS_EPS)
    outs = [ygs[r] * rstd * nws[r] for r in range(GROUP_HEADS)]
    return tuple(outs), tuple(news)


def _ssd_specs(dm, cidx):
    gw = GROUP_HEADS * HEAD_DIM
    nb0 = dm.DI // D_STATE
    par = pl.BlockSpec((None, 1, GROUP_HEADS), lambda g, c: (g, 0, 0))
    return dict(
        z=pl.BlockSpec((CHUNK, gw), lambda g, c: (cidx(c), g)),
        xs=pl.BlockSpec((CHUNK, gw), lambda g, c: (cidx(c), g)),
        b=pl.BlockSpec((CHUNK, D_STATE), lambda g, c: (cidx(c), nb0 + g)),
        c=pl.BlockSpec((CHUNK, D_STATE), lambda g, c: (cidx(c), nb0 + dm.G + g)),
        dtc=pl.BlockSpec((None, CHUNK, GROUP_HEADS), lambda g, c: (g, cidx(c), 0)),
        dtw=pl.BlockSpec((None, GROUP_HEADS, CHUNK), lambda g, c: (g, 0, cidx(c))),
        par=par,
        nw=pl.BlockSpec((1, gw), lambda g, c: (0, g)),
        st=pl.BlockSpec((None, None, GROUP_HEADS, HEAD_DIM, D_STATE), lambda g, c: (g, cidx(c), 0, 0, 0)),
    )


def _ssd_load(z_ref, xs_ref, b_ref, c_ref, dtc_ref, dtw_ref, dtb_ref, alog_ref, dsk_ref, nw_ref):
    hs = [slice(r * HEAD_DIM, (r + 1) * HEAD_DIM) for r in range(GROUP_HEADS)]
    xs = tuple(xs_ref[:, h] for h in hs)
    zs = tuple(z_ref[:, h] for h in hs)
    nws = tuple(nw_ref[:, h] for h in hs)
    dtc = tuple(dtc_ref[:, r:r + 1] for r in range(GROUP_HEADS))
    dtw = tuple(dtw_ref[r:r + 1, :] for r in range(GROUP_HEADS))
    dtbs = tuple(dtb_ref[:, r:r + 1] for r in range(GROUP_HEADS))
    alogs = tuple(alog_ref[:, r:r + 1] for r in range(GROUP_HEADS))
    dsks = tuple(dsk_ref[:, r:r + 1] for r in range(GROUP_HEADS))
    return hs, (xs, zs, nws, dtc, dtw, dtbs, alogs, dsks, b_ref[...], c_ref[...])


def ssd_fwd(proj, xbc, dtc, dtw, dtb, alog, dsk, nw, dm, name):
    nc = dm.S // CHUNK
    sp = _ssd_specs(dm, lambda c: c)

    def body(z_ref, xs_ref, b_ref, c_ref, dtc_ref, dtw_ref, dtb_ref, alog_ref, dsk_ref, nw_ref, y_ref, st_ref, prev):
        @pl.when(pl.program_id(1) == 0)
        def _():
            prev[...] = jnp.zeros_like(prev)

        hs, args = _ssd_load(z_ref, xs_ref, b_ref, c_ref, dtc_ref, dtw_ref, dtb_ref, alog_ref, dsk_ref, nw_ref)
        st_ref[...] = prev[...]
        outs, news = ssd_chunk(*args, tuple(prev[r] for r in range(GROUP_HEADS)))
        for r in range(GROUP_HEADS):
            y_ref[:, hs[r]] = outs[r].astype(y_ref.dtype)
            prev[r] = news[r]

    return pl.pallas_call(
        body, grid=(dm.G, nc),
        in_specs=[sp["z"], sp["xs"], sp["b"], sp["c"], sp["dtc"], sp["dtw"], sp["par"], sp["par"], sp["par"], sp["nw"]],
        out_specs=[sp["xs"], sp["st"]],
        out_shape=[SDS((dm.S, dm.DI), BF16), SDS((dm.G, nc, GROUP_HEADS, HEAD_DIM, D_STATE), F32)],
        scratch_shapes=[pltpu.VMEM((GROUP_HEADS, HEAD_DIM, D_STATE), F32)],
        compiler_params=_cparams(("parallel", "arbitrary")), name=name)(proj, xbc, xbc, xbc, dtc, dtw, dtb, alog, dsk, nw)


def ssd_bwd(proj, xbc, dtc, dtw, dtb, alog, dsk, nw, states, dy, dm, name):
    nc = dm.S // CHUNK
    sp = _ssd_specs(dm, lambda c: nc - 1 - c)
    gw = GROUP_HEADS * HEAD_DIM
    bc_spec = pl.BlockSpec((CHUNK, D_STATE), lambda g, c: (nc - 1 - c, g))

    def body(z_ref, xs_ref, b_ref, c_ref, dtc_ref, dtw_ref, dtb_ref, alog_ref, dsk_ref, nw_ref, st_ref, dy_ref,
             dz_ref, dxs_ref, db_ref, dc_ref, ddtc_ref, ddtw_ref, ddtb_ref, dalog_ref, ddsk_ref, dnw_ref, dprev):
        first = pl.program_id(1) == 0

        @pl.when(first)
        def _():
            dprev[...] = jnp.zeros_like(dprev)

        hs, args = _ssd_load(z_ref, xs_ref, b_ref, c_ref, dtc_ref, dtw_ref, dtb_ref, alog_ref, dsk_ref, nw_ref)
        prevs = tuple(st_ref[r] for r in range(GROUP_HEADS))
        _, vjp = jax.vjp(ssd_chunk, *args, prevs)
        ct = (tuple(dy_ref[:, h] for h in hs), tuple(dprev[r] for r in range(GROUP_HEADS)))
        gxs, gzs, gnws, gdtc, gdtw, gdtbs, galogs, gdsks, gb, gc, gprevs = vjp(ct)
        db_ref[...] = gb
        dc_ref[...] = gc
        for r in range(GROUP_HEADS):
            dxs_ref[:, hs[r]] = gxs[r]
            dz_ref[:, hs[r]] = gzs[r].astype(dz_ref.dtype)
            ddtc_ref[:, r:r + 1] = gdtc[r]
            ddtw_ref[r:r + 1, :] = gdtw[r]
            dprev[r] = gprevs[r]

        @pl.when(first)
        def _():
            for r in range(GROUP_HEADS):
                dnw_ref[:, hs[r]] = gnws[r]
                ddtb_ref[:, r:r + 1] = gdtbs[r]
                dalog_ref[:, r:r + 1] = galogs[r]
                ddsk_ref[:, r:r + 1] = gdsks[r]

        @pl.when(jnp.logical_not(first))
        def _():
            for r in range(GROUP_HEADS):
                dnw_ref[:, hs[r]] += gnws[r]
                ddtb_ref[:, r:r + 1] += gdtbs[r]
                dalog_ref[:, r:r + 1] += galogs[r]
                ddsk_ref[:, r:r + 1] += gdsks[r]

    xs_out = pl.BlockSpec((CHUNK, gw), lambda g, c: (nc - 1 - c, g))
    return pl.pallas_call(
        body, grid=(dm.G, nc),
        in_specs=[sp["z"], sp["xs"], sp["b"], sp["c"], sp["dtc"], sp["dtw"], sp["par"], sp["par"], sp["par"], sp["nw"],
                  sp["st"], xs_out],
        out_specs=[xs_out, xs_out, bc_spec, bc_spec, sp["dtc"], sp["dtw"], sp["par"], sp["par"], sp["par"], sp["nw"]],
        out_shape=[SDS((dm.S, dm.DI), BF16), SDS((dm.S, dm.DI), F32), SDS((dm.S, dm.G * D_STATE), F32),
                   SDS((dm.S, dm.G * D_STATE), F32), SDS((dm.G, dm.S, GROUP_HEADS), F32), SDS((dm.G, GROUP_HEADS, dm.S), F32),
                   SDS((dm.G, 1, GROUP_HEADS), F32), SDS((dm.G, 1, GROUP_HEADS), F32), SDS((dm.G, 1, GROUP_HEADS), F32),
                   SDS((1, dm.DI), F32)],
        scratch_shapes=[pltpu.VMEM((GROUP_HEADS, HEAD_DIM, D_STATE), F32)],
        compiler_params=_cparams(("parallel", "arbitrary")), name=name)(
            proj, xbc, xbc, xbc, dtc, dtw, dtb, alog, dsk, nw, states, dy)


def _split_bf16(v):
    hi = v.astype(BF16)
    return hi, (v - hi.astype(F32)).astype(BF16)


def _sb_tile(qh, kh, mask, scale, run, after_mat):
    z = _dot(qh, kh, NT) * scale
    sp = _softplus(z)
    lk = jnp.where(mask, -sp, 0.0)
    hi, lo = _split_bf16(lk)
    cum = jnp.dot(hi, after_mat, preferred_element_type=F32) + jnp.dot(lo, after_mat, preferred_element_type=F32)
    ls = z - sp
    w = jnp.where(mask, jnp.exp(ls + run + cum), 0.0)
    return lk, ls, w


def _att_specs(dm, s):
    t = ATT_TILE
    qb, kb, vb = dm.off["q"] // LANES, dm.off["k"] // LANES, dm.off["v"] // LANES
    return (pl.BlockSpec((t, LANES), lambda p, i: (i, qb + p)),
            pl.BlockSpec((s, LANES), lambda p, i: (0, kb + p)),
            pl.BlockSpec((s, LANES), lambda p, i: (0, vb + p)))


def attn_fwd(proj, dm, name):
    s, t = dm.S, ATT_TILE
    scale = HEAD_DIM ** -0.5
    hsl = [slice(h * HEAD_DIM, (h + 1) * HEAD_DIM) for h in range(2)]

    def body(q_ref, k_ref, v_ref, o_ref):
        i = pl.program_id(1)
        r_io = lax.broadcasted_iota(jnp.int32, (t, t), 0)
        c_io = lax.broadcasted_iota(jnp.int32, (t, t), 1)
        after_mat = (r_io > c_io).astype(BF16)
        qs = [q_ref[:, h].astype(BF16) for h in hsl]

        def step(jj, carry):
            j = i - jj
            r0 = pl.multiple_of(j * t, t)
            mask = (j * t + c_io) < (i * t + r_io)
            new = []
            for h in range(2):
                run, acc = carry[h]
                kh = k_ref[pl.ds(r0, t), hsl[h]]
                vh = v_ref[pl.ds(r0, t), hsl[h]]
                lk, _, w = _sb_tile(qs[h], kh, mask, scale, run, after_mat)
                new.append((run + jnp.sum(lk, axis=1, keepdims=True), acc + _dot(w, vh, NN)))
            return tuple(new)

        init = tuple((jnp.zeros((t, 1), F32), jnp.zeros((t, HEAD_DIM), F32)) for _ in range(2))
        res = lax.fori_loop(0, i + 1, step, init)
        for h in range(2):
            o_ref[:, hsl[h]] = res[h][1]

    qs_, ks_, vs_ = _att_specs(dm, s)
    return pl.pallas_call(
        body, grid=(dm.SBW // LANES, s // t), in_specs=[qs_, ks_, vs_],
        out_specs=pl.BlockSpec((t, LANES), lambda p, i: (i, p)),
        out_shape=SDS((s, dm.SBW), F32), compiler_params=_cparams(("parallel", "arbitrary")), name=name)(proj, proj, proj)


def attn_bwd(proj, do, dm, name):
    s, t = dm.S, ATT_TILE
    nq = s // t
    scale = HEAD_DIM ** -0.5
    hsl = [slice(h * HEAD_DIM, (h + 1) * HEAD_DIM) for h in range(2)]

    def body(q_ref, k_ref, v_ref, do_ref, dq_ref, dk_ref, dv_ref, dk_acc, dv_acc, g_scr, s_scr):
        i = pl.program_id(1)

        @pl.when(i == 0)
        def _():
            dk_acc[...] = jnp.zeros_like(dk_acc)
            dv_acc[...] = jnp.zeros_like(dv_acc)

        r_io = lax.broadcasted_iota(jnp.int32, (t, t), 0)
        c_io = lax.broadcasted_iota(jnp.int32, (t, t), 1)
        after_mat = (r_io > c_io).astype(BF16)
        before_mat = (r_io < c_io).astype(BF16)
        for h in range(2):
            qh = q_ref[:, hsl[h]].astype(BF16)
            doh = do_ref[:, hsl[h]].astype(BF16)

            def step1(jj, run, h=h, qh=qh, doh=doh):
                j = i - jj
                r0 = pl.multiple_of(j * t, t)
                mask = (j * t + c_io) < (i * t + r_io)
                kh = k_ref[pl.ds(r0, t), hsl[h]]
                vh = v_ref[pl.ds(r0, t), hsl[h]]
                lk, ls, w = _sb_tile(qh, kh, mask, scale, run, after_mat)
                g_scr[j] = _dot(doh, vh, NT) * w
                s_scr[j] = jnp.exp(ls)
                dv_acc[pl.ds(r0, t), hsl[h]] += _dot(w, doh, TN)
                return run + jnp.sum(lk, axis=1, keepdims=True)

            lax.fori_loop(0, i + 1, step1, jnp.zeros((t, 1), F32))

            def step2(j, carry, h=h, qh=qh):
                pre, dq = carry
                r0 = pl.multiple_of(j * t, t)
                mask = (j * t + c_io) < (i * t + r_io)
                g = g_scr[j]
                sig = s_scr[j]
                hi, lo = _split_bf16(g)
                pg = pre + jnp.dot(hi, before_mat, preferred_element_type=F32) + jnp.dot(lo, before_mat, preferred_element_type=F32)
                dz = (jnp.where(mask, g * (1.0 - sig) - pg * sig, 0.0) * scale).astype(BF16)
                kh = k_ref[pl.ds(r0, t), hsl[h]]
                dk_acc[pl.ds(r0, t), hsl[h]] += _dot(dz, qh, TN)
                return pre + jnp.sum(g, axis=1, keepdims=True), dq + _dot(dz, kh, NN)

            _, dq = lax.fori_loop(0, i + 1, step2, (jnp.zeros((t, 1), F32), jnp.zeros((t, HEAD_DIM), F32)))
            dq_ref[:, hsl[h]] = dq.astype(dq_ref.dtype)

        @pl.when(i == nq - 1)
        def _():
            dk_ref[...] = dk_acc[...].astype(dk_ref.dtype)
            dv_ref[...] = dv_acc[...].astype(dv_ref.dtype)

    qs_, ks_, vs_ = _att_specs(dm, s)
    tile_spec = pl.BlockSpec((t, LANES), lambda p, i: (i, p))
    full_spec = pl.BlockSpec((s, LANES), lambda p, i: (0, p))
    return pl.pallas_call(
        body, grid=(dm.SBW // LANES, nq), in_specs=[qs_, ks_, vs_, tile_spec],
        out_specs=[tile_spec, full_spec, full_spec],
        out_shape=[SDS((s, dm.SBW), BF16)] * 3,
        scratch_shapes=[pltpu.VMEM((s, LANES), F32), pltpu.VMEM((s, LANES), F32),
                        pltpu.VMEM((nq, t, t), F32), pltpu.VMEM((nq, t, t), F32)],
        compiler_params=_cparams(("parallel", "arbitrary")), name=name)(proj, proj, proj, do)


def adamw(w, g, m, v, name):
    rows, width = w.shape
    tile = _pick(rows, (240, 208, 16, 8))
    c1 = 1.0 / (1.0 - ADAM_B1 ** ADAM_STEP)
    c2 = 1.0 / (1.0 - ADAM_B2 ** ADAM_STEP)

    def body(w_ref, g_ref, m_ref, v_ref, d_ref, nm_ref, nv_ref):
        gg = g_ref[...]
        nm = ADAM_B1 * m_ref[...] + (1.0 - ADAM_B1) * gg
        nv = ADAM_B2 * v_ref[...] + (1.0 - ADAM_B2) * (gg * gg)
        d_ref[...] = -ADAM_LR * ((nm * c1) / (jnp.sqrt(nv * c2) + ADAM_EPS) + ADAM_WD * w_ref[...])
        nm_ref[...] = nm
        nv_ref[...] = nv

    spec = pl.BlockSpec((tile, width), lambda i: (i, 0))
    return pl.pallas_call(
        body, grid=(rows // tile,), in_specs=[spec] * 4, out_specs=[spec] * 3,
        out_shape=[SDS((rows, width), F32)] * 3, compiler_params=_cparams(("parallel",)), name=name)(w, g, m, v)


def _me():
    return lax.axis_index("x"), lax.axis_index("y"), lax.axis_index("c")


def _other_chips(x, y):
    return [(1 - x, y), (x, 1 - y), (1 - x, 1 - y)]


def gather_weights(wp):
    rows, width = wp.shape
    half = rows // 2

    def body(w_ref, out_ref, send_sems, recv_sems, local_sem):
        x, y, c = _me()
        sibling = (x, y, 1 - c)
        chips = _other_chips(x, y)

        def part(cx, cy, hf):
            return out_ref.at[2 * cx + cy, pl.ds(hf * half, half), :]

        def copy(k, src, dst, to):
            return pltpu.make_async_remote_copy(src_ref=src, dst_ref=dst, send_sem=send_sems.at[k], recv_sem=recv_sems.at[k],
                                                device_id=to, device_id_type=MESH_ID)

        mine = pltpu.make_async_copy(w_ref, out_ref.at[2 * x + y], local_sem)
        mine.start()
        first = [copy(j, w_ref.at[pl.ds(c * half, half), :], part(x, y, c), (cx, cy, c)) for j, (cx, cy) in enumerate(chips)]
        for cp in first:
            cp.start()
        passed = [copy(3 + j, part(cx, cy, c), part(cx, cy, c), sibling) for j, (cx, cy) in enumerate(chips)]
        for j, (cx, cy) in enumerate(chips):
            copy(j, part(cx, cy, c), part(cx, cy, c), (x, y, c)).wait_recv()
            passed[j].start()
        for j, (cx, cy) in enumerate(chips):
            copy(3 + j, part(cx, cy, 1 - c), part(cx, cy, 1 - c), (x, y, c)).wait_recv()
        for cp in first + passed:
            cp.wait_send()
        mine.wait()

    return pl.pallas_call(
        body, out_shape=SDS((N_CHIPS, rows, width), wp.dtype), in_specs=[HBM_SPEC], out_specs=HBM_SPEC,
        scratch_shapes=[pltpu.SemaphoreType.DMA((6,)), pltpu.SemaphoreType.DMA((6,)), pltpu.SemaphoreType.DMA],
        name="gather_weights")(wp)


def swap_halves(g):
    n, rows, width = g.shape
    half = rows // 2

    def body(g_ref, a_ref, send_sem, recv_sem):
        x, y, c = _me()
        cp = pltpu.make_async_remote_copy(src_ref=g_ref.at[:, pl.ds((1 - c) * half, half), :], dst_ref=a_ref,
                                          send_sem=send_sem, recv_sem=recv_sem, device_id=(x, y, 1 - c), device_id_type=MESH_ID)
        cp.start()
        cp.wait()

    return pl.pallas_call(
        body, out_shape=SDS((n, half, width), g.dtype), in_specs=[HBM_SPEC], out_specs=HBM_SPEC,
        scratch_shapes=[pltpu.SemaphoreType.DMA, pltpu.SemaphoreType.DMA], name="swap_halves")(g)


def add_half(g, a, c_idx, name):
    n, rows, width = g.shape
    half = rows // 2
    tile = _pick(half, (240, 208, 16, 8))
    nt = half // tile

    def body(c_ref, g_ref, a_ref, o_ref):
        o_ref[...] = g_ref[...] + a_ref[...]

    gs = pltpu.PrefetchScalarGridSpec(
        num_scalar_prefetch=1, grid=(n, nt),
        in_specs=[pl.BlockSpec((None, tile, width), lambda s, i, c_ref: (s, c_ref[0] * nt + i, 0)),
                  pl.BlockSpec((None, tile, width), lambda s, i, c_ref: (s, i, 0))],
        out_specs=pl.BlockSpec((None, tile, width), lambda s, i, c_ref: (s, i, 0)))
    return pl.pallas_call(body, grid_spec=gs, out_shape=SDS((n, half, width), g.dtype),
                          compiler_params=_cparams(("parallel", "parallel")), name=name)(c_idx, g, a)


def exchange_chips(sh, small):
    n, hf, width = sh.shape

    def body(s_ref, sm_ref, b_ref, all_ref, send_sems, recv_sems, local_sem):
        x, y, c = _me()
        chips = _other_chips(x, y)
        mine = pltpu.make_async_copy(sm_ref, all_ref.at[0], local_sem)
        mine.start()
        copies = []
        for j, (cx, cy) in enumerate(chips):
            copies.append(pltpu.make_async_remote_copy(
                src_ref=s_ref.at[2 * cx + cy], dst_ref=b_ref.at[j], send_sem=send_sems.at[j], recv_sem=recv_sems.at[j],
                device_id=(cx, cy, c), device_id_type=MESH_ID))
        for m in range(1, N_DEV):
            peer = (x ^ ((m >> 2) & 1), y ^ ((m >> 1) & 1), c ^ (m & 1))
            copies.append(pltpu.make_async_remote_copy(
                src_ref=sm_ref, dst_ref=all_ref.at[m], send_sem=send_sems.at[2 + m], recv_sem=recv_sems.at[2 + m],
                device_id=peer, device_id_type=MESH_ID))
        for cp in copies:
            cp.start()
        for cp in copies:
            cp.wait()
        mine.wait()

    return pl.pallas_call(
        body, out_shape=[SDS((3, hf, width), sh.dtype), SDS((N_DEV, SMALL_ROWS, width), small.dtype)],
        in_specs=[HBM_SPEC, HBM_SPEC], out_specs=[HBM_SPEC, HBM_SPEC],
        scratch_shapes=[pltpu.SemaphoreType.DMA((10,)), pltpu.SemaphoreType.DMA((10,)), pltpu.SemaphoreType.DMA],
        name="exchange_chips")(sh, small)


def add_chips(sh, b, k_idx, name):
    n, hf, width = sh.shape
    tile = _pick(hf, (240, 208, 16, 8))

    def body(k_ref, s_ref, b0, b1, b2, o_ref):
        o_ref[...] = ((s_ref[...] + b0[...]) + b1[...]) + b2[...]

    def bspec(j):
        return pl.BlockSpec((None, tile, width), lambda i, k_ref, j=j: (j, i, 0))

    gs = pltpu.PrefetchScalarGridSpec(
        num_scalar_prefetch=1, grid=(hf // tile,),
        in_specs=[pl.BlockSpec((None, tile, width), lambda i, k_ref: (k_ref[0], i, 0)), bspec(0), bspec(1), bspec(2)],
        out_specs=pl.BlockSpec((tile, width), lambda i, k_ref: (i, 0)))
    return pl.pallas_call(body, grid_spec=gs, out_shape=SDS((hf, width), sh.dtype),
                          compiler_params=_cparams(("parallel",)), name=name)(k_idx, sh, b, b, b)


def sum_small(allsm, me_idx, name):
    _, rows, width = allsm.shape

    def body(me_ref, a_ref, o_ref):
        me = me_ref[0]
        acc = a_ref[me]
        for dev in range(1, N_DEV):
            acc = acc + a_ref[jnp.bitwise_xor(me, dev)]
        o_ref[...] = acc

    gs = pltpu.PrefetchScalarGridSpec(
        num_scalar_prefetch=1, grid=(1,),
        in_specs=[pl.BlockSpec((N_DEV, rows, width), lambda i, me_ref: (0, 0, 0))],
        out_specs=pl.BlockSpec((rows, width), lambda i, me_ref: (0, 0)))
    return pl.pallas_call(body, grid_spec=gs, out_shape=SDS((rows, width), allsm.dtype),
                          compiler_params=_cparams(("arbitrary",)), name=name)(me_idx, allsm)


def join_halves(t):
    hf, width = t.shape

    def body(t_ref, o_ref, send_sem, recv_sem, local_sem):
        x, y, c = _me()
        mine = pltpu.make_async_copy(t_ref, o_ref.at[pl.ds(c * hf, hf), :], local_sem)
        mine.start()
        cp = pltpu.make_async_remote_copy(src_ref=t_ref, dst_ref=o_ref.at[pl.ds(c * hf, hf), :], send_sem=send_sem,
                                          recv_sem=recv_sem, device_id=(x, y, 1 - c), device_id_type=MESH_ID)
        cp.start()
        cp.wait_send()
        pltpu.make_async_remote_copy(src_ref=t_ref, dst_ref=o_ref.at[pl.ds((1 - c) * hf, hf), :], send_sem=send_sem,
                                     recv_sem=recv_sem, device_id=(x, y, 1 - c), device_id_type=MESH_ID).wait_recv()
        mine.wait()

    return pl.pallas_call(
        body, out_shape=SDS((2 * hf, width), t.dtype), in_specs=[HBM_SPEC], out_specs=HBM_SPEC,
        scratch_shapes=[pltpu.SemaphoreType.DMA, pltpu.SemaphoreType.DMA, pltpu.SemaphoreType.DMA], name="join_halves")(t)


def _pack_rows(shard_shapes, width):
    rows = sum((a * b) // width for a, b in shard_shapes.values())
    return -(-rows // 32) * 32


def pack_local(shards, width, total_rows, dtype):
    parts = [shards[n].reshape(-1, width).astype(dtype) for n in BIG_WEIGHTS]
    used = sum(p.shape[0] for p in parts)
    parts.append(jnp.zeros((total_rows - used, width), dtype))
    return jnp.concatenate(parts, axis=0)


def unpack_local(packed, shard_shapes, width):
    out, r0 = {}, 0
    for n in BIG_WEIGHTS:
        a, b = shard_shapes[n]
        nr = (a * b) // width
        out[n] = packed[r0:r0 + nr].reshape(a, b)
        r0 += nr
    return out


EXACT_IN_GATHER = ("conv_w",)
EXACT_TERMS = 3


def pack_gather(shards, width):
    parts = []
    for n in BIG_WEIGHTS:
        if n in EXACT_IN_GATHER:
            rest = shards[n].astype(F32)
            for _ in range(EXACT_TERMS):
                term = rest.astype(BF16)
                parts.append(term.reshape(-1, width))
                rest = rest - term.astype(F32)
        else:
            parts.append(shards[n].reshape(-1, width).astype(BF16))
    used = sum(p.shape[0] for p in parts)
    parts.append(jnp.zeros((-(-used // 32) * 32 - used, width), BF16))
    return jnp.concatenate(parts, axis=0)


def unpack_full(gathered, shard_shapes, width):
    out, r0 = {}, 0
    for n in BIG_WEIGHTS:
        a, b = shard_shapes[n]
        terms = EXACT_TERMS if n in EXACT_IN_GATHER else 1
        nr = (a * b) // width
        pieces = []
        for j in range(N_CHIPS):
            blk = gathered[j, r0:r0 + nr].reshape(a, b)
            for t in range(1, terms):
                blk = blk.astype(F32) + gathered[j, r0 + t * nr:r0 + (t + 1) * nr].reshape(a, b).astype(F32)
            pieces.append(blk)
        out[n] = jnp.concatenate(pieces, axis=SHARD_AXIS[n])
        r0 += terms * nr
    return out


def pack_full(grads, shard_shapes, width, total_rows):
    slabs = []
    for j in range(N_CHIPS):
        parts = []
        for n in BIG_WEIGHTS:
            a, b = shard_shapes[n]
            ax = SHARD_AXIS[n]
            sz = (a, b)[ax]
            piece = lax.slice_in_dim(grads[n], j * sz, (j + 1) * sz, axis=ax)
            parts.append(piece.reshape(-1, width))
        used = sum(p.shape[0] for p in parts)
        parts.append(jnp.zeros((total_rows - used, width), F32))
        slabs.append(jnp.concatenate(parts, axis=0))
    return jnp.stack(slabs, axis=0)


def _small_layout(sizes, width):
    lay, r = {}, 0
    for n in SMALL_WEIGHTS:
        nr = -(-sizes[n] // width)
        lay[n] = (r, nr, sizes[n])
        r += nr
    assert r <= SMALL_ROWS
    return lay


def pack_small(vals, lay, width):
    rows = []
    for n in SMALL_WEIGHTS:
        r, nr, sz = lay[n]
        v = vals[n].reshape(-1).astype(F32)
        rows.append(jnp.pad(v, (0, nr * width - sz)).reshape(nr, width))
    used = sum(r.shape[0] for r in rows)
    rows.append(jnp.zeros((SMALL_ROWS - used, width), F32))
    return jnp.concatenate(rows, axis=0)


def unpack_small(packed, lay):
    return {n: packed[r:r + nr].reshape(-1)[:sz].reshape(1, sz) for n, (r, nr, sz) in lay.items()}


def local_step(x, p, tgt, wf, sm, dm):
    s, d = dm.S, dm.D
    off = dm.off
    w_in = wf["w_in"]
    c0 = dm.DI + dm.CD
    w_all = jnp.concatenate(
        [w_in[:, :c0], w_in[:, c0 + dm.H:], wf["w_gate"], w_in[:, c0:c0 + dm.H],
         jnp.zeros((d, DT_PAD - dm.H), w_in.dtype)], axis=1).astype(BF16)
    g = dm.G
    per_group = lambda v: v.reshape(g, 1, GROUP_HEADS)
    dtb, alog, dsk = per_group(sm["dt_bias"]), per_group(sm["a_log"]), per_group(sm["d_skip"])
    b_gate = sm["b_gate"]
    b_ssd, b_sb = b_gate[:, :d], b_gate[:, d:]
    gcol = off["gate"] // d

    (n1,) = row_fwd("norm1", f_norm1, [x], [sm["norm_mix_pre"]], [(d, BF16)])
    proj = matmul(n1, w_all, name="in_proj")
    xbc = conv_fwd(proj, off["xbc"], dm.CD, wf["conv_w"].astype(F32), sm["conv_b"], "conv_fwd")
    dt_raw = proj[:, off["dt"]:off["dt"] + dm.H]
    dtc = dt_raw.reshape(s, g, GROUP_HEADS).transpose(1, 0, 2)
    dtw = dt_raw.reshape(s, g, GROUP_HEADS).transpose(1, 2, 0)
    y_ssd, states = ssd_fwd(proj, xbc, dtc, dtw, dtb, alog, dsk, sm["ssd_norm"], dm, "ssd_fwd")
    y_sb = attn_fwd(proj, dm, "attn_fwd")
    yb_ssd = matmul(y_ssd, wf["w_ssd_branch"], name="ssd_branch")
    yb_sb = matmul(y_sb, wf["w_sb_branch"], name="sb_branch")
    merge_rows = [(proj, d, gcol), (proj, d, gcol + 1), yb_ssd, yb_sb]
    (merged,) = row_fwd("merge", f_merge, merge_rows, [b_ssd, b_sb], [(d, BF16)])
    mo = matmul(merged, wf["w_out"], name="w_out")
    h1, n2 = row_fwd("mix_out", f_mix_out, [x, mo], [sm["norm_mix_post"], sm["norm_ffn_pre"]], [(d, F32), (d, BF16)])
    a1 = matmul(n2, wf["w_ff1"], name="ff1")
    (act,) = row_fwd("relu2", f_relu2, [a1], [], [(dm.DFF, BF16)])
    ff = matmul(act, wf["w_ff2"], name="ff2")
    (h2,) = row_fwd("ffn_out", f_ffn_out, [h1, ff], [sm["norm_ffn_post"]], [(d, F32)])
    pg = matmul(h2, wf["w_ple_gate"], name="ple_gate")
    pe = matmul(p, wf["w_ple"], name="ple_emb")

    gr = {}
    (dh2_a, dpg, dpe), (gr["norm_ple_post"], loss_cols) = row_bwd(
        "ple_loss", f_ple_loss, [h2, pg, pe, tgt], [sm["norm_ple_post"]], [None], [F32, BF16, BF16, None], primal_sum=True)
    loss = jnp.sum(loss_cols)
    gr["w_ple"] = matmul(p, dpe, ta=True, name="d_w_ple")
    gr["w_ple_gate"] = matmul(h2, dpg, ta=True, name="d_w_ple_gate")
    dh2_b = matmul(dpg, wf["w_ple_gate"], tb=True, name="d_h2")
    (dh1_a, dff), (gr["norm_ffn_post"],) = row_bwd(
        "ffn_out_bwd", f_ffn_out, [h1, ff], [sm["norm_ffn_post"]], [[dh2_a, dh2_b]], [F32, BF16])
    gr["w_ff2"] = matmul(act, dff, ta=True, name="d_w_ff2")
    dact = matmul(dff, wf["w_ff2"], tb=True, name="d_act")
    (da1,), _ = row_bwd("relu2_bwd", f_relu2, [a1], [], [[dact]], [BF16])
    gr["w_ff1"] = matmul(n2, da1, ta=True, name="d_w_ff1")
    dn2 = matmul(da1, wf["w_ff1"], tb=True, name="d_n2")
    (dx_a, dmo), (gr["norm_mix_post"], gr["norm_ffn_pre"]) = row_bwd(
        "mix_out_bwd", f_mix_out, [x, mo], [sm["norm_mix_post"], sm["norm_ffn_pre"]], [[dh1_a], [dn2]], [F32, BF16])
    gr["w_out"] = matmul(merged, dmo, ta=True, name="d_w_out")
    dmerged = matmul(dmo, wf["w_out"], tb=True, name="d_merged")
    (dgp_ssd, dgp_sb, dyb_ssd, dyb_sb), (db_ssd, db_sb) = row_bwd(
        "merge_bwd", f_merge, merge_rows, [b_ssd, b_sb], [[dmerged]], [BF16, BF16, BF16, BF16])
    gr["b_gate"] = jnp.concatenate([db_ssd, db_sb], axis=1)
    gr["w_ssd_branch"] = matmul(y_ssd, dyb_ssd, ta=True, name="d_w_ssd_branch")
    gr["w_sb_branch"] = matmul(y_sb, dyb_sb, ta=True, name="d_w_sb_branch")
    dy_ssd = matmul(dyb_ssd, wf["w_ssd_branch"], tb=True, name="d_y_ssd")
    dy_sb = matmul(dyb_sb, wf["w_sb_branch"], tb=True, name="d_y_sb")
    dq, dk, dv = attn_bwd(proj, dy_sb, dm, "attn_bwd")
    dz, dxs, dbm, dcm, ddtc, ddtw, ddtb, dalog, ddsk, gr["ssd_norm"] = ssd_bwd(
        proj, xbc, dtc, dtw, dtb, alog, dsk, sm["ssd_norm"], states, dy_ssd, dm, "ssd_bwd")
    gr["dt_bias"], gr["a_log"], gr["d_skip"] = (v.reshape(1, dm.H) for v in (ddtb, dalog, ddsk))
    conv_w32 = wf["conv_w"].astype(F32)
    du_x, dw_x, dcb_x = conv_bwd(proj, off["xbc"], dxs, 0, conv_w32, sm["conv_b"], "conv_bwd_x")
    du_b, dw_b, dcb_b = conv_bwd(proj, off["xbc"], dbm, dm.DI, conv_w32, sm["conv_b"], "conv_bwd_b")
    du_c, dw_c, dcb_c = conv_bwd(proj, off["xbc"], dcm, dm.DI + g * D_STATE, conv_w32, sm["conv_b"], "conv_bwd_c")
    gr["conv_w"] = jnp.concatenate([dw_x, dw_b, dw_c], axis=1)
    gr["conv_b"] = jnp.concatenate([dcb_x, dcb_b, dcb_c], axis=1)
    ddt = (ddtc.transpose(1, 0, 2) + ddtw.transpose(2, 0, 1)).reshape(s, dm.H)
    dproj = jnp.concatenate(
        [dz, du_x, du_b, du_c, dq, dk, dv, dgp_ssd, dgp_sb, ddt.astype(BF16), jnp.zeros((s, DT_PAD - dm.H), BF16)], axis=1)
    dw_all = matmul(n1, dproj, ta=True, name="d_w_all")
    gr["w_in"] = jnp.concatenate(
        [dw_all[:, :c0], dw_all[:, off["dt"]:off["dt"] + dm.H], dw_all[:, c0:c0 + 3 * dm.SBW]], axis=1)
    gr["w_gate"] = dw_all[:, off["gate"]:off["gate"] + 2 * d]
    dn1 = matmul(dproj, w_all, tb=True, name="d_n1")
    (dx_b,), (gr["norm_mix_pre"],) = row_bwd("norm1_bwd", f_norm1, [x], [sm["norm_mix_pre"]], [[dn1]], [F32])
    (grad_x,) = row_fwd("grad_x", lambda u, v: (u + v,), [dx_a, dx_b], [], [(d, F32)])
    return loss, grad_x, gr


def kernel(x, p, norm_mix_pre, w_in, conv_w, conv_b, dt_bias, a_log, d_skip, ssd_norm, w_ssd_branch, w_sb_branch, w_gate, b_gate, w_out, norm_mix_post, norm_ffn_pre, w_ff1, w_ff2, norm_ffn_post, w_ple, w_ple_gate, norm_ple_post, loss_target, m_norm_mix_pre, m_w_in, m_conv_w, m_conv_b, m_dt_bias, m_a_log, m_d_skip, m_ssd_norm, m_w_ssd_branch, m_w_sb_branch, m_w_gate, m_b_gate, m_w_out, m_norm_mix_post, m_norm_ffn_pre, m_w_ff1, m_w_ff2, m_norm_ffn_post, m_w_ple, m_w_ple_gate, m_norm_ple_post, v_norm_mix_pre, v_w_in, v_conv_w, v_conv_b, v_dt_bias, v_a_log, v_d_skip, v_ssd_norm, v_w_ssd_branch, v_w_sb_branch, v_w_gate, v_b_gate, v_w_out, v_norm_mix_post, v_norm_ffn_pre, v_w_ff1, v_w_ff2, v_norm_ffn_post, v_w_ple, v_w_ple_gate, v_norm_ple_post):
    loc = dict(locals())
    unbatch = lambda a: a[0] if a.ndim == 3 else a
    w = {n: unbatch(loc[n]) for n in ALL_WEIGHTS}
    m = {n: unbatch(loc["m_" + n]) for n in ALL_WEIGHTS}
    v = {n: unbatch(loc["v_" + n]) for n in ALL_WEIGHTS}
    xs, ps, tgt = x[0], p[0, 0], loss_target[0]
    s, d = xs.shape
    di = w["w_ssd_branch"].shape[0] * N_CHIPS
    cd = w["conv_b"].shape[1]
    dm = Dims(S=s, D=d, DI=di, H=w["dt_bias"].shape[1], G=(cd - di) // (2 * D_STATE), CD=cd,
              SBW=w["w_sb_branch"].shape[0] * N_CHIPS, DFF=w["w_ff2"].shape[0] * N_CHIPS, PLE=ps.shape[1])
    ix, iy, ic = lax.axis_index("x"), lax.axis_index("y"), lax.axis_index("c")
    chip_idx = jnp.reshape(2 * ix + iy, (1,)).astype(jnp.int32)
    core_idx = jnp.reshape(ic, (1,)).astype(jnp.int32)
    dev_idx = jnp.reshape(4 * ix + 2 * iy + ic, (1,)).astype(jnp.int32)

    shard_shapes = {n: w[n].shape for n in BIG_WEIGHTS}
    total_rows = _pack_rows(shard_shapes, d)
    gathered = gather_weights(pack_gather(w, d))
    wf = unpack_full(gathered, shard_shapes, d)
    sm = {n: w[n] for n in SMALL_WEIGHTS}

    loss_part, grad_x, gr = local_step(xs, ps, tgt, wf, sm, dm)
    loss = lax.psum(loss_part, ("x", "y", "c"))

    lay = _small_layout({n: w[n].shape[1] for n in SMALL_WEIGHTS}, d)
    gbig = pack_full(gr, shard_shapes, d, total_rows)
    gsmall = pack_small(gr, lay, d)
    from_sibling = swap_halves(gbig)
    pair_sum = add_half(gbig, from_sibling, core_idx, "add_half")
    from_chips, all_small = exchange_chips(pair_sum, gsmall)
    my_half = add_chips(pair_sum, from_chips, chip_idx, "add_chips")
    g_red = join_halves(my_half)
    gs_red = sum_small(all_small, dev_idx, "sum_small")

    d_big, nm_big, nv_big = adamw(pack_local(w, d, total_rows, F32), g_red, pack_local(m, d, total_rows, F32),
                                  pack_local(v, d, total_rows, F32), "adamw_big")
    d_sm, nm_sm, nv_sm = adamw(pack_small(w, lay, d), gs_red, pack_small(m, lay, d), pack_small(v, lay, d), "adamw_small")

    def unpack(big, small):
        out = unpack_local(big, shard_shapes, d)
        out.update(unpack_small(small, lay))
        return [out[n][None] if n in BIG_WEIGHTS else out[n] for n in ALL_WEIGHTS]

    return (loss, grad_x[None], *unpack(g_red, gs_red), *unpack(d_big, d_sm), *unpack(nm_big, nm_sm), *unpack(nv_big, nv_sm))
```

```python
import functools
from typing import NamedTuple

import jax
import jax.numpy as jnp
from jax import lax
from jax.experimental import pallas as pl
from jax.experimental.pallas import tpu as pltpu

F32 = jnp.float32
BF16 = jnp.bfloat16
SDS = jax.ShapeDtypeStruct

HEAD_DIM = 64
GROUP_HEADS = 4
D_STATE = 128
CHUNK = 128
ATT_TILE = 128
ATT_GROUP = 4
CONV_K = 4
CONV_COLS = 128
RMS_EPS = 1e-6
LANES = 128
DT_PAD = 512
N_CHIPS = 4
N_DEV = 8
SMALL_ROWS = 16
VMEM_LIMIT = 48 * 1024 * 1024

ADAM_LR = 0.001
ADAM_B1 = 0.9
ADAM_B2 = 0.999
ADAM_EPS = 1e-08
ADAM_WD = 0.01
ADAM_STEP = 10

MESH_ID = pl.DeviceIdType.MESH
HBM_SPEC = pl.BlockSpec(memory_space=pltpu.HBM)

BIG_WEIGHTS = ("w_in", "conv_w", "w_ssd_branch", "w_sb_branch", "w_gate", "w_out", "w_ff1", "w_ff2", "w_ple", "w_ple_gate")
SHARD_AXIS = {"w_in": 1, "conv_w": 1, "w_ssd_branch": 0, "w_sb_branch": 0, "w_gate": 1, "w_out": 0, "w_ff1": 1,
              "w_ff2": 0, "w_ple": 1, "w_ple_gate": 0}
SMALL_WEIGHTS = ("norm_mix_pre", "conv_b", "dt_bias", "a_log", "d_skip", "ssd_norm", "b_gate", "norm_mix_post",
                 "norm_ffn_pre", "norm_ffn_post", "norm_ple_post")
ALL_WEIGHTS = ("norm_mix_pre", "w_in", "conv_w", "conv_b", "dt_bias", "a_log", "d_skip", "ssd_norm", "w_ssd_branch",
               "w_sb_branch", "w_gate", "b_gate", "w_out", "norm_mix_post", "norm_ffn_pre", "w_ff1", "w_ff2",
               "norm_ffn_post", "w_ple", "w_ple_gate", "norm_ple_post")


class Dims(NamedTuple):
    S: int
    D: int
    DI: int
    H: int
    G: int
    CD: int
    SBW: int
    DFF: int
    PLE: int

    @property
    def NA(self):
        return self.DI + self.CD + 3 * self.SBW + 2 * self.D + DT_PAD

    @property
    def off(self):
        o = {}
        o["z"] = 0
        o["xbc"] = self.DI
        o["q"] = self.DI + self.CD
        o["k"] = o["q"] + self.SBW
        o["v"] = o["k"] + self.SBW
        o["gate"] = o["v"] + self.SBW
        o["dt"] = o["gate"] + 2 * self.D
        return o


def _cparams(sem):
    return pltpu.CompilerParams(dimension_semantics=sem, vmem_limit_bytes=VMEM_LIMIT)


def _pick(n, cands):
    for c in cands:
        if n % c == 0:
            return c
    raise ValueError(f"no tile for {n}")


def matmul(a, b, *, ta=False, tb=False, out_dtype=F32, name):
    m, k = (a.shape[1], a.shape[0]) if ta else a.shape
    n, kb = b.shape if tb else (b.shape[1], b.shape[0])
    assert k == kb, (a.shape, b.shape, ta, tb)
    tm = _pick(m, (1024, 512, 256, 128))
    tn = _pick(n, (512, 256, 128))
    tk = _pick(k, (1024, 512, 256, 128))
    nk = k // tk
    dims = (((0 if ta else 1,), (1 if tb else 0,)), ((), ()))

    def body(a_ref, b_ref, o_ref, acc_ref):
        part = lax.dot_general(a_ref[...].astype(BF16), b_ref[...].astype(BF16), dims, preferred_element_type=F32)
        if nk == 1:
            o_ref[...] = part.astype(o_ref.dtype)
        else:
            kk = pl.program_id(2)

            @pl.when(kk == 0)
            def _():
                acc_ref[...] = part

            @pl.when(kk > 0)
            def _():
                acc_ref[...] += part

            @pl.when(kk == nk - 1)
            def _():
                o_ref[...] = acc_ref[...].astype(o_ref.dtype)

    a_spec = pl.BlockSpec((tk, tm), lambda i, j, kk: (kk, i)) if ta else pl.BlockSpec((tm, tk), lambda i, j, kk: (i, kk))
    b_spec = pl.BlockSpec((tn, tk), lambda i, j, kk: (j, kk)) if tb else pl.BlockSpec((tk, tn), lambda i, j, kk: (kk, j))
    return pl.pallas_call(
        body, grid=(m // tm, n // tn, nk), in_specs=[a_spec, b_spec],
        out_specs=pl.BlockSpec((tm, tn), lambda i, j, kk: (i, j)),
        out_shape=SDS((m, n), out_dtype), scratch_shapes=[pltpu.VMEM((tm, tn), F32)],
        compiler_params=_cparams(("parallel", "parallel", "arbitrary")), name=name)(a, b)


def _row_spec(entry, tile):
    arr, width, cb = entry if isinstance(entry, tuple) else (entry, entry.shape[1], 0)
    return arr, pl.BlockSpec((tile, width), lambda i, cb=cb: (i, cb))


def _par_spec(p):
    return pl.BlockSpec(p.shape, lambda i: (0, 0))


def row_fwd(name, fn, rows, params, outs, tile=256):
    arrs, specs = zip(*[_row_spec(e, tile) for e in rows])
    s = arrs[0].shape[0]
    nr, npar = len(rows), len(params)

    def body(*refs):
        r = [x[...].astype(F32) for x in refs[:nr]]
        p = [x[...] for x in refs[nr:nr + npar]]
        res = fn(*r, *p)
        for o_ref, val in zip(refs[nr + npar:], res):
            o_ref[...] = val.astype(o_ref.dtype)

    return pl.pallas_call(
        body, grid=(s // tile,), in_specs=list(specs) + [_par_spec(p) for p in params],
        out_specs=[pl.BlockSpec((tile, w), lambda i: (i, 0)) for w, _ in outs],
        out_shape=[SDS((s, w), dt) for w, dt in outs],
        compiler_params=_cparams(("parallel",)), name=name)(*arrs, *params)


def row_bwd(name, fn, rows, params, cots, row_grads, tile=256, primal_sum=False):
    arrs, specs = zip(*[_row_spec(e, tile) for e in rows])
    s = arrs[0].shape[0]
    nr, npar = len(rows), len(params)
    cot_entries = [e for c in cots if c is not None for e in c]
    carrs, cspecs = zip(*[_row_spec(e, tile) for e in cot_entries]) if cot_entries else ((), ())
    nc = len(cot_entries)
    want = [i for i, d in enumerate(row_grads) if d is not None]

    def body(*refs):
        r = [x[...].astype(F32) for x in refs[:nr]]
        p = [x[...] for x in refs[nr:nr + npar]]
        cvals = [x[...].astype(F32) for x in refs[nr + npar:nr + npar + nc]]
        outs = refs[nr + npar + nc:]
        prim, vjp = jax.vjp(fn, *r, *p)
        ct, pos = [], 0
        for c, pr in zip(cots, prim):
            if c is None:
                ct.append(jnp.ones_like(pr))
            else:
                acc = cvals[pos]
                for extra in cvals[pos + 1:pos + len(c)]:
                    acc = acc + extra
                pos += len(c)
                ct.append(acc)
        grads = vjp(tuple(ct))
        for o_ref, i in zip(outs[:len(want)], want):
            o_ref[...] = grads[i].astype(o_ref.dtype)
        acc_refs = outs[len(want):]
        vals = [grads[nr + j] for j in range(npar)]
        if primal_sum:
            vals.append(jnp.sum(prim[0], axis=0, keepdims=True))
        first = pl.program_id(0) == 0

        @pl.when(first)
        def _():
            for a_ref, v in zip(acc_refs, vals):
                a_ref[...] = v

        @pl.when(jnp.logical_not(first))
        def _():
            for a_ref, v in zip(acc_refs, vals):
                a_ref[...] += v

    widths = [(e[1] if isinstance(e, tuple) else e.shape[1]) for e in rows]
    out_specs = [pl.BlockSpec((tile, widths[i]), lambda i_: (i_, 0)) for i in want]
    out_shape = [SDS((s, widths[i]), row_grads[i]) for i in want]
    pshapes = [p.shape for p in params]
    if primal_sum:
        pshapes.append((1, widths[0]))
    out_specs += [pl.BlockSpec(sh, lambda i_: (0, 0)) for sh in pshapes]
    out_shape += [SDS(sh, F32) for sh in pshapes]
    res = pl.pallas_call(
        body, grid=(s // tile,), in_specs=list(specs) + [_par_spec(p) for p in params] + list(cspecs),
        out_specs=out_specs, out_shape=out_shape,
        compiler_params=_cparams(("arbitrary",)), name=name)(*arrs, *params, *carrs)
    return res[:len(want)], res[len(want):]


def _rms(x, w):
    return x * lax.rsqrt(jnp.mean(x * x, axis=-1, keepdims=True) + RMS_EPS) * w


def _sigmoid(x):
    return jax.nn.sigmoid(x)


def _softplus(x):
    return jnp.maximum(x, 0.0) + jnp.log1p(jnp.exp(-jnp.abs(x)))


def f_norm1(x, w):
    return (_rms(x, w),)


def f_merge(gp_ssd, gp_sb, yb_ssd, yb_sb, b_ssd, b_sb):
    return (_sigmoid(gp_ssd + b_ssd) * yb_ssd + _sigmoid(gp_sb + b_sb) * yb_sb,)


def f_mix_out(x, mo, w_post, w_pre):
    h1 = x + _rms(mo, w_post)
    return h1, _rms(h1, w_pre)


def f_relu2(a1):
    return (jnp.square(jnp.maximum(a1, 0.0)),)


def f_ffn_out(h1, ff, w):
    return (h1 + _rms(ff, w),)


def f_ple_loss(h2, pg, pe, tgt, w):
    h3 = h2 + _rms(_sigmoid(pg) * pe, w)
    return (0.5 * jnp.square(h3 - tgt) * (1.0 / h2.shape[-1]),)


def _shift_down(u, d, rows):
    return u if d == 0 else jnp.where(rows >= d, pltpu.roll(u, d, 0), 0.0)


def _shift_up(u, d, rows):
    s = u.shape[0]
    return u if d == 0 else jnp.where(rows < s - d, pltpu.roll(u, s - d, 0), 0.0)


def conv_fwd(proj, col0, cd, conv_w, conv_b, name):
    s = proj.shape[0]
    cb0 = col0 // CONV_COLS

    def body(u_ref, w_ref, b_ref, o_ref):
        u = u_ref[...]
        rows = lax.broadcasted_iota(jnp.int32, u.shape, 0)
        y = jnp.broadcast_to(b_ref[...], u.shape)
        for k in range(CONV_K):
            y = y + w_ref[k:k + 1, :] * _shift_down(u, CONV_K - 1 - k, rows)
        o_ref[...] = y * _sigmoid(y)

    return pl.pallas_call(
        body, grid=(cd // CONV_COLS,),
        in_specs=[pl.BlockSpec((s, CONV_COLS), lambda i: (0, cb0 + i)),
                  pl.BlockSpec((CONV_K, CONV_COLS), lambda i: (0, i)),
                  pl.BlockSpec((1, CONV_COLS), lambda i: (0, i))],
        out_specs=pl.BlockSpec((s, CONV_COLS), lambda i: (0, i)),
        out_shape=SDS((s, cd), F32), compiler_params=_cparams(("parallel",)), name=name)(proj, conv_w, conv_b)


def conv_bwd(proj, col0, dout, ch0, conv_w, conv_b, name):
    s = proj.shape[0]
    ncb = dout.shape[1] // CONV_COLS
    cb0 = (col0 + ch0) // CONV_COLS
    wb0 = ch0 // CONV_COLS

    def body(u_ref, g_ref, w_ref, b_ref, du_ref, dw_ref, db_ref):
        u = u_ref[...]
        rows = lax.broadcasted_iota(jnp.int32, u.shape, 0)
        y = jnp.broadcast_to(b_ref[...], u.shape)
        for k in range(CONV_K):
            y = y + w_ref[k:k + 1, :] * _shift_down(u, CONV_K - 1 - k, rows)
        sg = _sigmoid(y)
        dy = g_ref[...] * (sg * (1.0 + y * (1.0 - sg)))
        du = jnp.zeros_like(u)
        for k in range(CONV_K):
            d = CONV_K - 1 - k
            du = du + w_ref[k:k + 1, :] * _shift_up(dy, d, rows)
            dw_ref[k:k + 1, :] = jnp.sum(dy * _shift_down(u, d, rows), axis=0, keepdims=True)
        du_ref[...] = du.astype(du_ref.dtype)
        db_ref[...] = jnp.sum(dy, axis=0, keepdims=True)

    return pl.pallas_call(
        body, grid=(ncb,),
        in_specs=[pl.BlockSpec((s, CONV_COLS), lambda i: (0, cb0 + i)),
                  pl.BlockSpec((s, CONV_COLS), lambda i: (0, i)),
                  pl.BlockSpec((CONV_K, CONV_COLS), lambda i: (0, wb0 + i)),
                  pl.BlockSpec((1, CONV_COLS), lambda i: (0, wb0 + i))],
        out_specs=[pl.BlockSpec((s, CONV_COLS), lambda i: (0, i)),
                   pl.BlockSpec((CONV_K, CONV_COLS), lambda i: (0, i)),
                   pl.BlockSpec((1, CONV_COLS), lambda i: (0, i))],
        out_shape=[SDS((s, ncb * CONV_COLS), BF16), SDS((CONV_K, ncb * CONV_COLS), F32), SDS((1, ncb * CONV_COLS), F32)],
        compiler_params=_cparams(("parallel",)), name=name)(proj, dout, conv_w, conv_b)


def _dot(a, b, dims):
    return lax.dot_general(a.astype(BF16), b.astype(BF16), (dims, ((), ())), preferred_element_type=F32)


NN = ((1,), (0,))
NT = ((1,), (1,))
TN = ((0,), (0,))


def ssd_chunk(xs, zs, nws, dtc, dtw, dtbs, alogs, dsks, bm, cm, prevs):
    ln = bm.shape[0]
    row = lax.broadcasted_iota(jnp.int32, (ln, ln), 0)
    col = lax.broadcasted_iota(jnp.int32, (ln, ln), 1)
    causal = row >= col
    tri = causal.astype(F32)
    cb = _dot(cm, bm, NT)
    ygs, news = [], []
    for r in range(GROUP_HEADS):
        a = -jnp.exp(alogs[r])
        dt_c = _softplus(dtc[r] + dtbs[r])
        da_c = dt_c * a
        da_w = _softplus(dtw[r] + dtbs[r]) * a
        acs_c = jnp.sum(tri * da_w, axis=1, keepdims=True)
        acs_w = jnp.sum((row <= col).astype(F32) * da_c, axis=0, keepdims=True)
        alast = jnp.sum(da_c, axis=0, keepdims=True)
        xd = xs[r] * dt_c
        decay = jnp.exp(jnp.where(causal, acs_c - acs_w, -jnp.inf))
        y = _dot(cb * decay, xd, NN)
        y = y + _dot(cm, prevs[r], NT) * jnp.exp(acs_c)
        y = y + xs[r] * dsks[r]
        st = _dot(xd * jnp.exp(alast - acs_c), bm, TN)
        news.append(prevs[r] * jnp.exp(alast) + st)
        ygs.append(y * (zs[r] * _sigmoid(zs[r])))
    ss = sum(jnp.sum(v * v, axis=-1, keepdims=True) for v in ygs)
    rstd = lax.rsqrt(ss * (1.0 / (GROUP_HEADS * HEAD_DIM)) + RMS_EPS)
    outs = [ygs[r] * rstd * nws[r] for r in range(GROUP_HEADS)]
    return tuple(outs), tuple(news)


def _ssd_specs(dm, cidx):
    gw = GROUP_HEADS * HEAD_DIM
    nb0 = dm.DI // D_STATE
    par = pl.BlockSpec((None, 1, GROUP_HEADS), lambda g, c: (g, 0, 0))
    return dict(
        z=pl.BlockSpec((CHUNK, gw), lambda g, c: (cidx(c), g)),
        xs=pl.BlockSpec((CHUNK, gw), lambda g, c: (cidx(c), g)),
        b=pl.BlockSpec((CHUNK, D_STATE), lambda g, c: (cidx(c), nb0 + g)),
        c=pl.BlockSpec((CHUNK, D_STATE), lambda g, c: (cidx(c), nb0 + dm.G + g)),
        dtc=pl.BlockSpec((None, CHUNK, GROUP_HEADS), lambda g, c: (g, cidx(c), 0)),
        dtw=pl.BlockSpec((None, GROUP_HEADS, CHUNK), lambda g, c: (g, 0, cidx(c))),
        par=par,
        nw=pl.BlockSpec((1, gw), lambda g, c: (0, g)),
        st=pl.BlockSpec((None, None, GROUP_HEADS, HEAD_DIM, D_STATE), lambda g, c: (g, cidx(c), 0, 0, 0)),
    )


def _ssd_load(z_ref, xs_ref, b_ref, c_ref, dtc_ref, dtw_ref, dtb_ref, alog_ref, dsk_ref, nw_ref):
    hs = [slice(r * HEAD_DIM, (r + 1) * HEAD_DIM) for r in range(GROUP_HEADS)]
    xs = tuple(xs_ref[:, h] for h in hs)
    zs = tuple(z_ref[:, h] for h in hs)
    nws = tuple(nw_ref[:, h] for h in hs)
    dtc = tuple(dtc_ref[:, r:r + 1] for r in range(GROUP_HEADS))
    dtw = tuple(dtw_ref[r:r + 1, :] for r in range(GROUP_HEADS))
    dtbs = tuple(dtb_ref[:, r:r + 1] for r in range(GROUP_HEADS))
    alogs = tuple(alog_ref[:, r:r + 1] for r in range(GROUP_HEADS))
    dsks = tuple(dsk_ref[:, r:r + 1] for r in range(GROUP_HEADS))
    return hs, (xs, zs, nws, dtc, dtw, dtbs, alogs, dsks, b_ref[...], c_ref[...])


def ssd_fwd(proj, xbc, dtc, dtw, dtb, alog, dsk, nw, dm, name):
    nc = dm.S // CHUNK
    sp = _ssd_specs(dm, lambda c: c)

    def body(z_ref, xs_ref, b_ref, c_ref, dtc_ref, dtw_ref, dtb_ref, alog_ref, dsk_ref, nw_ref, y_ref, st_ref, prev):
        @pl.when(pl.program_id(1) == 0)
        def _():
            prev[...] = jnp.zeros_like(prev)

        hs, args = _ssd_load(z_ref, xs_ref, b_ref, c_ref, dtc_ref, dtw_ref, dtb_ref, alog_ref, dsk_ref, nw_ref)
        st_ref[...] = prev[...]
        outs, news = ssd_chunk(*args, tuple(prev[r] for r in range(GROUP_HEADS)))
        for r in range(GROUP_HEADS):
            y_ref[:, hs[r]] = outs[r].astype(y_ref.dtype)
            prev[r] = news[r]

    return pl.pallas_call(
        body, grid=(dm.G, nc),
        in_specs=[sp["z"], sp["xs"], sp["b"], sp["c"], sp["dtc"], sp["dtw"], sp["par"], sp["par"], sp["par"], sp["nw"]],
        out_specs=[sp["xs"], sp["st"]],
        out_shape=[SDS((dm.S, dm.DI), BF16), SDS((dm.G, nc, GROUP_HEADS, HEAD_DIM, D_STATE), F32)],
        scratch_shapes=[pltpu.VMEM((GROUP_HEADS, HEAD_DIM, D_STATE), F32)],
        compiler_params=_cparams(("parallel", "arbitrary")), name=name)(proj, xbc, xbc, xbc, dtc, dtw, dtb, alog, dsk, nw)


def ssd_bwd(proj, xbc, dtc, dtw, dtb, alog, dsk, nw, states, dy, dm, name):
    nc = dm.S // CHUNK
    sp = _ssd_specs(dm, lambda c: nc - 1 - c)
    gw = GROUP_HEADS * HEAD_DIM
    bc_spec = pl.BlockSpec((CHUNK, D_STATE), lambda g, c: (nc - 1 - c, g))

    def body(z_ref, xs_ref, b_ref, c_ref, dtc_ref, dtw_ref, dtb_ref, alog_ref, dsk_ref, nw_ref, st_ref, dy_ref,
             dz_ref, dxs_ref, db_ref, dc_ref, ddtc_ref, ddtw_ref, ddtb_ref, dalog_ref, ddsk_ref, dnw_ref, dprev):
        first = pl.program_id(1) == 0

        @pl.when(first)
        def _():
            dprev[...] = jnp.zeros_like(dprev)

        hs, args = _ssd_load(z_ref, xs_ref, b_ref, c_ref, dtc_ref, dtw_ref, dtb_ref, alog_ref, dsk_ref, nw_ref)
        prevs = tuple(st_ref[r] for r in range(GROUP_HEADS))
        _, vjp = jax.vjp(ssd_chunk, *args, prevs)
        ct = (tuple(dy_ref[:, h] for h in hs), tuple(dprev[r] for r in range(GROUP_HEADS)))
        gxs, gzs, gnws, gdtc, gdtw, gdtbs, galogs, gdsks, gb, gc, gprevs = vjp(ct)
        db_ref[...] = gb
        dc_ref[...] = gc
        for r in range(GROUP_HEADS):
            dxs_ref[:, hs[r]] = gxs[r]
            dz_ref[:, hs[r]] = gzs[r].astype(dz_ref.dtype)
            ddtc_ref[:, r:r + 1] = gdtc[r]
            ddtw_ref[r:r + 1, :] = gdtw[r]
            dprev[r] = gprevs[r]

        @pl.when(first)
        def _():
            for r in range(GROUP_HEADS):
                dnw_ref[:, hs[r]] = gnws[r]
                ddtb_ref[:, r:r + 1] = gdtbs[r]
                dalog_ref[:, r:r + 1] = galogs[r]
                ddsk_ref[:, r:r + 1] = gdsks[r]

        @pl.when(jnp.logical_not(first))
        def _():
            for r in range(GROUP_HEADS):
                dnw_ref[:, hs[r]] += gnws[r]
                ddtb_ref[:, r:r + 1] += gdtbs[r]
                dalog_ref[:, r:r + 1] += galogs[r]
                ddsk_ref[:, r:r + 1] += gdsks[r]

    xs_out = pl.BlockSpec((CHUNK, gw), lambda g, c: (nc - 1 - c, g))
    return pl.pallas_call(
        body, grid=(dm.G, nc),
        in_specs=[sp["z"], sp["xs"], sp["b"], sp["c"], sp["dtc"], sp["dtw"], sp["par"], sp["par"], sp["par"], sp["nw"],
                  sp["st"], xs_out],
        out_specs=[xs_out, xs_out, bc_spec, bc_spec, sp["dtc"], sp["dtw"], sp["par"], sp["par"], sp["par"], sp["nw"]],
        out_shape=[SDS((dm.S, dm.DI), BF16), SDS((dm.S, dm.DI), F32), SDS((dm.S, dm.G * D_STATE), F32),
                   SDS((dm.S, dm.G * D_STATE), F32), SDS((dm.G, dm.S, GROUP_HEADS), F32), SDS((dm.G, GROUP_HEADS, dm.S), F32),
                   SDS((dm.G, 1, GROUP_HEADS), F32), SDS((dm.G, 1, GROUP_HEADS), F32), SDS((dm.G, 1, GROUP_HEADS), F32),
                   SDS((1, dm.DI), F32)],
        scratch_shapes=[pltpu.VMEM((GROUP_HEADS, HEAD_DIM, D_STATE), F32)],
        compiler_params=_cparams(("parallel", "arbitrary")), name=name)(
            proj, xbc, xbc, xbc, dtc, dtw, dtb, alog, dsk, nw, states, dy)


def _split_bf16(v):
    hi = v.astype(BF16)
    return hi, (v - hi.astype(F32)).astype(BF16)


def _tri(v, mat):
    hi, lo = _split_bf16(v)
    return jnp.dot(hi, mat, preferred_element_type=F32) + jnp.dot(lo, mat, preferred_element_type=F32)


def _blocks(v):
    return [v[:, b * ATT_TILE:(b + 1) * ATT_TILE] for b in range(v.shape[1] // ATT_TILE)]


def _sb_group(z, mask, run, after_mat):
    sp = jnp.maximum(z, 0.0) + jnp.log(1.0 + jnp.exp(-jnp.abs(z)))
    lk = -sp if mask is None else jnp.where(mask, -sp, 0.0)
    cums = [_tri(v, after_mat) for v in _blocks(lk)]
    sums = [jnp.sum(v, axis=1, keepdims=True) for v in _blocks(lk)]
    later = [None] * len(cums)
    for b in reversed(range(len(cums))):
        later[b] = run + cums[b]
        run = run + sums[b]
    ls = z - sp
    w = jnp.exp(ls + jnp.concatenate(later, axis=1))
    if mask is not None:
        w = jnp.where(mask, w, 0.0)
    return ls, w, run


def _group_mask(i, g, t, gw):
    rows = i * t + lax.broadcasted_iota(jnp.int32, (t, gw), 0)
    cols = g * gw + lax.broadcasted_iota(jnp.int32, (t, gw), 1)
    return cols < rows


def _att_specs(dm, s):
    t = ATT_TILE
    qb, kb, vb = dm.off["q"] // LANES, dm.off["k"] // LANES, dm.off["v"] // LANES
    return (pl.BlockSpec((t, LANES), lambda p, i: (i, qb + p)),
            pl.BlockSpec((s, LANES), lambda p, i: (0, kb + p)),
            pl.BlockSpec((s, LANES), lambda p, i: (0, vb + p)))


def attn_fwd(proj, dm, name):
    s, t = dm.S, ATT_TILE
    scale = HEAD_DIM ** -0.5
    hsl = [slice(h * HEAD_DIM, (h + 1) * HEAD_DIM) for h in range(2)]

    gw = ATT_GROUP * t

    def body(q_ref, k_ref, v_ref, o_ref):
        i = pl.program_id(1)
        gd = i // ATT_GROUP
        r_io = lax.broadcasted_iota(jnp.int32, (t, t), 0)
        c_io = lax.broadcasted_iota(jnp.int32, (t, t), 1)
        after_mat = (r_io > c_io).astype(BF16)
        qs = [q_ref[:, sl].astype(BF16) for sl in hsl]

        def group(g, carry, mask):
            r0 = pl.multiple_of(g * gw, gw)
            zs = [_dot(qs[h], k_ref[pl.ds(r0, gw), hsl[h]], NT) * scale for h in range(2)]
            res = [_sb_group(zs[h], mask, carry[h][0], after_mat) for h in range(2)]
            return tuple((res[h][2], carry[h][1] + _dot(res[h][1], v_ref[pl.ds(r0, gw), hsl[h]], NN)) for h in range(2))

        zero = (jnp.zeros((t, 1), F32), jnp.zeros((t, HEAD_DIM), F32))
        carry = group(gd, (zero, zero), _group_mask(i, gd, t, gw))
        carry = lax.fori_loop(1, gd + 1, lambda jj, c: group(gd - jj, c, None), carry)
        for h in range(2):
            o_ref[:, hsl[h]] = carry[h][1]

    qs_, ks_, vs_ = _att_specs(dm, s)
    return pl.pallas_call(
        body, grid=(dm.SBW // LANES, s // t), in_specs=[qs_, ks_, vs_],
        out_specs=pl.BlockSpec((t, LANES), lambda p, i: (i, p)),
        out_shape=SDS((s, dm.SBW), F32), compiler_params=_cparams(("parallel", "arbitrary")), name=name)(proj, proj, proj)


def attn_bwd(proj, do, dm, name):
    s, t = dm.S, ATT_TILE
    nq = s // t
    gw = ATT_GROUP * t
    scale = HEAD_DIM ** -0.5
    hsl = [slice(h * HEAD_DIM, (h + 1) * HEAD_DIM) for h in range(2)]

    def body(q_ref, k_ref, v_ref, do_ref, dq_ref, dk_ref, dv_ref, dk_acc, dv_acc, g_scr, s_scr):
        i = pl.program_id(1)

        @pl.when(i == 0)
        def _():
            dk_acc[...] = jnp.zeros_like(dk_acc)
            dv_acc[...] = jnp.zeros_like(dv_acc)

        gd = i // ATT_GROUP
        r_io = lax.broadcasted_iota(jnp.int32, (t, t), 0)
        c_io = lax.broadcasted_iota(jnp.int32, (t, t), 1)
        after_mat = (r_io > c_io).astype(BF16)
        before_mat = (r_io < c_io).astype(BF16)
        qs = [q_ref[:, sl].astype(BF16) for sl in hsl]
        dos = [do_ref[:, sl].astype(BF16) for sl in hsl]
        q_t = q_ref[...].T.astype(BF16)
        do_t = do_ref[...].T.astype(BF16)

        def pass1(g, runs, mask):
            r0 = pl.multiple_of(g * gw, gw)
            zs = [_dot(qs[h], k_ref[pl.ds(r0, gw), hsl[h]], NT) * scale for h in range(2)]
            dws = [_dot(dos[h], v_ref[pl.ds(r0, gw), hsl[h]], NT) for h in range(2)]
            out = []
            for h in range(2):
                ls, w, run = _sb_group(zs[h], mask, runs[h], after_mat)
                g_scr[h, g] = dws[h] * w
                s_scr[h, g] = jnp.exp(ls)
                dv_acc[g, hsl[h], :] += _dot(do_t[hsl[h]], w, NN)
                out.append(run)
            return tuple(out)

        diag_mask = _group_mask(i, gd, t, gw)
        zero_col = jnp.zeros((t, 1), F32)
        runs = pass1(gd, (zero_col, zero_col), diag_mask)
        lax.fori_loop(1, gd + 1, lambda jj, r: pass1(gd - jj, r, None), runs)

        def pass2(g, carry, mask):
            r0 = pl.multiple_of(g * gw, gw)
            out = []
            for h in range(2):
                pre, dq = carry[h]
                gg = g_scr[h, g]
                sig = s_scr[h, g]
                before = []
                for v in _blocks(gg):
                    before.append(pre + _tri(v, before_mat))
                    pre = pre + jnp.sum(v, axis=1, keepdims=True)
                dz = gg * (1.0 - sig) - jnp.concatenate(before, axis=1) * sig
                if mask is not None:
                    dz = jnp.where(mask, dz, 0.0)
                dz = (dz * scale).astype(BF16)
                dk_acc[g, hsl[h], :] += _dot(q_t[hsl[h]], dz, NN)
                out.append((pre, dq + _dot(dz, k_ref[pl.ds(r0, gw), hsl[h]], NN)))
            return tuple(out)

        zero = (zero_col, jnp.zeros((t, HEAD_DIM), F32))
        carry = lax.fori_loop(0, gd, lambda g, c: pass2(g, c, None), (zero, zero))
        carry = pass2(gd, carry, diag_mask)
        for h in range(2):
            dq_ref[:, hsl[h]] = carry[h][1].astype(dq_ref.dtype)

        @pl.when(i == nq - 1)
        def _():
            for g in range(s // gw):
                dk_ref[g * gw:(g + 1) * gw, :] = dk_acc[g].T.astype(dk_ref.dtype)
                dv_ref[g * gw:(g + 1) * gw, :] = dv_acc[g].T.astype(dv_ref.dtype)

    qs_, ks_, vs_ = _att_specs(dm, s)
    tile_spec = pl.BlockSpec((t, LANES), lambda p, i: (i, p))
    full_spec = pl.BlockSpec((s, LANES), lambda p, i: (0, p))
    return pl.pallas_call(
        body, grid=(dm.SBW // LANES, nq), in_specs=[qs_, ks_, vs_, tile_spec],
        out_specs=[tile_spec, full_spec, full_spec],
        out_shape=[SDS((s, dm.SBW), BF16)] * 3,
        scratch_shapes=[pltpu.VMEM((s // gw, LANES, gw), F32), pltpu.VMEM((s // gw, LANES, gw), F32),
                        pltpu.VMEM((2, s // gw, t, gw), F32), pltpu.VMEM((2, s // gw, t, gw), F32)],
        compiler_params=_cparams(("parallel", "arbitrary")), name=name)(proj, proj, proj, do)


def adamw(w, g, m, v, name):
    rows, width = w.shape
    tile = _pick(rows, (240, 208, 16, 8))
    c1 = 1.0 / (1.0 - ADAM_B1 ** ADAM_STEP)
    c2 = 1.0 / (1.0 - ADAM_B2 ** ADAM_STEP)

    def body(w_ref, g_ref, m_ref, v_ref, d_ref, nm_ref, nv_ref):
        gg = g_ref[...]
        nm = ADAM_B1 * m_ref[...] + (1.0 - ADAM_B1) * gg
        nv = ADAM_B2 * v_ref[...] + (1.0 - ADAM_B2) * (gg * gg)
        d_ref[...] = -ADAM_LR * ((nm * c1) / (jnp.sqrt(nv * c2) + ADAM_EPS) + ADAM_WD * w_ref[...])
        nm_ref[...] = nm
        nv_ref[...] = nv

    spec = pl.BlockSpec((tile, width), lambda i: (i, 0))
    return pl.pallas_call(
        body, grid=(rows // tile,), in_specs=[spec] * 4, out_specs=[spec] * 3,
        out_shape=[SDS((rows, width), F32)] * 3, compiler_params=_cparams(("parallel",)), name=name)(w, g, m, v)


def _me():
    return lax.axis_index("x"), lax.axis_index("y"), lax.axis_index("c")


def _other_chips(x, y):
    return [(1 - x, y), (x, 1 - y), (1 - x, 1 - y)]


def gather_weights(wp):
    rows, width = wp.shape
    half = rows // 2

    def body(w_ref, out_ref, send_sems, recv_sems, local_sem):
        x, y, c = _me()
        sibling = (x, y, 1 - c)
        chips = _other_chips(x, y)

        def part(cx, cy, hf):
            return out_ref.at[2 * cx + cy, pl.ds(hf * half, half), :]

        def copy(k, src, dst, to):
            return pltpu.make_async_remote_copy(src_ref=src, dst_ref=dst, send_sem=send_sems.at[k], recv_sem=recv_sems.at[k],
                                                device_id=to, device_id_type=MESH_ID)

        mine = pltpu.make_async_copy(w_ref, out_ref.at[2 * x + y], local_sem)
        mine.start()
        first = [copy(j, w_ref.at[pl.ds(c * half, half), :], part(x, y, c), (cx, cy, c)) for j, (cx, cy) in enumerate(chips)]
        for cp in first:
            cp.start()
        passed = [copy(3 + j, part(cx, cy, c), part(cx, cy, c), sibling) for j, (cx, cy) in enumerate(chips)]
        for j, (cx, cy) in enumerate(chips):
            copy(j, part(cx, cy, c), part(cx, cy, c), (x, y, c)).wait_recv()
            passed[j].start()
        for j, (cx, cy) in enumerate(chips):
            copy(3 + j, part(cx, cy, 1 - c), part(cx, cy, 1 - c), (x, y, c)).wait_recv()
        for cp in first + passed:
            cp.wait_send()
        mine.wait()

    return pl.pallas_call(
        body, out_shape=SDS((N_CHIPS, rows, width), wp.dtype), in_specs=[HBM_SPEC], out_specs=HBM_SPEC,
        scratch_shapes=[pltpu.SemaphoreType.DMA((6,)), pltpu.SemaphoreType.DMA((6,)), pltpu.SemaphoreType.DMA],
        name="gather_weights")(wp)


def swap_halves(g):
    n, rows, width = g.shape
    half = rows // 2

    def body(g_ref, a_ref, send_sem, recv_sem):
        x, y, c = _me()
        cp = pltpu.make_async_remote_copy(src_ref=g_ref.at[:, pl.ds((1 - c) * half, half), :], dst_ref=a_ref,
                                          send_sem=send_sem, recv_sem=recv_sem, device_id=(x, y, 1 - c), device_id_type=MESH_ID)
        cp.start()
        cp.wait()

    return pl.pallas_call(
        body, out_shape=SDS((n, half, width), g.dtype), in_specs=[HBM_SPEC], out_specs=HBM_SPEC,
        scratch_shapes=[pltpu.SemaphoreType.DMA, pltpu.SemaphoreType.DMA], name="swap_halves")(g)


def add_half(g, a, c_idx, name):
    n, rows, width = g.shape
    half = rows // 2
    tile = _pick(half, (240, 208, 16, 8))
    nt = half // tile

    def body(c_ref, g_ref, a_ref, o_ref):
        o_ref[...] = g_ref[...] + a_ref[...]

    gs = pltpu.PrefetchScalarGridSpec(
        num_scalar_prefetch=1, grid=(n, nt),
        in_specs=[pl.BlockSpec((None, tile, width), lambda s, i, c_ref: (s, c_ref[0] * nt + i, 0)),
                  pl.BlockSpec((None, tile, width), lambda s, i, c_ref: (s, i, 0))],
        out_specs=pl.BlockSpec((None, tile, width), lambda s, i, c_ref: (s, i, 0)))
    return pl.pallas_call(body, grid_spec=gs, out_shape=SDS((n, half, width), g.dtype),
                          compiler_params=_cparams(("parallel", "parallel")), name=name)(c_idx, g, a)


def exchange_chips(sh, small):
    n, hf, width = sh.shape

    def body(s_ref, sm_ref, b_ref, all_ref, send_sems, recv_sems, local_sem):
        x, y, c = _me()
        chips = _other_chips(x, y)
        mine = pltpu.make_async_copy(sm_ref, all_ref.at[0], local_sem)
        mine.start()
        copies = []
        for j, (cx, cy) in enumerate(chips):
            copies.append(pltpu.make_async_remote_copy(
                src_ref=s_ref.at[2 * cx + cy], dst_ref=b_ref.at[j], send_sem=send_sems.at[j], recv_sem=recv_sems.at[j],
                device_id=(cx, cy, c), device_id_type=MESH_ID))
        for m in range(1, N_DEV):
            peer = (x ^ ((m >> 2) & 1), y ^ ((m >> 1) & 1), c ^ (m & 1))
            copies.append(pltpu.make_async_remote_copy(
                src_ref=sm_ref, dst_ref=all_ref.at[m], send_sem=send_sems.at[2 + m], recv_sem=recv_sems.at[2 + m],
                device_id=peer, device_id_type=MESH_ID))
        for cp in copies:
            cp.start()
        for cp in copies:
            cp.wait()
        mine.wait()

    return pl.pallas_call(
        body, out_shape=[SDS((3, hf, width), sh.dtype), SDS((N_DEV, SMALL_ROWS, width), small.dtype)],
        in_specs=[HBM_SPEC, HBM_SPEC], out_specs=[HBM_SPEC, HBM_SPEC],
        scratch_shapes=[pltpu.SemaphoreType.DMA((10,)), pltpu.SemaphoreType.DMA((10,)), pltpu.SemaphoreType.DMA],
        name="exchange_chips")(sh, small)


def add_chips(sh, b, k_idx, name):
    n, hf, width = sh.shape
    tile = _pick(hf, (240, 208, 16, 8))

    def body(k_ref, s_ref, b0, b1, b2, o_ref):
        o_ref[...] = ((s_ref[...] + b0[...]) + b1[...]) + b2[...]

    def bspec(j):
        return pl.BlockSpec((None, tile, width), lambda i, k_ref, j=j: (j, i, 0))

    gs = pltpu.PrefetchScalarGridSpec(
        num_scalar_prefetch=1, grid=(hf // tile,),
        in_specs=[pl.BlockSpec((None, tile, width), lambda i, k_ref: (k_ref[0], i, 0)), bspec(0), bspec(1), bspec(2)],
        out_specs=pl.BlockSpec((tile, width), lambda i, k_ref: (i, 0)))
    return pl.pallas_call(body, grid_spec=gs, out_shape=SDS((hf, width), sh.dtype),
                          compiler_params=_cparams(("parallel",)), name=name)(k_idx, sh, b, b, b)


def sum_small(allsm, me_idx, name):
    _, rows, width = allsm.shape

    def body(me_ref, a_ref, o_ref):
        me = me_ref[0]
        acc = a_ref[me]
        for dev in range(1, N_DEV):
            acc = acc + a_ref[jnp.bitwise_xor(me, dev)]
        o_ref[...] = acc

    gs = pltpu.PrefetchScalarGridSpec(
        num_scalar_prefetch=1, grid=(1,),
        in_specs=[pl.BlockSpec((N_DEV, rows, width), lambda i, me_ref: (0, 0, 0))],
        out_specs=pl.BlockSpec((rows, width), lambda i, me_ref: (0, 0)))
    return pl.pallas_call(body, grid_spec=gs, out_shape=SDS((rows, width), allsm.dtype),
                          compiler_params=_cparams(("arbitrary",)), name=name)(me_idx, allsm)


def join_halves(t):
    hf, width = t.shape

    def body(t_ref, o_ref, send_sem, recv_sem, local_sem):
        x, y, c = _me()
        mine = pltpu.make_async_copy(t_ref, o_ref.at[pl.ds(c * hf, hf), :], local_sem)
        mine.start()
        cp = pltpu.make_async_remote_copy(src_ref=t_ref, dst_ref=o_ref.at[pl.ds(c * hf, hf), :], send_sem=send_sem,
                                          recv_sem=recv_sem, device_id=(x, y, 1 - c), device_id_type=MESH_ID)
        cp.start()
        cp.wait_send()
        pltpu.make_async_remote_copy(src_ref=t_ref, dst_ref=o_ref.at[pl.ds((1 - c) * hf, hf), :], send_sem=send_sem,
                                     recv_sem=recv_sem, device_id=(x, y, 1 - c), device_id_type=MESH_ID).wait_recv()
        mine.wait()

    return pl.pallas_call(
        body, out_shape=SDS((2 * hf, width), t.dtype), in_specs=[HBM_SPEC], out_specs=HBM_SPEC,
        scratch_shapes=[pltpu.SemaphoreType.DMA, pltpu.SemaphoreType.DMA, pltpu.SemaphoreType.DMA], name="join_halves")(t)


def _pack_rows(shard_shapes, width):
    rows = sum((a * b) // width for a, b in shard_shapes.values())
    return -(-rows // 32) * 32


def pack_local(shards, width, total_rows, dtype):
    parts = [shards[n].reshape(-1, width).astype(dtype) for n in BIG_WEIGHTS]
    used = sum(p.shape[0] for p in parts)
    parts.append(jnp.zeros((total_rows - used, width), dtype))
    return jnp.concatenate(parts, axis=0)


def unpack_local(packed, shard_shapes, width):
    out, r0 = {}, 0
    for n in BIG_WEIGHTS:
        a, b = shard_shapes[n]
        nr = (a * b) // width
        out[n] = packed[r0:r0 + nr].reshape(a, b)
        r0 += nr
    return out


EXACT_IN_GATHER = ("conv_w",)
EXACT_TERMS = 3


def pack_gather(shards, width):
    parts = []
    for n in BIG_WEIGHTS:
        if n in EXACT_IN_GATHER:
            rest = shards[n].astype(F32)
            for _ in range(EXACT_TERMS):
                term = rest.astype(BF16)
                parts.append(term.reshape(-1, width))
                rest = rest - term.astype(F32)
        else:
            parts.append(shards[n].reshape(-1, width).astype(BF16))
    used = sum(p.shape[0] for p in parts)
    parts.append(jnp.zeros((-(-used // 32) * 32 - used, width), BF16))
    return jnp.concatenate(parts, axis=0)


def unpack_full(gathered, shard_shapes, width):
    out, r0 = {}, 0
    for n in BIG_WEIGHTS:
        a, b = shard_shapes[n]
        terms = EXACT_TERMS if n in EXACT_IN_GATHER else 1
        nr = (a * b) // width
        pieces = []
        for j in range(N_CHIPS):
            blk = gathered[j, r0:r0 + nr].reshape(a, b)
            for t in range(1, terms):
                blk = blk.astype(F32) + gathered[j, r0 + t * nr:r0 + (t + 1) * nr].reshape(a, b).astype(F32)
            pieces.append(blk)
        out[n] = jnp.concatenate(pieces, axis=SHARD_AXIS[n])
        r0 += terms * nr
    return out


def pack_full(grads, shard_shapes, width, total_rows):
    slabs = []
    for j in range(N_CHIPS):
        parts = []
        for n in BIG_WEIGHTS:
            a, b = shard_shapes[n]
            ax = SHARD_AXIS[n]
            sz = (a, b)[ax]
            piece = lax.slice_in_dim(grads[n], j * sz, (j + 1) * sz, axis=ax)
            parts.append(piece.reshape(-1, width))
        used = sum(p.shape[0] for p in parts)
        parts.append(jnp.zeros((total_rows - used, width), F32))
        slabs.append(jnp.concatenate(parts, axis=0))
    return jnp.stack(slabs, axis=0)


def _small_layout(sizes, width):
    lay, r = {}, 0
    for n in SMALL_WEIGHTS:
        nr = -(-sizes[n] // width)
        lay[n] = (r, nr, sizes[n])
        r += nr
    assert r <= SMALL_ROWS
    return lay


def pack_small(vals, lay, width):
    rows = []
    for n in SMALL_WEIGHTS:
        r, nr, sz = lay[n]
        v = vals[n].reshape(-1).astype(F32)
        rows.append(jnp.pad(v, (0, nr * width - sz)).reshape(nr, width))
    used = sum(r.shape[0] for r in rows)
    rows.append(jnp.zeros((SMALL_ROWS - used, width), F32))
    return jnp.concatenate(rows, axis=0)


def unpack_small(packed, lay):
    return {n: packed[r:r + nr].reshape(-1)[:sz].reshape(1, sz) for n, (r, nr, sz) in lay.items()}


def local_step(x, p, tgt, wf, sm, dm):
    s, d = dm.S, dm.D
    off = dm.off
    w_in = wf["w_in"]
    c0 = dm.DI + dm.CD
    w_all = jnp.concatenate(
        [w_in[:, :c0], w_in[:, c0 + dm.H:], wf["w_gate"], w_in[:, c0:c0 + dm.H],
         jnp.zeros((d, DT_PAD - dm.H), w_in.dtype)], axis=1).astype(BF16)
    g = dm.G
    per_group = lambda v: v.reshape(g, 1, GROUP_HEADS)
    dtb, alog, dsk = per_group(sm["dt_bias"]), per_group(sm["a_log"]), per_group(sm["d_skip"])
    b_gate = sm["b_gate"]
    b_ssd, b_sb = b_gate[:, :d], b_gate[:, d:]
    gcol = off["gate"] // d

    (n1,) = row_fwd("norm1", f_norm1, [x], [sm["norm_mix_pre"]], [(d, BF16)])
    proj = matmul(n1, w_all, name="in_proj")
    xbc = conv_fwd(proj, off["xbc"], dm.CD, wf["conv_w"].astype(F32), sm["conv_b"], "conv_fwd")
    dt_raw = proj[:, off["dt"]:off["dt"] + dm.H]
    dtc = dt_raw.reshape(s, g, GROUP_HEADS).transpose(1, 0, 2)
    dtw = dt_raw.reshape(s, g, GROUP_HEADS).transpose(1, 2, 0)
    y_ssd, states = ssd_fwd(proj, xbc, dtc, dtw, dtb, alog, dsk, sm["ssd_norm"], dm, "ssd_fwd")
    y_sb = attn_fwd(proj, dm, "attn_fwd")
    yb_ssd = matmul(y_ssd, wf["w_ssd_branch"], name="ssd_branch")
    yb_sb = matmul(y_sb, wf["w_sb_branch"], name="sb_branch")
    merge_rows = [(proj, d, gcol), (proj, d, gcol + 1), yb_ssd, yb_sb]
    (merged,) = row_fwd("merge", f_merge, merge_rows, [b_ssd, b_sb], [(d, BF16)])
    mo = matmul(merged, wf["w_out"], name="w_out")
    h1, n2 = row_fwd("mix_out", f_mix_out, [x, mo], [sm["norm_mix_post"], sm["norm_ffn_pre"]], [(d, F32), (d, BF16)])
    a1 = matmul(n2, wf["w_ff1"], name="ff1")
    (act,) = row_fwd("relu2", f_relu2, [a1], [], [(dm.DFF, BF16)])
    ff = matmul(act, wf["w_ff2"], name="ff2")
    (h2,) = row_fwd("ffn_out", f_ffn_out, [h1, ff], [sm["norm_ffn_post"]], [(d, F32)])
    pg = matmul(h2, wf["w_ple_gate"], name="ple_gate")
    pe = matmul(p, wf["w_ple"], name="ple_emb")

    gr = {}
    (dh2_a, dpg, dpe), (gr["norm_ple_post"], loss_cols) = row_bwd(
        "ple_loss", f_ple_loss, [h2, pg, pe, tgt], [sm["norm_ple_post"]], [None], [F32, BF16, BF16, None], primal_sum=True)
    loss = jnp.sum(loss_cols)
    gr["w_ple"] = matmul(p, dpe, ta=True, name="d_w_ple")
    gr["w_ple_gate"] = matmul(h2, dpg, ta=True, name="d_w_ple_gate")
    dh2_b = matmul(dpg, wf["w_ple_gate"], tb=True, name="d_h2")
    (dh1_a, dff), (gr["norm_ffn_post"],) = row_bwd(
        "ffn_out_bwd", f_ffn_out, [h1, ff], [sm["norm_ffn_post"]], [[dh2_a, dh2_b]], [F32, BF16])
    gr["w_ff2"] = matmul(act, dff, ta=True, name="d_w_ff2")
    dact = matmul(dff, wf["w_ff2"], tb=True, name="d_act")
    (da1,), _ = row_bwd("relu2_bwd", f_relu2, [a1], [], [[dact]], [BF16])
    gr["w_ff1"] = matmul(n2, da1, ta=True, name="d_w_ff1")
    dn2 = matmul(da1, wf["w_ff1"], tb=True, name="d_n2")
    (dx_a, dmo), (gr["norm_mix_post"], gr["norm_ffn_pre"]) = row_bwd(
        "mix_out_bwd", f_mix_out, [x, mo], [sm["norm_mix_post"], sm["norm_ffn_pre"]], [[dh1_a], [dn2]], [F32, BF16])
    gr["w_out"] = matmul(merged, dmo, ta=True, name="d_w_out")
    dmerged = matmul(dmo, wf["w_out"], tb=True, name="d_merged")
    (dgp_ssd, dgp_sb, dyb_ssd, dyb_sb), (db_ssd, db_sb) = row_bwd(
        "merge_bwd", f_merge, merge_rows, [b_ssd, b_sb], [[dmerged]], [BF16, BF16, BF16, BF16])
    gr["b_gate"] = jnp.concatenate([db_ssd, db_sb], axis=1)
    gr["w_ssd_branch"] = matmul(y_ssd, dyb_ssd, ta=True, name="d_w_ssd_branch")
    gr["w_sb_branch"] = matmul(y_sb, dyb_sb, ta=True, name="d_w_sb_branch")
    dy_ssd = matmul(dyb_ssd, wf["w_ssd_branch"], tb=True, name="d_y_ssd")
    dy_sb = matmul(dyb_sb, wf["w_sb_branch"], tb=True, name="d_y_sb")
    dq, dk, dv = attn_bwd(proj, dy_sb, dm, "attn_bwd")
    dz, dxs, dbm, dcm, ddtc, ddtw, ddtb, dalog, ddsk, gr["ssd_norm"] = ssd_bwd(
        proj, xbc, dtc, dtw, dtb, alog, dsk, sm["ssd_norm"], states, dy_ssd, dm, "ssd_bwd")
    gr["dt_bias"], gr["a_log"], gr["d_skip"] = (v.reshape(1, dm.H) for v in (ddtb, dalog, ddsk))
    conv_w32 = wf["conv_w"].astype(F32)
    du_x, dw_x, dcb_x = conv_bwd(proj, off["xbc"], dxs, 0, conv_w32, sm["conv_b"], "conv_bwd_x")
    du_b, dw_b, dcb_b = conv_bwd(proj, off["xbc"], dbm, dm.DI, conv_w32, sm["conv_b"], "conv_bwd_b")
    du_c, dw_c, dcb_c = conv_bwd(proj, off["xbc"], dcm, dm.DI + g * D_STATE, conv_w32, sm["conv_b"], "conv_bwd_c")
    gr["conv_w"] = jnp.concatenate([dw_x, dw_b, dw_c], axis=1)
    gr["conv_b"] = jnp.concatenate([dcb_x, dcb_b, dcb_c], axis=1)
    ddt = (ddtc.transpose(1, 0, 2) + ddtw.transpose(2, 0, 1)).reshape(s, dm.H)
    dproj = jnp.concatenate(
        [dz, du_x, du_b, du_c, dq, dk, dv, dgp_ssd, dgp_sb, ddt.astype(BF16), jnp.zeros((s, DT_PAD - dm.H), BF16)], axis=1)
    dw_all = matmul(n1, dproj, ta=True, name="d_w_all")
    gr["w_in"] = jnp.concatenate(
        [dw_all[:, :c0], dw_all[:, off["dt"]:off["dt"] + dm.H], dw_all[:, c0:c0 + 3 * dm.SBW]], axis=1)
    gr["w_gate"] = dw_all[:, off["gate"]:off["gate"] + 2 * d]
    dn1 = matmul(dproj, w_all, tb=True, name="d_n1")
    (dx_b,), (gr["norm_mix_pre"],) = row_bwd("norm1_bwd", f_norm1, [x], [sm["norm_mix_pre"]], [[dn1]], [F32])
    (grad_x,) = row_fwd("grad_x", lambda u, v: (u + v,), [dx_a, dx_b], [], [(d, F32)])
    return loss, grad_x, gr


def kernel(x, p, norm_mix_pre, w_in, conv_w, conv_b, dt_bias, a_log, d_skip, ssd_norm, w_ssd_branch, w_sb_branch, w_gate, b_gate, w_out, norm_mix_post, norm_ffn_pre, w_ff1, w_ff2, norm_ffn_post, w_ple, w_ple_gate, norm_ple_post, loss_target, m_norm_mix_pre, m_w_in, m_conv_w, m_conv_b, m_dt_bias, m_a_log, m_d_skip, m_ssd_norm, m_w_ssd_branch, m_w_sb_branch, m_w_gate, m_b_gate, m_w_out, m_norm_mix_post, m_norm_ffn_pre, m_w_ff1, m_w_ff2, m_norm_ffn_post, m_w_ple, m_w_ple_gate, m_norm_ple_post, v_norm_mix_pre, v_w_in, v_conv_w, v_conv_b, v_dt_bias, v_a_log, v_d_skip, v_ssd_norm, v_w_ssd_branch, v_w_sb_branch, v_w_gate, v_b_gate, v_w_out, v_norm_mix_post, v_norm_ffn_pre, v_w_ff1, v_w_ff2, v_norm_ffn_post, v_w_ple, v_w_ple_gate, v_norm_ple_post):
    loc = dict(locals())
    unbatch = lambda a: a[0] if a.ndim == 3 else a
    w = {n: unbatch(loc[n]) for n in ALL_WEIGHTS}
    m = {n: unbatch(loc["m_" + n]) for n in ALL_WEIGHTS}
    v = {n: unbatch(loc["v_" + n]) for n in ALL_WEIGHTS}
    xs, ps, tgt = x[0], p[0, 0], loss_target[0]
    s, d = xs.shape
    di = w["w_ssd_branch"].shape[0] * N_CHIPS
    cd = w["conv_b"].shape[1]
    dm = Dims(S=s, D=d, DI=di, H=w["dt_bias"].shape[1], G=(cd - di) // (2 * D_STATE), CD=cd,
              SBW=w["w_sb_branch"].shape[0] * N_CHIPS, DFF=w["w_ff2"].shape[0] * N_CHIPS, PLE=ps.shape[1])
    ix, iy, ic = lax.axis_index("x"), lax.axis_index("y"), lax.axis_index("c")
    chip_idx = jnp.reshape(2 * ix + iy, (1,)).astype(jnp.int32)
    core_idx = jnp.reshape(ic, (1,)).astype(jnp.int32)
    dev_idx = jnp.reshape(4 * ix + 2 * iy + ic, (1,)).astype(jnp.int32)

    shard_shapes = {n: w[n].shape for n in BIG_WEIGHTS}
    total_rows = _pack_rows(shard_shapes, d)
    gathered = gather_weights(pack_gather(w, d))
    wf = unpack_full(gathered, shard_shapes, d)
    sm = {n: w[n] for n in SMALL_WEIGHTS}

    loss_part, grad_x, gr = local_step(xs, ps, tgt, wf, sm, dm)
    loss = lax.psum(loss_part, ("x", "y", "c"))

    lay = _small_layout({n: w[n].shape[1] for n in SMALL_WEIGHTS}, d)
    gbig = pack_full(gr, shard_shapes, d, total_rows)
    gsmall = pack_small(gr, lay, d)
    from_sibling = swap_halves(gbig)
    pair_sum = add_half(gbig, from_sibling, core_idx, "add_half")
    from_chips, all_small = exchange_chips(pair_sum, gsmall)
    my_half = add_chips(pair_sum, from_chips, chip_idx, "add_chips")
    g_red = join_halves(my_half)
    gs_red = sum_small(all_small, dev_idx, "sum_small")

    d_big, nm_big, nv_big = adamw(pack_local(w, d, total_rows, F32), g_red, pack_local(m, d, total_rows, F32),
                                  pack_local(v, d, total_rows, F32), "adamw_big")
    d_sm, nm_sm, nv_sm = adamw(pack_small(w, lay, d), gs_red, pack_small(m, lay, d), pack_small(v, lay, d), "adamw_small")

    def unpack(big, small):
        out = unpack_local(big, shard_shapes, d)
        out.update(unpack_small(small, lay))
        return [out[n][None] if n in BIG_WEIGHTS else out[n] for n in ALL_WEIGHTS]

    return (loss, grad_x[None], *unpack(g_red, gs_red), *unpack(d_big, d_sm), *unpack(nm_big, nm_sm), *unpack(nv_big, nv_sm))
```

```python
import functools
from typing import NamedTuple

import jax
import jax.numpy as jnp
from jax import lax
from jax.experimental import pallas as pl
from jax.experimental.pallas import tpu as pltpu

F32 = jnp.float32
BF16 = jnp.bfloat16
SDS = jax.ShapeDtypeStruct

HEAD_DIM = 64
GROUP_HEADS = 4
D_STATE = 128
CHUNK = 128
ATT_TILE = 128
ATT_GROUP = 4
CONV_K = 4
CONV_COLS = 128
RMS_EPS = 1e-6
LANES = 128
DT_PAD = 512
N_CHIPS = 4
N_DEV = 8
SMALL_ROWS = 16
VMEM_LIMIT = 48 * 1024 * 1024

ADAM_LR = 0.001
ADAM_B1 = 0.9
ADAM_B2 = 0.999
ADAM_EPS = 1e-08
ADAM_WD = 0.01
ADAM_STEP = 10

MESH_ID = pl.DeviceIdType.MESH
HBM_SPEC = pl.BlockSpec(memory_space=pltpu.HBM)

BIG_WEIGHTS = ("w_in", "conv_w", "w_ssd_branch", "w_sb_branch", "w_gate", "w_out", "w_ff1", "w_ff2", "w_ple", "w_ple_gate")
SHARD_AXIS = {"w_in": 1, "conv_w": 1, "w_ssd_branch": 0, "w_sb_branch": 0, "w_gate": 1, "w_out": 0, "w_ff1": 1,
              "w_ff2": 0, "w_ple": 1, "w_ple_gate": 0}
SMALL_WEIGHTS = ("norm_mix_pre", "conv_b", "dt_bias", "a_log", "d_skip", "ssd_norm", "b_gate", "norm_mix_post",
                 "norm_ffn_pre", "norm_ffn_post", "norm_ple_post")
ALL_WEIGHTS = ("norm_mix_pre", "w_in", "conv_w", "conv_b", "dt_bias", "a_log", "d_skip", "ssd_norm", "w_ssd_branch",
               "w_sb_branch", "w_gate", "b_gate", "w_out", "norm_mix_post", "norm_ffn_pre", "w_ff1", "w_ff2",
               "norm_ffn_post", "w_ple", "w_ple_gate", "norm_ple_post")


class Dims(NamedTuple):
    S: int
    D: int
    DI: int
    H: int
    G: int
    CD: int
    SBW: int
    DFF: int
    PLE: int

    @property
    def NA(self):
        return self.DI + self.CD + 3 * self.SBW + 2 * self.D + DT_PAD

    @property
    def off(self):
        o = {}
        o["z"] = 0
        o["xbc"] = self.DI
        o["q"] = self.DI + self.CD
        o["k"] = o["q"] + self.SBW
        o["v"] = o["k"] + self.SBW
        o["gate"] = o["v"] + self.SBW
        o["dt"] = o["gate"] + 2 * self.D
        return o


def _cparams(sem):
    return pltpu.CompilerParams(dimension_semantics=sem, vmem_limit_bytes=VMEM_LIMIT)


def _pick(n, cands):
    for c in cands:
        if n % c == 0:
            return c
    raise ValueError(f"no tile for {n}")


def matmul(a, b, *, ta=False, tb=False, out_dtype=F32, name):
    m, k = (a.shape[1], a.shape[0]) if ta else a.shape
    n, kb = b.shape if tb else (b.shape[1], b.shape[0])
    assert k == kb, (a.shape, b.shape, ta, tb)
    tm = _pick(m, (1024, 512, 256, 128))
    tn = _pick(n, (512, 256, 128))
    tk = _pick(k, (1024, 512, 256, 128))
    nk = k // tk
    dims = (((0 if ta else 1,), (1 if tb else 0,)), ((), ()))

    def body(a_ref, b_ref, o_ref, acc_ref):
        part = lax.dot_general(a_ref[...].astype(BF16), b_ref[...].astype(BF16), dims, preferred_element_type=F32)
        if nk == 1:
            o_ref[...] = part.astype(o_ref.dtype)
        else:
            kk = pl.program_id(2)

            @pl.when(kk == 0)
            def _():
                acc_ref[...] = part

            @pl.when(kk > 0)
            def _():
                acc_ref[...] += part

            @pl.when(kk == nk - 1)
            def _():
                o_ref[...] = acc_ref[...].astype(o_ref.dtype)

    a_spec = pl.BlockSpec((tk, tm), lambda i, j, kk: (kk, i)) if ta else pl.BlockSpec((tm, tk), lambda i, j, kk: (i, kk))
    b_spec = pl.BlockSpec((tn, tk), lambda i, j, kk: (j, kk)) if tb else pl.BlockSpec((tk, tn), lambda i, j, kk: (kk, j))
    return pl.pallas_call(
        body, grid=(m // tm, n // tn, nk), in_specs=[a_spec, b_spec],
        out_specs=pl.BlockSpec((tm, tn), lambda i, j, kk: (i, j)),
        out_shape=SDS((m, n), out_dtype), scratch_shapes=[pltpu.VMEM((tm, tn), F32)],
        compiler_params=_cparams(("parallel", "parallel", "arbitrary")), name=name)(a, b)


def _row_spec(entry, tile):
    arr, width, cb = entry if isinstance(entry, tuple) else (entry, entry.shape[1], 0)
    return arr, pl.BlockSpec((tile, width), lambda i, cb=cb: (i, cb))


def _par_spec(p):
    return pl.BlockSpec(p.shape, lambda i: (0, 0))


def row_fwd(name, fn, rows, params, outs, tile=256):
    arrs, specs = zip(*[_row_spec(e, tile) for e in rows])
    s = arrs[0].shape[0]
    nr, npar = len(rows), len(params)

    def body(*refs):
        r = [x[...].astype(F32) for x in refs[:nr]]
        p = [x[...] for x in refs[nr:nr + npar]]
        res = fn(*r, *p)
        for o_ref, val in zip(refs[nr + npar:], res):
            o_ref[...] = val.astype(o_ref.dtype)

    return pl.pallas_call(
        body, grid=(s // tile,), in_specs=list(specs) + [_par_spec(p) for p in params],
        out_specs=[pl.BlockSpec((tile, w), lambda i: (i, 0)) for w, _ in outs],
        out_shape=[SDS((s, w), dt) for w, dt in outs],
        compiler_params=_cparams(("parallel",)), name=name)(*arrs, *params)


def row_bwd(name, fn, rows, params, cots, row_grads, tile=256, primal_sum=False):
    arrs, specs = zip(*[_row_spec(e, tile) for e in rows])
    s = arrs[0].shape[0]
    nr, npar = len(rows), len(params)
    cot_entries = [e for c in cots if c is not None for e in c]
    carrs, cspecs = zip(*[_row_spec(e, tile) for e in cot_entries]) if cot_entries else ((), ())
    nc = len(cot_entries)
    want = [i for i, d in enumerate(row_grads) if d is not None]

    def body(*refs):
        r = [x[...].astype(F32) for x in refs[:nr]]
        p = [x[...] for x in refs[nr:nr + npar]]
        cvals = [x[...].astype(F32) for x in refs[nr + npar:nr + npar + nc]]
        outs = refs[nr + npar + nc:]
        prim, vjp = jax.vjp(fn, *r, *p)
        ct, pos = [], 0
        for c, pr in zip(cots, prim):
            if c is None:
                ct.append(jnp.ones_like(pr))
            else:
                acc = cvals[pos]
                for extra in cvals[pos + 1:pos + len(c)]:
                    acc = acc + extra
                pos += len(c)
                ct.append(acc)
        grads = vjp(tuple(ct))
        for o_ref, i in zip(outs[:len(want)], want):
            o_ref[...] = grads[i].astype(o_ref.dtype)
        acc_refs = outs[len(want):]
        vals = [grads[nr + j] for j in range(npar)]
        if primal_sum:
            vals.append(jnp.sum(prim[0], axis=0, keepdims=True))
        first = pl.program_id(0) == 0

        @pl.when(first)
        def _():
            for a_ref, v in zip(acc_refs, vals):
                a_ref[...] = v

        @pl.when(jnp.logical_not(first))
        def _():
            for a_ref, v in zip(acc_refs, vals):
                a_ref[...] += v

    widths = [(e[1] if isinstance(e, tuple) else e.shape[1]) for e in rows]
    out_specs = [pl.BlockSpec((tile, widths[i]), lambda i_: (i_, 0)) for i in want]
    out_shape = [SDS((s, widths[i]), row_grads[i]) for i in want]
    pshapes = [p.shape for p in params]
    if primal_sum:
        pshapes.append((1, widths[0]))
    out_specs += [pl.BlockSpec(sh, lambda i_: (0, 0)) for sh in pshapes]
    out_shape += [SDS(sh, F32) for sh in pshapes]
    res = pl.pallas_call(
        body, grid=(s // tile,), in_specs=list(specs) + [_par_spec(p) for p in params] + list(cspecs),
        out_specs=out_specs, out_shape=out_shape,
        compiler_params=_cparams(("arbitrary",)), name=name)(*arrs, *params, *carrs)
    return res[:len(want)], res[len(want):]


def _rms(x, w):
    return x * lax.rsqrt(jnp.mean(x * x, axis=-1, keepdims=True) + RMS_EPS) * w


def _sigmoid(x):
    return jax.nn.sigmoid(x)


def _softplus(x):
    return jnp.maximum(x, 0.0) + jnp.log1p(jnp.exp(-jnp.abs(x)))


def f_norm1(x, w):
    return (_rms(x, w),)


def f_dt(raw, bias):
    return (_softplus(raw + bias),)


def f_merge(gp_ssd, gp_sb, yb_ssd, yb_sb, b_ssd, b_sb):
    return (_sigmoid(gp_ssd + b_ssd) * yb_ssd + _sigmoid(gp_sb + b_sb) * yb_sb,)


def f_mix_out(x, mo, w_post, w_pre):
    h1 = x + _rms(mo, w_post)
    return h1, _rms(h1, w_pre)


def f_relu2(a1):
    return (jnp.square(jnp.maximum(a1, 0.0)),)


def f_ffn_out(h1, ff, w):
    return (h1 + _rms(ff, w),)


def f_ple_loss(h2, pg, pe, tgt, w):
    h3 = h2 + _rms(_sigmoid(pg) * pe, w)
    return (0.5 * jnp.square(h3 - tgt) * (1.0 / h2.shape[-1]),)


def _shift_down(u, d, rows):
    return u if d == 0 else jnp.where(rows >= d, pltpu.roll(u, d, 0), 0.0)


def _shift_up(u, d, rows):
    s = u.shape[0]
    return u if d == 0 else jnp.where(rows < s - d, pltpu.roll(u, s - d, 0), 0.0)


def conv_fwd(proj, col0, cd, conv_w, conv_b, name):
    s = proj.shape[0]
    cb0 = col0 // CONV_COLS

    def body(u_ref, w_ref, b_ref, o_ref):
        u = u_ref[...]
        rows = lax.broadcasted_iota(jnp.int32, u.shape, 0)
        y = jnp.broadcast_to(b_ref[...], u.shape)
        for k in range(CONV_K):
            y = y + w_ref[k:k + 1, :] * _shift_down(u, CONV_K - 1 - k, rows)
        o_ref[...] = y * _sigmoid(y)

    return pl.pallas_call(
        body, grid=(cd // CONV_COLS,),
        in_specs=[pl.BlockSpec((s, CONV_COLS), lambda i: (0, cb0 + i)),
                  pl.BlockSpec((CONV_K, CONV_COLS), lambda i: (0, i)),
                  pl.BlockSpec((1, CONV_COLS), lambda i: (0, i))],
        out_specs=pl.BlockSpec((s, CONV_COLS), lambda i: (0, i)),
        out_shape=SDS((s, cd), F32), compiler_params=_cparams(("parallel",)), name=name)(proj, conv_w, conv_b)


def conv_bwd(proj, col0, dout, ch0, conv_w, conv_b, name):
    s = proj.shape[0]
    ncb = dout.shape[1] // CONV_COLS
    cb0 = (col0 + ch0) // CONV_COLS
    wb0 = ch0 // CONV_COLS

    def body(u_ref, g_ref, w_ref, b_ref, du_ref, dw_ref, db_ref):
        u = u_ref[...]
        rows = lax.broadcasted_iota(jnp.int32, u.shape, 0)
        y = jnp.broadcast_to(b_ref[...], u.shape)
        for k in range(CONV_K):
            y = y + w_ref[k:k + 1, :] * _shift_down(u, CONV_K - 1 - k, rows)
        sg = _sigmoid(y)
        dy = g_ref[...] * (sg * (1.0 + y * (1.0 - sg)))
        du = jnp.zeros_like(u)
        for k in range(CONV_K):
            d = CONV_K - 1 - k
            du = du + w_ref[k:k + 1, :] * _shift_up(dy, d, rows)
            dw_ref[k:k + 1, :] = jnp.sum(dy * _shift_down(u, d, rows), axis=0, keepdims=True)
        du_ref[...] = du.astype(du_ref.dtype)
        db_ref[...] = jnp.sum(dy, axis=0, keepdims=True)

    return pl.pallas_call(
        body, grid=(ncb,),
        in_specs=[pl.BlockSpec((s, CONV_COLS), lambda i: (0, cb0 + i)),
                  pl.BlockSpec((s, CONV_COLS), lambda i: (0, i)),
                  pl.BlockSpec((CONV_K, CONV_COLS), lambda i: (0, wb0 + i)),
                  pl.BlockSpec((1, CONV_COLS), lambda i: (0, wb0 + i))],
        out_specs=[pl.BlockSpec((s, CONV_COLS), lambda i: (0, i)),
                   pl.BlockSpec((CONV_K, CONV_COLS), lambda i: (0, i)),
                   pl.BlockSpec((1, CONV_COLS), lambda i: (0, i))],
        out_shape=[SDS((s, ncb * CONV_COLS), BF16), SDS((CONV_K, ncb * CONV_COLS), F32), SDS((1, ncb * CONV_COLS), F32)],
        compiler_params=_cparams(("parallel",)), name=name)(proj, dout, conv_w, conv_b)


def _dot(a, b, dims):
    return lax.dot_general(a.astype(BF16), b.astype(BF16), (dims, ((), ())), preferred_element_type=F32)


NN = ((1,), (0,))
NT = ((1,), (1,))
TN = ((0,), (0,))


def ssd_chunk(xs, zs, nw, dtc, dtw, alogs, dsks, bm, cm, prev):
    ln = bm.shape[0]
    gw = GROUP_HEADS * HEAD_DIM
    row = lax.broadcasted_iota(jnp.int32, (ln, ln), 0)
    col = lax.broadcasted_iota(jnp.int32, (ln, ln), 1)
    causal = row >= col
    tri = causal.astype(F32)
    tri_t = (row <= col).astype(F32)
    lane_head = lax.broadcasted_iota(jnp.int32, (1, gw), 1) // HEAD_DIM
    sub_head = lax.broadcasted_iota(jnp.int32, (gw, 1), 0) // HEAD_DIM
    on_lanes = [(lane_head == r).astype(F32) for r in range(GROUP_HEADS)]
    on_rows = [(sub_head == r).astype(F32) for r in range(GROUP_HEADS)]
    cb = _dot(cm, bm, NT)
    decays, dt_full, acs_full, end_full, dsk_full, end_rows = [], 0.0, 0.0, 0.0, 0.0, 0.0
    for r in range(GROUP_HEADS):
        a = -jnp.exp(alogs[r])
        da_c = dtc[r] * a
        da_w = dtw[r] * a
        acs_c = jnp.sum(tri * da_w, axis=1, keepdims=True)
        acs_w = jnp.sum(tri_t * da_c, axis=0, keepdims=True)
        alast = jnp.sum(da_w, axis=1, keepdims=True)
        decays.append(jnp.exp(jnp.where(causal, acs_c - acs_w, -jnp.inf)))
        dt_full = dt_full + dtc[r] * on_lanes[r]
        acs_full = acs_full + acs_c * on_lanes[r]
        end_full = end_full + alast * on_lanes[r]
        dsk_full = dsk_full + dsks[r] * on_lanes[r]
        end_rows = end_rows + alast * on_rows[r]
    xd = xs * dt_full
    y = xs * dsk_full + _dot(cm, prev, NT) * jnp.exp(acs_full)
    for r in range(GROUP_HEADS):
        y = y + _dot(cb * decays[r], xd * on_lanes[r], NN)
    new_prev = prev * jnp.exp(end_rows) + _dot(xd * jnp.exp(end_full - acs_full), bm, TN)
    yg = y * (zs * _sigmoid(zs))
    rstd = lax.rsqrt(jnp.mean(yg * yg, axis=-1, keepdims=True) + RMS_EPS)
    return yg * rstd * nw, new_prev


def _ssd_specs(dm, cidx):
    gw = GROUP_HEADS * HEAD_DIM
    nb0 = dm.DI // D_STATE
    par = pl.BlockSpec((None, 1, GROUP_HEADS), lambda g, c: (g, 0, 0))
    return dict(
        z=pl.BlockSpec((CHUNK, gw), lambda g, c: (cidx(c), g)),
        xs=pl.BlockSpec((CHUNK, gw), lambda g, c: (cidx(c), g)),
        b=pl.BlockSpec((CHUNK, D_STATE), lambda g, c: (cidx(c), nb0 + g)),
        c=pl.BlockSpec((CHUNK, D_STATE), lambda g, c: (cidx(c), nb0 + dm.G + g)),
        dtc=pl.BlockSpec((None, CHUNK, GROUP_HEADS), lambda g, c: (g, cidx(c), 0)),
        dtw=pl.BlockSpec((None, GROUP_HEADS, CHUNK), lambda g, c: (g, 0, cidx(c))),
        par=par,
        nw=pl.BlockSpec((1, gw), lambda g, c: (0, g)),
        st=pl.BlockSpec((None, None, gw, D_STATE), lambda g, c: (g, cidx(c), 0, 0)),
    )


def _ssd_load(z_ref, xs_ref, b_ref, c_ref, dtc_ref, dtw_ref, alog_ref, dsk_ref, nw_ref):
    dtc = tuple(dtc_ref[:, r:r + 1] for r in range(GROUP_HEADS))
    dtw = tuple(dtw_ref[r:r + 1, :] for r in range(GROUP_HEADS))
    alogs = tuple(alog_ref[:, r:r + 1] for r in range(GROUP_HEADS))
    dsks = tuple(dsk_ref[:, r:r + 1] for r in range(GROUP_HEADS))
    return xs_ref[...], z_ref[...], nw_ref[...], dtc, dtw, alogs, dsks, b_ref[...], c_ref[...]


def ssd_fwd(proj, xbc, dtc, dtw, alog, dsk, nw, dm, name):
    nc = dm.S // CHUNK
    gw = GROUP_HEADS * HEAD_DIM
    sp = _ssd_specs(dm, lambda c: c)

    def body(z_ref, xs_ref, b_ref, c_ref, dtc_ref, dtw_ref, alog_ref, dsk_ref, nw_ref, y_ref, st_ref, prev):
        @pl.when(pl.program_id(1) == 0)
        def _():
            prev[...] = jnp.zeros_like(prev)

        args = _ssd_load(z_ref, xs_ref, b_ref, c_ref, dtc_ref, dtw_ref, alog_ref, dsk_ref, nw_ref)
        st_ref[...] = prev[...]
        out, new = ssd_chunk(*args, prev[...])
        y_ref[...] = out.astype(y_ref.dtype)
        prev[...] = new

    return pl.pallas_call(
        body, grid=(dm.G, nc),
        in_specs=[sp["z"], sp["xs"], sp["b"], sp["c"], sp["dtc"], sp["dtw"], sp["par"], sp["par"], sp["nw"]],
        out_specs=[sp["xs"], sp["st"]],
        out_shape=[SDS((dm.S, dm.DI), BF16), SDS((dm.G, nc, gw, D_STATE), F32)],
        scratch_shapes=[pltpu.VMEM((gw, D_STATE), F32)],
        compiler_params=_cparams(("parallel", "arbitrary")), name=name)(proj, xbc, xbc, xbc, dtc, dtw, alog, dsk, nw)


def ssd_bwd(proj, xbc, dtc, dtw, alog, dsk, nw, states, dy, dm, name):
    nc = dm.S // CHUNK
    sp = _ssd_specs(dm, lambda c: nc - 1 - c)
    gw = GROUP_HEADS * HEAD_DIM
    bc_spec = pl.BlockSpec((CHUNK, D_STATE), lambda g, c: (nc - 1 - c, g))

    def body(z_ref, xs_ref, b_ref, c_ref, dtc_ref, dtw_ref, alog_ref, dsk_ref, nw_ref, st_ref, dy_ref,
             dz_ref, dxs_ref, db_ref, dc_ref, ddtc_ref, ddtw_ref, dalog_ref, ddsk_ref, dnw_ref, dprev):
        first = pl.program_id(1) == 0

        @pl.when(first)
        def _():
            dprev[...] = jnp.zeros_like(dprev)

        args = _ssd_load(z_ref, xs_ref, b_ref, c_ref, dtc_ref, dtw_ref, alog_ref, dsk_ref, nw_ref)
        _, vjp = jax.vjp(ssd_chunk, *args, st_ref[...])
        gxs, gzs, gnw, gdtc, gdtw, galogs, gdsks, gb, gc, gprev = vjp((dy_ref[...], dprev[...]))
        dxs_ref[...] = gxs
        dz_ref[...] = gzs.astype(dz_ref.dtype)
        db_ref[...] = gb
        dc_ref[...] = gc
        dprev[...] = gprev
        for r in range(GROUP_HEADS):
            ddtc_ref[:, r:r + 1] = gdtc[r]
            ddtw_ref[r:r + 1, :] = gdtw[r]

        @pl.when(first)
        def _():
            dnw_ref[...] = gnw
            for r in range(GROUP_HEADS):
                dalog_ref[:, r:r + 1] = galogs[r]
                ddsk_ref[:, r:r + 1] = gdsks[r]

        @pl.when(jnp.logical_not(first))
        def _():
            dnw_ref[...] += gnw
            for r in range(GROUP_HEADS):
                dalog_ref[:, r:r + 1] += galogs[r]
                ddsk_ref[:, r:r + 1] += gdsks[r]

    xs_out = pl.BlockSpec((CHUNK, gw), lambda g, c: (nc - 1 - c, g))
    return pl.pallas_call(
        body, grid=(dm.G, nc),
        in_specs=[sp["z"], sp["xs"], sp["b"], sp["c"], sp["dtc"], sp["dtw"], sp["par"], sp["par"], sp["nw"],
                  sp["st"], xs_out],
        out_specs=[xs_out, xs_out, bc_spec, bc_spec, sp["dtc"], sp["dtw"], sp["par"], sp["par"], sp["nw"]],
        out_shape=[SDS((dm.S, dm.DI), BF16), SDS((dm.S, dm.DI), F32), SDS((dm.S, dm.G * D_STATE), F32),
                   SDS((dm.S, dm.G * D_STATE), F32), SDS((dm.G, dm.S, GROUP_HEADS), F32), SDS((dm.G, GROUP_HEADS, dm.S), F32),
                   SDS((dm.G, 1, GROUP_HEADS), F32), SDS((dm.G, 1, GROUP_HEADS), F32), SDS((1, dm.DI), F32)],
        scratch_shapes=[pltpu.VMEM((gw, D_STATE), F32)],
        compiler_params=_cparams(("parallel", "arbitrary")), name=name)(
            proj, xbc, xbc, xbc, dtc, dtw, alog, dsk, nw, states, dy)


def _split_bf16(v):
    hi = v.astype(BF16)
    return hi, (v - hi.astype(F32)).astype(BF16)


def _tri(v, mat):
    hi, lo = _split_bf16(v)
    return jnp.dot(hi, mat, preferred_element_type=F32) + jnp.dot(lo, mat, preferred_element_type=F32)


def _blocks(v):
    return [v[:, b * ATT_TILE:(b + 1) * ATT_TILE] for b in range(v.shape[1] // ATT_TILE)]


def _sb_group(z, mask, run, after_mat):
    sp = jnp.maximum(z, 0.0) + jnp.log(1.0 + jnp.exp(-jnp.abs(z)))
    lk = -sp if mask is None else jnp.where(mask, -sp, 0.0)
    cums = [_tri(v, after_mat) for v in _blocks(lk)]
    sums = [jnp.sum(v, axis=1, keepdims=True) for v in _blocks(lk)]
    later = [None] * len(cums)
    for b in reversed(range(len(cums))):
        later[b] = run + cums[b]
        run = run + sums[b]
    ls = z - sp
    w = jnp.exp(ls + jnp.concatenate(later, axis=1))
    if mask is not None:
        w = jnp.where(mask, w, 0.0)
    return ls, w, run


def _group_mask(i, g, t, gw):
    rows = i * t + lax.broadcasted_iota(jnp.int32, (t, gw), 0)
    cols = g * gw + lax.broadcasted_iota(jnp.int32, (t, gw), 1)
    return cols < rows


def _att_specs(dm, s):
    t = ATT_TILE
    qb, kb, vb = dm.off["q"] // LANES, dm.off["k"] // LANES, dm.off["v"] // LANES
    return (pl.BlockSpec((t, LANES), lambda p, i: (i, qb + p)),
            pl.BlockSpec((s, LANES), lambda p, i: (0, kb + p)),
            pl.BlockSpec((s, LANES), lambda p, i: (0, vb + p)))


def attn_fwd(proj, dm, name):
    s, t = dm.S, ATT_TILE
    scale = HEAD_DIM ** -0.5
    hsl = [slice(h * HEAD_DIM, (h + 1) * HEAD_DIM) for h in range(2)]

    gw = ATT_GROUP * t

    def body(q_ref, k_ref, v_ref, o_ref):
        i = pl.program_id(1)
        gd = i // ATT_GROUP
        r_io = lax.broadcasted_iota(jnp.int32, (t, t), 0)
        c_io = lax.broadcasted_iota(jnp.int32, (t, t), 1)
        after_mat = (r_io > c_io).astype(BF16)
        qs = [q_ref[:, sl].astype(BF16) for sl in hsl]

        def group(g, carry, mask):
            r0 = pl.multiple_of(g * gw, gw)
            zs = [_dot(qs[h], k_ref[pl.ds(r0, gw), hsl[h]], NT) * scale for h in range(2)]
            res = [_sb_group(zs[h], mask, carry[h][0], after_mat) for h in range(2)]
            return tuple((res[h][2], carry[h][1] + _dot(res[h][1], v_ref[pl.ds(r0, gw), hsl[h]], NN)) for h in range(2))

        zero = (jnp.zeros((t, 1), F32), jnp.zeros((t, HEAD_DIM), F32))
        carry = group(gd, (zero, zero), _group_mask(i, gd, t, gw))
        carry = lax.fori_loop(1, gd + 1, lambda jj, c: group(gd - jj, c, None), carry)
        for h in range(2):
            o_ref[:, hsl[h]] = carry[h][1]

    qs_, ks_, vs_ = _att_specs(dm, s)
    return pl.pallas_call(
        body, grid=(dm.SBW // LANES, s // t), in_specs=[qs_, ks_, vs_],
        out_specs=pl.BlockSpec((t, LANES), lambda p, i: (i, p)),
        out_shape=SDS((s, dm.SBW), F32), compiler_params=_cparams(("parallel", "arbitrary")), name=name)(proj, proj, proj)


def attn_bwd(proj, do, dm, name):
    s, t = dm.S, ATT_TILE
    nq = s // t
    gw = ATT_GROUP * t
    scale = HEAD_DIM ** -0.5
    hsl = [slice(h * HEAD_DIM, (h + 1) * HEAD_DIM) for h in range(2)]

    def body(q_ref, k_ref, v_ref, do_ref, dq_ref, dk_ref, dv_ref, dk_acc, dv_acc, g_scr, s_scr):
        i = pl.program_id(1)

        @pl.when(i == 0)
        def _():
            dk_acc[...] = jnp.zeros_like(dk_acc)
            dv_acc[...] = jnp.zeros_like(dv_acc)

        gd = i // ATT_GROUP
        r_io = lax.broadcasted_iota(jnp.int32, (t, t), 0)
        c_io = lax.broadcasted_iota(jnp.int32, (t, t), 1)
        after_mat = (r_io > c_io).astype(BF16)
        before_mat = (r_io < c_io).astype(BF16)
        qs = [q_ref[:, sl].astype(BF16) for sl in hsl]
        dos = [do_ref[:, sl].astype(BF16) for sl in hsl]
        q_t = q_ref[...].T.astype(BF16)
        do_t = do_ref[...].T.astype(BF16)

        def pass1(g, runs, mask):
            r0 = pl.multiple_of(g * gw, gw)
            zs = [_dot(qs[h], k_ref[pl.ds(r0, gw), hsl[h]], NT) * scale for h in range(2)]
            dws = [_dot(dos[h], v_ref[pl.ds(r0, gw), hsl[h]], NT) for h in range(2)]
            out = []
            for h in range(2):
                ls, w, run = _sb_group(zs[h], mask, runs[h], after_mat)
                g_scr[h, g] = dws[h] * w
                s_scr[h, g] = jnp.exp(ls)
                dv_acc[g, hsl[h], :] += _dot(do_t[hsl[h]], w, NN)
                out.append(run)
            return tuple(out)

        diag_mask = _group_mask(i, gd, t, gw)
        zero_col = jnp.zeros((t, 1), F32)
        runs = pass1(gd, (zero_col, zero_col), diag_mask)
        lax.fori_loop(1, gd + 1, lambda jj, r: pass1(gd - jj, r, None), runs)

        def pass2(g, carry, mask):
            r0 = pl.multiple_of(g * gw, gw)
            out = []
            for h in range(2):
                pre, dq = carry[h]
                gg = g_scr[h, g]
                sig = s_scr[h, g]
                before = []
                for v in _blocks(gg):
                    before.append(pre + _tri(v, before_mat))
                    pre = pre + jnp.sum(v, axis=1, keepdims=True)
                dz = gg * (1.0 - sig) - jnp.concatenate(before, axis=1) * sig
                if mask is not None:
                    dz = jnp.where(mask, dz, 0.0)
                dz = (dz * scale).astype(BF16)
                dk_acc[g, hsl[h], :] += _dot(q_t[hsl[h]], dz, NN)
                out.append((pre, dq + _dot(dz, k_ref[pl.ds(r0, gw), hsl[h]], NN)))
            return tuple(out)

        zero = (zero_col, jnp.zeros((t, HEAD_DIM), F32))
        carry = lax.fori_loop(0, gd, lambda g, c: pass2(g, c, None), (zero, zero))
        carry = pass2(gd, carry, diag_mask)
        for h in range(2):
            dq_ref[:, hsl[h]] = carry[h][1].astype(dq_ref.dtype)

        @pl.when(i == nq - 1)
        def _():
            for g in range(s // gw):
                dk_ref[g * gw:(g + 1) * gw, :] = dk_acc[g].T.astype(dk_ref.dtype)
                dv_ref[g * gw:(g + 1) * gw, :] = dv_acc[g].T.astype(dv_ref.dtype)

    qs_, ks_, vs_ = _att_specs(dm, s)
    tile_spec = pl.BlockSpec((t, LANES), lambda p, i: (i, p))
    full_spec = pl.BlockSpec((s, LANES), lambda p, i: (0, p))
    return pl.pallas_call(
        body, grid=(dm.SBW // LANES, nq), in_specs=[qs_, ks_, vs_, tile_spec],
        out_specs=[tile_spec, full_spec, full_spec],
        out_shape=[SDS((s, dm.SBW), BF16)] * 3,
        scratch_shapes=[pltpu.VMEM((s // gw, LANES, gw), F32), pltpu.VMEM((s // gw, LANES, gw), F32),
                        pltpu.VMEM((2, s // gw, t, gw), F32), pltpu.VMEM((2, s // gw, t, gw), F32)],
        compiler_params=_cparams(("parallel", "arbitrary")), name=name)(proj, proj, proj, do)


def adamw(w, g, m, v, name):
    rows, width = w.shape
    tile = _pick(rows, (256, 64, 16, 8, 4))
    c1 = 1.0 / (1.0 - ADAM_B1 ** ADAM_STEP)
    c2 = 1.0 / (1.0 - ADAM_B2 ** ADAM_STEP)

    def body(w_ref, g_ref, m_ref, v_ref, d_ref, nm_ref, nv_ref):
        gg = g_ref[...]
        nm = ADAM_B1 * m_ref[...] + (1.0 - ADAM_B1) * gg
        nv = ADAM_B2 * v_ref[...] + (1.0 - ADAM_B2) * (gg * gg)
        d_ref[...] = -ADAM_LR * ((nm * c1) / (jnp.sqrt(nv * c2) + ADAM_EPS) + ADAM_WD * w_ref[...])
        nm_ref[...] = nm
        nv_ref[...] = nv

    spec = pl.BlockSpec((tile, width), lambda i: (i, 0))
    return pl.pallas_call(
        body, grid=(rows // tile,), in_specs=[spec] * 4, out_specs=[spec] * 3,
        out_shape=[SDS((rows, width), F32)] * 3, compiler_params=_cparams(("parallel",)), name=name)(w, g, m, v)


def _me():
    return lax.axis_index("x"), lax.axis_index("y"), lax.axis_index("c")


def _other_chips(x, y):
    return [(1 - x, y), (x, 1 - y), (1 - x, 1 - y)]


def gather_weights(wp):
    rows, width = wp.shape
    half = rows // 2

    def body(w_ref, out_ref, send_sems, recv_sems, local_sem):
        x, y, c = _me()
        sibling = (x, y, 1 - c)
        chips = _other_chips(x, y)

        def part(cx, cy, hf):
            return out_ref.at[2 * cx + cy, pl.ds(hf * half, half), :]

        def copy(k, src, dst, to):
            return pltpu.make_async_remote_copy(src_ref=src, dst_ref=dst, send_sem=send_sems.at[k], recv_sem=recv_sems.at[k],
                                                device_id=to, device_id_type=MESH_ID)

        mine = pltpu.make_async_copy(w_ref, out_ref.at[2 * x + y], local_sem)
        mine.start()
        first = [copy(j, w_ref.at[pl.ds(c * half, half), :], part(x, y, c), (cx, cy, c)) for j, (cx, cy) in enumerate(chips)]
        for cp in first:
            cp.start()
        passed = [copy(3 + j, part(cx, cy, c), part(cx, cy, c), sibling) for j, (cx, cy) in enumerate(chips)]
        for j, (cx, cy) in enumerate(chips):
            copy(j, part(cx, cy, c), part(cx, cy, c), (x, y, c)).wait_recv()
            passed[j].start()
        for j, (cx, cy) in enumerate(chips):
            copy(3 + j, part(cx, cy, 1 - c), part(cx, cy, 1 - c), (x, y, c)).wait_recv()
        for cp in first + passed:
            cp.wait_send()
        mine.wait()

    return pl.pallas_call(
        body, out_shape=SDS((N_CHIPS, rows, width), wp.dtype), in_specs=[HBM_SPEC], out_specs=HBM_SPEC,
        scratch_shapes=[pltpu.SemaphoreType.DMA((6,)), pltpu.SemaphoreType.DMA((6,)), pltpu.SemaphoreType.DMA],
        name="gather_weights")(wp)


def swap_halves(g):
    n, rows, width = g.shape
    half = rows // 2

    def body(g_ref, a_ref, send_sem, recv_sem):
        x, y, c = _me()
        cp = pltpu.make_async_remote_copy(src_ref=g_ref.at[:, pl.ds((1 - c) * half, half), :], dst_ref=a_ref,
                                          send_sem=send_sem, recv_sem=recv_sem, device_id=(x, y, 1 - c), device_id_type=MESH_ID)
        cp.start()
        cp.wait()

    return pl.pallas_call(
        body, out_shape=SDS((n, half, width), g.dtype), in_specs=[HBM_SPEC], out_specs=HBM_SPEC,
        scratch_shapes=[pltpu.SemaphoreType.DMA, pltpu.SemaphoreType.DMA], name="swap_halves")(g)


def add_half(g, a, c_idx, name):
    n, rows, width = g.shape
    half = rows // 2
    tile = _pick(half, (240, 208, 16, 8))
    nt = half // tile

    def body(c_ref, g_ref, a_ref, o_ref):
        o_ref[...] = g_ref[...] + a_ref[...]

    gs = pltpu.PrefetchScalarGridSpec(
        num_scalar_prefetch=1, grid=(n, nt),
        in_specs=[pl.BlockSpec((None, tile, width), lambda s, i, c_ref: (s, c_ref[0] * nt + i, 0)),
                  pl.BlockSpec((None, tile, width), lambda s, i, c_ref: (s, i, 0))],
        out_specs=pl.BlockSpec((None, tile, width), lambda s, i, c_ref: (s, i, 0)))
    return pl.pallas_call(body, grid_spec=gs, out_shape=SDS((n, half, width), g.dtype),
                          compiler_params=_cparams(("parallel", "parallel")), name=name)(c_idx, g, a)


def exchange_chips(sh, small):
    n, hf, width = sh.shape

    def body(s_ref, sm_ref, b_ref, all_ref, send_sems, recv_sems, local_sem):
        x, y, c = _me()
        chips = _other_chips(x, y)
        mine = pltpu.make_async_copy(sm_ref, all_ref.at[0], local_sem)
        mine.start()
        copies = []
        for j, (cx, cy) in enumerate(chips):
            copies.append(pltpu.make_async_remote_copy(
                src_ref=s_ref.at[2 * cx + cy], dst_ref=b_ref.at[j], send_sem=send_sems.at[j], recv_sem=recv_sems.at[j],
                device_id=(cx, cy, c), device_id_type=MESH_ID))
        for m in range(1, N_DEV):
            peer = (x ^ ((m >> 2) & 1), y ^ ((m >> 1) & 1), c ^ (m & 1))
            copies.append(pltpu.make_async_remote_copy(
                src_ref=sm_ref, dst_ref=all_ref.at[m], send_sem=send_sems.at[2 + m], recv_sem=recv_sems.at[2 + m],
                device_id=peer, device_id_type=MESH_ID))
        for cp in copies:
            cp.start()
        for cp in copies:
            cp.wait()
        mine.wait()

    return pl.pallas_call(
        body, out_shape=[SDS((3, hf, width), sh.dtype), SDS((N_DEV, SMALL_ROWS, width), small.dtype)],
        in_specs=[HBM_SPEC, HBM_SPEC], out_specs=[HBM_SPEC, HBM_SPEC],
        scratch_shapes=[pltpu.SemaphoreType.DMA((10,)), pltpu.SemaphoreType.DMA((10,)), pltpu.SemaphoreType.DMA],
        name="exchange_chips")(sh, small)


def add_chips(sh, b, k_idx, name):
    n, hf, width = sh.shape
    tile = _pick(hf, (240, 208, 16, 8))

    def body(k_ref, s_ref, b0, b1, b2, o_ref):
        o_ref[...] = ((s_ref[...] + b0[...]) + b1[...]) + b2[...]

    def bspec(j):
        return pl.BlockSpec((None, tile, width), lambda i, k_ref, j=j: (j, i, 0))

    gs = pltpu.PrefetchScalarGridSpec(
        num_scalar_prefetch=1, grid=(hf // tile,),
        in_specs=[pl.BlockSpec((None, tile, width), lambda i, k_ref: (k_ref[0], i, 0)), bspec(0), bspec(1), bspec(2)],
        out_specs=pl.BlockSpec((tile, width), lambda i, k_ref: (i, 0)))
    return pl.pallas_call(body, grid_spec=gs, out_shape=SDS((hf, width), sh.dtype),
                          compiler_params=_cparams(("parallel",)), name=name)(k_idx, sh, b, b, b)


def sum_small(allsm, me_idx, name):
    _, rows, width = allsm.shape

    def body(me_ref, a_ref, o_ref):
        me = me_ref[0]
        acc = a_ref[me]
        for dev in range(1, N_DEV):
            acc = acc + a_ref[jnp.bitwise_xor(me, dev)]
        o_ref[...] = acc

    gs = pltpu.PrefetchScalarGridSpec(
        num_scalar_prefetch=1, grid=(1,),
        in_specs=[pl.BlockSpec((N_DEV, rows, width), lambda i, me_ref: (0, 0, 0))],
        out_specs=pl.BlockSpec((rows, width), lambda i, me_ref: (0, 0)))
    return pl.pallas_call(body, grid_spec=gs, out_shape=SDS((rows, width), allsm.dtype),
                          compiler_params=_cparams(("arbitrary",)), name=name)(me_idx, allsm)


def join_halves(t):
    hf, width = t.shape

    def body(t_ref, o_ref, send_sem, recv_sem, local_sem):
        x, y, c = _me()
        mine = pltpu.make_async_copy(t_ref, o_ref.at[pl.ds(c * hf, hf), :], local_sem)
        mine.start()
        cp = pltpu.make_async_remote_copy(src_ref=t_ref, dst_ref=o_ref.at[pl.ds(c * hf, hf), :], send_sem=send_sem,
                                          recv_sem=recv_sem, device_id=(x, y, 1 - c), device_id_type=MESH_ID)
        cp.start()
        cp.wait_send()
        pltpu.make_async_remote_copy(src_ref=t_ref, dst_ref=o_ref.at[pl.ds((1 - c) * hf, hf), :], send_sem=send_sem,
                                     recv_sem=recv_sem, device_id=(x, y, 1 - c), device_id_type=MESH_ID).wait_recv()
        mine.wait()

    return pl.pallas_call(
        body, out_shape=SDS((2 * hf, width), t.dtype), in_specs=[HBM_SPEC], out_specs=HBM_SPEC,
        scratch_shapes=[pltpu.SemaphoreType.DMA, pltpu.SemaphoreType.DMA, pltpu.SemaphoreType.DMA], name="join_halves")(t)


def _pack_rows(shard_shapes, width):
    rows = sum((a * b) // width for a, b in shard_shapes.values())
    return -(-rows // 32) * 32


def pack_local(shards, width, total_rows, dtype):
    parts = [shards[n].reshape(-1, width).astype(dtype) for n in BIG_WEIGHTS]
    used = sum(p.shape[0] for p in parts)
    parts.append(jnp.zeros((total_rows - used, width), dtype))
    return jnp.concatenate(parts, axis=0)


def unpack_local(packed, shard_shapes, width):
    out, r0 = {}, 0
    for n in BIG_WEIGHTS:
        a, b = shard_shapes[n]
        nr = (a * b) // width
        out[n] = packed[r0:r0 + nr].reshape(a, b)
        r0 += nr
    return out


EXACT_IN_GATHER = ("conv_w",)
EXACT_TERMS = 3


def pack_gather(shards, width):
    parts = []
    for n in BIG_WEIGHTS:
        if n in EXACT_IN_GATHER:
            rest = shards[n].astype(F32)
            for _ in range(EXACT_TERMS):
                term = rest.astype(BF16)
                parts.append(term.reshape(-1, width))
                rest = rest - term.astype(F32)
        else:
            parts.append(shards[n].reshape(-1, width).astype(BF16))
    used = sum(p.shape[0] for p in parts)
    parts.append(jnp.zeros((-(-used // 32) * 32 - used, width), BF16))
    return jnp.concatenate(parts, axis=0)


def unpack_full(gathered, shard_shapes, width):
    out, r0 = {}, 0
    for n in BIG_WEIGHTS:
        a, b = shard_shapes[n]
        terms = EXACT_TERMS if n in EXACT_IN_GATHER else 1
        nr = (a * b) // width
        pieces = []
        for j in range(N_CHIPS):
            blk = gathered[j, r0:r0 + nr].reshape(a, b)
            for t in range(1, terms):
                blk = blk.astype(F32) + gathered[j, r0 + t * nr:r0 + (t + 1) * nr].reshape(a, b).astype(F32)
            pieces.append(blk)
        out[n] = jnp.concatenate(pieces, axis=SHARD_AXIS[n])
        r0 += terms * nr
    return out


def pack_full(grads, shard_shapes, width, total_rows):
    slabs = []
    for j in range(N_CHIPS):
        parts = []
        for n in BIG_WEIGHTS:
            a, b = shard_shapes[n]
            ax = SHARD_AXIS[n]
            sz = (a, b)[ax]
            piece = lax.slice_in_dim(grads[n], j * sz, (j + 1) * sz, axis=ax)
            parts.append(piece.reshape(-1, width))
        used = sum(p.shape[0] for p in parts)
        parts.append(jnp.zeros((total_rows - used, width), F32))
        slabs.append(jnp.concatenate(parts, axis=0))
    return jnp.stack(slabs, axis=0)


def _small_layout(sizes, width):
    lay, r = {}, 0
    for n in SMALL_WEIGHTS:
        nr = -(-sizes[n] // width)
        lay[n] = (r, nr, sizes[n])
        r += nr
    assert r <= SMALL_ROWS
    return lay


def pack_small(vals, lay, width):
    rows = []
    for n in SMALL_WEIGHTS:
        r, nr, sz = lay[n]
        v = vals[n].reshape(-1).astype(F32)
        rows.append(jnp.pad(v, (0, nr * width - sz)).reshape(nr, width))
    used = sum(r.shape[0] for r in rows)
    rows.append(jnp.zeros((SMALL_ROWS - used, width), F32))
    return jnp.concatenate(rows, axis=0)


def unpack_small(packed, lay):
    return {n: packed[r:r + nr].reshape(-1)[:sz].reshape(1, sz) for n, (r, nr, sz) in lay.items()}


def local_step(x, p, tgt, wf, sm, dm):
    s, d = dm.S, dm.D
    off = dm.off
    w_in = wf["w_in"]
    c0 = dm.DI + dm.CD
    w_all = jnp.concatenate(
        [w_in[:, :c0], w_in[:, c0 + dm.H:], wf["w_gate"], w_in[:, c0:c0 + dm.H],
         jnp.zeros((d, DT_PAD - dm.H), w_in.dtype)], axis=1).astype(BF16)
    g = dm.G
    per_group = lambda v: v.reshape(g, 1, GROUP_HEADS)
    alog, dsk = per_group(sm["a_log"]), per_group(sm["d_skip"])
    b_gate = sm["b_gate"]
    b_ssd, b_sb = b_gate[:, :d], b_gate[:, d:]
    gcol = off["gate"] // d

    (n1,) = row_fwd("norm1", f_norm1, [x], [sm["norm_mix_pre"]], [(d, BF16)])
    proj = matmul(n1, w_all, name="in_proj")
    xbc = conv_fwd(proj, off["xbc"], dm.CD, wf["conv_w"].astype(F32), sm["conv_b"], "conv_fwd")
    dt_raw = proj[:, off["dt"]:off["dt"] + dm.H]
    (dt,) = row_fwd("dt", f_dt, [dt_raw], [sm["dt_bias"]], [(dm.H, F32)])
    dtc = dt.reshape(s, g, GROUP_HEADS).transpose(1, 0, 2)
    dtw = dt.reshape(s, g, GROUP_HEADS).transpose(1, 2, 0)
    y_ssd, states = ssd_fwd(proj, xbc, dtc, dtw, alog, dsk, sm["ssd_norm"], dm, "ssd_fwd")
    y_sb = attn_fwd(proj, dm, "attn_fwd")
    yb_ssd = matmul(y_ssd, wf["w_ssd_branch"], name="ssd_branch")
    yb_sb = matmul(y_sb, wf["w_sb_branch"], name="sb_branch")
    merge_rows = [(proj, d, gcol), (proj, d, gcol + 1), yb_ssd, yb_sb]
    (merged,) = row_fwd("merge", f_merge, merge_rows, [b_ssd, b_sb], [(d, BF16)])
    mo = matmul(merged, wf["w_out"], name="w_out")
    h1, n2 = row_fwd("mix_out", f_mix_out, [x, mo], [sm["norm_mix_post"], sm["norm_ffn_pre"]], [(d, F32), (d, BF16)])
    a1 = matmul(n2, wf["w_ff1"], name="ff1")
    (act,) = row_fwd("relu2", f_relu2, [a1], [], [(dm.DFF, BF16)])
    ff = matmul(act, wf["w_ff2"], name="ff2")
    (h2,) = row_fwd("ffn_out", f_ffn_out, [h1, ff], [sm["norm_ffn_post"]], [(d, F32)])
    pg = matmul(h2, wf["w_ple_gate"], name="ple_gate")
    pe = matmul(p, wf["w_ple"], name="ple_emb")

    gr = {}
    (dh2_a, dpg, dpe), (gr["norm_ple_post"], loss_cols) = row_bwd(
        "ple_loss", f_ple_loss, [h2, pg, pe, tgt], [sm["norm_ple_post"]], [None], [F32, BF16, BF16, None], primal_sum=True)
    loss = jnp.sum(loss_cols)
    gr["w_ple"] = matmul(p, dpe, ta=True, name="d_w_ple")
    gr["w_ple_gate"] = matmul(h2, dpg, ta=True, name="d_w_ple_gate")
    dh2_b = matmul(dpg, wf["w_ple_gate"], tb=True, name="d_h2")
    (dh1_a, dff), (gr["norm_ffn_post"],) = row_bwd(
        "ffn_out_bwd", f_ffn_out, [h1, ff], [sm["norm_ffn_post"]], [[dh2_a, dh2_b]], [F32, BF16])
    gr["w_ff2"] = matmul(act, dff, ta=True, name="d_w_ff2")
    dact = matmul(dff, wf["w_ff2"], tb=True, name="d_act")
    (da1,), _ = row_bwd("relu2_bwd", f_relu2, [a1], [], [[dact]], [BF16])
    gr["w_ff1"] = matmul(n2, da1, ta=True, name="d_w_ff1")
    dn2 = matmul(da1, wf["w_ff1"], tb=True, name="d_n2")
    (dx_a, dmo), (gr["norm_mix_post"], gr["norm_ffn_pre"]) = row_bwd(
        "mix_out_bwd", f_mix_out, [x, mo], [sm["norm_mix_post"], sm["norm_ffn_pre"]], [[dh1_a], [dn2]], [F32, BF16])
    gr["w_out"] = matmul(merged, dmo, ta=True, name="d_w_out")
    dmerged = matmul(dmo, wf["w_out"], tb=True, name="d_merged")
    (dgp_ssd, dgp_sb, dyb_ssd, dyb_sb), (db_ssd, db_sb) = row_bwd(
        "merge_bwd", f_merge, merge_rows, [b_ssd, b_sb], [[dmerged]], [BF16, BF16, BF16, BF16])
    gr["b_gate"] = jnp.concatenate([db_ssd, db_sb], axis=1)
    gr["w_ssd_branch"] = matmul(y_ssd, dyb_ssd, ta=True, name="d_w_ssd_branch")
    gr["w_sb_branch"] = matmul(y_sb, dyb_sb, ta=True, name="d_w_sb_branch")
    dy_ssd = matmul(dyb_ssd, wf["w_ssd_branch"], tb=True, name="d_y_ssd")
    dy_sb = matmul(dyb_sb, wf["w_sb_branch"], tb=True, name="d_y_sb")
    dq, dk, dv = attn_bwd(proj, dy_sb, dm, "attn_bwd")
    dz, dxs, dbm, dcm, ddtc, ddtw, dalog, ddsk, gr["ssd_norm"] = ssd_bwd(
        proj, xbc, dtc, dtw, alog, dsk, sm["ssd_norm"], states, dy_ssd, dm, "ssd_bwd")
    gr["a_log"], gr["d_skip"] = (v.reshape(1, dm.H) for v in (dalog, ddsk))
    ddt_post = (ddtc.transpose(1, 0, 2) + ddtw.transpose(2, 0, 1)).reshape(s, dm.H)
    (ddt,), (gr["dt_bias"],) = row_bwd("dt_bwd", f_dt, [dt_raw], [sm["dt_bias"]], [[ddt_post]], [BF16])
    conv_w32 = wf["conv_w"].astype(F32)
    du_x, dw_x, dcb_x = conv_bwd(proj, off["xbc"], dxs, 0, conv_w32, sm["conv_b"], "conv_bwd_x")
    du_b, dw_b, dcb_b = conv_bwd(proj, off["xbc"], dbm, dm.DI, conv_w32, sm["conv_b"], "conv_bwd_b")
    du_c, dw_c, dcb_c = conv_bwd(proj, off["xbc"], dcm, dm.DI + g * D_STATE, conv_w32, sm["conv_b"], "conv_bwd_c")
    gr["conv_w"] = jnp.concatenate([dw_x, dw_b, dw_c], axis=1)
    gr["conv_b"] = jnp.concatenate([dcb_x, dcb_b, dcb_c], axis=1)
    dproj = jnp.concatenate(
        [dz, du_x, du_b, du_c, dq, dk, dv, dgp_ssd, dgp_sb, ddt, jnp.zeros((s, DT_PAD - dm.H), BF16)], axis=1)
    dw_all = matmul(n1, dproj, ta=True, name="d_w_all")
    gr["w_in"] = jnp.concatenate(
        [dw_all[:, :c0], dw_all[:, off["dt"]:off["dt"] + dm.H], dw_all[:, c0:c0 + 3 * dm.SBW]], axis=1)
    gr["w_gate"] = dw_all[:, off["gate"]:off["gate"] + 2 * d]
    dn1 = matmul(dproj, w_all, tb=True, name="d_n1")
    (dx_b,), (gr["norm_mix_pre"],) = row_bwd("norm1_bwd", f_norm1, [x], [sm["norm_mix_pre"]], [[dn1]], [F32])
    (grad_x,) = row_fwd("grad_x", lambda u, v: (u + v,), [dx_a, dx_b], [], [(d, F32)])
    return loss, grad_x, gr


def kernel(x, p, norm_mix_pre, w_in, conv_w, conv_b, dt_bias, a_log, d_skip, ssd_norm, w_ssd_branch, w_sb_branch, w_gate, b_gate, w_out, norm_mix_post, norm_ffn_pre, w_ff1, w_ff2, norm_ffn_post, w_ple, w_ple_gate, norm_ple_post, loss_target, m_norm_mix_pre, m_w_in, m_conv_w, m_conv_b, m_dt_bias, m_a_log, m_d_skip, m_ssd_norm, m_w_ssd_branch, m_w_sb_branch, m_w_gate, m_b_gate, m_w_out, m_norm_mix_post, m_norm_ffn_pre, m_w_ff1, m_w_ff2, m_norm_ffn_post, m_w_ple, m_w_ple_gate, m_norm_ple_post, v_norm_mix_pre, v_w_in, v_conv_w, v_conv_b, v_dt_bias, v_a_log, v_d_skip, v_ssd_norm, v_w_ssd_branch, v_w_sb_branch, v_w_gate, v_b_gate, v_w_out, v_norm_mix_post, v_norm_ffn_pre, v_w_ff1, v_w_ff2, v_norm_ffn_post, v_w_ple, v_w_ple_gate, v_norm_ple_post):
    loc = dict(locals())
    unbatch = lambda a: a[0] if a.ndim == 3 else a
    w = {n: unbatch(loc[n]) for n in ALL_WEIGHTS}
    m = {n: unbatch(loc["m_" + n]) for n in ALL_WEIGHTS}
    v = {n: unbatch(loc["v_" + n]) for n in ALL_WEIGHTS}
    xs, ps, tgt = x[0], p[0, 0], loss_target[0]
    s, d = xs.shape
    di = w["w_ssd_branch"].shape[0] * N_CHIPS
    cd = w["conv_b"].shape[1]
    dm = Dims(S=s, D=d, DI=di, H=w["dt_bias"].shape[1], G=(cd - di) // (2 * D_STATE), CD=cd,
              SBW=w["w_sb_branch"].shape[0] * N_CHIPS, DFF=w["w_ff2"].shape[0] * N_CHIPS, PLE=ps.shape[1])
    ix, iy, ic = lax.axis_index("x"), lax.axis_index("y"), lax.axis_index("c")
    chip_idx = jnp.reshape(2 * ix + iy, (1,)).astype(jnp.int32)
    core_idx = jnp.reshape(ic, (1,)).astype(jnp.int32)
    dev_idx = jnp.reshape(4 * ix + 2 * iy + ic, (1,)).astype(jnp.int32)

    shard_shapes = {n: w[n].shape for n in BIG_WEIGHTS}
    total_rows = _pack_rows(shard_shapes, d)
    gathered = gather_weights(pack_gather(w, d))
    wf = unpack_full(gathered, shard_shapes, d)
    sm = {n: w[n] for n in SMALL_WEIGHTS}

    loss_part, grad_x, gr = local_step(xs, ps, tgt, wf, sm, dm)
    loss = lax.psum(loss_part, ("x", "y", "c"))

    lay = _small_layout({n: w[n].shape[1] for n in SMALL_WEIGHTS}, d)
    gbig = pack_full(gr, shard_shapes, d, total_rows)
    gsmall = pack_small(gr, lay, d)
    from_sibling = swap_halves(gbig)
    pair_sum = add_half(gbig, from_sibling, core_idx, "add_half")
    from_chips, all_small = exchange_chips(pair_sum, gsmall)
    my_half = add_chips(pair_sum, from_chips, chip_idx, "add_chips")
    g_red = join_halves(my_half)
    gs_red = sum_small(all_small, dev_idx, "sum_small")

    grads = unpack_local(g_red, shard_shapes, d)
    grads.update(unpack_small(gs_red, lay))
    delta, new_m, new_v = {}, {}, {}
    for n in BIG_WEIGHTS:
        delta[n], new_m[n], new_v[n] = adamw(w[n], grads[n], m[n], v[n], "adamw_" + n)
    d_sm, nm_sm, nv_sm = adamw(pack_small(w, lay, d), gs_red, pack_small(m, lay, d), pack_small(v, lay, d), "adamw_small")
    for out, packed in ((delta, d_sm), (new_m, nm_sm), (new_v, nv_sm)):
        out.update(unpack_small(packed, lay))

    def leaves(vals):
        return [vals[n][None] if n in BIG_WEIGHTS else vals[n] for n in ALL_WEIGHTS]

    return (loss, grad_x[None], *leaves(grads), *leaves(delta), *leaves(new_m), *leaves(new_v))
```

```python
import functools
from typing import NamedTuple

import jax
import jax.numpy as jnp
from jax import lax
from jax.experimental import pallas as pl
from jax.experimental.pallas import tpu as pltpu

F32 = jnp.float32
BF16 = jnp.bfloat16
SDS = jax.ShapeDtypeStruct

HEAD_DIM = 64
GROUP_HEADS = 4
D_STATE = 128
CHUNK = 128
ATT_TILE = 128
ATT_GROUP = 4
CONV_K = 4
CONV_COLS = 128
RMS_EPS = 1e-6
LANES = 128
DT_PAD = 512
N_CHIPS = 4
N_DEV = 8
SMALL_ROWS = 16
VMEM_LIMIT = 48 * 1024 * 1024

ADAM_LR = 0.001
ADAM_B1 = 0.9
ADAM_B2 = 0.999
ADAM_EPS = 1e-08
ADAM_WD = 0.01
ADAM_STEP = 10

MESH_ID = pl.DeviceIdType.MESH
HBM_SPEC = pl.BlockSpec(memory_space=pltpu.HBM)

BIG_WEIGHTS = ("w_in", "conv_w", "w_ssd_branch", "w_sb_branch", "w_gate", "w_out", "w_ff1", "w_ff2", "w_ple", "w_ple_gate")
SHARD_AXIS = {"w_in": 1, "conv_w": 1, "w_ssd_branch": 0, "w_sb_branch": 0, "w_gate": 1, "w_out": 0, "w_ff1": 1,
              "w_ff2": 0, "w_ple": 1, "w_ple_gate": 0}
SMALL_WEIGHTS = ("norm_mix_pre", "conv_b", "dt_bias", "a_log", "d_skip", "ssd_norm", "b_gate", "norm_mix_post",
                 "norm_ffn_pre", "norm_ffn_post", "norm_ple_post")
ALL_WEIGHTS = ("norm_mix_pre", "w_in", "conv_w", "conv_b", "dt_bias", "a_log", "d_skip", "ssd_norm", "w_ssd_branch",
               "w_sb_branch", "w_gate", "b_gate", "w_out", "norm_mix_post", "norm_ffn_pre", "w_ff1", "w_ff2",
               "norm_ffn_post", "w_ple", "w_ple_gate", "norm_ple_post")


class Dims(NamedTuple):
    S: int
    D: int
    DI: int
    H: int
    G: int
    CD: int
    SBW: int
    DFF: int
    PLE: int

    @property
    def NA(self):
        return self.DI + self.CD + 3 * self.SBW + 2 * self.D + DT_PAD

    @property
    def off(self):
        o = {}
        o["z"] = 0
        o["xbc"] = self.DI
        o["q"] = self.DI + self.CD
        o["k"] = o["q"] + self.SBW
        o["v"] = o["k"] + self.SBW
        o["gate"] = o["v"] + self.SBW
        o["dt"] = o["gate"] + 2 * self.D
        return o


def _cparams(sem):
    return pltpu.CompilerParams(dimension_semantics=sem, vmem_limit_bytes=VMEM_LIMIT)


def _pick(n, cands):
    for c in cands:
        if n % c == 0:
            return c
    raise ValueError(f"no tile for {n}")


def _grid_call(body, *, grid, in_specs, out_specs, out_shape, scratch, operands, name, rider=None):
    if rider is None:
        sem = ("parallel",) + ("arbitrary",) * (len(grid) - 1)
        return pl.pallas_call(body, grid=grid, in_specs=in_specs, out_specs=out_specs, out_shape=out_shape,
                              scratch_shapes=scratch, compiler_params=_cparams(sem), name=name)(*operands)
    return pl.pallas_call(
        _with_rider(body, rider, grid, len(in_specs), len(out_specs)), grid=grid,
        in_specs=list(in_specs) + [HBM_SPEC] * len(rider.operands),
        out_specs=list(out_specs) + [HBM_SPEC] * len(rider.out_shape),
        out_shape=list(out_shape) + list(rider.out_shape), scratch_shapes=list(scratch) + list(rider.scratch),
        compiler_params=_cparams(("arbitrary",) * len(grid)), name=name)(*operands, *rider.operands)


def matmul(a, b, *, ta=False, tb=False, out_dtype=F32, name, rider=None):
    m, k = (a.shape[1], a.shape[0]) if ta else a.shape
    n, kb = b.shape if tb else (b.shape[1], b.shape[0])
    assert k == kb, (a.shape, b.shape, ta, tb)
    tm = _pick(m, (1024, 512, 256, 128))
    tn = _pick(n, (512, 256, 128))
    tk = _pick(k, (1024, 512, 256, 128))
    nk = k // tk
    dims = (((0 if ta else 1,), (1 if tb else 0,)), ((), ()))

    def body(a_ref, b_ref, o_ref, acc_ref):
        part = lax.dot_general(a_ref[...].astype(BF16), b_ref[...].astype(BF16), dims, preferred_element_type=F32)
        if nk == 1:
            o_ref[...] = part.astype(o_ref.dtype)
        else:
            kk = pl.program_id(2)

            @pl.when(kk == 0)
            def _():
                acc_ref[...] = part

            @pl.when(kk > 0)
            def _():
                acc_ref[...] += part

            @pl.when(kk == nk - 1)
            def _():
                o_ref[...] = acc_ref[...].astype(o_ref.dtype)

    a_spec = pl.BlockSpec((tk, tm), lambda i, j, kk: (kk, i)) if ta else pl.BlockSpec((tm, tk), lambda i, j, kk: (i, kk))
    b_spec = pl.BlockSpec((tn, tk), lambda i, j, kk: (j, kk)) if tb else pl.BlockSpec((tk, tn), lambda i, j, kk: (kk, j))
    res = _grid_call(body, grid=(m // tm, n // tn, nk), in_specs=[a_spec, b_spec],
                     out_specs=[pl.BlockSpec((tm, tn), lambda i, j, kk: (i, j))], out_shape=[SDS((m, n), out_dtype)],
                     scratch=[pltpu.VMEM((tm, tn), F32)], operands=(a, b), name=name, rider=rider)
    return res[0] if rider is None else res


def _row_spec(entry, tile):
    arr, width, cb = entry if isinstance(entry, tuple) else (entry, entry.shape[1], 0)
    return arr, pl.BlockSpec((tile, width), lambda i, cb=cb: (i, cb))


def _par_spec(p):
    return pl.BlockSpec(p.shape, lambda i: (0, 0))


def row_fwd(name, fn, rows, params, outs, tile=256):
    arrs, specs = zip(*[_row_spec(e, tile) for e in rows])
    s = arrs[0].shape[0]
    nr, npar = len(rows), len(params)

    def body(*refs):
        r = [x[...].astype(F32) for x in refs[:nr]]
        p = [x[...] for x in refs[nr:nr + npar]]
        res = fn(*r, *p)
        for o_ref, val in zip(refs[nr + npar:], res):
            o_ref[...] = val.astype(o_ref.dtype)

    return pl.pallas_call(
        body, grid=(s // tile,), in_specs=list(specs) + [_par_spec(p) for p in params],
        out_specs=[pl.BlockSpec((tile, w), lambda i: (i, 0)) for w, _ in outs],
        out_shape=[SDS((s, w), dt) for w, dt in outs],
        compiler_params=_cparams(("parallel",)), name=name)(*arrs, *params)


def row_bwd(name, fn, rows, params, cots, row_grads, tile=256, primal_sum=False):
    arrs, specs = zip(*[_row_spec(e, tile) for e in rows])
    s = arrs[0].shape[0]
    nr, npar = len(rows), len(params)
    cot_entries = [e for c in cots if c is not None for e in c]
    carrs, cspecs = zip(*[_row_spec(e, tile) for e in cot_entries]) if cot_entries else ((), ())
    nc = len(cot_entries)
    want = [i for i, d in enumerate(row_grads) if d is not None]

    def body(*refs):
        r = [x[...].astype(F32) for x in refs[:nr]]
        p = [x[...] for x in refs[nr:nr + npar]]
        cvals = [x[...].astype(F32) for x in refs[nr + npar:nr + npar + nc]]
        outs = refs[nr + npar + nc:]
        prim, vjp = jax.vjp(fn, *r, *p)
        ct, pos = [], 0
        for c, pr in zip(cots, prim):
            if c is None:
                ct.append(jnp.ones_like(pr))
            else:
                acc = cvals[pos]
                for extra in cvals[pos + 1:pos + len(c)]:
                    acc = acc + extra
                pos += len(c)
                ct.append(acc)
        grads = vjp(tuple(ct))
        for o_ref, i in zip(outs[:len(want)], want):
            o_ref[...] = grads[i].astype(o_ref.dtype)
        acc_refs = outs[len(want):]
        vals = [grads[nr + j] for j in range(npar)]
        if primal_sum:
            vals.append(jnp.sum(prim[0], axis=0, keepdims=True))
        first = pl.program_id(0) == 0

        @pl.when(first)
        def _():
            for a_ref, v in zip(acc_refs, vals):
                a_ref[...] = v

        @pl.when(jnp.logical_not(first))
        def _():
            for a_ref, v in zip(acc_refs, vals):
                a_ref[...] += v

    widths = [(e[1] if isinstance(e, tuple) else e.shape[1]) for e in rows]
    out_specs = [pl.BlockSpec((tile, widths[i]), lambda i_: (i_, 0)) for i in want]
    out_shape = [SDS((s, widths[i]), row_grads[i]) for i in want]
    pshapes = [p.shape for p in params]
    if primal_sum:
        pshapes.append((1, widths[0]))
    out_specs += [pl.BlockSpec(sh, lambda i_: (0, 0)) for sh in pshapes]
    out_shape += [SDS(sh, F32) for sh in pshapes]
    res = pl.pallas_call(
        body, grid=(s // tile,), in_specs=list(specs) + [_par_spec(p) for p in params] + list(cspecs),
        out_specs=out_specs, out_shape=out_shape,
        compiler_params=_cparams(("arbitrary",)), name=name)(*arrs, *params, *carrs)
    return res[:len(want)], res[len(want):]


def _rms(x, w):
    return x * lax.rsqrt(jnp.mean(x * x, axis=-1, keepdims=True) + RMS_EPS) * w


def _sigmoid(x):
    return jax.nn.sigmoid(x)


def _softplus(x):
    return jnp.maximum(x, 0.0) + jnp.log1p(jnp.exp(-jnp.abs(x)))


def f_norm1(x, w):
    return (_rms(x, w),)


def f_dt(raw, bias):
    return (_softplus(raw + bias),)


def f_merge(gp_ssd, gp_sb, yb_ssd, yb_sb, b_ssd, b_sb):
    return (_sigmoid(gp_ssd + b_ssd) * yb_ssd + _sigmoid(gp_sb + b_sb) * yb_sb,)


def f_mix_out(x, mo, w_post, w_pre):
    h1 = x + _rms(mo, w_post)
    return h1, _rms(h1, w_pre)


def f_relu2(a1):
    return (jnp.square(jnp.maximum(a1, 0.0)),)


def f_ffn_out(h1, ff, w):
    return (h1 + _rms(ff, w),)


def f_ple_loss(h2, pg, pe, tgt, w):
    h3 = h2 + _rms(_sigmoid(pg) * pe, w)
    return (0.5 * jnp.square(h3 - tgt) * (1.0 / h2.shape[-1]),)


def _shift_down(u, d, rows):
    return u if d == 0 else jnp.where(rows >= d, pltpu.roll(u, d, 0), 0.0)


def _shift_up(u, d, rows):
    s = u.shape[0]
    return u if d == 0 else jnp.where(rows < s - d, pltpu.roll(u, s - d, 0), 0.0)


def conv_fwd(proj, col0, cd, conv_w, conv_b, name):
    s = proj.shape[0]
    cb0 = col0 // CONV_COLS

    def body(u_ref, w_ref, b_ref, o_ref):
        u = u_ref[...]
        rows = lax.broadcasted_iota(jnp.int32, u.shape, 0)
        y = jnp.broadcast_to(b_ref[...], u.shape)
        for k in range(CONV_K):
            y = y + w_ref[k:k + 1, :] * _shift_down(u, CONV_K - 1 - k, rows)
        o_ref[...] = y * _sigmoid(y)

    return pl.pallas_call(
        body, grid=(cd // CONV_COLS,),
        in_specs=[pl.BlockSpec((s, CONV_COLS), lambda i: (0, cb0 + i)),
                  pl.BlockSpec((CONV_K, CONV_COLS), lambda i: (0, i)),
                  pl.BlockSpec((1, CONV_COLS), lambda i: (0, i))],
        out_specs=pl.BlockSpec((s, CONV_COLS), lambda i: (0, i)),
        out_shape=SDS((s, cd), F32), compiler_params=_cparams(("parallel",)), name=name)(proj, conv_w, conv_b)


def conv_bwd(proj, col0, dout, ch0, conv_w, conv_b, name):
    s = proj.shape[0]
    ncb = dout.shape[1] // CONV_COLS
    cb0 = (col0 + ch0) // CONV_COLS
    wb0 = ch0 // CONV_COLS

    def body(u_ref, g_ref, w_ref, b_ref, du_ref, dw_ref, db_ref):
        u = u_ref[...]
        rows = lax.broadcasted_iota(jnp.int32, u.shape, 0)
        y = jnp.broadcast_to(b_ref[...], u.shape)
        for k in range(CONV_K):
            y = y + w_ref[k:k + 1, :] * _shift_down(u, CONV_K - 1 - k, rows)
        sg = _sigmoid(y)
        dy = g_ref[...] * (sg * (1.0 + y * (1.0 - sg)))
        du = jnp.zeros_like(u)
        for k in range(CONV_K):
            d = CONV_K - 1 - k
            du = du + w_ref[k:k + 1, :] * _shift_up(dy, d, rows)
            dw_ref[k:k + 1, :] = jnp.sum(dy * _shift_down(u, d, rows), axis=0, keepdims=True)
        du_ref[...] = du.astype(du_ref.dtype)
        db_ref[...] = jnp.sum(dy, axis=0, keepdims=True)

    return pl.pallas_call(
        body, grid=(ncb,),
        in_specs=[pl.BlockSpec((s, CONV_COLS), lambda i: (0, cb0 + i)),
                  pl.BlockSpec((s, CONV_COLS), lambda i: (0, i)),
                  pl.BlockSpec((CONV_K, CONV_COLS), lambda i: (0, wb0 + i)),
                  pl.BlockSpec((1, CONV_COLS), lambda i: (0, wb0 + i))],
        out_specs=[pl.BlockSpec((s, CONV_COLS), lambda i: (0, i)),
                   pl.BlockSpec((CONV_K, CONV_COLS), lambda i: (0, i)),
                   pl.BlockSpec((1, CONV_COLS), lambda i: (0, i))],
        out_shape=[SDS((s, ncb * CONV_COLS), BF16), SDS((CONV_K, ncb * CONV_COLS), F32), SDS((1, ncb * CONV_COLS), F32)],
        compiler_params=_cparams(("parallel",)), name=name)(proj, dout, conv_w, conv_b)


def _dot(a, b, dims):
    return lax.dot_general(a.astype(BF16), b.astype(BF16), (dims, ((), ())), preferred_element_type=F32)


NN = ((1,), (0,))
NT = ((1,), (1,))
TN = ((0,), (0,))


def ssd_chunk(xs, zs, nw, dtc, dtw, alogs, dsks, bm, cm, prev):
    ln = bm.shape[0]
    gw = GROUP_HEADS * HEAD_DIM
    row = lax.broadcasted_iota(jnp.int32, (ln, ln), 0)
    col = lax.broadcasted_iota(jnp.int32, (ln, ln), 1)
    causal = row >= col
    tri = causal.astype(F32)
    tri_t = (row <= col).astype(F32)
    lane_head = lax.broadcasted_iota(jnp.int32, (1, gw), 1) // HEAD_DIM
    sub_head = lax.broadcasted_iota(jnp.int32, (gw, 1), 0) // HEAD_DIM
    on_lanes = [(lane_head == r).astype(F32) for r in range(GROUP_HEADS)]
    on_rows = [(sub_head == r).astype(F32) for r in range(GROUP_HEADS)]
    cb = _dot(cm, bm, NT)
    decays, dt_full, acs_full, end_full, dsk_full, end_rows = [], 0.0, 0.0, 0.0, 0.0, 0.0
    for r in range(GROUP_HEADS):
        a = -jnp.exp(alogs[r])
        da_c = dtc[r] * a
        da_w = dtw[r] * a
        acs_c = jnp.sum(tri * da_w, axis=1, keepdims=True)
        acs_w = jnp.sum(tri_t * da_c, axis=0, keepdims=True)
        alast = jnp.sum(da_w, axis=1, keepdims=True)
        decays.append(jnp.exp(jnp.where(causal, acs_c - acs_w, -jnp.inf)))
        dt_full = dt_full + dtc[r] * on_lanes[r]
        acs_full = acs_full + acs_c * on_lanes[r]
        end_full = end_full + alast * on_lanes[r]
        dsk_full = dsk_full + dsks[r] * on_lanes[r]
        end_rows = end_rows + alast * on_rows[r]
    xd = xs * dt_full
    y = xs * dsk_full + _dot(cm, prev, NT) * jnp.exp(acs_full)
    for r in range(GROUP_HEADS):
        y = y + _dot(cb * decays[r], xd * on_lanes[r], NN)
    new_prev = prev * jnp.exp(end_rows) + _dot(xd * jnp.exp(end_full - acs_full), bm, TN)
    yg = y * (zs * _sigmoid(zs))
    rstd = lax.rsqrt(jnp.mean(yg * yg, axis=-1, keepdims=True) + RMS_EPS)
    return yg * rstd * nw, new_prev


def _ssd_specs(dm, cidx):
    gw = GROUP_HEADS * HEAD_DIM
    nb0 = dm.DI // D_STATE
    par = pl.BlockSpec((None, 1, GROUP_HEADS), lambda g, c: (g, 0, 0))
    return dict(
        z=pl.BlockSpec((CHUNK, gw), lambda g, c: (cidx(c), g)),
        xs=pl.BlockSpec((CHUNK, gw), lambda g, c: (cidx(c), g)),
        b=pl.BlockSpec((CHUNK, D_STATE), lambda g, c: (cidx(c), nb0 + g)),
        c=pl.BlockSpec((CHUNK, D_STATE), lambda g, c: (cidx(c), nb0 + dm.G + g)),
        dtc=pl.BlockSpec((None, CHUNK, GROUP_HEADS), lambda g, c: (g, cidx(c), 0)),
        dtw=pl.BlockSpec((None, GROUP_HEADS, CHUNK), lambda g, c: (g, 0, cidx(c))),
        par=par,
        nw=pl.BlockSpec((1, gw), lambda g, c: (0, g)),
        st=pl.BlockSpec((None, None, gw, D_STATE), lambda g, c: (g, cidx(c), 0, 0)),
    )


def _ssd_load(z_ref, xs_ref, b_ref, c_ref, dtc_ref, dtw_ref, alog_ref, dsk_ref, nw_ref):
    dtc = tuple(dtc_ref[:, r:r + 1] for r in range(GROUP_HEADS))
    dtw = tuple(dtw_ref[r:r + 1, :] for r in range(GROUP_HEADS))
    alogs = tuple(alog_ref[:, r:r + 1] for r in range(GROUP_HEADS))
    dsks = tuple(dsk_ref[:, r:r + 1] for r in range(GROUP_HEADS))
    return xs_ref[...], z_ref[...], nw_ref[...], dtc, dtw, alogs, dsks, b_ref[...], c_ref[...]


def ssd_fwd(proj, xbc, dtc, dtw, alog, dsk, nw, dm, name):
    nc = dm.S // CHUNK
    gw = GROUP_HEADS * HEAD_DIM
    sp = _ssd_specs(dm, lambda c: c)

    def body(z_ref, xs_ref, b_ref, c_ref, dtc_ref, dtw_ref, alog_ref, dsk_ref, nw_ref, y_ref, st_ref, prev):
        @pl.when(pl.program_id(1) == 0)
        def _():
            prev[...] = jnp.zeros_like(prev)

        args = _ssd_load(z_ref, xs_ref, b_ref, c_ref, dtc_ref, dtw_ref, alog_ref, dsk_ref, nw_ref)
        st_ref[...] = prev[...]
        out, new = ssd_chunk(*args, prev[...])
        y_ref[...] = out.astype(y_ref.dtype)
        prev[...] = new

    return pl.pallas_call(
        body, grid=(dm.G, nc),
        in_specs=[sp["z"], sp["xs"], sp["b"], sp["c"], sp["dtc"], sp["dtw"], sp["par"], sp["par"], sp["nw"]],
        out_specs=[sp["xs"], sp["st"]],
        out_shape=[SDS((dm.S, dm.DI), BF16), SDS((dm.G, nc, gw, D_STATE), F32)],
        scratch_shapes=[pltpu.VMEM((gw, D_STATE), F32)],
        compiler_params=_cparams(("parallel", "arbitrary")), name=name)(proj, xbc, xbc, xbc, dtc, dtw, alog, dsk, nw)


def ssd_bwd(proj, xbc, dtc, dtw, alog, dsk, nw, states, dy, dm, name):
    nc = dm.S // CHUNK
    sp = _ssd_specs(dm, lambda c: nc - 1 - c)
    gw = GROUP_HEADS * HEAD_DIM
    bc_spec = pl.BlockSpec((CHUNK, D_STATE), lambda g, c: (nc - 1 - c, g))

    def body(z_ref, xs_ref, b_ref, c_ref, dtc_ref, dtw_ref, alog_ref, dsk_ref, nw_ref, st_ref, dy_ref,
             dz_ref, dxs_ref, db_ref, dc_ref, ddtc_ref, ddtw_ref, dalog_ref, ddsk_ref, dnw_ref, dprev):
        first = pl.program_id(1) == 0

        @pl.when(first)
        def _():
            dprev[...] = jnp.zeros_like(dprev)

        args = _ssd_load(z_ref, xs_ref, b_ref, c_ref, dtc_ref, dtw_ref, alog_ref, dsk_ref, nw_ref)
        _, vjp = jax.vjp(ssd_chunk, *args, st_ref[...])
        gxs, gzs, gnw, gdtc, gdtw, galogs, gdsks, gb, gc, gprev = vjp((dy_ref[...], dprev[...]))
        dxs_ref[...] = gxs
        dz_ref[...] = gzs.astype(dz_ref.dtype)
        db_ref[...] = gb
        dc_ref[...] = gc
        dprev[...] = gprev
        for r in range(GROUP_HEADS):
            ddtc_ref[:, r:r + 1] = gdtc[r]
            ddtw_ref[r:r + 1, :] = gdtw[r]

        @pl.when(first)
        def _():
            dnw_ref[...] = gnw
            for r in range(GROUP_HEADS):
                dalog_ref[:, r:r + 1] = galogs[r]
                ddsk_ref[:, r:r + 1] = gdsks[r]

        @pl.when(jnp.logical_not(first))
        def _():
            dnw_ref[...] += gnw
            for r in range(GROUP_HEADS):
                dalog_ref[:, r:r + 1] += galogs[r]
                ddsk_ref[:, r:r + 1] += gdsks[r]

    xs_out = pl.BlockSpec((CHUNK, gw), lambda g, c: (nc - 1 - c, g))
    return pl.pallas_call(
        body, grid=(dm.G, nc),
        in_specs=[sp["z"], sp["xs"], sp["b"], sp["c"], sp["dtc"], sp["dtw"], sp["par"], sp["par"], sp["nw"],
                  sp["st"], xs_out],
        out_specs=[xs_out, xs_out, bc_spec, bc_spec, sp["dtc"], sp["dtw"], sp["par"], sp["par"], sp["nw"]],
        out_shape=[SDS((dm.S, dm.DI), BF16), SDS((dm.S, dm.DI), F32), SDS((dm.S, dm.G * D_STATE), F32),
                   SDS((dm.S, dm.G * D_STATE), F32), SDS((dm.G, dm.S, GROUP_HEADS), F32), SDS((dm.G, GROUP_HEADS, dm.S), F32),
                   SDS((dm.G, 1, GROUP_HEADS), F32), SDS((dm.G, 1, GROUP_HEADS), F32), SDS((1, dm.DI), F32)],
        scratch_shapes=[pltpu.VMEM((gw, D_STATE), F32)],
        compiler_params=_cparams(("parallel", "arbitrary")), name=name)(
            proj, xbc, xbc, xbc, dtc, dtw, alog, dsk, nw, states, dy)


def _split_bf16(v):
    hi = v.astype(BF16)
    return hi, (v - hi.astype(F32)).astype(BF16)


def _tri(v, mat):
    hi, lo = _split_bf16(v)
    return jnp.dot(hi, mat, preferred_element_type=F32) + jnp.dot(lo, mat, preferred_element_type=F32)


def _blocks(v):
    return [v[:, b * ATT_TILE:(b + 1) * ATT_TILE] for b in range(v.shape[1] // ATT_TILE)]


def _sb_group(z, mask, run, after_mat):
    sp = jnp.maximum(z, 0.0) + jnp.log(1.0 + jnp.exp(-jnp.abs(z)))
    lk = -sp if mask is None else jnp.where(mask, -sp, 0.0)
    cums = [_tri(v, after_mat) for v in _blocks(lk)]
    sums = [jnp.sum(v, axis=1, keepdims=True) for v in _blocks(lk)]
    later = [None] * len(cums)
    for b in reversed(range(len(cums))):
        later[b] = run + cums[b]
        run = run + sums[b]
    ls = z - sp
    w = jnp.exp(ls + jnp.concatenate(later, axis=1))
    if mask is not None:
        w = jnp.where(mask, w, 0.0)
    return ls, w, run


def _group_mask(i, g, t, gw):
    rows = i * t + lax.broadcasted_iota(jnp.int32, (t, gw), 0)
    cols = g * gw + lax.broadcasted_iota(jnp.int32, (t, gw), 1)
    return cols < rows


def _att_specs(dm, s):
    t = ATT_TILE
    qb, kb, vb = dm.off["q"] // LANES, dm.off["k"] // LANES, dm.off["v"] // LANES
    return (pl.BlockSpec((t, LANES), lambda p, i: (i, qb + p)),
            pl.BlockSpec((s, LANES), lambda p, i: (0, kb + p)),
            pl.BlockSpec((s, LANES), lambda p, i: (0, vb + p)))


def attn_fwd(proj, dm, name, rider=None):
    s, t = dm.S, ATT_TILE
    scale = HEAD_DIM ** -0.5
    hsl = [slice(h * HEAD_DIM, (h + 1) * HEAD_DIM) for h in range(2)]

    gw = ATT_GROUP * t

    def body(q_ref, k_ref, v_ref, o_ref):
        i = pl.program_id(1)
        gd = i // ATT_GROUP
        r_io = lax.broadcasted_iota(jnp.int32, (t, t), 0)
        c_io = lax.broadcasted_iota(jnp.int32, (t, t), 1)
        after_mat = (r_io > c_io).astype(BF16)
        qs = [q_ref[:, sl].astype(BF16) for sl in hsl]

        def group(g, carry, mask):
            r0 = pl.multiple_of(g * gw, gw)
            zs = [_dot(qs[h], k_ref[pl.ds(r0, gw), hsl[h]], NT) * scale for h in range(2)]
            res = [_sb_group(zs[h], mask, carry[h][0], after_mat) for h in range(2)]
            return tuple((res[h][2], carry[h][1] + _dot(res[h][1], v_ref[pl.ds(r0, gw), hsl[h]], NN)) for h in range(2))

        zero = (jnp.zeros((t, 1), F32), jnp.zeros((t, HEAD_DIM), F32))
        carry = group(gd, (zero, zero), _group_mask(i, gd, t, gw))
        carry = lax.fori_loop(1, gd + 1, lambda jj, c: group(gd - jj, c, None), carry)
        for h in range(2):
            o_ref[:, hsl[h]] = carry[h][1]

    qs_, ks_, vs_ = _att_specs(dm, s)
    res = _grid_call(body, grid=(dm.SBW // LANES, s // t), in_specs=[qs_, ks_, vs_],
                     out_specs=[pl.BlockSpec((t, LANES), lambda p, i: (i, p))], out_shape=[SDS((s, dm.SBW), F32)],
                     scratch=[], operands=(proj, proj, proj), name=name, rider=rider)
    return res[0] if rider is None else res


def attn_bwd(proj, do, dm, name, rider=None):
    s, t = dm.S, ATT_TILE
    nq = s // t
    gw = ATT_GROUP * t
    scale = HEAD_DIM ** -0.5
    hsl = [slice(h * HEAD_DIM, (h + 1) * HEAD_DIM) for h in range(2)]

    def body(q_ref, k_ref, v_ref, do_ref, dq_ref, dk_ref, dv_ref, dk_acc, dv_acc, g_scr, s_scr):
        i = pl.program_id(1)

        @pl.when(i == 0)
        def _():
            dk_acc[...] = jnp.zeros_like(dk_acc)
            dv_acc[...] = jnp.zeros_like(dv_acc)

        gd = i // ATT_GROUP
        r_io = lax.broadcasted_iota(jnp.int32, (t, t), 0)
        c_io = lax.broadcasted_iota(jnp.int32, (t, t), 1)
        after_mat = (r_io > c_io).astype(BF16)
        before_mat = (r_io < c_io).astype(BF16)
        qs = [q_ref[:, sl].astype(BF16) for sl in hsl]
        dos = [do_ref[:, sl].astype(BF16) for sl in hsl]
        q_t = q_ref[...].T.astype(BF16)
        do_t = do_ref[...].T.astype(BF16)

        def pass1(g, runs, mask):
            r0 = pl.multiple_of(g * gw, gw)
            zs = [_dot(qs[h], k_ref[pl.ds(r0, gw), hsl[h]], NT) * scale for h in range(2)]
            dws = [_dot(dos[h], v_ref[pl.ds(r0, gw), hsl[h]], NT) for h in range(2)]
            out = []
            for h in range(2):
                ls, w, run = _sb_group(zs[h], mask, runs[h], after_mat)
                g_scr[h, g] = dws[h] * w
                s_scr[h, g] = jnp.exp(ls)
                dv_acc[g, hsl[h], :] += _dot(do_t[hsl[h]], w, NN)
                out.append(run)
            return tuple(out)

        diag_mask = _group_mask(i, gd, t, gw)
        zero_col = jnp.zeros((t, 1), F32)
        runs = pass1(gd, (zero_col, zero_col), diag_mask)
        lax.fori_loop(1, gd + 1, lambda jj, r: pass1(gd - jj, r, None), runs)

        def pass2(g, carry, mask):
            r0 = pl.multiple_of(g * gw, gw)
            out = []
            for h in range(2):
                pre, dq = carry[h]
                gg = g_scr[h, g]
                sig = s_scr[h, g]
                before = []
                for v in _blocks(gg):
                    before.append(pre + _tri(v, before_mat))
                    pre = pre + jnp.sum(v, axis=1, keepdims=True)
                dz = gg * (1.0 - sig) - jnp.concatenate(before, axis=1) * sig
                if mask is not None:
                    dz = jnp.where(mask, dz, 0.0)
                dz = (dz * scale).astype(BF16)
                dk_acc[g, hsl[h], :] += _dot(q_t[hsl[h]], dz, NN)
                out.append((pre, dq + _dot(dz, k_ref[pl.ds(r0, gw), hsl[h]], NN)))
            return tuple(out)

        zero = (zero_col, jnp.zeros((t, HEAD_DIM), F32))
        carry = lax.fori_loop(0, gd, lambda g, c: pass2(g, c, None), (zero, zero))
        carry = pass2(gd, carry, diag_mask)
        for h in range(2):
            dq_ref[:, hsl[h]] = carry[h][1].astype(dq_ref.dtype)

        @pl.when(i == nq - 1)
        def _():
            for g in range(s // gw):
                dk_ref[g * gw:(g + 1) * gw, :] = dk_acc[g].T.astype(dk_ref.dtype)
                dv_ref[g * gw:(g + 1) * gw, :] = dv_acc[g].T.astype(dv_ref.dtype)

    qs_, ks_, vs_ = _att_specs(dm, s)
    tile_spec = pl.BlockSpec((t, LANES), lambda p, i: (i, p))
    full_spec = pl.BlockSpec((s, LANES), lambda p, i: (0, p))
    return _grid_call(
        body, grid=(dm.SBW // LANES, nq), in_specs=[qs_, ks_, vs_, tile_spec],
        out_specs=[tile_spec, full_spec, full_spec], out_shape=[SDS((s, dm.SBW), BF16)] * 3,
        scratch=[pltpu.VMEM((s // gw, LANES, gw), F32), pltpu.VMEM((s // gw, LANES, gw), F32),
                 pltpu.VMEM((2, s // gw, t, gw), F32), pltpu.VMEM((2, s // gw, t, gw), F32)],
        operands=(proj, proj, proj, do), name=name, rider=rider)


def adamw(w, g, m, v, name):
    rows, width = w.shape
    tile = _pick(rows, (256, 64, 16, 8, 4))
    c1 = 1.0 / (1.0 - ADAM_B1 ** ADAM_STEP)
    c2 = 1.0 / (1.0 - ADAM_B2 ** ADAM_STEP)

    def body(w_ref, g_ref, m_ref, v_ref, d_ref, nm_ref, nv_ref):
        gg = g_ref[...]
        nm = ADAM_B1 * m_ref[...] + (1.0 - ADAM_B1) * gg
        nv = ADAM_B2 * v_ref[...] + (1.0 - ADAM_B2) * (gg * gg)
        d_ref[...] = -ADAM_LR * ((nm * c1) / (jnp.sqrt(nv * c2) + ADAM_EPS) + ADAM_WD * w_ref[...])
        nm_ref[...] = nm
        nv_ref[...] = nv

    spec = pl.BlockSpec((tile, width), lambda i: (i, 0))
    return pl.pallas_call(
        body, grid=(rows // tile,), in_specs=[spec] * 4, out_specs=[spec] * 3,
        out_shape=[SDS((rows, width), F32)] * 3, compiler_params=_cparams(("parallel",)), name=name)(w, g, m, v)


def _me():
    return lax.axis_index("x"), lax.axis_index("y"), lax.axis_index("c")


def _other_chips(x, y):
    return [(1 - x, y), (x, 1 - y), (1 - x, 1 - y)]


def gather_weights(wp):
    rows, width = wp.shape
    half = rows // 2

    def body(w_ref, out_ref, send_sems, recv_sems, local_sem):
        x, y, c = _me()
        sibling = (x, y, 1 - c)
        chips = _other_chips(x, y)

        def part(cx, cy, hf):
            return out_ref.at[2 * cx + cy, pl.ds(hf * half, half), :]

        def copy(k, src, dst, to):
            return pltpu.make_async_remote_copy(src_ref=src, dst_ref=dst, send_sem=send_sems.at[k], recv_sem=recv_sems.at[k],
                                                device_id=to, device_id_type=MESH_ID)

        mine = pltpu.make_async_copy(w_ref, out_ref.at[2 * x + y], local_sem)
        mine.start()
        first = [copy(j, w_ref.at[pl.ds(c * half, half), :], part(x, y, c), (cx, cy, c)) for j, (cx, cy) in enumerate(chips)]
        for cp in first:
            cp.start()
        passed = [copy(3 + j, part(cx, cy, c), part(cx, cy, c), sibling) for j, (cx, cy) in enumerate(chips)]
        for j, (cx, cy) in enumerate(chips):
            copy(j, part(cx, cy, c), part(cx, cy, c), (x, y, c)).wait_recv()
            passed[j].start()
        for j, (cx, cy) in enumerate(chips):
            copy(3 + j, part(cx, cy, 1 - c), part(cx, cy, 1 - c), (x, y, c)).wait_recv()
        for cp in first + passed:
            cp.wait_send()
        mine.wait()

    return pl.pallas_call(
        body, out_shape=SDS((N_CHIPS, rows, width), wp.dtype), in_specs=[HBM_SPEC], out_specs=HBM_SPEC,
        scratch_shapes=[pltpu.SemaphoreType.DMA((6,)), pltpu.SemaphoreType.DMA((6,)), pltpu.SemaphoreType.DMA],
        name="gather_weights")(wp)


class Rider(NamedTuple):
    operands: tuple
    out_shape: tuple
    scratch: tuple
    start: object
    wait: object


def _with_rider(body, rider, grid, n_in, n_out):
    if rider is None:
        return body
    r_in, r_out = len(rider.operands), len(rider.out_shape)

    def full(*refs):
        ins, refs = refs[:n_in], refs[n_in:]
        rins, refs = refs[:r_in], refs[r_in:]
        outs, refs = refs[:n_out], refs[n_out:]
        routs, refs = refs[:r_out], refs[r_out:]
        scr, rscr = refs[:len(refs) - len(rider.scratch)], refs[len(refs) - len(rider.scratch):]
        ids = [pl.program_id(k) for k in range(len(grid))]
        first = functools.reduce(jnp.logical_and, [i == 0 for i in ids])
        last = functools.reduce(jnp.logical_and, [i == g - 1 for i, g in zip(ids, grid)])

        @pl.when(first)
        def _():
            rider.start(rins, routs, rscr)

        body(*ins, *outs, *scr)

        @pl.when(last)
        def _():
            rider.wait(rins, routs, rscr)

    return full


def gather_rider(wp):
    def copies(ins, outs, scr, sending):
        (w_ref,), (o_ref,), (send_sems, recv_sems, _) = ins, outs, scr
        x, y, c = _me()
        return [pltpu.make_async_remote_copy(src_ref=w_ref, dst_ref=o_ref.at[2 * x + y if sending else 2 * cx + cy],
                                             send_sem=send_sems.at[j], recv_sem=recv_sems.at[j], device_id=(cx, cy, c),
                                             device_id_type=MESH_ID)
                for j, (cx, cy) in enumerate(_other_chips(x, y))]

    def mine(ins, outs, scr):
        x, y, _ = _me()
        return pltpu.make_async_copy(ins[0], outs[0].at[2 * x + y], scr[2])

    def start(ins, outs, scr):
        mine(ins, outs, scr).start()
        for cp in copies(ins, outs, scr, True):
            cp.start()

    def wait(ins, outs, scr):
        for cp in copies(ins, outs, scr, False):
            cp.wait()
        mine(ins, outs, scr).wait()

    return Rider((wp,), (SDS((N_CHIPS,) + wp.shape, wp.dtype),),
                 (pltpu.SemaphoreType.DMA((3,)), pltpu.SemaphoreType.DMA((3,)), pltpu.SemaphoreType.DMA), start, wait)


def exchange_rider(sh):
    def copies(ins, outs, scr):
        (s_ref,), (b_ref,), (send_sems, recv_sems) = ins, outs, scr
        x, y, c = _me()
        return [pltpu.make_async_remote_copy(src_ref=s_ref.at[2 * cx + cy], dst_ref=b_ref.at[j], send_sem=send_sems.at[j],
                                             recv_sem=recv_sems.at[j], device_id=(cx, cy, c), device_id_type=MESH_ID)
                for j, (cx, cy) in enumerate(_other_chips(x, y))]

    def start(ins, outs, scr):
        for cp in copies(ins, outs, scr):
            cp.start()

    def wait(ins, outs, scr):
        for cp in copies(ins, outs, scr):
            cp.wait()

    return Rider((sh,), (SDS((3,) + sh.shape[1:], sh.dtype),),
                 (pltpu.SemaphoreType.DMA((3,)), pltpu.SemaphoreType.DMA((3,))), start, wait)


def swap_halves(g, name):
    n, rows, width = g.shape
    half = rows // 2

    def body(g_ref, a_ref, send_sem, recv_sem):
        x, y, c = _me()
        cp = pltpu.make_async_remote_copy(src_ref=g_ref.at[:, pl.ds((1 - c) * half, half), :], dst_ref=a_ref,
                                          send_sem=send_sem, recv_sem=recv_sem, device_id=(x, y, 1 - c), device_id_type=MESH_ID)
        cp.start()
        cp.wait()

    return pl.pallas_call(
        body, out_shape=SDS((n, half, width), g.dtype), in_specs=[HBM_SPEC], out_specs=HBM_SPEC,
        scratch_shapes=[pltpu.SemaphoreType.DMA, pltpu.SemaphoreType.DMA], name=name)(g)


def add_half(g, a, c_idx, name):
    n, rows, width = g.shape
    half = rows // 2
    tile = half // 2
    nt = half // tile

    def body(c_ref, g_ref, a_ref, o_ref, ob_ref):
        v = g_ref[...] + a_ref[...]
        o_ref[...] = v
        ob_ref[...] = v.astype(ob_ref.dtype)

    out_spec = pl.BlockSpec((None, tile, width), lambda s, i, c_ref: (s, i, 0))
    gs = pltpu.PrefetchScalarGridSpec(
        num_scalar_prefetch=1, grid=(n, nt),
        in_specs=[pl.BlockSpec((None, tile, width), lambda s, i, c_ref: (s, c_ref[0] * nt + i, 0)), out_spec],
        out_specs=[out_spec, out_spec])
    return pl.pallas_call(body, grid_spec=gs, out_shape=[SDS((n, half, width), F32), SDS((n, half, width), BF16)],
                          compiler_params=_cparams(("parallel", "parallel")), name=name)(c_idx, g, a)


def exchange_small(small):
    def body(sm_ref, all_ref, send_sems, recv_sems, local_sem):
        x, y, c = _me()
        mine = pltpu.make_async_copy(sm_ref, all_ref.at[0], local_sem)
        mine.start()
        copies = []
        for m in range(1, N_DEV):
            peer = (x ^ ((m >> 2) & 1), y ^ ((m >> 1) & 1), c ^ (m & 1))
            copies.append(pltpu.make_async_remote_copy(
                src_ref=sm_ref, dst_ref=all_ref.at[m], send_sem=send_sems.at[m - 1], recv_sem=recv_sems.at[m - 1],
                device_id=peer, device_id_type=MESH_ID))
        for cp in copies:
            cp.start()
        for cp in copies:
            cp.wait()
        mine.wait()

    return pl.pallas_call(
        body, out_shape=SDS((N_DEV,) + small.shape, small.dtype), in_specs=[HBM_SPEC], out_specs=HBM_SPEC,
        scratch_shapes=[pltpu.SemaphoreType.DMA((N_DEV - 1,)), pltpu.SemaphoreType.DMA((N_DEV - 1,)), pltpu.SemaphoreType.DMA],
        name="exchange_small")(small)


def add_chips(sh, b, k_idx, name):
    n, hf, width = sh.shape
    tile = hf // 2

    def body(k_ref, s_ref, b0, b1, b2, o_ref):
        o_ref[...] = ((s_ref[...] + b0[...].astype(F32)) + b1[...].astype(F32)) + b2[...].astype(F32)

    def bspec(j):
        return pl.BlockSpec((None, tile, width), lambda i, k_ref, j=j: (j, i, 0))

    gs = pltpu.PrefetchScalarGridSpec(
        num_scalar_prefetch=1, grid=(hf // tile,),
        in_specs=[pl.BlockSpec((None, tile, width), lambda i, k_ref: (k_ref[0], i, 0)), bspec(0), bspec(1), bspec(2)],
        out_specs=pl.BlockSpec((tile, width), lambda i, k_ref: (i, 0)))
    return pl.pallas_call(body, grid_spec=gs, out_shape=SDS((hf, width), sh.dtype),
                          compiler_params=_cparams(("parallel",)), name=name)(k_idx, sh, b, b, b)


def sum_small(allsm, me_idx, name):
    _, rows, width = allsm.shape

    def body(me_ref, a_ref, o_ref):
        me = me_ref[0]
        acc = a_ref[me]
        for dev in range(1, N_DEV):
            acc = acc + a_ref[jnp.bitwise_xor(me, dev)]
        o_ref[...] = acc

    gs = pltpu.PrefetchScalarGridSpec(
        num_scalar_prefetch=1, grid=(1,),
        in_specs=[pl.BlockSpec((N_DEV, rows, width), lambda i, me_ref: (0, 0, 0))],
        out_specs=pl.BlockSpec((rows, width), lambda i, me_ref: (0, 0)))
    return pl.pallas_call(body, grid_spec=gs, out_shape=SDS((rows, width), allsm.dtype),
                          compiler_params=_cparams(("arbitrary",)), name=name)(me_idx, allsm)


def join_halves(t, name):
    hf, width = t.shape

    def body(t_ref, o_ref, send_sem, recv_sem, local_sem):
        x, y, c = _me()
        mine = pltpu.make_async_copy(t_ref, o_ref.at[pl.ds(c * hf, hf), :], local_sem)
        mine.start()
        cp = pltpu.make_async_remote_copy(src_ref=t_ref, dst_ref=o_ref.at[pl.ds(c * hf, hf), :], send_sem=send_sem,
                                          recv_sem=recv_sem, device_id=(x, y, 1 - c), device_id_type=MESH_ID)
        cp.start()
        cp.wait_send()
        pltpu.make_async_remote_copy(src_ref=t_ref, dst_ref=o_ref.at[pl.ds((1 - c) * hf, hf), :], send_sem=send_sem,
                                     recv_sem=recv_sem, device_id=(x, y, 1 - c), device_id_type=MESH_ID).wait_recv()
        mine.wait()

    return pl.pallas_call(
        body, out_shape=SDS((2 * hf, width), t.dtype), in_specs=[HBM_SPEC], out_specs=HBM_SPEC,
        scratch_shapes=[pltpu.SemaphoreType.DMA, pltpu.SemaphoreType.DMA, pltpu.SemaphoreType.DMA], name=name)(t)


ROW_PAD = 64


def _pad_rows(rows):
    return -(-rows // ROW_PAD) * ROW_PAD


def _pack_rows(names, shard_shapes, width):
    return _pad_rows(sum((shard_shapes[n][0] * shard_shapes[n][1]) // width for n in names))


def unpack_local(packed, names, shard_shapes, width):
    out, r0 = {}, 0
    for n in names:
        a, b = shard_shapes[n]
        nr = (a * b) // width
        out[n] = packed[r0:r0 + nr].reshape(a, b)
        r0 += nr
    return out


EXACT_IN_GATHER = ("conv_w",)
EXACT_TERMS = 3


def pack_gather(shards, names, width):
    parts = []
    for n in names:
        if n in EXACT_IN_GATHER:
            rest = shards[n].astype(F32)
            for _ in range(EXACT_TERMS):
                term = rest.astype(BF16)
                parts.append(term.reshape(-1, width))
                rest = rest - term.astype(F32)
        else:
            parts.append(shards[n].reshape(-1, width).astype(BF16))
    used = sum(p.shape[0] for p in parts)
    parts.append(jnp.zeros((_pad_rows(used) - used, width), BF16))
    return jnp.concatenate(parts, axis=0)


def unpack_full(gathered, names, shard_shapes, width):
    out, r0 = {}, 0
    for n in names:
        a, b = shard_shapes[n]
        terms = EXACT_TERMS if n in EXACT_IN_GATHER else 1
        nr = (a * b) // width
        pieces = []
        for j in range(N_CHIPS):
            blk = gathered[j, r0:r0 + nr].reshape(a, b)
            for t in range(1, terms):
                blk = blk.astype(F32) + gathered[j, r0 + t * nr:r0 + (t + 1) * nr].reshape(a, b).astype(F32)
            pieces.append(blk)
        out[n] = jnp.concatenate(pieces, axis=SHARD_AXIS[n])
        r0 += terms * nr
    return out


def pack_full(grads, names, shard_shapes, width):
    total_rows = _pack_rows(names, shard_shapes, width)
    slabs = []
    for j in range(N_CHIPS):
        parts = []
        for n in names:
            a, b = shard_shapes[n]
            ax = SHARD_AXIS[n]
            sz = (a, b)[ax]
            piece = lax.slice_in_dim(grads[n], j * sz, (j + 1) * sz, axis=ax)
            parts.append(piece.reshape(-1, width))
        used = sum(p.shape[0] for p in parts)
        parts.append(jnp.zeros((total_rows - used, width), F32))
        slabs.append(jnp.concatenate(parts, axis=0))
    return jnp.stack(slabs, axis=0)


def _small_layout(sizes, width):
    lay, r = {}, 0
    for n in SMALL_WEIGHTS:
        nr = -(-sizes[n] // width)
        lay[n] = (r, nr, sizes[n])
        r += nr
    assert r <= SMALL_ROWS
    return lay


def pack_small(vals, lay, width):
    rows = []
    for n in SMALL_WEIGHTS:
        r, nr, sz = lay[n]
        v = vals[n].reshape(-1).astype(F32)
        rows.append(jnp.pad(v, (0, nr * width - sz)).reshape(nr, width))
    used = sum(r.shape[0] for r in rows)
    rows.append(jnp.zeros((SMALL_ROWS - used, width), F32))
    return jnp.concatenate(rows, axis=0)


def unpack_small(packed, lay):
    return {n: packed[r:r + nr].reshape(-1)[:sz].reshape(1, sz) for n, (r, nr, sz) in lay.items()}


PART_B = ("w_in", "conv_w", "w_gate")
PART_A = tuple(n for n in BIG_WEIGHTS if n not in PART_B)


class Dist(NamedTuple):
    packed_a: object
    shard_shapes: dict
    core_idx: object
    chip_idx: object


def local_step(x, p, tgt, wf, sm, dm, dist=None):
    s, d = dm.S, dm.D
    off = dm.off
    w_in = wf["w_in"]
    c0 = dm.DI + dm.CD
    w_all = jnp.concatenate(
        [w_in[:, :c0], w_in[:, c0 + dm.H:], wf["w_gate"], w_in[:, c0:c0 + dm.H],
         jnp.zeros((d, DT_PAD - dm.H), w_in.dtype)], axis=1).astype(BF16)
    g = dm.G
    per_group = lambda v: v.reshape(g, 1, GROUP_HEADS)
    alog, dsk = per_group(sm["a_log"]), per_group(sm["d_skip"])
    b_gate = sm["b_gate"]
    b_ssd, b_sb = b_gate[:, :d], b_gate[:, d:]
    gcol = off["gate"] // d

    (n1,) = row_fwd("norm1", f_norm1, [x], [sm["norm_mix_pre"]], [(d, BF16)])
    proj = matmul(n1, w_all, name="in_proj")
    xbc = conv_fwd(proj, off["xbc"], dm.CD, wf["conv_w"].astype(F32), sm["conv_b"], "conv_fwd")
    dt_raw = proj[:, off["dt"]:off["dt"] + dm.H]
    (dt,) = row_fwd("dt", f_dt, [dt_raw], [sm["dt_bias"]], [(dm.H, F32)])
    dtc = dt.reshape(s, g, GROUP_HEADS).transpose(1, 0, 2)
    dtw = dt.reshape(s, g, GROUP_HEADS).transpose(1, 2, 0)
    y_ssd, states = ssd_fwd(proj, xbc, dtc, dtw, alog, dsk, sm["ssd_norm"], dm, "ssd_fwd")
    if dist is None:
        y_sb = attn_fwd(proj, dm, "attn_fwd")
    else:
        y_sb, gathered_a = attn_fwd(proj, dm, "attn_fwd", rider=gather_rider(dist.packed_a))
        wf = {**wf, **unpack_full(gathered_a, PART_A, dist.shard_shapes, d)}
    yb_ssd = matmul(y_ssd, wf["w_ssd_branch"], name="ssd_branch")
    yb_sb = matmul(y_sb, wf["w_sb_branch"], name="sb_branch")
    merge_rows = [(proj, d, gcol), (proj, d, gcol + 1), yb_ssd, yb_sb]
    (merged,) = row_fwd("merge", f_merge, merge_rows, [b_ssd, b_sb], [(d, BF16)])
    mo = matmul(merged, wf["w_out"], name="w_out")
    h1, n2 = row_fwd("mix_out", f_mix_out, [x, mo], [sm["norm_mix_post"], sm["norm_ffn_pre"]], [(d, F32), (d, BF16)])
    a1 = matmul(n2, wf["w_ff1"], name="ff1")
    (act,) = row_fwd("relu2", f_relu2, [a1], [], [(dm.DFF, BF16)])
    ff = matmul(act, wf["w_ff2"], name="ff2")
    (h2,) = row_fwd("ffn_out", f_ffn_out, [h1, ff], [sm["norm_ffn_post"]], [(d, F32)])
    pg = matmul(h2, wf["w_ple_gate"], name="ple_gate")
    pe = matmul(p, wf["w_ple"], name="ple_emb")

    def reduce_start(names, tag):
        gbig = pack_full(gr, names, dist.shard_shapes, d)
        from_sibling = swap_halves(gbig, "swap_halves_" + tag)
        pair_sum, pair_sum_bf16 = add_half(gbig, from_sibling, dist.core_idx, "add_half_" + tag)
        return pair_sum, exchange_rider(pair_sum_bf16)

    def reduce_finish(names, pair_sum, from_chips, tag):
        my_half = add_chips(pair_sum, from_chips, dist.chip_idx, "add_chips_" + tag)
        return unpack_local(join_halves(my_half, "join_halves_" + tag), names, dist.shard_shapes, d)

    gr, reduced = {}, None
    (dh2_a, dpg, dpe), (gr["norm_ple_post"], loss_cols) = row_bwd(
        "ple_loss", f_ple_loss, [h2, pg, pe, tgt], [sm["norm_ple_post"]], [None], [F32, BF16, BF16, None], primal_sum=True)
    loss = jnp.sum(loss_cols)
    gr["w_ple"] = matmul(p, dpe, ta=True, name="d_w_ple")
    gr["w_ple_gate"] = matmul(h2, dpg, ta=True, name="d_w_ple_gate")
    dh2_b = matmul(dpg, wf["w_ple_gate"], tb=True, name="d_h2")
    (dh1_a, dff), (gr["norm_ffn_post"],) = row_bwd(
        "ffn_out_bwd", f_ffn_out, [h1, ff], [sm["norm_ffn_post"]], [[dh2_a, dh2_b]], [F32, BF16])
    gr["w_ff2"] = matmul(act, dff, ta=True, name="d_w_ff2")
    dact = matmul(dff, wf["w_ff2"], tb=True, name="d_act")
    (da1,), _ = row_bwd("relu2_bwd", f_relu2, [a1], [], [[dact]], [BF16])
    gr["w_ff1"] = matmul(n2, da1, ta=True, name="d_w_ff1")
    dn2 = matmul(da1, wf["w_ff1"], tb=True, name="d_n2")
    (dx_a, dmo), (gr["norm_mix_post"], gr["norm_ffn_pre"]) = row_bwd(
        "mix_out_bwd", f_mix_out, [x, mo], [sm["norm_mix_post"], sm["norm_ffn_pre"]], [[dh1_a], [dn2]], [F32, BF16])
    gr["w_out"] = matmul(merged, dmo, ta=True, name="d_w_out")
    dmerged = matmul(dmo, wf["w_out"], tb=True, name="d_merged")
    (dgp_ssd, dgp_sb, dyb_ssd, dyb_sb), (db_ssd, db_sb) = row_bwd(
        "merge_bwd", f_merge, merge_rows, [b_ssd, b_sb], [[dmerged]], [BF16, BF16, BF16, BF16])
    gr["b_gate"] = jnp.concatenate([db_ssd, db_sb], axis=1)
    gr["w_ssd_branch"] = matmul(y_ssd, dyb_ssd, ta=True, name="d_w_ssd_branch")
    gr["w_sb_branch"] = matmul(y_sb, dyb_sb, ta=True, name="d_w_sb_branch")
    dy_ssd = matmul(dyb_ssd, wf["w_ssd_branch"], tb=True, name="d_y_ssd")
    dy_sb = matmul(dyb_sb, wf["w_sb_branch"], tb=True, name="d_y_sb")
    if dist is None:
        dq, dk, dv = attn_bwd(proj, dy_sb, dm, "attn_bwd")
    else:
        pair_sum_a, rider_a = reduce_start(PART_A, "a")
        dq, dk, dv, from_chips_a = attn_bwd(proj, dy_sb, dm, "attn_bwd", rider=rider_a)
        reduced = reduce_finish(PART_A, pair_sum_a, from_chips_a, "a")
    dz, dxs, dbm, dcm, ddtc, ddtw, dalog, ddsk, gr["ssd_norm"] = ssd_bwd(
        proj, xbc, dtc, dtw, alog, dsk, sm["ssd_norm"], states, dy_ssd, dm, "ssd_bwd")
    gr["a_log"], gr["d_skip"] = (v.reshape(1, dm.H) for v in (dalog, ddsk))
    ddt_post = (ddtc.transpose(1, 0, 2) + ddtw.transpose(2, 0, 1)).reshape(s, dm.H)
    (ddt,), (gr["dt_bias"],) = row_bwd("dt_bwd", f_dt, [dt_raw], [sm["dt_bias"]], [[ddt_post]], [BF16])
    conv_w32 = wf["conv_w"].astype(F32)
    du_x, dw_x, dcb_x = conv_bwd(proj, off["xbc"], dxs, 0, conv_w32, sm["conv_b"], "conv_bwd_x")
    du_b, dw_b, dcb_b = conv_bwd(proj, off["xbc"], dbm, dm.DI, conv_w32, sm["conv_b"], "conv_bwd_b")
    du_c, dw_c, dcb_c = conv_bwd(proj, off["xbc"], dcm, dm.DI + g * D_STATE, conv_w32, sm["conv_b"], "conv_bwd_c")
    gr["conv_w"] = jnp.concatenate([dw_x, dw_b, dw_c], axis=1)
    gr["conv_b"] = jnp.concatenate([dcb_x, dcb_b, dcb_c], axis=1)
    dproj = jnp.concatenate(
        [dz, du_x, du_b, du_c, dq, dk, dv, dgp_ssd, dgp_sb, ddt, jnp.zeros((s, DT_PAD - dm.H), BF16)], axis=1)
    dw_all = matmul(n1, dproj, ta=True, name="d_w_all")
    gr["w_in"] = jnp.concatenate(
        [dw_all[:, :c0], dw_all[:, off["dt"]:off["dt"] + dm.H], dw_all[:, c0:c0 + 3 * dm.SBW]], axis=1)
    gr["w_gate"] = dw_all[:, off["gate"]:off["gate"] + 2 * d]
    if dist is None:
        dn1 = matmul(dproj, w_all, tb=True, name="d_n1")
    else:
        pair_sum_b, rider_b = reduce_start(PART_B, "b")
        dn1, from_chips_b = matmul(dproj, w_all, tb=True, name="d_n1", rider=rider_b)
        reduced.update(reduce_finish(PART_B, pair_sum_b, from_chips_b, "b"))
    (dx_b,), (gr["norm_mix_pre"],) = row_bwd("norm1_bwd", f_norm1, [x], [sm["norm_mix_pre"]], [[dn1]], [F32])
    (grad_x,) = row_fwd("grad_x", lambda u, v: (u + v,), [dx_a, dx_b], [], [(d, F32)])
    return loss, grad_x, gr, reduced


def kernel(x, p, norm_mix_pre, w_in, conv_w, conv_b, dt_bias, a_log, d_skip, ssd_norm, w_ssd_branch, w_sb_branch, w_gate, b_gate, w_out, norm_mix_post, norm_ffn_pre, w_ff1, w_ff2, norm_ffn_post, w_ple, w_ple_gate, norm_ple_post, loss_target, m_norm_mix_pre, m_w_in, m_conv_w, m_conv_b, m_dt_bias, m_a_log, m_d_skip, m_ssd_norm, m_w_ssd_branch, m_w_sb_branch, m_w_gate, m_b_gate, m_w_out, m_norm_mix_post, m_norm_ffn_pre, m_w_ff1, m_w_ff2, m_norm_ffn_post, m_w_ple, m_w_ple_gate, m_norm_ple_post, v_norm_mix_pre, v_w_in, v_conv_w, v_conv_b, v_dt_bias, v_a_log, v_d_skip, v_ssd_norm, v_w_ssd_branch, v_w_sb_branch, v_w_gate, v_b_gate, v_w_out, v_norm_mix_post, v_norm_ffn_pre, v_w_ff1, v_w_ff2, v_norm_ffn_post, v_w_ple, v_w_ple_gate, v_norm_ple_post):
    loc = dict(locals())
    unbatch = lambda a: a[0] if a.ndim == 3 else a
    w = {n: unbatch(loc[n]) for n in ALL_WEIGHTS}
    m = {n: unbatch(loc["m_" + n]) for n in ALL_WEIGHTS}
    v = {n: unbatch(loc["v_" + n]) for n in ALL_WEIGHTS}
    xs, ps, tgt = x[0], p[0, 0], loss_target[0]
    s, d = xs.shape
    di = w["w_ssd_branch"].shape[0] * N_CHIPS
    cd = w["conv_b"].shape[1]
    dm = Dims(S=s, D=d, DI=di, H=w["dt_bias"].shape[1], G=(cd - di) // (2 * D_STATE), CD=cd,
              SBW=w["w_sb_branch"].shape[0] * N_CHIPS, DFF=w["w_ff2"].shape[0] * N_CHIPS, PLE=ps.shape[1])
    ix, iy, ic = lax.axis_index("x"), lax.axis_index("y"), lax.axis_index("c")
    chip_idx = jnp.reshape(2 * ix + iy, (1,)).astype(jnp.int32)
    core_idx = jnp.reshape(ic, (1,)).astype(jnp.int32)
    dev_idx = jnp.reshape(4 * ix + 2 * iy + ic, (1,)).astype(jnp.int32)

    shard_shapes = {n: w[n].shape for n in BIG_WEIGHTS}
    wf = unpack_full(gather_weights(pack_gather(w, PART_B, d)), PART_B, shard_shapes, d)
    sm = {n: w[n] for n in SMALL_WEIGHTS}
    dist = Dist(packed_a=pack_gather(w, PART_A, d), shard_shapes=shard_shapes, core_idx=core_idx, chip_idx=chip_idx)

    loss_part, grad_x, gr, grads = local_step(xs, ps, tgt, wf, sm, dm, dist)
    loss = lax.psum(loss_part, ("x", "y", "c"))

    lay = _small_layout({n: w[n].shape[1] for n in SMALL_WEIGHTS}, d)
    gs_red = sum_small(exchange_small(pack_small(gr, lay, d)), dev_idx, "sum_small")
    grads.update(unpack_small(gs_red, lay))
    delta, new_m, new_v = {}, {}, {}
    for n in BIG_WEIGHTS:
        delta[n], new_m[n], new_v[n] = adamw(w[n], grads[n], m[n], v[n], "adamw_" + n)
    d_sm, nm_sm, nv_sm = adamw(pack_small(w, lay, d), gs_red, pack_small(m, lay, d), pack_small(v, lay, d), "adamw_small")
    for out, packed in ((delta, d_sm), (new_m, nm_sm), (new_v, nv_sm)):
        out.update(unpack_small(packed, lay))

    def leaves(vals):
        return [vals[n][None] if n in BIG_WEIGHTS else vals[n] for n in ALL_WEIGHTS]

    return (loss, grad_x[None], *leaves(grads), *leaves(delta), *leaves(new_m), *leaves(new_v))
```

```python
import functools
from typing import NamedTuple

import jax
import jax.numpy as jnp
from jax import lax
from jax.experimental import pallas as pl
from jax.experimental.pallas import tpu as pltpu

F32 = jnp.float32
BF16 = jnp.bfloat16
SDS = jax.ShapeDtypeStruct

HEAD_DIM = 64
GROUP_HEADS = 4
D_STATE = 128
CHUNK = 128
ATT_TILE = 128
ATT_GROUP = 4
ATT_DEAD = -160.0
LOG2E = 1.4426950408889634
CONV_K = 4
CONV_COLS = 128
RMS_EPS = 1e-6
LANES = 128
DT_PAD = 512
N_CHIPS = 4
N_DEV = 8
SMALL_ROWS = 16
VMEM_LIMIT = 48 * 1024 * 1024

ADAM_LR = 0.001
ADAM_B1 = 0.9
ADAM_B2 = 0.999
ADAM_EPS = 1e-08
ADAM_WD = 0.01
ADAM_STEP = 10

MESH_ID = pl.DeviceIdType.MESH
HBM_SPEC = pl.BlockSpec(memory_space=pltpu.HBM)

BIG_WEIGHTS = ("w_in", "conv_w", "w_ssd_branch", "w_sb_branch", "w_gate", "w_out", "w_ff1", "w_ff2", "w_ple", "w_ple_gate")
SHARD_AXIS = {"w_in": 1, "conv_w": 1, "w_ssd_branch": 0, "w_sb_branch": 0, "w_gate": 1, "w_out": 0, "w_ff1": 1,
              "w_ff2": 0, "w_ple": 1, "w_ple_gate": 0}
SMALL_WEIGHTS = ("norm_mix_pre", "conv_b", "dt_bias", "a_log", "d_skip", "ssd_norm", "b_gate", "norm_mix_post",
                 "norm_ffn_pre", "norm_ffn_post", "norm_ple_post")
ALL_WEIGHTS = ("norm_mix_pre", "w_in", "conv_w", "conv_b", "dt_bias", "a_log", "d_skip", "ssd_norm", "w_ssd_branch",
               "w_sb_branch", "w_gate", "b_gate", "w_out", "norm_mix_post", "norm_ffn_pre", "w_ff1", "w_ff2",
               "norm_ffn_post", "w_ple", "w_ple_gate", "norm_ple_post")


class Dims(NamedTuple):
    S: int
    D: int
    DI: int
    H: int
    G: int
    CD: int
    SBW: int
    DFF: int
    PLE: int

    @property
    def NA(self):
        return self.DI + self.CD + 3 * self.SBW + 2 * self.D + DT_PAD

    @property
    def off(self):
        o = {}
        o["z"] = 0
        o["xbc"] = self.DI
        o["q"] = self.DI + self.CD
        o["k"] = o["q"] + self.SBW
        o["v"] = o["k"] + self.SBW
        o["gate"] = o["v"] + self.SBW
        o["dt"] = o["gate"] + 2 * self.D
        return o


def _cparams(sem):
    return pltpu.CompilerParams(dimension_semantics=sem, vmem_limit_bytes=VMEM_LIMIT)


def _pick(n, cands):
    for c in cands:
        if n % c == 0:
            return c
    raise ValueError(f"no tile for {n}")


def _grid_call(body, *, grid, in_specs, out_specs, out_shape, scratch, operands, name, rider=None):
    if rider is None:
        sem = ("parallel",) + ("arbitrary",) * (len(grid) - 1)
        return pl.pallas_call(body, grid=grid, in_specs=in_specs, out_specs=out_specs, out_shape=out_shape,
                              scratch_shapes=scratch, compiler_params=_cparams(sem), name=name)(*operands)
    return pl.pallas_call(
        _with_rider(body, rider, grid, len(in_specs), len(out_specs)), grid=grid,
        in_specs=list(in_specs) + [HBM_SPEC] * len(rider.operands),
        out_specs=list(out_specs) + [HBM_SPEC] * len(rider.out_shape),
        out_shape=list(out_shape) + list(rider.out_shape), scratch_shapes=list(scratch) + list(rider.scratch),
        compiler_params=_cparams(("arbitrary",) * len(grid)), name=name)(*operands, *rider.operands)


def matmul(a, b, *, ta=False, tb=False, out_dtype=F32, name, rider=None):
    m, k = (a.shape[1], a.shape[0]) if ta else a.shape
    n, kb = b.shape if tb else (b.shape[1], b.shape[0])
    assert k == kb, (a.shape, b.shape, ta, tb)
    tm = _pick(m, (1024, 512, 256, 128))
    tn = _pick(n, (512, 256, 128))
    tk = _pick(k, (1024, 512, 256, 128))
    nk = k // tk
    dims = (((0 if ta else 1,), (1 if tb else 0,)), ((), ()))

    def body(a_ref, b_ref, o_ref, acc_ref):
        part = lax.dot_general(a_ref[...].astype(BF16), b_ref[...].astype(BF16), dims, preferred_element_type=F32)
        if nk == 1:
            o_ref[...] = part.astype(o_ref.dtype)
        else:
            kk = pl.program_id(2)

            @pl.when(kk == 0)
            def _():
                acc_ref[...] = part

            @pl.when(kk > 0)
            def _():
                acc_ref[...] += part

            @pl.when(kk == nk - 1)
            def _():
                o_ref[...] = acc_ref[...].astype(o_ref.dtype)

    a_spec = pl.BlockSpec((tk, tm), lambda i, j, kk: (kk, i)) if ta else pl.BlockSpec((tm, tk), lambda i, j, kk: (i, kk))
    b_spec = pl.BlockSpec((tn, tk), lambda i, j, kk: (j, kk)) if tb else pl.BlockSpec((tk, tn), lambda i, j, kk: (kk, j))
    res = _grid_call(body, grid=(m // tm, n // tn, nk), in_specs=[a_spec, b_spec],
                     out_specs=[pl.BlockSpec((tm, tn), lambda i, j, kk: (i, j))], out_shape=[SDS((m, n), out_dtype)],
                     scratch=[pltpu.VMEM((tm, tn), F32)], operands=(a, b), name=name, rider=rider)
    return res[0] if rider is None else res


def _row_spec(entry, tile):
    arr, width, cb = entry if isinstance(entry, tuple) else (entry, entry.shape[1], 0)
    return arr, pl.BlockSpec((tile, width), lambda i, cb=cb: (i, cb))


def _par_spec(p):
    return pl.BlockSpec(p.shape, lambda i: (0, 0))


def row_fwd(name, fn, rows, params, outs, tile=256):
    arrs, specs = zip(*[_row_spec(e, tile) for e in rows])
    s = arrs[0].shape[0]
    nr, npar = len(rows), len(params)

    def body(*refs):
        r = [x[...].astype(F32) for x in refs[:nr]]
        p = [x[...] for x in refs[nr:nr + npar]]
        res = fn(*r, *p)
        for o_ref, val in zip(refs[nr + npar:], res):
            o_ref[...] = val.astype(o_ref.dtype)

    return pl.pallas_call(
        body, grid=(s // tile,), in_specs=list(specs) + [_par_spec(p) for p in params],
        out_specs=[pl.BlockSpec((tile, w), lambda i: (i, 0)) for w, _ in outs],
        out_shape=[SDS((s, w), dt) for w, dt in outs],
        compiler_params=_cparams(("parallel",)), name=name)(*arrs, *params)


def row_bwd(name, fn, rows, params, cots, row_grads, tile=256, primal_sum=False):
    arrs, specs = zip(*[_row_spec(e, tile) for e in rows])
    s = arrs[0].shape[0]
    nr, npar = len(rows), len(params)
    cot_entries = [e for c in cots if c is not None for e in c]
    carrs, cspecs = zip(*[_row_spec(e, tile) for e in cot_entries]) if cot_entries else ((), ())
    nc = len(cot_entries)
    want = [i for i, d in enumerate(row_grads) if d is not None]

    def body(*refs):
        r = [x[...].astype(F32) for x in refs[:nr]]
        p = [x[...] for x in refs[nr:nr + npar]]
        cvals = [x[...].astype(F32) for x in refs[nr + npar:nr + npar + nc]]
        outs = refs[nr + npar + nc:]
        prim, vjp = jax.vjp(fn, *r, *p)
        ct, pos = [], 0
        for c, pr in zip(cots, prim):
            if c is None:
                ct.append(jnp.ones_like(pr))
            else:
                acc = cvals[pos]
                for extra in cvals[pos + 1:pos + len(c)]:
                    acc = acc + extra
                pos += len(c)
                ct.append(acc)
        grads = vjp(tuple(ct))
        for o_ref, i in zip(outs[:len(want)], want):
            o_ref[...] = grads[i].astype(o_ref.dtype)
        acc_refs = outs[len(want):]
        vals = [grads[nr + j] for j in range(npar)]
        if primal_sum:
            vals.append(jnp.sum(prim[0], axis=0, keepdims=True))
        first = pl.program_id(0) == 0

        @pl.when(first)
        def _():
            for a_ref, v in zip(acc_refs, vals):
                a_ref[...] = v

        @pl.when(jnp.logical_not(first))
        def _():
            for a_ref, v in zip(acc_refs, vals):
                a_ref[...] += v

    widths = [(e[1] if isinstance(e, tuple) else e.shape[1]) for e in rows]
    out_specs = [pl.BlockSpec((tile, widths[i]), lambda i_: (i_, 0)) for i in want]
    out_shape = [SDS((s, widths[i]), row_grads[i]) for i in want]
    pshapes = [p.shape for p in params]
    if primal_sum:
        pshapes.append((1, widths[0]))
    out_specs += [pl.BlockSpec(sh, lambda i_: (0, 0)) for sh in pshapes]
    out_shape += [SDS(sh, F32) for sh in pshapes]
    res = pl.pallas_call(
        body, grid=(s // tile,), in_specs=list(specs) + [_par_spec(p) for p in params] + list(cspecs),
        out_specs=out_specs, out_shape=out_shape,
        compiler_params=_cparams(("arbitrary",)), name=name)(*arrs, *params, *carrs)
    return res[:len(want)], res[len(want):]


def _rms(x, w):
    return x * lax.rsqrt(jnp.mean(x * x, axis=-1, keepdims=True) + RMS_EPS) * w


def _sigmoid(x):
    return jax.nn.sigmoid(x)


def _softplus(x):
    return jnp.maximum(x, 0.0) + jnp.log1p(jnp.exp(-jnp.abs(x)))


def f_norm1(x, w):
    return (_rms(x, w),)


def f_dt(raw, bias):
    return (_softplus(raw + bias),)


def f_merge(gp_ssd, gp_sb, yb_ssd, yb_sb, b_ssd, b_sb):
    return (_sigmoid(gp_ssd + b_ssd) * yb_ssd + _sigmoid(gp_sb + b_sb) * yb_sb,)


def f_mix_out(x, mo, w_post, w_pre):
    h1 = x + _rms(mo, w_post)
    return h1, _rms(h1, w_pre)


def f_relu2(a1):
    return (jnp.square(jnp.maximum(a1, 0.0)),)


def f_ffn_out(h1, ff, w):
    return (h1 + _rms(ff, w),)


def f_ple_loss(h2, pg, pe, tgt, w):
    h3 = h2 + _rms(_sigmoid(pg) * pe, w)
    return (0.5 * jnp.square(h3 - tgt) * (1.0 / h2.shape[-1]),)


def _shift_down(u, d, rows):
    return u if d == 0 else jnp.where(rows >= d, pltpu.roll(u, d, 0), 0.0)


def _shift_up(u, d, rows):
    s = u.shape[0]
    return u if d == 0 else jnp.where(rows < s - d, pltpu.roll(u, s - d, 0), 0.0)


def conv_fwd(proj, col0, cd, conv_w, conv_b, name):
    s = proj.shape[0]
    cb0 = col0 // CONV_COLS

    def body(u_ref, w_ref, b_ref, o_ref):
        u = u_ref[...]
        rows = lax.broadcasted_iota(jnp.int32, u.shape, 0)
        y = jnp.broadcast_to(b_ref[...], u.shape)
        for k in range(CONV_K):
            y = y + w_ref[k:k + 1, :] * _shift_down(u, CONV_K - 1 - k, rows)
        o_ref[...] = y * _sigmoid(y)

    return pl.pallas_call(
        body, grid=(cd // CONV_COLS,),
        in_specs=[pl.BlockSpec((s, CONV_COLS), lambda i: (0, cb0 + i)),
                  pl.BlockSpec((CONV_K, CONV_COLS), lambda i: (0, i)),
                  pl.BlockSpec((1, CONV_COLS), lambda i: (0, i))],
        out_specs=pl.BlockSpec((s, CONV_COLS), lambda i: (0, i)),
        out_shape=SDS((s, cd), F32), compiler_params=_cparams(("parallel",)), name=name)(proj, conv_w, conv_b)


def conv_bwd(proj, col0, dout, ch0, conv_w, conv_b, name):
    s = proj.shape[0]
    ncb = dout.shape[1] // CONV_COLS
    cb0 = (col0 + ch0) // CONV_COLS
    wb0 = ch0 // CONV_COLS

    def body(u_ref, g_ref, w_ref, b_ref, du_ref, dw_ref, db_ref):
        u = u_ref[...]
        rows = lax.broadcasted_iota(jnp.int32, u.shape, 0)
        y = jnp.broadcast_to(b_ref[...], u.shape)
        for k in range(CONV_K):
            y = y + w_ref[k:k + 1, :] * _shift_down(u, CONV_K - 1 - k, rows)
        sg = _sigmoid(y)
        dy = g_ref[...] * (sg * (1.0 + y * (1.0 - sg)))
        du = jnp.zeros_like(u)
        for k in range(CONV_K):
            d = CONV_K - 1 - k
            du = du + w_ref[k:k + 1, :] * _shift_up(dy, d, rows)
            dw_ref[k:k + 1, :] = jnp.sum(dy * _shift_down(u, d, rows), axis=0, keepdims=True)
        du_ref[...] = du.astype(du_ref.dtype)
        db_ref[...] = jnp.sum(dy, axis=0, keepdims=True)

    return pl.pallas_call(
        body, grid=(ncb,),
        in_specs=[pl.BlockSpec((s, CONV_COLS), lambda i: (0, cb0 + i)),
                  pl.BlockSpec((s, CONV_COLS), lambda i: (0, i)),
                  pl.BlockSpec((CONV_K, CONV_COLS), lambda i: (0, wb0 + i)),
                  pl.BlockSpec((1, CONV_COLS), lambda i: (0, wb0 + i))],
        out_specs=[pl.BlockSpec((s, CONV_COLS), lambda i: (0, i)),
                   pl.BlockSpec((CONV_K, CONV_COLS), lambda i: (0, i)),
                   pl.BlockSpec((1, CONV_COLS), lambda i: (0, i))],
        out_shape=[SDS((s, ncb * CONV_COLS), BF16), SDS((CONV_K, ncb * CONV_COLS), F32), SDS((1, ncb * CONV_COLS), F32)],
        compiler_params=_cparams(("parallel",)), name=name)(proj, dout, conv_w, conv_b)


def _dot(a, b, dims):
    return lax.dot_general(a.astype(BF16), b.astype(BF16), (dims, ((), ())), preferred_element_type=F32)


NN = ((1,), (0,))
NT = ((1,), (1,))
TN = ((0,), (0,))


def ssd_chunk(xs, zs, nw, dtc, dtw, alogs, dsks, bm, cm, prev):
    ln = bm.shape[0]
    gw = GROUP_HEADS * HEAD_DIM
    row = lax.broadcasted_iota(jnp.int32, (ln, ln), 0)
    col = lax.broadcasted_iota(jnp.int32, (ln, ln), 1)
    causal = row >= col
    tri = causal.astype(F32)
    tri_t = (row <= col).astype(F32)
    lane_head = lax.broadcasted_iota(jnp.int32, (1, gw), 1) // HEAD_DIM
    sub_head = lax.broadcasted_iota(jnp.int32, (gw, 1), 0) // HEAD_DIM
    on_lanes = [(lane_head == r).astype(F32) for r in range(GROUP_HEADS)]
    on_rows = [(sub_head == r).astype(F32) for r in range(GROUP_HEADS)]
    cb = _dot(cm, bm, NT)
    decays, dt_full, acs_full, end_full, dsk_full, end_rows = [], 0.0, 0.0, 0.0, 0.0, 0.0
    for r in range(GROUP_HEADS):
        a = -jnp.exp(alogs[r])
        da_c = dtc[r] * a
        da_w = dtw[r] * a
        acs_c = jnp.sum(tri * da_w, axis=1, keepdims=True)
        acs_w = jnp.sum(tri_t * da_c, axis=0, keepdims=True)
        alast = jnp.sum(da_w, axis=1, keepdims=True)
        decays.append(jnp.exp(jnp.where(causal, acs_c - acs_w, -jnp.inf)))
        dt_full = dt_full + dtc[r] * on_lanes[r]
        acs_full = acs_full + acs_c * on_lanes[r]
        end_full = end_full + alast * on_lanes[r]
        dsk_full = dsk_full + dsks[r] * on_lanes[r]
        end_rows = end_rows + alast * on_rows[r]
    xd = xs * dt_full
    y = xs * dsk_full + _dot(cm, prev, NT) * jnp.exp(acs_full)
    for r in range(GROUP_HEADS):
        y = y + _dot(cb * decays[r], xd * on_lanes[r], NN)
    new_prev = prev * jnp.exp(end_rows) + _dot(xd * jnp.exp(end_full - acs_full), bm, TN)
    yg = y * (zs * _sigmoid(zs))
    rstd = lax.rsqrt(jnp.mean(yg * yg, axis=-1, keepdims=True) + RMS_EPS)
    return yg * rstd * nw, new_prev


def _ssd_specs(dm, cidx):
    gw = GROUP_HEADS * HEAD_DIM
    nb0 = dm.DI // D_STATE
    par = pl.BlockSpec((None, 1, GROUP_HEADS), lambda g, c: (g, 0, 0))
    return dict(
        z=pl.BlockSpec((CHUNK, gw), lambda g, c: (cidx(c), g)),
        xs=pl.BlockSpec((CHUNK, gw), lambda g, c: (cidx(c), g)),
        b=pl.BlockSpec((CHUNK, D_STATE), lambda g, c: (cidx(c), nb0 + g)),
        c=pl.BlockSpec((CHUNK, D_STATE), lambda g, c: (cidx(c), nb0 + dm.G + g)),
        dtc=pl.BlockSpec((None, CHUNK, GROUP_HEADS), lambda g, c: (g, cidx(c), 0)),
        dtw=pl.BlockSpec((None, GROUP_HEADS, CHUNK), lambda g, c: (g, 0, cidx(c))),
        par=par,
        nw=pl.BlockSpec((1, gw), lambda g, c: (0, g)),
        st=pl.BlockSpec((None, None, gw, D_STATE), lambda g, c: (g, cidx(c), 0, 0)),
    )


def _ssd_load(z_ref, xs_ref, b_ref, c_ref, dtc_ref, dtw_ref, alog_ref, dsk_ref, nw_ref):
    dtc = tuple(dtc_ref[:, r:r + 1] for r in range(GROUP_HEADS))
    dtw = tuple(dtw_ref[r:r + 1, :] for r in range(GROUP_HEADS))
    alogs = tuple(alog_ref[:, r:r + 1] for r in range(GROUP_HEADS))
    dsks = tuple(dsk_ref[:, r:r + 1] for r in range(GROUP_HEADS))
    return xs_ref[...], z_ref[...], nw_ref[...], dtc, dtw, alogs, dsks, b_ref[...], c_ref[...]


def ssd_fwd(proj, xbc, dtc, dtw, alog, dsk, nw, dm, name):
    nc = dm.S // CHUNK
    gw = GROUP_HEADS * HEAD_DIM
    sp = _ssd_specs(dm, lambda c: c)

    def body(z_ref, xs_ref, b_ref, c_ref, dtc_ref, dtw_ref, alog_ref, dsk_ref, nw_ref, y_ref, st_ref, prev):
        @pl.when(pl.program_id(1) == 0)
        def _():
            prev[...] = jnp.zeros_like(prev)

        args = _ssd_load(z_ref, xs_ref, b_ref, c_ref, dtc_ref, dtw_ref, alog_ref, dsk_ref, nw_ref)
        st_ref[...] = prev[...]
        out, new = ssd_chunk(*args, prev[...])
        y_ref[...] = out.astype(y_ref.dtype)
        prev[...] = new

    return pl.pallas_call(
        body, grid=(dm.G, nc),
        in_specs=[sp["z"], sp["xs"], sp["b"], sp["c"], sp["dtc"], sp["dtw"], sp["par"], sp["par"], sp["nw"]],
        out_specs=[sp["xs"], sp["st"]],
        out_shape=[SDS((dm.S, dm.DI), BF16), SDS((dm.G, nc, gw, D_STATE), F32)],
        scratch_shapes=[pltpu.VMEM((gw, D_STATE), F32)],
        compiler_params=_cparams(("parallel", "arbitrary")), name=name)(proj, xbc, xbc, xbc, dtc, dtw, alog, dsk, nw)


def ssd_bwd(proj, xbc, dtc, dtw, alog, dsk, nw, states, dy, dm, name):
    nc = dm.S // CHUNK
    sp = _ssd_specs(dm, lambda c: nc - 1 - c)
    gw = GROUP_HEADS * HEAD_DIM
    bc_spec = pl.BlockSpec((CHUNK, D_STATE), lambda g, c: (nc - 1 - c, g))

    def body(z_ref, xs_ref, b_ref, c_ref, dtc_ref, dtw_ref, alog_ref, dsk_ref, nw_ref, st_ref, dy_ref,
             dz_ref, dxs_ref, db_ref, dc_ref, ddtc_ref, ddtw_ref, dalog_ref, ddsk_ref, dnw_ref, dprev):
        first = pl.program_id(1) == 0

        @pl.when(first)
        def _():
            dprev[...] = jnp.zeros_like(dprev)

        args = _ssd_load(z_ref, xs_ref, b_ref, c_ref, dtc_ref, dtw_ref, alog_ref, dsk_ref, nw_ref)
        _, vjp = jax.vjp(ssd_chunk, *args, st_ref[...])
        gxs, gzs, gnw, gdtc, gdtw, galogs, gdsks, gb, gc, gprev = vjp((dy_ref[...], dprev[...]))
        dxs_ref[...] = gxs
        dz_ref[...] = gzs.astype(dz_ref.dtype)
        db_ref[...] = gb
        dc_ref[...] = gc
        dprev[...] = gprev
        for r in range(GROUP_HEADS):
            ddtc_ref[:, r:r + 1] = gdtc[r]
            ddtw_ref[r:r + 1, :] = gdtw[r]

        @pl.when(first)
        def _():
            dnw_ref[...] = gnw
            for r in range(GROUP_HEADS):
                dalog_ref[:, r:r + 1] = galogs[r]
                ddsk_ref[:, r:r + 1] = gdsks[r]

        @pl.when(jnp.logical_not(first))
        def _():
            dnw_ref[...] += gnw
            for r in range(GROUP_HEADS):
                dalog_ref[:, r:r + 1] += galogs[r]
                ddsk_ref[:, r:r + 1] += gdsks[r]

    xs_out = pl.BlockSpec((CHUNK, gw), lambda g, c: (nc - 1 - c, g))
    return pl.pallas_call(
        body, grid=(dm.G, nc),
        in_specs=[sp["z"], sp["xs"], sp["b"], sp["c"], sp["dtc"], sp["dtw"], sp["par"], sp["par"], sp["nw"],
                  sp["st"], xs_out],
        out_specs=[xs_out, xs_out, bc_spec, bc_spec, sp["dtc"], sp["dtw"], sp["par"], sp["par"], sp["nw"]],
        out_shape=[SDS((dm.S, dm.DI), BF16), SDS((dm.S, dm.DI), F32), SDS((dm.S, dm.G * D_STATE), F32),
                   SDS((dm.S, dm.G * D_STATE), F32), SDS((dm.G, dm.S, GROUP_HEADS), F32), SDS((dm.G, GROUP_HEADS, dm.S), F32),
                   SDS((dm.G, 1, GROUP_HEADS), F32), SDS((dm.G, 1, GROUP_HEADS), F32), SDS((1, dm.DI), F32)],
        scratch_shapes=[pltpu.VMEM((gw, D_STATE), F32)],
        compiler_params=_cparams(("parallel", "arbitrary")), name=name)(
            proj, xbc, xbc, xbc, dtc, dtw, alog, dsk, nw, states, dy)


def _split_bf16(v):
    hi = v.astype(BF16)
    return hi, (v - hi.astype(F32)).astype(BF16)


def _tri(v, mat):
    hi, lo = _split_bf16(v)
    return jnp.dot(hi, mat, preferred_element_type=F32) + jnp.dot(lo, mat, preferred_element_type=F32)


def _blocks(v):
    return [v[:, b * ATT_TILE:(b + 1) * ATT_TILE] for b in range(v.shape[1] // ATT_TILE)]


def _sb_group(z, mask, run, after_mat):
    sp = jnp.maximum(z, 0.0) + jnp.log2(1.0 + jnp.exp2(-jnp.abs(z)))
    lk = -sp if mask is None else jnp.where(mask, -sp, 0.0)
    cums = [_tri(v, after_mat) for v in _blocks(lk)]
    sums = [jnp.sum(v, axis=1, keepdims=True) for v in _blocks(lk)]
    later = [None] * len(cums)
    for b in reversed(range(len(cums))):
        later[b] = run + cums[b]
        run = run + sums[b]
    ls = z - sp
    w = jnp.exp2(ls + jnp.concatenate(later, axis=1))
    if mask is not None:
        w = jnp.where(mask, w, 0.0)
    return ls, w, run


def _alive(run_a, run_b):
    return (jnp.max(jnp.maximum(run_a, run_b)) > ATT_DEAD).astype(jnp.int32)


def _group_mask(i, g, t, gw):
    rows = i * t + lax.broadcasted_iota(jnp.int32, (t, gw), 0)
    cols = g * gw + lax.broadcasted_iota(jnp.int32, (t, gw), 1)
    return cols < rows


def _att_specs(dm, s):
    t = ATT_TILE
    qb, kb, vb = dm.off["q"] // LANES, dm.off["k"] // LANES, dm.off["v"] // LANES
    return (pl.BlockSpec((t, LANES), lambda p, i: (i, qb + p)),
            pl.BlockSpec((s, LANES), lambda p, i: (0, kb + p)),
            pl.BlockSpec((s, LANES), lambda p, i: (0, vb + p)))


def attn_fwd(proj, dm, name, rider=None):
    s, t = dm.S, ATT_TILE
    scale = HEAD_DIM ** -0.5
    hsl = [slice(h * HEAD_DIM, (h + 1) * HEAD_DIM) for h in range(2)]

    gw = ATT_GROUP * t

    def body(q_ref, k_ref, v_ref, o_ref):
        i = pl.program_id(1)
        gd = i // ATT_GROUP
        r_io = lax.broadcasted_iota(jnp.int32, (t, t), 0)
        c_io = lax.broadcasted_iota(jnp.int32, (t, t), 1)
        after_mat = (r_io > c_io).astype(BF16)
        qs = [q_ref[:, sl].astype(BF16) for sl in hsl]

        def group(g, carry, mask):
            r0 = pl.multiple_of(g * gw, gw)
            zs = [_dot(qs[h], k_ref[pl.ds(r0, gw), hsl[h]], NT) * (scale * LOG2E) for h in range(2)]
            res = [_sb_group(zs[h], mask, carry[h][0], after_mat) for h in range(2)]
            return tuple((res[h][2], carry[h][1] + _dot(res[h][1], v_ref[pl.ds(r0, gw), hsl[h]], NN)) for h in range(2))

        zero = (jnp.zeros((t, 1), F32), jnp.zeros((t, HEAD_DIM), F32))
        carry = group(gd, (zero, zero), _group_mask(i, gd, t, gw))

        def step(st):
            jj, _, c = st
            c = group(gd - jj, c, None)
            return jj + 1, _alive(c[0][0], c[1][0]), c

        _, _, carry = lax.while_loop(lambda st: jnp.logical_and(st[0] <= gd, st[1] > 0), step,
                                     (jnp.int32(1), _alive(carry[0][0], carry[1][0]), carry))
        for h in range(2):
            o_ref[:, hsl[h]] = carry[h][1]

    qs_, ks_, vs_ = _att_specs(dm, s)
    res = _grid_call(body, grid=(dm.SBW // LANES, s // t), in_specs=[qs_, ks_, vs_],
                     out_specs=[pl.BlockSpec((t, LANES), lambda p, i: (i, p))], out_shape=[SDS((s, dm.SBW), F32)],
                     scratch=[], operands=(proj, proj, proj), name=name, rider=rider)
    return res[0] if rider is None else res


def attn_bwd(proj, do, dm, name, rider=None):
    s, t = dm.S, ATT_TILE
    nq = s // t
    gw = ATT_GROUP * t
    scale = HEAD_DIM ** -0.5
    hsl = [slice(h * HEAD_DIM, (h + 1) * HEAD_DIM) for h in range(2)]

    def body(q_ref, k_ref, v_ref, do_ref, dq_ref, dk_ref, dv_ref, dk_acc, dv_acc, g_scr, s_scr):
        i = pl.program_id(1)

        @pl.when(i == 0)
        def _():
            dk_acc[...] = jnp.zeros_like(dk_acc)
            dv_acc[...] = jnp.zeros_like(dv_acc)

        gd = i // ATT_GROUP
        r_io = lax.broadcasted_iota(jnp.int32, (t, t), 0)
        c_io = lax.broadcasted_iota(jnp.int32, (t, t), 1)
        after_mat = (r_io > c_io).astype(BF16)
        before_mat = (r_io < c_io).astype(BF16)
        qs = [q_ref[:, sl].astype(BF16) for sl in hsl]
        dos = [do_ref[:, sl].astype(BF16) for sl in hsl]
        q_t = q_ref[...].T.astype(BF16)
        do_t = do_ref[...].T.astype(BF16)

        def pass1(g, runs, mask):
            r0 = pl.multiple_of(g * gw, gw)
            zs = [_dot(qs[h], k_ref[pl.ds(r0, gw), hsl[h]], NT) * (scale * LOG2E) for h in range(2)]
            dws = [_dot(dos[h], v_ref[pl.ds(r0, gw), hsl[h]], NT) for h in range(2)]
            out = []
            for h in range(2):
                ls, w, run = _sb_group(zs[h], mask, runs[h], after_mat)
                g_scr[h, g] = dws[h] * w
                s_scr[h, g] = jnp.exp2(ls)
                dv_acc[g, hsl[h], :] += _dot(do_t[hsl[h]], w, NN)
                out.append(run)
            return tuple(out)

        diag_mask = _group_mask(i, gd, t, gw)
        zero_col = jnp.zeros((t, 1), F32)
        runs = pass1(gd, (zero_col, zero_col), diag_mask)

        def step1(st):
            jj, _, r = st
            r = pass1(gd - jj, r, None)
            return jj + 1, _alive(r[0], r[1]), r

        walked, _, _ = lax.while_loop(lambda st: jnp.logical_and(st[0] <= gd, st[1] > 0), step1,
                                      (jnp.int32(1), _alive(runs[0], runs[1]), runs))
        g_first = gd - (walked - 1)

        def pass2(g, carry, mask):
            r0 = pl.multiple_of(g * gw, gw)
            out = []
            for h in range(2):
                pre, dq = carry[h]
                gg = g_scr[h, g]
                sig = s_scr[h, g]
                before = []
                for v in _blocks(gg):
                    before.append(pre + _tri(v, before_mat))
                    pre = pre + jnp.sum(v, axis=1, keepdims=True)
                dz = gg * (1.0 - sig) - jnp.concatenate(before, axis=1) * sig
                if mask is not None:
                    dz = jnp.where(mask, dz, 0.0)
                dz = (dz * scale).astype(BF16)
                dk_acc[g, hsl[h], :] += _dot(q_t[hsl[h]], dz, NN)
                out.append((pre, dq + _dot(dz, k_ref[pl.ds(r0, gw), hsl[h]], NN)))
            return tuple(out)

        zero = (zero_col, jnp.zeros((t, HEAD_DIM), F32))
        carry = lax.fori_loop(g_first, gd, lambda g, c: pass2(g, c, None), (zero, zero))
        carry = pass2(gd, carry, diag_mask)
        for h in range(2):
            dq_ref[:, hsl[h]] = carry[h][1].astype(dq_ref.dtype)

        @pl.when(i == nq - 1)
        def _():
            for g in range(s // gw):
                dk_ref[g * gw:(g + 1) * gw, :] = dk_acc[g].T.astype(dk_ref.dtype)
                dv_ref[g * gw:(g + 1) * gw, :] = dv_acc[g].T.astype(dv_ref.dtype)

    qs_, ks_, vs_ = _att_specs(dm, s)
    tile_spec = pl.BlockSpec((t, LANES), lambda p, i: (i, p))
    full_spec = pl.BlockSpec((s, LANES), lambda p, i: (0, p))
    return _grid_call(
        body, grid=(dm.SBW // LANES, nq), in_specs=[qs_, ks_, vs_, tile_spec],
        out_specs=[tile_spec, full_spec, full_spec], out_shape=[SDS((s, dm.SBW), BF16)] * 3,
        scratch=[pltpu.VMEM((s // gw, LANES, gw), F32), pltpu.VMEM((s // gw, LANES, gw), F32),
                 pltpu.VMEM((2, s // gw, t, gw), F32), pltpu.VMEM((2, s // gw, t, gw), F32)],
        operands=(proj, proj, proj, do), name=name, rider=rider)


def adamw(w, g, m, v, name):
    rows, width = w.shape
    tile = _pick(rows, (256, 64, 16, 8, 4))
    c1 = 1.0 / (1.0 - ADAM_B1 ** ADAM_STEP)
    c2 = 1.0 / (1.0 - ADAM_B2 ** ADAM_STEP)

    def body(w_ref, g_ref, m_ref, v_ref, d_ref, nm_ref, nv_ref):
        gg = g_ref[...]
        nm = ADAM_B1 * m_ref[...] + (1.0 - ADAM_B1) * gg
        nv = ADAM_B2 * v_ref[...] + (1.0 - ADAM_B2) * (gg * gg)
        d_ref[...] = -ADAM_LR * ((nm * c1) / (jnp.sqrt(nv * c2) + ADAM_EPS) + ADAM_WD * w_ref[...])
        nm_ref[...] = nm
        nv_ref[...] = nv

    spec = pl.BlockSpec((tile, width), lambda i: (i, 0))
    return pl.pallas_call(
        body, grid=(rows // tile,), in_specs=[spec] * 4, out_specs=[spec] * 3,
        out_shape=[SDS((rows, width), F32)] * 3, compiler_params=_cparams(("parallel",)), name=name)(w, g, m, v)


def _me():
    return lax.axis_index("x"), lax.axis_index("y"), lax.axis_index("c")


def _other_chips(x, y):
    return [(1 - x, y), (x, 1 - y), (1 - x, 1 - y)]


def gather_weights(wp):
    rows, width = wp.shape
    half = rows // 2

    def body(w_ref, out_ref, send_sems, recv_sems, local_sem):
        x, y, c = _me()
        sibling = (x, y, 1 - c)
        chips = _other_chips(x, y)

        def part(cx, cy, hf):
            return out_ref.at[2 * cx + cy, hf]

        def copy(k, src, dst, to):
            return pltpu.make_async_remote_copy(src_ref=src, dst_ref=dst, send_sem=send_sems.at[k], recv_sem=recv_sems.at[k],
                                                device_id=to, device_id_type=MESH_ID)

        mine = pltpu.make_async_copy(w_ref, out_ref.at[2 * x + y], local_sem)
        mine.start()
        first = [copy(j, w_ref.at[c], part(x, y, c), (cx, cy, c)) for j, (cx, cy) in enumerate(chips)]
        for cp in first:
            cp.start()
        passed = [copy(3 + j, part(cx, cy, c), part(cx, cy, c), sibling) for j, (cx, cy) in enumerate(chips)]
        for j, (cx, cy) in enumerate(chips):
            copy(j, part(cx, cy, c), part(cx, cy, c), (x, y, c)).wait_recv()
            passed[j].start()
        for j, (cx, cy) in enumerate(chips):
            copy(3 + j, part(cx, cy, 1 - c), part(cx, cy, 1 - c), (x, y, c)).wait_recv()
        for cp in first + passed:
            cp.wait_send()
        mine.wait()

    return pl.pallas_call(
        body, out_shape=SDS((N_CHIPS, 2, half, width), wp.dtype), in_specs=[HBM_SPEC], out_specs=HBM_SPEC,
        scratch_shapes=[pltpu.SemaphoreType.DMA((6,)), pltpu.SemaphoreType.DMA((6,)), pltpu.SemaphoreType.DMA],
        name="gather_weights")(wp.reshape(2, half, width)).reshape(N_CHIPS, rows, width)


class Rider(NamedTuple):
    operands: tuple
    out_shape: tuple
    scratch: tuple
    start: object
    wait: object


def _with_rider(body, rider, grid, n_in, n_out):
    if rider is None:
        return body
    r_in, r_out = len(rider.operands), len(rider.out_shape)

    def full(*refs):
        ins, refs = refs[:n_in], refs[n_in:]
        rins, refs = refs[:r_in], refs[r_in:]
        outs, refs = refs[:n_out], refs[n_out:]
        routs, refs = refs[:r_out], refs[r_out:]
        scr, rscr = refs[:len(refs) - len(rider.scratch)], refs[len(refs) - len(rider.scratch):]
        ids = [pl.program_id(k) for k in range(len(grid))]
        first = functools.reduce(jnp.logical_and, [i == 0 for i in ids])
        last = functools.reduce(jnp.logical_and, [i == g - 1 for i, g in zip(ids, grid)])

        @pl.when(first)
        def _():
            rider.start(rins, routs, rscr)

        body(*ins, *outs, *scr)

        @pl.when(last)
        def _():
            rider.wait(rins, routs, rscr)

    return full


def gather_rider(wp):
    def copies(ins, outs, scr, sending):
        (w_ref,), (o_ref,), (send_sems, recv_sems, _) = ins, outs, scr
        x, y, c = _me()
        return [pltpu.make_async_remote_copy(src_ref=w_ref, dst_ref=o_ref.at[2 * x + y if sending else 2 * cx + cy],
                                             send_sem=send_sems.at[j], recv_sem=recv_sems.at[j], device_id=(cx, cy, c),
                                             device_id_type=MESH_ID)
                for j, (cx, cy) in enumerate(_other_chips(x, y))]

    def mine(ins, outs, scr):
        x, y, _ = _me()
        return pltpu.make_async_copy(ins[0], outs[0].at[2 * x + y], scr[2])

    def start(ins, outs, scr):
        mine(ins, outs, scr).start()
        for cp in copies(ins, outs, scr, True):
            cp.start()

    def wait(ins, outs, scr):
        for cp in copies(ins, outs, scr, False):
            cp.wait()
        mine(ins, outs, scr).wait()

    return Rider((wp,), (SDS((N_CHIPS,) + wp.shape, wp.dtype),),
                 (pltpu.SemaphoreType.DMA((3,)), pltpu.SemaphoreType.DMA((3,)), pltpu.SemaphoreType.DMA), start, wait)


def exchange_rider(sh):
    def copies(ins, outs, scr):
        (s_ref,), (b_ref,), (send_sems, recv_sems) = ins, outs, scr
        x, y, c = _me()
        return [pltpu.make_async_remote_copy(src_ref=s_ref.at[2 * cx + cy], dst_ref=b_ref.at[j], send_sem=send_sems.at[j],
                                             recv_sem=recv_sems.at[j], device_id=(cx, cy, c), device_id_type=MESH_ID)
                for j, (cx, cy) in enumerate(_other_chips(x, y))]

    def start(ins, outs, scr):
        for cp in copies(ins, outs, scr):
            cp.start()

    def wait(ins, outs, scr):
        for cp in copies(ins, outs, scr):
            cp.wait()

    return Rider((sh,), (SDS((3,) + sh.shape[1:], sh.dtype),),
                 (pltpu.SemaphoreType.DMA((3,)), pltpu.SemaphoreType.DMA((3,))), start, wait)


def swap_halves(g, name):
    n, rows, width = g.shape
    half = rows // 2

    def body(g_ref, a_ref, send_sem, recv_sem):
        x, y, c = _me()
        cp = pltpu.make_async_remote_copy(src_ref=g_ref.at[:, 1 - c], dst_ref=a_ref,
                                          send_sem=send_sem, recv_sem=recv_sem, device_id=(x, y, 1 - c), device_id_type=MESH_ID)
        cp.start()
        cp.wait()

    return pl.pallas_call(
        body, out_shape=SDS((n, half, width), g.dtype), in_specs=[HBM_SPEC], out_specs=HBM_SPEC,
        scratch_shapes=[pltpu.SemaphoreType.DMA, pltpu.SemaphoreType.DMA], name=name)(g.reshape(n, 2, half, width))


def add_half(g, a, c_idx, name):
    n, rows, width = g.shape
    half = rows // 2
    tile = half // 2
    nt = half // tile

    def body(c_ref, g_ref, a_ref, o_ref, ob_ref):
        v = g_ref[...] + a_ref[...]
        o_ref[...] = v
        ob_ref[...] = v.astype(ob_ref.dtype)

    out_spec = pl.BlockSpec((None, tile, width), lambda s, i, c_ref: (s, i, 0))
    gs = pltpu.PrefetchScalarGridSpec(
        num_scalar_prefetch=1, grid=(n, nt),
        in_specs=[pl.BlockSpec((None, tile, width), lambda s, i, c_ref: (s, c_ref[0] * nt + i, 0)), out_spec],
        out_specs=[out_spec, out_spec])
    return pl.pallas_call(body, grid_spec=gs, out_shape=[SDS((n, half, width), F32), SDS((n, half, width), BF16)],
                          compiler_params=_cparams(("parallel", "parallel")), name=name)(c_idx, g, a)


def exchange_small(small):
    def body(sm_ref, all_ref, send_sems, recv_sems, local_sem):
        x, y, c = _me()
        mine = pltpu.make_async_copy(sm_ref, all_ref.at[0], local_sem)
        mine.start()
        copies = []
        for m in range(1, N_DEV):
            peer = (x ^ ((m >> 2) & 1), y ^ ((m >> 1) & 1), c ^ (m & 1))
            copies.append(pltpu.make_async_remote_copy(
                src_ref=sm_ref, dst_ref=all_ref.at[m], send_sem=send_sems.at[m - 1], recv_sem=recv_sems.at[m - 1],
                device_id=peer, device_id_type=MESH_ID))
        for cp in copies:
            cp.start()
        for cp in copies:
            cp.wait()
        mine.wait()

    return pl.pallas_call(
        body, out_shape=SDS((N_DEV,) + small.shape, small.dtype), in_specs=[HBM_SPEC], out_specs=HBM_SPEC,
        scratch_shapes=[pltpu.SemaphoreType.DMA((N_DEV - 1,)), pltpu.SemaphoreType.DMA((N_DEV - 1,)), pltpu.SemaphoreType.DMA],
        name="exchange_small")(small)


def add_chips(sh, b, k_idx, name):
    n, hf, width = sh.shape
    tile = hf // 2

    def body(k_ref, s_ref, b0, b1, b2, o_ref):
        o_ref[...] = ((s_ref[...] + b0[...].astype(F32)) + b1[...].astype(F32)) + b2[...].astype(F32)

    def bspec(j):
        return pl.BlockSpec((None, tile, width), lambda i, k_ref, j=j: (j, i, 0))

    gs = pltpu.PrefetchScalarGridSpec(
        num_scalar_prefetch=1, grid=(hf // tile,),
        in_specs=[pl.BlockSpec((None, tile, width), lambda i, k_ref: (k_ref[0], i, 0)), bspec(0), bspec(1), bspec(2)],
        out_specs=pl.BlockSpec((tile, width), lambda i, k_ref: (i, 0)))
    return pl.pallas_call(body, grid_spec=gs, out_shape=SDS((hf, width), sh.dtype),
                          compiler_params=_cparams(("parallel",)), name=name)(k_idx, sh, b, b, b)


def sum_small(allsm, me_idx, name):
    _, rows, width = allsm.shape

    def body(me_ref, a_ref, o_ref):
        me = me_ref[0]
        acc = a_ref[me]
        for dev in range(1, N_DEV):
            acc = acc + a_ref[jnp.bitwise_xor(me, dev)]
        o_ref[...] = acc

    gs = pltpu.PrefetchScalarGridSpec(
        num_scalar_prefetch=1, grid=(1,),
        in_specs=[pl.BlockSpec((N_DEV, rows, width), lambda i, me_ref: (0, 0, 0))],
        out_specs=pl.BlockSpec((rows, width), lambda i, me_ref: (0, 0)))
    return pl.pallas_call(body, grid_spec=gs, out_shape=SDS((rows, width), allsm.dtype),
                          compiler_params=_cparams(("arbitrary",)), name=name)(me_idx, allsm)


def join_halves(t, name):
    hf, width = t.shape

    def body(t_ref, o_ref, send_sem, recv_sem, local_sem):
        x, y, c = _me()
        mine = pltpu.make_async_copy(t_ref, o_ref.at[c], local_sem)
        mine.start()
        cp = pltpu.make_async_remote_copy(src_ref=t_ref, dst_ref=o_ref.at[c], send_sem=send_sem,
                                          recv_sem=recv_sem, device_id=(x, y, 1 - c), device_id_type=MESH_ID)
        cp.start()
        cp.wait_send()
        pltpu.make_async_remote_copy(src_ref=t_ref, dst_ref=o_ref.at[1 - c], send_sem=send_sem,
                                     recv_sem=recv_sem, device_id=(x, y, 1 - c), device_id_type=MESH_ID).wait_recv()
        mine.wait()

    return pl.pallas_call(
        body, out_shape=SDS((2, hf, width), t.dtype), in_specs=[HBM_SPEC], out_specs=HBM_SPEC,
        scratch_shapes=[pltpu.SemaphoreType.DMA, pltpu.SemaphoreType.DMA, pltpu.SemaphoreType.DMA],
        name=name)(t).reshape(2 * hf, width)


ROW_PAD = 64


def _pad_rows(rows):
    return -(-rows // ROW_PAD) * ROW_PAD


def _pack_rows(names, shard_shapes, width):
    return _pad_rows(sum((shard_shapes[n][0] * shard_shapes[n][1]) // width for n in names))


def unpack_local(packed, names, shard_shapes, width):
    out, r0 = {}, 0
    for n in names:
        a, b = shard_shapes[n]
        nr = (a * b) // width
        out[n] = packed[r0:r0 + nr].reshape(a, b)
        r0 += nr
    return out


EXACT_IN_GATHER = ("conv_w",)
EXACT_TERMS = 3


def pack_gather(shards, names, width):
    parts = []
    for n in names:
        if n in EXACT_IN_GATHER:
            rest = shards[n].astype(F32)
            for _ in range(EXACT_TERMS):
                term = rest.astype(BF16)
                parts.append(term.reshape(-1, width))
                rest = rest - term.astype(F32)
        else:
            parts.append(shards[n].reshape(-1, width).astype(BF16))
    used = sum(p.shape[0] for p in parts)
    parts.append(jnp.zeros((_pad_rows(used) - used, width), BF16))
    return jnp.concatenate(parts, axis=0)


def unpack_full(gathered, names, shard_shapes, width):
    out, r0 = {}, 0
    for n in names:
        a, b = shard_shapes[n]
        terms = EXACT_TERMS if n in EXACT_IN_GATHER else 1
        nr = (a * b) // width
        pieces = []
        for j in range(N_CHIPS):
            blk = gathered[j, r0:r0 + nr].reshape(a, b)
            for t in range(1, terms):
                blk = blk.astype(F32) + gathered[j, r0 + t * nr:r0 + (t + 1) * nr].reshape(a, b).astype(F32)
            pieces.append(blk)
        out[n] = jnp.concatenate(pieces, axis=SHARD_AXIS[n])
        r0 += terms * nr
    return out


def pack_full(grads, names, shard_shapes, width):
    total_rows = _pack_rows(names, shard_shapes, width)
    slabs = []
    for j in range(N_CHIPS):
        parts = []
        for n in names:
            a, b = shard_shapes[n]
            ax = SHARD_AXIS[n]
            sz = (a, b)[ax]
            piece = lax.slice_in_dim(grads[n], j * sz, (j + 1) * sz, axis=ax)
            parts.append(piece.reshape(-1, width))
        used = sum(p.shape[0] for p in parts)
        parts.append(jnp.zeros((total_rows - used, width), F32))
        slabs.append(jnp.concatenate(parts, axis=0))
    return jnp.stack(slabs, axis=0)


def _small_layout(sizes, width):
    lay, r = {}, 0
    for n in SMALL_WEIGHTS:
        nr = -(-sizes[n] // width)
        lay[n] = (r, nr, sizes[n])
        r += nr
    assert r <= SMALL_ROWS
    return lay


def pack_small(vals, lay, width):
    rows = []
    for n in SMALL_WEIGHTS:
        r, nr, sz = lay[n]
        v = vals[n].reshape(-1).astype(F32)
        rows.append(jnp.pad(v, (0, nr * width - sz)).reshape(nr, width))
    used = sum(r.shape[0] for r in rows)
    rows.append(jnp.zeros((SMALL_ROWS - used, width), F32))
    return jnp.concatenate(rows, axis=0)


def unpack_small(packed, lay):
    return {n: packed[r:r + nr].reshape(-1)[:sz].reshape(1, sz) for n, (r, nr, sz) in lay.items()}


PART_B = ("w_in", "conv_w", "w_gate")
PART_A = tuple(n for n in BIG_WEIGHTS if n not in PART_B)


class Dist(NamedTuple):
    packed_a: object
    shard_shapes: dict
    core_idx: object
    chip_idx: object


def local_step(x, p, tgt, wf, sm, dm, dist=None):
    s, d = dm.S, dm.D
    off = dm.off
    w_in = wf["w_in"]
    c0 = dm.DI + dm.CD
    w_all = jnp.concatenate(
        [w_in[:, :c0], w_in[:, c0 + dm.H:], wf["w_gate"], w_in[:, c0:c0 + dm.H],
         jnp.zeros((d, DT_PAD - dm.H), w_in.dtype)], axis=1).astype(BF16)
    g = dm.G
    per_group = lambda v: v.reshape(g, 1, GROUP_HEADS)
    alog, dsk = per_group(sm["a_log"]), per_group(sm["d_skip"])
    b_gate = sm["b_gate"]
    b_ssd, b_sb = b_gate[:, :d], b_gate[:, d:]
    gcol = off["gate"] // d

    (n1,) = row_fwd("norm1", f_norm1, [x], [sm["norm_mix_pre"]], [(d, BF16)])
    proj = matmul(n1, w_all, name="in_proj")
    xbc = conv_fwd(proj, off["xbc"], dm.CD, wf["conv_w"].astype(F32), sm["conv_b"], "conv_fwd")
    dt_raw = proj[:, off["dt"]:off["dt"] + dm.H]
    (dt,) = row_fwd("dt", f_dt, [dt_raw], [sm["dt_bias"]], [(dm.H, F32)])
    dtc = dt.reshape(s, g, GROUP_HEADS).transpose(1, 0, 2)
    dtw = dt.reshape(s, g, GROUP_HEADS).transpose(1, 2, 0)
    y_ssd, states = ssd_fwd(proj, xbc, dtc, dtw, alog, dsk, sm["ssd_norm"], dm, "ssd_fwd")
    if dist is None:
        y_sb = attn_fwd(proj, dm, "attn_fwd")
    else:
        y_sb, gathered_a = attn_fwd(proj, dm, "attn_fwd", rider=gather_rider(dist.packed_a))
        wf = {**wf, **unpack_full(gathered_a, PART_A, dist.shard_shapes, d)}
    yb_ssd = matmul(y_ssd, wf["w_ssd_branch"], name="ssd_branch")
    yb_sb = matmul(y_sb, wf["w_sb_branch"], name="sb_branch")
    merge_rows = [(proj, d, gcol), (proj, d, gcol + 1), yb_ssd, yb_sb]
    (merged,) = row_fwd("merge", f_merge, merge_rows, [b_ssd, b_sb], [(d, BF16)])
    mo = matmul(merged, wf["w_out"], name="w_out")
    h1, n2 = row_fwd("mix_out", f_mix_out, [x, mo], [sm["norm_mix_post"], sm["norm_ffn_pre"]], [(d, F32), (d, BF16)])
    a1 = matmul(n2, wf["w_ff1"], name="ff1")
    (act,) = row_fwd("relu2", f_relu2, [a1], [], [(dm.DFF, BF16)])
    ff = matmul(act, wf["w_ff2"], name="ff2")
    (h2,) = row_fwd("ffn_out", f_ffn_out, [h1, ff], [sm["norm_ffn_post"]], [(d, F32)])
    pg = matmul(h2, wf["w_ple_gate"], name="ple_gate")
    pe = matmul(p, wf["w_ple"], name="ple_emb")

    def reduce_start(names, tag):
        gbig = pack_full(gr, names, dist.shard_shapes, d)
        from_sibling = swap_halves(gbig, "swap_halves_" + tag)
        pair_sum, pair_sum_bf16 = add_half(gbig, from_sibling, dist.core_idx, "add_half_" + tag)
        return pair_sum, exchange_rider(pair_sum_bf16)

    def reduce_finish(names, pair_sum, from_chips, tag):
        my_half = add_chips(pair_sum, from_chips, dist.chip_idx, "add_chips_" + tag)
        return unpack_local(join_halves(my_half, "join_halves_" + tag), names, dist.shard_shapes, d)

    gr, reduced = {}, None
    (dh2_a, dpg, dpe), (gr["norm_ple_post"], loss_cols) = row_bwd(
        "ple_loss", f_ple_loss, [h2, pg, pe, tgt], [sm["norm_ple_post"]], [None], [F32, BF16, BF16, None], primal_sum=True)
    loss = jnp.sum(loss_cols)
    gr["w_ple"] = matmul(p, dpe, ta=True, name="d_w_ple")
    gr["w_ple_gate"] = matmul(h2, dpg, ta=True, name="d_w_ple_gate")
    dh2_b = matmul(dpg, wf["w_ple_gate"], tb=True, name="d_h2")
    (dh1_a, dff), (gr["norm_ffn_post"],) = row_bwd(
        "ffn_out_bwd", f_ffn_out, [h1, ff], [sm["norm_ffn_post"]], [[dh2_a, dh2_b]], [F32, BF16])
    gr["w_ff2"] = matmul(act, dff, ta=True, name="d_w_ff2")
    dact = matmul(dff, wf["w_ff2"], tb=True, name="d_act")
    (da1,), _ = row_bwd("relu2_bwd", f_relu2, [a1], [], [[dact]], [BF16])
    gr["w_ff1"] = matmul(n2, da1, ta=True, name="d_w_ff1")
    dn2 = matmul(da1, wf["w_ff1"], tb=True, name="d_n2")
    (dx_a, dmo), (gr["norm_mix_post"], gr["norm_ffn_pre"]) = row_bwd(
        "mix_out_bwd", f_mix_out, [x, mo], [sm["norm_mix_post"], sm["norm_ffn_pre"]], [[dh1_a], [dn2]], [F32, BF16])
    gr["w_out"] = matmul(merged, dmo, ta=True, name="d_w_out")
    dmerged = matmul(dmo, wf["w_out"], tb=True, name="d_merged")
    (dgp_ssd, dgp_sb, dyb_ssd, dyb_sb), (db_ssd, db_sb) = row_bwd(
        "merge_bwd", f_merge, merge_rows, [b_ssd, b_sb], [[dmerged]], [BF16, BF16, BF16, BF16])
    gr["b_gate"] = jnp.concatenate([db_ssd, db_sb], axis=1)
    gr["w_ssd_branch"] = matmul(y_ssd, dyb_ssd, ta=True, name="d_w_ssd_branch")
    gr["w_sb_branch"] = matmul(y_sb, dyb_sb, ta=True, name="d_w_sb_branch")
    dy_ssd = matmul(dyb_ssd, wf["w_ssd_branch"], tb=True, name="d_y_ssd")
    dy_sb = matmul(dyb_sb, wf["w_sb_branch"], tb=True, name="d_y_sb")
    if dist is None:
        dq, dk, dv = attn_bwd(proj, dy_sb, dm, "attn_bwd")
    else:
        pair_sum_a, rider_a = reduce_start(PART_A, "a")
        dq, dk, dv, from_chips_a = attn_bwd(proj, dy_sb, dm, "attn_bwd", rider=rider_a)
        reduced = reduce_finish(PART_A, pair_sum_a, from_chips_a, "a")
    dz, dxs, dbm, dcm, ddtc, ddtw, dalog, ddsk, gr["ssd_norm"] = ssd_bwd(
        proj, xbc, dtc, dtw, alog, dsk, sm["ssd_norm"], states, dy_ssd, dm, "ssd_bwd")
    gr["a_log"], gr["d_skip"] = (v.reshape(1, dm.H) for v in (dalog, ddsk))
    ddt_post = (ddtc.transpose(1, 0, 2) + ddtw.transpose(2, 0, 1)).reshape(s, dm.H)
    (ddt,), (gr["dt_bias"],) = row_bwd("dt_bwd", f_dt, [dt_raw], [sm["dt_bias"]], [[ddt_post]], [BF16])
    conv_w32 = wf["conv_w"].astype(F32)
    du_x, dw_x, dcb_x = conv_bwd(proj, off["xbc"], dxs, 0, conv_w32, sm["conv_b"], "conv_bwd_x")
    du_b, dw_b, dcb_b = conv_bwd(proj, off["xbc"], dbm, dm.DI, conv_w32, sm["conv_b"], "conv_bwd_b")
    du_c, dw_c, dcb_c = conv_bwd(proj, off["xbc"], dcm, dm.DI + g * D_STATE, conv_w32, sm["conv_b"], "conv_bwd_c")
    gr["conv_w"] = jnp.concatenate([dw_x, dw_b, dw_c], axis=1)
    gr["conv_b"] = jnp.concatenate([dcb_x, dcb_b, dcb_c], axis=1)
    dproj = jnp.concatenate(
        [dz, du_x, du_b, du_c, dq, dk, dv, dgp_ssd, dgp_sb, ddt, jnp.zeros((s, DT_PAD - dm.H), BF16)], axis=1)
    dw_all = matmul(n1, dproj, ta=True, name="d_w_all")
    gr["w_in"] = jnp.concatenate(
        [dw_all[:, :c0], dw_all[:, off["dt"]:off["dt"] + dm.H], dw_all[:, c0:c0 + 3 * dm.SBW]], axis=1)
    gr["w_gate"] = dw_all[:, off["gate"]:off["gate"] + 2 * d]
    if dist is None:
        dn1 = matmul(dproj, w_all, tb=True, name="d_n1")
    else:
        pair_sum_b, rider_b = reduce_start(PART_B, "b")
        dn1, from_chips_b = matmul(dproj, w_all, tb=True, name="d_n1", rider=rider_b)
        reduced.update(reduce_finish(PART_B, pair_sum_b, from_chips_b, "b"))
    (dx_b,), (gr["norm_mix_pre"],) = row_bwd("norm1_bwd", f_norm1, [x], [sm["norm_mix_pre"]], [[dn1]], [F32])
    (grad_x,) = row_fwd("grad_x", lambda u, v: (u + v,), [dx_a, dx_b], [], [(d, F32)])
    return loss, grad_x, gr, reduced


def kernel(x, p, norm_mix_pre, w_in, conv_w, conv_b, dt_bias, a_log, d_skip, ssd_norm, w_ssd_branch, w_sb_branch, w_gate, b_gate, w_out, norm_mix_post, norm_ffn_pre, w_ff1, w_ff2, norm_ffn_post, w_ple, w_ple_gate, norm_ple_post, loss_target, m_norm_mix_pre, m_w_in, m_conv_w, m_conv_b, m_dt_bias, m_a_log, m_d_skip, m_ssd_norm, m_w_ssd_branch, m_w_sb_branch, m_w_gate, m_b_gate, m_w_out, m_norm_mix_post, m_norm_ffn_pre, m_w_ff1, m_w_ff2, m_norm_ffn_post, m_w_ple, m_w_ple_gate, m_norm_ple_post, v_norm_mix_pre, v_w_in, v_conv_w, v_conv_b, v_dt_bias, v_a_log, v_d_skip, v_ssd_norm, v_w_ssd_branch, v_w_sb_branch, v_w_gate, v_b_gate, v_w_out, v_norm_mix_post, v_norm_ffn_pre, v_w_ff1, v_w_ff2, v_norm_ffn_post, v_w_ple, v_w_ple_gate, v_norm_ple_post):
    loc = dict(locals())
    unbatch = lambda a: a[0] if a.ndim == 3 else a
    w = {n: unbatch(loc[n]) for n in ALL_WEIGHTS}
    m = {n: unbatch(loc["m_" + n]) for n in ALL_WEIGHTS}
    v = {n: unbatch(loc["v_" + n]) for n in ALL_WEIGHTS}
    xs, ps, tgt = x[0], p[0, 0], loss_target[0]
    s, d = xs.shape
    di = w["w_ssd_branch"].shape[0] * N_CHIPS
    cd = w["conv_b"].shape[1]
    dm = Dims(S=s, D=d, DI=di, H=w["dt_bias"].shape[1], G=(cd - di) // (2 * D_STATE), CD=cd,
              SBW=w["w_sb_branch"].shape[0] * N_CHIPS, DFF=w["w_ff2"].shape[0] * N_CHIPS, PLE=ps.shape[1])
    ix, iy, ic = lax.axis_index("x"), lax.axis_index("y"), lax.axis_index("c")
    chip_idx = jnp.reshape(2 * ix + iy, (1,)).astype(jnp.int32)
    core_idx = jnp.reshape(ic, (1,)).astype(jnp.int32)
    dev_idx = jnp.reshape(4 * ix + 2 * iy + ic, (1,)).astype(jnp.int32)

    shard_shapes = {n: w[n].shape for n in BIG_WEIGHTS}
    wf = unpack_full(gather_weights(pack_gather(w, PART_B, d)), PART_B, shard_shapes, d)
    sm = {n: w[n] for n in SMALL_WEIGHTS}
    dist = Dist(packed_a=pack_gather(w, PART_A, d), shard_shapes=shard_shapes, core_idx=core_idx, chip_idx=chip_idx)

    loss_part, grad_x, gr, grads = local_step(xs, ps, tgt, wf, sm, dm, dist)
    loss = lax.psum(loss_part, ("x", "y", "c"))

    lay = _small_layout({n: w[n].shape[1] for n in SMALL_WEIGHTS}, d)
    gs_red = sum_small(exchange_small(pack_small(gr, lay, d)), dev_idx, "sum_small")
    grads.update(unpack_small(gs_red, lay))
    delta, new_m, new_v = {}, {}, {}
    for n in BIG_WEIGHTS:
        delta[n], new_m[n], new_v[n] = adamw(w[n], grads[n], m[n], v[n], "adamw_" + n)
    d_sm, nm_sm, nv_sm = adamw(pack_small(w, lay, d), gs_red, pack_small(m, lay, d), pack_small(v, lay, d), "adamw_small")
    for out, packed in ((delta, d_sm), (new_m, nm_sm), (new_v, nv_sm)):
        out.update(unpack_small(packed, lay))

    def leaves(vals):
        return [vals[n][None] if n in BIG_WEIGHTS else vals[n] for n in ALL_WEIGHTS]

    return (loss, grad_x[None], *leaves(grads), *leaves(delta), *leaves(new_m), *leaves(new_v))
```

```python
import functools
from typing import NamedTuple

import jax
import jax.numpy as jnp
from jax import lax
from jax.experimental import pallas as pl
from jax.experimental.pallas import tpu as pltpu

F32 = jnp.float32
BF16 = jnp.bfloat16
SDS = jax.ShapeDtypeStruct

HEAD_DIM = 64
GROUP_HEADS = 4
D_STATE = 128
CHUNK = 128
ATT_TILE = 128
ATT_GROUP = 4
ATT_DEAD = -160.0
LOG2E = 1.4426950408889634
CONV_K = 4
CONV_COLS = 128
RMS_EPS = 1e-6
LANES = 128
DT_PAD = 512
N_CHIPS = 4
N_DEV = 8
SMALL_ROWS = 16
VMEM_LIMIT = 48 * 1024 * 1024
MAX_TK = 3072

ADAM_LR = 0.001
ADAM_B1 = 0.9
ADAM_B2 = 0.999
ADAM_EPS = 1e-08
ADAM_WD = 0.01
ADAM_STEP = 10

MESH_ID = pl.DeviceIdType.MESH
HBM_SPEC = pl.BlockSpec(memory_space=pltpu.HBM)

BIG_WEIGHTS = ("w_in", "conv_w", "w_ssd_branch", "w_sb_branch", "w_gate", "w_out", "w_ff1", "w_ff2", "w_ple", "w_ple_gate")
SHARD_AXIS = {"w_in": 1, "conv_w": 1, "w_ssd_branch": 0, "w_sb_branch": 0, "w_gate": 1, "w_out": 0, "w_ff1": 1,
              "w_ff2": 0, "w_ple": 1, "w_ple_gate": 0}
SMALL_WEIGHTS = ("norm_mix_pre", "conv_b", "dt_bias", "a_log", "d_skip", "ssd_norm", "b_gate", "norm_mix_post",
                 "norm_ffn_pre", "norm_ffn_post", "norm_ple_post")
ALL_WEIGHTS = ("norm_mix_pre", "w_in", "conv_w", "conv_b", "dt_bias", "a_log", "d_skip", "ssd_norm", "w_ssd_branch",
               "w_sb_branch", "w_gate", "b_gate", "w_out", "norm_mix_post", "norm_ffn_pre", "w_ff1", "w_ff2",
               "norm_ffn_post", "w_ple", "w_ple_gate", "norm_ple_post")


class Dims(NamedTuple):
    S: int
    D: int
    DI: int
    H: int
    G: int
    CD: int
    SBW: int
    DFF: int
    PLE: int

    @property
    def NA(self):
        return self.DI + self.CD + 3 * self.SBW + 2 * self.D + DT_PAD

    @property
    def off(self):
        o = {}
        o["z"] = 0
        o["xbc"] = self.DI
        o["q"] = self.DI + self.CD
        o["k"] = o["q"] + self.SBW
        o["v"] = o["k"] + self.SBW
        o["gate"] = o["v"] + self.SBW
        o["dt"] = o["gate"] + 2 * self.D
        return o


def _cparams(sem):
    return pltpu.CompilerParams(dimension_semantics=sem, vmem_limit_bytes=VMEM_LIMIT)


def _pick(n, cands):
    for c in cands:
        if n % c == 0:
            return c
    raise ValueError(f"no tile for {n}")


def _grid_call(body, *, grid, in_specs, out_specs, out_shape, scratch, operands, name, rider=None):
    if rider is None:
        sem = ("parallel",) + ("arbitrary",) * (len(grid) - 1)
        return pl.pallas_call(body, grid=grid, in_specs=in_specs, out_specs=out_specs, out_shape=out_shape,
                              scratch_shapes=scratch, compiler_params=_cparams(sem), name=name)(*operands)
    return pl.pallas_call(
        _with_rider(body, rider, grid, len(in_specs), len(out_specs)), grid=grid,
        in_specs=list(in_specs) + [HBM_SPEC] * len(rider.operands),
        out_specs=list(out_specs) + [HBM_SPEC] * len(rider.out_shape),
        out_shape=list(out_shape) + list(rider.out_shape), scratch_shapes=list(scratch) + list(rider.scratch),
        compiler_params=_cparams(("arbitrary",) * len(grid)), name=name)(*operands, *rider.operands)


def matmul(a, b, *, ta=False, tb=False, out_dtype=F32, name, rider=None):
    m, k = (a.shape[1], a.shape[0]) if ta else a.shape
    n, kb = b.shape if tb else (b.shape[1], b.shape[0])
    assert k == kb, (a.shape, b.shape, ta, tb)
    tm = _pick(m, (1024, 512, 256, 128))
    tn = _pick(n, (512, 256, 128))
    tk = max(t for t in range(LANES, min(k, MAX_TK) + 1, LANES) if k % t == 0)
    nk = k // tk
    dims = (((0 if ta else 1,), (1 if tb else 0,)), ((), ()))

    def body(a_ref, b_ref, o_ref, acc_ref):
        part = lax.dot_general(a_ref[...].astype(BF16), b_ref[...].astype(BF16), dims, preferred_element_type=F32)
        if nk == 1:
            o_ref[...] = part.astype(o_ref.dtype)
        else:
            kk = pl.program_id(2)

            @pl.when(kk == 0)
            def _():
                acc_ref[...] = part

            @pl.when(kk > 0)
            def _():
                acc_ref[...] += part

            @pl.when(kk == nk - 1)
            def _():
                o_ref[...] = acc_ref[...].astype(o_ref.dtype)

    a_spec = pl.BlockSpec((tk, tm), lambda i, j, kk: (kk, i)) if ta else pl.BlockSpec((tm, tk), lambda i, j, kk: (i, kk))
    b_spec = pl.BlockSpec((tn, tk), lambda i, j, kk: (j, kk)) if tb else pl.BlockSpec((tk, tn), lambda i, j, kk: (kk, j))
    res = _grid_call(body, grid=(m // tm, n // tn, nk), in_specs=[a_spec, b_spec],
                     out_specs=[pl.BlockSpec((tm, tn), lambda i, j, kk: (i, j))], out_shape=[SDS((m, n), out_dtype)],
                     scratch=[pltpu.VMEM((tm, tn), F32)], operands=(a, b), name=name, rider=rider)
    return res[0] if rider is None else res


def _row_spec(entry, tile):
    arr, width, cb = entry if isinstance(entry, tuple) else (entry, entry.shape[1], 0)
    return arr, pl.BlockSpec((tile, width), lambda i, cb=cb: (i, cb))


def _par_spec(p):
    return pl.BlockSpec(p.shape, lambda i: (0, 0))


def row_fwd(name, fn, rows, params, outs, tile=256):
    arrs, specs = zip(*[_row_spec(e, tile) for e in rows])
    s = arrs[0].shape[0]
    nr, npar = len(rows), len(params)

    def body(*refs):
        r = [x[...].astype(F32) for x in refs[:nr]]
        p = [x[...] for x in refs[nr:nr + npar]]
        res = fn(*r, *p)
        for o_ref, val in zip(refs[nr + npar:], res):
            o_ref[...] = val.astype(o_ref.dtype)

    return pl.pallas_call(
        body, grid=(s // tile,), in_specs=list(specs) + [_par_spec(p) for p in params],
        out_specs=[pl.BlockSpec((tile, w), lambda i: (i, 0)) for w, _ in outs],
        out_shape=[SDS((s, w), dt) for w, dt in outs],
        compiler_params=_cparams(("parallel",)), name=name)(*arrs, *params)


def row_bwd(name, fn, rows, params, cots, row_grads, tile=256, primal_sum=False):
    arrs, specs = zip(*[_row_spec(e, tile) for e in rows])
    s = arrs[0].shape[0]
    nr, npar = len(rows), len(params)
    cot_entries = [e for c in cots if c is not None for e in c]
    carrs, cspecs = zip(*[_row_spec(e, tile) for e in cot_entries]) if cot_entries else ((), ())
    nc = len(cot_entries)
    want = [i for i, d in enumerate(row_grads) if d is not None]

    def body(*refs):
        r = [x[...].astype(F32) for x in refs[:nr]]
        p = [x[...] for x in refs[nr:nr + npar]]
        cvals = [x[...].astype(F32) for x in refs[nr + npar:nr + npar + nc]]
        outs = refs[nr + npar + nc:]
        prim, vjp = jax.vjp(fn, *r, *p)
        ct, pos = [], 0
        for c, pr in zip(cots, prim):
            if c is None:
                ct.append(jnp.ones_like(pr))
            else:
                acc = cvals[pos]
                for extra in cvals[pos + 1:pos + len(c)]:
                    acc = acc + extra
                pos += len(c)
                ct.append(acc)
        grads = vjp(tuple(ct))
        for o_ref, i in zip(outs[:len(want)], want):
            o_ref[...] = grads[i].astype(o_ref.dtype)
        acc_refs = outs[len(want):]
        vals = [grads[nr + j] for j in range(npar)]
        if primal_sum:
            vals.append(jnp.sum(prim[0], axis=0, keepdims=True))
        first = pl.program_id(0) == 0

        @pl.when(first)
        def _():
            for a_ref, v in zip(acc_refs, vals):
                a_ref[...] = v

        @pl.when(jnp.logical_not(first))
        def _():
            for a_ref, v in zip(acc_refs, vals):
                a_ref[...] += v

    widths = [(e[1] if isinstance(e, tuple) else e.shape[1]) for e in rows]
    out_specs = [pl.BlockSpec((tile, widths[i]), lambda i_: (i_, 0)) for i in want]
    out_shape = [SDS((s, widths[i]), row_grads[i]) for i in want]
    pshapes = [p.shape for p in params]
    if primal_sum:
        pshapes.append((1, widths[0]))
    out_specs += [pl.BlockSpec(sh, lambda i_: (0, 0)) for sh in pshapes]
    out_shape += [SDS(sh, F32) for sh in pshapes]
    res = pl.pallas_call(
        body, grid=(s // tile,), in_specs=list(specs) + [_par_spec(p) for p in params] + list(cspecs),
        out_specs=out_specs, out_shape=out_shape,
        compiler_params=_cparams(("arbitrary",)), name=name)(*arrs, *params, *carrs)
    return res[:len(want)], res[len(want):]


def _rms(x, w):
    return x * lax.rsqrt(jnp.mean(x * x, axis=-1, keepdims=True) + RMS_EPS) * w


def _sigmoid(x):
    return jax.nn.sigmoid(x)


def _softplus(x):
    return jnp.maximum(x, 0.0) + jnp.log1p(jnp.exp(-jnp.abs(x)))


def f_norm1(x, w):
    return (_rms(x, w),)


def f_dt(raw, bias):
    return (_softplus(raw + bias),)


def f_merge(gp_ssd, gp_sb, yb_ssd, yb_sb, b_ssd, b_sb):
    return (_sigmoid(gp_ssd + b_ssd) * yb_ssd + _sigmoid(gp_sb + b_sb) * yb_sb,)


def f_mix_out(x, mo, w_post, w_pre):
    h1 = x + _rms(mo, w_post)
    return h1, _rms(h1, w_pre)


def f_relu2(a1):
    return (jnp.square(jnp.maximum(a1, 0.0)),)


def f_ffn_out(h1, ff, w):
    return (h1 + _rms(ff, w),)


def f_ple_loss(h2, pg, pe, tgt, w):
    h3 = h2 + _rms(_sigmoid(pg) * pe, w)
    return (0.5 * jnp.square(h3 - tgt) * (1.0 / h2.shape[-1]),)


def _shift_down(u, d, rows):
    return u if d == 0 else jnp.where(rows >= d, pltpu.roll(u, d, 0), 0.0)


def _shift_up(u, d, rows):
    s = u.shape[0]
    return u if d == 0 else jnp.where(rows < s - d, pltpu.roll(u, s - d, 0), 0.0)


def conv_fwd(proj, col0, cd, conv_w, conv_b, name):
    s = proj.shape[0]
    cb0 = col0 // CONV_COLS

    def body(u_ref, w_ref, b_ref, o_ref):
        u = u_ref[...]
        rows = lax.broadcasted_iota(jnp.int32, u.shape, 0)
        y = jnp.broadcast_to(b_ref[...], u.shape)
        for k in range(CONV_K):
            y = y + w_ref[k:k + 1, :] * _shift_down(u, CONV_K - 1 - k, rows)
        o_ref[...] = y * _sigmoid(y)

    return pl.pallas_call(
        body, grid=(cd // CONV_COLS,),
        in_specs=[pl.BlockSpec((s, CONV_COLS), lambda i: (0, cb0 + i)),
                  pl.BlockSpec((CONV_K, CONV_COLS), lambda i: (0, i)),
                  pl.BlockSpec((1, CONV_COLS), lambda i: (0, i))],
        out_specs=pl.BlockSpec((s, CONV_COLS), lambda i: (0, i)),
        out_shape=SDS((s, cd), F32), compiler_params=_cparams(("parallel",)), name=name)(proj, conv_w, conv_b)


def conv_bwd(proj, col0, dout, ch0, conv_w, conv_b, name):
    s = proj.shape[0]
    ncb = dout.shape[1] // CONV_COLS
    cb0 = (col0 + ch0) // CONV_COLS
    wb0 = ch0 // CONV_COLS

    def body(u_ref, g_ref, w_ref, b_ref, du_ref, dw_ref, db_ref):
        u = u_ref[...]
        rows = lax.broadcasted_iota(jnp.int32, u.shape, 0)
        y = jnp.broadcast_to(b_ref[...], u.shape)
        for k in range(CONV_K):
            y = y + w_ref[k:k + 1, :] * _shift_down(u, CONV_K - 1 - k, rows)
        sg = _sigmoid(y)
        dy = g_ref[...] * (sg * (1.0 + y * (1.0 - sg)))
        du = jnp.zeros_like(u)
        for k in range(CONV_K):
            d = CONV_K - 1 - k
            du = du + w_ref[k:k + 1, :] * _shift_up(dy, d, rows)
            dw_ref[k:k + 1, :] = jnp.sum(dy * _shift_down(u, d, rows), axis=0, keepdims=True)
        du_ref[...] = du.astype(du_ref.dtype)
        db_ref[...] = jnp.sum(dy, axis=0, keepdims=True)

    return pl.pallas_call(
        body, grid=(ncb,),
        in_specs=[pl.BlockSpec((s, CONV_COLS), lambda i: (0, cb0 + i)),
                  pl.BlockSpec((s, CONV_COLS), lambda i: (0, i)),
                  pl.BlockSpec((CONV_K, CONV_COLS), lambda i: (0, wb0 + i)),
                  pl.BlockSpec((1, CONV_COLS), lambda i: (0, wb0 + i))],
        out_specs=[pl.BlockSpec((s, CONV_COLS), lambda i: (0, i)),
                   pl.BlockSpec((CONV_K, CONV_COLS), lambda i: (0, i)),
                   pl.BlockSpec((1, CONV_COLS), lambda i: (0, i))],
        out_shape=[SDS((s, ncb * CONV_COLS), BF16), SDS((CONV_K, ncb * CONV_COLS), F32), SDS((1, ncb * CONV_COLS), F32)],
        compiler_params=_cparams(("parallel",)), name=name)(proj, dout, conv_w, conv_b)


def _dot(a, b, dims):
    return lax.dot_general(a.astype(BF16), b.astype(BF16), (dims, ((), ())), preferred_element_type=F32)


NN = ((1,), (0,))
NT = ((1,), (1,))
TN = ((0,), (0,))


def ssd_chunk(xs, zs, nw, dtc, dtw, alogs, dsks, bm, cm, prev):
    ln = bm.shape[0]
    gw = GROUP_HEADS * HEAD_DIM
    row = lax.broadcasted_iota(jnp.int32, (ln, ln), 0)
    col = lax.broadcasted_iota(jnp.int32, (ln, ln), 1)
    causal = row >= col
    tri = causal.astype(F32)
    tri_t = (row <= col).astype(F32)
    lane_head = lax.broadcasted_iota(jnp.int32, (1, gw), 1) // HEAD_DIM
    sub_head = lax.broadcasted_iota(jnp.int32, (gw, 1), 0) // HEAD_DIM
    on_lanes = [(lane_head == r).astype(F32) for r in range(GROUP_HEADS)]
    on_rows = [(sub_head == r).astype(F32) for r in range(GROUP_HEADS)]
    cb = _dot(cm, bm, NT)
    decays, dt_full, acs_full, end_full, dsk_full, end_rows = [], 0.0, 0.0, 0.0, 0.0, 0.0
    for r in range(GROUP_HEADS):
        a = -jnp.exp(alogs[r])
        da_c = dtc[r] * a
        da_w = dtw[r] * a
        acs_c = jnp.sum(tri * da_w, axis=1, keepdims=True)
        acs_w = jnp.sum(tri_t * da_c, axis=0, keepdims=True)
        alast = jnp.sum(da_w, axis=1, keepdims=True)
        decays.append(jnp.exp(jnp.where(causal, acs_c - acs_w, -jnp.inf)))
        dt_full = dt_full + dtc[r] * on_lanes[r]
        acs_full = acs_full + acs_c * on_lanes[r]
        end_full = end_full + alast * on_lanes[r]
        dsk_full = dsk_full + dsks[r] * on_lanes[r]
        end_rows = end_rows + alast * on_rows[r]
    xd = xs * dt_full
    y = xs * dsk_full + _dot(cm, prev, NT) * jnp.exp(acs_full)
    for r in range(GROUP_HEADS):
        y = y + _dot(cb * decays[r], xd * on_lanes[r], NN)
    new_prev = prev * jnp.exp(end_rows) + _dot(xd * jnp.exp(end_full - acs_full), bm, TN)
    yg = y * (zs * _sigmoid(zs))
    rstd = lax.rsqrt(jnp.mean(yg * yg, axis=-1, keepdims=True) + RMS_EPS)
    return yg * rstd * nw, new_prev


def _ssd_specs(dm, cidx):
    gw = GROUP_HEADS * HEAD_DIM
    nb0 = dm.DI // D_STATE
    par = pl.BlockSpec((None, 1, GROUP_HEADS), lambda g, c: (g, 0, 0))
    return dict(
        z=pl.BlockSpec((CHUNK, gw), lambda g, c: (cidx(c), g)),
        xs=pl.BlockSpec((CHUNK, gw), lambda g, c: (cidx(c), g)),
        b=pl.BlockSpec((CHUNK, D_STATE), lambda g, c: (cidx(c), nb0 + g)),
        c=pl.BlockSpec((CHUNK, D_STATE), lambda g, c: (cidx(c), nb0 + dm.G + g)),
        dtc=pl.BlockSpec((None, CHUNK, GROUP_HEADS), lambda g, c: (g, cidx(c), 0)),
        dtw=pl.BlockSpec((None, GROUP_HEADS, CHUNK), lambda g, c: (g, 0, cidx(c))),
        par=par,
        nw=pl.BlockSpec((1, gw), lambda g, c: (0, g)),
        st=pl.BlockSpec((None, None, gw, D_STATE), lambda g, c: (g, cidx(c), 0, 0)),
    )


def _ssd_load(z_ref, xs_ref, b_ref, c_ref, dtc_ref, dtw_ref, alog_ref, dsk_ref, nw_ref):
    dtc = tuple(dtc_ref[:, r:r + 1] for r in range(GROUP_HEADS))
    dtw = tuple(dtw_ref[r:r + 1, :] for r in range(GROUP_HEADS))
    alogs = tuple(alog_ref[:, r:r + 1] for r in range(GROUP_HEADS))
    dsks = tuple(dsk_ref[:, r:r + 1] for r in range(GROUP_HEADS))
    return xs_ref[...], z_ref[...], nw_ref[...], dtc, dtw, alogs, dsks, b_ref[...], c_ref[...]


def ssd_fwd(proj, xbc, dtc, dtw, alog, dsk, nw, dm, name):
    nc = dm.S // CHUNK
    gw = GROUP_HEADS * HEAD_DIM
    sp = _ssd_specs(dm, lambda c: c)

    def body(z_ref, xs_ref, b_ref, c_ref, dtc_ref, dtw_ref, alog_ref, dsk_ref, nw_ref, y_ref, st_ref, prev):
        @pl.when(pl.program_id(1) == 0)
        def _():
            prev[...] = jnp.zeros_like(prev)

        args = _ssd_load(z_ref, xs_ref, b_ref, c_ref, dtc_ref, dtw_ref, alog_ref, dsk_ref, nw_ref)
        st_ref[...] = prev[...]
        out, new = ssd_chunk(*args, prev[...])
        y_ref[...] = out.astype(y_ref.dtype)
        prev[...] = new

    return pl.pallas_call(
        body, grid=(dm.G, nc),
        in_specs=[sp["z"], sp["xs"], sp["b"], sp["c"], sp["dtc"], sp["dtw"], sp["par"], sp["par"], sp["nw"]],
        out_specs=[sp["xs"], sp["st"]],
        out_shape=[SDS((dm.S, dm.DI), BF16), SDS((dm.G, nc, gw, D_STATE), F32)],
        scratch_shapes=[pltpu.VMEM((gw, D_STATE), F32)],
        compiler_params=_cparams(("parallel", "arbitrary")), name=name)(proj, xbc, xbc, xbc, dtc, dtw, alog, dsk, nw)


def ssd_bwd(proj, xbc, dtc, dtw, alog, dsk, nw, states, dy, dm, name):
    nc = dm.S // CHUNK
    sp = _ssd_specs(dm, lambda c: nc - 1 - c)
    gw = GROUP_HEADS * HEAD_DIM
    bc_spec = pl.BlockSpec((CHUNK, D_STATE), lambda g, c: (nc - 1 - c, g))

    def body(z_ref, xs_ref, b_ref, c_ref, dtc_ref, dtw_ref, alog_ref, dsk_ref, nw_ref, st_ref, dy_ref,
             dz_ref, dxs_ref, db_ref, dc_ref, ddtc_ref, ddtw_ref, dalog_ref, ddsk_ref, dnw_ref, dprev):
        first = pl.program_id(1) == 0

        @pl.when(first)
        def _():
            dprev[...] = jnp.zeros_like(dprev)

        args = _ssd_load(z_ref, xs_ref, b_ref, c_ref, dtc_ref, dtw_ref, alog_ref, dsk_ref, nw_ref)
        _, vjp = jax.vjp(ssd_chunk, *args, st_ref[...])
        gxs, gzs, gnw, gdtc, gdtw, galogs, gdsks, gb, gc, gprev = vjp((dy_ref[...], dprev[...]))
        dxs_ref[...] = gxs
        dz_ref[...] = gzs.astype(dz_ref.dtype)
        db_ref[...] = gb
        dc_ref[...] = gc
        dprev[...] = gprev
        for r in range(GROUP_HEADS):
            ddtc_ref[:, r:r + 1] = gdtc[r]
            ddtw_ref[r:r + 1, :] = gdtw[r]

        @pl.when(first)
        def _():
            dnw_ref[...] = gnw
            for r in range(GROUP_HEADS):
                dalog_ref[:, r:r + 1] = galogs[r]
                ddsk_ref[:, r:r + 1] = gdsks[r]

        @pl.when(jnp.logical_not(first))
        def _():
            dnw_ref[...] += gnw
            for r in range(GROUP_HEADS):
                dalog_ref[:, r:r + 1] += galogs[r]
                ddsk_ref[:, r:r + 1] += gdsks[r]

    xs_out = pl.BlockSpec((CHUNK, gw), lambda g, c: (nc - 1 - c, g))
    return pl.pallas_call(
        body, grid=(dm.G, nc),
        in_specs=[sp["z"], sp["xs"], sp["b"], sp["c"], sp["dtc"], sp["dtw"], sp["par"], sp["par"], sp["nw"],
                  sp["st"], xs_out],
        out_specs=[xs_out, xs_out, bc_spec, bc_spec, sp["dtc"], sp["dtw"], sp["par"], sp["par"], sp["nw"]],
        out_shape=[SDS((dm.S, dm.DI), BF16), SDS((dm.S, dm.DI), F32), SDS((dm.S, dm.G * D_STATE), F32),
                   SDS((dm.S, dm.G * D_STATE), F32), SDS((dm.G, dm.S, GROUP_HEADS), F32), SDS((dm.G, GROUP_HEADS, dm.S), F32),
                   SDS((dm.G, 1, GROUP_HEADS), F32), SDS((dm.G, 1, GROUP_HEADS), F32), SDS((1, dm.DI), F32)],
        scratch_shapes=[pltpu.VMEM((gw, D_STATE), F32)],
        compiler_params=_cparams(("parallel", "arbitrary")), name=name)(
            proj, xbc, xbc, xbc, dtc, dtw, alog, dsk, nw, states, dy)


def _split_bf16(v):
    hi = v.astype(BF16)
    return hi, (v - hi.astype(F32)).astype(BF16)


def _tri(v, mat):
    hi, lo = _split_bf16(v)
    return jnp.dot(hi, mat, preferred_element_type=F32) + jnp.dot(lo, mat, preferred_element_type=F32)


def _blocks(v):
    return [v[:, b * ATT_TILE:(b + 1) * ATT_TILE] for b in range(v.shape[1] // ATT_TILE)]


def _sb_group(z, mask, run, after_mat):
    sp = jnp.maximum(z, 0.0) + jnp.log2(1.0 + jnp.exp2(-jnp.abs(z)))
    lk = -sp if mask is None else jnp.where(mask, -sp, 0.0)
    cums = [_tri(v, after_mat) for v in _blocks(lk)]
    sums = [jnp.sum(v, axis=1, keepdims=True) for v in _blocks(lk)]
    later = [None] * len(cums)
    for b in reversed(range(len(cums))):
        later[b] = run + cums[b]
        run = run + sums[b]
    ls = z - sp
    w = jnp.exp2(ls + jnp.concatenate(later, axis=1))
    if mask is not None:
        w = jnp.where(mask, w, 0.0)
    return ls, w, run


def _alive(run_a, run_b):
    return (jnp.max(jnp.maximum(run_a, run_b)) > ATT_DEAD).astype(jnp.int32)


def _group_mask(i, g, t, gw):
    rows = i * t + lax.broadcasted_iota(jnp.int32, (t, gw), 0)
    cols = g * gw + lax.broadcasted_iota(jnp.int32, (t, gw), 1)
    return cols < rows


def _att_specs(dm, s):
    t = ATT_TILE
    qb, kb, vb = dm.off["q"] // LANES, dm.off["k"] // LANES, dm.off["v"] // LANES
    return (pl.BlockSpec((t, LANES), lambda p, i: (i, qb + p)),
            pl.BlockSpec((s, LANES), lambda p, i: (0, kb + p)),
            pl.BlockSpec((s, LANES), lambda p, i: (0, vb + p)))


def attn_fwd(proj, dm, name, rider=None):
    s, t = dm.S, ATT_TILE
    scale = HEAD_DIM ** -0.5
    hsl = [slice(h * HEAD_DIM, (h + 1) * HEAD_DIM) for h in range(2)]

    gw = ATT_GROUP * t

    def body(q_ref, k_ref, v_ref, o_ref):
        i = pl.program_id(1)
        gd = i // ATT_GROUP
        r_io = lax.broadcasted_iota(jnp.int32, (t, t), 0)
        c_io = lax.broadcasted_iota(jnp.int32, (t, t), 1)
        after_mat = (r_io > c_io).astype(BF16)
        qs = [q_ref[:, sl].astype(BF16) for sl in hsl]

        def group(g, carry, mask):
            r0 = pl.multiple_of(g * gw, gw)
            zs = [_dot(qs[h], k_ref[pl.ds(r0, gw), hsl[h]], NT) * (scale * LOG2E) for h in range(2)]
            res = [_sb_group(zs[h], mask, carry[h][0], after_mat) for h in range(2)]
            return tuple((res[h][2], carry[h][1] + _dot(res[h][1], v_ref[pl.ds(r0, gw), hsl[h]], NN)) for h in range(2))

        zero = (jnp.zeros((t, 1), F32), jnp.zeros((t, HEAD_DIM), F32))
        carry = group(gd, (zero, zero), _group_mask(i, gd, t, gw))

        def step(st):
            jj, _, c = st
            c = group(gd - jj, c, None)
            return jj + 1, _alive(c[0][0], c[1][0]), c

        _, _, carry = lax.while_loop(lambda st: jnp.logical_and(st[0] <= gd, st[1] > 0), step,
                                     (jnp.int32(1), _alive(carry[0][0], carry[1][0]), carry))
        for h in range(2):
            o_ref[:, hsl[h]] = carry[h][1]

    qs_, ks_, vs_ = _att_specs(dm, s)
    res = _grid_call(body, grid=(dm.SBW // LANES, s // t), in_specs=[qs_, ks_, vs_],
                     out_specs=[pl.BlockSpec((t, LANES), lambda p, i: (i, p))], out_shape=[SDS((s, dm.SBW), F32)],
                     scratch=[], operands=(proj, proj, proj), name=name, rider=rider)
    return res[0] if rider is None else res


def attn_bwd(proj, do, dm, name, rider=None):
    s, t = dm.S, ATT_TILE
    nq = s // t
    gw = ATT_GROUP * t
    scale = HEAD_DIM ** -0.5
    hsl = [slice(h * HEAD_DIM, (h + 1) * HEAD_DIM) for h in range(2)]

    def body(q_ref, k_ref, v_ref, do_ref, dq_ref, dk_ref, dv_ref, dk_acc, dv_acc, g_scr, s_scr):
        i = pl.program_id(1)

        @pl.when(i == 0)
        def _():
            dk_acc[...] = jnp.zeros_like(dk_acc)
            dv_acc[...] = jnp.zeros_like(dv_acc)

        gd = i // ATT_GROUP
        r_io = lax.broadcasted_iota(jnp.int32, (t, t), 0)
        c_io = lax.broadcasted_iota(jnp.int32, (t, t), 1)
        after_mat = (r_io > c_io).astype(BF16)
        before_mat = (r_io < c_io).astype(BF16)
        qs = [q_ref[:, sl].astype(BF16) for sl in hsl]
        dos = [do_ref[:, sl].astype(BF16) for sl in hsl]
        q_t = q_ref[...].T.astype(BF16)
        do_t = do_ref[...].T.astype(BF16)

        def pass1(g, runs, mask):
            r0 = pl.multiple_of(g * gw, gw)
            zs = [_dot(qs[h], k_ref[pl.ds(r0, gw), hsl[h]], NT) * (scale * LOG2E) for h in range(2)]
            dws = [_dot(dos[h], v_ref[pl.ds(r0, gw), hsl[h]], NT) for h in range(2)]
            out = []
            for h in range(2):
                ls, w, run = _sb_group(zs[h], mask, runs[h], after_mat)
                g_scr[h, g] = dws[h] * w
                s_scr[h, g] = jnp.exp2(ls)
                dv_acc[g, hsl[h], :] += _dot(do_t[hsl[h]], w, NN)
                out.append(run)
            return tuple(out)

        diag_mask = _group_mask(i, gd, t, gw)
        zero_col = jnp.zeros((t, 1), F32)
        runs = pass1(gd, (zero_col, zero_col), diag_mask)

        def step1(st):
            jj, _, r = st
            r = pass1(gd - jj, r, None)
            return jj + 1, _alive(r[0], r[1]), r

        walked, _, _ = lax.while_loop(lambda st: jnp.logical_and(st[0] <= gd, st[1] > 0), step1,
                                      (jnp.int32(1), _alive(runs[0], runs[1]), runs))
        g_first = gd - (walked - 1)

        def pass2(g, carry, mask):
            r0 = pl.multiple_of(g * gw, gw)
            out = []
            for h in range(2):
                pre, dq = carry[h]
                gg = g_scr[h, g]
                sig = s_scr[h, g]
                before = []
                for v in _blocks(gg):
                    before.append(pre + _tri(v, before_mat))
                    pre = pre + jnp.sum(v, axis=1, keepdims=True)
                dz = gg * (1.0 - sig) - jnp.concatenate(before, axis=1) * sig
                if mask is not None:
                    dz = jnp.where(mask, dz, 0.0)
                dz = (dz * scale).astype(BF16)
                dk_acc[g, hsl[h], :] += _dot(q_t[hsl[h]], dz, NN)
                out.append((pre, dq + _dot(dz, k_ref[pl.ds(r0, gw), hsl[h]], NN)))
            return tuple(out)

        zero = (zero_col, jnp.zeros((t, HEAD_DIM), F32))
        carry = lax.fori_loop(g_first, gd, lambda g, c: pass2(g, c, None), (zero, zero))
        carry = pass2(gd, carry, diag_mask)
        for h in range(2):
            dq_ref[:, hsl[h]] = carry[h][1].astype(dq_ref.dtype)

        @pl.when(i == nq - 1)
        def _():
            for g in range(s // gw):
                dk_ref[g * gw:(g + 1) * gw, :] = dk_acc[g].T.astype(dk_ref.dtype)
                dv_ref[g * gw:(g + 1) * gw, :] = dv_acc[g].T.astype(dv_ref.dtype)

    qs_, ks_, vs_ = _att_specs(dm, s)
    tile_spec = pl.BlockSpec((t, LANES), lambda p, i: (i, p))
    full_spec = pl.BlockSpec((s, LANES), lambda p, i: (0, p))
    return _grid_call(
        body, grid=(dm.SBW // LANES, nq), in_specs=[qs_, ks_, vs_, tile_spec],
        out_specs=[tile_spec, full_spec, full_spec], out_shape=[SDS((s, dm.SBW), BF16)] * 3,
        scratch=[pltpu.VMEM((s // gw, LANES, gw), F32), pltpu.VMEM((s // gw, LANES, gw), F32),
                 pltpu.VMEM((2, s // gw, t, gw), F32), pltpu.VMEM((2, s // gw, t, gw), F32)],
        operands=(proj, proj, proj, do), name=name, rider=rider)


def adamw(w, g, m, v, name):
    rows, width = w.shape
    tile = _pick(rows, (256, 64, 16, 8, 4))
    c1 = 1.0 / (1.0 - ADAM_B1 ** ADAM_STEP)
    c2 = 1.0 / (1.0 - ADAM_B2 ** ADAM_STEP)

    def body(w_ref, g_ref, m_ref, v_ref, d_ref, nm_ref, nv_ref):
        gg = g_ref[...]
        nm = ADAM_B1 * m_ref[...] + (1.0 - ADAM_B1) * gg
        nv = ADAM_B2 * v_ref[...] + (1.0 - ADAM_B2) * (gg * gg)
        d_ref[...] = -ADAM_LR * ((nm * c1) / (jnp.sqrt(nv * c2) + ADAM_EPS) + ADAM_WD * w_ref[...])
        nm_ref[...] = nm
        nv_ref[...] = nv

    spec = pl.BlockSpec((tile, width), lambda i: (i, 0))
    return pl.pallas_call(
        body, grid=(rows // tile,), in_specs=[spec] * 4, out_specs=[spec] * 3,
        out_shape=[SDS((rows, width), F32)] * 3, compiler_params=_cparams(("parallel",)), name=name)(w, g, m, v)


def _me():
    return lax.axis_index("x"), lax.axis_index("y"), lax.axis_index("c")


def _other_chips(x, y):
    return [(1 - x, y), (x, 1 - y), (1 - x, 1 - y)]


def gather_weights(wp):
    rows, width = wp.shape
    half = rows // 2

    def body(w_ref, out_ref, send_sems, recv_sems):
        x, y, c = _me()
        sibling = (x, y, 1 - c)
        chips = _other_chips(x, y)

        def part(cx, cy, hf):
            return out_ref.at[2 * cx + cy, hf]

        def copy(k, src, dst, to):
            return pltpu.make_async_remote_copy(src_ref=src, dst_ref=dst, send_sem=send_sems.at[k], recv_sem=recv_sems.at[k],
                                                device_id=to, device_id_type=MESH_ID)

        first = [copy(j, w_ref.at[c], part(x, y, c), (cx, cy, c)) for j, (cx, cy) in enumerate(chips)]
        for cp in first:
            cp.start()
        passed = [copy(3 + j, part(cx, cy, c), part(cx, cy, c), sibling) for j, (cx, cy) in enumerate(chips)]
        for j, (cx, cy) in enumerate(chips):
            copy(j, part(cx, cy, c), part(cx, cy, c), (x, y, c)).wait_recv()
            passed[j].start()
        for j, (cx, cy) in enumerate(chips):
            copy(3 + j, part(cx, cy, 1 - c), part(cx, cy, 1 - c), (x, y, c)).wait_recv()
        for cp in first + passed:
            cp.wait_send()

    return pl.pallas_call(
        body, out_shape=SDS((N_CHIPS, 2, half, width), wp.dtype), in_specs=[HBM_SPEC], out_specs=HBM_SPEC,
        scratch_shapes=[pltpu.SemaphoreType.DMA((6,)), pltpu.SemaphoreType.DMA((6,))],
        name="gather_weights")(wp.reshape(2, half, width)).reshape(N_CHIPS, rows, width)


class Rider(NamedTuple):
    operands: tuple
    out_shape: tuple
    scratch: tuple
    start: object
    wait: object


def _with_rider(body, rider, grid, n_in, n_out):
    if rider is None:
        return body
    r_in, r_out = len(rider.operands), len(rider.out_shape)

    def full(*refs):
        ins, refs = refs[:n_in], refs[n_in:]
        rins, refs = refs[:r_in], refs[r_in:]
        outs, refs = refs[:n_out], refs[n_out:]
        routs, refs = refs[:r_out], refs[r_out:]
        scr, rscr = refs[:len(refs) - len(rider.scratch)], refs[len(refs) - len(rider.scratch):]
        ids = [pl.program_id(k) for k in range(len(grid))]
        first = functools.reduce(jnp.logical_and, [i == 0 for i in ids])
        last = functools.reduce(jnp.logical_and, [i == g - 1 for i, g in zip(ids, grid)])

        @pl.when(first)
        def _():
            rider.start(rins, routs, rscr)

        body(*ins, *outs, *scr)

        @pl.when(last)
        def _():
            rider.wait(rins, routs, rscr)

    return full


def gather_rider(wp):
    def copies(ins, outs, scr, sending):
        (w_ref,), (o_ref,), (send_sems, recv_sems) = ins, outs, scr
        x, y, c = _me()
        return [pltpu.make_async_remote_copy(src_ref=w_ref, dst_ref=o_ref.at[2 * x + y if sending else 2 * cx + cy],
                                             send_sem=send_sems.at[j], recv_sem=recv_sems.at[j], device_id=(cx, cy, c),
                                             device_id_type=MESH_ID)
                for j, (cx, cy) in enumerate(_other_chips(x, y))]

    def start(ins, outs, scr):
        for cp in copies(ins, outs, scr, True):
            cp.start()

    def wait(ins, outs, scr):
        for cp in copies(ins, outs, scr, False):
            cp.wait()

    return Rider((wp,), (SDS((N_CHIPS,) + wp.shape, wp.dtype),),
                 (pltpu.SemaphoreType.DMA((3,)), pltpu.SemaphoreType.DMA((3,))), start, wait)


def own_slab(gathered, wp, chip_idx):
    return lax.dynamic_update_slice(gathered, wp[None], (chip_idx[0], 0, 0))


def exchange_rider(sh):
    def copies(ins, outs, scr):
        (s_ref,), (b_ref,), (send_sems, recv_sems) = ins, outs, scr
        x, y, c = _me()
        return [pltpu.make_async_remote_copy(src_ref=s_ref.at[2 * cx + cy], dst_ref=b_ref.at[j], send_sem=send_sems.at[j],
                                             recv_sem=recv_sems.at[j], device_id=(cx, cy, c), device_id_type=MESH_ID)
                for j, (cx, cy) in enumerate(_other_chips(x, y))]

    def start(ins, outs, scr):
        for cp in copies(ins, outs, scr):
            cp.start()

    def wait(ins, outs, scr):
        for cp in copies(ins, outs, scr):
            cp.wait()

    return Rider((sh,), (SDS((3,) + sh.shape[1:], sh.dtype),),
                 (pltpu.SemaphoreType.DMA((3,)), pltpu.SemaphoreType.DMA((3,))), start, wait)


def swap_halves(g, name):
    n, rows, width = g.shape
    half = rows // 2

    def body(g_ref, a_ref, send_sem, recv_sem):
        x, y, c = _me()
        cp = pltpu.make_async_remote_copy(src_ref=g_ref.at[:, 1 - c], dst_ref=a_ref,
                                          send_sem=send_sem, recv_sem=recv_sem, device_id=(x, y, 1 - c), device_id_type=MESH_ID)
        cp.start()
        cp.wait()

    return pl.pallas_call(
        body, out_shape=SDS((n, half, width), g.dtype), in_specs=[HBM_SPEC], out_specs=HBM_SPEC,
        scratch_shapes=[pltpu.SemaphoreType.DMA, pltpu.SemaphoreType.DMA], name=name)(g.reshape(n, 2, half, width))


def add_half(g, a, c_idx, name):
    n, rows, width = g.shape
    half = rows // 2
    tile = half // 2
    nt = half // tile

    def body(c_ref, g_ref, a_ref, o_ref, ob_ref):
        v = g_ref[...] + a_ref[...]
        o_ref[...] = v
        ob_ref[...] = v.astype(ob_ref.dtype)

    out_spec = pl.BlockSpec((None, tile, width), lambda s, i, c_ref: (s, i, 0))
    gs = pltpu.PrefetchScalarGridSpec(
        num_scalar_prefetch=1, grid=(n, nt),
        in_specs=[pl.BlockSpec((None, tile, width), lambda s, i, c_ref: (s, c_ref[0] * nt + i, 0)), out_spec],
        out_specs=[out_spec, out_spec])
    return pl.pallas_call(body, grid_spec=gs, out_shape=[SDS((n, half, width), F32), SDS((n, half, width), BF16)],
                          compiler_params=_cparams(("parallel", "parallel")), name=name)(c_idx, g, a)


def exchange_small(small):
    def body(sm_ref, all_ref, send_sems, recv_sems, local_sem):
        x, y, c = _me()
        mine = pltpu.make_async_copy(sm_ref, all_ref.at[0], local_sem)
        mine.start()
        copies = []
        for m in range(1, N_DEV):
            peer = (x ^ ((m >> 2) & 1), y ^ ((m >> 1) & 1), c ^ (m & 1))
            copies.append(pltpu.make_async_remote_copy(
                src_ref=sm_ref, dst_ref=all_ref.at[m], send_sem=send_sems.at[m - 1], recv_sem=recv_sems.at[m - 1],
                device_id=peer, device_id_type=MESH_ID))
        for cp in copies:
            cp.start()
        for cp in copies:
            cp.wait()
        mine.wait()

    return pl.pallas_call(
        body, out_shape=SDS((N_DEV,) + small.shape, small.dtype), in_specs=[HBM_SPEC], out_specs=HBM_SPEC,
        scratch_shapes=[pltpu.SemaphoreType.DMA((N_DEV - 1,)), pltpu.SemaphoreType.DMA((N_DEV - 1,)), pltpu.SemaphoreType.DMA],
        name="exchange_small")(small)


def add_chips(sh, b, k_idx, name):
    n, hf, width = sh.shape
    tile = hf // 2

    def body(k_ref, s_ref, b0, b1, b2, o_ref):
        o_ref[...] = ((s_ref[...] + b0[...].astype(F32)) + b1[...].astype(F32)) + b2[...].astype(F32)

    def bspec(j):
        return pl.BlockSpec((None, tile, width), lambda i, k_ref, j=j: (j, i, 0))

    gs = pltpu.PrefetchScalarGridSpec(
        num_scalar_prefetch=1, grid=(hf // tile,),
        in_specs=[pl.BlockSpec((None, tile, width), lambda i, k_ref: (k_ref[0], i, 0)), bspec(0), bspec(1), bspec(2)],
        out_specs=pl.BlockSpec((tile, width), lambda i, k_ref: (i, 0)))
    return pl.pallas_call(body, grid_spec=gs, out_shape=SDS((hf, width), sh.dtype),
                          compiler_params=_cparams(("parallel",)), name=name)(k_idx, sh, b, b, b)


def sum_small(allsm, me_idx, name):
    _, rows, width = allsm.shape

    def body(me_ref, a_ref, o_ref):
        me = me_ref[0]
        acc = a_ref[me]
        for dev in range(1, N_DEV):
            acc = acc + a_ref[jnp.bitwise_xor(me, dev)]
        o_ref[...] = acc

    gs = pltpu.PrefetchScalarGridSpec(
        num_scalar_prefetch=1, grid=(1,),
        in_specs=[pl.BlockSpec((N_DEV, rows, width), lambda i, me_ref: (0, 0, 0))],
        out_specs=pl.BlockSpec((rows, width), lambda i, me_ref: (0, 0)))
    return pl.pallas_call(body, grid_spec=gs, out_shape=SDS((rows, width), allsm.dtype),
                          compiler_params=_cparams(("arbitrary",)), name=name)(me_idx, allsm)


def join_halves(t, core_idx, name):
    hf, width = t.shape

    def body(t_ref, o_ref, send_sem, recv_sem):
        x, y, c = _me()
        cp = pltpu.make_async_remote_copy(src_ref=t_ref, dst_ref=o_ref, send_sem=send_sem, recv_sem=recv_sem,
                                          device_id=(x, y, 1 - c), device_id_type=MESH_ID)
        cp.start()
        cp.wait()

    theirs = pl.pallas_call(
        body, out_shape=SDS((hf, width), t.dtype), in_specs=[HBM_SPEC], out_specs=HBM_SPEC,
        scratch_shapes=[pltpu.SemaphoreType.DMA, pltpu.SemaphoreType.DMA], name=name)(t)
    return jnp.where(core_idx[0] == 0, jnp.concatenate([t, theirs], axis=0), jnp.concatenate([theirs, t], axis=0))


ROW_PAD = 64


def _pad_rows(rows):
    return -(-rows // ROW_PAD) * ROW_PAD


def _pack_rows(names, shard_shapes, width):
    return _pad_rows(sum((shard_shapes[n][0] * shard_shapes[n][1]) // width for n in names))


def unpack_local(packed, names, shard_shapes, width):
    out, r0 = {}, 0
    for n in names:
        a, b = shard_shapes[n]
        nr = (a * b) // width
        out[n] = packed[r0:r0 + nr].reshape(a, b)
        r0 += nr
    return out


EXACT_IN_GATHER = ("conv_w",)
EXACT_TERMS = 3


def pack_gather(shards, names, width):
    parts = []
    for n in names:
        if n in EXACT_IN_GATHER:
            rest = shards[n].astype(F32)
            for _ in range(EXACT_TERMS):
                term = rest.astype(BF16)
                parts.append(term.reshape(-1, width))
                rest = rest - term.astype(F32)
        else:
            parts.append(shards[n].reshape(-1, width).astype(BF16))
    used = sum(p.shape[0] for p in parts)
    parts.append(jnp.zeros((_pad_rows(used) - used, width), BF16))
    return jnp.concatenate(parts, axis=0)


def unpack_full(gathered, names, shard_shapes, width):
    out, r0 = {}, 0
    for n in names:
        a, b = shard_shapes[n]
        terms = EXACT_TERMS if n in EXACT_IN_GATHER else 1
        nr = (a * b) // width
        pieces = []
        for j in range(N_CHIPS):
            blk = gathered[j, r0:r0 + nr].reshape(a, b)
            for t in range(1, terms):
                blk = blk.astype(F32) + gathered[j, r0 + t * nr:r0 + (t + 1) * nr].reshape(a, b).astype(F32)
            pieces.append(blk)
        out[n] = jnp.concatenate(pieces, axis=SHARD_AXIS[n])
        r0 += terms * nr
    return out


def pack_full(grads, names, shard_shapes, width):
    total_rows = _pack_rows(names, shard_shapes, width)
    slabs = []
    for j in range(N_CHIPS):
        parts = []
        for n in names:
            a, b = shard_shapes[n]
            ax = SHARD_AXIS[n]
            sz = (a, b)[ax]
            piece = lax.slice_in_dim(grads[n], j * sz, (j + 1) * sz, axis=ax)
            parts.append(piece.reshape(-1, width))
        used = sum(p.shape[0] for p in parts)
        parts.append(jnp.zeros((total_rows - used, width), F32))
        slabs.append(jnp.concatenate(parts, axis=0))
    return jnp.stack(slabs, axis=0)


def _small_layout(sizes, width):
    lay, r = {}, 0
    for n in SMALL_WEIGHTS:
        nr = -(-sizes[n] // width)
        lay[n] = (r, nr, sizes[n])
        r += nr
    assert r <= SMALL_ROWS
    return lay


def pack_small(vals, lay, width):
    rows = []
    for n in SMALL_WEIGHTS:
        r, nr, sz = lay[n]
        v = vals[n].reshape(-1).astype(F32)
        rows.append(jnp.pad(v, (0, nr * width - sz)).reshape(nr, width))
    used = sum(r.shape[0] for r in rows)
    rows.append(jnp.zeros((SMALL_ROWS - used, width), F32))
    return jnp.concatenate(rows, axis=0)


def unpack_small(packed, lay):
    return {n: packed[r:r + nr].reshape(-1)[:sz].reshape(1, sz) for n, (r, nr, sz) in lay.items()}


PART_B = ("w_in", "conv_w", "w_gate")
PART_A = tuple(n for n in BIG_WEIGHTS if n not in PART_B)


class Dist(NamedTuple):
    packed_a: object
    shard_shapes: dict
    core_idx: object
    chip_idx: object


def local_step(x, p, tgt, wf, sm, dm, dist=None):
    s, d = dm.S, dm.D
    off = dm.off
    w_in = wf["w_in"]
    c0 = dm.DI + dm.CD
    w_all = jnp.concatenate(
        [w_in[:, :c0], w_in[:, c0 + dm.H:], wf["w_gate"], w_in[:, c0:c0 + dm.H],
         jnp.zeros((d, DT_PAD - dm.H), w_in.dtype)], axis=1).astype(BF16)
    g = dm.G
    per_group = lambda v: v.reshape(g, 1, GROUP_HEADS)
    alog, dsk = per_group(sm["a_log"]), per_group(sm["d_skip"])
    b_gate = sm["b_gate"]
    b_ssd, b_sb = b_gate[:, :d], b_gate[:, d:]
    gcol = off["gate"] // d

    (n1,) = row_fwd("norm1", f_norm1, [x], [sm["norm_mix_pre"]], [(d, BF16)])
    proj = matmul(n1, w_all, name="in_proj")
    xbc = conv_fwd(proj, off["xbc"], dm.CD, wf["conv_w"].astype(F32), sm["conv_b"], "conv_fwd")
    dt_raw = proj[:, off["dt"]:off["dt"] + dm.H]
    (dt,) = row_fwd("dt", f_dt, [dt_raw], [sm["dt_bias"]], [(dm.H, F32)])
    dtc = dt.reshape(s, g, GROUP_HEADS).transpose(1, 0, 2)
    dtw = dt.reshape(s, g, GROUP_HEADS).transpose(1, 2, 0)
    y_ssd, states = ssd_fwd(proj, xbc, dtc, dtw, alog, dsk, sm["ssd_norm"], dm, "ssd_fwd")
    if dist is None:
        y_sb = attn_fwd(proj, dm, "attn_fwd")
    else:
        y_sb, gathered_a = attn_fwd(proj, dm, "attn_fwd", rider=gather_rider(dist.packed_a))
        gathered_a = own_slab(gathered_a, dist.packed_a, dist.chip_idx)
        wf = {**wf, **unpack_full(gathered_a, PART_A, dist.shard_shapes, d)}
    yb_ssd = matmul(y_ssd, wf["w_ssd_branch"], name="ssd_branch")
    yb_sb = matmul(y_sb, wf["w_sb_branch"], name="sb_branch")
    merge_rows = [(proj, d, gcol), (proj, d, gcol + 1), yb_ssd, yb_sb]
    (merged,) = row_fwd("merge", f_merge, merge_rows, [b_ssd, b_sb], [(d, BF16)])
    mo = matmul(merged, wf["w_out"], name="w_out")
    h1, n2 = row_fwd("mix_out", f_mix_out, [x, mo], [sm["norm_mix_post"], sm["norm_ffn_pre"]], [(d, F32), (d, BF16)])
    a1 = matmul(n2, wf["w_ff1"], name="ff1")
    (act,) = row_fwd("relu2", f_relu2, [a1], [], [(dm.DFF, BF16)])
    ff = matmul(act, wf["w_ff2"], name="ff2")
    (h2,) = row_fwd("ffn_out", f_ffn_out, [h1, ff], [sm["norm_ffn_post"]], [(d, F32)])
    pg = matmul(h2, wf["w_ple_gate"], name="ple_gate")
    pe = matmul(p, wf["w_ple"], name="ple_emb")

    def reduce_start(names, tag):
        gbig = pack_full(gr, names, dist.shard_shapes, d)
        from_sibling = swap_halves(gbig, "swap_halves_" + tag)
        pair_sum, pair_sum_bf16 = add_half(gbig, from_sibling, dist.core_idx, "add_half_" + tag)
        return pair_sum, exchange_rider(pair_sum_bf16)

    def reduce_finish(names, pair_sum, from_chips, tag):
        my_half = add_chips(pair_sum, from_chips, dist.chip_idx, "add_chips_" + tag)
        return unpack_local(join_halves(my_half, dist.core_idx, "join_halves_" + tag), names, dist.shard_shapes, d)

    gr, reduced = {}, None
    (dh2_a, dpg, dpe), (gr["norm_ple_post"], loss_cols) = row_bwd(
        "ple_loss", f_ple_loss, [h2, pg, pe, tgt], [sm["norm_ple_post"]], [None], [F32, BF16, BF16, None], primal_sum=True)
    loss = jnp.sum(loss_cols)
    gr["w_ple"] = matmul(p, dpe, ta=True, name="d_w_ple")
    gr["w_ple_gate"] = matmul(h2, dpg, ta=True, name="d_w_ple_gate")
    dh2_b = matmul(dpg, wf["w_ple_gate"], tb=True, name="d_h2")
    (dh1_a, dff), (gr["norm_ffn_post"],) = row_bwd(
        "ffn_out_bwd", f_ffn_out, [h1, ff], [sm["norm_ffn_post"]], [[dh2_a, dh2_b]], [F32, BF16])
    gr["w_ff2"] = matmul(act, dff, ta=True, name="d_w_ff2")
    dact = matmul(dff, wf["w_ff2"], tb=True, name="d_act")
    (da1,), _ = row_bwd("relu2_bwd", f_relu2, [a1], [], [[dact]], [BF16])
    gr["w_ff1"] = matmul(n2, da1, ta=True, name="d_w_ff1")
    dn2 = matmul(da1, wf["w_ff1"], tb=True, name="d_n2")
    (dx_a, dmo), (gr["norm_mix_post"], gr["norm_ffn_pre"]) = row_bwd(
        "mix_out_bwd", f_mix_out, [x, mo], [sm["norm_mix_post"], sm["norm_ffn_pre"]], [[dh1_a], [dn2]], [F32, BF16])
    gr["w_out"] = matmul(merged, dmo, ta=True, name="d_w_out")
    dmerged = matmul(dmo, wf["w_out"], tb=True, name="d_merged")
    (dgp_ssd, dgp_sb, dyb_ssd, dyb_sb), (db_ssd, db_sb) = row_bwd(
        "merge_bwd", f_merge, merge_rows, [b_ssd, b_sb], [[dmerged]], [BF16, BF16, BF16, BF16])
    gr["b_gate"] = jnp.concatenate([db_ssd, db_sb], axis=1)
    gr["w_ssd_branch"] = matmul(y_ssd, dyb_ssd, ta=True, name="d_w_ssd_branch")
    gr["w_sb_branch"] = matmul(y_sb, dyb_sb, ta=True, name="d_w_sb_branch")
    dy_ssd = matmul(dyb_ssd, wf["w_ssd_branch"], tb=True, name="d_y_ssd")
    dy_sb = matmul(dyb_sb, wf["w_sb_branch"], tb=True, name="d_y_sb")
    if dist is None:
        dq, dk, dv = attn_bwd(proj, dy_sb, dm, "attn_bwd")
    else:
        pair_sum_a, rider_a = reduce_start(PART_A, "a")
        dq, dk, dv, from_chips_a = attn_bwd(proj, dy_sb, dm, "attn_bwd", rider=rider_a)
        reduced = reduce_finish(PART_A, pair_sum_a, from_chips_a, "a")
    dz, dxs, dbm, dcm, ddtc, ddtw, dalog, ddsk, gr["ssd_norm"] = ssd_bwd(
        proj, xbc, dtc, dtw, alog, dsk, sm["ssd_norm"], states, dy_ssd, dm, "ssd_bwd")
    gr["a_log"], gr["d_skip"] = (v.reshape(1, dm.H) for v in (dalog, ddsk))
    ddt_post = (ddtc.transpose(1, 0, 2) + ddtw.transpose(2, 0, 1)).reshape(s, dm.H)
    (ddt,), (gr["dt_bias"],) = row_bwd("dt_bwd", f_dt, [dt_raw], [sm["dt_bias"]], [[ddt_post]], [BF16])
    conv_w32 = wf["conv_w"].astype(F32)
    du_x, dw_x, dcb_x = conv_bwd(proj, off["xbc"], dxs, 0, conv_w32, sm["conv_b"], "conv_bwd_x")
    du_b, dw_b, dcb_b = conv_bwd(proj, off["xbc"], dbm, dm.DI, conv_w32, sm["conv_b"], "conv_bwd_b")
    du_c, dw_c, dcb_c = conv_bwd(proj, off["xbc"], dcm, dm.DI + g * D_STATE, conv_w32, sm["conv_b"], "conv_bwd_c")
    gr["conv_w"] = jnp.concatenate([dw_x, dw_b, dw_c], axis=1)
    gr["conv_b"] = jnp.concatenate([dcb_x, dcb_b, dcb_c], axis=1)
    dproj = jnp.concatenate(
        [dz, du_x, du_b, du_c, dq, dk, dv, dgp_ssd, dgp_sb, ddt, jnp.zeros((s, DT_PAD - dm.H), BF16)], axis=1)
    dw_all = matmul(n1, dproj, ta=True, name="d_w_all")
    gr["w_in"] = jnp.concatenate(
        [dw_all[:, :c0], dw_all[:, off["dt"]:off["dt"] + dm.H], dw_all[:, c0:c0 + 3 * dm.SBW]], axis=1)
    gr["w_gate"] = dw_all[:, off["gate"]:off["gate"] + 2 * d]
    if dist is None:
        dn1 = matmul(dproj, w_all, tb=True, name="d_n1")
    else:
        pair_sum_b, rider_b = reduce_start(PART_B, "b")
        dn1, from_chips_b = matmul(dproj, w_all, tb=True, name="d_n1", rider=rider_b)
        reduced.update(reduce_finish(PART_B, pair_sum_b, from_chips_b, "b"))
    (dx_b,), (gr["norm_mix_pre"],) = row_bwd("norm1_bwd", f_norm1, [x], [sm["norm_mix_pre"]], [[dn1]], [F32])
    (grad_x,) = row_fwd("grad_x", lambda u, v: (u + v,), [dx_a, dx_b], [], [(d, F32)])
    return loss, grad_x, gr, reduced


def kernel(x, p, norm_mix_pre, w_in, conv_w, conv_b, dt_bias, a_log, d_skip, ssd_norm, w_ssd_branch, w_sb_branch, w_gate, b_gate, w_out, norm_mix_post, norm_ffn_pre, w_ff1, w_ff2, norm_ffn_post, w_ple, w_ple_gate, norm_ple_post, loss_target, m_norm_mix_pre, m_w_in, m_conv_w, m_conv_b, m_dt_bias, m_a_log, m_d_skip, m_ssd_norm, m_w_ssd_branch, m_w_sb_branch, m_w_gate, m_b_gate, m_w_out, m_norm_mix_post, m_norm_ffn_pre, m_w_ff1, m_w_ff2, m_norm_ffn_post, m_w_ple, m_w_ple_gate, m_norm_ple_post, v_norm_mix_pre, v_w_in, v_conv_w, v_conv_b, v_dt_bias, v_a_log, v_d_skip, v_ssd_norm, v_w_ssd_branch, v_w_sb_branch, v_w_gate, v_b_gate, v_w_out, v_norm_mix_post, v_norm_ffn_pre, v_w_ff1, v_w_ff2, v_norm_ffn_post, v_w_ple, v_w_ple_gate, v_norm_ple_post):
    loc = dict(locals())
    unbatch = lambda a: a[0] if a.ndim == 3 else a
    w = {n: unbatch(loc[n]) for n in ALL_WEIGHTS}
    m = {n: unbatch(loc["m_" + n]) for n in ALL_WEIGHTS}
    v = {n: unbatch(loc["v_" + n]) for n in ALL_WEIGHTS}
    xs, ps, tgt = x[0], p[0, 0], loss_target[0]
    s, d = xs.shape
    di = w["w_ssd_branch"].shape[0] * N_CHIPS
    cd = w["conv_b"].shape[1]
    dm = Dims(S=s, D=d, DI=di, H=w["dt_bias"].shape[1], G=(cd - di) // (2 * D_STATE), CD=cd,
              SBW=w["w_sb_branch"].shape[0] * N_CHIPS, DFF=w["w_ff2"].shape[0] * N_CHIPS, PLE=ps.shape[1])
    ix, iy, ic = lax.axis_index("x"), lax.axis_index("y"), lax.axis_index("c")
    chip_idx = jnp.reshape(2 * ix + iy, (1,)).astype(jnp.int32)
    core_idx = jnp.reshape(ic, (1,)).astype(jnp.int32)
    dev_idx = jnp.reshape(4 * ix + 2 * iy + ic, (1,)).astype(jnp.int32)

    shard_shapes = {n: w[n].shape for n in BIG_WEIGHTS}
    packed_b = pack_gather(w, PART_B, d)
    wf = unpack_full(own_slab(gather_weights(packed_b), packed_b, chip_idx), PART_B, shard_shapes, d)
    sm = {n: w[n] for n in SMALL_WEIGHTS}
    dist = Dist(packed_a=pack_gather(w, PART_A, d), shard_shapes=shard_shapes, core_idx=core_idx, chip_idx=chip_idx)

    loss_part, grad_x, gr, grads = local_step(xs, ps, tgt, wf, sm, dm, dist)
    loss = lax.psum(loss_part, ("x", "y", "c"))

    lay = _small_layout({n: w[n].shape[1] for n in SMALL_WEIGHTS}, d)
    gs_red = sum_small(exchange_small(pack_small(gr, lay, d)), dev_idx, "sum_small")
    grads.update(unpack_small(gs_red, lay))
    delta, new_m, new_v = {}, {}, {}
    for n in BIG_WEIGHTS:
        delta[n], new_m[n], new_v[n] = adamw(w[n], grads[n], m[n], v[n], "adamw_" + n)
    d_sm, nm_sm, nv_sm = adamw(pack_small(w, lay, d), gs_red, pack_small(m, lay, d), pack_small(v, lay, d), "adamw_small")
    for out, packed in ((delta, d_sm), (new_m, nm_sm), (new_v, nv_sm)):
        out.update(unpack_small(packed, lay))

    def leaves(vals):
        return [vals[n][None] if n in BIG_WEIGHTS else vals[n] for n in ALL_WEIGHTS]

    return (loss, grad_x[None], *leaves(grads), *leaves(delta), *leaves(new_m), *leaves(new_v))
```

```python
import functools
from typing import NamedTuple

import jax
import jax.numpy as jnp
from jax import lax
from jax.experimental import pallas as pl
from jax.experimental.pallas import tpu as pltpu

F32 = jnp.float32
BF16 = jnp.bfloat16
SDS = jax.ShapeDtypeStruct

HEAD_DIM = 64
GROUP_HEADS = 4
D_STATE = 128
CHUNK = 128
ATT_TILE = 128
ATT_GROUP = 4
ATT_DEAD = -160.0
LOG2E = 1.4426950408889634
CONV_K = 4
CONV_COLS = 128
RMS_EPS = 1e-6
LANES = 128
DT_PAD = 512
N_CHIPS = 4
N_DEV = 8
SMALL_ROWS = 16
VMEM_LIMIT = 48 * 1024 * 1024
MAX_TK = 3072

ADAM_LR = 0.001
ADAM_B1 = 0.9
ADAM_B2 = 0.999
ADAM_EPS = 1e-08
ADAM_WD = 0.01
ADAM_STEP = 10

MESH_ID = pl.DeviceIdType.MESH
HBM_SPEC = pl.BlockSpec(memory_space=pltpu.HBM)

BIG_WEIGHTS = ("w_in", "conv_w", "w_ssd_branch", "w_sb_branch", "w_gate", "w_out", "w_ff1", "w_ff2", "w_ple", "w_ple_gate")
SHARD_AXIS = {"w_in": 1, "conv_w": 1, "w_ssd_branch": 0, "w_sb_branch": 0, "w_gate": 1, "w_out": 0, "w_ff1": 1,
              "w_ff2": 0, "w_ple": 1, "w_ple_gate": 0}
SMALL_WEIGHTS = ("norm_mix_pre", "conv_b", "dt_bias", "a_log", "d_skip", "ssd_norm", "b_gate", "norm_mix_post",
                 "norm_ffn_pre", "norm_ffn_post", "norm_ple_post")
ALL_WEIGHTS = ("norm_mix_pre", "w_in", "conv_w", "conv_b", "dt_bias", "a_log", "d_skip", "ssd_norm", "w_ssd_branch",
               "w_sb_branch", "w_gate", "b_gate", "w_out", "norm_mix_post", "norm_ffn_pre", "w_ff1", "w_ff2",
               "norm_ffn_post", "w_ple", "w_ple_gate", "norm_ple_post")


class Dims(NamedTuple):
    S: int
    D: int
    DI: int
    H: int
    G: int
    CD: int
    SBW: int
    DFF: int
    PLE: int

    @property
    def NA(self):
        return self.DI + self.CD + 3 * self.SBW + 2 * self.D + DT_PAD

    @property
    def off(self):
        o = {}
        o["z"] = 0
        o["xbc"] = self.DI
        o["q"] = self.DI + self.CD
        o["k"] = o["q"] + self.SBW
        o["v"] = o["k"] + self.SBW
        o["gate"] = o["v"] + self.SBW
        o["dt"] = o["gate"] + 2 * self.D
        return o


def _cparams(sem):
    return pltpu.CompilerParams(dimension_semantics=sem, vmem_limit_bytes=VMEM_LIMIT)


def _pick(n, cands):
    for c in cands:
        if n % c == 0:
            return c
    raise ValueError(f"no tile for {n}")


def _grid_call(body, *, grid, in_specs, out_specs, out_shape, scratch, operands, name, rider=None):
    if rider is None:
        sem = ("parallel",) + ("arbitrary",) * (len(grid) - 1)
        return pl.pallas_call(body, grid=grid, in_specs=in_specs, out_specs=out_specs, out_shape=out_shape,
                              scratch_shapes=scratch, compiler_params=_cparams(sem), name=name)(*operands)
    return pl.pallas_call(
        _with_rider(body, rider, grid, len(in_specs), len(out_specs)), grid=grid,
        in_specs=list(in_specs) + [HBM_SPEC] * len(rider.operands),
        out_specs=list(out_specs) + [HBM_SPEC] * len(rider.out_shape),
        out_shape=list(out_shape) + list(rider.out_shape), scratch_shapes=list(scratch) + list(rider.scratch),
        compiler_params=_cparams(("arbitrary",) * len(grid)), name=name)(*operands, *rider.operands)


def matmul(a, b, *, ta=False, tb=False, out_dtype=F32, name, rider=None):
    m, k = (a.shape[1], a.shape[0]) if ta else a.shape
    n, kb = b.shape if tb else (b.shape[1], b.shape[0])
    assert k == kb, (a.shape, b.shape, ta, tb)
    tm = _pick(m, (1024, 512, 256, 128))
    tn = _pick(n, (512, 256, 128))
    tk = max(t for t in range(LANES, min(k, MAX_TK) + 1, LANES) if k % t == 0)
    nk = k // tk
    dims = (((0 if ta else 1,), (1 if tb else 0,)), ((), ()))

    def body(a_ref, b_ref, o_ref, acc_ref):
        part = lax.dot_general(a_ref[...].astype(BF16), b_ref[...].astype(BF16), dims, preferred_element_type=F32)
        if nk == 1:
            o_ref[...] = part.astype(o_ref.dtype)
        else:
            kk = pl.program_id(2)

            @pl.when(kk == 0)
            def _():
                acc_ref[...] = part

            @pl.when(kk > 0)
            def _():
                acc_ref[...] += part

            @pl.when(kk == nk - 1)
            def _():
                o_ref[...] = acc_ref[...].astype(o_ref.dtype)

    a_spec = pl.BlockSpec((tk, tm), lambda i, j, kk: (kk, i)) if ta else pl.BlockSpec((tm, tk), lambda i, j, kk: (i, kk))
    b_spec = pl.BlockSpec((tn, tk), lambda i, j, kk: (j, kk)) if tb else pl.BlockSpec((tk, tn), lambda i, j, kk: (kk, j))
    res = _grid_call(body, grid=(m // tm, n // tn, nk), in_specs=[a_spec, b_spec],
                     out_specs=[pl.BlockSpec((tm, tn), lambda i, j, kk: (i, j))], out_shape=[SDS((m, n), out_dtype)],
                     scratch=[pltpu.VMEM((tm, tn), F32)], operands=(a, b), name=name, rider=rider)
    return res[0] if rider is None else res


def _row_spec(entry, tile):
    arr, width, cb = entry if isinstance(entry, tuple) else (entry, entry.shape[1], 0)
    return arr, pl.BlockSpec((tile, width), lambda i, cb=cb: (i, cb))


def _par_spec(p):
    return pl.BlockSpec(p.shape, lambda i: (0, 0))


def row_fwd(name, fn, rows, params, outs, tile=256):
    arrs, specs = zip(*[_row_spec(e, tile) for e in rows])
    s = arrs[0].shape[0]
    nr, npar = len(rows), len(params)

    def body(*refs):
        r = [x[...].astype(F32) for x in refs[:nr]]
        p = [x[...] for x in refs[nr:nr + npar]]
        res = fn(*r, *p)
        for o_ref, val in zip(refs[nr + npar:], res):
            o_ref[...] = val.astype(o_ref.dtype)

    return pl.pallas_call(
        body, grid=(s // tile,), in_specs=list(specs) + [_par_spec(p) for p in params],
        out_specs=[pl.BlockSpec((tile, w), lambda i: (i, 0)) for w, _ in outs],
        out_shape=[SDS((s, w), dt) for w, dt in outs],
        compiler_params=_cparams(("parallel",)), name=name)(*arrs, *params)


def row_bwd(name, fn, rows, params, cots, row_grads, tile=256, primal_sum=False):
    arrs, specs = zip(*[_row_spec(e, tile) for e in rows])
    s = arrs[0].shape[0]
    nr, npar = len(rows), len(params)
    cot_entries = [e for c in cots if c is not None for e in c]
    carrs, cspecs = zip(*[_row_spec(e, tile) for e in cot_entries]) if cot_entries else ((), ())
    nc = len(cot_entries)
    want = [i for i, d in enumerate(row_grads) if d is not None]

    def body(*refs):
        r = [x[...].astype(F32) for x in refs[:nr]]
        p = [x[...] for x in refs[nr:nr + npar]]
        cvals = [x[...].astype(F32) for x in refs[nr + npar:nr + npar + nc]]
        outs = refs[nr + npar + nc:]
        prim, vjp = jax.vjp(fn, *r, *p)
        ct, pos = [], 0
        for c, pr in zip(cots, prim):
            if c is None:
                ct.append(jnp.ones_like(pr))
            else:
                acc = cvals[pos]
                for extra in cvals[pos + 1:pos + len(c)]:
                    acc = acc + extra
                pos += len(c)
                ct.append(acc)
        grads = vjp(tuple(ct))
        for o_ref, i in zip(outs[:len(want)], want):
            o_ref[...] = grads[i].astype(o_ref.dtype)
        acc_refs = outs[len(want):]
        vals = [grads[nr + j] for j in range(npar)]
        if primal_sum:
            vals.append(jnp.sum(prim[0], axis=0, keepdims=True))
        first = pl.program_id(0) == 0

        @pl.when(first)
        def _():
            for a_ref, v in zip(acc_refs, vals):
                a_ref[...] = v

        @pl.when(jnp.logical_not(first))
        def _():
            for a_ref, v in zip(acc_refs, vals):
                a_ref[...] += v

    widths = [(e[1] if isinstance(e, tuple) else e.shape[1]) for e in rows]
    out_specs = [pl.BlockSpec((tile, widths[i]), lambda i_: (i_, 0)) for i in want]
    out_shape = [SDS((s, widths[i]), row_grads[i]) for i in want]
    pshapes = [p.shape for p in params]
    if primal_sum:
        pshapes.append((1, widths[0]))
    out_specs += [pl.BlockSpec(sh, lambda i_: (0, 0)) for sh in pshapes]
    out_shape += [SDS(sh, F32) for sh in pshapes]
    res = pl.pallas_call(
        body, grid=(s // tile,), in_specs=list(specs) + [_par_spec(p) for p in params] + list(cspecs),
        out_specs=out_specs, out_shape=out_shape,
        compiler_params=_cparams(("arbitrary",)), name=name)(*arrs, *params, *carrs)
    return res[:len(want)], res[len(want):]


def _rms(x, w):
    return x * lax.rsqrt(jnp.mean(x * x, axis=-1, keepdims=True) + RMS_EPS) * w


def _sigmoid(x):
    return jax.nn.sigmoid(x)


def _softplus(x):
    return jnp.maximum(x, 0.0) + jnp.log1p(jnp.exp(-jnp.abs(x)))


def f_norm1(x, w):
    return (_rms(x, w),)


def f_dt(raw, bias):
    return (_softplus(raw + bias),)


def f_merge(gp_ssd, gp_sb, yb_ssd, yb_sb, b_ssd, b_sb):
    return (_sigmoid(gp_ssd + b_ssd) * yb_ssd + _sigmoid(gp_sb + b_sb) * yb_sb,)


def f_mix_out(x, mo, w_post, w_pre):
    h1 = x + _rms(mo, w_post)
    return h1, _rms(h1, w_pre)


def f_relu2(a1):
    return (jnp.square(jnp.maximum(a1, 0.0)),)


def f_ffn_out(h1, ff, w):
    return (h1 + _rms(ff, w),)


def f_ple_loss(h2, pg, pe, tgt, w):
    h3 = h2 + _rms(_sigmoid(pg) * pe, w)
    return (0.5 * jnp.square(h3 - tgt) * (1.0 / h2.shape[-1]),)


def _shift_down(u, d, rows):
    return u if d == 0 else jnp.where(rows >= d, pltpu.roll(u, d, 0), 0.0)


def _shift_up(u, d, rows):
    s = u.shape[0]
    return u if d == 0 else jnp.where(rows < s - d, pltpu.roll(u, s - d, 0), 0.0)


def conv_fwd(proj, col0, cd, conv_w, conv_b, name):
    s = proj.shape[0]
    cb0 = col0 // CONV_COLS

    def body(u_ref, w_ref, b_ref, o_ref):
        u = u_ref[...]
        rows = lax.broadcasted_iota(jnp.int32, u.shape, 0)
        y = jnp.broadcast_to(b_ref[...], u.shape)
        for k in range(CONV_K):
            y = y + w_ref[k:k + 1, :] * _shift_down(u, CONV_K - 1 - k, rows)
        o_ref[...] = y * _sigmoid(y)

    return pl.pallas_call(
        body, grid=(cd // CONV_COLS,),
        in_specs=[pl.BlockSpec((s, CONV_COLS), lambda i: (0, cb0 + i)),
                  pl.BlockSpec((CONV_K, CONV_COLS), lambda i: (0, i)),
                  pl.BlockSpec((1, CONV_COLS), lambda i: (0, i))],
        out_specs=pl.BlockSpec((s, CONV_COLS), lambda i: (0, i)),
        out_shape=SDS((s, cd), F32), compiler_params=_cparams(("parallel",)), name=name)(proj, conv_w, conv_b)


def conv_bwd(proj, col0, dout, ch0, conv_w, conv_b, name):
    s = proj.shape[0]
    ncb = dout.shape[1] // CONV_COLS
    cb0 = (col0 + ch0) // CONV_COLS
    wb0 = ch0 // CONV_COLS

    def body(u_ref, g_ref, w_ref, b_ref, du_ref, dw_ref, db_ref):
        u = u_ref[...]
        rows = lax.broadcasted_iota(jnp.int32, u.shape, 0)
        y = jnp.broadcast_to(b_ref[...], u.shape)
        for k in range(CONV_K):
            y = y + w_ref[k:k + 1, :] * _shift_down(u, CONV_K - 1 - k, rows)
        sg = _sigmoid(y)
        dy = g_ref[...] * (sg * (1.0 + y * (1.0 - sg)))
        du = jnp.zeros_like(u)
        for k in range(CONV_K):
            d = CONV_K - 1 - k
            du = du + w_ref[k:k + 1, :] * _shift_up(dy, d, rows)
            dw_ref[k:k + 1, :] = jnp.sum(dy * _shift_down(u, d, rows), axis=0, keepdims=True)
        du_ref[...] = du.astype(du_ref.dtype)
        db_ref[...] = jnp.sum(dy, axis=0, keepdims=True)

    return pl.pallas_call(
        body, grid=(ncb,),
        in_specs=[pl.BlockSpec((s, CONV_COLS), lambda i: (0, cb0 + i)),
                  pl.BlockSpec((s, CONV_COLS), lambda i: (0, i)),
                  pl.BlockSpec((CONV_K, CONV_COLS), lambda i: (0, wb0 + i)),
                  pl.BlockSpec((1, CONV_COLS), lambda i: (0, wb0 + i))],
        out_specs=[pl.BlockSpec((s, CONV_COLS), lambda i: (0, i)),
                   pl.BlockSpec((CONV_K, CONV_COLS), lambda i: (0, i)),
                   pl.BlockSpec((1, CONV_COLS), lambda i: (0, i))],
        out_shape=[SDS((s, ncb * CONV_COLS), BF16), SDS((CONV_K, ncb * CONV_COLS), F32), SDS((1, ncb * CONV_COLS), F32)],
        compiler_params=_cparams(("parallel",)), name=name)(proj, dout, conv_w, conv_b)


def _dot(a, b, dims):
    return lax.dot_general(a.astype(BF16), b.astype(BF16), (dims, ((), ())), preferred_element_type=F32)


NN = ((1,), (0,))
NT = ((1,), (1,))
TN = ((0,), (0,))


def ssd_chunk(xs, zs, nw, dtc, dtw, alogs, dsks, bm, cm, prev):
    ln = bm.shape[0]
    gw = GROUP_HEADS * HEAD_DIM
    row = lax.broadcasted_iota(jnp.int32, (ln, ln), 0)
    col = lax.broadcasted_iota(jnp.int32, (ln, ln), 1)
    causal = row >= col
    tri = causal.astype(F32)
    tri_t = (row <= col).astype(F32)
    lane_head = lax.broadcasted_iota(jnp.int32, (1, gw), 1) // HEAD_DIM
    sub_head = lax.broadcasted_iota(jnp.int32, (gw, 1), 0) // HEAD_DIM
    on_lanes = [(lane_head == r).astype(F32) for r in range(GROUP_HEADS)]
    on_rows = [(sub_head == r).astype(F32) for r in range(GROUP_HEADS)]
    cb = _dot(cm, bm, NT)
    decays, dt_full, acs_full, end_full, dsk_full, end_rows = [], 0.0, 0.0, 0.0, 0.0, 0.0
    for r in range(GROUP_HEADS):
        a = -jnp.exp(alogs[r])
        da_c = dtc[r] * a
        da_w = dtw[r] * a
        acs_c = jnp.sum(tri * da_w, axis=1, keepdims=True)
        acs_w = jnp.sum(tri_t * da_c, axis=0, keepdims=True)
        alast = jnp.sum(da_w, axis=1, keepdims=True)
        decays.append(jnp.exp(jnp.where(causal, acs_c - acs_w, -jnp.inf)))
        dt_full = dt_full + dtc[r] * on_lanes[r]
        acs_full = acs_full + acs_c * on_lanes[r]
        end_full = end_full + alast * on_lanes[r]
        dsk_full = dsk_full + dsks[r] * on_lanes[r]
        end_rows = end_rows + alast * on_rows[r]
    xd = xs * dt_full
    y = xs * dsk_full + _dot(cm, prev, NT) * jnp.exp(acs_full)
    for r in range(GROUP_HEADS):
        y = y + _dot(cb * decays[r], xd * on_lanes[r], NN)
    new_prev = prev * jnp.exp(end_rows) + _dot(xd * jnp.exp(end_full - acs_full), bm, TN)
    yg = y * (zs * _sigmoid(zs))
    rstd = lax.rsqrt(jnp.mean(yg * yg, axis=-1, keepdims=True) + RMS_EPS)
    return yg * rstd * nw, new_prev


def _ssd_specs(dm, cidx):
    gw = GROUP_HEADS * HEAD_DIM
    nb0 = dm.DI // D_STATE
    par = pl.BlockSpec((None, 1, GROUP_HEADS), lambda g, c: (g, 0, 0))
    return dict(
        z=pl.BlockSpec((CHUNK, gw), lambda g, c: (cidx(c), g)),
        xs=pl.BlockSpec((CHUNK, gw), lambda g, c: (cidx(c), g)),
        b=pl.BlockSpec((CHUNK, D_STATE), lambda g, c: (cidx(c), nb0 + g)),
        c=pl.BlockSpec((CHUNK, D_STATE), lambda g, c: (cidx(c), nb0 + dm.G + g)),
        dtc=pl.BlockSpec((None, CHUNK, GROUP_HEADS), lambda g, c: (g, cidx(c), 0)),
        dtw=pl.BlockSpec((None, GROUP_HEADS, CHUNK), lambda g, c: (g, 0, cidx(c))),
        par=par,
        nw=pl.BlockSpec((1, gw), lambda g, c: (0, g)),
        st=pl.BlockSpec((None, None, gw, D_STATE), lambda g, c: (g, cidx(c), 0, 0)),
    )


def _ssd_load(z_ref, xs_ref, b_ref, c_ref, dtc_ref, dtw_ref, alog_ref, dsk_ref, nw_ref):
    dtc = tuple(dtc_ref[:, r:r + 1] for r in range(GROUP_HEADS))
    dtw = tuple(dtw_ref[r:r + 1, :] for r in range(GROUP_HEADS))
    alogs = tuple(alog_ref[:, r:r + 1] for r in range(GROUP_HEADS))
    dsks = tuple(dsk_ref[:, r:r + 1] for r in range(GROUP_HEADS))
    return xs_ref[...], z_ref[...], nw_ref[...], dtc, dtw, alogs, dsks, b_ref[...], c_ref[...]


def ssd_fwd(proj, xbc, dtc, dtw, alog, dsk, nw, dm, name):
    nc = dm.S // CHUNK
    gw = GROUP_HEADS * HEAD_DIM
    sp = _ssd_specs(dm, lambda c: c)

    def body(z_ref, xs_ref, b_ref, c_ref, dtc_ref, dtw_ref, alog_ref, dsk_ref, nw_ref, y_ref, st_ref, prev):
        @pl.when(pl.program_id(1) == 0)
        def _():
            prev[...] = jnp.zeros_like(prev)

        args = _ssd_load(z_ref, xs_ref, b_ref, c_ref, dtc_ref, dtw_ref, alog_ref, dsk_ref, nw_ref)
        st_ref[...] = prev[...]
        out, new = ssd_chunk(*args, prev[...])
        y_ref[...] = out.astype(y_ref.dtype)
        prev[...] = new

    return pl.pallas_call(
        body, grid=(dm.G, nc),
        in_specs=[sp["z"], sp["xs"], sp["b"], sp["c"], sp["dtc"], sp["dtw"], sp["par"], sp["par"], sp["nw"]],
        out_specs=[sp["xs"], sp["st"]],
        out_shape=[SDS((dm.S, dm.DI), BF16), SDS((dm.G, nc, gw, D_STATE), F32)],
        scratch_shapes=[pltpu.VMEM((gw, D_STATE), F32)],
        compiler_params=_cparams(("parallel", "arbitrary")), name=name)(proj, xbc, xbc, xbc, dtc, dtw, alog, dsk, nw)


def ssd_bwd(proj, xbc, dtc, dtw, alog, dsk, nw, states, dy, dm, name):
    nc = dm.S // CHUNK
    sp = _ssd_specs(dm, lambda c: nc - 1 - c)
    gw = GROUP_HEADS * HEAD_DIM
    bc_spec = pl.BlockSpec((CHUNK, D_STATE), lambda g, c: (nc - 1 - c, g))

    def body(z_ref, xs_ref, b_ref, c_ref, dtc_ref, dtw_ref, alog_ref, dsk_ref, nw_ref, st_ref, dy_ref,
             dz_ref, dxs_ref, db_ref, dc_ref, ddtc_ref, ddtw_ref, dalog_ref, ddsk_ref, dnw_ref, dprev):
        first = pl.program_id(1) == 0

        @pl.when(first)
        def _():
            dprev[...] = jnp.zeros_like(dprev)

        args = _ssd_load(z_ref, xs_ref, b_ref, c_ref, dtc_ref, dtw_ref, alog_ref, dsk_ref, nw_ref)
        _, vjp = jax.vjp(ssd_chunk, *args, st_ref[...])
        gxs, gzs, gnw, gdtc, gdtw, galogs, gdsks, gb, gc, gprev = vjp((dy_ref[...], dprev[...]))
        dxs_ref[...] = gxs
        dz_ref[...] = gzs.astype(dz_ref.dtype)
        db_ref[...] = gb
        dc_ref[...] = gc
        dprev[...] = gprev
        for r in range(GROUP_HEADS):
            ddtc_ref[:, r:r + 1] = gdtc[r]
            ddtw_ref[r:r + 1, :] = gdtw[r]

        @pl.when(first)
        def _():
            dnw_ref[...] = gnw
            for r in range(GROUP_HEADS):
                dalog_ref[:, r:r + 1] = galogs[r]
                ddsk_ref[:, r:r + 1] = gdsks[r]

        @pl.when(jnp.logical_not(first))
        def _():
            dnw_ref[...] += gnw
            for r in range(GROUP_HEADS):
                dalog_ref[:, r:r + 1] += galogs[r]
                ddsk_ref[:, r:r + 1] += gdsks[r]

    xs_out = pl.BlockSpec((CHUNK, gw), lambda g, c: (nc - 1 - c, g))
    return pl.pallas_call(
        body, grid=(dm.G, nc),
        in_specs=[sp["z"], sp["xs"], sp["b"], sp["c"], sp["dtc"], sp["dtw"], sp["par"], sp["par"], sp["nw"],
                  sp["st"], xs_out],
        out_specs=[xs_out, xs_out, bc_spec, bc_spec, sp["dtc"], sp["dtw"], sp["par"], sp["par"], sp["nw"]],
        out_shape=[SDS((dm.S, dm.DI), BF16), SDS((dm.S, dm.DI), F32), SDS((dm.S, dm.G * D_STATE), F32),
                   SDS((dm.S, dm.G * D_STATE), F32), SDS((dm.G, dm.S, GROUP_HEADS), F32), SDS((dm.G, GROUP_HEADS, dm.S), F32),
                   SDS((dm.G, 1, GROUP_HEADS), F32), SDS((dm.G, 1, GROUP_HEADS), F32), SDS((1, dm.DI), F32)],
        scratch_shapes=[pltpu.VMEM((gw, D_STATE), F32)],
        compiler_params=_cparams(("parallel", "arbitrary")), name=name)(
            proj, xbc, xbc, xbc, dtc, dtw, alog, dsk, nw, states, dy)


def _split_bf16(v):
    hi = v.astype(BF16)
    return hi, (v - hi.astype(F32)).astype(BF16)


def _tri(v, mat):
    hi, lo = _split_bf16(v)
    return jnp.dot(hi, mat, preferred_element_type=F32) + jnp.dot(lo, mat, preferred_element_type=F32)


def _blocks(v):
    return [v[:, b * ATT_TILE:(b + 1) * ATT_TILE] for b in range(v.shape[1] // ATT_TILE)]


def _sb_group(z, mask, run, after_mat):
    sp = jnp.maximum(z, 0.0) + jnp.log2(1.0 + jnp.exp2(-jnp.abs(z)))
    lk = -sp if mask is None else jnp.where(mask, -sp, 0.0)
    cums = [_tri(v, after_mat) for v in _blocks(lk)]
    sums = [jnp.sum(v, axis=1, keepdims=True) for v in _blocks(lk)]
    later = [None] * len(cums)
    for b in reversed(range(len(cums))):
        later[b] = run + cums[b]
        run = run + sums[b]
    ls = z - sp
    w = jnp.exp2(ls + jnp.concatenate(later, axis=1))
    if mask is not None:
        w = jnp.where(mask, w, 0.0)
    return ls, w, run


def _alive(run_a, run_b):
    return (jnp.max(jnp.maximum(run_a, run_b)) > ATT_DEAD).astype(jnp.int32)


def _group_mask(i, g, t, gw):
    rows = i * t + lax.broadcasted_iota(jnp.int32, (t, gw), 0)
    cols = g * gw + lax.broadcasted_iota(jnp.int32, (t, gw), 1)
    return cols < rows


def _att_specs(dm, s):
    t = ATT_TILE
    qb, kb, vb = dm.off["q"] // LANES, dm.off["k"] // LANES, dm.off["v"] // LANES
    return (pl.BlockSpec((t, LANES), lambda p, i: (i, qb + p)),
            pl.BlockSpec((s, LANES), lambda p, i: (0, kb + p)),
            pl.BlockSpec((s, LANES), lambda p, i: (0, vb + p)))


def attn_fwd(proj, dm, name, rider=None):
    s, t = dm.S, ATT_TILE
    scale = HEAD_DIM ** -0.5
    hsl = [slice(h * HEAD_DIM, (h + 1) * HEAD_DIM) for h in range(2)]

    gw = ATT_GROUP * t

    def body(q_ref, k_ref, v_ref, o_ref):
        i = pl.program_id(1)
        gd = i // ATT_GROUP
        r_io = lax.broadcasted_iota(jnp.int32, (t, t), 0)
        c_io = lax.broadcasted_iota(jnp.int32, (t, t), 1)
        after_mat = (r_io > c_io).astype(BF16)
        qs = [q_ref[:, sl].astype(BF16) for sl in hsl]

        def group(g, carry, mask):
            r0 = pl.multiple_of(g * gw, gw)
            zs = [_dot(qs[h], k_ref[pl.ds(r0, gw), hsl[h]], NT) * (scale * LOG2E) for h in range(2)]
            res = [_sb_group(zs[h], mask, carry[h][0], after_mat) for h in range(2)]
            return tuple((res[h][2], carry[h][1] + _dot(res[h][1], v_ref[pl.ds(r0, gw), hsl[h]], NN)) for h in range(2))

        zero = (jnp.zeros((t, 1), F32), jnp.zeros((t, HEAD_DIM), F32))
        carry = group(gd, (zero, zero), _group_mask(i, gd, t, gw))

        def step(st):
            jj, _, c = st
            c = group(gd - jj, c, None)
            return jj + 1, _alive(c[0][0], c[1][0]), c

        _, _, carry = lax.while_loop(lambda st: jnp.logical_and(st[0] <= gd, st[1] > 0), step,
                                     (jnp.int32(1), _alive(carry[0][0], carry[1][0]), carry))
        for h in range(2):
            o_ref[:, hsl[h]] = carry[h][1]

    qs_, ks_, vs_ = _att_specs(dm, s)
    res = _grid_call(body, grid=(dm.SBW // LANES, s // t), in_specs=[qs_, ks_, vs_],
                     out_specs=[pl.BlockSpec((t, LANES), lambda p, i: (i, p))], out_shape=[SDS((s, dm.SBW), F32)],
                     scratch=[], operands=(proj, proj, proj), name=name, rider=rider)
    return res[0] if rider is None else res


def attn_bwd(proj, do, dm, name, rider=None):
    s, t = dm.S, ATT_TILE
    nq = s // t
    gw = ATT_GROUP * t
    scale = HEAD_DIM ** -0.5
    hsl = [slice(h * HEAD_DIM, (h + 1) * HEAD_DIM) for h in range(2)]

    def body(q_ref, k_ref, v_ref, do_ref, dq_ref, dk_ref, dv_ref, dk_acc, dv_acc, g_scr, s_scr):
        i = pl.program_id(1)

        @pl.when(i == 0)
        def _():
            dk_acc[...] = jnp.zeros_like(dk_acc)
            dv_acc[...] = jnp.zeros_like(dv_acc)

        gd = i // ATT_GROUP
        r_io = lax.broadcasted_iota(jnp.int32, (t, t), 0)
        c_io = lax.broadcasted_iota(jnp.int32, (t, t), 1)
        after_mat = (r_io > c_io).astype(BF16)
        before_mat = (r_io < c_io).astype(BF16)
        qs = [q_ref[:, sl].astype(BF16) for sl in hsl]
        dos = [do_ref[:, sl].astype(BF16) for sl in hsl]
        q_t = q_ref[...].T.astype(BF16)
        do_t = do_ref[...].T.astype(BF16)

        def pass1(g, runs, mask):
            r0 = pl.multiple_of(g * gw, gw)
            zs = [_dot(qs[h], k_ref[pl.ds(r0, gw), hsl[h]], NT) * (scale * LOG2E) for h in range(2)]
            dws = [_dot(dos[h], v_ref[pl.ds(r0, gw), hsl[h]], NT) for h in range(2)]
            out = []
            for h in range(2):
                ls, w, run = _sb_group(zs[h], mask, runs[h], after_mat)
                g_scr[h, g] = dws[h] * w
                s_scr[h, g] = jnp.exp2(ls)
                dv_acc[g, hsl[h], :] += _dot(do_t[hsl[h]], w, NN)
                out.append(run)
            return tuple(out)

        diag_mask = _group_mask(i, gd, t, gw)
        zero_col = jnp.zeros((t, 1), F32)
        runs = pass1(gd, (zero_col, zero_col), diag_mask)

        def step1(st):
            jj, _, r = st
            r = pass1(gd - jj, r, None)
            return jj + 1, _alive(r[0], r[1]), r

        walked, _, _ = lax.while_loop(lambda st: jnp.logical_and(st[0] <= gd, st[1] > 0), step1,
                                      (jnp.int32(1), _alive(runs[0], runs[1]), runs))
        g_first = gd - (walked - 1)

        def pass2(g, carry, mask):
            r0 = pl.multiple_of(g * gw, gw)
            out = []
            for h in range(2):
                pre, dq = carry[h]
                gg = g_scr[h, g]
                sig = s_scr[h, g]
                before = []
                for v in _blocks(gg):
                    before.append(pre + _tri(v, before_mat))
                    pre = pre + jnp.sum(v, axis=1, keepdims=True)
                dz = gg * (1.0 - sig) - jnp.concatenate(before, axis=1) * sig
                if mask is not None:
                    dz = jnp.where(mask, dz, 0.0)
                dz = (dz * scale).astype(BF16)
                dk_acc[g, hsl[h], :] += _dot(q_t[hsl[h]], dz, NN)
                out.append((pre, dq + _dot(dz, k_ref[pl.ds(r0, gw), hsl[h]], NN)))
            return tuple(out)

        zero = (zero_col, jnp.zeros((t, HEAD_DIM), F32))
        carry = lax.fori_loop(g_first, gd, lambda g, c: pass2(g, c, None), (zero, zero))
        carry = pass2(gd, carry, diag_mask)
        for h in range(2):
            dq_ref[:, hsl[h]] = carry[h][1].astype(dq_ref.dtype)

        @pl.when(i == nq - 1)
        def _():
            for g in range(s // gw):
                dk_ref[g * gw:(g + 1) * gw, :] = dk_acc[g].T.astype(dk_ref.dtype)
                dv_ref[g * gw:(g + 1) * gw, :] = dv_acc[g].T.astype(dv_ref.dtype)

    qs_, ks_, vs_ = _att_specs(dm, s)
    tile_spec = pl.BlockSpec((t, LANES), lambda p, i: (i, p))
    full_spec = pl.BlockSpec((s, LANES), lambda p, i: (0, p))
    return _grid_call(
        body, grid=(dm.SBW // LANES, nq), in_specs=[qs_, ks_, vs_, tile_spec],
        out_specs=[tile_spec, full_spec, full_spec], out_shape=[SDS((s, dm.SBW), BF16)] * 3,
        scratch=[pltpu.VMEM((s // gw, LANES, gw), F32), pltpu.VMEM((s // gw, LANES, gw), F32),
                 pltpu.VMEM((2, s // gw, t, gw), F32), pltpu.VMEM((2, s // gw, t, gw), F32)],
        operands=(proj, proj, proj, do), name=name, rider=rider)


def adamw(w, g, m, v, name):
    rows, width = w.shape
    tile = _pick(rows, (256, 64, 16, 8, 4))
    c1 = 1.0 / (1.0 - ADAM_B1 ** ADAM_STEP)
    c2 = 1.0 / (1.0 - ADAM_B2 ** ADAM_STEP)

    def body(w_ref, g_ref, m_ref, v_ref, d_ref, nm_ref, nv_ref):
        gg = g_ref[...]
        nm = ADAM_B1 * m_ref[...] + (1.0 - ADAM_B1) * gg
        nv = ADAM_B2 * v_ref[...] + (1.0 - ADAM_B2) * (gg * gg)
        d_ref[...] = -ADAM_LR * ((nm * c1) / (jnp.sqrt(nv * c2) + ADAM_EPS) + ADAM_WD * w_ref[...])
        nm_ref[...] = nm
        nv_ref[...] = nv

    spec = pl.BlockSpec((tile, width), lambda i: (i, 0))
    return pl.pallas_call(
        body, grid=(rows // tile,), in_specs=[spec] * 4, out_specs=[spec] * 3,
        out_shape=[SDS((rows, width), F32)] * 3, compiler_params=_cparams(("parallel",)), name=name)(w, g, m, v)


def _me():
    return lax.axis_index("x"), lax.axis_index("y"), lax.axis_index("c")


def _other_chips(x, y):
    return [(1 - x, y), (x, 1 - y), (1 - x, 1 - y)]


def gather_weights(wp):
    rows, width = wp.shape
    half = rows // 2

    def body(w_ref, out_ref, send_sems, recv_sems):
        x, y, c = _me()
        sibling = (x, y, 1 - c)
        chips = _other_chips(x, y)

        def part(cx, cy, hf):
            return out_ref.at[2 * cx + cy, hf]

        def copy(k, src, dst, to):
            return pltpu.make_async_remote_copy(src_ref=src, dst_ref=dst, send_sem=send_sems.at[k], recv_sem=recv_sems.at[k],
                                                device_id=to, device_id_type=MESH_ID)

        first = [copy(j, w_ref.at[c], part(x, y, c), (cx, cy, c)) for j, (cx, cy) in enumerate(chips)]
        for cp in first:
            cp.start()
        passed = [copy(3 + j, part(cx, cy, c), part(cx, cy, c), sibling) for j, (cx, cy) in enumerate(chips)]
        for j, (cx, cy) in enumerate(chips):
            copy(j, part(cx, cy, c), part(cx, cy, c), (x, y, c)).wait_recv()
            passed[j].start()
        for j, (cx, cy) in enumerate(chips):
            copy(3 + j, part(cx, cy, 1 - c), part(cx, cy, 1 - c), (x, y, c)).wait_recv()
        for cp in first + passed:
            cp.wait_send()

    return pl.pallas_call(
        body, out_shape=SDS((N_CHIPS, 2, half, width), wp.dtype), in_specs=[HBM_SPEC], out_specs=HBM_SPEC,
        scratch_shapes=[pltpu.SemaphoreType.DMA((6,)), pltpu.SemaphoreType.DMA((6,))],
        name="gather_weights")(wp.reshape(2, half, width)).reshape(N_CHIPS, rows, width)


class Rider(NamedTuple):
    operands: tuple
    out_shape: tuple
    scratch: tuple
    start: object
    wait: object


def _with_rider(body, rider, grid, n_in, n_out):
    if rider is None:
        return body
    r_in, r_out = len(rider.operands), len(rider.out_shape)

    def full(*refs):
        ins, refs = refs[:n_in], refs[n_in:]
        rins, refs = refs[:r_in], refs[r_in:]
        outs, refs = refs[:n_out], refs[n_out:]
        routs, refs = refs[:r_out], refs[r_out:]
        scr, rscr = refs[:len(refs) - len(rider.scratch)], refs[len(refs) - len(rider.scratch):]
        ids = [pl.program_id(k) for k in range(len(grid))]
        first = functools.reduce(jnp.logical_and, [i == 0 for i in ids])
        last = functools.reduce(jnp.logical_and, [i == g - 1 for i, g in zip(ids, grid)])

        @pl.when(first)
        def _():
            rider.start(rins, routs, rscr)

        body(*ins, *outs, *scr)

        @pl.when(last)
        def _():
            rider.wait(rins, routs, rscr)

    return full


def gather_rider(wp):
    def copies(ins, outs, scr, sending):
        (w_ref,), (o_ref,), (send_sems, recv_sems) = ins, outs, scr
        x, y, c = _me()
        return [pltpu.make_async_remote_copy(src_ref=w_ref, dst_ref=o_ref.at[2 * x + y if sending else 2 * cx + cy],
                                             send_sem=send_sems.at[j], recv_sem=recv_sems.at[j], device_id=(cx, cy, c),
                                             device_id_type=MESH_ID)
                for j, (cx, cy) in enumerate(_other_chips(x, y))]

    def start(ins, outs, scr):
        for cp in copies(ins, outs, scr, True):
            cp.start()

    def wait(ins, outs, scr):
        for cp in copies(ins, outs, scr, False):
            cp.wait()

    return Rider((wp,), (SDS((N_CHIPS,) + wp.shape, wp.dtype),),
                 (pltpu.SemaphoreType.DMA((3,)), pltpu.SemaphoreType.DMA((3,))), start, wait)


def own_slab(gathered, wp, chip_idx):
    return lax.dynamic_update_slice(gathered, wp[None], (chip_idx[0], 0, 0))


def exchange_rider(sh):
    def copies(ins, outs, scr):
        (s_ref,), (b_ref,), (send_sems, recv_sems) = ins, outs, scr
        x, y, c = _me()
        return [pltpu.make_async_remote_copy(src_ref=s_ref.at[2 * cx + cy], dst_ref=b_ref.at[j], send_sem=send_sems.at[j],
                                             recv_sem=recv_sems.at[j], device_id=(cx, cy, c), device_id_type=MESH_ID)
                for j, (cx, cy) in enumerate(_other_chips(x, y))]

    def start(ins, outs, scr):
        for cp in copies(ins, outs, scr):
            cp.start()

    def wait(ins, outs, scr):
        for cp in copies(ins, outs, scr):
            cp.wait()

    return Rider((sh,), (SDS((3,) + sh.shape[1:], sh.dtype),),
                 (pltpu.SemaphoreType.DMA((3,)), pltpu.SemaphoreType.DMA((3,))), start, wait)


def swap_halves(g, name):
    n, rows, width = g.shape
    half = rows // 2

    def body(g_ref, a_ref, send_sem, recv_sem):
        x, y, c = _me()
        cp = pltpu.make_async_remote_copy(src_ref=g_ref.at[:, 1 - c], dst_ref=a_ref,
                                          send_sem=send_sem, recv_sem=recv_sem, device_id=(x, y, 1 - c), device_id_type=MESH_ID)
        cp.start()
        cp.wait()

    return pl.pallas_call(
        body, out_shape=SDS((n, half, width), g.dtype), in_specs=[HBM_SPEC], out_specs=HBM_SPEC,
        scratch_shapes=[pltpu.SemaphoreType.DMA, pltpu.SemaphoreType.DMA], name=name)(g.reshape(n, 2, half, width))


def add_half(g, a, c_idx, name):
    n, rows, width = g.shape
    half = rows // 2
    tile = half // 2
    nt = half // tile

    def body(c_ref, g_ref, a_ref, o_ref, ob_ref):
        v = g_ref[...] + a_ref[...]
        o_ref[...] = v
        ob_ref[...] = v.astype(ob_ref.dtype)

    out_spec = pl.BlockSpec((None, tile, width), lambda s, i, c_ref: (s, i, 0))
    gs = pltpu.PrefetchScalarGridSpec(
        num_scalar_prefetch=1, grid=(n, nt),
        in_specs=[pl.BlockSpec((None, tile, width), lambda s, i, c_ref: (s, c_ref[0] * nt + i, 0)), out_spec],
        out_specs=[out_spec, out_spec])
    return pl.pallas_call(body, grid_spec=gs, out_shape=[SDS((n, half, width), F32), SDS((n, half, width), BF16)],
                          compiler_params=_cparams(("parallel", "parallel")), name=name)(c_idx, g, a)


def exchange_small(small):
    def body(sm_ref, all_ref, send_sems, recv_sems, local_sem):
        x, y, c = _me()
        mine = pltpu.make_async_copy(sm_ref, all_ref.at[0], local_sem)
        mine.start()
        copies = []
        for m in range(1, N_DEV):
            peer = (x ^ ((m >> 2) & 1), y ^ ((m >> 1) & 1), c ^ (m & 1))
            copies.append(pltpu.make_async_remote_copy(
                src_ref=sm_ref, dst_ref=all_ref.at[m], send_sem=send_sems.at[m - 1], recv_sem=recv_sems.at[m - 1],
                device_id=peer, device_id_type=MESH_ID))
        for cp in copies:
            cp.start()
        for cp in copies:
            cp.wait()
        mine.wait()

    return pl.pallas_call(
        body, out_shape=SDS((N_DEV,) + small.shape, small.dtype), in_specs=[HBM_SPEC], out_specs=HBM_SPEC,
        scratch_shapes=[pltpu.SemaphoreType.DMA((N_DEV - 1,)), pltpu.SemaphoreType.DMA((N_DEV - 1,)), pltpu.SemaphoreType.DMA],
        name="exchange_small")(small)


def add_chips(sh, b, k_idx, name):
    n, hf, width = sh.shape
    tile = hf // 2

    def body(k_ref, s_ref, b0, b1, b2, o_ref):
        o_ref[...] = ((s_ref[...] + b0[...].astype(F32)) + b1[...].astype(F32)) + b2[...].astype(F32)

    def bspec(j):
        return pl.BlockSpec((None, tile, width), lambda i, k_ref, j=j: (j, i, 0))

    gs = pltpu.PrefetchScalarGridSpec(
        num_scalar_prefetch=1, grid=(hf // tile,),
        in_specs=[pl.BlockSpec((None, tile, width), lambda i, k_ref: (k_ref[0], i, 0)), bspec(0), bspec(1), bspec(2)],
        out_specs=pl.BlockSpec((tile, width), lambda i, k_ref: (i, 0)))
    return pl.pallas_call(body, grid_spec=gs, out_shape=SDS((hf, width), sh.dtype),
                          compiler_params=_cparams(("parallel",)), name=name)(k_idx, sh, b, b, b)


def sum_small(allsm, me_idx, name):
    _, rows, width = allsm.shape

    def body(me_ref, a_ref, o_ref):
        me = me_ref[0]
        acc = a_ref[me]
        for dev in range(1, N_DEV):
            acc = acc + a_ref[jnp.bitwise_xor(me, dev)]
        o_ref[...] = acc

    gs = pltpu.PrefetchScalarGridSpec(
        num_scalar_prefetch=1, grid=(1,),
        in_specs=[pl.BlockSpec((N_DEV, rows, width), lambda i, me_ref: (0, 0, 0))],
        out_specs=pl.BlockSpec((rows, width), lambda i, me_ref: (0, 0)))
    return pl.pallas_call(body, grid_spec=gs, out_shape=SDS((rows, width), allsm.dtype),
                          compiler_params=_cparams(("arbitrary",)), name=name)(me_idx, allsm)


def join_halves(t, core_idx, name):
    hf, width = t.shape

    def body(t_ref, o_ref, send_sem, recv_sem):
        x, y, c = _me()
        cp = pltpu.make_async_remote_copy(src_ref=t_ref, dst_ref=o_ref, send_sem=send_sem, recv_sem=recv_sem,
                                          device_id=(x, y, 1 - c), device_id_type=MESH_ID)
        cp.start()
        cp.wait()

    theirs = pl.pallas_call(
        body, out_shape=SDS((hf, width), t.dtype), in_specs=[HBM_SPEC], out_specs=HBM_SPEC,
        scratch_shapes=[pltpu.SemaphoreType.DMA, pltpu.SemaphoreType.DMA], name=name)(t)
    return jnp.where(core_idx[0] == 0, jnp.concatenate([t, theirs], axis=0), jnp.concatenate([theirs, t], axis=0))


ROW_PAD = 64


def _pad_rows(rows):
    return -(-rows // ROW_PAD) * ROW_PAD


def _pack_rows(names, shard_shapes, width):
    return _pad_rows(sum((shard_shapes[n][0] * shard_shapes[n][1]) // width for n in names))


def unpack_local(packed, names, shard_shapes, width):
    out, r0 = {}, 0
    for n in names:
        a, b = shard_shapes[n]
        nr = (a * b) // width
        out[n] = packed[r0:r0 + nr].reshape(a, b)
        r0 += nr
    return out


EXACT_IN_GATHER = ("conv_w",)
EXACT_TERMS = 3


def pack_gather(shards, names, width):
    parts = []
    for n in names:
        if n in EXACT_IN_GATHER:
            rest = shards[n].astype(F32)
            for _ in range(EXACT_TERMS):
                term = rest.astype(BF16)
                parts.append(term.reshape(-1, width))
                rest = rest - term.astype(F32)
        else:
            parts.append(shards[n].reshape(-1, width).astype(BF16))
    used = sum(p.shape[0] for p in parts)
    parts.append(jnp.zeros((_pad_rows(used) - used, width), BF16))
    return jnp.concatenate(parts, axis=0)


def unpack_full(gathered, names, shard_shapes, width):
    out, r0 = {}, 0
    for n in names:
        a, b = shard_shapes[n]
        terms = EXACT_TERMS if n in EXACT_IN_GATHER else 1
        nr = (a * b) // width
        pieces = []
        for j in range(N_CHIPS):
            blk = gathered[j, r0:r0 + nr].reshape(a, b)
            for t in range(1, terms):
                blk = blk.astype(F32) + gathered[j, r0 + t * nr:r0 + (t + 1) * nr].reshape(a, b).astype(F32)
            pieces.append(blk)
        out[n] = jnp.concatenate(pieces, axis=SHARD_AXIS[n])
        r0 += terms * nr
    return out


def pack_full(grads, names, shard_shapes, width):
    total_rows = _pack_rows(names, shard_shapes, width)
    slabs = []
    for j in range(N_CHIPS):
        parts = []
        for n in names:
            a, b = shard_shapes[n]
            ax = SHARD_AXIS[n]
            sz = (a, b)[ax]
            piece = lax.slice_in_dim(grads[n], j * sz, (j + 1) * sz, axis=ax)
            parts.append(piece.reshape(-1, width))
        used = sum(p.shape[0] for p in parts)
        parts.append(jnp.zeros((total_rows - used, width), F32))
        slabs.append(jnp.concatenate(parts, axis=0))
    return jnp.stack(slabs, axis=0)


def _small_layout(sizes, width):
    lay, r = {}, 0
    for n in SMALL_WEIGHTS:
        nr = -(-sizes[n] // width)
        lay[n] = (r, nr, sizes[n])
        r += nr
    assert r <= SMALL_ROWS
    return lay


def pack_small(vals, lay, width):
    rows = []
    for n in SMALL_WEIGHTS:
        r, nr, sz = lay[n]
        v = vals[n].reshape(-1).astype(F32)
        rows.append(jnp.pad(v, (0, nr * width - sz)).reshape(nr, width))
    used = sum(r.shape[0] for r in rows)
    rows.append(jnp.zeros((SMALL_ROWS - used, width), F32))
    return jnp.concatenate(rows, axis=0)


def unpack_small(packed, lay):
    return {n: packed[r:r + nr].reshape(-1)[:sz].reshape(1, sz) for n, (r, nr, sz) in lay.items()}


PART_B = ("w_in", "w_gate")
PART_A = tuple(n for n in BIG_WEIGHTS if n not in PART_B)


class Dist(NamedTuple):
    packed_a: object
    shard_shapes: dict
    core_idx: object
    chip_idx: object


def build_w_all(w_in, w_gate, dm):
    c0 = dm.DI + dm.CD
    return jnp.concatenate(
        [w_in[:, :c0], w_in[:, c0 + dm.H:], w_gate, w_in[:, c0:c0 + dm.H],
         jnp.zeros((dm.D, DT_PAD - dm.H), w_in.dtype)], axis=1).astype(BF16)


def pack_b(w_in_shard, w_gate_shard):
    return jnp.concatenate([w_in_shard, w_gate_shard], axis=1)


def unpack_b(slabs, dm):
    n_in = (dm.NA - DT_PAD - 2 * dm.D + dm.H) // N_CHIPS
    return (jnp.concatenate([slabs[j, :, :n_in] for j in range(N_CHIPS)], axis=1),
            jnp.concatenate([slabs[j, :, n_in:] for j in range(N_CHIPS)], axis=1))


def slabs_b(dw_all, dm):
    c0 = dm.DI + dm.CD
    off = dm.off
    dw_in = jnp.concatenate([dw_all[:, :c0], dw_all[:, off["dt"]:off["dt"] + dm.H], dw_all[:, c0:c0 + 3 * dm.SBW]], axis=1)
    dw_gate = dw_all[:, off["gate"]:off["gate"] + 2 * dm.D]
    n_in, n_gate = dw_in.shape[1] // N_CHIPS, dw_gate.shape[1] // N_CHIPS
    return jnp.stack([pack_b(dw_in[:, j * n_in:(j + 1) * n_in], dw_gate[:, j * n_gate:(j + 1) * n_gate])
                      for j in range(N_CHIPS)], axis=0)


def local_step(x, p, tgt, wf, sm, dm, dist=None):
    s, d = dm.S, dm.D
    off = dm.off
    c0 = dm.DI + dm.CD
    w_all = wf["w_all"] if "w_all" in wf else build_w_all(wf["w_in"], wf["w_gate"], dm)
    g = dm.G
    per_group = lambda v: v.reshape(g, 1, GROUP_HEADS)
    alog, dsk = per_group(sm["a_log"]), per_group(sm["d_skip"])
    b_gate = sm["b_gate"]
    b_ssd, b_sb = b_gate[:, :d], b_gate[:, d:]
    gcol = off["gate"] // d

    (n1,) = row_fwd("norm1", f_norm1, [x], [sm["norm_mix_pre"]], [(d, BF16)])
    proj = matmul(n1, w_all, name="in_proj")
    if dist is None:
        y_sb = attn_fwd(proj, dm, "attn_fwd")
    else:
        y_sb, gathered_a = attn_fwd(proj, dm, "attn_fwd", rider=gather_rider(dist.packed_a))
        gathered_a = own_slab(gathered_a, dist.packed_a, dist.chip_idx)
        wf = {**wf, **unpack_full(gathered_a, PART_A, dist.shard_shapes, d)}
    xbc = conv_fwd(proj, off["xbc"], dm.CD, wf["conv_w"].astype(F32), sm["conv_b"], "conv_fwd")
    dt_raw = proj[:, off["dt"]:off["dt"] + dm.H]
    (dt,) = row_fwd("dt", f_dt, [dt_raw], [sm["dt_bias"]], [(dm.H, F32)])
    dtc = dt.reshape(s, g, GROUP_HEADS).transpose(1, 0, 2)
    dtw = dt.reshape(s, g, GROUP_HEADS).transpose(1, 2, 0)
    y_ssd, states = ssd_fwd(proj, xbc, dtc, dtw, alog, dsk, sm["ssd_norm"], dm, "ssd_fwd")
    yb_ssd = matmul(y_ssd, wf["w_ssd_branch"], name="ssd_branch")
    yb_sb = matmul(y_sb, wf["w_sb_branch"], name="sb_branch")
    merge_rows = [(proj, d, gcol), (proj, d, gcol + 1), yb_ssd, yb_sb]
    (merged,) = row_fwd("merge", f_merge, merge_rows, [b_ssd, b_sb], [(d, BF16)])
    mo = matmul(merged, wf["w_out"], name="w_out")
    h1, n2 = row_fwd("mix_out", f_mix_out, [x, mo], [sm["norm_mix_post"], sm["norm_ffn_pre"]], [(d, F32), (d, BF16)])
    a1 = matmul(n2, wf["w_ff1"], name="ff1")
    (act,) = row_fwd("relu2", f_relu2, [a1], [], [(dm.DFF, BF16)])
    ff = matmul(act, wf["w_ff2"], name="ff2")
    (h2,) = row_fwd("ffn_out", f_ffn_out, [h1, ff], [sm["norm_ffn_post"]], [(d, F32)])
    pg = matmul(h2, wf["w_ple_gate"], name="ple_gate")
    pe = matmul(p, wf["w_ple"], name="ple_emb")

    def reduce_start(gbig, tag):
        from_sibling = swap_halves(gbig, "swap_halves_" + tag)
        pair_sum, pair_sum_bf16 = add_half(gbig, from_sibling, dist.core_idx, "add_half_" + tag)
        return pair_sum, exchange_rider(pair_sum_bf16)

    def reduce_finish(pair_sum, from_chips, tag):
        my_half = add_chips(pair_sum, from_chips, dist.chip_idx, "add_chips_" + tag)
        return join_halves(my_half, dist.core_idx, "join_halves_" + tag)

    gr, reduced = {}, None
    (dh2_a, dpg, dpe), (gr["norm_ple_post"], loss_cols) = row_bwd(
        "ple_loss", f_ple_loss, [h2, pg, pe, tgt], [sm["norm_ple_post"]], [None], [F32, BF16, BF16, None], primal_sum=True)
    loss = jnp.sum(loss_cols)
    gr["w_ple"] = matmul(p, dpe, ta=True, name="d_w_ple")
    gr["w_ple_gate"] = matmul(h2, dpg, ta=True, name="d_w_ple_gate")
    dh2_b = matmul(dpg, wf["w_ple_gate"], tb=True, name="d_h2")
    (dh1_a, dff), (gr["norm_ffn_post"],) = row_bwd(
        "ffn_out_bwd", f_ffn_out, [h1, ff], [sm["norm_ffn_post"]], [[dh2_a, dh2_b]], [F32, BF16])
    gr["w_ff2"] = matmul(act, dff, ta=True, name="d_w_ff2")
    dact = matmul(dff, wf["w_ff2"], tb=True, name="d_act")
    (da1,), _ = row_bwd("relu2_bwd", f_relu2, [a1], [], [[dact]], [BF16])
    gr["w_ff1"] = matmul(n2, da1, ta=True, name="d_w_ff1")
    dn2 = matmul(da1, wf["w_ff1"], tb=True, name="d_n2")
    (dx_a, dmo), (gr["norm_mix_post"], gr["norm_ffn_pre"]) = row_bwd(
        "mix_out_bwd", f_mix_out, [x, mo], [sm["norm_mix_post"], sm["norm_ffn_pre"]], [[dh1_a], [dn2]], [F32, BF16])
    gr["w_out"] = matmul(merged, dmo, ta=True, name="d_w_out")
    dmerged = matmul(dmo, wf["w_out"], tb=True, name="d_merged")
    (dgp_ssd, dgp_sb, dyb_ssd, dyb_sb), (db_ssd, db_sb) = row_bwd(
        "merge_bwd", f_merge, merge_rows, [b_ssd, b_sb], [[dmerged]], [BF16, BF16, BF16, BF16])
    gr["b_gate"] = jnp.concatenate([db_ssd, db_sb], axis=1)
    gr["w_ssd_branch"] = matmul(y_ssd, dyb_ssd, ta=True, name="d_w_ssd_branch")
    gr["w_sb_branch"] = matmul(y_sb, dyb_sb, ta=True, name="d_w_sb_branch")
    dy_ssd = matmul(dyb_ssd, wf["w_ssd_branch"], tb=True, name="d_y_ssd")
    dy_sb = matmul(dyb_sb, wf["w_sb_branch"], tb=True, name="d_y_sb")
    dz, dxs, dbm, dcm, ddtc, ddtw, dalog, ddsk, gr["ssd_norm"] = ssd_bwd(
        proj, xbc, dtc, dtw, alog, dsk, sm["ssd_norm"], states, dy_ssd, dm, "ssd_bwd")
    gr["a_log"], gr["d_skip"] = (v.reshape(1, dm.H) for v in (dalog, ddsk))
    ddt_post = (ddtc.transpose(1, 0, 2) + ddtw.transpose(2, 0, 1)).reshape(s, dm.H)
    (ddt,), (gr["dt_bias"],) = row_bwd("dt_bwd", f_dt, [dt_raw], [sm["dt_bias"]], [[ddt_post]], [BF16])
    conv_w32 = wf["conv_w"].astype(F32)
    du_x, dw_x, dcb_x = conv_bwd(proj, off["xbc"], dxs, 0, conv_w32, sm["conv_b"], "conv_bwd_x")
    du_b, dw_b, dcb_b = conv_bwd(proj, off["xbc"], dbm, dm.DI, conv_w32, sm["conv_b"], "conv_bwd_b")
    du_c, dw_c, dcb_c = conv_bwd(proj, off["xbc"], dcm, dm.DI + g * D_STATE, conv_w32, sm["conv_b"], "conv_bwd_c")
    gr["conv_w"] = jnp.concatenate([dw_x, dw_b, dw_c], axis=1)
    gr["conv_b"] = jnp.concatenate([dcb_x, dcb_b, dcb_c], axis=1)
    if dist is None:
        dq, dk, dv = attn_bwd(proj, dy_sb, dm, "attn_bwd")
    else:
        pair_sum_a, rider_a = reduce_start(pack_full(gr, PART_A, dist.shard_shapes, d), "a")
        dq, dk, dv, from_chips_a = attn_bwd(proj, dy_sb, dm, "attn_bwd", rider=rider_a)
        reduced = unpack_local(reduce_finish(pair_sum_a, from_chips_a, "a"), PART_A, dist.shard_shapes, d)
    dproj = jnp.concatenate(
        [dz, du_x, du_b, du_c, dq, dk, dv, dgp_ssd, dgp_sb, ddt, jnp.zeros((s, DT_PAD - dm.H), BF16)], axis=1)
    dw_all = matmul(n1, dproj, ta=True, name="d_w_all")
    if dist is None:
        gr["w_in"] = jnp.concatenate(
            [dw_all[:, :c0], dw_all[:, off["dt"]:off["dt"] + dm.H], dw_all[:, c0:c0 + 3 * dm.SBW]], axis=1)
        gr["w_gate"] = dw_all[:, off["gate"]:off["gate"] + 2 * d]
        dn1 = matmul(dproj, w_all, tb=True, name="d_n1")
    else:
        pair_sum_b, rider_b = reduce_start(slabs_b(dw_all, dm), "b")
        dn1, from_chips_b = matmul(dproj, w_all, tb=True, name="d_n1", rider=rider_b)
        reduced_b = reduce_finish(pair_sum_b, from_chips_b, "b")
        n_in = dist.shard_shapes["w_in"][1]
        reduced.update(w_in=reduced_b[:, :n_in], w_gate=reduced_b[:, n_in:])
    (dx_b,), (gr["norm_mix_pre"],) = row_bwd("norm1_bwd", f_norm1, [x], [sm["norm_mix_pre"]], [[dn1]], [F32])
    (grad_x,) = row_fwd("grad_x", lambda u, v: (u + v,), [dx_a, dx_b], [], [(d, F32)])
    return loss, grad_x, gr, reduced


def kernel(x, p, norm_mix_pre, w_in, conv_w, conv_b, dt_bias, a_log, d_skip, ssd_norm, w_ssd_branch, w_sb_branch, w_gate, b_gate, w_out, norm_mix_post, norm_ffn_pre, w_ff1, w_ff2, norm_ffn_post, w_ple, w_ple_gate, norm_ple_post, loss_target, m_norm_mix_pre, m_w_in, m_conv_w, m_conv_b, m_dt_bias, m_a_log, m_d_skip, m_ssd_norm, m_w_ssd_branch, m_w_sb_branch, m_w_gate, m_b_gate, m_w_out, m_norm_mix_post, m_norm_ffn_pre, m_w_ff1, m_w_ff2, m_norm_ffn_post, m_w_ple, m_w_ple_gate, m_norm_ple_post, v_norm_mix_pre, v_w_in, v_conv_w, v_conv_b, v_dt_bias, v_a_log, v_d_skip, v_ssd_norm, v_w_ssd_branch, v_w_sb_branch, v_w_gate, v_b_gate, v_w_out, v_norm_mix_post, v_norm_ffn_pre, v_w_ff1, v_w_ff2, v_norm_ffn_post, v_w_ple, v_w_ple_gate, v_norm_ple_post):
    loc = dict(locals())
    unbatch = lambda a: a[0] if a.ndim == 3 else a
    w = {n: unbatch(loc[n]) for n in ALL_WEIGHTS}
    m = {n: unbatch(loc["m_" + n]) for n in ALL_WEIGHTS}
    v = {n: unbatch(loc["v_" + n]) for n in ALL_WEIGHTS}
    xs, ps, tgt = x[0], p[0, 0], loss_target[0]
    s, d = xs.shape
    di = w["w_ssd_branch"].shape[0] * N_CHIPS
    cd = w["conv_b"].shape[1]
    dm = Dims(S=s, D=d, DI=di, H=w["dt_bias"].shape[1], G=(cd - di) // (2 * D_STATE), CD=cd,
              SBW=w["w_sb_branch"].shape[0] * N_CHIPS, DFF=w["w_ff2"].shape[0] * N_CHIPS, PLE=ps.shape[1])
    ix, iy, ic = lax.axis_index("x"), lax.axis_index("y"), lax.axis_index("c")
    chip_idx = jnp.reshape(2 * ix + iy, (1,)).astype(jnp.int32)
    core_idx = jnp.reshape(ic, (1,)).astype(jnp.int32)
    dev_idx = jnp.reshape(4 * ix + 2 * iy + ic, (1,)).astype(jnp.int32)

    shard_shapes = {n: w[n].shape for n in BIG_WEIGHTS}
    packed_b = pack_b(w["w_in"], w["w_gate"]).astype(BF16)
    wf = {"w_all": build_w_all(*unpack_b(own_slab(gather_weights(packed_b), packed_b, chip_idx), dm), dm)}
    sm = {n: w[n] for n in SMALL_WEIGHTS}
    dist = Dist(packed_a=pack_gather(w, PART_A, d), shard_shapes=shard_shapes, core_idx=core_idx, chip_idx=chip_idx)

    loss_part, grad_x, gr, grads = local_step(xs, ps, tgt, wf, sm, dm, dist)
    loss = lax.psum(loss_part, ("x", "y", "c"))

    lay = _small_layout({n: w[n].shape[1] for n in SMALL_WEIGHTS}, d)
    gs_red = sum_small(exchange_small(pack_small(gr, lay, d)), dev_idx, "sum_small")
    grads.update(unpack_small(gs_red, lay))
    delta, new_m, new_v = {}, {}, {}
    for n in BIG_WEIGHTS:
        delta[n], new_m[n], new_v[n] = adamw(w[n], grads[n], m[n], v[n], "adamw_" + n)
    d_sm, nm_sm, nv_sm = adamw(pack_small(w, lay, d), gs_red, pack_small(m, lay, d), pack_small(v, lay, d), "adamw_small")
    for out, packed in ((delta, d_sm), (new_m, nm_sm), (new_v, nv_sm)):
        out.update(unpack_small(packed, lay))

    def leaves(vals):
        return [vals[n][None] if n in BIG_WEIGHTS else vals[n] for n in ALL_WEIGHTS]

    return (loss, grad_x[None], *leaves(grads), *leaves(delta), *leaves(new_m), *leaves(new_v))
```

```python
import functools
from typing import NamedTuple

import jax
import jax.numpy as jnp
from jax import lax
from jax.experimental import pallas as pl
from jax.experimental.pallas import tpu as pltpu

F32 = jnp.float32
BF16 = jnp.bfloat16
SDS = jax.ShapeDtypeStruct

HEAD_DIM = 64
GROUP_HEADS = 4
D_STATE = 128
CHUNK = 128
ATT_TILE = 128
ATT_GROUP = 4
ATT_DEAD = -160.0
LOG2E = 1.4426950408889634
CONV_K = 4
CONV_COLS = 128
RMS_EPS = 1e-6
LANES = 128
DT_PAD = 512
N_CHIPS = 4
N_DEV = 8
SMALL_ROWS = 16
VMEM_LIMIT = 48 * 1024 * 1024
MAX_TK = 3072

ADAM_LR = 0.001
ADAM_B1 = 0.9
ADAM_B2 = 0.999
ADAM_EPS = 1e-08
ADAM_WD = 0.01
ADAM_STEP = 10

MESH_ID = pl.DeviceIdType.MESH
HBM_SPEC = pl.BlockSpec(memory_space=pltpu.HBM)

BIG_WEIGHTS = ("w_in", "conv_w", "w_ssd_branch", "w_sb_branch", "w_gate", "w_out", "w_ff1", "w_ff2", "w_ple", "w_ple_gate")
SHARD_AXIS = {"w_in": 1, "conv_w": 1, "w_ssd_branch": 0, "w_sb_branch": 0, "w_gate": 1, "w_out": 0, "w_ff1": 1,
              "w_ff2": 0, "w_ple": 1, "w_ple_gate": 0}
SMALL_WEIGHTS = ("norm_mix_pre", "conv_b", "dt_bias", "a_log", "d_skip", "ssd_norm", "b_gate", "norm_mix_post",
                 "norm_ffn_pre", "norm_ffn_post", "norm_ple_post")
ALL_WEIGHTS = ("norm_mix_pre", "w_in", "conv_w", "conv_b", "dt_bias", "a_log", "d_skip", "ssd_norm", "w_ssd_branch",
               "w_sb_branch", "w_gate", "b_gate", "w_out", "norm_mix_post", "norm_ffn_pre", "w_ff1", "w_ff2",
               "norm_ffn_post", "w_ple", "w_ple_gate", "norm_ple_post")


class Dims(NamedTuple):
    S: int
    D: int
    DI: int
    H: int
    G: int
    CD: int
    SBW: int
    DFF: int
    PLE: int

    @property
    def NA(self):
        return self.DI + self.CD + 3 * self.SBW + 2 * self.D + DT_PAD

    @property
    def off(self):
        o = {}
        o["z"] = 0
        o["xbc"] = self.DI
        o["q"] = self.DI + self.CD
        o["k"] = o["q"] + self.SBW
        o["v"] = o["k"] + self.SBW
        o["gate"] = o["v"] + self.SBW
        o["dt"] = o["gate"] + 2 * self.D
        return o


def _cparams(sem):
    return pltpu.CompilerParams(dimension_semantics=sem, vmem_limit_bytes=VMEM_LIMIT)


def _pick(n, cands):
    for c in cands:
        if n % c == 0:
            return c
    raise ValueError(f"no tile for {n}")


def _grid_call(body, *, grid, in_specs, out_specs, out_shape, scratch, operands, name, rider=None):
    if rider is None:
        sem = ("parallel",) + ("arbitrary",) * (len(grid) - 1)
        return pl.pallas_call(body, grid=grid, in_specs=in_specs, out_specs=out_specs, out_shape=out_shape,
                              scratch_shapes=scratch, compiler_params=_cparams(sem), name=name)(*operands)
    return pl.pallas_call(
        _with_rider(body, rider, grid, len(in_specs), len(out_specs)), grid=grid,
        in_specs=list(in_specs) + [HBM_SPEC] * len(rider.operands),
        out_specs=list(out_specs) + [HBM_SPEC] * len(rider.out_shape),
        out_shape=list(out_shape) + list(rider.out_shape), scratch_shapes=list(scratch) + list(rider.scratch),
        compiler_params=_cparams(("arbitrary",) * len(grid)), name=name)(*operands, *rider.operands)


def matmul(a, b, *, ta=False, tb=False, out_dtype=F32, name, rider=None):
    m, k = (a.shape[1], a.shape[0]) if ta else a.shape
    n, kb = b.shape if tb else (b.shape[1], b.shape[0])
    assert k == kb, (a.shape, b.shape, ta, tb)
    tm = _pick(m, (1024, 512, 256, 128))
    tn = _pick(n, (512, 256, 128))
    tk = max(t for t in range(LANES, min(k, MAX_TK) + 1, LANES) if k % t == 0)
    nk = k // tk
    dims = (((0 if ta else 1,), (1 if tb else 0,)), ((), ()))

    def body(a_ref, b_ref, o_ref, acc_ref):
        part = lax.dot_general(a_ref[...].astype(BF16), b_ref[...].astype(BF16), dims, preferred_element_type=F32)
        if nk == 1:
            o_ref[...] = part.astype(o_ref.dtype)
        else:
            kk = pl.program_id(2)

            @pl.when(kk == 0)
            def _():
                acc_ref[...] = part

            @pl.when(kk > 0)
            def _():
                acc_ref[...] += part

            @pl.when(kk == nk - 1)
            def _():
                o_ref[...] = acc_ref[...].astype(o_ref.dtype)

    a_spec = pl.BlockSpec((tk, tm), lambda i, j, kk: (kk, i)) if ta else pl.BlockSpec((tm, tk), lambda i, j, kk: (i, kk))
    b_spec = pl.BlockSpec((tn, tk), lambda i, j, kk: (j, kk)) if tb else pl.BlockSpec((tk, tn), lambda i, j, kk: (kk, j))
    res = _grid_call(body, grid=(m // tm, n // tn, nk), in_specs=[a_spec, b_spec],
                     out_specs=[pl.BlockSpec((tm, tn), lambda i, j, kk: (i, j))], out_shape=[SDS((m, n), out_dtype)],
                     scratch=[pltpu.VMEM((tm, tn), F32)], operands=(a, b), name=name, rider=rider)
    return res[0] if rider is None else res


def _row_spec(entry, tile):
    arr, width, cb = entry if isinstance(entry, tuple) else (entry, entry.shape[1], 0)
    return arr, pl.BlockSpec((tile, width), lambda i, cb=cb: (i, cb))


def _par_spec(p):
    return pl.BlockSpec(p.shape, lambda i: (0, 0))


def row_fwd(name, fn, rows, params, outs, tile=256):
    arrs, specs = zip(*[_row_spec(e, tile) for e in rows])
    s = arrs[0].shape[0]
    nr, npar = len(rows), len(params)

    def body(*refs):
        r = [x[...].astype(F32) for x in refs[:nr]]
        p = [x[...] for x in refs[nr:nr + npar]]
        res = fn(*r, *p)
        for o_ref, val in zip(refs[nr + npar:], res):
            o_ref[...] = val.astype(o_ref.dtype)

    return pl.pallas_call(
        body, grid=(s // tile,), in_specs=list(specs) + [_par_spec(p) for p in params],
        out_specs=[pl.BlockSpec((tile, w), lambda i: (i, 0)) for w, _ in outs],
        out_shape=[SDS((s, w), dt) for w, dt in outs],
        compiler_params=_cparams(("parallel",)), name=name)(*arrs, *params)


def row_bwd(name, fn, rows, params, cots, row_grads, tile=256, primal_sum=False):
    arrs, specs = zip(*[_row_spec(e, tile) for e in rows])
    s = arrs[0].shape[0]
    nr, npar = len(rows), len(params)
    cot_entries = [e for c in cots if c is not None for e in c]
    carrs, cspecs = zip(*[_row_spec(e, tile) for e in cot_entries]) if cot_entries else ((), ())
    nc = len(cot_entries)
    want = [i for i, d in enumerate(row_grads) if d is not None]

    def body(*refs):
        r = [x[...].astype(F32) for x in refs[:nr]]
        p = [x[...] for x in refs[nr:nr + npar]]
        cvals = [x[...].astype(F32) for x in refs[nr + npar:nr + npar + nc]]
        outs = refs[nr + npar + nc:]
        prim, vjp = jax.vjp(fn, *r, *p)
        ct, pos = [], 0
        for c, pr in zip(cots, prim):
            if c is None:
                ct.append(jnp.ones_like(pr))
            else:
                acc = cvals[pos]
                for extra in cvals[pos + 1:pos + len(c)]:
                    acc = acc + extra
                pos += len(c)
                ct.append(acc)
        grads = vjp(tuple(ct))
        for o_ref, i in zip(outs[:len(want)], want):
            o_ref[...] = grads[i].astype(o_ref.dtype)
        acc_refs = outs[len(want):]
        vals = [grads[nr + j] for j in range(npar)]
        if primal_sum:
            vals.append(jnp.sum(prim[0], axis=0, keepdims=True))
        first = pl.program_id(0) == 0

        @pl.when(first)
        def _():
            for a_ref, v in zip(acc_refs, vals):
                a_ref[...] = v

        @pl.when(jnp.logical_not(first))
        def _():
            for a_ref, v in zip(acc_refs, vals):
                a_ref[...] += v

    widths = [(e[1] if isinstance(e, tuple) else e.shape[1]) for e in rows]
    out_specs = [pl.BlockSpec((tile, widths[i]), lambda i_: (i_, 0)) for i in want]
    out_shape = [SDS((s, widths[i]), row_grads[i]) for i in want]
    pshapes = [p.shape for p in params]
    if primal_sum:
        pshapes.append((1, widths[0]))
    out_specs += [pl.BlockSpec(sh, lambda i_: (0, 0)) for sh in pshapes]
    out_shape += [SDS(sh, F32) for sh in pshapes]
    res = pl.pallas_call(
        body, grid=(s // tile,), in_specs=list(specs) + [_par_spec(p) for p in params] + list(cspecs),
        out_specs=out_specs, out_shape=out_shape,
        compiler_params=_cparams(("arbitrary",)), name=name)(*arrs, *params, *carrs)
    return res[:len(want)], res[len(want):]


def _rms(x, w):
    return x * lax.rsqrt(jnp.mean(x * x, axis=-1, keepdims=True) + RMS_EPS) * w


def _sigmoid(x):
    return jax.nn.sigmoid(x)


def _softplus(x):
    return jnp.maximum(x, 0.0) + jnp.log1p(jnp.exp(-jnp.abs(x)))


def f_norm1(x, w):
    return (_rms(x, w),)


def f_dt(raw, bias):
    return (_softplus(raw + bias),)


def f_merge(gp_ssd, gp_sb, yb_ssd, yb_sb, b_ssd, b_sb):
    return (_sigmoid(gp_ssd + b_ssd) * yb_ssd + _sigmoid(gp_sb + b_sb) * yb_sb,)


def f_mix_out(x, mo, w_post, w_pre):
    h1 = x + _rms(mo, w_post)
    return h1, _rms(h1, w_pre)


def f_relu2(a1):
    return (jnp.square(jnp.maximum(a1, 0.0)),)


def f_ffn_out(h1, ff, w):
    return (h1 + _rms(ff, w),)


def f_ple_loss(h2, pg, pe, tgt, w):
    h3 = h2 + _rms(_sigmoid(pg) * pe, w)
    return (0.5 * jnp.square(h3 - tgt) * (1.0 / h2.shape[-1]),)


def _shift_down(u, d, rows):
    return u if d == 0 else jnp.where(rows >= d, pltpu.roll(u, d, 0), 0.0)


def _shift_up(u, d, rows):
    s = u.shape[0]
    return u if d == 0 else jnp.where(rows < s - d, pltpu.roll(u, s - d, 0), 0.0)


def conv_fwd(proj, col0, cd, conv_w, conv_b, name):
    s = proj.shape[0]
    cb0 = col0 // CONV_COLS

    def body(u_ref, w_ref, b_ref, o_ref):
        u = u_ref[...]
        rows = lax.broadcasted_iota(jnp.int32, u.shape, 0)
        y = jnp.broadcast_to(b_ref[...], u.shape)
        for k in range(CONV_K):
            y = y + w_ref[k:k + 1, :] * _shift_down(u, CONV_K - 1 - k, rows)
        o_ref[...] = y * _sigmoid(y)

    return pl.pallas_call(
        body, grid=(cd // CONV_COLS,),
        in_specs=[pl.BlockSpec((s, CONV_COLS), lambda i: (0, cb0 + i)),
                  pl.BlockSpec((CONV_K, CONV_COLS), lambda i: (0, i)),
                  pl.BlockSpec((1, CONV_COLS), lambda i: (0, i))],
        out_specs=pl.BlockSpec((s, CONV_COLS), lambda i: (0, i)),
        out_shape=SDS((s, cd), F32), compiler_params=_cparams(("parallel",)), name=name)(proj, conv_w, conv_b)


def conv_bwd(proj, col0, dout, ch0, conv_w, conv_b, name):
    s = proj.shape[0]
    ncb = dout.shape[1] // CONV_COLS
    cb0 = (col0 + ch0) // CONV_COLS
    wb0 = ch0 // CONV_COLS

    def body(u_ref, g_ref, w_ref, b_ref, du_ref, dw_ref, db_ref):
        u = u_ref[...]
        rows = lax.broadcasted_iota(jnp.int32, u.shape, 0)
        y = jnp.broadcast_to(b_ref[...], u.shape)
        for k in range(CONV_K):
            y = y + w_ref[k:k + 1, :] * _shift_down(u, CONV_K - 1 - k, rows)
        sg = _sigmoid(y)
        dy = g_ref[...] * (sg * (1.0 + y * (1.0 - sg)))
        du = jnp.zeros_like(u)
        for k in range(CONV_K):
            d = CONV_K - 1 - k
            du = du + w_ref[k:k + 1, :] * _shift_up(dy, d, rows)
            dw_ref[k:k + 1, :] = jnp.sum(dy * _shift_down(u, d, rows), axis=0, keepdims=True)
        du_ref[...] = du.astype(du_ref.dtype)
        db_ref[...] = jnp.sum(dy, axis=0, keepdims=True)

    return pl.pallas_call(
        body, grid=(ncb,),
        in_specs=[pl.BlockSpec((s, CONV_COLS), lambda i: (0, cb0 + i)),
                  pl.BlockSpec((s, CONV_COLS), lambda i: (0, i)),
                  pl.BlockSpec((CONV_K, CONV_COLS), lambda i: (0, wb0 + i)),
                  pl.BlockSpec((1, CONV_COLS), lambda i: (0, wb0 + i))],
        out_specs=[pl.BlockSpec((s, CONV_COLS), lambda i: (0, i)),
                   pl.BlockSpec((CONV_K, CONV_COLS), lambda i: (0, i)),
                   pl.BlockSpec((1, CONV_COLS), lambda i: (0, i))],
        out_shape=[SDS((s, ncb * CONV_COLS), BF16), SDS((CONV_K, ncb * CONV_COLS), F32), SDS((1, ncb * CONV_COLS), F32)],
        compiler_params=_cparams(("parallel",)), name=name)(proj, dout, conv_w, conv_b)


def _dot(a, b, dims):
    return lax.dot_general(a.astype(BF16), b.astype(BF16), (dims, ((), ())), preferred_element_type=F32)


NN = ((1,), (0,))
NT = ((1,), (1,))
TN = ((0,), (0,))


def ssd_chunk(xs, zs, nw, dtc, dtw, alogs, dsks, bm, cm, prev):
    ln = bm.shape[0]
    gw = GROUP_HEADS * HEAD_DIM
    row = lax.broadcasted_iota(jnp.int32, (ln, ln), 0)
    col = lax.broadcasted_iota(jnp.int32, (ln, ln), 1)
    causal = row >= col
    tri = causal.astype(F32)
    tri_t = (row <= col).astype(F32)
    lane_head = lax.broadcasted_iota(jnp.int32, (1, gw), 1) // HEAD_DIM
    sub_head = lax.broadcasted_iota(jnp.int32, (gw, 1), 0) // HEAD_DIM
    on_lanes = [(lane_head == r).astype(F32) for r in range(GROUP_HEADS)]
    on_rows = [(sub_head == r).astype(F32) for r in range(GROUP_HEADS)]
    cb = _dot(cm, bm, NT)
    decays, dt_full, acs_full, end_full, dsk_full, end_rows = [], 0.0, 0.0, 0.0, 0.0, 0.0
    for r in range(GROUP_HEADS):
        a = -jnp.exp(alogs[r])
        da_c = dtc[r] * a
        da_w = dtw[r] * a
        acs_c = jnp.sum(tri * da_w, axis=1, keepdims=True)
        acs_w = jnp.sum(tri_t * da_c, axis=0, keepdims=True)
        alast = jnp.sum(da_w, axis=1, keepdims=True)
        decays.append(jnp.exp(jnp.where(causal, acs_c - acs_w, -jnp.inf)))
        dt_full = dt_full + dtc[r] * on_lanes[r]
        acs_full = acs_full + acs_c * on_lanes[r]
        end_full = end_full + alast * on_lanes[r]
        dsk_full = dsk_full + dsks[r] * on_lanes[r]
        end_rows = end_rows + alast * on_rows[r]
    xd = xs * dt_full
    y = xs * dsk_full + _dot(cm, prev, NT) * jnp.exp(acs_full)
    for r in range(GROUP_HEADS):
        y = y + _dot(cb * decays[r], xd * on_lanes[r], NN)
    new_prev = prev * jnp.exp(end_rows) + _dot(xd * jnp.exp(end_full - acs_full), bm, TN)
    yg = y * (zs * _sigmoid(zs))
    rstd = lax.rsqrt(jnp.mean(yg * yg, axis=-1, keepdims=True) + RMS_EPS)
    return yg * rstd * nw, new_prev


SSD_STEP_GROUPS = 1


def _ssd_specs(dm, cidx):
    u = SSD_STEP_GROUPS
    gw = GROUP_HEADS * HEAD_DIM
    nb0 = dm.DI // D_STATE
    assert nb0 % u == 0 and dm.G % u == 0
    par = pl.BlockSpec((u, 1, GROUP_HEADS), lambda g, c: (g, 0, 0))
    return dict(
        z=pl.BlockSpec((CHUNK, u * gw), lambda g, c: (cidx(c), g)),
        xs=pl.BlockSpec((CHUNK, u * gw), lambda g, c: (cidx(c), g)),
        b=pl.BlockSpec((CHUNK, u * D_STATE), lambda g, c: (cidx(c), nb0 // u + g)),
        c=pl.BlockSpec((CHUNK, u * D_STATE), lambda g, c: (cidx(c), (nb0 + dm.G) // u + g)),
        dtc=pl.BlockSpec((u, CHUNK, GROUP_HEADS), lambda g, c: (g, cidx(c), 0)),
        dtw=pl.BlockSpec((u, GROUP_HEADS, CHUNK), lambda g, c: (g, 0, cidx(c))),
        par=par,
        nw=pl.BlockSpec((1, u * gw), lambda g, c: (0, g)),
        st=pl.BlockSpec((u, None, gw, D_STATE), lambda g, c: (g, cidx(c), 0, 0)),
    )


def _ssd_load(k, z_ref, xs_ref, b_ref, c_ref, dtc_ref, dtw_ref, alog_ref, dsk_ref, nw_ref):
    gw = GROUP_HEADS * HEAD_DIM
    wide, narrow = slice(k * gw, (k + 1) * gw), slice(k * D_STATE, (k + 1) * D_STATE)
    dtc = tuple(dtc_ref[k, :, r:r + 1] for r in range(GROUP_HEADS))
    dtw = tuple(dtw_ref[k, r:r + 1, :] for r in range(GROUP_HEADS))
    alogs = tuple(alog_ref[k, :, r:r + 1] for r in range(GROUP_HEADS))
    dsks = tuple(dsk_ref[k, :, r:r + 1] for r in range(GROUP_HEADS))
    return xs_ref[:, wide], z_ref[:, wide], nw_ref[:, wide], dtc, dtw, alogs, dsks, b_ref[:, narrow], c_ref[:, narrow]


def ssd_fwd(proj, xbc, dtc, dtw, alog, dsk, nw, dm, name):
    nc = dm.S // CHUNK
    u = SSD_STEP_GROUPS
    gw = GROUP_HEADS * HEAD_DIM
    sp = _ssd_specs(dm, lambda c: c)

    def body(z_ref, xs_ref, b_ref, c_ref, dtc_ref, dtw_ref, alog_ref, dsk_ref, nw_ref, y_ref, st_ref, prev):
        @pl.when(pl.program_id(1) == 0)
        def _():
            prev[...] = jnp.zeros_like(prev)

        st_ref[...] = prev[...]
        for k in range(u):
            args = _ssd_load(k, z_ref, xs_ref, b_ref, c_ref, dtc_ref, dtw_ref, alog_ref, dsk_ref, nw_ref)
            out, new = ssd_chunk(*args, prev[k])
            y_ref[:, k * gw:(k + 1) * gw] = out.astype(y_ref.dtype)
            prev[k] = new

    return pl.pallas_call(
        body, grid=(dm.G // u, nc),
        in_specs=[sp["z"], sp["xs"], sp["b"], sp["c"], sp["dtc"], sp["dtw"], sp["par"], sp["par"], sp["nw"]],
        out_specs=[sp["xs"], sp["st"]],
        out_shape=[SDS((dm.S, dm.DI), BF16), SDS((dm.G, nc, gw, D_STATE), F32)],
        scratch_shapes=[pltpu.VMEM((u, gw, D_STATE), F32)],
        compiler_params=_cparams(("parallel", "arbitrary")), name=name)(proj, xbc, xbc, xbc, dtc, dtw, alog, dsk, nw)


def ssd_bwd(proj, xbc, dtc, dtw, alog, dsk, nw, states, dy, dm, name):
    nc = dm.S // CHUNK
    u = SSD_STEP_GROUPS
    sp = _ssd_specs(dm, lambda c: nc - 1 - c)
    gw = GROUP_HEADS * HEAD_DIM
    bc_spec = pl.BlockSpec((CHUNK, u * D_STATE), lambda g, c: (nc - 1 - c, g))

    def body(z_ref, xs_ref, b_ref, c_ref, dtc_ref, dtw_ref, alog_ref, dsk_ref, nw_ref, st_ref, dy_ref,
             dz_ref, dxs_ref, db_ref, dc_ref, ddtc_ref, ddtw_ref, dalog_ref, ddsk_ref, dnw_ref, dprev):
        first = pl.program_id(1) == 0

        @pl.when(first)
        def _():
            dprev[...] = jnp.zeros_like(dprev)

        param_grads = []
        for k in range(u):
            wide, narrow = slice(k * gw, (k + 1) * gw), slice(k * D_STATE, (k + 1) * D_STATE)
            args = _ssd_load(k, z_ref, xs_ref, b_ref, c_ref, dtc_ref, dtw_ref, alog_ref, dsk_ref, nw_ref)
            _, vjp = jax.vjp(ssd_chunk, *args, st_ref[k])
            gxs, gzs, gnw, gdtc, gdtw, galogs, gdsks, gb, gc, gprev = vjp((dy_ref[:, wide], dprev[k]))
            dxs_ref[:, wide] = gxs
            dz_ref[:, wide] = gzs.astype(dz_ref.dtype)
            db_ref[:, narrow] = gb
            dc_ref[:, narrow] = gc
            dprev[k] = gprev
            for r in range(GROUP_HEADS):
                ddtc_ref[k, :, r:r + 1] = gdtc[r]
                ddtw_ref[k, r:r + 1, :] = gdtw[r]
            param_grads.append((wide, gnw, galogs, gdsks))

        @pl.when(first)
        def _():
            for k, (wide, gnw, galogs, gdsks) in enumerate(param_grads):
                dnw_ref[:, wide] = gnw
                for r in range(GROUP_HEADS):
                    dalog_ref[k, :, r:r + 1] = galogs[r]
                    ddsk_ref[k, :, r:r + 1] = gdsks[r]

        @pl.when(jnp.logical_not(first))
        def _():
            for k, (wide, gnw, galogs, gdsks) in enumerate(param_grads):
                dnw_ref[:, wide] += gnw
                for r in range(GROUP_HEADS):
                    dalog_ref[k, :, r:r + 1] += galogs[r]
                    ddsk_ref[k, :, r:r + 1] += gdsks[r]

    xs_out = pl.BlockSpec((CHUNK, u * gw), lambda g, c: (nc - 1 - c, g))
    return pl.pallas_call(
        body, grid=(dm.G // u, nc),
        in_specs=[sp["z"], sp["xs"], sp["b"], sp["c"], sp["dtc"], sp["dtw"], sp["par"], sp["par"], sp["nw"],
                  sp["st"], xs_out],
        out_specs=[xs_out, xs_out, bc_spec, bc_spec, sp["dtc"], sp["dtw"], sp["par"], sp["par"], sp["nw"]],
        out_shape=[SDS((dm.S, dm.DI), BF16), SDS((dm.S, dm.DI), F32), SDS((dm.S, dm.G * D_STATE), F32),
                   SDS((dm.S, dm.G * D_STATE), F32), SDS((dm.G, dm.S, GROUP_HEADS), F32), SDS((dm.G, GROUP_HEADS, dm.S), F32),
                   SDS((dm.G, 1, GROUP_HEADS), F32), SDS((dm.G, 1, GROUP_HEADS), F32), SDS((1, dm.DI), F32)],
        scratch_shapes=[pltpu.VMEM((u, gw, D_STATE), F32)],
        compiler_params=_cparams(("parallel", "arbitrary")), name=name)(
            proj, xbc, xbc, xbc, dtc, dtw, alog, dsk, nw, states, dy)


def _split_bf16(v):
    hi = v.astype(BF16)
    return hi, (v - hi.astype(F32)).astype(BF16)


def _tri(v, mat):
    hi, lo = _split_bf16(v)
    return jnp.dot(hi, mat, preferred_element_type=F32) + jnp.dot(lo, mat, preferred_element_type=F32)


def _blocks(v):
    return [v[:, b * ATT_TILE:(b + 1) * ATT_TILE] for b in range(v.shape[1] // ATT_TILE)]


def _sb_group(z, mask, run, after_mat):
    sp = jnp.maximum(z, 0.0) + jnp.log2(1.0 + jnp.exp2(-jnp.abs(z)))
    lk = -sp if mask is None else jnp.where(mask, -sp, 0.0)
    cums = [_tri(v, after_mat) for v in _blocks(lk)]
    sums = [jnp.sum(v, axis=1, keepdims=True) for v in _blocks(lk)]
    later = [None] * len(cums)
    for b in reversed(range(len(cums))):
        later[b] = run + cums[b]
        run = run + sums[b]
    ls = z - sp
    w = jnp.exp2(ls + jnp.concatenate(later, axis=1))
    if mask is not None:
        w = jnp.where(mask, w, 0.0)
    return ls, w, run


def _alive(run_a, run_b):
    return (jnp.max(jnp.maximum(run_a, run_b)) > ATT_DEAD).astype(jnp.int32)


def _group_mask(i, g, t, gw):
    rows = i * t + lax.broadcasted_iota(jnp.int32, (t, gw), 0)
    cols = g * gw + lax.broadcasted_iota(jnp.int32, (t, gw), 1)
    return cols < rows


def _window(i, jj, t):
    gw = ATT_GROUP * t
    end = (i + 1 - ATT_GROUP * jj) * t
    r0 = pl.multiple_of(jnp.maximum(end - gw, 0), t)
    rows = i * t + lax.broadcasted_iota(jnp.int32, (t, gw), 0)
    cols = r0 + lax.broadcasted_iota(jnp.int32, (t, gw), 1)
    return r0, jnp.logical_and(cols < rows, cols < end)


def _att_specs(dm, s):
    t = ATT_TILE
    qb, kb, vb = dm.off["q"] // LANES, dm.off["k"] // LANES, dm.off["v"] // LANES
    return (pl.BlockSpec((t, LANES), lambda p, i: (i, qb + p)),
            pl.BlockSpec((s, LANES), lambda p, i: (0, kb + p)),
            pl.BlockSpec((s, LANES), lambda p, i: (0, vb + p)))


def attn_fwd(proj, dm, name, rider=None):
    s, t = dm.S, ATT_TILE
    scale = HEAD_DIM ** -0.5
    hsl = [slice(h * HEAD_DIM, (h + 1) * HEAD_DIM) for h in range(2)]

    gw = ATT_GROUP * t

    def body(q_ref, k_ref, v_ref, o_ref):
        i = pl.program_id(1)
        gd = i // ATT_GROUP
        r_io = lax.broadcasted_iota(jnp.int32, (t, t), 0)
        c_io = lax.broadcasted_iota(jnp.int32, (t, t), 1)
        after_mat = (r_io > c_io).astype(BF16)
        qs = [q_ref[:, sl].astype(BF16) for sl in hsl]

        def group(jj, carry):
            r0, mask = _window(i, jj, t)
            zs = [_dot(qs[h], k_ref[pl.ds(r0, gw), hsl[h]], NT) * (scale * LOG2E) for h in range(2)]
            res = [_sb_group(zs[h], mask, carry[h][0], after_mat) for h in range(2)]
            return tuple((res[h][2], carry[h][1] + _dot(res[h][1], v_ref[pl.ds(r0, gw), hsl[h]], NN)) for h in range(2))

        zero = (jnp.zeros((t, 1), F32), jnp.zeros((t, HEAD_DIM), F32))
        carry = group(0, (zero, zero))

        def step(st):
            jj, _, c = st
            c = group(jj, c)
            return jj + 1, _alive(c[0][0], c[1][0]), c

        _, _, carry = lax.while_loop(lambda st: jnp.logical_and(st[0] <= gd, st[1] > 0), step,
                                     (jnp.int32(1), _alive(carry[0][0], carry[1][0]), carry))
        for h in range(2):
            o_ref[:, hsl[h]] = carry[h][1]

    qs_, ks_, vs_ = _att_specs(dm, s)
    res = _grid_call(body, grid=(dm.SBW // LANES, s // t), in_specs=[qs_, ks_, vs_],
                     out_specs=[pl.BlockSpec((t, LANES), lambda p, i: (i, p))], out_shape=[SDS((s, dm.SBW), F32)],
                     scratch=[], operands=(proj, proj, proj), name=name, rider=rider)
    return res[0] if rider is None else res


def attn_bwd(proj, do, dm, name, rider=None):
    s, t = dm.S, ATT_TILE
    nq = s // t
    gw = ATT_GROUP * t
    scale = HEAD_DIM ** -0.5
    hsl = [slice(h * HEAD_DIM, (h + 1) * HEAD_DIM) for h in range(2)]

    def body(q_ref, k_ref, v_ref, do_ref, dq_ref, dk_ref, dv_ref, dk_acc, dv_acc, g_scr, s_scr):
        i = pl.program_id(1)

        @pl.when(i == 0)
        def _():
            dk_acc[...] = jnp.zeros_like(dk_acc)
            dv_acc[...] = jnp.zeros_like(dv_acc)

        gd = i // ATT_GROUP
        r_io = lax.broadcasted_iota(jnp.int32, (t, t), 0)
        c_io = lax.broadcasted_iota(jnp.int32, (t, t), 1)
        after_mat = (r_io > c_io).astype(BF16)
        before_mat = (r_io < c_io).astype(BF16)
        qs = [q_ref[:, sl].astype(BF16) for sl in hsl]
        dos = [do_ref[:, sl].astype(BF16) for sl in hsl]
        q_t = q_ref[...].T.astype(BF16)
        do_t = do_ref[...].T.astype(BF16)

        def pass1(jj, runs):
            r0, mask = _window(i, jj, t)
            zs = [_dot(qs[h], k_ref[pl.ds(r0, gw), hsl[h]], NT) * (scale * LOG2E) for h in range(2)]
            dws = [_dot(dos[h], v_ref[pl.ds(r0, gw), hsl[h]], NT) for h in range(2)]
            out = []
            for h in range(2):
                ls, w, run = _sb_group(zs[h], mask, runs[h], after_mat)
                g_scr[h, jj] = dws[h] * w
                s_scr[h, jj] = jnp.exp2(ls)
                dv_acc[hsl[h], pl.ds(r0, gw)] += _dot(do_t[hsl[h]], w, NN)
                out.append(run)
            return tuple(out)

        zero_col = jnp.zeros((t, 1), F32)
        runs = pass1(0, (zero_col, zero_col))

        def step1(st):
            jj, _, r = st
            r = pass1(jj, r)
            return jj + 1, _alive(r[0], r[1]), r

        walked, _, _ = lax.while_loop(lambda st: jnp.logical_and(st[0] <= gd, st[1] > 0), step1,
                                      (jnp.int32(1), _alive(runs[0], runs[1]), runs))

        def pass2(jj, carry):
            r0, mask = _window(i, jj, t)
            out = []
            for h in range(2):
                pre, dq = carry[h]
                gg = g_scr[h, jj]
                sig = s_scr[h, jj]
                before = []
                for v in _blocks(gg):
                    before.append(pre + _tri(v, before_mat))
                    pre = pre + jnp.sum(v, axis=1, keepdims=True)
                dz = jnp.where(mask, gg * (1.0 - sig) - jnp.concatenate(before, axis=1) * sig, 0.0)
                dz = (dz * scale).astype(BF16)
                dk_acc[hsl[h], pl.ds(r0, gw)] += _dot(q_t[hsl[h]], dz, NN)
                out.append((pre, dq + _dot(dz, k_ref[pl.ds(r0, gw), hsl[h]], NN)))
            return tuple(out)

        zero = (zero_col, jnp.zeros((t, HEAD_DIM), F32))
        carry = lax.fori_loop(1, walked, lambda n, c: pass2(walked - n, c), (zero, zero))
        carry = pass2(0, carry)
        for h in range(2):
            dq_ref[:, hsl[h]] = carry[h][1].astype(dq_ref.dtype)

        @pl.when(i == nq - 1)
        def _():
            for g in range(s // gw):
                dk_ref[g * gw:(g + 1) * gw, :] = dk_acc[:, g * gw:(g + 1) * gw].T.astype(dk_ref.dtype)
                dv_ref[g * gw:(g + 1) * gw, :] = dv_acc[:, g * gw:(g + 1) * gw].T.astype(dv_ref.dtype)

    qs_, ks_, vs_ = _att_specs(dm, s)
    tile_spec = pl.BlockSpec((t, LANES), lambda p, i: (i, p))
    full_spec = pl.BlockSpec((s, LANES), lambda p, i: (0, p))
    return _grid_call(
        body, grid=(dm.SBW // LANES, nq), in_specs=[qs_, ks_, vs_, tile_spec],
        out_specs=[tile_spec, full_spec, full_spec], out_shape=[SDS((s, dm.SBW), BF16)] * 3,
        scratch=[pltpu.VMEM((LANES, s), F32), pltpu.VMEM((LANES, s), F32),
                 pltpu.VMEM((2, s // gw, t, gw), F32), pltpu.VMEM((2, s // gw, t, gw), F32)],
        operands=(proj, proj, proj, do), name=name, rider=rider)


def adamw(w, g, m, v, name):
    rows, width = w.shape
    tile = _pick(rows, (256, 64, 16, 8, 4))
    c1 = 1.0 / (1.0 - ADAM_B1 ** ADAM_STEP)
    c2 = 1.0 / (1.0 - ADAM_B2 ** ADAM_STEP)

    def body(w_ref, g_ref, m_ref, v_ref, d_ref, nm_ref, nv_ref):
        gg = g_ref[...]
        nm = ADAM_B1 * m_ref[...] + (1.0 - ADAM_B1) * gg
        nv = ADAM_B2 * v_ref[...] + (1.0 - ADAM_B2) * (gg * gg)
        d_ref[...] = -ADAM_LR * ((nm * c1) / (jnp.sqrt(nv * c2) + ADAM_EPS) + ADAM_WD * w_ref[...])
        nm_ref[...] = nm
        nv_ref[...] = nv

    spec = pl.BlockSpec((tile, width), lambda i: (i, 0))
    return pl.pallas_call(
        body, grid=(rows // tile,), in_specs=[spec] * 4, out_specs=[spec] * 3,
        out_shape=[SDS((rows, width), F32)] * 3, compiler_params=_cparams(("parallel",)), name=name)(w, g, m, v)


def _me():
    return lax.axis_index("x"), lax.axis_index("y"), lax.axis_index("c")


def _other_chips(x, y):
    return [(1 - x, y), (x, 1 - y), (1 - x, 1 - y)]


def gather_weights(wp):
    rows, width = wp.shape
    half = rows // 2

    def body(w_ref, out_ref, send_sems, recv_sems):
        x, y, c = _me()
        sibling = (x, y, 1 - c)
        chips = _other_chips(x, y)

        def part(cx, cy, hf):
            return out_ref.at[2 * cx + cy, hf]

        def copy(k, src, dst, to):
            return pltpu.make_async_remote_copy(src_ref=src, dst_ref=dst, send_sem=send_sems.at[k], recv_sem=recv_sems.at[k],
                                                device_id=to, device_id_type=MESH_ID)

        first = [copy(j, w_ref.at[c], part(x, y, c), (cx, cy, c)) for j, (cx, cy) in enumerate(chips)]
        for cp in first:
            cp.start()
        passed = [copy(3 + j, part(cx, cy, c), part(cx, cy, c), sibling) for j, (cx, cy) in enumerate(chips)]
        for j, (cx, cy) in enumerate(chips):
            copy(j, part(cx, cy, c), part(cx, cy, c), (x, y, c)).wait_recv()
            passed[j].start()
        for j, (cx, cy) in enumerate(chips):
            copy(3 + j, part(cx, cy, 1 - c), part(cx, cy, 1 - c), (x, y, c)).wait_recv()
        for cp in first + passed:
            cp.wait_send()

    return pl.pallas_call(
        body, out_shape=SDS((N_CHIPS, 2, half, width), wp.dtype), in_specs=[HBM_SPEC], out_specs=HBM_SPEC,
        scratch_shapes=[pltpu.SemaphoreType.DMA((6,)), pltpu.SemaphoreType.DMA((6,))],
        name="gather_weights")(wp.reshape(2, half, width)).reshape(N_CHIPS, rows, width)


class Rider(NamedTuple):
    operands: tuple
    out_shape: tuple
    scratch: tuple
    start: object
    wait: object


def _with_rider(body, rider, grid, n_in, n_out):
    if rider is None:
        return body
    r_in, r_out = len(rider.operands), len(rider.out_shape)

    def full(*refs):
        ins, refs = refs[:n_in], refs[n_in:]
        rins, refs = refs[:r_in], refs[r_in:]
        outs, refs = refs[:n_out], refs[n_out:]
        routs, refs = refs[:r_out], refs[r_out:]
        scr, rscr = refs[:len(refs) - len(rider.scratch)], refs[len(refs) - len(rider.scratch):]
        ids = [pl.program_id(k) for k in range(len(grid))]
        first = functools.reduce(jnp.logical_and, [i == 0 for i in ids])
        last = functools.reduce(jnp.logical_and, [i == g - 1 for i, g in zip(ids, grid)])

        @pl.when(first)
        def _():
            rider.start(rins, routs, rscr)

        body(*ins, *outs, *scr)

        @pl.when(last)
        def _():
            rider.wait(rins, routs, rscr)

    return full


def gather_rider(wp):
    def copies(ins, outs, scr, sending):
        (w_ref,), (o_ref,), (send_sems, recv_sems) = ins, outs, scr
        x, y, c = _me()
        return [pltpu.make_async_remote_copy(src_ref=w_ref, dst_ref=o_ref.at[2 * x + y if sending else 2 * cx + cy],
                                             send_sem=send_sems.at[j], recv_sem=recv_sems.at[j], device_id=(cx, cy, c),
                                             device_id_type=MESH_ID)
                for j, (cx, cy) in enumerate(_other_chips(x, y))]

    def start(ins, outs, scr):
        for cp in copies(ins, outs, scr, True):
            cp.start()

    def wait(ins, outs, scr):
        for cp in copies(ins, outs, scr, False):
            cp.wait()

    return Rider((wp,), (SDS((N_CHIPS,) + wp.shape, wp.dtype),),
                 (pltpu.SemaphoreType.DMA((3,)), pltpu.SemaphoreType.DMA((3,))), start, wait)


def own_slab(gathered, wp, chip_idx):
    return lax.dynamic_update_slice(gathered, wp[None], (chip_idx[0], 0, 0))


def exchange_rider(sh):
    def copies(ins, outs, scr):
        (s_ref,), (b_ref,), (send_sems, recv_sems) = ins, outs, scr
        x, y, c = _me()
        return [pltpu.make_async_remote_copy(src_ref=s_ref.at[2 * cx + cy], dst_ref=b_ref.at[j], send_sem=send_sems.at[j],
                                             recv_sem=recv_sems.at[j], device_id=(cx, cy, c), device_id_type=MESH_ID)
                for j, (cx, cy) in enumerate(_other_chips(x, y))]

    def start(ins, outs, scr):
        for cp in copies(ins, outs, scr):
            cp.start()

    def wait(ins, outs, scr):
        for cp in copies(ins, outs, scr):
            cp.wait()

    return Rider((sh,), (SDS((3,) + sh.shape[1:], sh.dtype),),
                 (pltpu.SemaphoreType.DMA((3,)), pltpu.SemaphoreType.DMA((3,))), start, wait)


def swap_halves(g, name):
    n, rows, width = g.shape
    half = rows // 2

    def body(g_ref, a_ref, send_sem, recv_sem):
        x, y, c = _me()
        cp = pltpu.make_async_remote_copy(src_ref=g_ref.at[:, 1 - c], dst_ref=a_ref,
                                          send_sem=send_sem, recv_sem=recv_sem, device_id=(x, y, 1 - c), device_id_type=MESH_ID)
        cp.start()
        cp.wait()

    return pl.pallas_call(
        body, out_shape=SDS((n, half, width), g.dtype), in_specs=[HBM_SPEC], out_specs=HBM_SPEC,
        scratch_shapes=[pltpu.SemaphoreType.DMA, pltpu.SemaphoreType.DMA], name=name)(g.reshape(n, 2, half, width))


def add_half(g, a, c_idx, name):
    n, rows, width = g.shape
    half = rows // 2
    tile = half // 2
    nt = half // tile

    def body(c_ref, g_ref, a_ref, o_ref, ob_ref):
        v = g_ref[...] + a_ref[...]
        o_ref[...] = v
        ob_ref[...] = v.astype(ob_ref.dtype)

    out_spec = pl.BlockSpec((None, tile, width), lambda s, i, c_ref: (s, i, 0))
    gs = pltpu.PrefetchScalarGridSpec(
        num_scalar_prefetch=1, grid=(n, nt),
        in_specs=[pl.BlockSpec((None, tile, width), lambda s, i, c_ref: (s, c_ref[0] * nt + i, 0)), out_spec],
        out_specs=[out_spec, out_spec])
    return pl.pallas_call(body, grid_spec=gs, out_shape=[SDS((n, half, width), F32), SDS((n, half, width), BF16)],
                          compiler_params=_cparams(("parallel", "parallel")), name=name)(c_idx, g, a)


def exchange_small(small):
    def body(sm_ref, all_ref, send_sems, recv_sems, local_sem):
        x, y, c = _me()
        mine = pltpu.make_async_copy(sm_ref, all_ref.at[0], local_sem)
        mine.start()
        copies = []
        for m in range(1, N_DEV):
            peer = (x ^ ((m >> 2) & 1), y ^ ((m >> 1) & 1), c ^ (m & 1))
            copies.append(pltpu.make_async_remote_copy(
                src_ref=sm_ref, dst_ref=all_ref.at[m], send_sem=send_sems.at[m - 1], recv_sem=recv_sems.at[m - 1],
                device_id=peer, device_id_type=MESH_ID))
        for cp in copies:
            cp.start()
        for cp in copies:
            cp.wait()
        mine.wait()

    return pl.pallas_call(
        body, out_shape=SDS((N_DEV,) + small.shape, small.dtype), in_specs=[HBM_SPEC], out_specs=HBM_SPEC,
        scratch_shapes=[pltpu.SemaphoreType.DMA((N_DEV - 1,)), pltpu.SemaphoreType.DMA((N_DEV - 1,)), pltpu.SemaphoreType.DMA],
        name="exchange_small")(small)


def add_chips(sh, b, k_idx, name):
    n, hf, width = sh.shape
    tile = hf // 2

    def body(k_ref, s_ref, b0, b1, b2, o_ref):
        o_ref[...] = ((s_ref[...] + b0[...].astype(F32)) + b1[...].astype(F32)) + b2[...].astype(F32)

    def bspec(j):
        return pl.BlockSpec((None, tile, width), lambda i, k_ref, j=j: (j, i, 0))

    gs = pltpu.PrefetchScalarGridSpec(
        num_scalar_prefetch=1, grid=(hf // tile,),
        in_specs=[pl.BlockSpec((None, tile, width), lambda i, k_ref: (k_ref[0], i, 0)), bspec(0), bspec(1), bspec(2)],
        out_specs=pl.BlockSpec((tile, width), lambda i, k_ref: (i, 0)))
    return pl.pallas_call(body, grid_spec=gs, out_shape=SDS((hf, width), sh.dtype),
                          compiler_params=_cparams(("parallel",)), name=name)(k_idx, sh, b, b, b)


def sum_small(allsm, me_idx, name):
    _, rows, width = allsm.shape

    def body(me_ref, a_ref, o_ref):
        me = me_ref[0]
        acc = a_ref[me]
        for dev in range(1, N_DEV):
            acc = acc + a_ref[jnp.bitwise_xor(me, dev)]
        o_ref[...] = acc

    gs = pltpu.PrefetchScalarGridSpec(
        num_scalar_prefetch=1, grid=(1,),
        in_specs=[pl.BlockSpec((N_DEV, rows, width), lambda i, me_ref: (0, 0, 0))],
        out_specs=pl.BlockSpec((rows, width), lambda i, me_ref: (0, 0)))
    return pl.pallas_call(body, grid_spec=gs, out_shape=SDS((rows, width), allsm.dtype),
                          compiler_params=_cparams(("arbitrary",)), name=name)(me_idx, allsm)


def join_halves(t, core_idx, name):
    hf, width = t.shape

    def body(t_ref, o_ref, send_sem, recv_sem):
        x, y, c = _me()
        cp = pltpu.make_async_remote_copy(src_ref=t_ref, dst_ref=o_ref, send_sem=send_sem, recv_sem=recv_sem,
                                          device_id=(x, y, 1 - c), device_id_type=MESH_ID)
        cp.start()
        cp.wait()

    theirs = pl.pallas_call(
        body, out_shape=SDS((hf, width), t.dtype), in_specs=[HBM_SPEC], out_specs=HBM_SPEC,
        scratch_shapes=[pltpu.SemaphoreType.DMA, pltpu.SemaphoreType.DMA], name=name)(t)
    return jnp.where(core_idx[0] == 0, jnp.concatenate([t, theirs], axis=0), jnp.concatenate([theirs, t], axis=0))


ROW_PAD = 64


def _pad_rows(rows):
    return -(-rows // ROW_PAD) * ROW_PAD


def _pack_rows(names, shard_shapes, width):
    return _pad_rows(sum((shard_shapes[n][0] * shard_shapes[n][1]) // width for n in names))


def unpack_local(packed, names, shard_shapes, width):
    out, r0 = {}, 0
    for n in names:
        a, b = shard_shapes[n]
        nr = (a * b) // width
        out[n] = packed[r0:r0 + nr].reshape(a, b)
        r0 += nr
    return out


EXACT_IN_GATHER = ("conv_w",)
EXACT_TERMS = 3


def pack_gather(shards, names, width):
    parts = []
    for n in names:
        if n in EXACT_IN_GATHER:
            rest = shards[n].astype(F32)
            for _ in range(EXACT_TERMS):
                term = rest.astype(BF16)
                parts.append(term.reshape(-1, width))
                rest = rest - term.astype(F32)
        else:
            parts.append(shards[n].reshape(-1, width).astype(BF16))
    used = sum(p.shape[0] for p in parts)
    parts.append(jnp.zeros((_pad_rows(used) - used, width), BF16))
    return jnp.concatenate(parts, axis=0)


def unpack_full(gathered, names, shard_shapes, width):
    out, r0 = {}, 0
    for n in names:
        a, b = shard_shapes[n]
        terms = EXACT_TERMS if n in EXACT_IN_GATHER else 1
        nr = (a * b) // width
        pieces = []
        for j in range(N_CHIPS):
            blk = gathered[j, r0:r0 + nr].reshape(a, b)
            for t in range(1, terms):
                blk = blk.astype(F32) + gathered[j, r0 + t * nr:r0 + (t + 1) * nr].reshape(a, b).astype(F32)
            pieces.append(blk)
        out[n] = jnp.concatenate(pieces, axis=SHARD_AXIS[n])
        r0 += terms * nr
    return out


def pack_full(grads, names, shard_shapes, width):
    total_rows = _pack_rows(names, shard_shapes, width)
    slabs = []
    for j in range(N_CHIPS):
        parts = []
        for n in names:
            a, b = shard_shapes[n]
            ax = SHARD_AXIS[n]
            sz = (a, b)[ax]
            piece = lax.slice_in_dim(grads[n], j * sz, (j + 1) * sz, axis=ax)
            parts.append(piece.reshape(-1, width))
        used = sum(p.shape[0] for p in parts)
        parts.append(jnp.zeros((total_rows - used, width), F32))
        slabs.append(jnp.concatenate(parts, axis=0))
    return jnp.stack(slabs, axis=0)


def _small_layout(sizes, width):
    lay, r = {}, 0
    for n in SMALL_WEIGHTS:
        nr = -(-sizes[n] // width)
        lay[n] = (r, nr, sizes[n])
        r += nr
    assert r <= SMALL_ROWS
    return lay


def pack_small(vals, lay, width):
    rows = []
    for n in SMALL_WEIGHTS:
        r, nr, sz = lay[n]
        v = vals[n].reshape(-1).astype(F32)
        rows.append(jnp.pad(v, (0, nr * width - sz)).reshape(nr, width))
    used = sum(r.shape[0] for r in rows)
    rows.append(jnp.zeros((SMALL_ROWS - used, width), F32))
    return jnp.concatenate(rows, axis=0)


def unpack_small(packed, lay):
    return {n: packed[r:r + nr].reshape(-1)[:sz].reshape(1, sz) for n, (r, nr, sz) in lay.items()}


PART_B = ("w_in", "w_gate")
PART_A = tuple(n for n in BIG_WEIGHTS if n not in PART_B)


class Dist(NamedTuple):
    packed_a: object
    shard_shapes: dict
    core_idx: object
    chip_idx: object


def build_w_all(w_in, w_gate, dm):
    c0 = dm.DI + dm.CD
    return jnp.concatenate(
        [w_in[:, :c0], w_in[:, c0 + dm.H:], w_gate, w_in[:, c0:c0 + dm.H],
         jnp.zeros((dm.D, DT_PAD - dm.H), w_in.dtype)], axis=1).astype(BF16)


def pack_b(w_in_shard, w_gate_shard):
    return jnp.concatenate([w_in_shard, w_gate_shard], axis=1)


def unpack_b(slabs, dm):
    n_in = (dm.NA - DT_PAD - 2 * dm.D + dm.H) // N_CHIPS
    return (jnp.concatenate([slabs[j, :, :n_in] for j in range(N_CHIPS)], axis=1),
            jnp.concatenate([slabs[j, :, n_in:] for j in range(N_CHIPS)], axis=1))


def slabs_b(dw_all, dm):
    c0 = dm.DI + dm.CD
    off = dm.off
    dw_in = jnp.concatenate([dw_all[:, :c0], dw_all[:, off["dt"]:off["dt"] + dm.H], dw_all[:, c0:c0 + 3 * dm.SBW]], axis=1)
    dw_gate = dw_all[:, off["gate"]:off["gate"] + 2 * dm.D]
    n_in, n_gate = dw_in.shape[1] // N_CHIPS, dw_gate.shape[1] // N_CHIPS
    return jnp.stack([pack_b(dw_in[:, j * n_in:(j + 1) * n_in], dw_gate[:, j * n_gate:(j + 1) * n_gate])
                      for j in range(N_CHIPS)], axis=0)


def local_step(x, p, tgt, wf, sm, dm, dist=None):
    s, d = dm.S, dm.D
    off = dm.off
    c0 = dm.DI + dm.CD
    w_all = wf["w_all"] if "w_all" in wf else build_w_all(wf["w_in"], wf["w_gate"], dm)
    g = dm.G
    per_group = lambda v: v.reshape(g, 1, GROUP_HEADS)
    alog, dsk = per_group(sm["a_log"]), per_group(sm["d_skip"])
    b_gate = sm["b_gate"]
    b_ssd, b_sb = b_gate[:, :d], b_gate[:, d:]
    gcol = off["gate"] // d

    (n1,) = row_fwd("norm1", f_norm1, [x], [sm["norm_mix_pre"]], [(d, BF16)])
    proj = matmul(n1, w_all, name="in_proj")
    if dist is None:
        y_sb = attn_fwd(proj, dm, "attn_fwd")
    else:
        y_sb, gathered_a = attn_fwd(proj, dm, "attn_fwd", rider=gather_rider(dist.packed_a))
        gathered_a = own_slab(gathered_a, dist.packed_a, dist.chip_idx)
        wf = {**wf, **unpack_full(gathered_a, PART_A, dist.shard_shapes, d)}
    xbc = conv_fwd(proj, off["xbc"], dm.CD, wf["conv_w"].astype(F32), sm["conv_b"], "conv_fwd")
    dt_raw = proj[:, off["dt"]:off["dt"] + dm.H]
    (dt,) = row_fwd("dt", f_dt, [dt_raw], [sm["dt_bias"]], [(dm.H, F32)])
    dtc = dt.reshape(s, g, GROUP_HEADS).transpose(1, 0, 2)
    dtw = dt.reshape(s, g, GROUP_HEADS).transpose(1, 2, 0)
    y_ssd, states = ssd_fwd(proj, xbc, dtc, dtw, alog, dsk, sm["ssd_norm"], dm, "ssd_fwd")
    yb_ssd = matmul(y_ssd, wf["w_ssd_branch"], name="ssd_branch")
    yb_sb = matmul(y_sb, wf["w_sb_branch"], name="sb_branch")
    merge_rows = [(proj, d, gcol), (proj, d, gcol + 1), yb_ssd, yb_sb]
    (merged,) = row_fwd("merge", f_merge, merge_rows, [b_ssd, b_sb], [(d, BF16)])
    mo = matmul(merged, wf["w_out"], name="w_out")
    h1, n2 = row_fwd("mix_out", f_mix_out, [x, mo], [sm["norm_mix_post"], sm["norm_ffn_pre"]], [(d, F32), (d, BF16)])
    a1 = matmul(n2, wf["w_ff1"], name="ff1")
    (act,) = row_fwd("relu2", f_relu2, [a1], [], [(dm.DFF, BF16)])
    ff = matmul(act, wf["w_ff2"], name="ff2")
    (h2,) = row_fwd("ffn_out", f_ffn_out, [h1, ff], [sm["norm_ffn_post"]], [(d, F32)])
    pg = matmul(h2, wf["w_ple_gate"], name="ple_gate")
    pe = matmul(p, wf["w_ple"], name="ple_emb")

    def reduce_start(gbig, tag):
        from_sibling = swap_halves(gbig, "swap_halves_" + tag)
        pair_sum, pair_sum_bf16 = add_half(gbig, from_sibling, dist.core_idx, "add_half_" + tag)
        return pair_sum, exchange_rider(pair_sum_bf16)

    def reduce_finish(pair_sum, from_chips, tag):
        my_half = add_chips(pair_sum, from_chips, dist.chip_idx, "add_chips_" + tag)
        return join_halves(my_half, dist.core_idx, "join_halves_" + tag)

    gr, reduced = {}, None
    (dh2_a, dpg, dpe), (gr["norm_ple_post"], loss_cols) = row_bwd(
        "ple_loss", f_ple_loss, [h2, pg, pe, tgt], [sm["norm_ple_post"]], [None], [F32, BF16, BF16, None], primal_sum=True)
    loss = jnp.sum(loss_cols)
    gr["w_ple"] = matmul(p, dpe, ta=True, name="d_w_ple")
    gr["w_ple_gate"] = matmul(h2, dpg, ta=True, name="d_w_ple_gate")
    dh2_b = matmul(dpg, wf["w_ple_gate"], tb=True, name="d_h2")
    (dh1_a, dff), (gr["norm_ffn_post"],) = row_bwd(
        "ffn_out_bwd", f_ffn_out, [h1, ff], [sm["norm_ffn_post"]], [[dh2_a, dh2_b]], [F32, BF16])
    gr["w_ff2"] = matmul(act, dff, ta=True, name="d_w_ff2")
    dact = matmul(dff, wf["w_ff2"], tb=True, name="d_act")
    (da1,), _ = row_bwd("relu2_bwd", f_relu2, [a1], [], [[dact]], [BF16])
    gr["w_ff1"] = matmul(n2, da1, ta=True, name="d_w_ff1")
    dn2 = matmul(da1, wf["w_ff1"], tb=True, name="d_n2")
    (dx_a, dmo), (gr["norm_mix_post"], gr["norm_ffn_pre"]) = row_bwd(
        "mix_out_bwd", f_mix_out, [x, mo], [sm["norm_mix_post"], sm["norm_ffn_pre"]], [[dh1_a], [dn2]], [F32, BF16])
    gr["w_out"] = matmul(merged, dmo, ta=True, name="d_w_out")
    dmerged = matmul(dmo, wf["w_out"], tb=True, name="d_merged")
    (dgp_ssd, dgp_sb, dyb_ssd, dyb_sb), (db_ssd, db_sb) = row_bwd(
        "merge_bwd", f_merge, merge_rows, [b_ssd, b_sb], [[dmerged]], [BF16, BF16, BF16, BF16])
    gr["b_gate"] = jnp.concatenate([db_ssd, db_sb], axis=1)
    gr["w_ssd_branch"] = matmul(y_ssd, dyb_ssd, ta=True, name="d_w_ssd_branch")
    gr["w_sb_branch"] = matmul(y_sb, dyb_sb, ta=True, name="d_w_sb_branch")
    dy_ssd = matmul(dyb_ssd, wf["w_ssd_branch"], tb=True, name="d_y_ssd")
    dy_sb = matmul(dyb_sb, wf["w_sb_branch"], tb=True, name="d_y_sb")
    dz, dxs, dbm, dcm, ddtc, ddtw, dalog, ddsk, gr["ssd_norm"] = ssd_bwd(
        proj, xbc, dtc, dtw, alog, dsk, sm["ssd_norm"], states, dy_ssd, dm, "ssd_bwd")
    gr["a_log"], gr["d_skip"] = (v.reshape(1, dm.H) for v in (dalog, ddsk))
    ddt_post = (ddtc.transpose(1, 0, 2) + ddtw.transpose(2, 0, 1)).reshape(s, dm.H)
    (ddt,), (gr["dt_bias"],) = row_bwd("dt_bwd", f_dt, [dt_raw], [sm["dt_bias"]], [[ddt_post]], [BF16])
    conv_w32 = wf["conv_w"].astype(F32)
    du_x, dw_x, dcb_x = conv_bwd(proj, off["xbc"], dxs, 0, conv_w32, sm["conv_b"], "conv_bwd_x")
    du_b, dw_b, dcb_b = conv_bwd(proj, off["xbc"], dbm, dm.DI, conv_w32, sm["conv_b"], "conv_bwd_b")
    du_c, dw_c, dcb_c = conv_bwd(proj, off["xbc"], dcm, dm.DI + g * D_STATE, conv_w32, sm["conv_b"], "conv_bwd_c")
    gr["conv_w"] = jnp.concatenate([dw_x, dw_b, dw_c], axis=1)
    gr["conv_b"] = jnp.concatenate([dcb_x, dcb_b, dcb_c], axis=1)
    if dist is None:
        dq, dk, dv = attn_bwd(proj, dy_sb, dm, "attn_bwd")
    else:
        pair_sum_a, rider_a = reduce_start(pack_full(gr, PART_A, dist.shard_shapes, d), "a")
        dq, dk, dv, from_chips_a = attn_bwd(proj, dy_sb, dm, "attn_bwd", rider=rider_a)
        reduced = unpack_local(reduce_finish(pair_sum_a, from_chips_a, "a"), PART_A, dist.shard_shapes, d)
    dproj = jnp.concatenate(
        [dz, du_x, du_b, du_c, dq, dk, dv, dgp_ssd, dgp_sb, ddt, jnp.zeros((s, DT_PAD - dm.H), BF16)], axis=1)
    dw_all = matmul(n1, dproj, ta=True, name="d_w_all")
    if dist is None:
        gr["w_in"] = jnp.concatenate(
            [dw_all[:, :c0], dw_all[:, off["dt"]:off["dt"] + dm.H], dw_all[:, c0:c0 + 3 * dm.SBW]], axis=1)
        gr["w_gate"] = dw_all[:, off["gate"]:off["gate"] + 2 * d]
        dn1 = matmul(dproj, w_all, tb=True, name="d_n1")
    else:
        pair_sum_b, rider_b = reduce_start(slabs_b(dw_all, dm), "b")
        dn1, from_chips_b = matmul(dproj, w_all, tb=True, name="d_n1", rider=rider_b)
        reduced_b = reduce_finish(pair_sum_b, from_chips_b, "b")
        n_in = dist.shard_shapes["w_in"][1]
        reduced.update(w_in=reduced_b[:, :n_in], w_gate=reduced_b[:, n_in:])
    (dx_b,), (gr["norm_mix_pre"],) = row_bwd("norm1_bwd", f_norm1, [x], [sm["norm_mix_pre"]], [[dn1]], [F32])
    (grad_x,) = row_fwd("grad_x", lambda u, v: (u + v,), [dx_a, dx_b], [], [(d, F32)])
    return loss, grad_x, gr, reduced


def kernel(x, p, norm_mix_pre, w_in, conv_w, conv_b, dt_bias, a_log, d_skip, ssd_norm, w_ssd_branch, w_sb_branch, w_gate, b_gate, w_out, norm_mix_post, norm_ffn_pre, w_ff1, w_ff2, norm_ffn_post, w_ple, w_ple_gate, norm_ple_post, loss_target, m_norm_mix_pre, m_w_in, m_conv_w, m_conv_b, m_dt_bias, m_a_log, m_d_skip, m_ssd_norm, m_w_ssd_branch, m_w_sb_branch, m_w_gate, m_b_gate, m_w_out, m_norm_mix_post, m_norm_ffn_pre, m_w_ff1, m_w_ff2, m_norm_ffn_post, m_w_ple, m_w_ple_gate, m_norm_ple_post, v_norm_mix_pre, v_w_in, v_conv_w, v_conv_b, v_dt_bias, v_a_log, v_d_skip, v_ssd_norm, v_w_ssd_branch, v_w_sb_branch, v_w_gate, v_b_gate, v_w_out, v_norm_mix_post, v_norm_ffn_pre, v_w_ff1, v_w_ff2, v_norm_ffn_post, v_w_ple, v_w_ple_gate, v_norm_ple_post):
    loc = dict(locals())
    unbatch = lambda a: a[0] if a.ndim == 3 else a
    w = {n: unbatch(loc[n]) for n in ALL_WEIGHTS}
    m = {n: unbatch(loc["m_" + n]) for n in ALL_WEIGHTS}
    v = {n: unbatch(loc["v_" + n]) for n in ALL_WEIGHTS}
    xs, ps, tgt = x[0], p[0, 0], loss_target[0]
    s, d = xs.shape
    di = w["w_ssd_branch"].shape[0] * N_CHIPS
    cd = w["conv_b"].shape[1]
    dm = Dims(S=s, D=d, DI=di, H=w["dt_bias"].shape[1], G=(cd - di) // (2 * D_STATE), CD=cd,
              SBW=w["w_sb_branch"].shape[0] * N_CHIPS, DFF=w["w_ff2"].shape[0] * N_CHIPS, PLE=ps.shape[1])
    ix, iy, ic = lax.axis_index("x"), lax.axis_index("y"), lax.axis_index("c")
    chip_idx = jnp.reshape(2 * ix + iy, (1,)).astype(jnp.int32)
    core_idx = jnp.reshape(ic, (1,)).astype(jnp.int32)
    dev_idx = jnp.reshape(4 * ix + 2 * iy + ic, (1,)).astype(jnp.int32)

    shard_shapes = {n: w[n].shape for n in BIG_WEIGHTS}
    packed_b = pack_b(w["w_in"], w["w_gate"]).astype(BF16)
    wf = {"w_all": build_w_all(*unpack_b(own_slab(gather_weights(packed_b), packed_b, chip_idx), dm), dm)}
    sm = {n: w[n] for n in SMALL_WEIGHTS}
    dist = Dist(packed_a=pack_gather(w, PART_A, d), shard_shapes=shard_shapes, core_idx=core_idx, chip_idx=chip_idx)

    loss_part, grad_x, gr, grads = local_step(xs, ps, tgt, wf, sm, dm, dist)
    loss = lax.psum(loss_part, ("x", "y", "c"))

    lay = _small_layout({n: w[n].shape[1] for n in SMALL_WEIGHTS}, d)
    gs_red = sum_small(exchange_small(pack_small(gr, lay, d)), dev_idx, "sum_small")
    grads.update(unpack_small(gs_red, lay))
    delta, new_m, new_v = {}, {}, {}
    for n in BIG_WEIGHTS:
        delta[n], new_m[n], new_v[n] = adamw(w[n], grads[n], m[n], v[n], "adamw_" + n)
    d_sm, nm_sm, nv_sm = adamw(pack_small(w, lay, d), gs_red, pack_small(m, lay, d), pack_small(v, lay, d), "adamw_small")
    for out, packed in ((delta, d_sm), (new_m, nm_sm), (new_v, nv_sm)):
        out.update(unpack_small(packed, lay))

    def leaves(vals):
        return [vals[n][None] if n in BIG_WEIGHTS else vals[n] for n in ALL_WEIGHTS]

    return (loss, grad_x[None], *leaves(grads), *leaves(delta), *leaves(new_m), *leaves(new_v))
```

```python
import functools
import math
from typing import NamedTuple

import jax
import jax.numpy as jnp
from jax import lax
from jax.experimental import pallas as pl
from jax.experimental.pallas import tpu as pltpu

F32 = jnp.float32
BF16 = jnp.bfloat16
SDS = jax.ShapeDtypeStruct

HEAD_DIM = 64
GROUP_HEADS = 4
D_STATE = 128
CHUNK = 128
ATT_TILE = 128
ATT_GROUP = 4
ATT_DEAD = -160.0
LOG2E = 1.4426950408889634
CONV_K = 4
CONV_COLS = 128
RMS_EPS = 1e-6
LANES = 128
DT_PAD = 512
N_CHIPS = 4
N_DEV = 8
SMALL_ROWS = 16
VMEM_LIMIT = 48 * 1024 * 1024
MAX_TK = 3072

ADAM_LR = 0.001
ADAM_B1 = 0.9
ADAM_B2 = 0.999
ADAM_EPS = 1e-08
ADAM_WD = 0.01
ADAM_STEP = 10

MESH_ID = pl.DeviceIdType.MESH
HBM_SPEC = pl.BlockSpec(memory_space=pltpu.HBM)

BIG_WEIGHTS = ("w_in", "conv_w", "w_ssd_branch", "w_sb_branch", "w_gate", "w_out", "w_ff1", "w_ff2", "w_ple", "w_ple_gate")
SHARD_AXIS = {"w_in": 1, "conv_w": 1, "w_ssd_branch": 0, "w_sb_branch": 0, "w_gate": 1, "w_out": 0, "w_ff1": 1,
              "w_ff2": 0, "w_ple": 1, "w_ple_gate": 0}
SMALL_WEIGHTS = ("norm_mix_pre", "conv_b", "dt_bias", "a_log", "d_skip", "ssd_norm", "b_gate", "norm_mix_post",
                 "norm_ffn_pre", "norm_ffn_post", "norm_ple_post")
ALL_WEIGHTS = ("norm_mix_pre", "w_in", "conv_w", "conv_b", "dt_bias", "a_log", "d_skip", "ssd_norm", "w_ssd_branch",
               "w_sb_branch", "w_gate", "b_gate", "w_out", "norm_mix_post", "norm_ffn_pre", "w_ff1", "w_ff2",
               "norm_ffn_post", "w_ple", "w_ple_gate", "norm_ple_post")


class Dims(NamedTuple):
    S: int
    D: int
    DI: int
    H: int
    G: int
    CD: int
    SBW: int
    DFF: int
    PLE: int

    @property
    def NA(self):
        return self.DI + self.CD + 3 * self.SBW + 2 * self.D + DT_PAD

    @property
    def off(self):
        o = {}
        o["z"] = 0
        o["xbc"] = self.DI
        o["q"] = self.DI + self.CD
        o["k"] = o["q"] + self.SBW
        o["v"] = o["k"] + self.SBW
        o["gate"] = o["v"] + self.SBW
        o["dt"] = o["gate"] + 2 * self.D
        return o


def _cparams(sem):
    return pltpu.CompilerParams(dimension_semantics=sem, vmem_limit_bytes=VMEM_LIMIT)


def _pick(n, cands):
    for c in cands:
        if n % c == 0:
            return c
    raise ValueError(f"no tile for {n}")


def _grid_call(body, *, grid, in_specs, out_specs, out_shape, scratch, operands, name, rider=None):
    if rider is None:
        sem = ("parallel",) + ("arbitrary",) * (len(grid) - 1)
        return pl.pallas_call(body, grid=grid, in_specs=in_specs, out_specs=out_specs, out_shape=out_shape,
                              scratch_shapes=scratch, compiler_params=_cparams(sem), name=name)(*operands)
    return pl.pallas_call(
        _with_rider(body, rider, grid, len(in_specs), len(out_specs)), grid=grid,
        in_specs=list(in_specs) + [HBM_SPEC] * len(rider.operands),
        out_specs=list(out_specs) + [HBM_SPEC] * len(rider.out_shape),
        out_shape=list(out_shape) + list(rider.out_shape), scratch_shapes=list(scratch) + list(rider.scratch),
        compiler_params=_cparams(("arbitrary",) * len(grid)), name=name)(*operands, *rider.operands)


class SlabOut(NamedTuple):
    buf: object
    row0: int
    rows: int
    axis: int


def matmul(a, b, *, ta=False, tb=False, out_dtype=F32, name, rider=None, slab=None):
    m, k = (a.shape[1], a.shape[0]) if ta else a.shape
    n, kb = b.shape if tb else (b.shape[1], b.shape[0])
    assert k == kb, (a.shape, b.shape, ta, tb)
    tm = _pick(m, (1024, 512, 256, 128))
    tn = _pick(n, (512, 256, 128))
    if slab is not None:
        tm = _pick(math.gcd(slab.rows, slab.row0), (1024, 512, 256, 128, 64, 32, 16, 8))
        tn = _pick(slab.buf.shape[2], (512, 256, 128))
    tk = max(t for t in range(LANES, min(k, MAX_TK) + 1, LANES) if k % t == 0)
    nk = k // tk
    dims = (((0 if ta else 1,), (1 if tb else 0,)), ((), ()))

    def body(a_ref, b_ref, o_ref, acc_ref):
        part = lax.dot_general(a_ref[...].astype(BF16), b_ref[...].astype(BF16), dims, preferred_element_type=F32)
        if nk == 1:
            o_ref[...] = part.astype(o_ref.dtype)
        else:
            kk = pl.program_id(2)

            @pl.when(kk == 0)
            def _():
                acc_ref[...] = part

            @pl.when(kk > 0)
            def _():
                acc_ref[...] += part

            @pl.when(kk == nk - 1)
            def _():
                o_ref[...] = acc_ref[...].astype(o_ref.dtype)

    a_spec = pl.BlockSpec((tk, tm), lambda i, j, kk: (kk, i)) if ta else pl.BlockSpec((tm, tk), lambda i, j, kk: (i, kk))
    b_spec = pl.BlockSpec((tn, tk), lambda i, j, kk: (j, kk)) if tb else pl.BlockSpec((tk, tn), lambda i, j, kk: (kk, j))
    if slab is not None:
        width = slab.buf.shape[2]
        rb0, per_shard = slab.row0 // tm, slab.rows // tm
        if slab.axis == 0:
            where = lambda i, j, kk: (i // per_shard, rb0 + i % per_shard, j)
        else:
            where = lambda i, j, kk: (j // (width // tn), rb0 + i, j % (width // tn))
        return pl.pallas_call(
            lambda a_ref, b_ref, buf_ref, o_ref, acc_ref: body(a_ref, b_ref, o_ref, acc_ref),
            grid=(m // tm, n // tn, nk), in_specs=[a_spec, b_spec, pl.BlockSpec(memory_space=pl.ANY)],
            out_specs=pl.BlockSpec((None, tm, tn), where), out_shape=SDS(slab.buf.shape, slab.buf.dtype),
            scratch_shapes=[pltpu.VMEM((tm, tn), F32)], input_output_aliases={2: 0},
            compiler_params=_cparams(("parallel", "parallel", "arbitrary")), name=name)(a, b, slab.buf)
    res = _grid_call(body, grid=(m // tm, n // tn, nk), in_specs=[a_spec, b_spec],
                     out_specs=[pl.BlockSpec((tm, tn), lambda i, j, kk: (i, j))], out_shape=[SDS((m, n), out_dtype)],
                     scratch=[pltpu.VMEM((tm, tn), F32)], operands=(a, b), name=name, rider=rider)
    return res[0] if rider is None else res


def _row_spec(entry, tile):
    arr, width, cb = entry if isinstance(entry, tuple) else (entry, entry.shape[1], 0)
    return arr, pl.BlockSpec((tile, width), lambda i, cb=cb: (i, cb))


def _par_spec(p):
    return pl.BlockSpec(p.shape, lambda i: (0, 0))


def row_fwd(name, fn, rows, params, outs, tile=256):
    arrs, specs = zip(*[_row_spec(e, tile) for e in rows])
    s = arrs[0].shape[0]
    nr, npar = len(rows), len(params)

    def body(*refs):
        r = [x[...].astype(F32) for x in refs[:nr]]
        p = [x[...] for x in refs[nr:nr + npar]]
        res = fn(*r, *p)
        for o_ref, val in zip(refs[nr + npar:], res):
            o_ref[...] = val.astype(o_ref.dtype)

    return pl.pallas_call(
        body, grid=(s // tile,), in_specs=list(specs) + [_par_spec(p) for p in params],
        out_specs=[pl.BlockSpec((tile, w), lambda i: (i, 0)) for w, _ in outs],
        out_shape=[SDS((s, w), dt) for w, dt in outs],
        compiler_params=_cparams(("parallel",)), name=name)(*arrs, *params)


def row_bwd(name, fn, rows, params, cots, row_grads, tile=256, primal_sum=False):
    arrs, specs = zip(*[_row_spec(e, tile) for e in rows])
    s = arrs[0].shape[0]
    nr, npar = len(rows), len(params)
    cot_entries = [e for c in cots if c is not None for e in c]
    carrs, cspecs = zip(*[_row_spec(e, tile) for e in cot_entries]) if cot_entries else ((), ())
    nc = len(cot_entries)
    want = [i for i, d in enumerate(row_grads) if d is not None]

    def body(*refs):
        r = [x[...].astype(F32) for x in refs[:nr]]
        p = [x[...] for x in refs[nr:nr + npar]]
        cvals = [x[...].astype(F32) for x in refs[nr + npar:nr + npar + nc]]
        outs = refs[nr + npar + nc:]
        prim, vjp = jax.vjp(fn, *r, *p)
        ct, pos = [], 0
        for c, pr in zip(cots, prim):
            if c is None:
                ct.append(jnp.ones_like(pr))
            else:
                acc = cvals[pos]
                for extra in cvals[pos + 1:pos + len(c)]:
                    acc = acc + extra
                pos += len(c)
                ct.append(acc)
        grads = vjp(tuple(ct))
        for o_ref, i in zip(outs[:len(want)], want):
            o_ref[...] = grads[i].astype(o_ref.dtype)
        acc_refs = outs[len(want):]
        vals = [grads[nr + j] for j in range(npar)]
        if primal_sum:
            vals.append(jnp.sum(prim[0], axis=0, keepdims=True))
        first = pl.program_id(0) == 0

        @pl.when(first)
        def _():
            for a_ref, v in zip(acc_refs, vals):
                a_ref[...] = v

        @pl.when(jnp.logical_not(first))
        def _():
            for a_ref, v in zip(acc_refs, vals):
                a_ref[...] += v

    widths = [(e[1] if isinstance(e, tuple) else e.shape[1]) for e in rows]
    out_specs = [pl.BlockSpec((tile, widths[i]), lambda i_: (i_, 0)) for i in want]
    out_shape = [SDS((s, widths[i]), row_grads[i]) for i in want]
    pshapes = [p.shape for p in params]
    if primal_sum:
        pshapes.append((1, widths[0]))
    out_specs += [pl.BlockSpec(sh, lambda i_: (0, 0)) for sh in pshapes]
    out_shape += [SDS(sh, F32) for sh in pshapes]
    res = pl.pallas_call(
        body, grid=(s // tile,), in_specs=list(specs) + [_par_spec(p) for p in params] + list(cspecs),
        out_specs=out_specs, out_shape=out_shape,
        compiler_params=_cparams(("arbitrary",)), name=name)(*arrs, *params, *carrs)
    return res[:len(want)], res[len(want):]


def _rms(x, w):
    return x * lax.rsqrt(jnp.mean(x * x, axis=-1, keepdims=True) + RMS_EPS) * w


def _sigmoid(x):
    return jax.nn.sigmoid(x)


def _softplus(x):
    return jnp.maximum(x, 0.0) + jnp.log1p(jnp.exp(-jnp.abs(x)))


def f_norm1(x, w):
    return (_rms(x, w),)


def f_dt(raw, bias):
    return (_softplus(raw + bias),)


def f_merge(gp_ssd, gp_sb, yb_ssd, yb_sb, b_ssd, b_sb):
    return (_sigmoid(gp_ssd + b_ssd) * yb_ssd + _sigmoid(gp_sb + b_sb) * yb_sb,)


def f_mix_out(x, mo, w_post, w_pre):
    h1 = x + _rms(mo, w_post)
    return h1, _rms(h1, w_pre)


def f_relu2(a1):
    return (jnp.square(jnp.maximum(a1, 0.0)),)


def f_ffn_out(h1, ff, w):
    return (h1 + _rms(ff, w),)


def f_ple_loss(h2, pg, pe, tgt, w):
    h3 = h2 + _rms(_sigmoid(pg) * pe, w)
    return (0.5 * jnp.square(h3 - tgt) * (1.0 / h2.shape[-1]),)


def _shift_down(u, d, rows):
    return u if d == 0 else jnp.where(rows >= d, pltpu.roll(u, d, 0), 0.0)


def _shift_up(u, d, rows):
    s = u.shape[0]
    return u if d == 0 else jnp.where(rows < s - d, pltpu.roll(u, s - d, 0), 0.0)


def conv_fwd(proj, col0, cd, conv_w, conv_b, name):
    s = proj.shape[0]
    cb0 = col0 // CONV_COLS

    def body(u_ref, w_ref, b_ref, o_ref):
        u = u_ref[...]
        rows = lax.broadcasted_iota(jnp.int32, u.shape, 0)
        y = jnp.broadcast_to(b_ref[...], u.shape)
        for k in range(CONV_K):
            y = y + w_ref[k:k + 1, :] * _shift_down(u, CONV_K - 1 - k, rows)
        o_ref[...] = y * _sigmoid(y)

    return pl.pallas_call(
        body, grid=(cd // CONV_COLS,),
        in_specs=[pl.BlockSpec((s, CONV_COLS), lambda i: (0, cb0 + i)),
                  pl.BlockSpec((CONV_K, CONV_COLS), lambda i: (0, i)),
                  pl.BlockSpec((1, CONV_COLS), lambda i: (0, i))],
        out_specs=pl.BlockSpec((s, CONV_COLS), lambda i: (0, i)),
        out_shape=SDS((s, cd), F32), compiler_params=_cparams(("parallel",)), name=name)(proj, conv_w, conv_b)


def conv_bwd(proj, col0, dout, ch0, conv_w, conv_b, dproj, name):
    s = proj.shape[0]
    ncb = dout.shape[1] // CONV_COLS
    cb0 = (col0 + ch0) // CONV_COLS
    wb0 = ch0 // CONV_COLS

    def body(u_ref, g_ref, w_ref, b_ref, _, du_ref, dw_ref, db_ref):
        u = u_ref[...]
        rows = lax.broadcasted_iota(jnp.int32, u.shape, 0)
        y = jnp.broadcast_to(b_ref[...], u.shape)
        for k in range(CONV_K):
            y = y + w_ref[k:k + 1, :] * _shift_down(u, CONV_K - 1 - k, rows)
        sg = _sigmoid(y)
        dy = g_ref[...] * (sg * (1.0 + y * (1.0 - sg)))
        du = jnp.zeros_like(u)
        for k in range(CONV_K):
            d = CONV_K - 1 - k
            du = du + w_ref[k:k + 1, :] * _shift_up(dy, d, rows)
            dw_ref[k:k + 1, :] = jnp.sum(dy * _shift_down(u, d, rows), axis=0, keepdims=True)
        du_ref[...] = du.astype(du_ref.dtype)
        db_ref[...] = jnp.sum(dy, axis=0, keepdims=True)

    return pl.pallas_call(
        body, grid=(ncb,),
        in_specs=[pl.BlockSpec((s, CONV_COLS), lambda i: (0, cb0 + i)),
                  pl.BlockSpec((s, CONV_COLS), lambda i: (0, i)),
                  pl.BlockSpec((CONV_K, CONV_COLS), lambda i: (0, wb0 + i)),
                  pl.BlockSpec((1, CONV_COLS), lambda i: (0, wb0 + i)),
                  pl.BlockSpec(memory_space=pl.ANY)],
        out_specs=[pl.BlockSpec((s, CONV_COLS), lambda i: (0, cb0 + i)),
                   pl.BlockSpec((CONV_K, CONV_COLS), lambda i: (0, i)),
                   pl.BlockSpec((1, CONV_COLS), lambda i: (0, i))],
        out_shape=[SDS(dproj.shape, dproj.dtype), SDS((CONV_K, ncb * CONV_COLS), F32), SDS((1, ncb * CONV_COLS), F32)],
        input_output_aliases={4: 0},
        compiler_params=_cparams(("parallel",)), name=name)(proj, dout, conv_w, conv_b, dproj)


def _dot(a, b, dims):
    return lax.dot_general(a.astype(BF16), b.astype(BF16), (dims, ((), ())), preferred_element_type=F32)


NN = ((1,), (0,))
NT = ((1,), (1,))
TN = ((0,), (0,))


def ssd_chunk(xs, zs, nw, dtc, dtw, alogs, dsks, bm, cm, prev):
    ln = bm.shape[0]
    gw = GROUP_HEADS * HEAD_DIM
    row = lax.broadcasted_iota(jnp.int32, (ln, ln), 0)
    col = lax.broadcasted_iota(jnp.int32, (ln, ln), 1)
    causal = row >= col
    tri = causal.astype(F32)
    tri_t = (row <= col).astype(F32)
    lane_head = lax.broadcasted_iota(jnp.int32, (1, gw), 1) // HEAD_DIM
    sub_head = lax.broadcasted_iota(jnp.int32, (gw, 1), 0) // HEAD_DIM
    on_lanes = [(lane_head == r).astype(F32) for r in range(GROUP_HEADS)]
    on_rows = [(sub_head == r).astype(F32) for r in range(GROUP_HEADS)]
    cb = _dot(cm, bm, NT)
    decays, dt_full, acs_full, end_full, dsk_full, end_rows = [], 0.0, 0.0, 0.0, 0.0, 0.0
    for r in range(GROUP_HEADS):
        a = -jnp.exp(alogs[r])
        da_c = dtc[r] * a
        da_w = dtw[r] * a
        acs_c = jnp.sum(tri * da_w, axis=1, keepdims=True)
        acs_w = jnp.sum(tri_t * da_c, axis=0, keepdims=True)
        alast = jnp.sum(da_w, axis=1, keepdims=True)
        decays.append(jnp.exp(jnp.where(causal, acs_c - acs_w, -jnp.inf)))
        dt_full = dt_full + dtc[r] * on_lanes[r]
        acs_full = acs_full + acs_c * on_lanes[r]
        end_full = end_full + alast * on_lanes[r]
        dsk_full = dsk_full + dsks[r] * on_lanes[r]
        end_rows = end_rows + alast * on_rows[r]
    xd = xs * dt_full
    y = xs * dsk_full + _dot(cm, prev, NT) * jnp.exp(acs_full)
    for r in range(GROUP_HEADS):
        y = y + _dot(cb * decays[r], xd * on_lanes[r], NN)
    new_prev = prev * jnp.exp(end_rows) + _dot(xd * jnp.exp(end_full - acs_full), bm, TN)
    yg = y * (zs * _sigmoid(zs))
    rstd = lax.rsqrt(jnp.mean(yg * yg, axis=-1, keepdims=True) + RMS_EPS)
    return yg * rstd * nw, new_prev


SSD_STEP_GROUPS = 1


def _ssd_specs(dm, cidx):
    u = SSD_STEP_GROUPS
    gw = GROUP_HEADS * HEAD_DIM
    nb0 = dm.DI // D_STATE
    assert nb0 % u == 0 and dm.G % u == 0
    par = pl.BlockSpec((u, 1, GROUP_HEADS), lambda g, c: (g, 0, 0))
    return dict(
        z=pl.BlockSpec((CHUNK, u * gw), lambda g, c: (cidx(c), g)),
        xs=pl.BlockSpec((CHUNK, u * gw), lambda g, c: (cidx(c), g)),
        b=pl.BlockSpec((CHUNK, u * D_STATE), lambda g, c: (cidx(c), nb0 // u + g)),
        c=pl.BlockSpec((CHUNK, u * D_STATE), lambda g, c: (cidx(c), (nb0 + dm.G) // u + g)),
        dtc=pl.BlockSpec((u, CHUNK, GROUP_HEADS), lambda g, c: (g, cidx(c), 0)),
        dtw=pl.BlockSpec((u, GROUP_HEADS, CHUNK), lambda g, c: (g, 0, cidx(c))),
        par=par,
        nw=pl.BlockSpec((1, u * gw), lambda g, c: (0, g)),
        st=pl.BlockSpec((u, None, gw, D_STATE), lambda g, c: (g, cidx(c), 0, 0)),
    )


def _ssd_load(k, z_ref, xs_ref, b_ref, c_ref, dtc_ref, dtw_ref, alog_ref, dsk_ref, nw_ref):
    gw = GROUP_HEADS * HEAD_DIM
    wide, narrow = slice(k * gw, (k + 1) * gw), slice(k * D_STATE, (k + 1) * D_STATE)
    dtc = tuple(dtc_ref[k, :, r:r + 1] for r in range(GROUP_HEADS))
    dtw = tuple(dtw_ref[k, r:r + 1, :] for r in range(GROUP_HEADS))
    alogs = tuple(alog_ref[k, :, r:r + 1] for r in range(GROUP_HEADS))
    dsks = tuple(dsk_ref[k, :, r:r + 1] for r in range(GROUP_HEADS))
    return xs_ref[:, wide], z_ref[:, wide], nw_ref[:, wide], dtc, dtw, alogs, dsks, b_ref[:, narrow], c_ref[:, narrow]


def ssd_fwd(proj, xbc, dtc, dtw, alog, dsk, nw, dm, name):
    nc = dm.S // CHUNK
    u = SSD_STEP_GROUPS
    gw = GROUP_HEADS * HEAD_DIM
    sp = _ssd_specs(dm, lambda c: c)

    def body(z_ref, xs_ref, b_ref, c_ref, dtc_ref, dtw_ref, alog_ref, dsk_ref, nw_ref, y_ref, st_ref, prev):
        @pl.when(pl.program_id(1) == 0)
        def _():
            prev[...] = jnp.zeros_like(prev)

        st_ref[...] = prev[...]
        for k in range(u):
            args = _ssd_load(k, z_ref, xs_ref, b_ref, c_ref, dtc_ref, dtw_ref, alog_ref, dsk_ref, nw_ref)
            out, new = ssd_chunk(*args, prev[k])
            y_ref[:, k * gw:(k + 1) * gw] = out.astype(y_ref.dtype)
            prev[k] = new

    return pl.pallas_call(
        body, grid=(dm.G // u, nc),
        in_specs=[sp["z"], sp["xs"], sp["b"], sp["c"], sp["dtc"], sp["dtw"], sp["par"], sp["par"], sp["nw"]],
        out_specs=[sp["xs"], sp["st"]],
        out_shape=[SDS((dm.S, dm.DI), BF16), SDS((dm.G, nc, gw, D_STATE), F32)],
        scratch_shapes=[pltpu.VMEM((u, gw, D_STATE), F32)],
        compiler_params=_cparams(("parallel", "arbitrary")), name=name)(proj, xbc, xbc, xbc, dtc, dtw, alog, dsk, nw)


def ssd_bwd(proj, xbc, dtc, dtw, alog, dsk, nw, states, dy, dm, name):
    nc = dm.S // CHUNK
    u = SSD_STEP_GROUPS
    sp = _ssd_specs(dm, lambda c: nc - 1 - c)
    gw = GROUP_HEADS * HEAD_DIM
    bc_spec = pl.BlockSpec((CHUNK, u * D_STATE), lambda g, c: (nc - 1 - c, g))

    def body(z_ref, xs_ref, b_ref, c_ref, dtc_ref, dtw_ref, alog_ref, dsk_ref, nw_ref, st_ref, dy_ref,
             dz_ref, dxs_ref, db_ref, dc_ref, ddtc_ref, ddtw_ref, dalog_ref, ddsk_ref, dnw_ref, dprev):
        first = pl.program_id(1) == 0

        @pl.when(first)
        def _():
            dprev[...] = jnp.zeros_like(dprev)

        param_grads = []
        for k in range(u):
            wide, narrow = slice(k * gw, (k + 1) * gw), slice(k * D_STATE, (k + 1) * D_STATE)
            args = _ssd_load(k, z_ref, xs_ref, b_ref, c_ref, dtc_ref, dtw_ref, alog_ref, dsk_ref, nw_ref)
            _, vjp = jax.vjp(ssd_chunk, *args, st_ref[k])
            gxs, gzs, gnw, gdtc, gdtw, galogs, gdsks, gb, gc, gprev = vjp((dy_ref[:, wide], dprev[k]))
            dxs_ref[:, wide] = gxs
            dz_ref[:, wide] = gzs.astype(dz_ref.dtype)
            db_ref[:, narrow] = gb
            dc_ref[:, narrow] = gc
            dprev[k] = gprev
            for r in range(GROUP_HEADS):
                ddtc_ref[k, :, r:r + 1] = gdtc[r]
                ddtw_ref[k, r:r + 1, :] = gdtw[r]
            param_grads.append((wide, gnw, galogs, gdsks))

        @pl.when(first)
        def _():
            for k, (wide, gnw, galogs, gdsks) in enumerate(param_grads):
                dnw_ref[:, wide] = gnw
                for r in range(GROUP_HEADS):
                    dalog_ref[k, :, r:r + 1] = galogs[r]
                    ddsk_ref[k, :, r:r + 1] = gdsks[r]

        @pl.when(jnp.logical_not(first))
        def _():
            for k, (wide, gnw, galogs, gdsks) in enumerate(param_grads):
                dnw_ref[:, wide] += gnw
                for r in range(GROUP_HEADS):
                    dalog_ref[k, :, r:r + 1] += galogs[r]
                    ddsk_ref[k, :, r:r + 1] += gdsks[r]

    xs_out = pl.BlockSpec((CHUNK, u * gw), lambda g, c: (nc - 1 - c, g))
    return pl.pallas_call(
        body, grid=(dm.G // u, nc),
        in_specs=[sp["z"], sp["xs"], sp["b"], sp["c"], sp["dtc"], sp["dtw"], sp["par"], sp["par"], sp["nw"],
                  sp["st"], xs_out],
        out_specs=[xs_out, xs_out, bc_spec, bc_spec, sp["dtc"], sp["dtw"], sp["par"], sp["par"], sp["nw"]],
        out_shape=[SDS((dm.S, dm.NA), BF16), SDS((dm.S, dm.DI), F32), SDS((dm.S, dm.G * D_STATE), F32),
                   SDS((dm.S, dm.G * D_STATE), F32), SDS((dm.G, dm.S, GROUP_HEADS), F32), SDS((dm.G, GROUP_HEADS, dm.S), F32),
                   SDS((dm.G, 1, GROUP_HEADS), F32), SDS((dm.G, 1, GROUP_HEADS), F32), SDS((1, dm.DI), F32)],
        scratch_shapes=[pltpu.VMEM((u, gw, D_STATE), F32)],
        compiler_params=_cparams(("parallel", "arbitrary")), name=name)(
            proj, xbc, xbc, xbc, dtc, dtw, alog, dsk, nw, states, dy)


def _split_bf16(v):
    hi = v.astype(BF16)
    return hi, (v - hi.astype(F32)).astype(BF16)


def _tri(v, mat):
    hi, lo = _split_bf16(v)
    return jnp.dot(hi, mat, preferred_element_type=F32) + jnp.dot(lo, mat, preferred_element_type=F32)


def _blocks(v):
    return [v[:, b * ATT_TILE:(b + 1) * ATT_TILE] for b in range(v.shape[1] // ATT_TILE)]


def _sb_group(z, mask, run, after_mat):
    sp = jnp.maximum(z, 0.0) + jnp.log2(1.0 + jnp.exp2(-jnp.abs(z)))
    lk = -sp if mask is None else jnp.where(mask, -sp, 0.0)
    cums = [_tri(v, after_mat) for v in _blocks(lk)]
    sums = [jnp.sum(v, axis=1, keepdims=True) for v in _blocks(lk)]
    later = [None] * len(cums)
    for b in reversed(range(len(cums))):
        later[b] = run + cums[b]
        run = run + sums[b]
    ls = z - sp
    w = jnp.exp2(ls + jnp.concatenate(later, axis=1))
    if mask is not None:
        w = jnp.where(mask, w, 0.0)
    return ls, w, run


def _alive(run_a, run_b):
    return (jnp.max(jnp.maximum(run_a, run_b)) > ATT_DEAD).astype(jnp.int32)


def _group_mask(i, g, t, gw):
    rows = i * t + lax.broadcasted_iota(jnp.int32, (t, gw), 0)
    cols = g * gw + lax.broadcasted_iota(jnp.int32, (t, gw), 1)
    return cols < rows


def _window(i, jj, t):
    gw = ATT_GROUP * t
    end = (i + 1 - ATT_GROUP * jj) * t
    r0 = pl.multiple_of(jnp.maximum(end - gw, 0), t)
    rows = i * t + lax.broadcasted_iota(jnp.int32, (t, gw), 0)
    cols = r0 + lax.broadcasted_iota(jnp.int32, (t, gw), 1)
    return r0, jnp.logical_and(cols < rows, cols < end)


def _att_specs(dm, s):
    t = ATT_TILE
    qb, kb, vb = dm.off["q"] // LANES, dm.off["k"] // LANES, dm.off["v"] // LANES
    return (pl.BlockSpec((t, LANES), lambda p, i: (i, qb + p)),
            pl.BlockSpec((s, LANES), lambda p, i: (0, kb + p)),
            pl.BlockSpec((s, LANES), lambda p, i: (0, vb + p)))


def attn_fwd(proj, dm, name, rider=None):
    s, t = dm.S, ATT_TILE
    scale = HEAD_DIM ** -0.5
    hsl = [slice(h * HEAD_DIM, (h + 1) * HEAD_DIM) for h in range(2)]

    gw = ATT_GROUP * t

    def body(q_ref, k_ref, v_ref, o_ref):
        i = pl.program_id(1)
        gd = i // ATT_GROUP
        r_io = lax.broadcasted_iota(jnp.int32, (t, t), 0)
        c_io = lax.broadcasted_iota(jnp.int32, (t, t), 1)
        after_mat = (r_io > c_io).astype(BF16)
        qs = [q_ref[:, sl].astype(BF16) for sl in hsl]

        def group(jj, carry):
            r0, mask = _window(i, jj, t)
            zs = [_dot(qs[h], k_ref[pl.ds(r0, gw), hsl[h]], NT) * (scale * LOG2E) for h in range(2)]
            res = [_sb_group(zs[h], mask, carry[h][0], after_mat) for h in range(2)]
            return tuple((res[h][2], carry[h][1] + _dot(res[h][1], v_ref[pl.ds(r0, gw), hsl[h]], NN)) for h in range(2))

        zero = (jnp.zeros((t, 1), F32), jnp.zeros((t, HEAD_DIM), F32))
        carry = group(0, (zero, zero))

        def step(st):
            jj, _, c = st
            c = group(jj, c)
            return jj + 1, _alive(c[0][0], c[1][0]), c

        _, _, carry = lax.while_loop(lambda st: jnp.logical_and(st[0] <= gd, st[1] > 0), step,
                                     (jnp.int32(1), _alive(carry[0][0], carry[1][0]), carry))
        for h in range(2):
            o_ref[:, hsl[h]] = carry[h][1]

    qs_, ks_, vs_ = _att_specs(dm, s)
    res = _grid_call(body, grid=(dm.SBW // LANES, s // t), in_specs=[qs_, ks_, vs_],
                     out_specs=[pl.BlockSpec((t, LANES), lambda p, i: (i, p))], out_shape=[SDS((s, dm.SBW), F32)],
                     scratch=[], operands=(proj, proj, proj), name=name, rider=rider)
    return res[0] if rider is None else res


def attn_bwd(proj, do, dm, name, rider=None):
    s, t = dm.S, ATT_TILE
    nq = s // t
    gw = ATT_GROUP * t
    scale = HEAD_DIM ** -0.5
    hsl = [slice(h * HEAD_DIM, (h + 1) * HEAD_DIM) for h in range(2)]

    def body(q_ref, k_ref, v_ref, do_ref, dq_ref, dk_ref, dv_ref, dk_acc, dv_acc, g_scr, s_scr):
        i = pl.program_id(1)

        @pl.when(i == 0)
        def _():
            dk_acc[...] = jnp.zeros_like(dk_acc)
            dv_acc[...] = jnp.zeros_like(dv_acc)

        gd = i // ATT_GROUP
        r_io = lax.broadcasted_iota(jnp.int32, (t, t), 0)
        c_io = lax.broadcasted_iota(jnp.int32, (t, t), 1)
        after_mat = (r_io > c_io).astype(BF16)
        before_mat = (r_io < c_io).astype(BF16)
        qs = [q_ref[:, sl].astype(BF16) for sl in hsl]
        dos = [do_ref[:, sl].astype(BF16) for sl in hsl]
        q_t = q_ref[...].T.astype(BF16)
        do_t = do_ref[...].T.astype(BF16)

        def pass1(jj, runs):
            r0, mask = _window(i, jj, t)
            zs = [_dot(qs[h], k_ref[pl.ds(r0, gw), hsl[h]], NT) * (scale * LOG2E) for h in range(2)]
            dws = [_dot(dos[h], v_ref[pl.ds(r0, gw), hsl[h]], NT) for h in range(2)]
            out = []
            for h in range(2):
                ls, w, run = _sb_group(zs[h], mask, runs[h], after_mat)
                g_scr[h, jj] = dws[h] * w
                s_scr[h, jj] = jnp.exp2(ls)
                dv_acc[hsl[h], pl.ds(r0, gw)] += _dot(do_t[hsl[h]], w, NN)
                out.append(run)
            return tuple(out)

        zero_col = jnp.zeros((t, 1), F32)
        runs = pass1(0, (zero_col, zero_col))

        def step1(st):
            jj, _, r = st
            r = pass1(jj, r)
            return jj + 1, _alive(r[0], r[1]), r

        walked, _, _ = lax.while_loop(lambda st: jnp.logical_and(st[0] <= gd, st[1] > 0), step1,
                                      (jnp.int32(1), _alive(runs[0], runs[1]), runs))

        def pass2(jj, carry):
            r0, mask = _window(i, jj, t)
            out = []
            for h in range(2):
                pre, dq = carry[h]
                gg = g_scr[h, jj]
                sig = s_scr[h, jj]
                before = []
                for v in _blocks(gg):
                    before.append(pre + _tri(v, before_mat))
                    pre = pre + jnp.sum(v, axis=1, keepdims=True)
                dz = jnp.where(mask, gg * (1.0 - sig) - jnp.concatenate(before, axis=1) * sig, 0.0)
                dz = (dz * scale).astype(BF16)
                dk_acc[hsl[h], pl.ds(r0, gw)] += _dot(q_t[hsl[h]], dz, NN)
                out.append((pre, dq + _dot(dz, k_ref[pl.ds(r0, gw), hsl[h]], NN)))
            return tuple(out)

        zero = (zero_col, jnp.zeros((t, HEAD_DIM), F32))
        carry = lax.fori_loop(1, walked, lambda n, c: pass2(walked - n, c), (zero, zero))
        carry = pass2(0, carry)
        for h in range(2):
            dq_ref[:, hsl[h]] = carry[h][1].astype(dq_ref.dtype)

        @pl.when(i == nq - 1)
        def _():
            for g in range(s // gw):
                dk_ref[g * gw:(g + 1) * gw, :] = dk_acc[:, g * gw:(g + 1) * gw].T.astype(dk_ref.dtype)
                dv_ref[g * gw:(g + 1) * gw, :] = dv_acc[:, g * gw:(g + 1) * gw].T.astype(dv_ref.dtype)

    qs_, ks_, vs_ = _att_specs(dm, s)
    tile_spec = pl.BlockSpec((t, LANES), lambda p, i: (i, p))
    full_spec = pl.BlockSpec((s, LANES), lambda p, i: (0, p))
    return _grid_call(
        body, grid=(dm.SBW // LANES, nq), in_specs=[qs_, ks_, vs_, tile_spec],
        out_specs=[tile_spec, full_spec, full_spec], out_shape=[SDS((s, dm.SBW), BF16)] * 3,
        scratch=[pltpu.VMEM((LANES, s), F32), pltpu.VMEM((LANES, s), F32),
                 pltpu.VMEM((2, s // gw, t, gw), F32), pltpu.VMEM((2, s // gw, t, gw), F32)],
        operands=(proj, proj, proj, do), name=name, rider=rider)


def adamw(w, g, m, v, name):
    rows, width = w.shape
    tile = _pick(rows, (256, 64, 16, 8, 4))
    c1 = 1.0 / (1.0 - ADAM_B1 ** ADAM_STEP)
    c2 = 1.0 / (1.0 - ADAM_B2 ** ADAM_STEP)

    def body(w_ref, g_ref, m_ref, v_ref, d_ref, nm_ref, nv_ref):
        gg = g_ref[...]
        nm = ADAM_B1 * m_ref[...] + (1.0 - ADAM_B1) * gg
        nv = ADAM_B2 * v_ref[...] + (1.0 - ADAM_B2) * (gg * gg)
        d_ref[...] = -ADAM_LR * ((nm * c1) / (jnp.sqrt(nv * c2) + ADAM_EPS) + ADAM_WD * w_ref[...])
        nm_ref[...] = nm
        nv_ref[...] = nv

    spec = pl.BlockSpec((tile, width), lambda i: (i, 0))
    return pl.pallas_call(
        body, grid=(rows // tile,), in_specs=[spec] * 4, out_specs=[spec] * 3,
        out_shape=[SDS((rows, width), F32)] * 3, compiler_params=_cparams(("parallel",)), name=name)(w, g, m, v)


def _me():
    return lax.axis_index("x"), lax.axis_index("y"), lax.axis_index("c")


def _other_chips(x, y):
    return [(1 - x, y), (x, 1 - y), (1 - x, 1 - y)]


def gather_weights(wp):
    rows, width = wp.shape
    half = rows // 2

    def body(w_ref, out_ref, send_sems, recv_sems):
        x, y, c = _me()
        sibling = (x, y, 1 - c)
        chips = _other_chips(x, y)

        def part(cx, cy, hf):
            return out_ref.at[2 * cx + cy, hf]

        def copy(k, src, dst, to):
            return pltpu.make_async_remote_copy(src_ref=src, dst_ref=dst, send_sem=send_sems.at[k], recv_sem=recv_sems.at[k],
                                                device_id=to, device_id_type=MESH_ID)

        first = [copy(j, w_ref.at[c], part(x, y, c), (cx, cy, c)) for j, (cx, cy) in enumerate(chips)]
        for cp in first:
            cp.start()
        passed = [copy(3 + j, part(cx, cy, c), part(cx, cy, c), sibling) for j, (cx, cy) in enumerate(chips)]
        for j, (cx, cy) in enumerate(chips):
            copy(j, part(cx, cy, c), part(cx, cy, c), (x, y, c)).wait_recv()
            passed[j].start()
        for j, (cx, cy) in enumerate(chips):
            copy(3 + j, part(cx, cy, 1 - c), part(cx, cy, 1 - c), (x, y, c)).wait_recv()
        for cp in first + passed:
            cp.wait_send()

    return pl.pallas_call(
        body, out_shape=SDS((N_CHIPS, 2, half, width), wp.dtype), in_specs=[HBM_SPEC], out_specs=HBM_SPEC,
        scratch_shapes=[pltpu.SemaphoreType.DMA((6,)), pltpu.SemaphoreType.DMA((6,))],
        name="gather_weights")(wp.reshape(2, half, width)).reshape(N_CHIPS, rows, width)


class Rider(NamedTuple):
    operands: tuple
    out_shape: tuple
    scratch: tuple
    start: object
    wait: object


def _with_rider(body, rider, grid, n_in, n_out):
    if rider is None:
        return body
    r_in, r_out = len(rider.operands), len(rider.out_shape)

    def full(*refs):
        ins, refs = refs[:n_in], refs[n_in:]
        rins, refs = refs[:r_in], refs[r_in:]
        outs, refs = refs[:n_out], refs[n_out:]
        routs, refs = refs[:r_out], refs[r_out:]
        scr, rscr = refs[:len(refs) - len(rider.scratch)], refs[len(refs) - len(rider.scratch):]
        ids = [pl.program_id(k) for k in range(len(grid))]
        first = functools.reduce(jnp.logical_and, [i == 0 for i in ids])
        last = functools.reduce(jnp.logical_and, [i == g - 1 for i, g in zip(ids, grid)])

        @pl.when(first)
        def _():
            rider.start(rins, routs, rscr)

        body(*ins, *outs, *scr)

        @pl.when(last)
        def _():
            rider.wait(rins, routs, rscr)

    return full


def gather_rider(wp):
    def copies(ins, outs, scr, sending):
        (w_ref,), (o_ref,), (send_sems, recv_sems) = ins, outs, scr
        x, y, c = _me()
        return [pltpu.make_async_remote_copy(src_ref=w_ref, dst_ref=o_ref.at[2 * x + y if sending else 2 * cx + cy],
                                             send_sem=send_sems.at[j], recv_sem=recv_sems.at[j], device_id=(cx, cy, c),
                                             device_id_type=MESH_ID)
                for j, (cx, cy) in enumerate(_other_chips(x, y))]

    def start(ins, outs, scr):
        for cp in copies(ins, outs, scr, True):
            cp.start()

    def wait(ins, outs, scr):
        for cp in copies(ins, outs, scr, False):
            cp.wait()

    return Rider((wp,), (SDS((N_CHIPS,) + wp.shape, wp.dtype),),
                 (pltpu.SemaphoreType.DMA((3,)), pltpu.SemaphoreType.DMA((3,))), start, wait)


def own_slab(gathered, wp, chip_idx):
    return lax.dynamic_update_slice(gathered, wp[None], (chip_idx[0], 0, 0))


def exchange_rider(sh):
    def copies(ins, outs, scr):
        (s_ref,), (b_ref,), (send_sems, recv_sems) = ins, outs, scr
        x, y, c = _me()
        return [pltpu.make_async_remote_copy(src_ref=s_ref.at[2 * cx + cy], dst_ref=b_ref.at[j], send_sem=send_sems.at[j],
                                             recv_sem=recv_sems.at[j], device_id=(cx, cy, c), device_id_type=MESH_ID)
                for j, (cx, cy) in enumerate(_other_chips(x, y))]

    def start(ins, outs, scr):
        for cp in copies(ins, outs, scr):
            cp.start()

    def wait(ins, outs, scr):
        for cp in copies(ins, outs, scr):
            cp.wait()

    return Rider((sh,), (SDS((3,) + sh.shape[1:], sh.dtype),),
                 (pltpu.SemaphoreType.DMA((3,)), pltpu.SemaphoreType.DMA((3,))), start, wait)


def swap_halves(g, name):
    n, rows, width = g.shape
    half = rows // 2

    def body(g_ref, a_ref, send_sem, recv_sem):
        x, y, c = _me()
        cp = pltpu.make_async_remote_copy(src_ref=g_ref.at[:, 1 - c], dst_ref=a_ref,
                                          send_sem=send_sem, recv_sem=recv_sem, device_id=(x, y, 1 - c), device_id_type=MESH_ID)
        cp.start()
        cp.wait()

    return pl.pallas_call(
        body, out_shape=SDS((n, half, width), g.dtype), in_specs=[HBM_SPEC], out_specs=HBM_SPEC,
        scratch_shapes=[pltpu.SemaphoreType.DMA, pltpu.SemaphoreType.DMA], name=name)(g.reshape(n, 2, half, width))


def add_half(g, a, c_idx, name):
    n, rows, width = g.shape
    half = rows // 2
    tile = half // 2
    nt = half // tile

    def body(c_ref, g_ref, a_ref, o_ref, ob_ref):
        v = g_ref[...] + a_ref[...]
        o_ref[...] = v
        ob_ref[...] = v.astype(ob_ref.dtype)

    out_spec = pl.BlockSpec((None, tile, width), lambda s, i, c_ref: (s, i, 0))
    gs = pltpu.PrefetchScalarGridSpec(
        num_scalar_prefetch=1, grid=(n, nt),
        in_specs=[pl.BlockSpec((None, tile, width), lambda s, i, c_ref: (s, c_ref[0] * nt + i, 0)), out_spec],
        out_specs=[out_spec, out_spec])
    return pl.pallas_call(body, grid_spec=gs, out_shape=[SDS((n, half, width), F32), SDS((n, half, width), BF16)],
                          compiler_params=_cparams(("parallel", "parallel")), name=name)(c_idx, g, a)


def exchange_small(small):
    def body(sm_ref, all_ref, send_sems, recv_sems, local_sem):
        x, y, c = _me()
        mine = pltpu.make_async_copy(sm_ref, all_ref.at[0], local_sem)
        mine.start()
        copies = []
        for m in range(1, N_DEV):
            peer = (x ^ ((m >> 2) & 1), y ^ ((m >> 1) & 1), c ^ (m & 1))
            copies.append(pltpu.make_async_remote_copy(
                src_ref=sm_ref, dst_ref=all_ref.at[m], send_sem=send_sems.at[m - 1], recv_sem=recv_sems.at[m - 1],
                device_id=peer, device_id_type=MESH_ID))
        for cp in copies:
            cp.start()
        for cp in copies:
            cp.wait()
        mine.wait()

    return pl.pallas_call(
        body, out_shape=SDS((N_DEV,) + small.shape, small.dtype), in_specs=[HBM_SPEC], out_specs=HBM_SPEC,
        scratch_shapes=[pltpu.SemaphoreType.DMA((N_DEV - 1,)), pltpu.SemaphoreType.DMA((N_DEV - 1,)), pltpu.SemaphoreType.DMA],
        name="exchange_small")(small)


def add_chips(sh, b, k_idx, name):
    n, hf, width = sh.shape
    tile = hf // 2

    def body(k_ref, s_ref, b0, b1, b2, o_ref):
        o_ref[...] = ((s_ref[...] + b0[...].astype(F32)) + b1[...].astype(F32)) + b2[...].astype(F32)

    def bspec(j):
        return pl.BlockSpec((None, tile, width), lambda i, k_ref, j=j: (j, i, 0))

    gs = pltpu.PrefetchScalarGridSpec(
        num_scalar_prefetch=1, grid=(hf // tile,),
        in_specs=[pl.BlockSpec((None, tile, width), lambda i, k_ref: (k_ref[0], i, 0)), bspec(0), bspec(1), bspec(2)],
        out_specs=pl.BlockSpec((tile, width), lambda i, k_ref: (i, 0)))
    return pl.pallas_call(body, grid_spec=gs, out_shape=SDS((hf, width), sh.dtype),
                          compiler_params=_cparams(("parallel",)), name=name)(k_idx, sh, b, b, b)


def sum_small(allsm, me_idx, name):
    _, rows, width = allsm.shape

    def body(me_ref, a_ref, o_ref):
        me = me_ref[0]
        acc = a_ref[me]
        for dev in range(1, N_DEV):
            acc = acc + a_ref[jnp.bitwise_xor(me, dev)]
        o_ref[...] = acc

    gs = pltpu.PrefetchScalarGridSpec(
        num_scalar_prefetch=1, grid=(1,),
        in_specs=[pl.BlockSpec((N_DEV, rows, width), lambda i, me_ref: (0, 0, 0))],
        out_specs=pl.BlockSpec((rows, width), lambda i, me_ref: (0, 0)))
    return pl.pallas_call(body, grid_spec=gs, out_shape=SDS((rows, width), allsm.dtype),
                          compiler_params=_cparams(("arbitrary",)), name=name)(me_idx, allsm)


def join_halves(t, core_idx, name):
    hf, width = t.shape

    def body(t_ref, o_ref, send_sem, recv_sem):
        x, y, c = _me()
        cp = pltpu.make_async_remote_copy(src_ref=t_ref, dst_ref=o_ref, send_sem=send_sem, recv_sem=recv_sem,
                                          device_id=(x, y, 1 - c), device_id_type=MESH_ID)
        cp.start()
        cp.wait()

    theirs = pl.pallas_call(
        body, out_shape=SDS((hf, width), t.dtype), in_specs=[HBM_SPEC], out_specs=HBM_SPEC,
        scratch_shapes=[pltpu.SemaphoreType.DMA, pltpu.SemaphoreType.DMA], name=name)(t)
    return jnp.where(core_idx[0] == 0, jnp.concatenate([t, theirs], axis=0), jnp.concatenate([theirs, t], axis=0))


ROW_PAD = 64


def _pad_rows(rows):
    return -(-rows // ROW_PAD) * ROW_PAD


def _pack_rows(names, shard_shapes, width):
    return _pad_rows(sum((shard_shapes[n][0] * shard_shapes[n][1]) // width for n in names))


def unpack_local(packed, names, shard_shapes, width):
    out, r0 = {}, 0
    for n in names:
        a, b = shard_shapes[n]
        nr = (a * b) // width
        out[n] = packed[r0:r0 + nr].reshape(a, b)
        r0 += nr
    return out


EXACT_IN_GATHER = ("conv_w",)
EXACT_TERMS = 3


def pack_gather(shards, names, width):
    parts = []
    for n in names:
        if n in EXACT_IN_GATHER:
            rest = shards[n].astype(F32)
            for _ in range(EXACT_TERMS):
                term = rest.astype(BF16)
                parts.append(term.reshape(-1, width))
                rest = rest - term.astype(F32)
        else:
            parts.append(shards[n].reshape(-1, width).astype(BF16))
    used = sum(p.shape[0] for p in parts)
    parts.append(jnp.zeros((_pad_rows(used) - used, width), BF16))
    return jnp.concatenate(parts, axis=0)


def unpack_full(gathered, names, shard_shapes, width):
    out, r0 = {}, 0
    for n in names:
        a, b = shard_shapes[n]
        terms = EXACT_TERMS if n in EXACT_IN_GATHER else 1
        nr = (a * b) // width
        pieces = []
        for j in range(N_CHIPS):
            blk = gathered[j, r0:r0 + nr].reshape(a, b)
            for t in range(1, terms):
                blk = blk.astype(F32) + gathered[j, r0 + t * nr:r0 + (t + 1) * nr].reshape(a, b).astype(F32)
            pieces.append(blk)
        out[n] = jnp.concatenate(pieces, axis=SHARD_AXIS[n])
        r0 += terms * nr
    return out


def pack_full(grads, names, shard_shapes, width, pad=True):
    total_rows = _pack_rows(names, shard_shapes, width) if pad else sum(
        (shard_shapes[n][0] * shard_shapes[n][1]) // width for n in names)
    slabs = []
    for j in range(N_CHIPS):
        parts = []
        for n in names:
            a, b = shard_shapes[n]
            ax = SHARD_AXIS[n]
            sz = (a, b)[ax]
            piece = lax.slice_in_dim(grads[n], j * sz, (j + 1) * sz, axis=ax)
            parts.append(piece.reshape(-1, width))
        used = sum(p.shape[0] for p in parts)
        parts.append(jnp.zeros((total_rows - used, width), F32))
        slabs.append(jnp.concatenate(parts, axis=0))
    return jnp.stack(slabs, axis=0)


def _small_layout(sizes, width):
    lay, r = {}, 0
    for n in SMALL_WEIGHTS:
        nr = -(-sizes[n] // width)
        lay[n] = (r, nr, sizes[n])
        r += nr
    assert r <= SMALL_ROWS
    return lay


def pack_small(vals, lay, width):
    rows = []
    for n in SMALL_WEIGHTS:
        r, nr, sz = lay[n]
        v = vals[n].reshape(-1).astype(F32)
        rows.append(jnp.pad(v, (0, nr * width - sz)).reshape(nr, width))
    used = sum(r.shape[0] for r in rows)
    rows.append(jnp.zeros((SMALL_ROWS - used, width), F32))
    return jnp.concatenate(rows, axis=0)


def unpack_small(packed, lay):
    return {n: packed[r:r + nr].reshape(-1)[:sz].reshape(1, sz) for n, (r, nr, sz) in lay.items()}


PART_B = ("w_in", "w_gate")
PART_A = ("w_ff1", "w_ff2", "w_ssd_branch", "w_sb_branch", "w_out", "w_ple_gate", "w_ple", "conv_w")
DIRECT_A = PART_A[:6]


class Dist(NamedTuple):
    packed_a: object
    shard_shapes: dict
    core_idx: object
    chip_idx: object


def build_w_all(w_in, w_gate, dm):
    c0 = dm.DI + dm.CD
    return jnp.concatenate(
        [w_in[:, :c0], w_in[:, c0 + dm.H:], w_gate, w_in[:, c0:c0 + dm.H],
         jnp.zeros((dm.D, DT_PAD - dm.H), w_in.dtype)], axis=1).astype(BF16)


def pack_b(w_in_shard, w_gate_shard):
    return jnp.concatenate([w_in_shard, w_gate_shard], axis=1)


def unpack_b(slabs, dm):
    n_in = (dm.NA - DT_PAD - 2 * dm.D + dm.H) // N_CHIPS
    return (jnp.concatenate([slabs[j, :, :n_in] for j in range(N_CHIPS)], axis=1),
            jnp.concatenate([slabs[j, :, n_in:] for j in range(N_CHIPS)], axis=1))


def slabs_b(dw_all, dm):
    c0 = dm.DI + dm.CD
    off = dm.off
    dw_in = jnp.concatenate([dw_all[:, :c0], dw_all[:, off["dt"]:off["dt"] + dm.H], dw_all[:, c0:c0 + 3 * dm.SBW]], axis=1)
    dw_gate = dw_all[:, off["gate"]:off["gate"] + 2 * dm.D]
    n_in, n_gate = dw_in.shape[1] // N_CHIPS, dw_gate.shape[1] // N_CHIPS
    return jnp.stack([pack_b(dw_in[:, j * n_in:(j + 1) * n_in], dw_gate[:, j * n_gate:(j + 1) * n_gate])
                      for j in range(N_CHIPS)], axis=0)


def local_step(x, p, tgt, wf, sm, dm, dist=None):
    s, d = dm.S, dm.D
    off = dm.off
    c0 = dm.DI + dm.CD
    w_all = wf["w_all"] if "w_all" in wf else build_w_all(wf["w_in"], wf["w_gate"], dm)
    g = dm.G
    per_group = lambda v: v.reshape(g, 1, GROUP_HEADS)
    alog, dsk = per_group(sm["a_log"]), per_group(sm["d_skip"])
    b_gate = sm["b_gate"]
    b_ssd, b_sb = b_gate[:, :d], b_gate[:, d:]
    gcol = off["gate"] // d

    (n1,) = row_fwd("norm1", f_norm1, [x], [sm["norm_mix_pre"]], [(d, BF16)])
    proj = matmul(n1, w_all, name="in_proj")
    if dist is None:
        y_sb = attn_fwd(proj, dm, "attn_fwd")
    else:
        y_sb, gathered_a = attn_fwd(proj, dm, "attn_fwd", rider=gather_rider(dist.packed_a))
        gathered_a = own_slab(gathered_a, dist.packed_a, dist.chip_idx)
        wf = {**wf, **unpack_full(gathered_a, PART_A, dist.shard_shapes, d)}
    xbc = conv_fwd(proj, off["xbc"], dm.CD, wf["conv_w"].astype(F32), sm["conv_b"], "conv_fwd")
    dt_raw = proj[:, off["dt"]:off["dt"] + dm.H]
    (dt,) = row_fwd("dt", f_dt, [dt_raw], [sm["dt_bias"]], [(dm.H, F32)])
    dtc = dt.reshape(s, g, GROUP_HEADS).transpose(1, 0, 2)
    dtw = dt.reshape(s, g, GROUP_HEADS).transpose(1, 2, 0)
    y_ssd, states = ssd_fwd(proj, xbc, dtc, dtw, alog, dsk, sm["ssd_norm"], dm, "ssd_fwd")
    yb_ssd = matmul(y_ssd, wf["w_ssd_branch"], name="ssd_branch")
    yb_sb = matmul(y_sb, wf["w_sb_branch"], name="sb_branch")
    merge_rows = [(proj, d, gcol), (proj, d, gcol + 1), yb_ssd, yb_sb]
    (merged,) = row_fwd("merge", f_merge, merge_rows, [b_ssd, b_sb], [(d, BF16)])
    mo = matmul(merged, wf["w_out"], name="w_out")
    h1, n2 = row_fwd("mix_out", f_mix_out, [x, mo], [sm["norm_mix_post"], sm["norm_ffn_pre"]], [(d, F32), (d, BF16)])
    a1 = matmul(n2, wf["w_ff1"], name="ff1")
    (act,) = row_fwd("relu2", f_relu2, [a1], [], [(dm.DFF, BF16)])
    ff = matmul(act, wf["w_ff2"], name="ff2")
    (h2,) = row_fwd("ffn_out", f_ffn_out, [h1, ff], [sm["norm_ffn_post"]], [(d, F32)])
    pg = matmul(h2, wf["w_ple_gate"], name="ple_gate")
    pe = matmul(p, wf["w_ple"], name="ple_emb")

    def reduce_start(gbig, tag):
        from_sibling = swap_halves(gbig, "swap_halves_" + tag)
        pair_sum, pair_sum_bf16 = add_half(gbig, from_sibling, dist.core_idx, "add_half_" + tag)
        return pair_sum, exchange_rider(pair_sum_bf16)

    def reduce_finish(pair_sum, from_chips, tag):
        my_half = add_chips(pair_sum, from_chips, dist.chip_idx, "add_chips_" + tag)
        return join_halves(my_half, dist.core_idx, "join_halves_" + tag)

    gr, reduced = {}, None
    packed_grads = [None if dist is None else jnp.zeros((N_CHIPS, _pack_rows(PART_A, dist.shard_shapes, d), d), F32)]

    def weight_grad(wname, a, dy):
        if dist is None:
            gr[wname] = matmul(a, dy, ta=True, name="d_" + wname)
            return
        row0 = 0
        for n in PART_A[:PART_A.index(wname)]:
            row0 += (dist.shard_shapes[n][0] * dist.shard_shapes[n][1]) // d
        packed_grads[0] = matmul(a, dy, ta=True, name="d_" + wname,
                                 slab=SlabOut(packed_grads[0], row0, dist.shard_shapes[wname][0], SHARD_AXIS[wname]))

    (dh2_a, dpg, dpe), (gr["norm_ple_post"], loss_cols) = row_bwd(
        "ple_loss", f_ple_loss, [h2, pg, pe, tgt], [sm["norm_ple_post"]], [None], [F32, BF16, BF16, None], primal_sum=True)
    loss = jnp.sum(loss_cols)
    gr["w_ple"] = matmul(p, dpe, ta=True, name="d_w_ple")
    weight_grad("w_ple_gate", h2, dpg)
    dh2_b = matmul(dpg, wf["w_ple_gate"], tb=True, name="d_h2")
    (dh1_a, dff), (gr["norm_ffn_post"],) = row_bwd(
        "ffn_out_bwd", f_ffn_out, [h1, ff], [sm["norm_ffn_post"]], [[dh2_a, dh2_b]], [F32, BF16])
    weight_grad("w_ff2", act, dff)
    dact = matmul(dff, wf["w_ff2"], tb=True, name="d_act")
    (da1,), _ = row_bwd("relu2_bwd", f_relu2, [a1], [], [[dact]], [BF16])
    weight_grad("w_ff1", n2, da1)
    dn2 = matmul(da1, wf["w_ff1"], tb=True, name="d_n2")
    (dx_a, dmo), (gr["norm_mix_post"], gr["norm_ffn_pre"]) = row_bwd(
        "mix_out_bwd", f_mix_out, [x, mo], [sm["norm_mix_post"], sm["norm_ffn_pre"]], [[dh1_a], [dn2]], [F32, BF16])
    weight_grad("w_out", merged, dmo)
    dmerged = matmul(dmo, wf["w_out"], tb=True, name="d_merged")
    (dgp_ssd, dgp_sb, dyb_ssd, dyb_sb), (db_ssd, db_sb) = row_bwd(
        "merge_bwd", f_merge, merge_rows, [b_ssd, b_sb], [[dmerged]], [BF16, BF16, BF16, BF16])
    gr["b_gate"] = jnp.concatenate([db_ssd, db_sb], axis=1)
    weight_grad("w_ssd_branch", y_ssd, dyb_ssd)
    weight_grad("w_sb_branch", y_sb, dyb_sb)
    dy_ssd = matmul(dyb_ssd, wf["w_ssd_branch"], tb=True, name="d_y_ssd")
    dy_sb = matmul(dyb_sb, wf["w_sb_branch"], tb=True, name="d_y_sb")
    dproj, dxs, dbm, dcm, ddtc, ddtw, dalog, ddsk, gr["ssd_norm"] = ssd_bwd(
        proj, xbc, dtc, dtw, alog, dsk, sm["ssd_norm"], states, dy_ssd, dm, "ssd_bwd")
    gr["a_log"], gr["d_skip"] = (v.reshape(1, dm.H) for v in (dalog, ddsk))
    ddt_post = (ddtc.transpose(1, 0, 2) + ddtw.transpose(2, 0, 1)).reshape(s, dm.H)
    (ddt,), (gr["dt_bias"],) = row_bwd("dt_bwd", f_dt, [dt_raw], [sm["dt_bias"]], [[ddt_post]], [BF16])
    conv_w32 = wf["conv_w"].astype(F32)
    dproj, dw_x, dcb_x = conv_bwd(proj, off["xbc"], dxs, 0, conv_w32, sm["conv_b"], dproj, "conv_bwd_x")
    dproj, dw_b, dcb_b = conv_bwd(proj, off["xbc"], dbm, dm.DI, conv_w32, sm["conv_b"], dproj, "conv_bwd_b")
    dproj, dw_c, dcb_c = conv_bwd(proj, off["xbc"], dcm, dm.DI + g * D_STATE, conv_w32, sm["conv_b"], dproj, "conv_bwd_c")
    gr["conv_w"] = jnp.concatenate([dw_x, dw_b, dw_c], axis=1)
    gr["conv_b"] = jnp.concatenate([dcb_x, dcb_b, dcb_c], axis=1)
    if dist is None:
        dq, dk, dv = attn_bwd(proj, dy_sb, dm, "attn_bwd")
    else:
        rest = tuple(n for n in PART_A if n not in DIRECT_A)
        row0 = sum((dist.shard_shapes[n][0] * dist.shard_shapes[n][1]) // d for n in DIRECT_A)
        small_rows = pack_full(gr, rest, dist.shard_shapes, d, pad=False)
        gbig_a = lax.dynamic_update_slice(packed_grads[0], small_rows, (0, row0, 0))
        pair_sum_a, rider_a = reduce_start(gbig_a, "a")
        dq, dk, dv, from_chips_a = attn_bwd(proj, dy_sb, dm, "attn_bwd", rider=rider_a)
        reduced = unpack_local(reduce_finish(pair_sum_a, from_chips_a, "a"), PART_A, dist.shard_shapes, d)
    tail = jnp.concatenate([dq, dk, dv, dgp_ssd, dgp_sb, ddt, jnp.zeros((s, DT_PAD - dm.H), BF16)], axis=1)
    dproj = lax.dynamic_update_slice(dproj, tail, (0, off["q"]))
    dw_all = matmul(n1, dproj, ta=True, name="d_w_all")
    if dist is None:
        gr["w_in"] = jnp.concatenate(
            [dw_all[:, :c0], dw_all[:, off["dt"]:off["dt"] + dm.H], dw_all[:, c0:c0 + 3 * dm.SBW]], axis=1)
        gr["w_gate"] = dw_all[:, off["gate"]:off["gate"] + 2 * d]
        dn1 = matmul(dproj, w_all, tb=True, name="d_n1")
    else:
        pair_sum_b, rider_b = reduce_start(slabs_b(dw_all, dm), "b")
        dn1, from_chips_b = matmul(dproj, w_all, tb=True, name="d_n1", rider=rider_b)
        reduced_b = reduce_finish(pair_sum_b, from_chips_b, "b")
        n_in = dist.shard_shapes["w_in"][1]
        reduced.update(w_in=reduced_b[:, :n_in], w_gate=reduced_b[:, n_in:])
    (dx_b,), (gr["norm_mix_pre"],) = row_bwd("norm1_bwd", f_norm1, [x], [sm["norm_mix_pre"]], [[dn1]], [F32])
    (grad_x,) = row_fwd("grad_x", lambda u, v: (u + v,), [dx_a, dx_b], [], [(d, F32)])
    return loss, grad_x, gr, reduced


def kernel(x, p, norm_mix_pre, w_in, conv_w, conv_b, dt_bias, a_log, d_skip, ssd_norm, w_ssd_branch, w_sb_branch, w_gate, b_gate, w_out, norm_mix_post, norm_ffn_pre, w_ff1, w_ff2, norm_ffn_post, w_ple, w_ple_gate, norm_ple_post, loss_target, m_norm_mix_pre, m_w_in, m_conv_w, m_conv_b, m_dt_bias, m_a_log, m_d_skip, m_ssd_norm, m_w_ssd_branch, m_w_sb_branch, m_w_gate, m_b_gate, m_w_out, m_norm_mix_post, m_norm_ffn_pre, m_w_ff1, m_w_ff2, m_norm_ffn_post, m_w_ple, m_w_ple_gate, m_norm_ple_post, v_norm_mix_pre, v_w_in, v_conv_w, v_conv_b, v_dt_bias, v_a_log, v_d_skip, v_ssd_norm, v_w_ssd_branch, v_w_sb_branch, v_w_gate, v_b_gate, v_w_out, v_norm_mix_post, v_norm_ffn_pre, v_w_ff1, v_w_ff2, v_norm_ffn_post, v_w_ple, v_w_ple_gate, v_norm_ple_post):
    loc = dict(locals())
    unbatch = lambda a: a[0] if a.ndim == 3 else a
    w = {n: unbatch(loc[n]) for n in ALL_WEIGHTS}
    m = {n: unbatch(loc["m_" + n]) for n in ALL_WEIGHTS}
    v = {n: unbatch(loc["v_" + n]) for n in ALL_WEIGHTS}
    xs, ps, tgt = x[0], p[0, 0], loss_target[0]
    s, d = xs.shape
    di = w["w_ssd_branch"].shape[0] * N_CHIPS
    cd = w["conv_b"].shape[1]
    dm = Dims(S=s, D=d, DI=di, H=w["dt_bias"].shape[1], G=(cd - di) // (2 * D_STATE), CD=cd,
              SBW=w["w_sb_branch"].shape[0] * N_CHIPS, DFF=w["w_ff2"].shape[0] * N_CHIPS, PLE=ps.shape[1])
    ix, iy, ic = lax.axis_index("x"), lax.axis_index("y"), lax.axis_index("c")
    chip_idx = jnp.reshape(2 * ix + iy, (1,)).astype(jnp.int32)
    core_idx = jnp.reshape(ic, (1,)).astype(jnp.int32)
    dev_idx = jnp.reshape(4 * ix + 2 * iy + ic, (1,)).astype(jnp.int32)

    shard_shapes = {n: w[n].shape for n in BIG_WEIGHTS}
    packed_b = pack_b(w["w_in"], w["w_gate"]).astype(BF16)
    wf = {"w_all": build_w_all(*unpack_b(own_slab(gather_weights(packed_b), packed_b, chip_idx), dm), dm)}
    sm = {n: w[n] for n in SMALL_WEIGHTS}
    dist = Dist(packed_a=pack_gather(w, PART_A, d), shard_shapes=shard_shapes, core_idx=core_idx, chip_idx=chip_idx)

    loss_part, grad_x, gr, grads = local_step(xs, ps, tgt, wf, sm, dm, dist)
    loss = lax.psum(loss_part, ("x", "y", "c"))

    lay = _small_layout({n: w[n].shape[1] for n in SMALL_WEIGHTS}, d)
    gs_red = sum_small(exchange_small(pack_small(gr, lay, d)), dev_idx, "sum_small")
    grads.update(unpack_small(gs_red, lay))
    delta, new_m, new_v = {}, {}, {}
    for n in BIG_WEIGHTS:
        delta[n], new_m[n], new_v[n] = adamw(w[n], grads[n], m[n], v[n], "adamw_" + n)
    d_sm, nm_sm, nv_sm = adamw(pack_small(w, lay, d), gs_red, pack_small(m, lay, d), pack_small(v, lay, d), "adamw_small")
    for out, packed in ((delta, d_sm), (new_m, nm_sm), (new_v, nv_sm)):
        out.update(unpack_small(packed, lay))

    def leaves(vals):
        return [vals[n][None] if n in BIG_WEIGHTS else vals[n] for n in ALL_WEIGHTS]

    return (loss, grad_x[None], *leaves(grads), *leaves(delta), *leaves(new_m), *leaves(new_v))
```

```python
import functools
import math
from typing import NamedTuple

import jax
import jax.numpy as jnp
from jax import lax
from jax.experimental import pallas as pl
from jax.experimental.pallas import tpu as pltpu

F32 = jnp.float32
BF16 = jnp.bfloat16
SDS = jax.ShapeDtypeStruct

HEAD_DIM = 64
GROUP_HEADS = 4
D_STATE = 128
CHUNK = 128
ATT_TILE = 128
ATT_GROUP = 4
ATT_DEAD = -160.0
LOG2E = 1.4426950408889634
CONV_K = 4
CONV_COLS = 128
RMS_EPS = 1e-6
LANES = 128
DT_PAD = 512
N_CHIPS = 4
N_DEV = 8
SMALL_ROWS = 16
VMEM_LIMIT = 48 * 1024 * 1024
MAX_TK = 3072

ADAM_LR = 0.001
ADAM_B1 = 0.9
ADAM_B2 = 0.999
ADAM_EPS = 1e-08
ADAM_WD = 0.01
ADAM_STEP = 10

MESH_ID = pl.DeviceIdType.MESH
HBM_SPEC = pl.BlockSpec(memory_space=pltpu.HBM)

BIG_WEIGHTS = ("w_in", "conv_w", "w_ssd_branch", "w_sb_branch", "w_gate", "w_out", "w_ff1", "w_ff2", "w_ple", "w_ple_gate")
SHARD_AXIS = {"w_in": 1, "conv_w": 1, "w_ssd_branch": 0, "w_sb_branch": 0, "w_gate": 1, "w_out": 0, "w_ff1": 1,
              "w_ff2": 0, "w_ple": 1, "w_ple_gate": 0}
SMALL_WEIGHTS = ("norm_mix_pre", "conv_b", "dt_bias", "a_log", "d_skip", "ssd_norm", "b_gate", "norm_mix_post",
                 "norm_ffn_pre", "norm_ffn_post", "norm_ple_post")
ALL_WEIGHTS = ("norm_mix_pre", "w_in", "conv_w", "conv_b", "dt_bias", "a_log", "d_skip", "ssd_norm", "w_ssd_branch",
               "w_sb_branch", "w_gate", "b_gate", "w_out", "norm_mix_post", "norm_ffn_pre", "w_ff1", "w_ff2",
               "norm_ffn_post", "w_ple", "w_ple_gate", "norm_ple_post")


class Dims(NamedTuple):
    S: int
    D: int
    DI: int
    H: int
    G: int
    CD: int
    SBW: int
    DFF: int
    PLE: int

    @property
    def NA(self):
        return self.DI + self.CD + 3 * self.SBW + 2 * self.D + DT_PAD

    @property
    def off(self):
        o = {}
        o["z"] = 0
        o["xbc"] = self.DI
        o["q"] = self.DI + self.CD
        o["k"] = o["q"] + self.SBW
        o["v"] = o["k"] + self.SBW
        o["gate"] = o["v"] + self.SBW
        o["dt"] = o["gate"] + 2 * self.D
        return o


def _cparams(sem):
    return pltpu.CompilerParams(dimension_semantics=sem, vmem_limit_bytes=VMEM_LIMIT)


def _pick(n, cands):
    for c in cands:
        if n % c == 0:
            return c
    raise ValueError(f"no tile for {n}")


def _grid_call(body, *, grid, in_specs, out_specs, out_shape, scratch, operands, name, rider=None):
    if rider is None:
        sem = ("parallel",) + ("arbitrary",) * (len(grid) - 1)
        return pl.pallas_call(body, grid=grid, in_specs=in_specs, out_specs=out_specs, out_shape=out_shape,
                              scratch_shapes=scratch, compiler_params=_cparams(sem), name=name)(*operands)
    return pl.pallas_call(
        _with_rider(body, rider, grid, len(in_specs), len(out_specs)), grid=grid,
        in_specs=list(in_specs) + [HBM_SPEC] * len(rider.operands),
        out_specs=list(out_specs) + [HBM_SPEC] * len(rider.out_shape),
        out_shape=list(out_shape) + list(rider.out_shape), scratch_shapes=list(scratch) + list(rider.scratch),
        compiler_params=_cparams(("arbitrary",) * len(grid)), name=name)(*operands, *rider.operands)


class SlabOut(NamedTuple):
    buf: object
    row0: int
    rows: int
    axis: int


def matmul(a, b, *, ta=False, tb=False, out_dtype=F32, name, rider=None, slab=None):
    m, k = (a.shape[1], a.shape[0]) if ta else a.shape
    n, kb = b.shape if tb else (b.shape[1], b.shape[0])
    assert k == kb, (a.shape, b.shape, ta, tb)
    tm = _pick(m, (1024, 512, 256, 128))
    tn = _pick(n, (512, 256, 128))
    if slab is not None:
        tm = _pick(math.gcd(slab.rows, slab.row0), (1024, 512, 256, 128, 64, 32, 16, 8))
        tn = _pick(slab.buf.shape[2], (512, 256, 128))
    tk = max(t for t in range(LANES, min(k, MAX_TK) + 1, LANES) if k % t == 0)
    nk = k // tk
    dims = (((0 if ta else 1,), (1 if tb else 0,)), ((), ()))

    def body(a_ref, b_ref, o_ref, acc_ref):
        part = lax.dot_general(a_ref[...].astype(BF16), b_ref[...].astype(BF16), dims, preferred_element_type=F32)
        if nk == 1:
            o_ref[...] = part.astype(o_ref.dtype)
        else:
            kk = pl.program_id(2)

            @pl.when(kk == 0)
            def _():
                acc_ref[...] = part

            @pl.when(kk > 0)
            def _():
                acc_ref[...] += part

            @pl.when(kk == nk - 1)
            def _():
                o_ref[...] = acc_ref[...].astype(o_ref.dtype)

    a_spec = pl.BlockSpec((tk, tm), lambda i, j, kk: (kk, i)) if ta else pl.BlockSpec((tm, tk), lambda i, j, kk: (i, kk))
    b_spec = pl.BlockSpec((tn, tk), lambda i, j, kk: (j, kk)) if tb else pl.BlockSpec((tk, tn), lambda i, j, kk: (kk, j))
    if slab is not None:
        width = slab.buf.shape[2]
        rb0, per_shard = slab.row0 // tm, slab.rows // tm
        if slab.axis == 0:
            where = lambda i, j, kk: (i // per_shard, rb0 + i % per_shard, j)
        else:
            where = lambda i, j, kk: (j // (width // tn), rb0 + i, j % (width // tn))
        return pl.pallas_call(
            lambda a_ref, b_ref, buf_ref, o_ref, acc_ref: body(a_ref, b_ref, o_ref, acc_ref),
            grid=(m // tm, n // tn, nk), in_specs=[a_spec, b_spec, pl.BlockSpec(memory_space=pl.ANY)],
            out_specs=pl.BlockSpec((None, tm, tn), where), out_shape=SDS(slab.buf.shape, slab.buf.dtype),
            scratch_shapes=[pltpu.VMEM((tm, tn), F32)], input_output_aliases={2: 0},
            compiler_params=_cparams(("parallel", "parallel", "arbitrary")), name=name)(a, b, slab.buf)
    res = _grid_call(body, grid=(m // tm, n // tn, nk), in_specs=[a_spec, b_spec],
                     out_specs=[pl.BlockSpec((tm, tn), lambda i, j, kk: (i, j))], out_shape=[SDS((m, n), out_dtype)],
                     scratch=[pltpu.VMEM((tm, tn), F32)], operands=(a, b), name=name, rider=rider)
    return res[0] if rider is None else res


def _row_spec(entry, tile):
    arr, width, cb = entry if isinstance(entry, tuple) else (entry, entry.shape[1], 0)
    return arr, pl.BlockSpec((tile, width), lambda i, cb=cb: (i, cb))


def _par_spec(p):
    return pl.BlockSpec(p.shape, lambda i: (0, 0))


def row_fwd(name, fn, rows, params, outs, tile=256):
    arrs, specs = zip(*[_row_spec(e, tile) for e in rows])
    s = arrs[0].shape[0]
    nr, npar = len(rows), len(params)

    def body(*refs):
        r = [x[...].astype(F32) for x in refs[:nr]]
        p = [x[...] for x in refs[nr:nr + npar]]
        res = fn(*r, *p)
        for o_ref, val in zip(refs[nr + npar:], res):
            o_ref[...] = val.astype(o_ref.dtype)

    return pl.pallas_call(
        body, grid=(s // tile,), in_specs=list(specs) + [_par_spec(p) for p in params],
        out_specs=[pl.BlockSpec((tile, w), lambda i: (i, 0)) for w, _ in outs],
        out_shape=[SDS((s, w), dt) for w, dt in outs],
        compiler_params=_cparams(("parallel",)), name=name)(*arrs, *params)


def row_bwd(name, fn, rows, params, cots, row_grads, tile=256, primal_sum=False):
    arrs, specs = zip(*[_row_spec(e, tile) for e in rows])
    s = arrs[0].shape[0]
    nr, npar = len(rows), len(params)
    cot_entries = [e for c in cots if c is not None for e in c]
    carrs, cspecs = zip(*[_row_spec(e, tile) for e in cot_entries]) if cot_entries else ((), ())
    nc = len(cot_entries)
    want = [i for i, d in enumerate(row_grads) if d is not None]

    def body(*refs):
        r = [x[...].astype(F32) for x in refs[:nr]]
        p = [x[...] for x in refs[nr:nr + npar]]
        cvals = [x[...].astype(F32) for x in refs[nr + npar:nr + npar + nc]]
        outs = refs[nr + npar + nc:]
        prim, vjp = jax.vjp(fn, *r, *p)
        ct, pos = [], 0
        for c, pr in zip(cots, prim):
            if c is None:
                ct.append(jnp.ones_like(pr))
            else:
                acc = cvals[pos]
                for extra in cvals[pos + 1:pos + len(c)]:
                    acc = acc + extra
                pos += len(c)
                ct.append(acc)
        grads = vjp(tuple(ct))
        for o_ref, i in zip(outs[:len(want)], want):
            o_ref[...] = grads[i].astype(o_ref.dtype)
        acc_refs = outs[len(want):]
        vals = [grads[nr + j] for j in range(npar)]
        if primal_sum:
            vals.append(jnp.sum(prim[0], axis=0, keepdims=True))
        first = pl.program_id(0) == 0

        @pl.when(first)
        def _():
            for a_ref, v in zip(acc_refs, vals):
                a_ref[...] = v

        @pl.when(jnp.logical_not(first))
        def _():
            for a_ref, v in zip(acc_refs, vals):
                a_ref[...] += v

    widths = [(e[1] if isinstance(e, tuple) else e.shape[1]) for e in rows]
    out_specs = [pl.BlockSpec((tile, widths[i]), lambda i_: (i_, 0)) for i in want]
    out_shape = [SDS((s, widths[i]), row_grads[i]) for i in want]
    pshapes = [p.shape for p in params]
    if primal_sum:
        pshapes.append((1, widths[0]))
    out_specs += [pl.BlockSpec(sh, lambda i_: (0, 0)) for sh in pshapes]
    out_shape += [SDS(sh, F32) for sh in pshapes]
    res = pl.pallas_call(
        body, grid=(s // tile,), in_specs=list(specs) + [_par_spec(p) for p in params] + list(cspecs),
        out_specs=out_specs, out_shape=out_shape,
        compiler_params=_cparams(("arbitrary",)), name=name)(*arrs, *params, *carrs)
    return res[:len(want)], res[len(want):]


def _rms(x, w):
    return x * lax.rsqrt(jnp.mean(x * x, axis=-1, keepdims=True) + RMS_EPS) * w


def _sigmoid(x):
    return jax.nn.sigmoid(x)


def _softplus(x):
    return jnp.maximum(x, 0.0) + jnp.log1p(jnp.exp(-jnp.abs(x)))


def f_norm1(x, w):
    return (_rms(x, w),)


def f_dt(raw, bias):
    return (_softplus(raw + bias),)


def f_merge(gp_ssd, gp_sb, yb_ssd, yb_sb, b_ssd, b_sb):
    return (_sigmoid(gp_ssd + b_ssd) * yb_ssd + _sigmoid(gp_sb + b_sb) * yb_sb,)


def f_mix_out(x, mo, w_post, w_pre):
    h1 = x + _rms(mo, w_post)
    return h1, _rms(h1, w_pre)


def f_relu2(a1):
    return (jnp.square(jnp.maximum(a1, 0.0)),)


def f_ffn_out(h1, ff, w):
    return (h1 + _rms(ff, w),)


def f_ple_loss(h2, pg, pe, tgt, w):
    h3 = h2 + _rms(_sigmoid(pg) * pe, w)
    return (0.5 * jnp.square(h3 - tgt) * (1.0 / h2.shape[-1]),)


def _shift_down(u, d, rows):
    return u if d == 0 else jnp.where(rows >= d, pltpu.roll(u, d, 0), 0.0)


def _shift_up(u, d, rows):
    s = u.shape[0]
    return u if d == 0 else jnp.where(rows < s - d, pltpu.roll(u, s - d, 0), 0.0)


def conv_fwd(proj, col0, cd, conv_w, conv_b, name):
    s = proj.shape[0]
    cb0 = col0 // CONV_COLS

    def body(u_ref, w_ref, b_ref, o_ref):
        u = u_ref[...]
        rows = lax.broadcasted_iota(jnp.int32, u.shape, 0)
        y = jnp.broadcast_to(b_ref[...], u.shape)
        for k in range(CONV_K):
            y = y + w_ref[k:k + 1, :] * _shift_down(u, CONV_K - 1 - k, rows)
        o_ref[...] = y * _sigmoid(y)

    return pl.pallas_call(
        body, grid=(cd // CONV_COLS,),
        in_specs=[pl.BlockSpec((s, CONV_COLS), lambda i: (0, cb0 + i)),
                  pl.BlockSpec((CONV_K, CONV_COLS), lambda i: (0, i)),
                  pl.BlockSpec((1, CONV_COLS), lambda i: (0, i))],
        out_specs=pl.BlockSpec((s, CONV_COLS), lambda i: (0, i)),
        out_shape=SDS((s, cd), F32), compiler_params=_cparams(("parallel",)), name=name)(proj, conv_w, conv_b)


def conv_bwd(proj, col0, dout, ch0, conv_w, conv_b, dproj, name):
    s = proj.shape[0]
    ncb = dout.shape[1] // CONV_COLS
    cb0 = (col0 + ch0) // CONV_COLS
    wb0 = ch0 // CONV_COLS

    def body(u_ref, g_ref, w_ref, b_ref, _, du_ref, dw_ref, db_ref):
        u = u_ref[...]
        rows = lax.broadcasted_iota(jnp.int32, u.shape, 0)
        y = jnp.broadcast_to(b_ref[...], u.shape)
        for k in range(CONV_K):
            y = y + w_ref[k:k + 1, :] * _shift_down(u, CONV_K - 1 - k, rows)
        sg = _sigmoid(y)
        dy = g_ref[...] * (sg * (1.0 + y * (1.0 - sg)))
        du = jnp.zeros_like(u)
        for k in range(CONV_K):
            d = CONV_K - 1 - k
            du = du + w_ref[k:k + 1, :] * _shift_up(dy, d, rows)
            dw_ref[k:k + 1, :] = jnp.sum(dy * _shift_down(u, d, rows), axis=0, keepdims=True)
        du_ref[...] = du.astype(du_ref.dtype)
        db_ref[...] = jnp.sum(dy, axis=0, keepdims=True)

    return pl.pallas_call(
        body, grid=(ncb,),
        in_specs=[pl.BlockSpec((s, CONV_COLS), lambda i: (0, cb0 + i)),
                  pl.BlockSpec((s, CONV_COLS), lambda i: (0, i)),
                  pl.BlockSpec((CONV_K, CONV_COLS), lambda i: (0, wb0 + i)),
                  pl.BlockSpec((1, CONV_COLS), lambda i: (0, wb0 + i)),
                  pl.BlockSpec(memory_space=pl.ANY)],
        out_specs=[pl.BlockSpec((s, CONV_COLS), lambda i: (0, cb0 + i)),
                   pl.BlockSpec((CONV_K, CONV_COLS), lambda i: (0, i)),
                   pl.BlockSpec((1, CONV_COLS), lambda i: (0, i))],
        out_shape=[SDS(dproj.shape, dproj.dtype), SDS((CONV_K, ncb * CONV_COLS), F32), SDS((1, ncb * CONV_COLS), F32)],
        input_output_aliases={4: 0},
        compiler_params=_cparams(("parallel",)), name=name)(proj, dout, conv_w, conv_b, dproj)


def _dot(a, b, dims):
    return lax.dot_general(a.astype(BF16), b.astype(BF16), (dims, ((), ())), preferred_element_type=F32)


NN = ((1,), (0,))
NT = ((1,), (1,))
TN = ((0,), (0,))


def ssd_chunk(xs, zs, nw, dtc, dtw, alogs, dsks, bm, cm, prev):
    ln = bm.shape[0]
    gw = GROUP_HEADS * HEAD_DIM
    row = lax.broadcasted_iota(jnp.int32, (ln, ln), 0)
    col = lax.broadcasted_iota(jnp.int32, (ln, ln), 1)
    causal = row >= col
    tri = causal.astype(F32)
    tri_t = (row <= col).astype(F32)
    lane_head = lax.broadcasted_iota(jnp.int32, (1, gw), 1) // HEAD_DIM
    sub_head = lax.broadcasted_iota(jnp.int32, (gw, 1), 0) // HEAD_DIM
    on_lanes = [(lane_head == r).astype(F32) for r in range(GROUP_HEADS)]
    on_rows = [(sub_head == r).astype(F32) for r in range(GROUP_HEADS)]
    cb = _dot(cm, bm, NT)
    decays, dt_full, acs_full, end_full, dsk_full, end_rows = [], 0.0, 0.0, 0.0, 0.0, 0.0
    for r in range(GROUP_HEADS):
        a = -jnp.exp(alogs[r])
        da_c = dtc[r] * a
        da_w = dtw[r] * a
        acs_c = jnp.sum(tri * da_w, axis=1, keepdims=True)
        acs_w = jnp.sum(tri_t * da_c, axis=0, keepdims=True)
        alast = jnp.sum(da_w, axis=1, keepdims=True)
        decays.append(jnp.exp(jnp.where(causal, acs_c - acs_w, -jnp.inf)))
        dt_full = dt_full + dtc[r] * on_lanes[r]
        acs_full = acs_full + acs_c * on_lanes[r]
        end_full = end_full + alast * on_lanes[r]
        dsk_full = dsk_full + dsks[r] * on_lanes[r]
        end_rows = end_rows + alast * on_rows[r]
    xd = xs * dt_full
    y = xs * dsk_full + _dot(cm, prev, NT) * jnp.exp(acs_full)
    for r in range(GROUP_HEADS):
        y = y + _dot(cb * decays[r], xd * on_lanes[r], NN)
    new_prev = prev * jnp.exp(end_rows) + _dot(xd * jnp.exp(end_full - acs_full), bm, TN)
    yg = y * (zs * _sigmoid(zs))
    rstd = lax.rsqrt(jnp.mean(yg * yg, axis=-1, keepdims=True) + RMS_EPS)
    return yg * rstd * nw, new_prev


SSD_STEP_GROUPS = 1


def _ssd_specs(dm, cidx):
    u = SSD_STEP_GROUPS
    gw = GROUP_HEADS * HEAD_DIM
    nb0 = dm.DI // D_STATE
    assert nb0 % u == 0 and dm.G % u == 0
    par = pl.BlockSpec((u, 1, GROUP_HEADS), lambda g, c: (g, 0, 0))
    return dict(
        z=pl.BlockSpec((CHUNK, u * gw), lambda g, c: (cidx(c), g)),
        xs=pl.BlockSpec((CHUNK, u * gw), lambda g, c: (cidx(c), g)),
        b=pl.BlockSpec((CHUNK, u * D_STATE), lambda g, c: (cidx(c), nb0 // u + g)),
        c=pl.BlockSpec((CHUNK, u * D_STATE), lambda g, c: (cidx(c), (nb0 + dm.G) // u + g)),
        dtc=pl.BlockSpec((u, CHUNK, GROUP_HEADS), lambda g, c: (g, cidx(c), 0)),
        dtw=pl.BlockSpec((u, GROUP_HEADS, CHUNK), lambda g, c: (g, 0, cidx(c))),
        par=par,
        nw=pl.BlockSpec((1, u * gw), lambda g, c: (0, g)),
        st=pl.BlockSpec((u, None, gw, D_STATE), lambda g, c: (g, cidx(c), 0, 0)),
    )


def _ssd_load(k, z_ref, xs_ref, b_ref, c_ref, dtc_ref, dtw_ref, alog_ref, dsk_ref, nw_ref):
    gw = GROUP_HEADS * HEAD_DIM
    wide, narrow = slice(k * gw, (k + 1) * gw), slice(k * D_STATE, (k + 1) * D_STATE)
    dtc = tuple(dtc_ref[k, :, r:r + 1] for r in range(GROUP_HEADS))
    dtw = tuple(dtw_ref[k, r:r + 1, :] for r in range(GROUP_HEADS))
    alogs = tuple(alog_ref[k, :, r:r + 1] for r in range(GROUP_HEADS))
    dsks = tuple(dsk_ref[k, :, r:r + 1] for r in range(GROUP_HEADS))
    return xs_ref[:, wide], z_ref[:, wide], nw_ref[:, wide], dtc, dtw, alogs, dsks, b_ref[:, narrow], c_ref[:, narrow]


def ssd_fwd(proj, xbc, dtc, dtw, alog, dsk, nw, dm, name):
    nc = dm.S // CHUNK
    u = SSD_STEP_GROUPS
    gw = GROUP_HEADS * HEAD_DIM
    sp = _ssd_specs(dm, lambda c: c)

    def body(z_ref, xs_ref, b_ref, c_ref, dtc_ref, dtw_ref, alog_ref, dsk_ref, nw_ref, y_ref, st_ref, prev):
        @pl.when(pl.program_id(1) == 0)
        def _():
            prev[...] = jnp.zeros_like(prev)

        st_ref[...] = prev[...]
        for k in range(u):
            args = _ssd_load(k, z_ref, xs_ref, b_ref, c_ref, dtc_ref, dtw_ref, alog_ref, dsk_ref, nw_ref)
            out, new = ssd_chunk(*args, prev[k])
            y_ref[:, k * gw:(k + 1) * gw] = out.astype(y_ref.dtype)
            prev[k] = new

    return pl.pallas_call(
        body, grid=(dm.G // u, nc),
        in_specs=[sp["z"], sp["xs"], sp["b"], sp["c"], sp["dtc"], sp["dtw"], sp["par"], sp["par"], sp["nw"]],
        out_specs=[sp["xs"], sp["st"]],
        out_shape=[SDS((dm.S, dm.DI), BF16), SDS((dm.G, nc, gw, D_STATE), F32)],
        scratch_shapes=[pltpu.VMEM((u, gw, D_STATE), F32)],
        compiler_params=_cparams(("parallel", "arbitrary")), name=name)(proj, xbc, xbc, xbc, dtc, dtw, alog, dsk, nw)


def ssd_bwd(proj, xbc, dtc, dtw, alog, dsk, nw, states, dy, dm, name):
    nc = dm.S // CHUNK
    u = SSD_STEP_GROUPS
    sp = _ssd_specs(dm, lambda c: nc - 1 - c)
    gw = GROUP_HEADS * HEAD_DIM
    bc_spec = pl.BlockSpec((CHUNK, u * D_STATE), lambda g, c: (nc - 1 - c, g))

    def body(z_ref, xs_ref, b_ref, c_ref, dtc_ref, dtw_ref, alog_ref, dsk_ref, nw_ref, st_ref, dy_ref,
             dz_ref, dxs_ref, db_ref, dc_ref, ddtc_ref, ddtw_ref, dalog_ref, ddsk_ref, dnw_ref, dprev):
        first = pl.program_id(1) == 0

        @pl.when(first)
        def _():
            dprev[...] = jnp.zeros_like(dprev)

        param_grads = []
        for k in range(u):
            wide, narrow = slice(k * gw, (k + 1) * gw), slice(k * D_STATE, (k + 1) * D_STATE)
            args = _ssd_load(k, z_ref, xs_ref, b_ref, c_ref, dtc_ref, dtw_ref, alog_ref, dsk_ref, nw_ref)
            _, vjp = jax.vjp(ssd_chunk, *args, st_ref[k])
            gxs, gzs, gnw, gdtc, gdtw, galogs, gdsks, gb, gc, gprev = vjp((dy_ref[:, wide], dprev[k]))
            dxs_ref[:, wide] = gxs
            dz_ref[:, wide] = gzs.astype(dz_ref.dtype)
            db_ref[:, narrow] = gb
            dc_ref[:, narrow] = gc
            dprev[k] = gprev
            for r in range(GROUP_HEADS):
                ddtc_ref[k, :, r:r + 1] = gdtc[r]
                ddtw_ref[k, r:r + 1, :] = gdtw[r]
            param_grads.append((wide, gnw, galogs, gdsks))

        @pl.when(first)
        def _():
            for k, (wide, gnw, galogs, gdsks) in enumerate(param_grads):
                dnw_ref[:, wide] = gnw
                for r in range(GROUP_HEADS):
                    dalog_ref[k, :, r:r + 1] = galogs[r]
                    ddsk_ref[k, :, r:r + 1] = gdsks[r]

        @pl.when(jnp.logical_not(first))
        def _():
            for k, (wide, gnw, galogs, gdsks) in enumerate(param_grads):
                dnw_ref[:, wide] += gnw
                for r in range(GROUP_HEADS):
                    dalog_ref[k, :, r:r + 1] += galogs[r]
                    ddsk_ref[k, :, r:r + 1] += gdsks[r]

    xs_out = pl.BlockSpec((CHUNK, u * gw), lambda g, c: (nc - 1 - c, g))
    return pl.pallas_call(
        body, grid=(dm.G // u, nc),
        in_specs=[sp["z"], sp["xs"], sp["b"], sp["c"], sp["dtc"], sp["dtw"], sp["par"], sp["par"], sp["nw"],
                  sp["st"], xs_out],
        out_specs=[xs_out, xs_out, bc_spec, bc_spec, sp["dtc"], sp["dtw"], sp["par"], sp["par"], sp["nw"]],
        out_shape=[SDS((dm.S, dm.NA), BF16), SDS((dm.S, dm.DI), F32), SDS((dm.S, dm.G * D_STATE), F32),
                   SDS((dm.S, dm.G * D_STATE), F32), SDS((dm.G, dm.S, GROUP_HEADS), F32), SDS((dm.G, GROUP_HEADS, dm.S), F32),
                   SDS((dm.G, 1, GROUP_HEADS), F32), SDS((dm.G, 1, GROUP_HEADS), F32), SDS((1, dm.DI), F32)],
        scratch_shapes=[pltpu.VMEM((u, gw, D_STATE), F32)],
        compiler_params=_cparams(("parallel", "arbitrary")), name=name)(
            proj, xbc, xbc, xbc, dtc, dtw, alog, dsk, nw, states, dy)


def _split_bf16(v):
    hi = v.astype(BF16)
    return hi, (v - hi.astype(F32)).astype(BF16)


def _tri(v, mat):
    hi, lo = _split_bf16(v)
    return jnp.dot(hi, mat, preferred_element_type=F32) + jnp.dot(lo, mat, preferred_element_type=F32)


def _blocks(v):
    return [v[:, b * ATT_TILE:(b + 1) * ATT_TILE] for b in range(v.shape[1] // ATT_TILE)]


def _sb_group(z, mask, run, after_mat):
    sp = jnp.maximum(z, 0.0) + jnp.log2(1.0 + jnp.exp2(-jnp.abs(z)))
    lk = -sp if mask is None else jnp.where(mask, -sp, 0.0)
    cums = [_tri(v, after_mat) for v in _blocks(lk)]
    sums = [jnp.sum(v, axis=1, keepdims=True) for v in _blocks(lk)]
    later = [None] * len(cums)
    for b in reversed(range(len(cums))):
        later[b] = run + cums[b]
        run = run + sums[b]
    ls = z - sp
    w = jnp.exp2(ls + jnp.concatenate(later, axis=1))
    if mask is not None:
        w = jnp.where(mask, w, 0.0)
    return ls, w, run


def _alive(run_a, run_b):
    return (jnp.max(jnp.maximum(run_a, run_b)) > ATT_DEAD).astype(jnp.int32)


def _window(i, jj, t):
    gw = ATT_GROUP * t
    end = (i + 1 - ATT_GROUP * jj) * t
    r0 = pl.multiple_of(jnp.maximum(end - gw, 0), t)
    rows = i * t + lax.broadcasted_iota(jnp.int32, (t, gw), 0)
    cols = r0 + lax.broadcasted_iota(jnp.int32, (t, gw), 1)
    return r0, jnp.logical_and(cols < rows, cols < end)


def _att_specs(dm, s):
    t = ATT_TILE
    qb, kb, vb = dm.off["q"] // LANES, dm.off["k"] // LANES, dm.off["v"] // LANES
    return (pl.BlockSpec((t, LANES), lambda p, i: (i, qb + p)),
            pl.BlockSpec((s, LANES), lambda p, i: (0, kb + p)),
            pl.BlockSpec((s, LANES), lambda p, i: (0, vb + p)))


def attn_fwd(proj, dm, name, rider=None):
    s, t = dm.S, ATT_TILE
    scale = HEAD_DIM ** -0.5
    hsl = [slice(h * HEAD_DIM, (h + 1) * HEAD_DIM) for h in range(2)]

    gw = ATT_GROUP * t

    def body(q_ref, k_ref, v_ref, o_ref):
        i = pl.program_id(1)
        gd = i // ATT_GROUP
        r_io = lax.broadcasted_iota(jnp.int32, (t, t), 0)
        c_io = lax.broadcasted_iota(jnp.int32, (t, t), 1)
        after_mat = (r_io > c_io).astype(BF16)
        qs = [q_ref[:, sl].astype(BF16) for sl in hsl]

        def group(jj, carry):
            r0, mask = _window(i, jj, t)
            zs = [_dot(qs[h], k_ref[pl.ds(r0, gw), hsl[h]], NT) * (scale * LOG2E) for h in range(2)]
            res = [_sb_group(zs[h], mask, carry[h][0], after_mat) for h in range(2)]
            return tuple((res[h][2], carry[h][1] + _dot(res[h][1], v_ref[pl.ds(r0, gw), hsl[h]], NN)) for h in range(2))

        zero = (jnp.zeros((t, 1), F32), jnp.zeros((t, HEAD_DIM), F32))
        carry = group(0, (zero, zero))

        def step(st):
            jj, _, c = st
            c = group(jj, c)
            return jj + 1, _alive(c[0][0], c[1][0]), c

        _, _, carry = lax.while_loop(lambda st: jnp.logical_and(st[0] <= gd, st[1] > 0), step,
                                     (jnp.int32(1), _alive(carry[0][0], carry[1][0]), carry))
        for h in range(2):
            o_ref[:, hsl[h]] = carry[h][1]

    qs_, ks_, vs_ = _att_specs(dm, s)
    res = _grid_call(body, grid=(dm.SBW // LANES, s // t), in_specs=[qs_, ks_, vs_],
                     out_specs=[pl.BlockSpec((t, LANES), lambda p, i: (i, p))], out_shape=[SDS((s, dm.SBW), F32)],
                     scratch=[], operands=(proj, proj, proj), name=name, rider=rider)
    return res[0] if rider is None else res


def attn_bwd(proj, do, dm, name, rider=None):
    s, t = dm.S, ATT_TILE
    nq = s // t
    gw = ATT_GROUP * t
    scale = HEAD_DIM ** -0.5
    hsl = [slice(h * HEAD_DIM, (h + 1) * HEAD_DIM) for h in range(2)]

    def body(q_ref, k_ref, v_ref, do_ref, dq_ref, dk_ref, dv_ref, dk_acc, dv_acc, g_scr, s_scr):
        i = pl.program_id(1)

        @pl.when(i == 0)
        def _():
            dk_acc[...] = jnp.zeros_like(dk_acc)
            dv_acc[...] = jnp.zeros_like(dv_acc)

        gd = i // ATT_GROUP
        r_io = lax.broadcasted_iota(jnp.int32, (t, t), 0)
        c_io = lax.broadcasted_iota(jnp.int32, (t, t), 1)
        after_mat = (r_io > c_io).astype(BF16)
        before_mat = (r_io < c_io).astype(BF16)
        qs = [q_ref[:, sl].astype(BF16) for sl in hsl]
        dos = [do_ref[:, sl].astype(BF16) for sl in hsl]
        q_t = q_ref[...].T.astype(BF16)
        do_t = do_ref[...].T.astype(BF16)

        def pass1(jj, runs):
            r0, mask = _window(i, jj, t)
            zs = [_dot(qs[h], k_ref[pl.ds(r0, gw), hsl[h]], NT) * (scale * LOG2E) for h in range(2)]
            dws = [_dot(dos[h], v_ref[pl.ds(r0, gw), hsl[h]], NT) for h in range(2)]
            out = []
            for h in range(2):
                ls, w, run = _sb_group(zs[h], mask, runs[h], after_mat)
                g_scr[h, jj] = dws[h] * w
                s_scr[h, jj] = jnp.exp2(ls)
                dv_acc[hsl[h], pl.ds(r0, gw)] += _dot(do_t[hsl[h]], w, NN)
                out.append(run)
            return tuple(out)

        zero_col = jnp.zeros((t, 1), F32)
        runs = pass1(0, (zero_col, zero_col))

        def step1(st):
            jj, _, r = st
            r = pass1(jj, r)
            return jj + 1, _alive(r[0], r[1]), r

        walked, _, _ = lax.while_loop(lambda st: jnp.logical_and(st[0] <= gd, st[1] > 0), step1,
                                      (jnp.int32(1), _alive(runs[0], runs[1]), runs))

        def pass2(jj, carry):
            r0, mask = _window(i, jj, t)
            out = []
            for h in range(2):
                pre, dq = carry[h]
                gg = g_scr[h, jj]
                sig = s_scr[h, jj]
                before = []
                for v in _blocks(gg):
                    before.append(pre + _tri(v, before_mat))
                    pre = pre + jnp.sum(v, axis=1, keepdims=True)
                dz = jnp.where(mask, gg * (1.0 - sig) - jnp.concatenate(before, axis=1) * sig, 0.0)
                dz = (dz * scale).astype(BF16)
                dk_acc[hsl[h], pl.ds(r0, gw)] += _dot(q_t[hsl[h]], dz, NN)
                out.append((pre, dq + _dot(dz, k_ref[pl.ds(r0, gw), hsl[h]], NN)))
            return tuple(out)

        zero = (zero_col, jnp.zeros((t, HEAD_DIM), F32))
        carry = lax.fori_loop(1, walked, lambda n, c: pass2(walked - n, c), (zero, zero))
        carry = pass2(0, carry)
        for h in range(2):
            dq_ref[:, hsl[h]] = carry[h][1].astype(dq_ref.dtype)

        @pl.when(i == nq - 1)
        def _():
            for g in range(s // gw):
                dk_ref[g * gw:(g + 1) * gw, :] = dk_acc[:, g * gw:(g + 1) * gw].T.astype(dk_ref.dtype)
                dv_ref[g * gw:(g + 1) * gw, :] = dv_acc[:, g * gw:(g + 1) * gw].T.astype(dv_ref.dtype)

    qs_, ks_, vs_ = _att_specs(dm, s)
    tile_spec = pl.BlockSpec((t, LANES), lambda p, i: (i, p))
    full_spec = pl.BlockSpec((s, LANES), lambda p, i: (0, p))
    return _grid_call(
        body, grid=(dm.SBW // LANES, nq), in_specs=[qs_, ks_, vs_, tile_spec],
        out_specs=[tile_spec, full_spec, full_spec], out_shape=[SDS((s, dm.SBW), BF16)] * 3,
        scratch=[pltpu.VMEM((LANES, s), F32), pltpu.VMEM((LANES, s), F32),
                 pltpu.VMEM((2, s // gw, t, gw), F32), pltpu.VMEM((2, s // gw, t, gw), F32)],
        operands=(proj, proj, proj, do), name=name, rider=rider)


def adamw(w, g, m, v, name):
    rows, width = w.shape
    tile = _pick(rows, (256, 64, 16, 8, 4))
    c1 = 1.0 / (1.0 - ADAM_B1 ** ADAM_STEP)
    c2 = 1.0 / (1.0 - ADAM_B2 ** ADAM_STEP)

    def body(w_ref, g_ref, m_ref, v_ref, d_ref, nm_ref, nv_ref):
        gg = g_ref[...]
        nm = ADAM_B1 * m_ref[...] + (1.0 - ADAM_B1) * gg
        nv = ADAM_B2 * v_ref[...] + (1.0 - ADAM_B2) * (gg * gg)
        d_ref[...] = -ADAM_LR * ((nm * c1) / (jnp.sqrt(nv * c2) + ADAM_EPS) + ADAM_WD * w_ref[...])
        nm_ref[...] = nm
        nv_ref[...] = nv

    spec = pl.BlockSpec((tile, width), lambda i: (i, 0))
    return pl.pallas_call(
        body, grid=(rows // tile,), in_specs=[spec] * 4, out_specs=[spec] * 3,
        out_shape=[SDS((rows, width), F32)] * 3, compiler_params=_cparams(("parallel",)), name=name)(w, g, m, v)


def _me():
    return lax.axis_index("x"), lax.axis_index("y"), lax.axis_index("c")


def _other_chips(x, y):
    return [(1 - x, y), (x, 1 - y), (1 - x, 1 - y)]


def gather_weights(wp):
    rows, width = wp.shape
    half = rows // 2

    def body(w_ref, out_ref, send_sems, recv_sems):
        x, y, c = _me()
        sibling = (x, y, 1 - c)
        chips = _other_chips(x, y)

        def part(cx, cy, hf):
            return out_ref.at[2 * cx + cy, hf]

        def copy(k, src, dst, to):
            return pltpu.make_async_remote_copy(src_ref=src, dst_ref=dst, send_sem=send_sems.at[k], recv_sem=recv_sems.at[k],
                                                device_id=to, device_id_type=MESH_ID)

        first = [copy(j, w_ref.at[c], part(x, y, c), (cx, cy, c)) for j, (cx, cy) in enumerate(chips)]
        for cp in first:
            cp.start()
        passed = [copy(3 + j, part(cx, cy, c), part(cx, cy, c), sibling) for j, (cx, cy) in enumerate(chips)]
        for j, (cx, cy) in enumerate(chips):
            copy(j, part(cx, cy, c), part(cx, cy, c), (x, y, c)).wait_recv()
            passed[j].start()
        for j, (cx, cy) in enumerate(chips):
            copy(3 + j, part(cx, cy, 1 - c), part(cx, cy, 1 - c), (x, y, c)).wait_recv()
        for cp in first + passed:
            cp.wait_send()

    return pl.pallas_call(
        body, out_shape=SDS((N_CHIPS, 2, half, width), wp.dtype), in_specs=[HBM_SPEC], out_specs=HBM_SPEC,
        scratch_shapes=[pltpu.SemaphoreType.DMA((6,)), pltpu.SemaphoreType.DMA((6,))],
        name="gather_weights")(wp.reshape(2, half, width)).reshape(N_CHIPS, rows, width)


class Rider(NamedTuple):
    operands: tuple
    out_shape: tuple
    scratch: tuple
    start: object
    wait: object


def _with_rider(body, rider, grid, n_in, n_out):
    if rider is None:
        return body
    r_in, r_out = len(rider.operands), len(rider.out_shape)

    def full(*refs):
        ins, refs = refs[:n_in], refs[n_in:]
        rins, refs = refs[:r_in], refs[r_in:]
        outs, refs = refs[:n_out], refs[n_out:]
        routs, refs = refs[:r_out], refs[r_out:]
        scr, rscr = refs[:len(refs) - len(rider.scratch)], refs[len(refs) - len(rider.scratch):]
        ids = [pl.program_id(k) for k in range(len(grid))]
        first = functools.reduce(jnp.logical_and, [i == 0 for i in ids])
        last = functools.reduce(jnp.logical_and, [i == g - 1 for i, g in zip(ids, grid)])

        @pl.when(first)
        def _():
            rider.start(rins, routs, rscr)

        body(*ins, *outs, *scr)

        @pl.when(last)
        def _():
            rider.wait(rins, routs, rscr)

    return full


def gather_rider(wp):
    def copies(ins, outs, scr, sending):
        (w_ref,), (o_ref,), (send_sems, recv_sems) = ins, outs, scr
        x, y, c = _me()
        return [pltpu.make_async_remote_copy(src_ref=w_ref, dst_ref=o_ref.at[2 * x + y if sending else 2 * cx + cy],
                                             send_sem=send_sems.at[j], recv_sem=recv_sems.at[j], device_id=(cx, cy, c),
                                             device_id_type=MESH_ID)
                for j, (cx, cy) in enumerate(_other_chips(x, y))]

    def start(ins, outs, scr):
        for cp in copies(ins, outs, scr, True):
            cp.start()

    def wait(ins, outs, scr):
        for cp in copies(ins, outs, scr, False):
            cp.wait()

    return Rider((wp,), (SDS((N_CHIPS,) + wp.shape, wp.dtype),),
                 (pltpu.SemaphoreType.DMA((3,)), pltpu.SemaphoreType.DMA((3,))), start, wait)


def own_slab(gathered, wp, chip_idx):
    return lax.dynamic_update_slice(gathered, wp[None], (chip_idx[0], 0, 0))


def exchange_rider(sh):
    def copies(ins, outs, scr):
        (s_ref,), (b_ref,), (send_sems, recv_sems) = ins, outs, scr
        x, y, c = _me()
        return [pltpu.make_async_remote_copy(src_ref=s_ref.at[2 * cx + cy], dst_ref=b_ref.at[j], send_sem=send_sems.at[j],
                                             recv_sem=recv_sems.at[j], device_id=(cx, cy, c), device_id_type=MESH_ID)
                for j, (cx, cy) in enumerate(_other_chips(x, y))]

    def start(ins, outs, scr):
        for cp in copies(ins, outs, scr):
            cp.start()

    def wait(ins, outs, scr):
        for cp in copies(ins, outs, scr):
            cp.wait()

    return Rider((sh,), (SDS((3,) + sh.shape[1:], sh.dtype),),
                 (pltpu.SemaphoreType.DMA((3,)), pltpu.SemaphoreType.DMA((3,))), start, wait)


def swap_halves(g, name):
    n, rows, width = g.shape
    half = rows // 2

    def body(g_ref, a_ref, send_sem, recv_sem):
        x, y, c = _me()
        cp = pltpu.make_async_remote_copy(src_ref=g_ref.at[:, 1 - c], dst_ref=a_ref,
                                          send_sem=send_sem, recv_sem=recv_sem, device_id=(x, y, 1 - c), device_id_type=MESH_ID)
        cp.start()
        cp.wait()

    return pl.pallas_call(
        body, out_shape=SDS((n, half, width), g.dtype), in_specs=[HBM_SPEC], out_specs=HBM_SPEC,
        scratch_shapes=[pltpu.SemaphoreType.DMA, pltpu.SemaphoreType.DMA], name=name)(g.reshape(n, 2, half, width))


def add_half(g, a, c_idx, name):
    n, rows, width = g.shape
    half = rows // 2
    tile = half // 2
    nt = half // tile

    def body(c_ref, g_ref, a_ref, o_ref, ob_ref):
        v = g_ref[...] + a_ref[...]
        o_ref[...] = v
        ob_ref[...] = v.astype(ob_ref.dtype)

    out_spec = pl.BlockSpec((None, tile, width), lambda s, i, c_ref: (s, i, 0))
    gs = pltpu.PrefetchScalarGridSpec(
        num_scalar_prefetch=1, grid=(n, nt),
        in_specs=[pl.BlockSpec((None, tile, width), lambda s, i, c_ref: (s, c_ref[0] * nt + i, 0)), out_spec],
        out_specs=[out_spec, out_spec])
    return pl.pallas_call(body, grid_spec=gs, out_shape=[SDS((n, half, width), F32), SDS((n, half, width), BF16)],
                          compiler_params=_cparams(("parallel", "parallel")), name=name)(c_idx, g, a)


def exchange_small(small):
    def body(sm_ref, all_ref, send_sems, recv_sems, local_sem):
        x, y, c = _me()
        mine = pltpu.make_async_copy(sm_ref, all_ref.at[0], local_sem)
        mine.start()
        copies = []
        for m in range(1, N_DEV):
            peer = (x ^ ((m >> 2) & 1), y ^ ((m >> 1) & 1), c ^ (m & 1))
            copies.append(pltpu.make_async_remote_copy(
                src_ref=sm_ref, dst_ref=all_ref.at[m], send_sem=send_sems.at[m - 1], recv_sem=recv_sems.at[m - 1],
                device_id=peer, device_id_type=MESH_ID))
        for cp in copies:
            cp.start()
        for cp in copies:
            cp.wait()
        mine.wait()

    return pl.pallas_call(
        body, out_shape=SDS((N_DEV,) + small.shape, small.dtype), in_specs=[HBM_SPEC], out_specs=HBM_SPEC,
        scratch_shapes=[pltpu.SemaphoreType.DMA((N_DEV - 1,)), pltpu.SemaphoreType.DMA((N_DEV - 1,)), pltpu.SemaphoreType.DMA],
        name="exchange_small")(small)


def add_chips(sh, b, k_idx, name):
    n, hf, width = sh.shape
    tile = hf // 2

    def body(k_ref, s_ref, b0, b1, b2, o_ref):
        o_ref[...] = ((s_ref[...] + b0[...].astype(F32)) + b1[...].astype(F32)) + b2[...].astype(F32)

    def bspec(j):
        return pl.BlockSpec((None, tile, width), lambda i, k_ref, j=j: (j, i, 0))

    gs = pltpu.PrefetchScalarGridSpec(
        num_scalar_prefetch=1, grid=(hf // tile,),
        in_specs=[pl.BlockSpec((None, tile, width), lambda i, k_ref: (k_ref[0], i, 0)), bspec(0), bspec(1), bspec(2)],
        out_specs=pl.BlockSpec((tile, width), lambda i, k_ref: (i, 0)))
    return pl.pallas_call(body, grid_spec=gs, out_shape=SDS((hf, width), sh.dtype),
                          compiler_params=_cparams(("parallel",)), name=name)(k_idx, sh, b, b, b)


def sum_small(allsm, me_idx, name):
    _, rows, width = allsm.shape

    def body(me_ref, a_ref, o_ref):
        me = me_ref[0]
        acc = a_ref[me]
        for dev in range(1, N_DEV):
            acc = acc + a_ref[jnp.bitwise_xor(me, dev)]
        o_ref[...] = acc

    gs = pltpu.PrefetchScalarGridSpec(
        num_scalar_prefetch=1, grid=(1,),
        in_specs=[pl.BlockSpec((N_DEV, rows, width), lambda i, me_ref: (0, 0, 0))],
        out_specs=pl.BlockSpec((rows, width), lambda i, me_ref: (0, 0)))
    return pl.pallas_call(body, grid_spec=gs, out_shape=SDS((rows, width), allsm.dtype),
                          compiler_params=_cparams(("arbitrary",)), name=name)(me_idx, allsm)


def join_halves(t, core_idx, name):
    hf, width = t.shape

    def body(t_ref, o_ref, send_sem, recv_sem):
        x, y, c = _me()
        cp = pltpu.make_async_remote_copy(src_ref=t_ref, dst_ref=o_ref, send_sem=send_sem, recv_sem=recv_sem,
                                          device_id=(x, y, 1 - c), device_id_type=MESH_ID)
        cp.start()
        cp.wait()

    theirs = pl.pallas_call(
        body, out_shape=SDS((hf, width), t.dtype), in_specs=[HBM_SPEC], out_specs=HBM_SPEC,
        scratch_shapes=[pltpu.SemaphoreType.DMA, pltpu.SemaphoreType.DMA], name=name)(t)
    return jnp.where(core_idx[0] == 0, jnp.concatenate([t, theirs], axis=0), jnp.concatenate([theirs, t], axis=0))


ROW_PAD = 64


def _pad_rows(rows):
    return -(-rows // ROW_PAD) * ROW_PAD


def _pack_rows(names, shard_shapes, width):
    return _pad_rows(sum((shard_shapes[n][0] * shard_shapes[n][1]) // width for n in names))


def unpack_local(packed, names, shard_shapes, width):
    out, r0 = {}, 0
    for n in names:
        a, b = shard_shapes[n]
        nr = (a * b) // width
        out[n] = packed[r0:r0 + nr].reshape(a, b)
        r0 += nr
    return out


EXACT_IN_GATHER = ("conv_w",)
EXACT_TERMS = 3


def pack_gather(shards, names, width):
    parts = []
    for n in names:
        if n in EXACT_IN_GATHER:
            rest = shards[n].astype(F32)
            for _ in range(EXACT_TERMS):
                term = rest.astype(BF16)
                parts.append(term.reshape(-1, width))
                rest = rest - term.astype(F32)
        else:
            parts.append(shards[n].reshape(-1, width).astype(BF16))
    used = sum(p.shape[0] for p in parts)
    parts.append(jnp.zeros((_pad_rows(used) - used, width), BF16))
    return jnp.concatenate(parts, axis=0)


def unpack_full(gathered, names, shard_shapes, width):
    out, r0 = {}, 0
    for n in names:
        a, b = shard_shapes[n]
        terms = EXACT_TERMS if n in EXACT_IN_GATHER else 1
        nr = (a * b) // width
        pieces = []
        for j in range(N_CHIPS):
            blk = gathered[j, r0:r0 + nr].reshape(a, b)
            for t in range(1, terms):
                blk = blk.astype(F32) + gathered[j, r0 + t * nr:r0 + (t + 1) * nr].reshape(a, b).astype(F32)
            pieces.append(blk)
        out[n] = jnp.concatenate(pieces, axis=SHARD_AXIS[n])
        r0 += terms * nr
    return out


def pack_full(grads, names, shard_shapes, width, pad=True):
    total_rows = _pack_rows(names, shard_shapes, width) if pad else sum(
        (shard_shapes[n][0] * shard_shapes[n][1]) // width for n in names)
    slabs = []
    for j in range(N_CHIPS):
        parts = []
        for n in names:
            a, b = shard_shapes[n]
            ax = SHARD_AXIS[n]
            sz = (a, b)[ax]
            piece = lax.slice_in_dim(grads[n], j * sz, (j + 1) * sz, axis=ax)
            parts.append(piece.reshape(-1, width))
        used = sum(p.shape[0] for p in parts)
        parts.append(jnp.zeros((total_rows - used, width), F32))
        slabs.append(jnp.concatenate(parts, axis=0))
    return jnp.stack(slabs, axis=0)


def _small_layout(sizes, width):
    lay, r = {}, 0
    for n in SMALL_WEIGHTS:
        nr = -(-sizes[n] // width)
        lay[n] = (r, nr, sizes[n])
        r += nr
    assert r <= SMALL_ROWS
    return lay


def pack_small(vals, lay, width):
    rows = []
    for n in SMALL_WEIGHTS:
        r, nr, sz = lay[n]
        v = vals[n].reshape(-1).astype(F32)
        rows.append(jnp.pad(v, (0, nr * width - sz)).reshape(nr, width))
    used = sum(r.shape[0] for r in rows)
    rows.append(jnp.zeros((SMALL_ROWS - used, width), F32))
    return jnp.concatenate(rows, axis=0)


def unpack_small(packed, lay):
    return {n: packed[r:r + nr].reshape(-1)[:sz].reshape(1, sz) for n, (r, nr, sz) in lay.items()}


PART_B = ("w_in", "w_gate")
PART_A = ("w_ff1", "w_ff2", "w_ssd_branch", "w_sb_branch", "w_out", "w_ple_gate", "w_ple", "conv_w")
DIRECT_A = PART_A[:6]


class Dist(NamedTuple):
    packed_a: object
    shard_shapes: dict
    core_idx: object
    chip_idx: object


def build_w_all(w_in, w_gate, dm):
    c0 = dm.DI + dm.CD
    return jnp.concatenate(
        [w_in[:, :c0], w_in[:, c0 + dm.H:], w_gate, w_in[:, c0:c0 + dm.H],
         jnp.zeros((dm.D, DT_PAD - dm.H), w_in.dtype)], axis=1).astype(BF16)


def pack_b(w_in_shard, w_gate_shard):
    return jnp.concatenate([w_in_shard, w_gate_shard], axis=1)


def w_all_from_slabs(slabs, dm):
    n_in = (dm.NA - DT_PAD - 2 * dm.D + dm.H) // N_CHIPS
    c0 = dm.DI + dm.CD

    def w_in_cols(a, b):
        return [slabs[j, :, max(a, j * n_in) - j * n_in:min(b, (j + 1) * n_in) - j * n_in]
                for j in range(N_CHIPS) if max(a, j * n_in) < min(b, (j + 1) * n_in)]

    parts = w_in_cols(0, c0) + w_in_cols(c0 + dm.H, N_CHIPS * n_in) + [slabs[j, :, n_in:] for j in range(N_CHIPS)]
    parts += w_in_cols(c0, c0 + dm.H) + [jnp.zeros((dm.D, DT_PAD - dm.H), slabs.dtype)]
    return jnp.concatenate(parts, axis=1).astype(BF16)


def _w_in_columns(lo, hi, dm):
    c0 = dm.DI + dm.CD
    segments = [(0, c0, 0), (c0, c0 + dm.H, dm.off["dt"]), (c0 + dm.H, c0 + dm.H + 3 * dm.SBW, c0)]
    return [(w0 + max(lo, a) - a, w0 + min(hi, b) - a) for a, b, w0 in segments if max(lo, a) < min(hi, b)]


def slabs_b(dw_all, dm):
    n_in = (dm.NA - DT_PAD - 2 * dm.D + dm.H) // N_CHIPS
    n_gate = 2 * dm.D // N_CHIPS
    slabs = []
    for j in range(N_CHIPS):
        cols = _w_in_columns(j * n_in, (j + 1) * n_in, dm) + [(dm.off["gate"] + j * n_gate, dm.off["gate"] + (j + 1) * n_gate)]
        slabs.append(jnp.concatenate([dw_all[:, a:b] for a, b in cols], axis=1))
    return jnp.stack(slabs, axis=0)


def local_step(x, p, tgt, wf, sm, dm, dist=None):
    s, d = dm.S, dm.D
    off = dm.off
    c0 = dm.DI + dm.CD
    w_all = wf["w_all"] if "w_all" in wf else build_w_all(wf["w_in"], wf["w_gate"], dm)
    g = dm.G
    per_group = lambda v: v.reshape(g, 1, GROUP_HEADS)
    alog, dsk = per_group(sm["a_log"]), per_group(sm["d_skip"])
    b_gate = sm["b_gate"]
    b_ssd, b_sb = b_gate[:, :d], b_gate[:, d:]
    gcol = off["gate"] // d

    (n1,) = row_fwd("norm1", f_norm1, [x], [sm["norm_mix_pre"]], [(d, BF16)])
    proj = matmul(n1, w_all, name="in_proj")
    if dist is None:
        y_sb = attn_fwd(proj, dm, "attn_fwd")
    else:
        y_sb, gathered_a = attn_fwd(proj, dm, "attn_fwd", rider=gather_rider(dist.packed_a))
        gathered_a = own_slab(gathered_a, dist.packed_a, dist.chip_idx)
        wf = {**wf, **unpack_full(gathered_a, PART_A, dist.shard_shapes, d)}
    xbc = conv_fwd(proj, off["xbc"], dm.CD, wf["conv_w"].astype(F32), sm["conv_b"], "conv_fwd")
    dt_raw = proj[:, off["dt"]:off["dt"] + dm.H]
    (dt,) = row_fwd("dt", f_dt, [dt_raw], [sm["dt_bias"]], [(dm.H, F32)])
    dtc = dt.reshape(s, g, GROUP_HEADS).transpose(1, 0, 2)
    dtw = dt.reshape(s, g, GROUP_HEADS).transpose(1, 2, 0)
    y_ssd, states = ssd_fwd(proj, xbc, dtc, dtw, alog, dsk, sm["ssd_norm"], dm, "ssd_fwd")
    yb_ssd = matmul(y_ssd, wf["w_ssd_branch"], name="ssd_branch")
    yb_sb = matmul(y_sb, wf["w_sb_branch"], name="sb_branch")
    merge_rows = [(proj, d, gcol), (proj, d, gcol + 1), yb_ssd, yb_sb]
    (merged,) = row_fwd("merge", f_merge, merge_rows, [b_ssd, b_sb], [(d, BF16)])
    mo = matmul(merged, wf["w_out"], name="w_out")
    h1, n2 = row_fwd("mix_out", f_mix_out, [x, mo], [sm["norm_mix_post"], sm["norm_ffn_pre"]], [(d, F32), (d, BF16)])
    a1 = matmul(n2, wf["w_ff1"], name="ff1")
    (act,) = row_fwd("relu2", f_relu2, [a1], [], [(dm.DFF, BF16)])
    ff = matmul(act, wf["w_ff2"], name="ff2")
    (h2,) = row_fwd("ffn_out", f_ffn_out, [h1, ff], [sm["norm_ffn_post"]], [(d, F32)])
    pg = matmul(h2, wf["w_ple_gate"], name="ple_gate")
    pe = matmul(p, wf["w_ple"], name="ple_emb")

    def reduce_start(gbig, tag):
        from_sibling = swap_halves(gbig, "swap_halves_" + tag)
        pair_sum, pair_sum_bf16 = add_half(gbig, from_sibling, dist.core_idx, "add_half_" + tag)
        return pair_sum, exchange_rider(pair_sum_bf16)

    def reduce_finish(pair_sum, from_chips, tag):
        my_half = add_chips(pair_sum, from_chips, dist.chip_idx, "add_chips_" + tag)
        return join_halves(my_half, dist.core_idx, "join_halves_" + tag)

    gr, reduced = {}, None
    packed_grads = [None if dist is None else jnp.zeros((N_CHIPS, _pack_rows(PART_A, dist.shard_shapes, d), d), F32)]

    def weight_grad(wname, a, dy):
        if dist is None:
            gr[wname] = matmul(a, dy, ta=True, name="d_" + wname)
            return
        row0 = 0
        for n in PART_A[:PART_A.index(wname)]:
            row0 += (dist.shard_shapes[n][0] * dist.shard_shapes[n][1]) // d
        packed_grads[0] = matmul(a, dy, ta=True, name="d_" + wname,
                                 slab=SlabOut(packed_grads[0], row0, dist.shard_shapes[wname][0], SHARD_AXIS[wname]))

    (dh2_a, dpg, dpe), (gr["norm_ple_post"], loss_cols) = row_bwd(
        "ple_loss", f_ple_loss, [h2, pg, pe, tgt], [sm["norm_ple_post"]], [None], [F32, BF16, BF16, None], primal_sum=True)
    loss = jnp.sum(loss_cols)
    gr["w_ple"] = matmul(p, dpe, ta=True, name="d_w_ple")
    weight_grad("w_ple_gate", h2, dpg)
    dh2_b = matmul(dpg, wf["w_ple_gate"], tb=True, name="d_h2")
    (dh1_a, dff), (gr["norm_ffn_post"],) = row_bwd(
        "ffn_out_bwd", f_ffn_out, [h1, ff], [sm["norm_ffn_post"]], [[dh2_a, dh2_b]], [F32, BF16])
    weight_grad("w_ff2", act, dff)
    dact = matmul(dff, wf["w_ff2"], tb=True, name="d_act")
    (da1,), _ = row_bwd("relu2_bwd", f_relu2, [a1], [], [[dact]], [BF16])
    weight_grad("w_ff1", n2, da1)
    dn2 = matmul(da1, wf["w_ff1"], tb=True, name="d_n2")
    (dx_a, dmo), (gr["norm_mix_post"], gr["norm_ffn_pre"]) = row_bwd(
        "mix_out_bwd", f_mix_out, [x, mo], [sm["norm_mix_post"], sm["norm_ffn_pre"]], [[dh1_a], [dn2]], [F32, BF16])
    weight_grad("w_out", merged, dmo)
    dmerged = matmul(dmo, wf["w_out"], tb=True, name="d_merged")
    (dgp_ssd, dgp_sb, dyb_ssd, dyb_sb), (db_ssd, db_sb) = row_bwd(
        "merge_bwd", f_merge, merge_rows, [b_ssd, b_sb], [[dmerged]], [BF16, BF16, BF16, BF16])
    gr["b_gate"] = jnp.concatenate([db_ssd, db_sb], axis=1)
    weight_grad("w_ssd_branch", y_ssd, dyb_ssd)
    weight_grad("w_sb_branch", y_sb, dyb_sb)
    dy_ssd = matmul(dyb_ssd, wf["w_ssd_branch"], tb=True, name="d_y_ssd")
    dy_sb = matmul(dyb_sb, wf["w_sb_branch"], tb=True, name="d_y_sb")
    dproj, dxs, dbm, dcm, ddtc, ddtw, dalog, ddsk, gr["ssd_norm"] = ssd_bwd(
        proj, xbc, dtc, dtw, alog, dsk, sm["ssd_norm"], states, dy_ssd, dm, "ssd_bwd")
    gr["a_log"], gr["d_skip"] = (v.reshape(1, dm.H) for v in (dalog, ddsk))
    ddt_post = (ddtc.transpose(1, 0, 2) + ddtw.transpose(2, 0, 1)).reshape(s, dm.H)
    (ddt,), (gr["dt_bias"],) = row_bwd("dt_bwd", f_dt, [dt_raw], [sm["dt_bias"]], [[ddt_post]], [BF16])
    conv_w32 = wf["conv_w"].astype(F32)
    dproj, dw_x, dcb_x = conv_bwd(proj, off["xbc"], dxs, 0, conv_w32, sm["conv_b"], dproj, "conv_bwd_x")
    dproj, dw_b, dcb_b = conv_bwd(proj, off["xbc"], dbm, dm.DI, conv_w32, sm["conv_b"], dproj, "conv_bwd_b")
    dproj, dw_c, dcb_c = conv_bwd(proj, off["xbc"], dcm, dm.DI + g * D_STATE, conv_w32, sm["conv_b"], dproj, "conv_bwd_c")
    gr["conv_w"] = jnp.concatenate([dw_x, dw_b, dw_c], axis=1)
    gr["conv_b"] = jnp.concatenate([dcb_x, dcb_b, dcb_c], axis=1)
    if dist is None:
        dq, dk, dv = attn_bwd(proj, dy_sb, dm, "attn_bwd")
    else:
        rest = tuple(n for n in PART_A if n not in DIRECT_A)
        row0 = sum((dist.shard_shapes[n][0] * dist.shard_shapes[n][1]) // d for n in DIRECT_A)
        small_rows = pack_full(gr, rest, dist.shard_shapes, d, pad=False)
        gbig_a = lax.dynamic_update_slice(packed_grads[0], small_rows, (0, row0, 0))
        pair_sum_a, rider_a = reduce_start(gbig_a, "a")
        dq, dk, dv, from_chips_a = attn_bwd(proj, dy_sb, dm, "attn_bwd", rider=rider_a)
        reduced = unpack_local(reduce_finish(pair_sum_a, from_chips_a, "a"), PART_A, dist.shard_shapes, d)
    col = off["q"]
    for piece in (dq, dk, dv, dgp_ssd, dgp_sb, jnp.concatenate([ddt, jnp.zeros((s, DT_PAD - dm.H), BF16)], axis=1)):
        dproj = lax.dynamic_update_slice(dproj, piece, (0, col))
        col += piece.shape[1]
    dw_all = matmul(n1, dproj, ta=True, name="d_w_all")
    if dist is None:
        gr["w_in"] = jnp.concatenate(
            [dw_all[:, :c0], dw_all[:, off["dt"]:off["dt"] + dm.H], dw_all[:, c0:c0 + 3 * dm.SBW]], axis=1)
        gr["w_gate"] = dw_all[:, off["gate"]:off["gate"] + 2 * d]
        dn1 = matmul(dproj, w_all, tb=True, name="d_n1")
    else:
        pair_sum_b, rider_b = reduce_start(slabs_b(dw_all, dm), "b")
        dn1, from_chips_b = matmul(dproj, w_all, tb=True, name="d_n1", rider=rider_b)
        reduced_b = reduce_finish(pair_sum_b, from_chips_b, "b")
        n_in = dist.shard_shapes["w_in"][1]
        reduced.update(w_in=reduced_b[:, :n_in], w_gate=reduced_b[:, n_in:])
    (dx_b,), (gr["norm_mix_pre"],) = row_bwd("norm1_bwd", f_norm1, [x], [sm["norm_mix_pre"]], [[dn1]], [F32])
    (grad_x,) = row_fwd("grad_x", lambda u, v: (u + v,), [dx_a, dx_b], [], [(d, F32)])
    return loss, grad_x, gr, reduced


def kernel(x, p, norm_mix_pre, w_in, conv_w, conv_b, dt_bias, a_log, d_skip, ssd_norm, w_ssd_branch, w_sb_branch, w_gate, b_gate, w_out, norm_mix_post, norm_ffn_pre, w_ff1, w_ff2, norm_ffn_post, w_ple, w_ple_gate, norm_ple_post, loss_target, m_norm_mix_pre, m_w_in, m_conv_w, m_conv_b, m_dt_bias, m_a_log, m_d_skip, m_ssd_norm, m_w_ssd_branch, m_w_sb_branch, m_w_gate, m_b_gate, m_w_out, m_norm_mix_post, m_norm_ffn_pre, m_w_ff1, m_w_ff2, m_norm_ffn_post, m_w_ple, m_w_ple_gate, m_norm_ple_post, v_norm_mix_pre, v_w_in, v_conv_w, v_conv_b, v_dt_bias, v_a_log, v_d_skip, v_ssd_norm, v_w_ssd_branch, v_w_sb_branch, v_w_gate, v_b_gate, v_w_out, v_norm_mix_post, v_norm_ffn_pre, v_w_ff1, v_w_ff2, v_norm_ffn_post, v_w_ple, v_w_ple_gate, v_norm_ple_post):
    loc = dict(locals())
    unbatch = lambda a: a[0] if a.ndim == 3 else a
    w = {n: unbatch(loc[n]) for n in ALL_WEIGHTS}
    m = {n: unbatch(loc["m_" + n]) for n in ALL_WEIGHTS}
    v = {n: unbatch(loc["v_" + n]) for n in ALL_WEIGHTS}
    xs, ps, tgt = x[0], p[0, 0], loss_target[0]
    s, d = xs.shape
    di = w["w_ssd_branch"].shape[0] * N_CHIPS
    cd = w["conv_b"].shape[1]
    dm = Dims(S=s, D=d, DI=di, H=w["dt_bias"].shape[1], G=(cd - di) // (2 * D_STATE), CD=cd,
              SBW=w["w_sb_branch"].shape[0] * N_CHIPS, DFF=w["w_ff2"].shape[0] * N_CHIPS, PLE=ps.shape[1])
    ix, iy, ic = lax.axis_index("x"), lax.axis_index("y"), lax.axis_index("c")
    chip_idx = jnp.reshape(2 * ix + iy, (1,)).astype(jnp.int32)
    core_idx = jnp.reshape(ic, (1,)).astype(jnp.int32)
    dev_idx = jnp.reshape(4 * ix + 2 * iy + ic, (1,)).astype(jnp.int32)

    shard_shapes = {n: w[n].shape for n in BIG_WEIGHTS}
    packed_b = pack_b(w["w_in"], w["w_gate"]).astype(BF16)
    wf = {"w_all": w_all_from_slabs(own_slab(gather_weights(packed_b), packed_b, chip_idx), dm)}
    sm = {n: w[n] for n in SMALL_WEIGHTS}
    dist = Dist(packed_a=pack_gather(w, PART_A, d), shard_shapes=shard_shapes, core_idx=core_idx, chip_idx=chip_idx)

    loss_part, grad_x, gr, grads = local_step(xs, ps, tgt, wf, sm, dm, dist)
    loss = lax.psum(loss_part, ("x", "y", "c"))

    lay = _small_layout({n: w[n].shape[1] for n in SMALL_WEIGHTS}, d)
    gs_red = sum_small(exchange_small(pack_small(gr, lay, d)), dev_idx, "sum_small")
    grads.update(unpack_small(gs_red, lay))
    delta, new_m, new_v = {}, {}, {}
    for n in BIG_WEIGHTS:
        delta[n], new_m[n], new_v[n] = adamw(w[n], grads[n], m[n], v[n], "adamw_" + n)
    d_sm, nm_sm, nv_sm = adamw(pack_small(w, lay, d), gs_red, pack_small(m, lay, d), pack_small(v, lay, d), "adamw_small")
    for out, packed in ((delta, d_sm), (new_m, nm_sm), (new_v, nv_sm)):
        out.update(unpack_small(packed, lay))

    def leaves(vals):
        return [vals[n][None] if n in BIG_WEIGHTS else vals[n] for n in ALL_WEIGHTS]

    return (loss, grad_x[None], *leaves(grads), *leaves(delta), *leaves(new_m), *leaves(new_v))
```

```python
import functools
import math
from typing import NamedTuple

import jax
import jax.numpy as jnp
from jax import lax
from jax.experimental import pallas as pl
from jax.experimental.pallas import tpu as pltpu

F32 = jnp.float32
BF16 = jnp.bfloat16
SDS = jax.ShapeDtypeStruct

HEAD_DIM = 64
GROUP_HEADS = 4
D_STATE = 128
CHUNK = 128
ATT_TILE = 128
ATT_GROUP = 4
ATT_DEAD = -160.0
LOG2E = 1.4426950408889634
CONV_K = 4
CONV_COLS = 128
RMS_EPS = 1e-6
LANES = 128
DT_PAD = 512
N_CHIPS = 4
N_DEV = 8
SMALL_ROWS = 16
VMEM_LIMIT = 48 * 1024 * 1024
MAX_TK = 3072

ADAM_LR = 0.001
ADAM_B1 = 0.9
ADAM_B2 = 0.999
ADAM_EPS = 1e-08
ADAM_WD = 0.01
ADAM_STEP = 10

MESH_ID = pl.DeviceIdType.MESH
HBM_SPEC = pl.BlockSpec(memory_space=pltpu.HBM)

BIG_WEIGHTS = ("w_in", "conv_w", "w_ssd_branch", "w_sb_branch", "w_gate", "w_out", "w_ff1", "w_ff2", "w_ple", "w_ple_gate")
SHARD_AXIS = {"w_in": 1, "conv_w": 1, "w_ssd_branch": 0, "w_sb_branch": 0, "w_gate": 1, "w_out": 0, "w_ff1": 1,
              "w_ff2": 0, "w_ple": 1, "w_ple_gate": 0}
SMALL_WEIGHTS = ("norm_mix_pre", "conv_b", "dt_bias", "a_log", "d_skip", "ssd_norm", "b_gate", "norm_mix_post",
                 "norm_ffn_pre", "norm_ffn_post", "norm_ple_post")
ALL_WEIGHTS = ("norm_mix_pre", "w_in", "conv_w", "conv_b", "dt_bias", "a_log", "d_skip", "ssd_norm", "w_ssd_branch",
               "w_sb_branch", "w_gate", "b_gate", "w_out", "norm_mix_post", "norm_ffn_pre", "w_ff1", "w_ff2",
               "norm_ffn_post", "w_ple", "w_ple_gate", "norm_ple_post")


class Dims(NamedTuple):
    S: int
    D: int
    DI: int
    H: int
    G: int
    CD: int
    SBW: int
    DFF: int
    PLE: int

    @property
    def NA(self):
        return self.DI + self.CD + 3 * self.SBW + 2 * self.D + DT_PAD

    @property
    def off(self):
        o = {}
        o["z"] = 0
        o["xbc"] = self.DI
        o["q"] = self.DI + self.CD
        o["k"] = o["q"] + self.SBW
        o["v"] = o["k"] + self.SBW
        o["gate"] = o["v"] + self.SBW
        o["dt"] = o["gate"] + 2 * self.D
        return o


def _cparams(sem):
    return pltpu.CompilerParams(dimension_semantics=sem, vmem_limit_bytes=VMEM_LIMIT)


def _pick(n, cands):
    for c in cands:
        if n % c == 0:
            return c
    raise ValueError(f"no tile for {n}")


def _grid_call(body, *, grid, in_specs, out_specs, out_shape, scratch, operands, name, rider=None):
    if rider is None:
        sem = ("parallel",) + ("arbitrary",) * (len(grid) - 1)
        return pl.pallas_call(body, grid=grid, in_specs=in_specs, out_specs=out_specs, out_shape=out_shape,
                              scratch_shapes=scratch, compiler_params=_cparams(sem), name=name)(*operands)
    return pl.pallas_call(
        _with_rider(body, rider, grid, len(in_specs), len(out_specs)), grid=grid,
        in_specs=list(in_specs) + [HBM_SPEC] * len(rider.operands),
        out_specs=list(out_specs) + [HBM_SPEC] * len(rider.out_shape),
        out_shape=list(out_shape) + list(rider.out_shape), scratch_shapes=list(scratch) + list(rider.scratch),
        compiler_params=_cparams(("arbitrary",) * len(grid)), name=name)(*operands, *rider.operands)


class SlabOut(NamedTuple):
    buf: object
    row0: int
    rows: int
    axis: int


def matmul(a, b, *, ta=False, tb=False, out_dtype=F32, name, rider=None, slab=None, finish=None, sides=()):
    m, k = (a.shape[1], a.shape[0]) if ta else a.shape
    n, kb = b.shape if tb else (b.shape[1], b.shape[0])
    assert k == kb, (a.shape, b.shape, ta, tb)
    tm = _pick(m, (1024, 512, 256, 128))
    tn = _pick(n, (512, 256, 128))
    if slab is not None:
        tm = _pick(math.gcd(slab.rows, slab.row0), (1024, 512, 256, 128, 64, 32, 16, 8))
        tn = _pick(slab.buf.shape[2], (512, 256, 128))
    tk = max(t for t in range(LANES, min(k, MAX_TK) + 1, LANES) if k % t == 0)
    nk = k // tk
    dims = (((0 if ta else 1,), (1 if tb else 0,)), ((), ()))

    def body(a_ref, b_ref, o_ref, acc_ref):
        part = lax.dot_general(a_ref[...].astype(BF16), b_ref[...].astype(BF16), dims, preferred_element_type=F32)
        if nk == 1:
            o_ref[...] = part.astype(o_ref.dtype)
        else:
            kk = pl.program_id(2)

            @pl.when(kk == 0)
            def _():
                acc_ref[...] = part

            @pl.when(kk > 0)
            def _():
                acc_ref[...] += part

            @pl.when(kk == nk - 1)
            def _():
                o_ref[...] = acc_ref[...].astype(o_ref.dtype)

    a_spec = pl.BlockSpec((tk, tm), lambda i, j, kk: (kk, i)) if ta else pl.BlockSpec((tm, tk), lambda i, j, kk: (i, kk))
    b_spec = pl.BlockSpec((tn, tk), lambda i, j, kk: (j, kk)) if tb else pl.BlockSpec((tk, tn), lambda i, j, kk: (kk, j))
    if slab is not None:
        width = slab.buf.shape[2]
        rb0, per_shard = slab.row0 // tm, slab.rows // tm
        if slab.axis == 0:
            where = lambda i, j, kk: (i // per_shard, rb0 + i % per_shard, j)
        else:
            where = lambda i, j, kk: (j // (width // tn), rb0 + i, j % (width // tn))
        return pl.pallas_call(
            lambda a_ref, b_ref, buf_ref, o_ref, acc_ref: body(a_ref, b_ref, o_ref, acc_ref),
            grid=(m // tm, n // tn, nk), in_specs=[a_spec, b_spec, pl.BlockSpec(memory_space=pl.ANY)],
            out_specs=pl.BlockSpec((None, tm, tn), where), out_shape=SDS(slab.buf.shape, slab.buf.dtype),
            scratch_shapes=[pltpu.VMEM((tm, tn), F32)], input_output_aliases={2: 0},
            compiler_params=_cparams(("parallel", "parallel", "arbitrary")), name=name)(a, b, slab.buf)
    tile_spec = pl.BlockSpec((tm, tn), lambda i, j, kk: (i, j))
    if finish is not None:
        def fused(*refs):
            a_ref, b_ref, side_refs = refs[0], refs[1], refs[2:2 + len(sides)]
            o_refs, acc_ref = refs[2 + len(sides):-1], refs[-1]

            def emit(acc):
                for o_ref, val in zip(o_refs, finish(acc, *[r[...] for r in side_refs])):
                    o_ref[...] = val.astype(o_ref.dtype)

            part = lax.dot_general(a_ref[...].astype(BF16), b_ref[...].astype(BF16), dims, preferred_element_type=F32)
            if nk == 1:
                emit(part)
            else:
                kk = pl.program_id(2)

                @pl.when(kk == 0)
                def _():
                    acc_ref[...] = part

                @pl.when(kk > 0)
                def _():
                    acc_ref[...] += part

                @pl.when(kk == nk - 1)
                def _():
                    emit(acc_ref[...])

        return pl.pallas_call(
            fused, grid=(m // tm, n // tn, nk), in_specs=[a_spec, b_spec] + [tile_spec] * len(sides),
            out_specs=[tile_spec] * len(out_dtype), out_shape=[SDS((m, n), dt) for dt in out_dtype],
            scratch_shapes=[pltpu.VMEM((tm, tn), F32)],
            compiler_params=_cparams(("parallel", "parallel", "arbitrary")), name=name)(a, b, *sides)
    res = _grid_call(body, grid=(m // tm, n // tn, nk), in_specs=[a_spec, b_spec],
                     out_specs=[tile_spec], out_shape=[SDS((m, n), out_dtype)],
                     scratch=[pltpu.VMEM((tm, tn), F32)], operands=(a, b), name=name, rider=rider)
    return res[0] if rider is None else res


def _row_spec(entry, tile):
    arr, width, cb = entry if isinstance(entry, tuple) else (entry, entry.shape[1], 0)
    return arr, pl.BlockSpec((tile, width), lambda i, cb=cb: (i, cb))


def _par_spec(p):
    return pl.BlockSpec(p.shape, lambda i: (0, 0))


def row_fwd(name, fn, rows, params, outs, tile=256):
    arrs, specs = zip(*[_row_spec(e, tile) for e in rows])
    s = arrs[0].shape[0]
    nr, npar = len(rows), len(params)

    def body(*refs):
        r = [x[...].astype(F32) for x in refs[:nr]]
        p = [x[...] for x in refs[nr:nr + npar]]
        res = fn(*r, *p)
        for o_ref, val in zip(refs[nr + npar:], res):
            o_ref[...] = val.astype(o_ref.dtype)

    return pl.pallas_call(
        body, grid=(s // tile,), in_specs=list(specs) + [_par_spec(p) for p in params],
        out_specs=[pl.BlockSpec((tile, w), lambda i: (i, 0)) for w, _ in outs],
        out_shape=[SDS((s, w), dt) for w, dt in outs],
        compiler_params=_cparams(("parallel",)), name=name)(*arrs, *params)


def row_bwd(name, fn, rows, params, cots, row_grads, tile=256, primal_sum=False):
    arrs, specs = zip(*[_row_spec(e, tile) for e in rows])
    s = arrs[0].shape[0]
    nr, npar = len(rows), len(params)
    cot_entries = [e for c in cots if c is not None for e in c]
    carrs, cspecs = zip(*[_row_spec(e, tile) for e in cot_entries]) if cot_entries else ((), ())
    nc = len(cot_entries)
    want = [i for i, d in enumerate(row_grads) if d is not None]

    def body(*refs):
        r = [x[...].astype(F32) for x in refs[:nr]]
        p = [x[...] for x in refs[nr:nr + npar]]
        cvals = [x[...].astype(F32) for x in refs[nr + npar:nr + npar + nc]]
        outs = refs[nr + npar + nc:]
        prim, vjp = jax.vjp(fn, *r, *p)
        ct, pos = [], 0
        for c, pr in zip(cots, prim):
            if c is None:
                ct.append(jnp.ones_like(pr))
            else:
                acc = cvals[pos]
                for extra in cvals[pos + 1:pos + len(c)]:
                    acc = acc + extra
                pos += len(c)
                ct.append(acc)
        grads = vjp(tuple(ct))
        for o_ref, i in zip(outs[:len(want)], want):
            o_ref[...] = grads[i].astype(o_ref.dtype)
        acc_refs = outs[len(want):]
        vals = [grads[nr + j] for j in range(npar)]
        if primal_sum:
            vals.append(jnp.sum(prim[0], axis=0, keepdims=True))
        first = pl.program_id(0) == 0

        @pl.when(first)
        def _():
            for a_ref, v in zip(acc_refs, vals):
                a_ref[...] = v

        @pl.when(jnp.logical_not(first))
        def _():
            for a_ref, v in zip(acc_refs, vals):
                a_ref[...] += v

    widths = [(e[1] if isinstance(e, tuple) else e.shape[1]) for e in rows]
    out_specs = [pl.BlockSpec((tile, widths[i]), lambda i_: (i_, 0)) for i in want]
    out_shape = [SDS((s, widths[i]), row_grads[i]) for i in want]
    pshapes = [p.shape for p in params]
    if primal_sum:
        pshapes.append((1, widths[0]))
    out_specs += [pl.BlockSpec(sh, lambda i_: (0, 0)) for sh in pshapes]
    out_shape += [SDS(sh, F32) for sh in pshapes]
    res = pl.pallas_call(
        body, grid=(s // tile,), in_specs=list(specs) + [_par_spec(p) for p in params] + list(cspecs),
        out_specs=out_specs, out_shape=out_shape,
        compiler_params=_cparams(("arbitrary",)), name=name)(*arrs, *params, *carrs)
    return res[:len(want)], res[len(want):]


def _rms(x, w):
    return x * lax.rsqrt(jnp.mean(x * x, axis=-1, keepdims=True) + RMS_EPS) * w


def _sigmoid(x):
    return jax.nn.sigmoid(x)


def _softplus(x):
    return jnp.maximum(x, 0.0) + jnp.log1p(jnp.exp(-jnp.abs(x)))


def f_norm1(x, w):
    return (_rms(x, w),)


def f_dt(raw, bias):
    return (_softplus(raw + bias),)


def f_merge(gp_ssd, gp_sb, yb_ssd, yb_sb, b_ssd, b_sb):
    return (_sigmoid(gp_ssd + b_ssd) * yb_ssd + _sigmoid(gp_sb + b_sb) * yb_sb,)


def f_mix_out(x, mo, w_post, w_pre):
    h1 = x + _rms(mo, w_post)
    return h1, _rms(h1, w_pre)


def f_relu2(a1):
    return (jnp.square(jnp.maximum(a1, 0.0)),)


def f_ffn_out(h1, ff, w):
    return (h1 + _rms(ff, w),)


def f_ple_loss(h2, pg, pe, tgt, w):
    h3 = h2 + _rms(_sigmoid(pg) * pe, w)
    return (0.5 * jnp.square(h3 - tgt) * (1.0 / h2.shape[-1]),)


def _shift_down(u, d, rows):
    return u if d == 0 else jnp.where(rows >= d, pltpu.roll(u, d, 0), 0.0)


def _shift_up(u, d, rows):
    s = u.shape[0]
    return u if d == 0 else jnp.where(rows < s - d, pltpu.roll(u, s - d, 0), 0.0)


def conv_fwd(proj, col0, cd, conv_w, conv_b, name):
    s = proj.shape[0]
    cb0 = col0 // CONV_COLS

    def body(u_ref, w_ref, b_ref, o_ref):
        u = u_ref[...]
        rows = lax.broadcasted_iota(jnp.int32, u.shape, 0)
        y = jnp.broadcast_to(b_ref[...], u.shape)
        for k in range(CONV_K):
            y = y + w_ref[k:k + 1, :] * _shift_down(u, CONV_K - 1 - k, rows)
        o_ref[...] = y * _sigmoid(y)

    return pl.pallas_call(
        body, grid=(cd // CONV_COLS,),
        in_specs=[pl.BlockSpec((s, CONV_COLS), lambda i: (0, cb0 + i)),
                  pl.BlockSpec((CONV_K, CONV_COLS), lambda i: (0, i)),
                  pl.BlockSpec((1, CONV_COLS), lambda i: (0, i))],
        out_specs=pl.BlockSpec((s, CONV_COLS), lambda i: (0, i)),
        out_shape=SDS((s, cd), F32), compiler_params=_cparams(("parallel",)), name=name)(proj, conv_w, conv_b)


def conv_bwd(proj, col0, dout, ch0, conv_w, conv_b, dproj, name):
    s = proj.shape[0]
    ncb = dout.shape[1] // CONV_COLS
    cb0 = (col0 + ch0) // CONV_COLS
    wb0 = ch0 // CONV_COLS

    def body(u_ref, g_ref, w_ref, b_ref, _, du_ref, dw_ref, db_ref):
        u = u_ref[...]
        rows = lax.broadcasted_iota(jnp.int32, u.shape, 0)
        y = jnp.broadcast_to(b_ref[...], u.shape)
        for k in range(CONV_K):
            y = y + w_ref[k:k + 1, :] * _shift_down(u, CONV_K - 1 - k, rows)
        sg = _sigmoid(y)
        dy = g_ref[...] * (sg * (1.0 + y * (1.0 - sg)))
        du = jnp.zeros_like(u)
        for k in range(CONV_K):
            d = CONV_K - 1 - k
            du = du + w_ref[k:k + 1, :] * _shift_up(dy, d, rows)
            dw_ref[k:k + 1, :] = jnp.sum(dy * _shift_down(u, d, rows), axis=0, keepdims=True)
        du_ref[...] = du.astype(du_ref.dtype)
        db_ref[...] = jnp.sum(dy, axis=0, keepdims=True)

    return pl.pallas_call(
        body, grid=(ncb,),
        in_specs=[pl.BlockSpec((s, CONV_COLS), lambda i: (0, cb0 + i)),
                  pl.BlockSpec((s, CONV_COLS), lambda i: (0, i)),
                  pl.BlockSpec((CONV_K, CONV_COLS), lambda i: (0, wb0 + i)),
                  pl.BlockSpec((1, CONV_COLS), lambda i: (0, wb0 + i)),
                  pl.BlockSpec(memory_space=pl.ANY)],
        out_specs=[pl.BlockSpec((s, CONV_COLS), lambda i: (0, cb0 + i)),
                   pl.BlockSpec((CONV_K, CONV_COLS), lambda i: (0, i)),
                   pl.BlockSpec((1, CONV_COLS), lambda i: (0, i))],
        out_shape=[SDS(dproj.shape, dproj.dtype), SDS((CONV_K, ncb * CONV_COLS), F32), SDS((1, ncb * CONV_COLS), F32)],
        input_output_aliases={4: 0},
        compiler_params=_cparams(("parallel",)), name=name)(proj, dout, conv_w, conv_b, dproj)


def _dot(a, b, dims):
    return lax.dot_general(a.astype(BF16), b.astype(BF16), (dims, ((), ())), preferred_element_type=F32)


NN = ((1,), (0,))
NT = ((1,), (1,))
TN = ((0,), (0,))


def ssd_chunk(xs, zs, nw, dtc, dtw, alogs, dsks, bm, cm, prev):
    ln = bm.shape[0]
    gw = GROUP_HEADS * HEAD_DIM
    row = lax.broadcasted_iota(jnp.int32, (ln, ln), 0)
    col = lax.broadcasted_iota(jnp.int32, (ln, ln), 1)
    causal = row >= col
    tri = causal.astype(F32)
    tri_t = (row <= col).astype(F32)
    lane_head = lax.broadcasted_iota(jnp.int32, (1, gw), 1) // HEAD_DIM
    sub_head = lax.broadcasted_iota(jnp.int32, (gw, 1), 0) // HEAD_DIM
    on_lanes = [(lane_head == r).astype(F32) for r in range(GROUP_HEADS)]
    on_rows = [(sub_head == r).astype(F32) for r in range(GROUP_HEADS)]
    cb = _dot(cm, bm, NT)
    decays, dt_full, acs_full, end_full, dsk_full, end_rows = [], 0.0, 0.0, 0.0, 0.0, 0.0
    for r in range(GROUP_HEADS):
        a = -jnp.exp(alogs[r])
        da_c = dtc[r] * a
        da_w = dtw[r] * a
        acs_c = jnp.sum(tri * da_w, axis=1, keepdims=True)
        acs_w = jnp.sum(tri_t * da_c, axis=0, keepdims=True)
        alast = jnp.sum(da_w, axis=1, keepdims=True)
        decays.append(jnp.exp(jnp.where(causal, acs_c - acs_w, -jnp.inf)))
        dt_full = dt_full + dtc[r] * on_lanes[r]
        acs_full = acs_full + acs_c * on_lanes[r]
        end_full = end_full + alast * on_lanes[r]
        dsk_full = dsk_full + dsks[r] * on_lanes[r]
        end_rows = end_rows + alast * on_rows[r]
    xd = xs * dt_full
    y = xs * dsk_full + _dot(cm, prev, NT) * jnp.exp(acs_full)
    for r in range(GROUP_HEADS):
        y = y + _dot(cb * decays[r], xd * on_lanes[r], NN)
    new_prev = prev * jnp.exp(end_rows) + _dot(xd * jnp.exp(end_full - acs_full), bm, TN)
    yg = y * (zs * _sigmoid(zs))
    rstd = lax.rsqrt(jnp.mean(yg * yg, axis=-1, keepdims=True) + RMS_EPS)
    return yg * rstd * nw, new_prev


SSD_STEP_GROUPS = 1


def _ssd_specs(dm, cidx):
    u = SSD_STEP_GROUPS
    gw = GROUP_HEADS * HEAD_DIM
    nb0 = dm.DI // D_STATE
    assert nb0 % u == 0 and dm.G % u == 0
    par = pl.BlockSpec((u, 1, GROUP_HEADS), lambda g, c: (g, 0, 0))
    return dict(
        z=pl.BlockSpec((CHUNK, u * gw), lambda g, c: (cidx(c), g)),
        xs=pl.BlockSpec((CHUNK, u * gw), lambda g, c: (cidx(c), g)),
        b=pl.BlockSpec((CHUNK, u * D_STATE), lambda g, c: (cidx(c), nb0 // u + g)),
        c=pl.BlockSpec((CHUNK, u * D_STATE), lambda g, c: (cidx(c), (nb0 + dm.G) // u + g)),
        dtc=pl.BlockSpec((u, CHUNK, GROUP_HEADS), lambda g, c: (g, cidx(c), 0)),
        dtw=pl.BlockSpec((u, GROUP_HEADS, CHUNK), lambda g, c: (g, 0, cidx(c))),
        par=par,
        nw=pl.BlockSpec((1, u * gw), lambda g, c: (0, g)),
        st=pl.BlockSpec((u, None, gw, D_STATE), lambda g, c: (g, cidx(c), 0, 0)),
    )


def _ssd_load(k, z_ref, xs_ref, b_ref, c_ref, dtc_ref, dtw_ref, alog_ref, dsk_ref, nw_ref):
    gw = GROUP_HEADS * HEAD_DIM
    wide, narrow = slice(k * gw, (k + 1) * gw), slice(k * D_STATE, (k + 1) * D_STATE)
    dtc = tuple(dtc_ref[k, :, r:r + 1] for r in range(GROUP_HEADS))
    dtw = tuple(dtw_ref[k, r:r + 1, :] for r in range(GROUP_HEADS))
    alogs = tuple(alog_ref[k, :, r:r + 1] for r in range(GROUP_HEADS))
    dsks = tuple(dsk_ref[k, :, r:r + 1] for r in range(GROUP_HEADS))
    return xs_ref[:, wide], z_ref[:, wide], nw_ref[:, wide], dtc, dtw, alogs, dsks, b_ref[:, narrow], c_ref[:, narrow]


def ssd_fwd(proj, xbc, dtc, dtw, alog, dsk, nw, dm, name):
    nc = dm.S // CHUNK
    u = SSD_STEP_GROUPS
    gw = GROUP_HEADS * HEAD_DIM
    sp = _ssd_specs(dm, lambda c: c)

    def body(z_ref, xs_ref, b_ref, c_ref, dtc_ref, dtw_ref, alog_ref, dsk_ref, nw_ref, y_ref, st_ref, prev):
        @pl.when(pl.program_id(1) == 0)
        def _():
            prev[...] = jnp.zeros_like(prev)

        st_ref[...] = prev[...]
        for k in range(u):
            args = _ssd_load(k, z_ref, xs_ref, b_ref, c_ref, dtc_ref, dtw_ref, alog_ref, dsk_ref, nw_ref)
            out, new = ssd_chunk(*args, prev[k])
            y_ref[:, k * gw:(k + 1) * gw] = out.astype(y_ref.dtype)
            prev[k] = new

    return pl.pallas_call(
        body, grid=(dm.G // u, nc),
        in_specs=[sp["z"], sp["xs"], sp["b"], sp["c"], sp["dtc"], sp["dtw"], sp["par"], sp["par"], sp["nw"]],
        out_specs=[sp["xs"], sp["st"]],
        out_shape=[SDS((dm.S, dm.DI), BF16), SDS((dm.G, nc, gw, D_STATE), F32)],
        scratch_shapes=[pltpu.VMEM((u, gw, D_STATE), F32)],
        compiler_params=_cparams(("parallel", "arbitrary")), name=name)(proj, xbc, xbc, xbc, dtc, dtw, alog, dsk, nw)


def ssd_bwd(proj, xbc, dtc, dtw, alog, dsk, nw, states, dy, dm, name):
    nc = dm.S // CHUNK
    u = SSD_STEP_GROUPS
    sp = _ssd_specs(dm, lambda c: nc - 1 - c)
    gw = GROUP_HEADS * HEAD_DIM
    bc_spec = pl.BlockSpec((CHUNK, u * D_STATE), lambda g, c: (nc - 1 - c, g))

    def body(z_ref, xs_ref, b_ref, c_ref, dtc_ref, dtw_ref, alog_ref, dsk_ref, nw_ref, st_ref, dy_ref,
             dz_ref, dxs_ref, db_ref, dc_ref, ddtc_ref, ddtw_ref, dalog_ref, ddsk_ref, dnw_ref, dprev):
        first = pl.program_id(1) == 0

        @pl.when(first)
        def _():
            dprev[...] = jnp.zeros_like(dprev)

        param_grads = []
        for k in range(u):
            wide, narrow = slice(k * gw, (k + 1) * gw), slice(k * D_STATE, (k + 1) * D_STATE)
            args = _ssd_load(k, z_ref, xs_ref, b_ref, c_ref, dtc_ref, dtw_ref, alog_ref, dsk_ref, nw_ref)
            _, vjp = jax.vjp(ssd_chunk, *args, st_ref[k])
            gxs, gzs, gnw, gdtc, gdtw, galogs, gdsks, gb, gc, gprev = vjp((dy_ref[:, wide], dprev[k]))
            dxs_ref[:, wide] = gxs
            dz_ref[:, wide] = gzs.astype(dz_ref.dtype)
            db_ref[:, narrow] = gb
            dc_ref[:, narrow] = gc
            dprev[k] = gprev
            for r in range(GROUP_HEADS):
                ddtc_ref[k, :, r:r + 1] = gdtc[r]
                ddtw_ref[k, r:r + 1, :] = gdtw[r]
            param_grads.append((wide, gnw, galogs, gdsks))

        @pl.when(first)
        def _():
            for k, (wide, gnw, galogs, gdsks) in enumerate(param_grads):
                dnw_ref[:, wide] = gnw
                for r in range(GROUP_HEADS):
                    dalog_ref[k, :, r:r + 1] = galogs[r]
                    ddsk_ref[k, :, r:r + 1] = gdsks[r]

        @pl.when(jnp.logical_not(first))
        def _():
            for k, (wide, gnw, galogs, gdsks) in enumerate(param_grads):
                dnw_ref[:, wide] += gnw
                for r in range(GROUP_HEADS):
                    dalog_ref[k, :, r:r + 1] += galogs[r]
                    ddsk_ref[k, :, r:r + 1] += gdsks[r]

    xs_out = pl.BlockSpec((CHUNK, u * gw), lambda g, c: (nc - 1 - c, g))
    return pl.pallas_call(
        body, grid=(dm.G // u, nc),
        in_specs=[sp["z"], sp["xs"], sp["b"], sp["c"], sp["dtc"], sp["dtw"], sp["par"], sp["par"], sp["nw"],
                  sp["st"], xs_out],
        out_specs=[xs_out, xs_out, bc_spec, bc_spec, sp["dtc"], sp["dtw"], sp["par"], sp["par"], sp["nw"]],
        out_shape=[SDS((dm.S, dm.NA), BF16), SDS((dm.S, dm.DI), F32), SDS((dm.S, dm.G * D_STATE), F32),
                   SDS((dm.S, dm.G * D_STATE), F32), SDS((dm.G, dm.S, GROUP_HEADS), F32), SDS((dm.G, GROUP_HEADS, dm.S), F32),
                   SDS((dm.G, 1, GROUP_HEADS), F32), SDS((dm.G, 1, GROUP_HEADS), F32), SDS((1, dm.DI), F32)],
        scratch_shapes=[pltpu.VMEM((u, gw, D_STATE), F32)],
        compiler_params=_cparams(("parallel", "arbitrary")), name=name)(
            proj, xbc, xbc, xbc, dtc, dtw, alog, dsk, nw, states, dy)


def _split_bf16(v):
    hi = v.astype(BF16)
    return hi, (v - hi.astype(F32)).astype(BF16)


def _tri(v, mat):
    hi, lo = _split_bf16(v)
    return jnp.dot(hi, mat, preferred_element_type=F32) + jnp.dot(lo, mat, preferred_element_type=F32)


def _blocks(v):
    return [v[:, b * ATT_TILE:(b + 1) * ATT_TILE] for b in range(v.shape[1] // ATT_TILE)]


def _sb_group(z, mask, run, after_mat):
    sp = jnp.maximum(z, 0.0) + jnp.log2(1.0 + jnp.exp2(-jnp.abs(z)))
    lk = -sp if mask is None else jnp.where(mask, -sp, 0.0)
    cums = [_tri(v, after_mat) for v in _blocks(lk)]
    sums = [jnp.sum(v, axis=1, keepdims=True) for v in _blocks(lk)]
    later = [None] * len(cums)
    for b in reversed(range(len(cums))):
        later[b] = run + cums[b]
        run = run + sums[b]
    ls = z - sp
    w = jnp.exp2(ls + jnp.concatenate(later, axis=1))
    if mask is not None:
        w = jnp.where(mask, w, 0.0)
    return ls, w, run


def _alive(run_a, run_b):
    return (jnp.max(jnp.maximum(run_a, run_b)) > ATT_DEAD).astype(jnp.int32)


def _window(i, jj, t):
    gw = ATT_GROUP * t
    end = (i + 1 - ATT_GROUP * jj) * t
    r0 = pl.multiple_of(jnp.maximum(end - gw, 0), t)
    rows = i * t + lax.broadcasted_iota(jnp.int32, (t, gw), 0)
    cols = r0 + lax.broadcasted_iota(jnp.int32, (t, gw), 1)
    return r0, jnp.logical_and(cols < rows, cols < end)


def _att_specs(dm, s):
    t = ATT_TILE
    qb, kb, vb = dm.off["q"] // LANES, dm.off["k"] // LANES, dm.off["v"] // LANES
    return (pl.BlockSpec((t, LANES), lambda p, i: (i, qb + p)),
            pl.BlockSpec((s, LANES), lambda p, i: (0, kb + p)),
            pl.BlockSpec((s, LANES), lambda p, i: (0, vb + p)))


def attn_fwd(proj, dm, name, rider=None):
    s, t = dm.S, ATT_TILE
    scale = HEAD_DIM ** -0.5
    hsl = [slice(h * HEAD_DIM, (h + 1) * HEAD_DIM) for h in range(2)]

    gw = ATT_GROUP * t

    def body(q_ref, k_ref, v_ref, o_ref):
        i = pl.program_id(1)
        gd = i // ATT_GROUP
        r_io = lax.broadcasted_iota(jnp.int32, (t, t), 0)
        c_io = lax.broadcasted_iota(jnp.int32, (t, t), 1)
        after_mat = (r_io > c_io).astype(BF16)
        qs = [q_ref[:, sl].astype(BF16) for sl in hsl]

        def group(jj, carry):
            r0, mask = _window(i, jj, t)
            zs = [_dot(qs[h], k_ref[pl.ds(r0, gw), hsl[h]], NT) * (scale * LOG2E) for h in range(2)]
            res = [_sb_group(zs[h], mask, carry[h][0], after_mat) for h in range(2)]
            return tuple((res[h][2], carry[h][1] + _dot(res[h][1], v_ref[pl.ds(r0, gw), hsl[h]], NN)) for h in range(2))

        zero = (jnp.zeros((t, 1), F32), jnp.zeros((t, HEAD_DIM), F32))
        carry = group(0, (zero, zero))

        def step(st):
            jj, _, c = st
            c = group(jj, c)
            return jj + 1, _alive(c[0][0], c[1][0]), c

        _, _, carry = lax.while_loop(lambda st: jnp.logical_and(st[0] <= gd, st[1] > 0), step,
                                     (jnp.int32(1), _alive(carry[0][0], carry[1][0]), carry))
        for h in range(2):
            o_ref[:, hsl[h]] = carry[h][1]

    qs_, ks_, vs_ = _att_specs(dm, s)
    res = _grid_call(body, grid=(dm.SBW // LANES, s // t), in_specs=[qs_, ks_, vs_],
                     out_specs=[pl.BlockSpec((t, LANES), lambda p, i: (i, p))], out_shape=[SDS((s, dm.SBW), F32)],
                     scratch=[], operands=(proj, proj, proj), name=name, rider=rider)
    return res[0] if rider is None else res


def attn_bwd(proj, do, dm, name, rider=None):
    s, t = dm.S, ATT_TILE
    nq = s // t
    gw = ATT_GROUP * t
    scale = HEAD_DIM ** -0.5
    hsl = [slice(h * HEAD_DIM, (h + 1) * HEAD_DIM) for h in range(2)]

    def body(q_ref, k_ref, v_ref, do_ref, dq_ref, dk_ref, dv_ref, dk_acc, dv_acc, g_scr, s_scr):
        i = pl.program_id(1)

        @pl.when(i == 0)
        def _():
            dk_acc[...] = jnp.zeros_like(dk_acc)
            dv_acc[...] = jnp.zeros_like(dv_acc)

        gd = i // ATT_GROUP
        r_io = lax.broadcasted_iota(jnp.int32, (t, t), 0)
        c_io = lax.broadcasted_iota(jnp.int32, (t, t), 1)
        after_mat = (r_io > c_io).astype(BF16)
        before_mat = (r_io < c_io).astype(BF16)
        qs = [q_ref[:, sl].astype(BF16) for sl in hsl]
        dos = [do_ref[:, sl].astype(BF16) for sl in hsl]
        q_t = q_ref[...].T.astype(BF16)
        do_t = do_ref[...].T.astype(BF16)

        def pass1(jj, runs):
            r0, mask = _window(i, jj, t)
            zs = [_dot(qs[h], k_ref[pl.ds(r0, gw), hsl[h]], NT) * (scale * LOG2E) for h in range(2)]
            dws = [_dot(dos[h], v_ref[pl.ds(r0, gw), hsl[h]], NT) for h in range(2)]
            out = []
            for h in range(2):
                ls, w, run = _sb_group(zs[h], mask, runs[h], after_mat)
                g_scr[h, jj] = dws[h] * w
                s_scr[h, jj] = jnp.exp2(ls)
                dv_acc[hsl[h], pl.ds(r0, gw)] += _dot(do_t[hsl[h]], w, NN)
                out.append(run)
            return tuple(out)

        zero_col = jnp.zeros((t, 1), F32)
        runs = pass1(0, (zero_col, zero_col))

        def step1(st):
            jj, _, r = st
            r = pass1(jj, r)
            return jj + 1, _alive(r[0], r[1]), r

        walked, _, _ = lax.while_loop(lambda st: jnp.logical_and(st[0] <= gd, st[1] > 0), step1,
                                      (jnp.int32(1), _alive(runs[0], runs[1]), runs))

        def pass2(jj, carry):
            r0, mask = _window(i, jj, t)
            out = []
            for h in range(2):
                pre, dq = carry[h]
                gg = g_scr[h, jj]
                sig = s_scr[h, jj]
                before = []
                for v in _blocks(gg):
                    before.append(pre + _tri(v, before_mat))
                    pre = pre + jnp.sum(v, axis=1, keepdims=True)
                dz = jnp.where(mask, gg * (1.0 - sig) - jnp.concatenate(before, axis=1) * sig, 0.0)
                dz = (dz * scale).astype(BF16)
                dk_acc[hsl[h], pl.ds(r0, gw)] += _dot(q_t[hsl[h]], dz, NN)
                out.append((pre, dq + _dot(dz, k_ref[pl.ds(r0, gw), hsl[h]], NN)))
            return tuple(out)

        zero = (zero_col, jnp.zeros((t, HEAD_DIM), F32))
        carry = lax.fori_loop(1, walked, lambda n, c: pass2(walked - n, c), (zero, zero))
        carry = pass2(0, carry)
        for h in range(2):
            dq_ref[:, hsl[h]] = carry[h][1].astype(dq_ref.dtype)

        @pl.when(i == nq - 1)
        def _():
            for g in range(s // gw):
                dk_ref[g * gw:(g + 1) * gw, :] = dk_acc[:, g * gw:(g + 1) * gw].T.astype(dk_ref.dtype)
                dv_ref[g * gw:(g + 1) * gw, :] = dv_acc[:, g * gw:(g + 1) * gw].T.astype(dv_ref.dtype)

    qs_, ks_, vs_ = _att_specs(dm, s)
    tile_spec = pl.BlockSpec((t, LANES), lambda p, i: (i, p))
    full_spec = pl.BlockSpec((s, LANES), lambda p, i: (0, p))
    return _grid_call(
        body, grid=(dm.SBW // LANES, nq), in_specs=[qs_, ks_, vs_, tile_spec],
        out_specs=[tile_spec, full_spec, full_spec], out_shape=[SDS((s, dm.SBW), BF16)] * 3,
        scratch=[pltpu.VMEM((LANES, s), F32), pltpu.VMEM((LANES, s), F32),
                 pltpu.VMEM((2, s // gw, t, gw), F32), pltpu.VMEM((2, s // gw, t, gw), F32)],
        operands=(proj, proj, proj, do), name=name, rider=rider)


def adamw(w, g, m, v, name):
    rows, width = w.shape
    tile = _pick(rows, (256, 64, 16, 8, 4))
    c1 = 1.0 / (1.0 - ADAM_B1 ** ADAM_STEP)
    c2 = 1.0 / (1.0 - ADAM_B2 ** ADAM_STEP)

    def body(w_ref, g_ref, m_ref, v_ref, d_ref, nm_ref, nv_ref):
        gg = g_ref[...]
        nm = ADAM_B1 * m_ref[...] + (1.0 - ADAM_B1) * gg
        nv = ADAM_B2 * v_ref[...] + (1.0 - ADAM_B2) * (gg * gg)
        d_ref[...] = -ADAM_LR * ((nm * c1) / (jnp.sqrt(nv * c2) + ADAM_EPS) + ADAM_WD * w_ref[...])
        nm_ref[...] = nm
        nv_ref[...] = nv

    spec = pl.BlockSpec((tile, width), lambda i: (i, 0))
    return pl.pallas_call(
        body, grid=(rows // tile,), in_specs=[spec] * 4, out_specs=[spec] * 3,
        out_shape=[SDS((rows, width), F32)] * 3, compiler_params=_cparams(("parallel",)), name=name)(w, g, m, v)


def _me():
    return lax.axis_index("x"), lax.axis_index("y"), lax.axis_index("c")


def _other_chips(x, y):
    return [(1 - x, y), (x, 1 - y), (1 - x, 1 - y)]


def gather_weights(wp):
    rows, width = wp.shape
    half = rows // 2

    def body(w_ref, out_ref, send_sems, recv_sems):
        x, y, c = _me()
        sibling = (x, y, 1 - c)
        chips = _other_chips(x, y)

        def part(cx, cy, hf):
            return out_ref.at[2 * cx + cy, hf]

        def copy(k, src, dst, to):
            return pltpu.make_async_remote_copy(src_ref=src, dst_ref=dst, send_sem=send_sems.at[k], recv_sem=recv_sems.at[k],
                                                device_id=to, device_id_type=MESH_ID)

        first = [copy(j, w_ref.at[c], part(x, y, c), (cx, cy, c)) for j, (cx, cy) in enumerate(chips)]
        for cp in first:
            cp.start()
        passed = [copy(3 + j, part(cx, cy, c), part(cx, cy, c), sibling) for j, (cx, cy) in enumerate(chips)]
        for j, (cx, cy) in enumerate(chips):
            copy(j, part(cx, cy, c), part(cx, cy, c), (x, y, c)).wait_recv()
            passed[j].start()
        for j, (cx, cy) in enumerate(chips):
            copy(3 + j, part(cx, cy, 1 - c), part(cx, cy, 1 - c), (x, y, c)).wait_recv()
        for cp in first + passed:
            cp.wait_send()

    return pl.pallas_call(
        body, out_shape=SDS((N_CHIPS, 2, half, width), wp.dtype), in_specs=[HBM_SPEC], out_specs=HBM_SPEC,
        scratch_shapes=[pltpu.SemaphoreType.DMA((6,)), pltpu.SemaphoreType.DMA((6,))],
        name="gather_weights")(wp.reshape(2, half, width)).reshape(N_CHIPS, rows, width)


class Rider(NamedTuple):
    operands: tuple
    out_shape: tuple
    scratch: tuple
    start: object
    wait: object


def _with_rider(body, rider, grid, n_in, n_out):
    if rider is None:
        return body
    r_in, r_out = len(rider.operands), len(rider.out_shape)

    def full(*refs):
        ins, refs = refs[:n_in], refs[n_in:]
        rins, refs = refs[:r_in], refs[r_in:]
        outs, refs = refs[:n_out], refs[n_out:]
        routs, refs = refs[:r_out], refs[r_out:]
        scr, rscr = refs[:len(refs) - len(rider.scratch)], refs[len(refs) - len(rider.scratch):]
        ids = [pl.program_id(k) for k in range(len(grid))]
        first = functools.reduce(jnp.logical_and, [i == 0 for i in ids])
        last = functools.reduce(jnp.logical_and, [i == g - 1 for i, g in zip(ids, grid)])

        @pl.when(first)
        def _():
            rider.start(rins, routs, rscr)

        body(*ins, *outs, *scr)

        @pl.when(last)
        def _():
            rider.wait(rins, routs, rscr)

    return full


def gather_rider(wp):
    def copies(ins, outs, scr, sending):
        (w_ref,), (o_ref,), (send_sems, recv_sems) = ins, outs, scr
        x, y, c = _me()
        return [pltpu.make_async_remote_copy(src_ref=w_ref, dst_ref=o_ref.at[2 * x + y if sending else 2 * cx + cy],
                                             send_sem=send_sems.at[j], recv_sem=recv_sems.at[j], device_id=(cx, cy, c),
                                             device_id_type=MESH_ID)
                for j, (cx, cy) in enumerate(_other_chips(x, y))]

    def start(ins, outs, scr):
        for cp in copies(ins, outs, scr, True):
            cp.start()

    def wait(ins, outs, scr):
        for cp in copies(ins, outs, scr, False):
            cp.wait()

    return Rider((wp,), (SDS((N_CHIPS,) + wp.shape, wp.dtype),),
                 (pltpu.SemaphoreType.DMA((3,)), pltpu.SemaphoreType.DMA((3,))), start, wait)


def own_slab(gathered, wp, chip_idx):
    return lax.dynamic_update_slice(gathered, wp[None], (chip_idx[0], 0, 0))


def exchange_rider(sh):
    def copies(ins, outs, scr):
        (s_ref,), (b_ref,), (send_sems, recv_sems) = ins, outs, scr
        x, y, c = _me()
        return [pltpu.make_async_remote_copy(src_ref=s_ref.at[2 * cx + cy], dst_ref=b_ref.at[j], send_sem=send_sems.at[j],
                                             recv_sem=recv_sems.at[j], device_id=(cx, cy, c), device_id_type=MESH_ID)
                for j, (cx, cy) in enumerate(_other_chips(x, y))]

    def start(ins, outs, scr):
        for cp in copies(ins, outs, scr):
            cp.start()

    def wait(ins, outs, scr):
        for cp in copies(ins, outs, scr):
            cp.wait()

    return Rider((sh,), (SDS((3,) + sh.shape[1:], sh.dtype),),
                 (pltpu.SemaphoreType.DMA((3,)), pltpu.SemaphoreType.DMA((3,))), start, wait)


def swap_halves(g, name):
    n, rows, width = g.shape
    half = rows // 2

    def body(g_ref, a_ref, send_sem, recv_sem):
        x, y, c = _me()
        cp = pltpu.make_async_remote_copy(src_ref=g_ref.at[:, pl.ds((1 - c) * half, half), :], dst_ref=a_ref,
                                          send_sem=send_sem, recv_sem=recv_sem, device_id=(x, y, 1 - c), device_id_type=MESH_ID)
        cp.start()
        cp.wait()

    return pl.pallas_call(
        body, out_shape=SDS((n, half, width), g.dtype), in_specs=[HBM_SPEC], out_specs=HBM_SPEC,
        scratch_shapes=[pltpu.SemaphoreType.DMA, pltpu.SemaphoreType.DMA], name=name)(g)


def add_half(g, a, c_idx, name):
    n, rows, width = g.shape
    half = rows // 2
    tile = half // 2
    nt = half // tile

    def body(c_ref, g_ref, a_ref, o_ref, ob_ref):
        v = g_ref[...] + a_ref[...]
        o_ref[...] = v
        ob_ref[...] = v.astype(ob_ref.dtype)

    out_spec = pl.BlockSpec((None, tile, width), lambda s, i, c_ref: (s, i, 0))
    gs = pltpu.PrefetchScalarGridSpec(
        num_scalar_prefetch=1, grid=(n, nt),
        in_specs=[pl.BlockSpec((None, tile, width), lambda s, i, c_ref: (s, c_ref[0] * nt + i, 0)), out_spec],
        out_specs=[out_spec, out_spec])
    return pl.pallas_call(body, grid_spec=gs, out_shape=[SDS((n, half, width), F32), SDS((n, half, width), BF16)],
                          compiler_params=_cparams(("parallel", "parallel")), name=name)(c_idx, g, a)


def exchange_small(small):
    def body(sm_ref, all_ref, send_sems, recv_sems, local_sem):
        x, y, c = _me()
        mine = pltpu.make_async_copy(sm_ref, all_ref.at[0], local_sem)
        mine.start()
        copies = []
        for m in range(1, N_DEV):
            peer = (x ^ ((m >> 2) & 1), y ^ ((m >> 1) & 1), c ^ (m & 1))
            copies.append(pltpu.make_async_remote_copy(
                src_ref=sm_ref, dst_ref=all_ref.at[m], send_sem=send_sems.at[m - 1], recv_sem=recv_sems.at[m - 1],
                device_id=peer, device_id_type=MESH_ID))
        for cp in copies:
            cp.start()
        for cp in copies:
            cp.wait()
        mine.wait()

    return pl.pallas_call(
        body, out_shape=SDS((N_DEV,) + small.shape, small.dtype), in_specs=[HBM_SPEC], out_specs=HBM_SPEC,
        scratch_shapes=[pltpu.SemaphoreType.DMA((N_DEV - 1,)), pltpu.SemaphoreType.DMA((N_DEV - 1,)), pltpu.SemaphoreType.DMA],
        name="exchange_small")(small)


def add_chips(sh, b, k_idx, name):
    n, hf, width = sh.shape
    tile = hf // 2

    def body(k_ref, s_ref, b0, b1, b2, o_ref):
        o_ref[...] = ((s_ref[...] + b0[...].astype(F32)) + b1[...].astype(F32)) + b2[...].astype(F32)

    def bspec(j):
        return pl.BlockSpec((None, tile, width), lambda i, k_ref, j=j: (j, i, 0))

    gs = pltpu.PrefetchScalarGridSpec(
        num_scalar_prefetch=1, grid=(hf // tile,),
        in_specs=[pl.BlockSpec((None, tile, width), lambda i, k_ref: (k_ref[0], i, 0)), bspec(0), bspec(1), bspec(2)],
        out_specs=pl.BlockSpec((tile, width), lambda i, k_ref: (i, 0)))
    return pl.pallas_call(body, grid_spec=gs, out_shape=SDS((hf, width), sh.dtype),
                          compiler_params=_cparams(("parallel",)), name=name)(k_idx, sh, b, b, b)


def sum_small(allsm, me_idx, name):
    _, rows, width = allsm.shape

    def body(me_ref, a_ref, o_ref):
        me = me_ref[0]
        acc = a_ref[me]
        for dev in range(1, N_DEV):
            acc = acc + a_ref[jnp.bitwise_xor(me, dev)]
        o_ref[...] = acc

    gs = pltpu.PrefetchScalarGridSpec(
        num_scalar_prefetch=1, grid=(1,),
        in_specs=[pl.BlockSpec((N_DEV, rows, width), lambda i, me_ref: (0, 0, 0))],
        out_specs=pl.BlockSpec((rows, width), lambda i, me_ref: (0, 0)))
    return pl.pallas_call(body, grid_spec=gs, out_shape=SDS((rows, width), allsm.dtype),
                          compiler_params=_cparams(("arbitrary",)), name=name)(me_idx, allsm)


def join_halves(t, core_idx, name):
    hf, width = t.shape

    def body(t_ref, o_ref, send_sem, recv_sem):
        x, y, c = _me()
        cp = pltpu.make_async_remote_copy(src_ref=t_ref, dst_ref=o_ref, send_sem=send_sem, recv_sem=recv_sem,
                                          device_id=(x, y, 1 - c), device_id_type=MESH_ID)
        cp.start()
        cp.wait()

    theirs = pl.pallas_call(
        body, out_shape=SDS((hf, width), t.dtype), in_specs=[HBM_SPEC], out_specs=HBM_SPEC,
        scratch_shapes=[pltpu.SemaphoreType.DMA, pltpu.SemaphoreType.DMA], name=name)(t)
    return jnp.where(core_idx[0] == 0, jnp.concatenate([t, theirs], axis=0), jnp.concatenate([theirs, t], axis=0))


ROW_PAD = 64


def _pad_rows(rows):
    return -(-rows // ROW_PAD) * ROW_PAD


def _pack_rows(names, shard_shapes, width):
    return _pad_rows(sum((shard_shapes[n][0] * shard_shapes[n][1]) // width for n in names))


def unpack_local(packed, names, shard_shapes, width):
    out, r0 = {}, 0
    for n in names:
        a, b = shard_shapes[n]
        nr = (a * b) // width
        out[n] = packed[r0:r0 + nr].reshape(a, b)
        r0 += nr
    return out


EXACT_IN_GATHER = ("conv_w",)
EXACT_TERMS = 3


def pack_gather(shards, names, width):
    parts = []
    for n in names:
        if n in EXACT_IN_GATHER:
            rest = shards[n].astype(F32)
            for _ in range(EXACT_TERMS):
                term = rest.astype(BF16)
                parts.append(term.reshape(-1, width))
                rest = rest - term.astype(F32)
        else:
            parts.append(shards[n].reshape(-1, width).astype(BF16))
    used = sum(p.shape[0] for p in parts)
    parts.append(jnp.zeros((_pad_rows(used) - used, width), BF16))
    return jnp.concatenate(parts, axis=0)


def unpack_full(gathered, names, shard_shapes, width):
    out, r0 = {}, 0
    for n in names:
        a, b = shard_shapes[n]
        terms = EXACT_TERMS if n in EXACT_IN_GATHER else 1
        nr = (a * b) // width
        pieces = []
        for j in range(N_CHIPS):
            blk = gathered[j, r0:r0 + nr].reshape(a, b)
            for t in range(1, terms):
                blk = blk.astype(F32) + gathered[j, r0 + t * nr:r0 + (t + 1) * nr].reshape(a, b).astype(F32)
            pieces.append(blk)
        out[n] = jnp.concatenate(pieces, axis=SHARD_AXIS[n])
        r0 += terms * nr
    return out


def pack_full(grads, names, shard_shapes, width, pad=True):
    total_rows = _pack_rows(names, shard_shapes, width) if pad else sum(
        (shard_shapes[n][0] * shard_shapes[n][1]) // width for n in names)
    slabs = []
    for j in range(N_CHIPS):
        parts = []
        for n in names:
            a, b = shard_shapes[n]
            ax = SHARD_AXIS[n]
            sz = (a, b)[ax]
            piece = lax.slice_in_dim(grads[n], j * sz, (j + 1) * sz, axis=ax)
            parts.append(piece.reshape(-1, width))
        used = sum(p.shape[0] for p in parts)
        parts.append(jnp.zeros((total_rows - used, width), F32))
        slabs.append(jnp.concatenate(parts, axis=0))
    return jnp.stack(slabs, axis=0)


def _small_layout(sizes, width):
    lay, r = {}, 0
    for n in SMALL_WEIGHTS:
        nr = -(-sizes[n] // width)
        lay[n] = (r, nr, sizes[n])
        r += nr
    assert r <= SMALL_ROWS
    return lay


def pack_small(vals, lay, width):
    rows = []
    for n in SMALL_WEIGHTS:
        r, nr, sz = lay[n]
        v = vals[n].reshape(-1).astype(F32)
        rows.append(jnp.pad(v, (0, nr * width - sz)).reshape(nr, width))
    used = sum(r.shape[0] for r in rows)
    rows.append(jnp.zeros((SMALL_ROWS - used, width), F32))
    return jnp.concatenate(rows, axis=0)


def unpack_small(packed, lay):
    return {n: packed[r:r + nr].reshape(-1)[:sz].reshape(1, sz) for n, (r, nr, sz) in lay.items()}


PART_B = ("w_in", "w_gate")
PART_A = ("w_ff1", "w_ff2", "w_ssd_branch", "w_sb_branch", "w_out", "w_ple_gate", "w_ple", "conv_w")
DIRECT_A = PART_A[:6]


class Dist(NamedTuple):
    packed_a: object
    shard_shapes: dict
    core_idx: object
    chip_idx: object


def build_w_all(w_in, w_gate, dm):
    c0 = dm.DI + dm.CD
    return jnp.concatenate(
        [w_in[:, :c0], w_in[:, c0 + dm.H:], w_gate, w_in[:, c0:c0 + dm.H],
         jnp.zeros((dm.D, DT_PAD - dm.H), w_in.dtype)], axis=1).astype(BF16)


def pack_b(w_in_shard, w_gate_shard):
    return jnp.concatenate([w_in_shard, w_gate_shard], axis=1)


def w_all_from_slabs(slabs, dm):
    n_in = (dm.NA - DT_PAD - 2 * dm.D + dm.H) // N_CHIPS
    c0 = dm.DI + dm.CD

    def w_in_cols(a, b):
        return [slabs[j, :, max(a, j * n_in) - j * n_in:min(b, (j + 1) * n_in) - j * n_in]
                for j in range(N_CHIPS) if max(a, j * n_in) < min(b, (j + 1) * n_in)]

    parts = w_in_cols(0, c0) + w_in_cols(c0 + dm.H, N_CHIPS * n_in) + [slabs[j, :, n_in:] for j in range(N_CHIPS)]
    parts += w_in_cols(c0, c0 + dm.H) + [jnp.zeros((dm.D, DT_PAD - dm.H), slabs.dtype)]
    return jnp.concatenate(parts, axis=1).astype(BF16)


def _w_in_columns(lo, hi, dm):
    c0 = dm.DI + dm.CD
    segments = [(0, c0, 0), (c0, c0 + dm.H, dm.off["dt"]), (c0 + dm.H, c0 + dm.H + 3 * dm.SBW, c0)]
    return [(w0 + max(lo, a) - a, w0 + min(hi, b) - a) for a, b, w0 in segments if max(lo, a) < min(hi, b)]


def slabs_b(dw_all, dm):
    n_in = (dm.NA - DT_PAD - 2 * dm.D + dm.H) // N_CHIPS
    n_gate = 2 * dm.D // N_CHIPS
    slabs = []
    for j in range(N_CHIPS):
        cols = _w_in_columns(j * n_in, (j + 1) * n_in, dm) + [(dm.off["gate"] + j * n_gate, dm.off["gate"] + (j + 1) * n_gate)]
        slabs.append(jnp.concatenate([dw_all[:, a:b] for a, b in cols], axis=1))
    return jnp.stack(slabs, axis=0)


def local_step(x, p, tgt, wf, sm, dm, dist=None):
    s, d = dm.S, dm.D
    off = dm.off
    c0 = dm.DI + dm.CD
    w_all = wf["w_all"] if "w_all" in wf else build_w_all(wf["w_in"], wf["w_gate"], dm)
    g = dm.G
    per_group = lambda v: v.reshape(g, 1, GROUP_HEADS)
    alog, dsk = per_group(sm["a_log"]), per_group(sm["d_skip"])
    b_gate = sm["b_gate"]
    b_ssd, b_sb = b_gate[:, :d], b_gate[:, d:]
    gcol = off["gate"] // d

    (n1,) = row_fwd("norm1", f_norm1, [x], [sm["norm_mix_pre"]], [(d, BF16)])
    proj = matmul(n1, w_all, name="in_proj")
    if dist is None:
        y_sb = attn_fwd(proj, dm, "attn_fwd")
    else:
        y_sb, gathered_a = attn_fwd(proj, dm, "attn_fwd", rider=gather_rider(dist.packed_a))
        gathered_a = own_slab(gathered_a, dist.packed_a, dist.chip_idx)
        wf = {**wf, **unpack_full(gathered_a, PART_A, dist.shard_shapes, d)}
    xbc = conv_fwd(proj, off["xbc"], dm.CD, wf["conv_w"].astype(F32), sm["conv_b"], "conv_fwd")
    dt_raw = proj[:, off["dt"]:off["dt"] + dm.H]
    (dt,) = row_fwd("dt", f_dt, [dt_raw], [sm["dt_bias"]], [(dm.H, F32)])
    dtc = dt.reshape(s, g, GROUP_HEADS).transpose(1, 0, 2)
    dtw = dt.reshape(s, g, GROUP_HEADS).transpose(1, 2, 0)
    y_ssd, states = ssd_fwd(proj, xbc, dtc, dtw, alog, dsk, sm["ssd_norm"], dm, "ssd_fwd")
    yb_ssd = matmul(y_ssd, wf["w_ssd_branch"], name="ssd_branch")
    yb_sb = matmul(y_sb, wf["w_sb_branch"], name="sb_branch")
    merge_rows = [(proj, d, gcol), (proj, d, gcol + 1), yb_ssd, yb_sb]
    (merged,) = row_fwd("merge", f_merge, merge_rows, [b_ssd, b_sb], [(d, BF16)])
    mo = matmul(merged, wf["w_out"], name="w_out")
    h1, n2 = row_fwd("mix_out", f_mix_out, [x, mo], [sm["norm_mix_post"], sm["norm_ffn_pre"]], [(d, F32), (d, BF16)])
    a1, act = matmul(n2, wf["w_ff1"], name="ff1", finish=lambda v: (v,) + f_relu2(v), out_dtype=(F32, BF16))
    ff = matmul(act, wf["w_ff2"], name="ff2")
    (h2,) = row_fwd("ffn_out", f_ffn_out, [h1, ff], [sm["norm_ffn_post"]], [(d, F32)])
    pg = matmul(h2, wf["w_ple_gate"], name="ple_gate")
    pe = matmul(p, wf["w_ple"], name="ple_emb")

    def reduce_start(gbig, tag):
        from_sibling = swap_halves(gbig, "swap_halves_" + tag)
        pair_sum, pair_sum_bf16 = add_half(gbig, from_sibling, dist.core_idx, "add_half_" + tag)
        return pair_sum, exchange_rider(pair_sum_bf16)

    def reduce_finish(pair_sum, from_chips, tag):
        my_half = add_chips(pair_sum, from_chips, dist.chip_idx, "add_chips_" + tag)
        return join_halves(my_half, dist.core_idx, "join_halves_" + tag)

    gr, reduced = {}, None
    packed_grads = [None if dist is None else jnp.zeros((N_CHIPS, _pack_rows(PART_A, dist.shard_shapes, d), d), F32)]

    def weight_grad(wname, a, dy):
        if dist is None:
            gr[wname] = matmul(a, dy, ta=True, name="d_" + wname)
            return
        row0 = 0
        for n in PART_A[:PART_A.index(wname)]:
            row0 += (dist.shard_shapes[n][0] * dist.shard_shapes[n][1]) // d
        packed_grads[0] = matmul(a, dy, ta=True, name="d_" + wname,
                                 slab=SlabOut(packed_grads[0], row0, dist.shard_shapes[wname][0], SHARD_AXIS[wname]))

    (dh2_a, dpg, dpe), (gr["norm_ple_post"], loss_cols) = row_bwd(
        "ple_loss", f_ple_loss, [h2, pg, pe, tgt], [sm["norm_ple_post"]], [None], [F32, BF16, BF16, None], primal_sum=True)
    loss = jnp.sum(loss_cols)
    gr["w_ple"] = matmul(p, dpe, ta=True, name="d_w_ple")
    weight_grad("w_ple_gate", h2, dpg)
    dh2_b = matmul(dpg, wf["w_ple_gate"], tb=True, name="d_h2")
    (dh1_a, dff), (gr["norm_ffn_post"],) = row_bwd(
        "ffn_out_bwd", f_ffn_out, [h1, ff], [sm["norm_ffn_post"]], [[dh2_a, dh2_b]], [F32, BF16])
    weight_grad("w_ff2", act, dff)
    (da1,) = matmul(dff, wf["w_ff2"], tb=True, name="d_act", sides=(a1,),
                    finish=lambda dact, a1_tile: (dact * (2.0 * jnp.maximum(a1_tile, 0.0)),), out_dtype=(BF16,))
    weight_grad("w_ff1", n2, da1)
    dn2 = matmul(da1, wf["w_ff1"], tb=True, name="d_n2")
    (dx_a, dmo), (gr["norm_mix_post"], gr["norm_ffn_pre"]) = row_bwd(
        "mix_out_bwd", f_mix_out, [x, mo], [sm["norm_mix_post"], sm["norm_ffn_pre"]], [[dh1_a], [dn2]], [F32, BF16])
    weight_grad("w_out", merged, dmo)
    dmerged = matmul(dmo, wf["w_out"], tb=True, name="d_merged")
    (dgp_ssd, dgp_sb, dyb_ssd, dyb_sb), (db_ssd, db_sb) = row_bwd(
        "merge_bwd", f_merge, merge_rows, [b_ssd, b_sb], [[dmerged]], [BF16, BF16, BF16, BF16])
    gr["b_gate"] = jnp.concatenate([db_ssd, db_sb], axis=1)
    weight_grad("w_ssd_branch", y_ssd, dyb_ssd)
    weight_grad("w_sb_branch", y_sb, dyb_sb)
    dy_ssd = matmul(dyb_ssd, wf["w_ssd_branch"], tb=True, name="d_y_ssd")
    dy_sb = matmul(dyb_sb, wf["w_sb_branch"], tb=True, name="d_y_sb")
    dproj, dxs, dbm, dcm, ddtc, ddtw, dalog, ddsk, gr["ssd_norm"] = ssd_bwd(
        proj, xbc, dtc, dtw, alog, dsk, sm["ssd_norm"], states, dy_ssd, dm, "ssd_bwd")
    gr["a_log"], gr["d_skip"] = (v.reshape(1, dm.H) for v in (dalog, ddsk))
    ddt_post = (ddtc.transpose(1, 0, 2) + ddtw.transpose(2, 0, 1)).reshape(s, dm.H)
    (ddt,), (gr["dt_bias"],) = row_bwd("dt_bwd", f_dt, [dt_raw], [sm["dt_bias"]], [[ddt_post]], [BF16])
    conv_w32 = wf["conv_w"].astype(F32)
    dproj, dw_x, dcb_x = conv_bwd(proj, off["xbc"], dxs, 0, conv_w32, sm["conv_b"], dproj, "conv_bwd_x")
    dproj, dw_b, dcb_b = conv_bwd(proj, off["xbc"], dbm, dm.DI, conv_w32, sm["conv_b"], dproj, "conv_bwd_b")
    dproj, dw_c, dcb_c = conv_bwd(proj, off["xbc"], dcm, dm.DI + g * D_STATE, conv_w32, sm["conv_b"], dproj, "conv_bwd_c")
    gr["conv_w"] = jnp.concatenate([dw_x, dw_b, dw_c], axis=1)
    gr["conv_b"] = jnp.concatenate([dcb_x, dcb_b, dcb_c], axis=1)
    if dist is None:
        dq, dk, dv = attn_bwd(proj, dy_sb, dm, "attn_bwd")
    else:
        rest = tuple(n for n in PART_A if n not in DIRECT_A)
        row0 = sum((dist.shard_shapes[n][0] * dist.shard_shapes[n][1]) // d for n in DIRECT_A)
        small_rows = pack_full(gr, rest, dist.shard_shapes, d, pad=False)
        gbig_a = lax.dynamic_update_slice(packed_grads[0], small_rows, (0, row0, 0))
        pair_sum_a, rider_a = reduce_start(gbig_a, "a")
        dq, dk, dv, from_chips_a = attn_bwd(proj, dy_sb, dm, "attn_bwd", rider=rider_a)
        reduced = unpack_local(reduce_finish(pair_sum_a, from_chips_a, "a"), PART_A, dist.shard_shapes, d)
    col = off["q"]
    for piece in (dq, dk, dv, dgp_ssd, dgp_sb, jnp.concatenate([ddt, jnp.zeros((s, DT_PAD - dm.H), BF16)], axis=1)):
        dproj = lax.dynamic_update_slice(dproj, piece, (0, col))
        col += piece.shape[1]
    dw_all = matmul(n1, dproj, ta=True, name="d_w_all")
    if dist is None:
        gr["w_in"] = jnp.concatenate(
            [dw_all[:, :c0], dw_all[:, off["dt"]:off["dt"] + dm.H], dw_all[:, c0:c0 + 3 * dm.SBW]], axis=1)
        gr["w_gate"] = dw_all[:, off["gate"]:off["gate"] + 2 * d]
        dn1 = matmul(dproj, w_all, tb=True, name="d_n1")
    else:
        pair_sum_b, rider_b = reduce_start(slabs_b(dw_all, dm), "b")
        dn1, from_chips_b = matmul(dproj, w_all, tb=True, name="d_n1", rider=rider_b)
        reduced_b = reduce_finish(pair_sum_b, from_chips_b, "b")
        n_in = dist.shard_shapes["w_in"][1]
        reduced.update(w_in=reduced_b[:, :n_in], w_gate=reduced_b[:, n_in:])
    (grad_x,), (gr["norm_mix_pre"],) = row_bwd("norm1_bwd", lambda u, w: (_rms(u, w), u), [x], [sm["norm_mix_pre"]],
                                               [[dn1], [dx_a]], [F32])
    return loss, grad_x, gr, reduced


def kernel(x, p, norm_mix_pre, w_in, conv_w, conv_b, dt_bias, a_log, d_skip, ssd_norm, w_ssd_branch, w_sb_branch, w_gate, b_gate, w_out, norm_mix_post, norm_ffn_pre, w_ff1, w_ff2, norm_ffn_post, w_ple, w_ple_gate, norm_ple_post, loss_target, m_norm_mix_pre, m_w_in, m_conv_w, m_conv_b, m_dt_bias, m_a_log, m_d_skip, m_ssd_norm, m_w_ssd_branch, m_w_sb_branch, m_w_gate, m_b_gate, m_w_out, m_norm_mix_post, m_norm_ffn_pre, m_w_ff1, m_w_ff2, m_norm_ffn_post, m_w_ple, m_w_ple_gate, m_norm_ple_post, v_norm_mix_pre, v_w_in, v_conv_w, v_conv_b, v_dt_bias, v_a_log, v_d_skip, v_ssd_norm, v_w_ssd_branch, v_w_sb_branch, v_w_gate, v_b_gate, v_w_out, v_norm_mix_post, v_norm_ffn_pre, v_w_ff1, v_w_ff2, v_norm_ffn_post, v_w_ple, v_w_ple_gate, v_norm_ple_post):
    loc = dict(locals())
    unbatch = lambda a: a[0] if a.ndim == 3 else a
    w = {n: unbatch(loc[n]) for n in ALL_WEIGHTS}
    m = {n: unbatch(loc["m_" + n]) for n in ALL_WEIGHTS}
    v = {n: unbatch(loc["v_" + n]) for n in ALL_WEIGHTS}
    xs, ps, tgt = x[0], p[0, 0], loss_target[0]
    s, d = xs.shape
    di = w["w_ssd_branch"].shape[0] * N_CHIPS
    cd = w["conv_b"].shape[1]
    dm = Dims(S=s, D=d, DI=di, H=w["dt_bias"].shape[1], G=(cd - di) // (2 * D_STATE), CD=cd,
              SBW=w["w_sb_branch"].shape[0] * N_CHIPS, DFF=w["w_ff2"].shape[0] * N_CHIPS, PLE=ps.shape[1])
    ix, iy, ic = lax.axis_index("x"), lax.axis_index("y"), lax.axis_index("c")
    chip_idx = jnp.reshape(2 * ix + iy, (1,)).astype(jnp.int32)
    core_idx = jnp.reshape(ic, (1,)).astype(jnp.int32)
    dev_idx = jnp.reshape(4 * ix + 2 * iy + ic, (1,)).astype(jnp.int32)

    shard_shapes = {n: w[n].shape for n in BIG_WEIGHTS}
    packed_b = pack_b(w["w_in"], w["w_gate"]).astype(BF16)
    wf = {"w_all": w_all_from_slabs(own_slab(gather_weights(packed_b), packed_b, chip_idx), dm)}
    sm = {n: w[n] for n in SMALL_WEIGHTS}
    dist = Dist(packed_a=pack_gather(w, PART_A, d), shard_shapes=shard_shapes, core_idx=core_idx, chip_idx=chip_idx)

    loss_part, grad_x, gr, grads = local_step(xs, ps, tgt, wf, sm, dm, dist)
    loss = lax.psum(loss_part, ("x", "y", "c"))

    lay = _small_layout({n: w[n].shape[1] for n in SMALL_WEIGHTS}, d)
    gs_red = sum_small(exchange_small(pack_small(gr, lay, d)), dev_idx, "sum_small")
    grads.update(unpack_small(gs_red, lay))
    delta, new_m, new_v = {}, {}, {}
    for n in BIG_WEIGHTS:
        delta[n], new_m[n], new_v[n] = adamw(w[n], grads[n], m[n], v[n], "adamw_" + n)
    d_sm, nm_sm, nv_sm = adamw(pack_small(w, lay, d), gs_red, pack_small(m, lay, d), pack_small(v, lay, d), "adamw_small")
    for out, packed in ((delta, d_sm), (new_m, nm_sm), (new_v, nv_sm)):
        out.update(unpack_small(packed, lay))

    def leaves(vals):
        return [vals[n][None] if n in BIG_WEIGHTS else vals[n] for n in ALL_WEIGHTS]

    return (loss, grad_x[None], *leaves(grads), *leaves(delta), *leaves(new_m), *leaves(new_v))
```

```python
import functools
import math
from typing import NamedTuple

import jax
import jax.numpy as jnp
from jax import lax
from jax.experimental import pallas as pl
from jax.experimental.pallas import tpu as pltpu

F32 = jnp.float32
BF16 = jnp.bfloat16
SDS = jax.ShapeDtypeStruct

HEAD_DIM = 64
GROUP_HEADS = 4
D_STATE = 128
CHUNK = 128
ATT_TILE = 128
ATT_GROUP = 3
ATT_DEAD = -160.0
LOG2E = 1.4426950408889634
CONV_K = 4
CONV_COLS = 128
RMS_EPS = 1e-6
LANES = 128
DT_PAD = 512
N_CHIPS = 4
N_DEV = 8
SMALL_ROWS = 16
VMEM_LIMIT = 48 * 1024 * 1024
MAX_TK = 3072

ADAM_LR = 0.001
ADAM_B1 = 0.9
ADAM_B2 = 0.999
ADAM_EPS = 1e-08
ADAM_WD = 0.01
ADAM_STEP = 10

MESH_ID = pl.DeviceIdType.MESH
HBM_SPEC = pl.BlockSpec(memory_space=pltpu.HBM)

BIG_WEIGHTS = ("w_in", "conv_w", "w_ssd_branch", "w_sb_branch", "w_gate", "w_out", "w_ff1", "w_ff2", "w_ple", "w_ple_gate")
SHARD_AXIS = {"w_in": 1, "conv_w": 1, "w_ssd_branch": 0, "w_sb_branch": 0, "w_gate": 1, "w_out": 0, "w_ff1": 1,
              "w_ff2": 0, "w_ple": 1, "w_ple_gate": 0}
SMALL_WEIGHTS = ("norm_mix_pre", "conv_b", "dt_bias", "a_log", "d_skip", "ssd_norm", "b_gate", "norm_mix_post",
                 "norm_ffn_pre", "norm_ffn_post", "norm_ple_post")
ALL_WEIGHTS = ("norm_mix_pre", "w_in", "conv_w", "conv_b", "dt_bias", "a_log", "d_skip", "ssd_norm", "w_ssd_branch",
               "w_sb_branch", "w_gate", "b_gate", "w_out", "norm_mix_post", "norm_ffn_pre", "w_ff1", "w_ff2",
               "norm_ffn_post", "w_ple", "w_ple_gate", "norm_ple_post")


class Dims(NamedTuple):
    S: int
    D: int
    DI: int
    H: int
    G: int
    CD: int
    SBW: int
    DFF: int
    PLE: int

    @property
    def NA(self):
        return self.DI + self.CD + 3 * self.SBW + 2 * self.D + DT_PAD

    @property
    def off(self):
        o = {}
        o["z"] = 0
        o["xbc"] = self.DI
        o["q"] = self.DI + self.CD
        o["k"] = o["q"] + self.SBW
        o["v"] = o["k"] + self.SBW
        o["gate"] = o["v"] + self.SBW
        o["dt"] = o["gate"] + 2 * self.D
        return o


def _cparams(sem):
    return pltpu.CompilerParams(dimension_semantics=sem, vmem_limit_bytes=VMEM_LIMIT)


def _pick(n, cands):
    for c in cands:
        if n % c == 0:
            return c
    raise ValueError(f"no tile for {n}")


def _grid_call(body, *, grid, in_specs, out_specs, out_shape, scratch, operands, name, rider=None):
    if rider is None:
        sem = ("parallel",) + ("arbitrary",) * (len(grid) - 1)
        return pl.pallas_call(body, grid=grid, in_specs=in_specs, out_specs=out_specs, out_shape=out_shape,
                              scratch_shapes=scratch, compiler_params=_cparams(sem), name=name)(*operands)
    return pl.pallas_call(
        _with_rider(body, rider, grid, len(in_specs), len(out_specs)), grid=grid,
        in_specs=list(in_specs) + [HBM_SPEC] * len(rider.operands),
        out_specs=list(out_specs) + [HBM_SPEC] * len(rider.out_shape),
        out_shape=list(out_shape) + list(rider.out_shape), scratch_shapes=list(scratch) + list(rider.scratch),
        compiler_params=_cparams(("arbitrary",) * len(grid)), name=name)(*operands, *rider.operands)


class SlabOut(NamedTuple):
    buf: object
    row0: int
    rows: int
    axis: int


def matmul(a, b, *, ta=False, tb=False, out_dtype=F32, name, rider=None, slab=None, finish=None, sides=()):
    m, k = (a.shape[1], a.shape[0]) if ta else a.shape
    n, kb = b.shape if tb else (b.shape[1], b.shape[0])
    assert k == kb, (a.shape, b.shape, ta, tb)
    tm = _pick(m, (1024, 512, 256, 128))
    tn = _pick(n, (512, 256, 128))
    if slab is not None:
        tm = _pick(math.gcd(slab.rows, slab.row0), (1024, 512, 256, 128, 64, 32, 16, 8))
        tn = _pick(slab.buf.shape[2], (512, 256, 128))
    tk = max(t for t in range(LANES, min(k, MAX_TK) + 1, LANES) if k % t == 0)
    nk = k // tk
    dims = (((0 if ta else 1,), (1 if tb else 0,)), ((), ()))

    def body(a_ref, b_ref, o_ref, acc_ref):
        part = lax.dot_general(a_ref[...].astype(BF16), b_ref[...].astype(BF16), dims, preferred_element_type=F32)
        if nk == 1:
            o_ref[...] = part.astype(o_ref.dtype)
        else:
            kk = pl.program_id(2)

            @pl.when(kk == 0)
            def _():
                acc_ref[...] = part

            @pl.when(kk > 0)
            def _():
                acc_ref[...] += part

            @pl.when(kk == nk - 1)
            def _():
                o_ref[...] = acc_ref[...].astype(o_ref.dtype)

    a_spec = pl.BlockSpec((tk, tm), lambda i, j, kk: (kk, i)) if ta else pl.BlockSpec((tm, tk), lambda i, j, kk: (i, kk))
    b_spec = pl.BlockSpec((tn, tk), lambda i, j, kk: (j, kk)) if tb else pl.BlockSpec((tk, tn), lambda i, j, kk: (kk, j))
    if slab is not None:
        width = slab.buf.shape[2]
        rb0, per_shard = slab.row0 // tm, slab.rows // tm
        if slab.axis == 0:
            where = lambda i, j, kk: (i // per_shard, rb0 + i % per_shard, j)
        else:
            where = lambda i, j, kk: (j // (width // tn), rb0 + i, j % (width // tn))
        return pl.pallas_call(
            lambda a_ref, b_ref, buf_ref, o_ref, acc_ref: body(a_ref, b_ref, o_ref, acc_ref),
            grid=(m // tm, n // tn, nk), in_specs=[a_spec, b_spec, pl.BlockSpec(memory_space=pl.ANY)],
            out_specs=pl.BlockSpec((None, tm, tn), where), out_shape=SDS(slab.buf.shape, slab.buf.dtype),
            scratch_shapes=[pltpu.VMEM((tm, tn), F32)], input_output_aliases={2: 0},
            compiler_params=_cparams(("parallel", "parallel", "arbitrary")), name=name)(a, b, slab.buf)
    tile_spec = pl.BlockSpec((tm, tn), lambda i, j, kk: (i, j))
    if finish is not None:
        def fused(*refs):
            a_ref, b_ref, side_refs = refs[0], refs[1], refs[2:2 + len(sides)]
            o_refs, acc_ref = refs[2 + len(sides):-1], refs[-1]

            def emit(acc):
                for o_ref, val in zip(o_refs, finish(acc, *[r[...] for r in side_refs])):
                    o_ref[...] = val.astype(o_ref.dtype)

            part = lax.dot_general(a_ref[...].astype(BF16), b_ref[...].astype(BF16), dims, preferred_element_type=F32)
            if nk == 1:
                emit(part)
            else:
                kk = pl.program_id(2)

                @pl.when(kk == 0)
                def _():
                    acc_ref[...] = part

                @pl.when(kk > 0)
                def _():
                    acc_ref[...] += part

                @pl.when(kk == nk - 1)
                def _():
                    emit(acc_ref[...])

        return pl.pallas_call(
            fused, grid=(m // tm, n // tn, nk), in_specs=[a_spec, b_spec] + [tile_spec] * len(sides),
            out_specs=[tile_spec] * len(out_dtype), out_shape=[SDS((m, n), dt) for dt in out_dtype],
            scratch_shapes=[pltpu.VMEM((tm, tn), F32)],
            compiler_params=_cparams(("parallel", "parallel", "arbitrary")), name=name)(a, b, *sides)
    res = _grid_call(body, grid=(m // tm, n // tn, nk), in_specs=[a_spec, b_spec],
                     out_specs=[tile_spec], out_shape=[SDS((m, n), out_dtype)],
                     scratch=[pltpu.VMEM((tm, tn), F32)], operands=(a, b), name=name, rider=rider)
    return res[0] if rider is None else res


def _row_spec(entry, tile):
    arr, width, cb = entry if isinstance(entry, tuple) else (entry, entry.shape[1], 0)
    return arr, pl.BlockSpec((tile, width), lambda i, cb=cb: (i, cb))


def _par_spec(p):
    return pl.BlockSpec(p.shape, lambda i: (0, 0))


def row_fwd(name, fn, rows, params, outs, tile=256):
    arrs, specs = zip(*[_row_spec(e, tile) for e in rows])
    s = arrs[0].shape[0]
    nr, npar = len(rows), len(params)

    def body(*refs):
        r = [x[...].astype(F32) for x in refs[:nr]]
        p = [x[...] for x in refs[nr:nr + npar]]
        res = fn(*r, *p)
        for o_ref, val in zip(refs[nr + npar:], res):
            o_ref[...] = val.astype(o_ref.dtype)

    return pl.pallas_call(
        body, grid=(s // tile,), in_specs=list(specs) + [_par_spec(p) for p in params],
        out_specs=[pl.BlockSpec((tile, w), lambda i: (i, 0)) for w, _ in outs],
        out_shape=[SDS((s, w), dt) for w, dt in outs],
        compiler_params=_cparams(("parallel",)), name=name)(*arrs, *params)


def row_bwd(name, fn, rows, params, cots, row_grads, tile=256, primal_sum=False):
    arrs, specs = zip(*[_row_spec(e, tile) for e in rows])
    s = arrs[0].shape[0]
    nr, npar = len(rows), len(params)
    cot_entries = [e for c in cots if c is not None for e in c]
    carrs, cspecs = zip(*[_row_spec(e, tile) for e in cot_entries]) if cot_entries else ((), ())
    nc = len(cot_entries)
    want = [i for i, d in enumerate(row_grads) if d is not None]

    def body(*refs):
        r = [x[...].astype(F32) for x in refs[:nr]]
        p = [x[...] for x in refs[nr:nr + npar]]
        cvals = [x[...].astype(F32) for x in refs[nr + npar:nr + npar + nc]]
        outs = refs[nr + npar + nc:]
        prim, vjp = jax.vjp(fn, *r, *p)
        ct, pos = [], 0
        for c, pr in zip(cots, prim):
            if c is None:
                ct.append(jnp.ones_like(pr))
            else:
                acc = cvals[pos]
                for extra in cvals[pos + 1:pos + len(c)]:
                    acc = acc + extra
                pos += len(c)
                ct.append(acc)
        grads = vjp(tuple(ct))
        for o_ref, i in zip(outs[:len(want)], want):
            o_ref[...] = grads[i].astype(o_ref.dtype)
        acc_refs = outs[len(want):]
        vals = [grads[nr + j] for j in range(npar)]
        if primal_sum:
            vals.append(jnp.sum(prim[0], axis=0, keepdims=True))
        first = pl.program_id(0) == 0

        @pl.when(first)
        def _():
            for a_ref, v in zip(acc_refs, vals):
                a_ref[...] = v

        @pl.when(jnp.logical_not(first))
        def _():
            for a_ref, v in zip(acc_refs, vals):
                a_ref[...] += v

    widths = [(e[1] if isinstance(e, tuple) else e.shape[1]) for e in rows]
    out_specs = [pl.BlockSpec((tile, widths[i]), lambda i_: (i_, 0)) for i in want]
    out_shape = [SDS((s, widths[i]), row_grads[i]) for i in want]
    pshapes = [p.shape for p in params]
    if primal_sum:
        pshapes.append((1, widths[0]))
    out_specs += [pl.BlockSpec(sh, lambda i_: (0, 0)) for sh in pshapes]
    out_shape += [SDS(sh, F32) for sh in pshapes]
    res = pl.pallas_call(
        body, grid=(s // tile,), in_specs=list(specs) + [_par_spec(p) for p in params] + list(cspecs),
        out_specs=out_specs, out_shape=out_shape,
        compiler_params=_cparams(("arbitrary",)), name=name)(*arrs, *params, *carrs)
    return res[:len(want)], res[len(want):]


def _rms(x, w):
    return x * lax.rsqrt(jnp.mean(x * x, axis=-1, keepdims=True) + RMS_EPS) * w


def _sigmoid(x):
    return jax.nn.sigmoid(x)


def _softplus(x):
    return jnp.maximum(x, 0.0) + jnp.log1p(jnp.exp(-jnp.abs(x)))


def f_norm1(x, w):
    return (_rms(x, w),)


def f_dt(raw, bias):
    return (_softplus(raw + bias),)


def f_merge(gp_ssd, gp_sb, yb_ssd, yb_sb, b_ssd, b_sb):
    return (_sigmoid(gp_ssd + b_ssd) * yb_ssd + _sigmoid(gp_sb + b_sb) * yb_sb,)


def f_mix_out(x, mo, w_post, w_pre):
    h1 = x + _rms(mo, w_post)
    return h1, _rms(h1, w_pre)


def f_relu2(a1):
    return (jnp.square(jnp.maximum(a1, 0.0)),)


def f_ffn_out(h1, ff, w):
    return (h1 + _rms(ff, w),)


def f_ple_loss(h2, pg, pe, tgt, w):
    h3 = h2 + _rms(_sigmoid(pg) * pe, w)
    return (0.5 * jnp.square(h3 - tgt) * (1.0 / h2.shape[-1]),)


def _shift_down(u, d, rows):
    return u if d == 0 else jnp.where(rows >= d, pltpu.roll(u, d, 0), 0.0)


def _shift_up(u, d, rows):
    s = u.shape[0]
    return u if d == 0 else jnp.where(rows < s - d, pltpu.roll(u, s - d, 0), 0.0)


def conv_fwd(proj, col0, cd, conv_w, conv_b, name):
    s = proj.shape[0]
    cb0 = col0 // CONV_COLS

    def body(u_ref, w_ref, b_ref, o_ref):
        u = u_ref[...]
        rows = lax.broadcasted_iota(jnp.int32, u.shape, 0)
        y = jnp.broadcast_to(b_ref[...], u.shape)
        for k in range(CONV_K):
            y = y + w_ref[k:k + 1, :] * _shift_down(u, CONV_K - 1 - k, rows)
        o_ref[...] = y * _sigmoid(y)

    return pl.pallas_call(
        body, grid=(cd // CONV_COLS,),
        in_specs=[pl.BlockSpec((s, CONV_COLS), lambda i: (0, cb0 + i)),
                  pl.BlockSpec((CONV_K, CONV_COLS), lambda i: (0, i)),
                  pl.BlockSpec((1, CONV_COLS), lambda i: (0, i))],
        out_specs=pl.BlockSpec((s, CONV_COLS), lambda i: (0, i)),
        out_shape=SDS((s, cd), F32), compiler_params=_cparams(("parallel",)), name=name)(proj, conv_w, conv_b)


def conv_bwd(proj, col0, dout, ch0, conv_w, conv_b, dproj, name):
    s = proj.shape[0]
    ncb = dout.shape[1] // CONV_COLS
    cb0 = (col0 + ch0) // CONV_COLS
    wb0 = ch0 // CONV_COLS

    def body(u_ref, g_ref, w_ref, b_ref, _, du_ref, dw_ref, db_ref):
        u = u_ref[...]
        rows = lax.broadcasted_iota(jnp.int32, u.shape, 0)
        y = jnp.broadcast_to(b_ref[...], u.shape)
        for k in range(CONV_K):
            y = y + w_ref[k:k + 1, :] * _shift_down(u, CONV_K - 1 - k, rows)
        sg = _sigmoid(y)
        dy = g_ref[...] * (sg * (1.0 + y * (1.0 - sg)))
        du = jnp.zeros_like(u)
        for k in range(CONV_K):
            d = CONV_K - 1 - k
            du = du + w_ref[k:k + 1, :] * _shift_up(dy, d, rows)
            dw_ref[k:k + 1, :] = jnp.sum(dy * _shift_down(u, d, rows), axis=0, keepdims=True)
        du_ref[...] = du.astype(du_ref.dtype)
        db_ref[...] = jnp.sum(dy, axis=0, keepdims=True)

    return pl.pallas_call(
        body, grid=(ncb,),
        in_specs=[pl.BlockSpec((s, CONV_COLS), lambda i: (0, cb0 + i)),
                  pl.BlockSpec((s, CONV_COLS), lambda i: (0, i)),
                  pl.BlockSpec((CONV_K, CONV_COLS), lambda i: (0, wb0 + i)),
                  pl.BlockSpec((1, CONV_COLS), lambda i: (0, wb0 + i)),
                  pl.BlockSpec(memory_space=pl.ANY)],
        out_specs=[pl.BlockSpec((s, CONV_COLS), lambda i: (0, cb0 + i)),
                   pl.BlockSpec((CONV_K, CONV_COLS), lambda i: (0, i)),
                   pl.BlockSpec((1, CONV_COLS), lambda i: (0, i))],
        out_shape=[SDS(dproj.shape, dproj.dtype), SDS((CONV_K, ncb * CONV_COLS), F32), SDS((1, ncb * CONV_COLS), F32)],
        input_output_aliases={4: 0},
        compiler_params=_cparams(("parallel",)), name=name)(proj, dout, conv_w, conv_b, dproj)


def _dot(a, b, dims):
    return lax.dot_general(a.astype(BF16), b.astype(BF16), (dims, ((), ())), preferred_element_type=F32)


NN = ((1,), (0,))
NT = ((1,), (1,))
TN = ((0,), (0,))


def ssd_chunk(xs, zs, nw, dtc, dtw, alogs, dsks, bm, cm, prev):
    ln = bm.shape[0]
    gw = GROUP_HEADS * HEAD_DIM
    row = lax.broadcasted_iota(jnp.int32, (ln, ln), 0)
    col = lax.broadcasted_iota(jnp.int32, (ln, ln), 1)
    causal = row >= col
    tri = causal.astype(F32)
    tri_t = (row <= col).astype(F32)
    lane_head = lax.broadcasted_iota(jnp.int32, (1, gw), 1) // HEAD_DIM
    sub_head = lax.broadcasted_iota(jnp.int32, (gw, 1), 0) // HEAD_DIM
    on_lanes = [(lane_head == r).astype(F32) for r in range(GROUP_HEADS)]
    on_rows = [(sub_head == r).astype(F32) for r in range(GROUP_HEADS)]
    cb = _dot(cm, bm, NT)
    decays, dt_full, acs_full, end_full, dsk_full, end_rows = [], 0.0, 0.0, 0.0, 0.0, 0.0
    for r in range(GROUP_HEADS):
        a = -jnp.exp(alogs[r])
        da_c = dtc[r] * a
        da_w = dtw[r] * a
        acs_c = jnp.sum(tri * da_w, axis=1, keepdims=True)
        acs_w = jnp.sum(tri_t * da_c, axis=0, keepdims=True)
        alast = jnp.sum(da_w, axis=1, keepdims=True)
        decays.append(jnp.exp(jnp.where(causal, acs_c - acs_w, -jnp.inf)))
        dt_full = dt_full + dtc[r] * on_lanes[r]
        acs_full = acs_full + acs_c * on_lanes[r]
        end_full = end_full + alast * on_lanes[r]
        dsk_full = dsk_full + dsks[r] * on_lanes[r]
        end_rows = end_rows + alast * on_rows[r]
    xd = xs * dt_full
    y = xs * dsk_full + _dot(cm, prev, NT) * jnp.exp(acs_full)
    for r in range(GROUP_HEADS):
        y = y + _dot(cb * decays[r], xd * on_lanes[r], NN)
    new_prev = prev * jnp.exp(end_rows) + _dot(xd * jnp.exp(end_full - acs_full), bm, TN)
    yg = y * (zs * _sigmoid(zs))
    rstd = lax.rsqrt(jnp.mean(yg * yg, axis=-1, keepdims=True) + RMS_EPS)
    return yg * rstd * nw, new_prev


SSD_STEP_GROUPS = 1


def _ssd_specs(dm, cidx):
    u = SSD_STEP_GROUPS
    gw = GROUP_HEADS * HEAD_DIM
    nb0 = dm.DI // D_STATE
    assert nb0 % u == 0 and dm.G % u == 0
    par = pl.BlockSpec((u, 1, GROUP_HEADS), lambda g, c: (g, 0, 0))
    return dict(
        z=pl.BlockSpec((CHUNK, u * gw), lambda g, c: (cidx(c), g)),
        xs=pl.BlockSpec((CHUNK, u * gw), lambda g, c: (cidx(c), g)),
        b=pl.BlockSpec((CHUNK, u * D_STATE), lambda g, c: (cidx(c), nb0 // u + g)),
        c=pl.BlockSpec((CHUNK, u * D_STATE), lambda g, c: (cidx(c), (nb0 + dm.G) // u + g)),
        dtc=pl.BlockSpec((u, CHUNK, GROUP_HEADS), lambda g, c: (g, cidx(c), 0)),
        dtw=pl.BlockSpec((u, GROUP_HEADS, CHUNK), lambda g, c: (g, 0, cidx(c))),
        par=par,
        nw=pl.BlockSpec((1, u * gw), lambda g, c: (0, g)),
        st=pl.BlockSpec((u, None, gw, D_STATE), lambda g, c: (g, cidx(c), 0, 0)),
    )


def _ssd_load(k, z_ref, xs_ref, b_ref, c_ref, dtc_ref, dtw_ref, alog_ref, dsk_ref, nw_ref):
    gw = GROUP_HEADS * HEAD_DIM
    wide, narrow = slice(k * gw, (k + 1) * gw), slice(k * D_STATE, (k + 1) * D_STATE)
    dtc = tuple(dtc_ref[k, :, r:r + 1] for r in range(GROUP_HEADS))
    dtw = tuple(dtw_ref[k, r:r + 1, :] for r in range(GROUP_HEADS))
    alogs = tuple(alog_ref[k, :, r:r + 1] for r in range(GROUP_HEADS))
    dsks = tuple(dsk_ref[k, :, r:r + 1] for r in range(GROUP_HEADS))
    return xs_ref[:, wide], z_ref[:, wide], nw_ref[:, wide], dtc, dtw, alogs, dsks, b_ref[:, narrow], c_ref[:, narrow]


def ssd_fwd(proj, xbc, dtc, dtw, alog, dsk, nw, dm, name):
    nc = dm.S // CHUNK
    u = SSD_STEP_GROUPS
    gw = GROUP_HEADS * HEAD_DIM
    sp = _ssd_specs(dm, lambda c: c)

    def body(z_ref, xs_ref, b_ref, c_ref, dtc_ref, dtw_ref, alog_ref, dsk_ref, nw_ref, y_ref, st_ref, prev):
        @pl.when(pl.program_id(1) == 0)
        def _():
            prev[...] = jnp.zeros_like(prev)

        st_ref[...] = prev[...]
        for k in range(u):
            args = _ssd_load(k, z_ref, xs_ref, b_ref, c_ref, dtc_ref, dtw_ref, alog_ref, dsk_ref, nw_ref)
            out, new = ssd_chunk(*args, prev[k])
            y_ref[:, k * gw:(k + 1) * gw] = out.astype(y_ref.dtype)
            prev[k] = new

    return pl.pallas_call(
        body, grid=(dm.G // u, nc),
        in_specs=[sp["z"], sp["xs"], sp["b"], sp["c"], sp["dtc"], sp["dtw"], sp["par"], sp["par"], sp["nw"]],
        out_specs=[sp["xs"], sp["st"]],
        out_shape=[SDS((dm.S, dm.DI), BF16), SDS((dm.G, nc, gw, D_STATE), F32)],
        scratch_shapes=[pltpu.VMEM((u, gw, D_STATE), F32)],
        compiler_params=_cparams(("parallel", "arbitrary")), name=name)(proj, xbc, xbc, xbc, dtc, dtw, alog, dsk, nw)


def ssd_bwd(proj, xbc, dtc, dtw, alog, dsk, nw, states, dy, dm, name):
    nc = dm.S // CHUNK
    u = SSD_STEP_GROUPS
    sp = _ssd_specs(dm, lambda c: nc - 1 - c)
    gw = GROUP_HEADS * HEAD_DIM
    bc_spec = pl.BlockSpec((CHUNK, u * D_STATE), lambda g, c: (nc - 1 - c, g))

    def body(z_ref, xs_ref, b_ref, c_ref, dtc_ref, dtw_ref, alog_ref, dsk_ref, nw_ref, st_ref, dy_ref,
             dz_ref, dxs_ref, db_ref, dc_ref, ddtc_ref, ddtw_ref, dalog_ref, ddsk_ref, dnw_ref, dprev):
        first = pl.program_id(1) == 0

        @pl.when(first)
        def _():
            dprev[...] = jnp.zeros_like(dprev)

        param_grads = []
        for k in range(u):
            wide, narrow = slice(k * gw, (k + 1) * gw), slice(k * D_STATE, (k + 1) * D_STATE)
            args = _ssd_load(k, z_ref, xs_ref, b_ref, c_ref, dtc_ref, dtw_ref, alog_ref, dsk_ref, nw_ref)
            _, vjp = jax.vjp(ssd_chunk, *args, st_ref[k])
            gxs, gzs, gnw, gdtc, gdtw, galogs, gdsks, gb, gc, gprev = vjp((dy_ref[:, wide], dprev[k]))
            dxs_ref[:, wide] = gxs
            dz_ref[:, wide] = gzs.astype(dz_ref.dtype)
            db_ref[:, narrow] = gb
            dc_ref[:, narrow] = gc
            dprev[k] = gprev
            for r in range(GROUP_HEADS):
                ddtc_ref[k, :, r:r + 1] = gdtc[r]
                ddtw_ref[k, r:r + 1, :] = gdtw[r]
            param_grads.append((wide, gnw, galogs, gdsks))

        @pl.when(first)
        def _():
            for k, (wide, gnw, galogs, gdsks) in enumerate(param_grads):
                dnw_ref[:, wide] = gnw
                for r in range(GROUP_HEADS):
                    dalog_ref[k, :, r:r + 1] = galogs[r]
                    ddsk_ref[k, :, r:r + 1] = gdsks[r]

        @pl.when(jnp.logical_not(first))
        def _():
            for k, (wide, gnw, galogs, gdsks) in enumerate(param_grads):
                dnw_ref[:, wide] += gnw
                for r in range(GROUP_HEADS):
                    dalog_ref[k, :, r:r + 1] += galogs[r]
                    ddsk_ref[k, :, r:r + 1] += gdsks[r]

    xs_out = pl.BlockSpec((CHUNK, u * gw), lambda g, c: (nc - 1 - c, g))
    return pl.pallas_call(
        body, grid=(dm.G // u, nc),
        in_specs=[sp["z"], sp["xs"], sp["b"], sp["c"], sp["dtc"], sp["dtw"], sp["par"], sp["par"], sp["nw"],
                  sp["st"], xs_out],
        out_specs=[xs_out, xs_out, bc_spec, bc_spec, sp["dtc"], sp["dtw"], sp["par"], sp["par"], sp["nw"]],
        out_shape=[SDS((dm.S, dm.NA), BF16), SDS((dm.S, dm.DI), F32), SDS((dm.S, dm.G * D_STATE), F32),
                   SDS((dm.S, dm.G * D_STATE), F32), SDS((dm.G, dm.S, GROUP_HEADS), F32), SDS((dm.G, GROUP_HEADS, dm.S), F32),
                   SDS((dm.G, 1, GROUP_HEADS), F32), SDS((dm.G, 1, GROUP_HEADS), F32), SDS((1, dm.DI), F32)],
        scratch_shapes=[pltpu.VMEM((u, gw, D_STATE), F32)],
        compiler_params=_cparams(("parallel", "arbitrary")), name=name)(
            proj, xbc, xbc, xbc, dtc, dtw, alog, dsk, nw, states, dy)


def _split_bf16(v):
    hi = v.astype(BF16)
    return hi, (v - hi.astype(F32)).astype(BF16)


def _tri(v, mat):
    hi, lo = _split_bf16(v)
    return jnp.dot(hi, mat, preferred_element_type=F32) + jnp.dot(lo, mat, preferred_element_type=F32)


def _blocks(v):
    return [v[:, b * ATT_TILE:(b + 1) * ATT_TILE] for b in range(v.shape[1] // ATT_TILE)]


def _sb_group(z, mask, run, after_mat):
    sp = jnp.maximum(z, 0.0) + jnp.log2(1.0 + jnp.exp2(-jnp.abs(z)))
    lk = -sp if mask is None else jnp.where(mask, -sp, 0.0)
    cums = [_tri(v, after_mat) for v in _blocks(lk)]
    sums = [jnp.sum(v, axis=1, keepdims=True) for v in _blocks(lk)]
    later = [None] * len(cums)
    for b in reversed(range(len(cums))):
        later[b] = run + cums[b]
        run = run + sums[b]
    ls = z - sp
    w = jnp.exp2(ls + jnp.concatenate(later, axis=1))
    if mask is not None:
        w = jnp.where(mask, w, 0.0)
    return ls, w, run


def _alive(run_a, run_b):
    return (jnp.max(jnp.maximum(run_a, run_b)) > ATT_DEAD).astype(jnp.int32)


def _window(i, jj, t):
    gw = ATT_GROUP * t
    end = (i + 1 - ATT_GROUP * jj) * t
    r0 = pl.multiple_of(jnp.maximum(end - gw, 0), t)
    rows = i * t + lax.broadcasted_iota(jnp.int32, (t, gw), 0)
    cols = r0 + lax.broadcasted_iota(jnp.int32, (t, gw), 1)
    return r0, jnp.logical_and(cols < rows, cols < end)


def _att_specs(dm, s):
    t = ATT_TILE
    qb, kb, vb = dm.off["q"] // LANES, dm.off["k"] // LANES, dm.off["v"] // LANES
    return (pl.BlockSpec((t, LANES), lambda p, i: (i, qb + p)),
            pl.BlockSpec((s, LANES), lambda p, i: (0, kb + p)),
            pl.BlockSpec((s, LANES), lambda p, i: (0, vb + p)))


def attn_fwd(proj, dm, name, rider=None):
    s, t = dm.S, ATT_TILE
    scale = HEAD_DIM ** -0.5
    hsl = [slice(h * HEAD_DIM, (h + 1) * HEAD_DIM) for h in range(2)]

    gw = ATT_GROUP * t

    def body(q_ref, k_ref, v_ref, o_ref):
        i = pl.program_id(1)
        gd = i // ATT_GROUP
        r_io = lax.broadcasted_iota(jnp.int32, (t, t), 0)
        c_io = lax.broadcasted_iota(jnp.int32, (t, t), 1)
        after_mat = (r_io > c_io).astype(BF16)
        qs = [q_ref[:, sl].astype(BF16) for sl in hsl]

        def group(jj, carry):
            r0, mask = _window(i, jj, t)
            zs = [_dot(qs[h], k_ref[pl.ds(r0, gw), hsl[h]], NT) * (scale * LOG2E) for h in range(2)]
            res = [_sb_group(zs[h], mask, carry[h][0], after_mat) for h in range(2)]
            return tuple((res[h][2], carry[h][1] + _dot(res[h][1], v_ref[pl.ds(r0, gw), hsl[h]], NN)) for h in range(2))

        zero = (jnp.zeros((t, 1), F32), jnp.zeros((t, HEAD_DIM), F32))
        carry = group(0, (zero, zero))

        def step(st):
            jj, _, c = st
            c = group(jj, c)
            return jj + 1, _alive(c[0][0], c[1][0]), c

        _, _, carry = lax.while_loop(lambda st: jnp.logical_and(st[0] <= gd, st[1] > 0), step,
                                     (jnp.int32(1), _alive(carry[0][0], carry[1][0]), carry))
        for h in range(2):
            o_ref[:, hsl[h]] = carry[h][1]

    qs_, ks_, vs_ = _att_specs(dm, s)
    res = _grid_call(body, grid=(dm.SBW // LANES, s // t), in_specs=[qs_, ks_, vs_],
                     out_specs=[pl.BlockSpec((t, LANES), lambda p, i: (i, p))], out_shape=[SDS((s, dm.SBW), F32)],
                     scratch=[], operands=(proj, proj, proj), name=name, rider=rider)
    return res[0] if rider is None else res


def attn_bwd(proj, do, dm, name, rider=None):
    s, t = dm.S, ATT_TILE
    nq = s // t
    gw = ATT_GROUP * t
    n_win = (nq - 1) // ATT_GROUP + 1
    scale = HEAD_DIM ** -0.5
    hsl = [slice(h * HEAD_DIM, (h + 1) * HEAD_DIM) for h in range(2)]

    def body(q_ref, k_ref, v_ref, do_ref, dq_ref, dk_ref, dv_ref, dk_acc, dv_acc, g_scr, s_scr):
        i = pl.program_id(1)

        @pl.when(i == 0)
        def _():
            dk_acc[...] = jnp.zeros_like(dk_acc)
            dv_acc[...] = jnp.zeros_like(dv_acc)

        gd = i // ATT_GROUP
        r_io = lax.broadcasted_iota(jnp.int32, (t, t), 0)
        c_io = lax.broadcasted_iota(jnp.int32, (t, t), 1)
        after_mat = (r_io > c_io).astype(BF16)
        before_mat = (r_io < c_io).astype(BF16)
        qs = [q_ref[:, sl].astype(BF16) for sl in hsl]
        dos = [do_ref[:, sl].astype(BF16) for sl in hsl]
        q_t = q_ref[...].T.astype(BF16)
        do_t = do_ref[...].T.astype(BF16)

        def pass1(jj, runs):
            r0, mask = _window(i, jj, t)
            zs = [_dot(qs[h], k_ref[pl.ds(r0, gw), hsl[h]], NT) * (scale * LOG2E) for h in range(2)]
            dws = [_dot(dos[h], v_ref[pl.ds(r0, gw), hsl[h]], NT) for h in range(2)]
            out = []
            for h in range(2):
                ls, w, run = _sb_group(zs[h], mask, runs[h], after_mat)
                g_scr[h, jj] = dws[h] * w
                s_scr[h, jj] = jnp.exp2(ls)
                dv_acc[hsl[h], pl.ds(r0, gw)] += _dot(do_t[hsl[h]], w, NN)
                out.append(run)
            return tuple(out)

        zero_col = jnp.zeros((t, 1), F32)
        runs = pass1(0, (zero_col, zero_col))

        def step1(st):
            jj, _, r = st
            r = pass1(jj, r)
            return jj + 1, _alive(r[0], r[1]), r

        walked, _, _ = lax.while_loop(lambda st: jnp.logical_and(st[0] <= gd, st[1] > 0), step1,
                                      (jnp.int32(1), _alive(runs[0], runs[1]), runs))

        def pass2(jj, carry):
            r0, mask = _window(i, jj, t)
            out = []
            for h in range(2):
                pre, dq = carry[h]
                gg = g_scr[h, jj]
                sig = s_scr[h, jj]
                before = []
                for v in _blocks(gg):
                    before.append(pre + _tri(v, before_mat))
                    pre = pre + jnp.sum(v, axis=1, keepdims=True)
                dz = jnp.where(mask, gg * (1.0 - sig) - jnp.concatenate(before, axis=1) * sig, 0.0)
                dz = (dz * scale).astype(BF16)
                dk_acc[hsl[h], pl.ds(r0, gw)] += _dot(q_t[hsl[h]], dz, NN)
                out.append((pre, dq + _dot(dz, k_ref[pl.ds(r0, gw), hsl[h]], NN)))
            return tuple(out)

        zero = (zero_col, jnp.zeros((t, HEAD_DIM), F32))
        carry = lax.fori_loop(1, walked, lambda n, c: pass2(walked - n, c), (zero, zero))
        carry = pass2(0, carry)
        for h in range(2):
            dq_ref[:, hsl[h]] = carry[h][1].astype(dq_ref.dtype)

        @pl.when(i == nq - 1)
        def _():
            for b in range(nq):
                rows = slice(b * t, (b + 1) * t)
                dk_ref[rows, :] = dk_acc[:, rows].T.astype(dk_ref.dtype)
                dv_ref[rows, :] = dv_acc[:, rows].T.astype(dv_ref.dtype)

    qs_, ks_, vs_ = _att_specs(dm, s)
    tile_spec = pl.BlockSpec((t, LANES), lambda p, i: (i, p))
    full_spec = pl.BlockSpec((s, LANES), lambda p, i: (0, p))
    return _grid_call(
        body, grid=(dm.SBW // LANES, nq), in_specs=[qs_, ks_, vs_, tile_spec],
        out_specs=[tile_spec, full_spec, full_spec], out_shape=[SDS((s, dm.SBW), BF16)] * 3,
        scratch=[pltpu.VMEM((LANES, s), F32), pltpu.VMEM((LANES, s), F32),
                 pltpu.VMEM((2, n_win, t, gw), F32), pltpu.VMEM((2, n_win, t, gw), F32)],
        operands=(proj, proj, proj, do), name=name, rider=rider)


def adamw(w, g, m, v, name):
    rows, width = w.shape
    tile = _pick(rows, (256, 64, 16, 8, 4))
    c1 = 1.0 / (1.0 - ADAM_B1 ** ADAM_STEP)
    c2 = 1.0 / (1.0 - ADAM_B2 ** ADAM_STEP)

    def body(w_ref, g_ref, m_ref, v_ref, d_ref, nm_ref, nv_ref):
        gg = g_ref[...]
        nm = ADAM_B1 * m_ref[...] + (1.0 - ADAM_B1) * gg
        nv = ADAM_B2 * v_ref[...] + (1.0 - ADAM_B2) * (gg * gg)
        d_ref[...] = -ADAM_LR * ((nm * c1) / (jnp.sqrt(nv * c2) + ADAM_EPS) + ADAM_WD * w_ref[...])
        nm_ref[...] = nm
        nv_ref[...] = nv

    spec = pl.BlockSpec((tile, width), lambda i: (i, 0))
    return pl.pallas_call(
        body, grid=(rows // tile,), in_specs=[spec] * 4, out_specs=[spec] * 3,
        out_shape=[SDS((rows, width), F32)] * 3, compiler_params=_cparams(("parallel",)), name=name)(w, g, m, v)


def _me():
    return lax.axis_index("x"), lax.axis_index("y"), lax.axis_index("c")


def _other_chips(x, y):
    return [(1 - x, y), (x, 1 - y), (1 - x, 1 - y)]


def gather_weights(wp):
    rows, width = wp.shape
    half = rows // 2

    def body(w_ref, out_ref, send_sems, recv_sems):
        x, y, c = _me()
        sibling = (x, y, 1 - c)
        chips = _other_chips(x, y)

        def part(cx, cy, hf):
            return out_ref.at[2 * cx + cy, hf]

        def copy(k, src, dst, to):
            return pltpu.make_async_remote_copy(src_ref=src, dst_ref=dst, send_sem=send_sems.at[k], recv_sem=recv_sems.at[k],
                                                device_id=to, device_id_type=MESH_ID)

        first = [copy(j, w_ref.at[c], part(x, y, c), (cx, cy, c)) for j, (cx, cy) in enumerate(chips)]
        for cp in first:
            cp.start()
        passed = [copy(3 + j, part(cx, cy, c), part(cx, cy, c), sibling) for j, (cx, cy) in enumerate(chips)]
        for j, (cx, cy) in enumerate(chips):
            copy(j, part(cx, cy, c), part(cx, cy, c), (x, y, c)).wait_recv()
            passed[j].start()
        for j, (cx, cy) in enumerate(chips):
            copy(3 + j, part(cx, cy, 1 - c), part(cx, cy, 1 - c), (x, y, c)).wait_recv()
        for cp in first + passed:
            cp.wait_send()

    return pl.pallas_call(
        body, out_shape=SDS((N_CHIPS, 2, half, width), wp.dtype), in_specs=[HBM_SPEC], out_specs=HBM_SPEC,
        scratch_shapes=[pltpu.SemaphoreType.DMA((6,)), pltpu.SemaphoreType.DMA((6,))],
        name="gather_weights")(wp.reshape(2, half, width)).reshape(N_CHIPS, rows, width)


class Rider(NamedTuple):
    operands: tuple
    out_shape: tuple
    scratch: tuple
    start: object
    wait: object


def _with_rider(body, rider, grid, n_in, n_out):
    if rider is None:
        return body
    r_in, r_out = len(rider.operands), len(rider.out_shape)

    def full(*refs):
        ins, refs = refs[:n_in], refs[n_in:]
        rins, refs = refs[:r_in], refs[r_in:]
        outs, refs = refs[:n_out], refs[n_out:]
        routs, refs = refs[:r_out], refs[r_out:]
        scr, rscr = refs[:len(refs) - len(rider.scratch)], refs[len(refs) - len(rider.scratch):]
        ids = [pl.program_id(k) for k in range(len(grid))]
        first = functools.reduce(jnp.logical_and, [i == 0 for i in ids])
        last = functools.reduce(jnp.logical_and, [i == g - 1 for i, g in zip(ids, grid)])

        @pl.when(first)
        def _():
            rider.start(rins, routs, rscr)

        body(*ins, *outs, *scr)

        @pl.when(last)
        def _():
            rider.wait(rins, routs, rscr)

    return full


def gather_rider(wp):
    def copies(ins, outs, scr, sending):
        (w_ref,), (o_ref,), (send_sems, recv_sems) = ins, outs, scr
        x, y, c = _me()
        return [pltpu.make_async_remote_copy(src_ref=w_ref, dst_ref=o_ref.at[2 * x + y if sending else 2 * cx + cy],
                                             send_sem=send_sems.at[j], recv_sem=recv_sems.at[j], device_id=(cx, cy, c),
                                             device_id_type=MESH_ID)
                for j, (cx, cy) in enumerate(_other_chips(x, y))]

    def start(ins, outs, scr):
        for cp in copies(ins, outs, scr, True):
            cp.start()

    def wait(ins, outs, scr):
        for cp in copies(ins, outs, scr, False):
            cp.wait()

    return Rider((wp,), (SDS((N_CHIPS,) + wp.shape, wp.dtype),),
                 (pltpu.SemaphoreType.DMA((3,)), pltpu.SemaphoreType.DMA((3,))), start, wait)


def own_slab(gathered, wp, chip_idx):
    return lax.dynamic_update_slice(gathered, wp[None], (chip_idx[0], 0, 0))


def exchange_rider(sh):
    def copies(ins, outs, scr):
        (s_ref,), (b_ref,), (send_sems, recv_sems) = ins, outs, scr
        x, y, c = _me()
        return [pltpu.make_async_remote_copy(src_ref=s_ref.at[2 * cx + cy], dst_ref=b_ref.at[j], send_sem=send_sems.at[j],
                                             recv_sem=recv_sems.at[j], device_id=(cx, cy, c), device_id_type=MESH_ID)
                for j, (cx, cy) in enumerate(_other_chips(x, y))]

    def start(ins, outs, scr):
        for cp in copies(ins, outs, scr):
            cp.start()

    def wait(ins, outs, scr):
        for cp in copies(ins, outs, scr):
            cp.wait()

    return Rider((sh,), (SDS((3,) + sh.shape[1:], sh.dtype),),
                 (pltpu.SemaphoreType.DMA((3,)), pltpu.SemaphoreType.DMA((3,))), start, wait)


def swap_halves(g, name):
    n, rows, width = g.shape
    half = rows // 2

    def body(g_ref, a_ref, send_sem, recv_sem):
        x, y, c = _me()
        cp = pltpu.make_async_remote_copy(src_ref=g_ref.at[:, pl.ds((1 - c) * half, half), :], dst_ref=a_ref,
                                          send_sem=send_sem, recv_sem=recv_sem, device_id=(x, y, 1 - c), device_id_type=MESH_ID)
        cp.start()
        cp.wait()

    return pl.pallas_call(
        body, out_shape=SDS((n, half, width), g.dtype), in_specs=[HBM_SPEC], out_specs=HBM_SPEC,
        scratch_shapes=[pltpu.SemaphoreType.DMA, pltpu.SemaphoreType.DMA], name=name)(g)


def add_half(g, a, c_idx, name):
    n, rows, width = g.shape
    half = rows // 2
    tile = half // 2
    nt = half // tile

    def body(c_ref, g_ref, a_ref, o_ref, ob_ref):
        v = g_ref[...] + a_ref[...]
        o_ref[...] = v
        ob_ref[...] = v.astype(ob_ref.dtype)

    out_spec = pl.BlockSpec((None, tile, width), lambda s, i, c_ref: (s, i, 0))
    gs = pltpu.PrefetchScalarGridSpec(
        num_scalar_prefetch=1, grid=(n, nt),
        in_specs=[pl.BlockSpec((None, tile, width), lambda s, i, c_ref: (s, c_ref[0] * nt + i, 0)), out_spec],
        out_specs=[out_spec, out_spec])
    return pl.pallas_call(body, grid_spec=gs, out_shape=[SDS((n, half, width), F32), SDS((n, half, width), BF16)],
                          compiler_params=_cparams(("parallel", "parallel")), name=name)(c_idx, g, a)


def exchange_small(small):
    def body(sm_ref, all_ref, send_sems, recv_sems, local_sem):
        x, y, c = _me()
        mine = pltpu.make_async_copy(sm_ref, all_ref.at[0], local_sem)
        mine.start()
        copies = []
        for m in range(1, N_DEV):
            peer = (x ^ ((m >> 2) & 1), y ^ ((m >> 1) & 1), c ^ (m & 1))
            copies.append(pltpu.make_async_remote_copy(
                src_ref=sm_ref, dst_ref=all_ref.at[m], send_sem=send_sems.at[m - 1], recv_sem=recv_sems.at[m - 1],
                device_id=peer, device_id_type=MESH_ID))
        for cp in copies:
            cp.start()
        for cp in copies:
            cp.wait()
        mine.wait()

    return pl.pallas_call(
        body, out_shape=SDS((N_DEV,) + small.shape, small.dtype), in_specs=[HBM_SPEC], out_specs=HBM_SPEC,
        scratch_shapes=[pltpu.SemaphoreType.DMA((N_DEV - 1,)), pltpu.SemaphoreType.DMA((N_DEV - 1,)), pltpu.SemaphoreType.DMA],
        name="exchange_small")(small)


def add_chips(sh, b, k_idx, name):
    n, hf, width = sh.shape
    tile = hf // 2

    def body(k_ref, s_ref, b0, b1, b2, o_ref):
        o_ref[...] = ((s_ref[...] + b0[...].astype(F32)) + b1[...].astype(F32)) + b2[...].astype(F32)

    def bspec(j):
        return pl.BlockSpec((None, tile, width), lambda i, k_ref, j=j: (j, i, 0))

    gs = pltpu.PrefetchScalarGridSpec(
        num_scalar_prefetch=1, grid=(hf // tile,),
        in_specs=[pl.BlockSpec((None, tile, width), lambda i, k_ref: (k_ref[0], i, 0)), bspec(0), bspec(1), bspec(2)],
        out_specs=pl.BlockSpec((tile, width), lambda i, k_ref: (i, 0)))
    return pl.pallas_call(body, grid_spec=gs, out_shape=SDS((hf, width), sh.dtype),
                          compiler_params=_cparams(("parallel",)), name=name)(k_idx, sh, b, b, b)


def sum_small(allsm, me_idx, name):
    _, rows, width = allsm.shape

    def body(me_ref, a_ref, o_ref):
        me = me_ref[0]
        acc = a_ref[me]
        for dev in range(1, N_DEV):
            acc = acc + a_ref[jnp.bitwise_xor(me, dev)]
        o_ref[...] = acc

    gs = pltpu.PrefetchScalarGridSpec(
        num_scalar_prefetch=1, grid=(1,),
        in_specs=[pl.BlockSpec((N_DEV, rows, width), lambda i, me_ref: (0, 0, 0))],
        out_specs=pl.BlockSpec((rows, width), lambda i, me_ref: (0, 0)))
    return pl.pallas_call(body, grid_spec=gs, out_shape=SDS((rows, width), allsm.dtype),
                          compiler_params=_cparams(("arbitrary",)), name=name)(me_idx, allsm)


def join_halves(t, core_idx, name):
    hf, width = t.shape

    def body(t_ref, o_ref, send_sem, recv_sem):
        x, y, c = _me()
        cp = pltpu.make_async_remote_copy(src_ref=t_ref, dst_ref=o_ref, send_sem=send_sem, recv_sem=recv_sem,
                                          device_id=(x, y, 1 - c), device_id_type=MESH_ID)
        cp.start()
        cp.wait()

    theirs = pl.pallas_call(
        body, out_shape=SDS((hf, width), t.dtype), in_specs=[HBM_SPEC], out_specs=HBM_SPEC,
        scratch_shapes=[pltpu.SemaphoreType.DMA, pltpu.SemaphoreType.DMA], name=name)(t)
    return jnp.where(core_idx[0] == 0, jnp.concatenate([t, theirs], axis=0), jnp.concatenate([theirs, t], axis=0))


ROW_PAD = 64


def _pad_rows(rows):
    return -(-rows // ROW_PAD) * ROW_PAD


def _pack_rows(names, shard_shapes, width):
    return _pad_rows(sum((shard_shapes[n][0] * shard_shapes[n][1]) // width for n in names))


def unpack_local(packed, names, shard_shapes, width):
    out, r0 = {}, 0
    for n in names:
        a, b = shard_shapes[n]
        nr = (a * b) // width
        out[n] = packed[r0:r0 + nr].reshape(a, b)
        r0 += nr
    return out


EXACT_IN_GATHER = ("conv_w",)
EXACT_TERMS = 3


def pack_gather(shards, names, width):
    parts = []
    for n in names:
        if n in EXACT_IN_GATHER:
            rest = shards[n].astype(F32)
            for _ in range(EXACT_TERMS):
                term = rest.astype(BF16)
                parts.append(term.reshape(-1, width))
                rest = rest - term.astype(F32)
        else:
            parts.append(shards[n].reshape(-1, width).astype(BF16))
    used = sum(p.shape[0] for p in parts)
    parts.append(jnp.zeros((_pad_rows(used) - used, width), BF16))
    return jnp.concatenate(parts, axis=0)


def unpack_full(gathered, names, shard_shapes, width):
    out, r0 = {}, 0
    for n in names:
        a, b = shard_shapes[n]
        terms = EXACT_TERMS if n in EXACT_IN_GATHER else 1
        nr = (a * b) // width
        pieces = []
        for j in range(N_CHIPS):
            blk = gathered[j, r0:r0 + nr].reshape(a, b)
            for t in range(1, terms):
                blk = blk.astype(F32) + gathered[j, r0 + t * nr:r0 + (t + 1) * nr].reshape(a, b).astype(F32)
            pieces.append(blk)
        out[n] = jnp.concatenate(pieces, axis=SHARD_AXIS[n])
        r0 += terms * nr
    return out


def pack_full(grads, names, shard_shapes, width, pad=True):
    total_rows = _pack_rows(names, shard_shapes, width) if pad else sum(
        (shard_shapes[n][0] * shard_shapes[n][1]) // width for n in names)
    slabs = []
    for j in range(N_CHIPS):
        parts = []
        for n in names:
            a, b = shard_shapes[n]
            ax = SHARD_AXIS[n]
            sz = (a, b)[ax]
            piece = lax.slice_in_dim(grads[n], j * sz, (j + 1) * sz, axis=ax)
            parts.append(piece.reshape(-1, width))
        used = sum(p.shape[0] for p in parts)
        parts.append(jnp.zeros((total_rows - used, width), F32))
        slabs.append(jnp.concatenate(parts, axis=0))
    return jnp.stack(slabs, axis=0)


def _small_layout(sizes, width):
    lay, r = {}, 0
    for n in SMALL_WEIGHTS:
        nr = -(-sizes[n] // width)
        lay[n] = (r, nr, sizes[n])
        r += nr
    assert r <= SMALL_ROWS
    return lay


def pack_small(vals, lay, width):
    rows = []
    for n in SMALL_WEIGHTS:
        r, nr, sz = lay[n]
        v = vals[n].reshape(-1).astype(F32)
        rows.append(jnp.pad(v, (0, nr * width - sz)).reshape(nr, width))
    used = sum(r.shape[0] for r in rows)
    rows.append(jnp.zeros((SMALL_ROWS - used, width), F32))
    return jnp.concatenate(rows, axis=0)


def unpack_small(packed, lay):
    return {n: packed[r:r + nr].reshape(-1)[:sz].reshape(1, sz) for n, (r, nr, sz) in lay.items()}


PART_B = ("w_in", "w_gate")
PART_A = ("w_ff1", "w_ff2", "w_ssd_branch", "w_sb_branch", "w_out", "w_ple_gate", "w_ple", "conv_w")
DIRECT_A = PART_A[:6]


class Dist(NamedTuple):
    packed_a: object
    shard_shapes: dict
    core_idx: object
    chip_idx: object


def build_w_all(w_in, w_gate, dm):
    c0 = dm.DI + dm.CD
    return jnp.concatenate(
        [w_in[:, :c0], w_in[:, c0 + dm.H:], w_gate, w_in[:, c0:c0 + dm.H],
         jnp.zeros((dm.D, DT_PAD - dm.H), w_in.dtype)], axis=1).astype(BF16)


def pack_b(w_in_shard, w_gate_shard):
    return jnp.concatenate([w_in_shard, w_gate_shard], axis=1)


def w_all_from_slabs(slabs, dm):
    n_in = (dm.NA - DT_PAD - 2 * dm.D + dm.H) // N_CHIPS
    c0 = dm.DI + dm.CD

    def w_in_cols(a, b):
        return [slabs[j, :, max(a, j * n_in) - j * n_in:min(b, (j + 1) * n_in) - j * n_in]
                for j in range(N_CHIPS) if max(a, j * n_in) < min(b, (j + 1) * n_in)]

    parts = w_in_cols(0, c0) + w_in_cols(c0 + dm.H, N_CHIPS * n_in) + [slabs[j, :, n_in:] for j in range(N_CHIPS)]
    parts += w_in_cols(c0, c0 + dm.H) + [jnp.zeros((dm.D, DT_PAD - dm.H), slabs.dtype)]
    return jnp.concatenate(parts, axis=1).astype(BF16)


def _w_in_columns(lo, hi, dm):
    c0 = dm.DI + dm.CD
    segments = [(0, c0, 0), (c0, c0 + dm.H, dm.off["dt"]), (c0 + dm.H, c0 + dm.H + 3 * dm.SBW, c0)]
    return [(w0 + max(lo, a) - a, w0 + min(hi, b) - a) for a, b, w0 in segments if max(lo, a) < min(hi, b)]


def slabs_b(dw_all, dm):
    n_in = (dm.NA - DT_PAD - 2 * dm.D + dm.H) // N_CHIPS
    n_gate = 2 * dm.D // N_CHIPS
    slabs = []
    for j in range(N_CHIPS):
        cols = _w_in_columns(j * n_in, (j + 1) * n_in, dm) + [(dm.off["gate"] + j * n_gate, dm.off["gate"] + (j + 1) * n_gate)]
        slabs.append(jnp.concatenate([dw_all[:, a:b] for a, b in cols], axis=1))
    return jnp.stack(slabs, axis=0)


def local_step(x, p, tgt, wf, sm, dm, dist=None):
    s, d = dm.S, dm.D
    off = dm.off
    c0 = dm.DI + dm.CD
    w_all = wf["w_all"] if "w_all" in wf else build_w_all(wf["w_in"], wf["w_gate"], dm)
    g = dm.G
    per_group = lambda v: v.reshape(g, 1, GROUP_HEADS)
    alog, dsk = per_group(sm["a_log"]), per_group(sm["d_skip"])
    b_gate = sm["b_gate"]
    b_ssd, b_sb = b_gate[:, :d], b_gate[:, d:]
    gcol = off["gate"] // d

    (n1,) = row_fwd("norm1", f_norm1, [x], [sm["norm_mix_pre"]], [(d, BF16)])
    proj = matmul(n1, w_all, name="in_proj")
    if dist is None:
        y_sb = attn_fwd(proj, dm, "attn_fwd")
    else:
        y_sb, gathered_a = attn_fwd(proj, dm, "attn_fwd", rider=gather_rider(dist.packed_a))
        gathered_a = own_slab(gathered_a, dist.packed_a, dist.chip_idx)
        wf = {**wf, **unpack_full(gathered_a, PART_A, dist.shard_shapes, d)}
    xbc = conv_fwd(proj, off["xbc"], dm.CD, wf["conv_w"].astype(F32), sm["conv_b"], "conv_fwd")
    dt_raw = proj[:, off["dt"]:off["dt"] + dm.H]
    (dt,) = row_fwd("dt", f_dt, [dt_raw], [sm["dt_bias"]], [(dm.H, F32)])
    dtc = dt.reshape(s, g, GROUP_HEADS).transpose(1, 0, 2)
    dtw = dt.reshape(s, g, GROUP_HEADS).transpose(1, 2, 0)
    y_ssd, states = ssd_fwd(proj, xbc, dtc, dtw, alog, dsk, sm["ssd_norm"], dm, "ssd_fwd")
    yb_ssd = matmul(y_ssd, wf["w_ssd_branch"], name="ssd_branch")
    yb_sb = matmul(y_sb, wf["w_sb_branch"], name="sb_branch")
    merge_rows = [(proj, d, gcol), (proj, d, gcol + 1), yb_ssd, yb_sb]
    (merged,) = row_fwd("merge", f_merge, merge_rows, [b_ssd, b_sb], [(d, BF16)])
    mo = matmul(merged, wf["w_out"], name="w_out")
    h1, n2 = row_fwd("mix_out", f_mix_out, [x, mo], [sm["norm_mix_post"], sm["norm_ffn_pre"]], [(d, F32), (d, BF16)])
    a1, act = matmul(n2, wf["w_ff1"], name="ff1", finish=lambda v: (v,) + f_relu2(v), out_dtype=(F32, BF16))
    ff = matmul(act, wf["w_ff2"], name="ff2")
    (h2,) = row_fwd("ffn_out", f_ffn_out, [h1, ff], [sm["norm_ffn_post"]], [(d, F32)])
    pg = matmul(h2, wf["w_ple_gate"], name="ple_gate")
    pe = matmul(p, wf["w_ple"], name="ple_emb")

    def reduce_start(gbig, tag):
        from_sibling = swap_halves(gbig, "swap_halves_" + tag)
        pair_sum, pair_sum_bf16 = add_half(gbig, from_sibling, dist.core_idx, "add_half_" + tag)
        return pair_sum, exchange_rider(pair_sum_bf16)

    def reduce_finish(pair_sum, from_chips, tag):
        my_half = add_chips(pair_sum, from_chips, dist.chip_idx, "add_chips_" + tag)
        return join_halves(my_half, dist.core_idx, "join_halves_" + tag)

    gr, reduced = {}, None
    packed_grads = [None if dist is None else jnp.zeros((N_CHIPS, _pack_rows(PART_A, dist.shard_shapes, d), d), F32)]

    def weight_grad(wname, a, dy):
        if dist is None:
            gr[wname] = matmul(a, dy, ta=True, name="d_" + wname)
            return
        row0 = 0
        for n in PART_A[:PART_A.index(wname)]:
            row0 += (dist.shard_shapes[n][0] * dist.shard_shapes[n][1]) // d
        packed_grads[0] = matmul(a, dy, ta=True, name="d_" + wname,
                                 slab=SlabOut(packed_grads[0], row0, dist.shard_shapes[wname][0], SHARD_AXIS[wname]))

    (dh2_a, dpg, dpe), (gr["norm_ple_post"], loss_cols) = row_bwd(
        "ple_loss", f_ple_loss, [h2, pg, pe, tgt], [sm["norm_ple_post"]], [None], [F32, BF16, BF16, None], primal_sum=True)
    loss = jnp.sum(loss_cols)
    gr["w_ple"] = matmul(p, dpe, ta=True, name="d_w_ple")
    weight_grad("w_ple_gate", h2, dpg)
    dh2_b = matmul(dpg, wf["w_ple_gate"], tb=True, name="d_h2")
    (dh1_a, dff), (gr["norm_ffn_post"],) = row_bwd(
        "ffn_out_bwd", f_ffn_out, [h1, ff], [sm["norm_ffn_post"]], [[dh2_a, dh2_b]], [F32, BF16])
    weight_grad("w_ff2", act, dff)
    (da1,) = matmul(dff, wf["w_ff2"], tb=True, name="d_act", sides=(a1,),
                    finish=lambda dact, a1_tile: (dact * (2.0 * jnp.maximum(a1_tile, 0.0)),), out_dtype=(BF16,))
    weight_grad("w_ff1", n2, da1)
    dn2 = matmul(da1, wf["w_ff1"], tb=True, name="d_n2")
    (dx_a, dmo), (gr["norm_mix_post"], gr["norm_ffn_pre"]) = row_bwd(
        "mix_out_bwd", f_mix_out, [x, mo], [sm["norm_mix_post"], sm["norm_ffn_pre"]], [[dh1_a], [dn2]], [F32, BF16])
    weight_grad("w_out", merged, dmo)
    dmerged = matmul(dmo, wf["w_out"], tb=True, name="d_merged")
    (dgp_ssd, dgp_sb, dyb_ssd, dyb_sb), (db_ssd, db_sb) = row_bwd(
        "merge_bwd", f_merge, merge_rows, [b_ssd, b_sb], [[dmerged]], [BF16, BF16, BF16, BF16])
    gr["b_gate"] = jnp.concatenate([db_ssd, db_sb], axis=1)
    weight_grad("w_ssd_branch", y_ssd, dyb_ssd)
    weight_grad("w_sb_branch", y_sb, dyb_sb)
    dy_ssd = matmul(dyb_ssd, wf["w_ssd_branch"], tb=True, name="d_y_ssd")
    dy_sb = matmul(dyb_sb, wf["w_sb_branch"], tb=True, name="d_y_sb")
    dproj, dxs, dbm, dcm, ddtc, ddtw, dalog, ddsk, gr["ssd_norm"] = ssd_bwd(
        proj, xbc, dtc, dtw, alog, dsk, sm["ssd_norm"], states, dy_ssd, dm, "ssd_bwd")
    gr["a_log"], gr["d_skip"] = (v.reshape(1, dm.H) for v in (dalog, ddsk))
    ddt_post = (ddtc.transpose(1, 0, 2) + ddtw.transpose(2, 0, 1)).reshape(s, dm.H)
    (ddt,), (gr["dt_bias"],) = row_bwd("dt_bwd", f_dt, [dt_raw], [sm["dt_bias"]], [[ddt_post]], [BF16])
    conv_w32 = wf["conv_w"].astype(F32)
    dproj, dw_x, dcb_x = conv_bwd(proj, off["xbc"], dxs, 0, conv_w32, sm["conv_b"], dproj, "conv_bwd_x")
    dproj, dw_b, dcb_b = conv_bwd(proj, off["xbc"], dbm, dm.DI, conv_w32, sm["conv_b"], dproj, "conv_bwd_b")
    dproj, dw_c, dcb_c = conv_bwd(proj, off["xbc"], dcm, dm.DI + g * D_STATE, conv_w32, sm["conv_b"], dproj, "conv_bwd_c")
    gr["conv_w"] = jnp.concatenate([dw_x, dw_b, dw_c], axis=1)
    gr["conv_b"] = jnp.concatenate([dcb_x, dcb_b, dcb_c], axis=1)
    if dist is None:
        dq, dk, dv = attn_bwd(proj, dy_sb, dm, "attn_bwd")
    else:
        rest = tuple(n for n in PART_A if n not in DIRECT_A)
        row0 = sum((dist.shard_shapes[n][0] * dist.shard_shapes[n][1]) // d for n in DIRECT_A)
        small_rows = pack_full(gr, rest, dist.shard_shapes, d, pad=False)
        gbig_a = lax.dynamic_update_slice(packed_grads[0], small_rows, (0, row0, 0))
        pair_sum_a, rider_a = reduce_start(gbig_a, "a")
        dq, dk, dv, from_chips_a = attn_bwd(proj, dy_sb, dm, "attn_bwd", rider=rider_a)
        reduced = unpack_local(reduce_finish(pair_sum_a, from_chips_a, "a"), PART_A, dist.shard_shapes, d)
    col = off["q"]
    for piece in (dq, dk, dv, dgp_ssd, dgp_sb, jnp.concatenate([ddt, jnp.zeros((s, DT_PAD - dm.H), BF16)], axis=1)):
        dproj = lax.dynamic_update_slice(dproj, piece, (0, col))
        col += piece.shape[1]
    dw_all = matmul(n1, dproj, ta=True, name="d_w_all")
    if dist is None:
        gr["w_in"] = jnp.concatenate(
            [dw_all[:, :c0], dw_all[:, off["dt"]:off["dt"] + dm.H], dw_all[:, c0:c0 + 3 * dm.SBW]], axis=1)
        gr["w_gate"] = dw_all[:, off["gate"]:off["gate"] + 2 * d]
        dn1 = matmul(dproj, w_all, tb=True, name="d_n1")
    else:
        pair_sum_b, rider_b = reduce_start(slabs_b(dw_all, dm), "b")
        dn1, from_chips_b = matmul(dproj, w_all, tb=True, name="d_n1", rider=rider_b)
        reduced_b = reduce_finish(pair_sum_b, from_chips_b, "b")
        n_in = dist.shard_shapes["w_in"][1]
        reduced.update(w_in=reduced_b[:, :n_in], w_gate=reduced_b[:, n_in:])
    (grad_x,), (gr["norm_mix_pre"],) = row_bwd("norm1_bwd", lambda u, w: (_rms(u, w), u), [x], [sm["norm_mix_pre"]],
                                               [[dn1], [dx_a]], [F32])
    return loss, grad_x, gr, reduced


def kernel(x, p, norm_mix_pre, w_in, conv_w, conv_b, dt_bias, a_log, d_skip, ssd_norm, w_ssd_branch, w_sb_branch, w_gate, b_gate, w_out, norm_mix_post, norm_ffn_pre, w_ff1, w_ff2, norm_ffn_post, w_ple, w_ple_gate, norm_ple_post, loss_target, m_norm_mix_pre, m_w_in, m_conv_w, m_conv_b, m_dt_bias, m_a_log, m_d_skip, m_ssd_norm, m_w_ssd_branch, m_w_sb_branch, m_w_gate, m_b_gate, m_w_out, m_norm_mix_post, m_norm_ffn_pre, m_w_ff1, m_w_ff2, m_norm_ffn_post, m_w_ple, m_w_ple_gate, m_norm_ple_post, v_norm_mix_pre, v_w_in, v_conv_w, v_conv_b, v_dt_bias, v_a_log, v_d_skip, v_ssd_norm, v_w_ssd_branch, v_w_sb_branch, v_w_gate, v_b_gate, v_w_out, v_norm_mix_post, v_norm_ffn_pre, v_w_ff1, v_w_ff2, v_norm_ffn_post, v_w_ple, v_w_ple_gate, v_norm_ple_post):
    loc = dict(locals())
    unbatch = lambda a: a[0] if a.ndim == 3 else a
    w = {n: unbatch(loc[n]) for n in ALL_WEIGHTS}
    m = {n: unbatch(loc["m_" + n]) for n in ALL_WEIGHTS}
    v = {n: unbatch(loc["v_" + n]) for n in ALL_WEIGHTS}
    xs, ps, tgt = x[0], p[0, 0], loss_target[0]
    s, d = xs.shape
    di = w["w_ssd_branch"].shape[0] * N_CHIPS
    cd = w["conv_b"].shape[1]
    dm = Dims(S=s, D=d, DI=di, H=w["dt_bias"].shape[1], G=(cd - di) // (2 * D_STATE), CD=cd,
              SBW=w["w_sb_branch"].shape[0] * N_CHIPS, DFF=w["w_ff2"].shape[0] * N_CHIPS, PLE=ps.shape[1])
    ix, iy, ic = lax.axis_index("x"), lax.axis_index("y"), lax.axis_index("c")
    chip_idx = jnp.reshape(2 * ix + iy, (1,)).astype(jnp.int32)
    core_idx = jnp.reshape(ic, (1,)).astype(jnp.int32)
    dev_idx = jnp.reshape(4 * ix + 2 * iy + ic, (1,)).astype(jnp.int32)

    shard_shapes = {n: w[n].shape for n in BIG_WEIGHTS}
    packed_b = pack_b(w["w_in"], w["w_gate"]).astype(BF16)
    wf = {"w_all": w_all_from_slabs(own_slab(gather_weights(packed_b), packed_b, chip_idx), dm)}
    sm = {n: w[n] for n in SMALL_WEIGHTS}
    dist = Dist(packed_a=pack_gather(w, PART_A, d), shard_shapes=shard_shapes, core_idx=core_idx, chip_idx=chip_idx)

    loss_part, grad_x, gr, grads = local_step(xs, ps, tgt, wf, sm, dm, dist)
    loss = lax.psum(loss_part, ("x", "y", "c"))

    lay = _small_layout({n: w[n].shape[1] for n in SMALL_WEIGHTS}, d)
    gs_red = sum_small(exchange_small(pack_small(gr, lay, d)), dev_idx, "sum_small")
    grads.update(unpack_small(gs_red, lay))
    delta, new_m, new_v = {}, {}, {}
    for n in BIG_WEIGHTS:
        delta[n], new_m[n], new_v[n] = adamw(w[n], grads[n], m[n], v[n], "adamw_" + n)
    d_sm, nm_sm, nv_sm = adamw(pack_small(w, lay, d), gs_red, pack_small(m, lay, d), pack_small(v, lay, d), "adamw_small")
    for out, packed in ((delta, d_sm), (new_m, nm_sm), (new_v, nv_sm)):
        out.update(unpack_small(packed, lay))

    def leaves(vals):
        return [vals[n][None] if n in BIG_WEIGHTS else vals[n] for n in ALL_WEIGHTS]

    return (loss, grad_x[None], *leaves(grads), *leaves(delta), *leaves(new_m), *leaves(new_v))
```

```python
import functools
import math
from typing import NamedTuple

import jax
import jax.numpy as jnp
from jax import lax
from jax.experimental import pallas as pl
from jax.experimental.pallas import tpu as pltpu

F32 = jnp.float32
BF16 = jnp.bfloat16
SDS = jax.ShapeDtypeStruct

HEAD_DIM = 64
GROUP_HEADS = 4
D_STATE = 128
CHUNK = 128
ATT_TILE = 128
ATT_GROUP = 3
ATT_DEAD = -160.0
LOG2E = 1.4426950408889634
CONV_K = 4
CONV_COLS = 128
RMS_EPS = 1e-6
LANES = 128
DT_PAD = 512
N_CHIPS = 4
N_DEV = 8
SMALL_ROWS = 16
VMEM_LIMIT = 48 * 1024 * 1024
MAX_TK = 3072

ADAM_LR = 0.001
ADAM_B1 = 0.9
ADAM_B2 = 0.999
ADAM_EPS = 1e-08
ADAM_WD = 0.01
ADAM_STEP = 10

MESH_ID = pl.DeviceIdType.MESH
HBM_SPEC = pl.BlockSpec(memory_space=pltpu.HBM)

BIG_WEIGHTS = ("w_in", "conv_w", "w_ssd_branch", "w_sb_branch", "w_gate", "w_out", "w_ff1", "w_ff2", "w_ple", "w_ple_gate")
SHARD_AXIS = {"w_in": 1, "conv_w": 1, "w_ssd_branch": 0, "w_sb_branch": 0, "w_gate": 1, "w_out": 0, "w_ff1": 1,
              "w_ff2": 0, "w_ple": 1, "w_ple_gate": 0}
SMALL_WEIGHTS = ("norm_mix_pre", "conv_b", "dt_bias", "a_log", "d_skip", "ssd_norm", "b_gate", "norm_mix_post",
                 "norm_ffn_pre", "norm_ffn_post", "norm_ple_post")
ALL_WEIGHTS = ("norm_mix_pre", "w_in", "conv_w", "conv_b", "dt_bias", "a_log", "d_skip", "ssd_norm", "w_ssd_branch",
               "w_sb_branch", "w_gate", "b_gate", "w_out", "norm_mix_post", "norm_ffn_pre", "w_ff1", "w_ff2",
               "norm_ffn_post", "w_ple", "w_ple_gate", "norm_ple_post")


class Dims(NamedTuple):
    S: int
    D: int
    DI: int
    H: int
    G: int
    CD: int
    SBW: int
    DFF: int
    PLE: int

    @property
    def NA(self):
        return self.DI + self.CD + 3 * self.SBW + 2 * self.D + DT_PAD

    @property
    def off(self):
        o = {}
        o["z"] = 0
        o["xbc"] = self.DI
        o["q"] = self.DI + self.CD
        o["k"] = o["q"] + self.SBW
        o["v"] = o["k"] + self.SBW
        o["gate"] = o["v"] + self.SBW
        o["dt"] = o["gate"] + 2 * self.D
        return o


def _cparams(sem):
    return pltpu.CompilerParams(dimension_semantics=sem, vmem_limit_bytes=VMEM_LIMIT)


def _pick(n, cands):
    for c in cands:
        if n % c == 0:
            return c
    raise ValueError(f"no tile for {n}")


def _grid_call(body, *, grid, in_specs, out_specs, out_shape, scratch, operands, name, rider=None):
    if rider is None:
        sem = ("parallel",) + ("arbitrary",) * (len(grid) - 1)
        return pl.pallas_call(body, grid=grid, in_specs=in_specs, out_specs=out_specs, out_shape=out_shape,
                              scratch_shapes=scratch, compiler_params=_cparams(sem), name=name)(*operands)
    return pl.pallas_call(
        _with_rider(body, rider, grid, len(in_specs), len(out_specs)), grid=grid,
        in_specs=list(in_specs) + [HBM_SPEC] * len(rider.operands),
        out_specs=list(out_specs) + [HBM_SPEC] * len(rider.out_shape),
        out_shape=list(out_shape) + list(rider.out_shape), scratch_shapes=list(scratch) + list(rider.scratch),
        compiler_params=_cparams(("arbitrary",) * len(grid)), name=name)(*operands, *rider.operands)


class SlabOut(NamedTuple):
    buf: object
    row0: int
    rows: int
    axis: int


def matmul(a, b, *, ta=False, tb=False, out_dtype=F32, name, rider=None, slab=None, finish=None, sides=()):
    m, k = (a.shape[1], a.shape[0]) if ta else a.shape
    n, kb = b.shape if tb else (b.shape[1], b.shape[0])
    assert k == kb, (a.shape, b.shape, ta, tb)
    tm = _pick(m, (1024, 512, 256, 128))
    tn = _pick(n, (512, 256, 128))
    if slab is not None:
        tm = _pick(math.gcd(slab.rows, slab.row0), (1024, 512, 256, 128, 64, 32, 16, 8))
        tn = _pick(slab.buf.shape[2], (512, 256, 128))
    tk = max(t for t in range(LANES, min(k, MAX_TK) + 1, LANES) if k % t == 0)
    nk = k // tk
    dims = (((0 if ta else 1,), (1 if tb else 0,)), ((), ()))

    def body(a_ref, b_ref, o_ref, acc_ref):
        part = lax.dot_general(a_ref[...].astype(BF16), b_ref[...].astype(BF16), dims, preferred_element_type=F32)
        if nk == 1:
            o_ref[...] = part.astype(o_ref.dtype)
        else:
            kk = pl.program_id(2)

            @pl.when(kk == 0)
            def _():
                acc_ref[...] = part

            @pl.when(kk > 0)
            def _():
                acc_ref[...] += part

            @pl.when(kk == nk - 1)
            def _():
                o_ref[...] = acc_ref[...].astype(o_ref.dtype)

    a_spec = pl.BlockSpec((tk, tm), lambda i, j, kk: (kk, i)) if ta else pl.BlockSpec((tm, tk), lambda i, j, kk: (i, kk))
    b_spec = pl.BlockSpec((tn, tk), lambda i, j, kk: (j, kk)) if tb else pl.BlockSpec((tk, tn), lambda i, j, kk: (kk, j))
    if slab is not None:
        width = slab.buf.shape[2]
        rb0, per_shard = slab.row0 // tm, slab.rows // tm
        if slab.axis == 0:
            where = lambda i, j, kk: (i // per_shard, rb0 + i % per_shard, j)
        else:
            where = lambda i, j, kk: (j // (width // tn), rb0 + i, j % (width // tn))
        return pl.pallas_call(
            lambda a_ref, b_ref, buf_ref, o_ref, acc_ref: body(a_ref, b_ref, o_ref, acc_ref),
            grid=(m // tm, n // tn, nk), in_specs=[a_spec, b_spec, pl.BlockSpec(memory_space=pl.ANY)],
            out_specs=pl.BlockSpec((None, tm, tn), where), out_shape=SDS(slab.buf.shape, slab.buf.dtype),
            scratch_shapes=[pltpu.VMEM((tm, tn), F32)], input_output_aliases={2: 0},
            compiler_params=_cparams(("parallel", "parallel", "arbitrary")), name=name)(a, b, slab.buf)
    tile_spec = pl.BlockSpec((tm, tn), lambda i, j, kk: (i, j))
    if finish is not None:
        def fused(*refs):
            a_ref, b_ref, side_refs = refs[0], refs[1], refs[2:2 + len(sides)]
            o_refs, acc_ref = refs[2 + len(sides):-1], refs[-1]

            def emit(acc):
                for o_ref, val in zip(o_refs, finish(acc, *[r[...] for r in side_refs])):
                    o_ref[...] = val.astype(o_ref.dtype)

            part = lax.dot_general(a_ref[...].astype(BF16), b_ref[...].astype(BF16), dims, preferred_element_type=F32)
            if nk == 1:
                emit(part)
            else:
                kk = pl.program_id(2)

                @pl.when(kk == 0)
                def _():
                    acc_ref[...] = part

                @pl.when(kk > 0)
                def _():
                    acc_ref[...] += part

                @pl.when(kk == nk - 1)
                def _():
                    emit(acc_ref[...])

        return pl.pallas_call(
            fused, grid=(m // tm, n // tn, nk), in_specs=[a_spec, b_spec] + [tile_spec] * len(sides),
            out_specs=[tile_spec] * len(out_dtype), out_shape=[SDS((m, n), dt) for dt in out_dtype],
            scratch_shapes=[pltpu.VMEM((tm, tn), F32)],
            compiler_params=_cparams(("parallel", "parallel", "arbitrary")), name=name)(a, b, *sides)
    res = _grid_call(body, grid=(m // tm, n // tn, nk), in_specs=[a_spec, b_spec],
                     out_specs=[tile_spec], out_shape=[SDS((m, n), out_dtype)],
                     scratch=[pltpu.VMEM((tm, tn), F32)], operands=(a, b), name=name, rider=rider)
    return res[0] if rider is None else res


def _row_spec(entry, tile):
    arr, width, cb = entry if isinstance(entry, tuple) else (entry, entry.shape[1], 0)
    return arr, pl.BlockSpec((tile, width), lambda i, cb=cb: (i, cb))


def _par_spec(p):
    return pl.BlockSpec(p.shape, lambda i: (0, 0))


def row_fwd(name, fn, rows, params, outs, tile=256):
    arrs, specs = zip(*[_row_spec(e, tile) for e in rows])
    s = arrs[0].shape[0]
    nr, npar = len(rows), len(params)

    def body(*refs):
        r = [x[...].astype(F32) for x in refs[:nr]]
        p = [x[...] for x in refs[nr:nr + npar]]
        res = fn(*r, *p)
        for o_ref, val in zip(refs[nr + npar:], res):
            o_ref[...] = val.astype(o_ref.dtype)

    return pl.pallas_call(
        body, grid=(s // tile,), in_specs=list(specs) + [_par_spec(p) for p in params],
        out_specs=[pl.BlockSpec((tile, w), lambda i: (i, 0)) for w, _ in outs],
        out_shape=[SDS((s, w), dt) for w, dt in outs],
        compiler_params=_cparams(("parallel",)), name=name)(*arrs, *params)


def row_bwd(name, fn, rows, params, cots, row_grads, tile=256, primal_sum=False):
    arrs, specs = zip(*[_row_spec(e, tile) for e in rows])
    s = arrs[0].shape[0]
    nr, npar = len(rows), len(params)
    cot_entries = [e for c in cots if c is not None for e in c]
    carrs, cspecs = zip(*[_row_spec(e, tile) for e in cot_entries]) if cot_entries else ((), ())
    nc = len(cot_entries)
    want = [i for i, d in enumerate(row_grads) if d is not None]

    def body(*refs):
        r = [x[...].astype(F32) for x in refs[:nr]]
        p = [x[...] for x in refs[nr:nr + npar]]
        cvals = [x[...].astype(F32) for x in refs[nr + npar:nr + npar + nc]]
        outs = refs[nr + npar + nc:]
        prim, vjp = jax.vjp(fn, *r, *p)
        ct, pos = [], 0
        for c, pr in zip(cots, prim):
            if c is None:
                ct.append(jnp.ones_like(pr))
            else:
                acc = cvals[pos]
                for extra in cvals[pos + 1:pos + len(c)]:
                    acc = acc + extra
                pos += len(c)
                ct.append(acc)
        grads = vjp(tuple(ct))
        for o_ref, i in zip(outs[:len(want)], want):
            o_ref[...] = grads[i].astype(o_ref.dtype)
        acc_refs = outs[len(want):]
        vals = [grads[nr + j] for j in range(npar)]
        if primal_sum:
            vals.append(jnp.sum(prim[0], axis=0, keepdims=True))
        first = pl.program_id(0) == 0

        @pl.when(first)
        def _():
            for a_ref, v in zip(acc_refs, vals):
                a_ref[...] = v

        @pl.when(jnp.logical_not(first))
        def _():
            for a_ref, v in zip(acc_refs, vals):
                a_ref[...] += v

    widths = [(e[1] if isinstance(e, tuple) else e.shape[1]) for e in rows]
    out_specs = [pl.BlockSpec((tile, widths[i]), lambda i_: (i_, 0)) for i in want]
    out_shape = [SDS((s, widths[i]), row_grads[i]) for i in want]
    pshapes = [p.shape for p in params]
    if primal_sum:
        pshapes.append((1, widths[0]))
    out_specs += [pl.BlockSpec(sh, lambda i_: (0, 0)) for sh in pshapes]
    out_shape += [SDS(sh, F32) for sh in pshapes]
    res = pl.pallas_call(
        body, grid=(s // tile,), in_specs=list(specs) + [_par_spec(p) for p in params] + list(cspecs),
        out_specs=out_specs, out_shape=out_shape,
        compiler_params=_cparams(("arbitrary",)), name=name)(*arrs, *params, *carrs)
    return res[:len(want)], res[len(want):]


def _rms(x, w):
    return x * lax.rsqrt(jnp.mean(x * x, axis=-1, keepdims=True) + RMS_EPS) * w


def _sigmoid(x):
    return jax.nn.sigmoid(x)


def _softplus(x):
    return jnp.maximum(x, 0.0) + jnp.log1p(jnp.exp(-jnp.abs(x)))


def f_norm1(x, w):
    return (_rms(x, w),)


def f_dt(raw, bias):
    return (_softplus(raw + bias),)


def f_merge(gp_ssd, gp_sb, yb_ssd, yb_sb, b_ssd, b_sb):
    return (_sigmoid(gp_ssd + b_ssd) * yb_ssd + _sigmoid(gp_sb + b_sb) * yb_sb,)


def f_mix_out(x, mo, w_post, w_pre):
    h1 = x + _rms(mo, w_post)
    return h1, _rms(h1, w_pre)


def f_relu2(a1):
    return (jnp.square(jnp.maximum(a1, 0.0)),)


def f_ffn_out(h1, ff, w):
    return (h1 + _rms(ff, w),)


def f_ple_loss(h2, pg, pe, tgt, w):
    h3 = h2 + _rms(_sigmoid(pg) * pe, w)
    return (0.5 * jnp.square(h3 - tgt) * (1.0 / h2.shape[-1]),)


def _shift_down(u, d, rows):
    return u if d == 0 else jnp.where(rows >= d, pltpu.roll(u, d, 0), 0.0)


def _shift_up(u, d, rows):
    s = u.shape[0]
    return u if d == 0 else jnp.where(rows < s - d, pltpu.roll(u, s - d, 0), 0.0)


def conv_fwd(proj, col0, cd, conv_w, conv_b, name):
    s = proj.shape[0]
    cb0 = col0 // CONV_COLS

    def body(u_ref, w_ref, b_ref, o_ref):
        u = u_ref[...]
        rows = lax.broadcasted_iota(jnp.int32, u.shape, 0)
        y = jnp.broadcast_to(b_ref[...], u.shape)
        for k in range(CONV_K):
            y = y + w_ref[k:k + 1, :] * _shift_down(u, CONV_K - 1 - k, rows)
        o_ref[...] = y * _sigmoid(y)

    return pl.pallas_call(
        body, grid=(cd // CONV_COLS,),
        in_specs=[pl.BlockSpec((s, CONV_COLS), lambda i: (0, cb0 + i)),
                  pl.BlockSpec((CONV_K, CONV_COLS), lambda i: (0, i)),
                  pl.BlockSpec((1, CONV_COLS), lambda i: (0, i))],
        out_specs=pl.BlockSpec((s, CONV_COLS), lambda i: (0, i)),
        out_shape=SDS((s, cd), F32), compiler_params=_cparams(("parallel",)), name=name)(proj, conv_w, conv_b)


def conv_bwd(proj, col0, dout, ch0, conv_w, conv_b, dproj, name):
    s = proj.shape[0]
    ncb = dout.shape[1] // CONV_COLS
    cb0 = (col0 + ch0) // CONV_COLS
    wb0 = ch0 // CONV_COLS

    def body(u_ref, g_ref, w_ref, b_ref, _, du_ref, dw_ref, db_ref):
        u = u_ref[...]
        rows = lax.broadcasted_iota(jnp.int32, u.shape, 0)
        y = jnp.broadcast_to(b_ref[...], u.shape)
        for k in range(CONV_K):
            y = y + w_ref[k:k + 1, :] * _shift_down(u, CONV_K - 1 - k, rows)
        sg = _sigmoid(y)
        dy = g_ref[...] * (sg * (1.0 + y * (1.0 - sg)))
        du = jnp.zeros_like(u)
        for k in range(CONV_K):
            d = CONV_K - 1 - k
            du = du + w_ref[k:k + 1, :] * _shift_up(dy, d, rows)
            dw_ref[k:k + 1, :] = jnp.sum(dy * _shift_down(u, d, rows), axis=0, keepdims=True)
        du_ref[...] = du.astype(du_ref.dtype)
        db_ref[...] = jnp.sum(dy, axis=0, keepdims=True)

    return pl.pallas_call(
        body, grid=(ncb,),
        in_specs=[pl.BlockSpec((s, CONV_COLS), lambda i: (0, cb0 + i)),
                  pl.BlockSpec((s, CONV_COLS), lambda i: (0, i)),
                  pl.BlockSpec((CONV_K, CONV_COLS), lambda i: (0, wb0 + i)),
                  pl.BlockSpec((1, CONV_COLS), lambda i: (0, wb0 + i)),
                  pl.BlockSpec(memory_space=pl.ANY)],
        out_specs=[pl.BlockSpec((s, CONV_COLS), lambda i: (0, cb0 + i)),
                   pl.BlockSpec((CONV_K, CONV_COLS), lambda i: (0, i)),
                   pl.BlockSpec((1, CONV_COLS), lambda i: (0, i))],
        out_shape=[SDS(dproj.shape, dproj.dtype), SDS((CONV_K, ncb * CONV_COLS), F32), SDS((1, ncb * CONV_COLS), F32)],
        input_output_aliases={4: 0},
        compiler_params=_cparams(("parallel",)), name=name)(proj, dout, conv_w, conv_b, dproj)


def _dot(a, b, dims):
    return lax.dot_general(a.astype(BF16), b.astype(BF16), (dims, ((), ())), preferred_element_type=F32)


NN = ((1,), (0,))
NT = ((1,), (1,))
TN = ((0,), (0,))


def ssd_chunk(xs, zs, nw, dtc, dtw, alogs, dsks, bm, cm, prev):
    ln = bm.shape[0]
    gw = GROUP_HEADS * HEAD_DIM
    row = lax.broadcasted_iota(jnp.int32, (ln, ln), 0)
    col = lax.broadcasted_iota(jnp.int32, (ln, ln), 1)
    causal = row >= col
    tri = causal.astype(F32)
    tri_t = (row <= col).astype(F32)
    lane_head = lax.broadcasted_iota(jnp.int32, (1, gw), 1) // HEAD_DIM
    sub_head = lax.broadcasted_iota(jnp.int32, (gw, 1), 0) // HEAD_DIM
    on_lanes = [(lane_head == r).astype(F32) for r in range(GROUP_HEADS)]
    on_rows = [(sub_head == r).astype(F32) for r in range(GROUP_HEADS)]
    cb = _dot(cm, bm, NT)
    decays, dt_full, acs_full, end_full, dsk_full, end_rows = [], 0.0, 0.0, 0.0, 0.0, 0.0
    for r in range(GROUP_HEADS):
        a = -jnp.exp(alogs[r])
        da_c = dtc[r] * a
        da_w = dtw[r] * a
        acs_c = jnp.sum(tri * da_w, axis=1, keepdims=True)
        acs_w = jnp.sum(tri_t * da_c, axis=0, keepdims=True)
        alast = jnp.sum(da_w, axis=1, keepdims=True)
        decays.append(jnp.exp(jnp.where(causal, acs_c - acs_w, -jnp.inf)))
        dt_full = dt_full + dtc[r] * on_lanes[r]
        acs_full = acs_full + acs_c * on_lanes[r]
        end_full = end_full + alast * on_lanes[r]
        dsk_full = dsk_full + dsks[r] * on_lanes[r]
        end_rows = end_rows + alast * on_rows[r]
    xd = xs * dt_full
    y = xs * dsk_full + _dot(cm, prev, NT) * jnp.exp(acs_full)
    for r in range(GROUP_HEADS):
        y = y + _dot(cb * decays[r], xd * on_lanes[r], NN)
    new_prev = prev * jnp.exp(end_rows) + _dot(xd * jnp.exp(end_full - acs_full), bm, TN)
    yg = y * (zs * _sigmoid(zs))
    rstd = lax.rsqrt(jnp.mean(yg * yg, axis=-1, keepdims=True) + RMS_EPS)
    return yg * rstd * nw, new_prev


SSD_STEP_GROUPS = 1


def _ssd_specs(dm, cidx):
    u = SSD_STEP_GROUPS
    gw = GROUP_HEADS * HEAD_DIM
    nb0 = dm.DI // D_STATE
    assert nb0 % u == 0 and dm.G % u == 0
    par = pl.BlockSpec((u, 1, GROUP_HEADS), lambda g, c: (g, 0, 0))
    return dict(
        z=pl.BlockSpec((CHUNK, u * gw), lambda g, c: (cidx(c), g)),
        xs=pl.BlockSpec((CHUNK, u * gw), lambda g, c: (cidx(c), g)),
        b=pl.BlockSpec((CHUNK, u * D_STATE), lambda g, c: (cidx(c), nb0 // u + g)),
        c=pl.BlockSpec((CHUNK, u * D_STATE), lambda g, c: (cidx(c), (nb0 + dm.G) // u + g)),
        dtc=pl.BlockSpec((u, CHUNK, GROUP_HEADS), lambda g, c: (g, cidx(c), 0)),
        dtw=pl.BlockSpec((u, GROUP_HEADS, CHUNK), lambda g, c: (g, 0, cidx(c))),
        par=par,
        nw=pl.BlockSpec((1, u * gw), lambda g, c: (0, g)),
        st=pl.BlockSpec((u, None, gw, D_STATE), lambda g, c: (g, cidx(c), 0, 0)),
    )


def _ssd_load(k, z_ref, xs_ref, b_ref, c_ref, dtc_ref, dtw_ref, alog_ref, dsk_ref, nw_ref):
    gw = GROUP_HEADS * HEAD_DIM
    wide, narrow = slice(k * gw, (k + 1) * gw), slice(k * D_STATE, (k + 1) * D_STATE)
    dtc = tuple(dtc_ref[k, :, r:r + 1] for r in range(GROUP_HEADS))
    dtw = tuple(dtw_ref[k, r:r + 1, :] for r in range(GROUP_HEADS))
    alogs = tuple(alog_ref[k, :, r:r + 1] for r in range(GROUP_HEADS))
    dsks = tuple(dsk_ref[k, :, r:r + 1] for r in range(GROUP_HEADS))
    return xs_ref[:, wide], z_ref[:, wide], nw_ref[:, wide], dtc, dtw, alogs, dsks, b_ref[:, narrow], c_ref[:, narrow]


def ssd_fwd(proj, xbc, dtc, dtw, alog, dsk, nw, dm, name, rider=None):
    nc = dm.S // CHUNK
    u = SSD_STEP_GROUPS
    gw = GROUP_HEADS * HEAD_DIM
    sp = _ssd_specs(dm, lambda c: c)

    def body(z_ref, xs_ref, b_ref, c_ref, dtc_ref, dtw_ref, alog_ref, dsk_ref, nw_ref, y_ref, st_ref, prev):
        @pl.when(pl.program_id(1) == 0)
        def _():
            prev[...] = jnp.zeros_like(prev)

        st_ref[...] = prev[...]
        for k in range(u):
            args = _ssd_load(k, z_ref, xs_ref, b_ref, c_ref, dtc_ref, dtw_ref, alog_ref, dsk_ref, nw_ref)
            out, new = ssd_chunk(*args, prev[k])
            y_ref[:, k * gw:(k + 1) * gw] = out.astype(y_ref.dtype)
            prev[k] = new

    return _grid_call(
        body, grid=(dm.G // u, nc),
        in_specs=[sp["z"], sp["xs"], sp["b"], sp["c"], sp["dtc"], sp["dtw"], sp["par"], sp["par"], sp["nw"]],
        out_specs=[sp["xs"], sp["st"]],
        out_shape=[SDS((dm.S, dm.DI), BF16), SDS((dm.G, nc, gw, D_STATE), F32)],
        scratch=[pltpu.VMEM((u, gw, D_STATE), F32)],
        operands=(proj, xbc, xbc, xbc, dtc, dtw, alog, dsk, nw), name=name, rider=rider)


def ssd_bwd(proj, xbc, dtc, dtw, alog, dsk, nw, states, dy, dm, name):
    nc = dm.S // CHUNK
    u = SSD_STEP_GROUPS
    sp = _ssd_specs(dm, lambda c: nc - 1 - c)
    gw = GROUP_HEADS * HEAD_DIM
    bc_spec = pl.BlockSpec((CHUNK, u * D_STATE), lambda g, c: (nc - 1 - c, g))

    def body(z_ref, xs_ref, b_ref, c_ref, dtc_ref, dtw_ref, alog_ref, dsk_ref, nw_ref, st_ref, dy_ref,
             dz_ref, dxs_ref, db_ref, dc_ref, ddtc_ref, ddtw_ref, dalog_ref, ddsk_ref, dnw_ref, dprev):
        first = pl.program_id(1) == 0

        @pl.when(first)
        def _():
            dprev[...] = jnp.zeros_like(dprev)

        param_grads = []
        for k in range(u):
            wide, narrow = slice(k * gw, (k + 1) * gw), slice(k * D_STATE, (k + 1) * D_STATE)
            args = _ssd_load(k, z_ref, xs_ref, b_ref, c_ref, dtc_ref, dtw_ref, alog_ref, dsk_ref, nw_ref)
            _, vjp = jax.vjp(ssd_chunk, *args, st_ref[k])
            gxs, gzs, gnw, gdtc, gdtw, galogs, gdsks, gb, gc, gprev = vjp((dy_ref[:, wide], dprev[k]))
            dxs_ref[:, wide] = gxs
            dz_ref[:, wide] = gzs.astype(dz_ref.dtype)
            db_ref[:, narrow] = gb
            dc_ref[:, narrow] = gc
            dprev[k] = gprev
            for r in range(GROUP_HEADS):
                ddtc_ref[k, :, r:r + 1] = gdtc[r]
                ddtw_ref[k, r:r + 1, :] = gdtw[r]
            param_grads.append((wide, gnw, galogs, gdsks))

        @pl.when(first)
        def _():
            for k, (wide, gnw, galogs, gdsks) in enumerate(param_grads):
                dnw_ref[:, wide] = gnw
                for r in range(GROUP_HEADS):
                    dalog_ref[k, :, r:r + 1] = galogs[r]
                    ddsk_ref[k, :, r:r + 1] = gdsks[r]

        @pl.when(jnp.logical_not(first))
        def _():
            for k, (wide, gnw, galogs, gdsks) in enumerate(param_grads):
                dnw_ref[:, wide] += gnw
                for r in range(GROUP_HEADS):
                    dalog_ref[k, :, r:r + 1] += galogs[r]
                    ddsk_ref[k, :, r:r + 1] += gdsks[r]

    xs_out = pl.BlockSpec((CHUNK, u * gw), lambda g, c: (nc - 1 - c, g))
    return pl.pallas_call(
        body, grid=(dm.G // u, nc),
        in_specs=[sp["z"], sp["xs"], sp["b"], sp["c"], sp["dtc"], sp["dtw"], sp["par"], sp["par"], sp["nw"],
                  sp["st"], xs_out],
        out_specs=[xs_out, xs_out, bc_spec, bc_spec, sp["dtc"], sp["dtw"], sp["par"], sp["par"], sp["nw"]],
        out_shape=[SDS((dm.S, dm.NA), BF16), SDS((dm.S, dm.DI), F32), SDS((dm.S, dm.G * D_STATE), F32),
                   SDS((dm.S, dm.G * D_STATE), F32), SDS((dm.G, dm.S, GROUP_HEADS), F32), SDS((dm.G, GROUP_HEADS, dm.S), F32),
                   SDS((dm.G, 1, GROUP_HEADS), F32), SDS((dm.G, 1, GROUP_HEADS), F32), SDS((1, dm.DI), F32)],
        scratch_shapes=[pltpu.VMEM((u, gw, D_STATE), F32)],
        compiler_params=_cparams(("parallel", "arbitrary")), name=name)(
            proj, xbc, xbc, xbc, dtc, dtw, alog, dsk, nw, states, dy)


def _split_bf16(v):
    hi = v.astype(BF16)
    return hi, (v - hi.astype(F32)).astype(BF16)


def _tri(v, mat):
    hi, lo = _split_bf16(v)
    return jnp.dot(hi, mat, preferred_element_type=F32) + jnp.dot(lo, mat, preferred_element_type=F32)


def _blocks(v):
    return [v[:, b * ATT_TILE:(b + 1) * ATT_TILE] for b in range(v.shape[1] // ATT_TILE)]


def _sb_group(z, mask, run, after_mat):
    sp = jnp.maximum(z, 0.0) + jnp.log2(1.0 + jnp.exp2(-jnp.abs(z)))
    lk = -sp if mask is None else jnp.where(mask, -sp, 0.0)
    cums = [_tri(v, after_mat) for v in _blocks(lk)]
    sums = [jnp.sum(v, axis=1, keepdims=True) for v in _blocks(lk)]
    later = [None] * len(cums)
    for b in reversed(range(len(cums))):
        later[b] = run + cums[b]
        run = run + sums[b]
    ls = z - sp
    w = jnp.exp2(ls + jnp.concatenate(later, axis=1))
    if mask is not None:
        w = jnp.where(mask, w, 0.0)
    return ls, w, run


def _alive(run_a, run_b):
    return (jnp.max(jnp.maximum(run_a, run_b)) > ATT_DEAD).astype(jnp.int32)


def _window(i, jj, t):
    gw = ATT_GROUP * t
    end = (i + 1 - ATT_GROUP * jj) * t
    r0 = pl.multiple_of(jnp.maximum(end - gw, 0), t)
    rows = i * t + lax.broadcasted_iota(jnp.int32, (t, gw), 0)
    cols = r0 + lax.broadcasted_iota(jnp.int32, (t, gw), 1)
    return r0, jnp.logical_and(cols < rows, cols < end)


def _att_specs(dm, s):
    t = ATT_TILE
    qb, kb, vb = dm.off["q"] // LANES, dm.off["k"] // LANES, dm.off["v"] // LANES
    return (pl.BlockSpec((t, LANES), lambda p, i: (i, qb + p)),
            pl.BlockSpec((s, LANES), lambda p, i: (0, kb + p)),
            pl.BlockSpec((s, LANES), lambda p, i: (0, vb + p)))


def attn_fwd(proj, dm, name, rider=None):
    s, t = dm.S, ATT_TILE
    scale = HEAD_DIM ** -0.5
    hsl = [slice(h * HEAD_DIM, (h + 1) * HEAD_DIM) for h in range(2)]

    gw = ATT_GROUP * t

    def body(q_ref, k_ref, v_ref, o_ref):
        i = pl.program_id(1)
        gd = i // ATT_GROUP
        r_io = lax.broadcasted_iota(jnp.int32, (t, t), 0)
        c_io = lax.broadcasted_iota(jnp.int32, (t, t), 1)
        after_mat = (r_io > c_io).astype(BF16)
        qs = [q_ref[:, sl].astype(BF16) for sl in hsl]

        def group(jj, carry):
            r0, mask = _window(i, jj, t)
            zs = [_dot(qs[h], k_ref[pl.ds(r0, gw), hsl[h]], NT) * (scale * LOG2E) for h in range(2)]
            res = [_sb_group(zs[h], mask, carry[h][0], after_mat) for h in range(2)]
            return tuple((res[h][2], carry[h][1] + _dot(res[h][1], v_ref[pl.ds(r0, gw), hsl[h]], NN)) for h in range(2))

        zero = (jnp.zeros((t, 1), F32), jnp.zeros((t, HEAD_DIM), F32))
        carry = group(0, (zero, zero))

        def step(st):
            jj, _, c = st
            c = group(jj, c)
            return jj + 1, _alive(c[0][0], c[1][0]), c

        _, _, carry = lax.while_loop(lambda st: jnp.logical_and(st[0] <= gd, st[1] > 0), step,
                                     (jnp.int32(1), _alive(carry[0][0], carry[1][0]), carry))
        for h in range(2):
            o_ref[:, hsl[h]] = carry[h][1]

    qs_, ks_, vs_ = _att_specs(dm, s)
    res = _grid_call(body, grid=(dm.SBW // LANES, s // t), in_specs=[qs_, ks_, vs_],
                     out_specs=[pl.BlockSpec((t, LANES), lambda p, i: (i, p))], out_shape=[SDS((s, dm.SBW), F32)],
                     scratch=[], operands=(proj, proj, proj), name=name, rider=rider)
    return res[0] if rider is None else res


def attn_bwd(proj, do, dm, name, rider=None):
    s, t = dm.S, ATT_TILE
    nq = s // t
    gw = ATT_GROUP * t
    n_win = (nq - 1) // ATT_GROUP + 1
    scale = HEAD_DIM ** -0.5
    hsl = [slice(h * HEAD_DIM, (h + 1) * HEAD_DIM) for h in range(2)]

    def body(q_ref, k_ref, v_ref, do_ref, dq_ref, dk_ref, dv_ref, dk_acc, dv_acc, g_scr, s_scr):
        i = pl.program_id(1)

        @pl.when(i == 0)
        def _():
            dk_acc[...] = jnp.zeros_like(dk_acc)
            dv_acc[...] = jnp.zeros_like(dv_acc)

        gd = i // ATT_GROUP
        r_io = lax.broadcasted_iota(jnp.int32, (t, t), 0)
        c_io = lax.broadcasted_iota(jnp.int32, (t, t), 1)
        after_mat = (r_io > c_io).astype(BF16)
        before_mat = (r_io < c_io).astype(BF16)
        qs = [q_ref[:, sl].astype(BF16) for sl in hsl]
        dos = [do_ref[:, sl].astype(BF16) for sl in hsl]
        q_t = q_ref[...].T.astype(BF16)
        do_t = do_ref[...].T.astype(BF16)

        def pass1(jj, runs):
            r0, mask = _window(i, jj, t)
            zs = [_dot(qs[h], k_ref[pl.ds(r0, gw), hsl[h]], NT) * (scale * LOG2E) for h in range(2)]
            dws = [_dot(dos[h], v_ref[pl.ds(r0, gw), hsl[h]], NT) for h in range(2)]
            out = []
            for h in range(2):
                ls, w, run = _sb_group(zs[h], mask, runs[h], after_mat)
                g_scr[h, jj] = dws[h] * w
                s_scr[h, jj] = jnp.exp2(ls)
                dv_acc[hsl[h], pl.ds(r0, gw)] += _dot(do_t[hsl[h]], w, NN)
                out.append(run)
            return tuple(out)

        zero_col = jnp.zeros((t, 1), F32)
        runs = pass1(0, (zero_col, zero_col))

        def step1(st):
            jj, _, r = st
            r = pass1(jj, r)
            return jj + 1, _alive(r[0], r[1]), r

        walked, _, _ = lax.while_loop(lambda st: jnp.logical_and(st[0] <= gd, st[1] > 0), step1,
                                      (jnp.int32(1), _alive(runs[0], runs[1]), runs))

        def pass2(jj, carry):
            r0, mask = _window(i, jj, t)
            out = []
            for h in range(2):
                pre, dq = carry[h]
                gg = g_scr[h, jj]
                sig = s_scr[h, jj]
                before = []
                for v in _blocks(gg):
                    before.append(pre + _tri(v, before_mat))
                    pre = pre + jnp.sum(v, axis=1, keepdims=True)
                dz = jnp.where(mask, gg * (1.0 - sig) - jnp.concatenate(before, axis=1) * sig, 0.0)
                dz = (dz * scale).astype(BF16)
                dk_acc[hsl[h], pl.ds(r0, gw)] += _dot(q_t[hsl[h]], dz, NN)
                out.append((pre, dq + _dot(dz, k_ref[pl.ds(r0, gw), hsl[h]], NN)))
            return tuple(out)

        zero = (zero_col, jnp.zeros((t, HEAD_DIM), F32))
        carry = lax.fori_loop(1, walked, lambda n, c: pass2(walked - n, c), (zero, zero))
        carry = pass2(0, carry)
        for h in range(2):
            dq_ref[:, hsl[h]] = carry[h][1].astype(dq_ref.dtype)

        @pl.when(i == nq - 1)
        def _():
            for b in range(nq):
                rows = slice(b * t, (b + 1) * t)
                dk_ref[rows, :] = dk_acc[:, rows].T.astype(dk_ref.dtype)
                dv_ref[rows, :] = dv_acc[:, rows].T.astype(dv_ref.dtype)

    qs_, ks_, vs_ = _att_specs(dm, s)
    tile_spec = pl.BlockSpec((t, LANES), lambda p, i: (i, p))
    full_spec = pl.BlockSpec((s, LANES), lambda p, i: (0, p))
    return _grid_call(
        body, grid=(dm.SBW // LANES, nq), in_specs=[qs_, ks_, vs_, tile_spec],
        out_specs=[tile_spec, full_spec, full_spec], out_shape=[SDS((s, dm.SBW), BF16)] * 3,
        scratch=[pltpu.VMEM((LANES, s), F32), pltpu.VMEM((LANES, s), F32),
                 pltpu.VMEM((2, n_win, t, gw), F32), pltpu.VMEM((2, n_win, t, gw), F32)],
        operands=(proj, proj, proj, do), name=name, rider=rider)


def adamw(w, g, m, v, name):
    rows, width = w.shape
    tile = _pick(rows, (256, 64, 16, 8, 4))
    c1 = 1.0 / (1.0 - ADAM_B1 ** ADAM_STEP)
    c2 = 1.0 / (1.0 - ADAM_B2 ** ADAM_STEP)

    def body(w_ref, g_ref, m_ref, v_ref, d_ref, nm_ref, nv_ref):
        gg = g_ref[...]
        nm = ADAM_B1 * m_ref[...] + (1.0 - ADAM_B1) * gg
        nv = ADAM_B2 * v_ref[...] + (1.0 - ADAM_B2) * (gg * gg)
        d_ref[...] = -ADAM_LR * ((nm * c1) / (jnp.sqrt(nv * c2) + ADAM_EPS) + ADAM_WD * w_ref[...])
        nm_ref[...] = nm
        nv_ref[...] = nv

    spec = pl.BlockSpec((tile, width), lambda i: (i, 0))
    return pl.pallas_call(
        body, grid=(rows // tile,), in_specs=[spec] * 4, out_specs=[spec] * 3,
        out_shape=[SDS((rows, width), F32)] * 3, compiler_params=_cparams(("parallel",)), name=name)(w, g, m, v)


def _me():
    return lax.axis_index("x"), lax.axis_index("y"), lax.axis_index("c")


def _other_chips(x, y):
    return [(1 - x, y), (x, 1 - y), (1 - x, 1 - y)]


def gather_weights(wp):
    rows, width = wp.shape
    half = rows // 2

    def body(w_ref, out_ref, send_sems, recv_sems):
        x, y, c = _me()
        sibling = (x, y, 1 - c)
        chips = _other_chips(x, y)

        def part(cx, cy, hf):
            return out_ref.at[2 * cx + cy, hf]

        def copy(k, src, dst, to):
            return pltpu.make_async_remote_copy(src_ref=src, dst_ref=dst, send_sem=send_sems.at[k], recv_sem=recv_sems.at[k],
                                                device_id=to, device_id_type=MESH_ID)

        first = [copy(j, w_ref.at[c], part(x, y, c), (cx, cy, c)) for j, (cx, cy) in enumerate(chips)]
        for cp in first:
            cp.start()
        passed = [copy(3 + j, part(cx, cy, c), part(cx, cy, c), sibling) for j, (cx, cy) in enumerate(chips)]
        for j, (cx, cy) in enumerate(chips):
            copy(j, part(cx, cy, c), part(cx, cy, c), (x, y, c)).wait_recv()
            passed[j].start()
        for j, (cx, cy) in enumerate(chips):
            copy(3 + j, part(cx, cy, 1 - c), part(cx, cy, 1 - c), (x, y, c)).wait_recv()
        for cp in first + passed:
            cp.wait_send()

    return pl.pallas_call(
        body, out_shape=SDS((N_CHIPS, 2, half, width), wp.dtype), in_specs=[HBM_SPEC], out_specs=HBM_SPEC,
        scratch_shapes=[pltpu.SemaphoreType.DMA((6,)), pltpu.SemaphoreType.DMA((6,))],
        name="gather_weights")(wp.reshape(2, half, width)).reshape(N_CHIPS, rows, width)


class Rider(NamedTuple):
    operands: tuple
    out_shape: tuple
    scratch: tuple
    start: object
    wait: object


def _with_rider(body, rider, grid, n_in, n_out):
    if rider is None:
        return body
    r_in, r_out = len(rider.operands), len(rider.out_shape)

    def full(*refs):
        ins, refs = refs[:n_in], refs[n_in:]
        rins, refs = refs[:r_in], refs[r_in:]
        outs, refs = refs[:n_out], refs[n_out:]
        routs, refs = refs[:r_out], refs[r_out:]
        scr, rscr = refs[:len(refs) - len(rider.scratch)], refs[len(refs) - len(rider.scratch):]
        ids = [pl.program_id(k) for k in range(len(grid))]
        first = functools.reduce(jnp.logical_and, [i == 0 for i in ids])
        last = functools.reduce(jnp.logical_and, [i == g - 1 for i, g in zip(ids, grid)])

        @pl.when(first)
        def _():
            rider.start(rins, routs, rscr)

        body(*ins, *outs, *scr)

        @pl.when(last)
        def _():
            rider.wait(rins, routs, rscr)

    return full


def gather_rider(wp):
    def copies(ins, outs, scr, sending):
        (w_ref,), (o_ref,), (send_sems, recv_sems) = ins, outs, scr
        x, y, c = _me()
        return [pltpu.make_async_remote_copy(src_ref=w_ref, dst_ref=o_ref.at[2 * x + y if sending else 2 * cx + cy],
                                             send_sem=send_sems.at[j], recv_sem=recv_sems.at[j], device_id=(cx, cy, c),
                                             device_id_type=MESH_ID)
                for j, (cx, cy) in enumerate(_other_chips(x, y))]

    def start(ins, outs, scr):
        for cp in copies(ins, outs, scr, True):
            cp.start()

    def wait(ins, outs, scr):
        for cp in copies(ins, outs, scr, False):
            cp.wait()

    return Rider((wp,), (SDS((N_CHIPS,) + wp.shape, wp.dtype),),
                 (pltpu.SemaphoreType.DMA((3,)), pltpu.SemaphoreType.DMA((3,))), start, wait)


def own_slab(gathered, wp, chip_idx):
    return lax.dynamic_update_slice(gathered, wp[None], (chip_idx[0], 0, 0))


def exchange_rider(sh):
    def copies(ins, outs, scr):
        (s_ref,), (b_ref,), (send_sems, recv_sems) = ins, outs, scr
        x, y, c = _me()
        return [pltpu.make_async_remote_copy(src_ref=s_ref.at[2 * cx + cy], dst_ref=b_ref.at[j], send_sem=send_sems.at[j],
                                             recv_sem=recv_sems.at[j], device_id=(cx, cy, c), device_id_type=MESH_ID)
                for j, (cx, cy) in enumerate(_other_chips(x, y))]

    def start(ins, outs, scr):
        for cp in copies(ins, outs, scr):
            cp.start()

    def wait(ins, outs, scr):
        for cp in copies(ins, outs, scr):
            cp.wait()

    return Rider((sh,), (SDS((3,) + sh.shape[1:], sh.dtype),),
                 (pltpu.SemaphoreType.DMA((3,)), pltpu.SemaphoreType.DMA((3,))), start, wait)


def swap_halves(g, name):
    n, rows, width = g.shape
    half = rows // 2

    def body(g_ref, a_ref, send_sem, recv_sem):
        x, y, c = _me()
        cp = pltpu.make_async_remote_copy(src_ref=g_ref.at[:, pl.ds((1 - c) * half, half), :], dst_ref=a_ref,
                                          send_sem=send_sem, recv_sem=recv_sem, device_id=(x, y, 1 - c), device_id_type=MESH_ID)
        cp.start()
        cp.wait()

    return pl.pallas_call(
        body, out_shape=SDS((n, half, width), g.dtype), in_specs=[HBM_SPEC], out_specs=HBM_SPEC,
        scratch_shapes=[pltpu.SemaphoreType.DMA, pltpu.SemaphoreType.DMA], name=name)(g)


def add_half(g, a, c_idx, name):
    n, rows, width = g.shape
    half = rows // 2
    tile = half // 2
    nt = half // tile

    def body(c_ref, g_ref, a_ref, o_ref, ob_ref):
        v = g_ref[...] + a_ref[...]
        o_ref[...] = v
        ob_ref[...] = v.astype(ob_ref.dtype)

    out_spec = pl.BlockSpec((None, tile, width), lambda s, i, c_ref: (s, i, 0))
    gs = pltpu.PrefetchScalarGridSpec(
        num_scalar_prefetch=1, grid=(n, nt),
        in_specs=[pl.BlockSpec((None, tile, width), lambda s, i, c_ref: (s, c_ref[0] * nt + i, 0)), out_spec],
        out_specs=[out_spec, out_spec])
    return pl.pallas_call(body, grid_spec=gs, out_shape=[SDS((n, half, width), F32), SDS((n, half, width), BF16)],
                          compiler_params=_cparams(("parallel", "parallel")), name=name)(c_idx, g, a)


def exchange_small(small):
    def body(sm_ref, all_ref, send_sems, recv_sems, local_sem):
        x, y, c = _me()
        mine = pltpu.make_async_copy(sm_ref, all_ref.at[0], local_sem)
        mine.start()
        copies = []
        for m in range(1, N_DEV):
            peer = (x ^ ((m >> 2) & 1), y ^ ((m >> 1) & 1), c ^ (m & 1))
            copies.append(pltpu.make_async_remote_copy(
                src_ref=sm_ref, dst_ref=all_ref.at[m], send_sem=send_sems.at[m - 1], recv_sem=recv_sems.at[m - 1],
                device_id=peer, device_id_type=MESH_ID))
        for cp in copies:
            cp.start()
        for cp in copies:
            cp.wait()
        mine.wait()

    return pl.pallas_call(
        body, out_shape=SDS((N_DEV,) + small.shape, small.dtype), in_specs=[HBM_SPEC], out_specs=HBM_SPEC,
        scratch_shapes=[pltpu.SemaphoreType.DMA((N_DEV - 1,)), pltpu.SemaphoreType.DMA((N_DEV - 1,)), pltpu.SemaphoreType.DMA],
        name="exchange_small")(small)


def add_chips(sh, b, k_idx, name):
    n, hf, width = sh.shape
    tile = hf // 2

    def body(k_ref, s_ref, b0, b1, b2, o_ref):
        o_ref[...] = ((s_ref[...] + b0[...].astype(F32)) + b1[...].astype(F32)) + b2[...].astype(F32)

    def bspec(j):
        return pl.BlockSpec((None, tile, width), lambda i, k_ref, j=j: (j, i, 0))

    gs = pltpu.PrefetchScalarGridSpec(
        num_scalar_prefetch=1, grid=(hf // tile,),
        in_specs=[pl.BlockSpec((None, tile, width), lambda i, k_ref: (k_ref[0], i, 0)), bspec(0), bspec(1), bspec(2)],
        out_specs=pl.BlockSpec((tile, width), lambda i, k_ref: (i, 0)))
    return pl.pallas_call(body, grid_spec=gs, out_shape=SDS((hf, width), sh.dtype),
                          compiler_params=_cparams(("parallel",)), name=name)(k_idx, sh, b, b, b)


def sum_small(allsm, me_idx, name):
    _, rows, width = allsm.shape

    def body(me_ref, a_ref, o_ref):
        me = me_ref[0]
        acc = a_ref[me]
        for dev in range(1, N_DEV):
            acc = acc + a_ref[jnp.bitwise_xor(me, dev)]
        o_ref[...] = acc

    gs = pltpu.PrefetchScalarGridSpec(
        num_scalar_prefetch=1, grid=(1,),
        in_specs=[pl.BlockSpec((N_DEV, rows, width), lambda i, me_ref: (0, 0, 0))],
        out_specs=pl.BlockSpec((rows, width), lambda i, me_ref: (0, 0)))
    return pl.pallas_call(body, grid_spec=gs, out_shape=SDS((rows, width), allsm.dtype),
                          compiler_params=_cparams(("arbitrary",)), name=name)(me_idx, allsm)


def join_halves(t, core_idx, name):
    hf, width = t.shape

    def body(t_ref, o_ref, send_sem, recv_sem):
        x, y, c = _me()
        cp = pltpu.make_async_remote_copy(src_ref=t_ref, dst_ref=o_ref, send_sem=send_sem, recv_sem=recv_sem,
                                          device_id=(x, y, 1 - c), device_id_type=MESH_ID)
        cp.start()
        cp.wait()

    theirs = pl.pallas_call(
        body, out_shape=SDS((hf, width), t.dtype), in_specs=[HBM_SPEC], out_specs=HBM_SPEC,
        scratch_shapes=[pltpu.SemaphoreType.DMA, pltpu.SemaphoreType.DMA], name=name)(t)
    return jnp.where(core_idx[0] == 0, jnp.concatenate([t, theirs], axis=0), jnp.concatenate([theirs, t], axis=0))


ROW_PAD = 64


def _pad_rows(rows):
    return -(-rows // ROW_PAD) * ROW_PAD


def _pack_rows(names, shard_shapes, width):
    return _pad_rows(sum((shard_shapes[n][0] * shard_shapes[n][1]) // width for n in names))


def unpack_local(packed, names, shard_shapes, width):
    out, r0 = {}, 0
    for n in names:
        a, b = shard_shapes[n]
        nr = (a * b) // width
        out[n] = packed[r0:r0 + nr].reshape(a, b)
        r0 += nr
    return out


EXACT_IN_GATHER = ("conv_w",)
EXACT_TERMS = 3


def pack_gather(shards, names, width):
    parts = []
    for n in names:
        if n in EXACT_IN_GATHER:
            rest = shards[n].astype(F32)
            for _ in range(EXACT_TERMS):
                term = rest.astype(BF16)
                parts.append(term.reshape(-1, width))
                rest = rest - term.astype(F32)
        else:
            parts.append(shards[n].reshape(-1, width).astype(BF16))
    used = sum(p.shape[0] for p in parts)
    parts.append(jnp.zeros((_pad_rows(used) - used, width), BF16))
    return jnp.concatenate(parts, axis=0)


def unpack_full(gathered, names, shard_shapes, width):
    out, r0 = {}, 0
    for n in names:
        a, b = shard_shapes[n]
        terms = EXACT_TERMS if n in EXACT_IN_GATHER else 1
        nr = (a * b) // width
        pieces = []
        for j in range(N_CHIPS):
            blk = gathered[j, r0:r0 + nr].reshape(a, b)
            for t in range(1, terms):
                blk = blk.astype(F32) + gathered[j, r0 + t * nr:r0 + (t + 1) * nr].reshape(a, b).astype(F32)
            pieces.append(blk)
        out[n] = jnp.concatenate(pieces, axis=SHARD_AXIS[n])
        r0 += terms * nr
    return out


def pack_full(grads, names, shard_shapes, width, pad=True):
    total_rows = _pack_rows(names, shard_shapes, width) if pad else sum(
        (shard_shapes[n][0] * shard_shapes[n][1]) // width for n in names)
    slabs = []
    for j in range(N_CHIPS):
        parts = []
        for n in names:
            a, b = shard_shapes[n]
            ax = SHARD_AXIS[n]
            sz = (a, b)[ax]
            piece = lax.slice_in_dim(grads[n], j * sz, (j + 1) * sz, axis=ax)
            parts.append(piece.reshape(-1, width))
        used = sum(p.shape[0] for p in parts)
        parts.append(jnp.zeros((total_rows - used, width), F32))
        slabs.append(jnp.concatenate(parts, axis=0))
    return jnp.stack(slabs, axis=0)


def _small_layout(sizes, width):
    lay, r = {}, 0
    for n in SMALL_WEIGHTS:
        nr = -(-sizes[n] // width)
        lay[n] = (r, nr, sizes[n])
        r += nr
    assert r <= SMALL_ROWS
    return lay


def pack_small(vals, lay, width):
    rows = []
    for n in SMALL_WEIGHTS:
        r, nr, sz = lay[n]
        v = vals[n].reshape(-1).astype(F32)
        rows.append(jnp.pad(v, (0, nr * width - sz)).reshape(nr, width))
    used = sum(r.shape[0] for r in rows)
    rows.append(jnp.zeros((SMALL_ROWS - used, width), F32))
    return jnp.concatenate(rows, axis=0)


def unpack_small(packed, lay):
    return {n: packed[r:r + nr].reshape(-1)[:sz].reshape(1, sz) for n, (r, nr, sz) in lay.items()}


PART_B = ("w_in", "w_gate")
PART_A = ("w_ff1", "w_ff2", "w_ssd_branch", "w_sb_branch", "w_out", "w_ple_gate", "w_ple", "conv_w")
DIRECT_A = PART_A[:6]


GATHER_EARLY = ("conv_w", "w_ssd_branch", "w_sb_branch", "w_out")
GATHER_LATE = ("w_ff1", "w_ff2", "w_ple_gate", "w_ple")


class Dist(NamedTuple):
    packed_early: object
    packed_late: object
    shard_shapes: dict
    core_idx: object
    chip_idx: object


def build_w_all(w_in, w_gate, dm):
    c0 = dm.DI + dm.CD
    return jnp.concatenate(
        [w_in[:, :c0], w_in[:, c0 + dm.H:], w_gate, w_in[:, c0:c0 + dm.H],
         jnp.zeros((dm.D, DT_PAD - dm.H), w_in.dtype)], axis=1).astype(BF16)


def pack_b(w_in_shard, w_gate_shard):
    return jnp.concatenate([w_in_shard, w_gate_shard], axis=1)


def w_all_from_slabs(slabs, dm):
    n_in = (dm.NA - DT_PAD - 2 * dm.D + dm.H) // N_CHIPS
    c0 = dm.DI + dm.CD

    def w_in_cols(a, b):
        return [slabs[j, :, max(a, j * n_in) - j * n_in:min(b, (j + 1) * n_in) - j * n_in]
                for j in range(N_CHIPS) if max(a, j * n_in) < min(b, (j + 1) * n_in)]

    parts = w_in_cols(0, c0) + w_in_cols(c0 + dm.H, N_CHIPS * n_in) + [slabs[j, :, n_in:] for j in range(N_CHIPS)]
    parts += w_in_cols(c0, c0 + dm.H) + [jnp.zeros((dm.D, DT_PAD - dm.H), slabs.dtype)]
    return jnp.concatenate(parts, axis=1).astype(BF16)


def _w_in_columns(lo, hi, dm):
    c0 = dm.DI + dm.CD
    segments = [(0, c0, 0), (c0, c0 + dm.H, dm.off["dt"]), (c0 + dm.H, c0 + dm.H + 3 * dm.SBW, c0)]
    return [(w0 + max(lo, a) - a, w0 + min(hi, b) - a) for a, b, w0 in segments if max(lo, a) < min(hi, b)]


def slabs_b(dw_all, dm):
    n_in = (dm.NA - DT_PAD - 2 * dm.D + dm.H) // N_CHIPS
    n_gate = 2 * dm.D // N_CHIPS
    slabs = []
    for j in range(N_CHIPS):
        cols = _w_in_columns(j * n_in, (j + 1) * n_in, dm) + [(dm.off["gate"] + j * n_gate, dm.off["gate"] + (j + 1) * n_gate)]
        slabs.append(jnp.concatenate([dw_all[:, a:b] for a, b in cols], axis=1))
    return jnp.stack(slabs, axis=0)


def local_step(x, p, tgt, wf, sm, dm, dist=None):
    s, d = dm.S, dm.D
    off = dm.off
    c0 = dm.DI + dm.CD
    w_all = wf["w_all"] if "w_all" in wf else build_w_all(wf["w_in"], wf["w_gate"], dm)
    g = dm.G
    per_group = lambda v: v.reshape(g, 1, GROUP_HEADS)
    alog, dsk = per_group(sm["a_log"]), per_group(sm["d_skip"])
    b_gate = sm["b_gate"]
    b_ssd, b_sb = b_gate[:, :d], b_gate[:, d:]
    gcol = off["gate"] // d

    (n1,) = row_fwd("norm1", f_norm1, [x], [sm["norm_mix_pre"]], [(d, BF16)])
    proj = matmul(n1, w_all, name="in_proj")
    if dist is None:
        y_sb = attn_fwd(proj, dm, "attn_fwd")
    else:
        y_sb, gathered = attn_fwd(proj, dm, "attn_fwd", rider=gather_rider(dist.packed_early))
        gathered = own_slab(gathered, dist.packed_early, dist.chip_idx)
        wf = {**wf, **unpack_full(gathered, GATHER_EARLY, dist.shard_shapes, d)}
    xbc = conv_fwd(proj, off["xbc"], dm.CD, wf["conv_w"].astype(F32), sm["conv_b"], "conv_fwd")
    dt_raw = proj[:, off["dt"]:off["dt"] + dm.H]
    (dt,) = row_fwd("dt", f_dt, [dt_raw], [sm["dt_bias"]], [(dm.H, F32)])
    dtc = dt.reshape(s, g, GROUP_HEADS).transpose(1, 0, 2)
    dtw = dt.reshape(s, g, GROUP_HEADS).transpose(1, 2, 0)
    if dist is None:
        y_ssd, states = ssd_fwd(proj, xbc, dtc, dtw, alog, dsk, sm["ssd_norm"], dm, "ssd_fwd")
    else:
        y_ssd, states, gathered = ssd_fwd(proj, xbc, dtc, dtw, alog, dsk, sm["ssd_norm"], dm, "ssd_fwd",
                                          rider=gather_rider(dist.packed_late))
        gathered = own_slab(gathered, dist.packed_late, dist.chip_idx)
        wf = {**wf, **unpack_full(gathered, GATHER_LATE, dist.shard_shapes, d)}
    yb_ssd = matmul(y_ssd, wf["w_ssd_branch"], name="ssd_branch")
    yb_sb = matmul(y_sb, wf["w_sb_branch"], name="sb_branch")
    merge_rows = [(proj, d, gcol), (proj, d, gcol + 1), yb_ssd, yb_sb]
    (merged,) = row_fwd("merge", f_merge, merge_rows, [b_ssd, b_sb], [(d, BF16)])
    mo = matmul(merged, wf["w_out"], name="w_out")
    h1, n2 = row_fwd("mix_out", f_mix_out, [x, mo], [sm["norm_mix_post"], sm["norm_ffn_pre"]], [(d, F32), (d, BF16)])
    a1, act = matmul(n2, wf["w_ff1"], name="ff1", finish=lambda v: (v,) + f_relu2(v), out_dtype=(F32, BF16))
    ff = matmul(act, wf["w_ff2"], name="ff2")
    (h2,) = row_fwd("ffn_out", f_ffn_out, [h1, ff], [sm["norm_ffn_post"]], [(d, F32)])
    pg = matmul(h2, wf["w_ple_gate"], name="ple_gate")
    pe = matmul(p, wf["w_ple"], name="ple_emb")

    def reduce_start(gbig, tag):
        from_sibling = swap_halves(gbig, "swap_halves_" + tag)
        pair_sum, pair_sum_bf16 = add_half(gbig, from_sibling, dist.core_idx, "add_half_" + tag)
        return pair_sum, exchange_rider(pair_sum_bf16)

    def reduce_finish(pair_sum, from_chips, tag):
        my_half = add_chips(pair_sum, from_chips, dist.chip_idx, "add_chips_" + tag)
        return join_halves(my_half, dist.core_idx, "join_halves_" + tag)

    gr, reduced = {}, None
    packed_grads = [None if dist is None else jnp.zeros((N_CHIPS, _pack_rows(PART_A, dist.shard_shapes, d), d), F32)]

    def weight_grad(wname, a, dy):
        if dist is None:
            gr[wname] = matmul(a, dy, ta=True, name="d_" + wname)
            return
        row0 = 0
        for n in PART_A[:PART_A.index(wname)]:
            row0 += (dist.shard_shapes[n][0] * dist.shard_shapes[n][1]) // d
        packed_grads[0] = matmul(a, dy, ta=True, name="d_" + wname,
                                 slab=SlabOut(packed_grads[0], row0, dist.shard_shapes[wname][0], SHARD_AXIS[wname]))

    (dh2_a, dpg, dpe), (gr["norm_ple_post"], loss_cols) = row_bwd(
        "ple_loss", f_ple_loss, [h2, pg, pe, tgt], [sm["norm_ple_post"]], [None], [F32, BF16, BF16, None], primal_sum=True)
    loss = jnp.sum(loss_cols)
    gr["w_ple"] = matmul(p, dpe, ta=True, name="d_w_ple")
    weight_grad("w_ple_gate", h2, dpg)
    dh2_b = matmul(dpg, wf["w_ple_gate"], tb=True, name="d_h2")
    (dh1_a, dff), (gr["norm_ffn_post"],) = row_bwd(
        "ffn_out_bwd", f_ffn_out, [h1, ff], [sm["norm_ffn_post"]], [[dh2_a, dh2_b]], [F32, BF16])
    weight_grad("w_ff2", act, dff)
    (da1,) = matmul(dff, wf["w_ff2"], tb=True, name="d_act", sides=(a1,),
                    finish=lambda dact, a1_tile: (dact * (2.0 * jnp.maximum(a1_tile, 0.0)),), out_dtype=(BF16,))
    weight_grad("w_ff1", n2, da1)
    dn2 = matmul(da1, wf["w_ff1"], tb=True, name="d_n2")
    (dx_a, dmo), (gr["norm_mix_post"], gr["norm_ffn_pre"]) = row_bwd(
        "mix_out_bwd", f_mix_out, [x, mo], [sm["norm_mix_post"], sm["norm_ffn_pre"]], [[dh1_a], [dn2]], [F32, BF16])
    weight_grad("w_out", merged, dmo)
    dmerged = matmul(dmo, wf["w_out"], tb=True, name="d_merged")
    (dgp_ssd, dgp_sb, dyb_ssd, dyb_sb), (db_ssd, db_sb) = row_bwd(
        "merge_bwd", f_merge, merge_rows, [b_ssd, b_sb], [[dmerged]], [BF16, BF16, BF16, BF16])
    gr["b_gate"] = jnp.concatenate([db_ssd, db_sb], axis=1)
    weight_grad("w_ssd_branch", y_ssd, dyb_ssd)
    weight_grad("w_sb_branch", y_sb, dyb_sb)
    dy_ssd = matmul(dyb_ssd, wf["w_ssd_branch"], tb=True, name="d_y_ssd")
    dy_sb = matmul(dyb_sb, wf["w_sb_branch"], tb=True, name="d_y_sb")
    dproj, dxs, dbm, dcm, ddtc, ddtw, dalog, ddsk, gr["ssd_norm"] = ssd_bwd(
        proj, xbc, dtc, dtw, alog, dsk, sm["ssd_norm"], states, dy_ssd, dm, "ssd_bwd")
    gr["a_log"], gr["d_skip"] = (v.reshape(1, dm.H) for v in (dalog, ddsk))
    ddt_post = (ddtc.transpose(1, 0, 2) + ddtw.transpose(2, 0, 1)).reshape(s, dm.H)
    (ddt,), (gr["dt_bias"],) = row_bwd("dt_bwd", f_dt, [dt_raw], [sm["dt_bias"]], [[ddt_post]], [BF16])
    conv_w32 = wf["conv_w"].astype(F32)
    dproj, dw_x, dcb_x = conv_bwd(proj, off["xbc"], dxs, 0, conv_w32, sm["conv_b"], dproj, "conv_bwd_x")
    dproj, dw_b, dcb_b = conv_bwd(proj, off["xbc"], dbm, dm.DI, conv_w32, sm["conv_b"], dproj, "conv_bwd_b")
    dproj, dw_c, dcb_c = conv_bwd(proj, off["xbc"], dcm, dm.DI + g * D_STATE, conv_w32, sm["conv_b"], dproj, "conv_bwd_c")
    gr["conv_w"] = jnp.concatenate([dw_x, dw_b, dw_c], axis=1)
    gr["conv_b"] = jnp.concatenate([dcb_x, dcb_b, dcb_c], axis=1)
    if dist is None:
        dq, dk, dv = attn_bwd(proj, dy_sb, dm, "attn_bwd")
    else:
        rest = tuple(n for n in PART_A if n not in DIRECT_A)
        row0 = sum((dist.shard_shapes[n][0] * dist.shard_shapes[n][1]) // d for n in DIRECT_A)
        small_rows = pack_full(gr, rest, dist.shard_shapes, d, pad=False)
        gbig_a = lax.dynamic_update_slice(packed_grads[0], small_rows, (0, row0, 0))
        pair_sum_a, rider_a = reduce_start(gbig_a, "a")
        dq, dk, dv, from_chips_a = attn_bwd(proj, dy_sb, dm, "attn_bwd", rider=rider_a)
        reduced = unpack_local(reduce_finish(pair_sum_a, from_chips_a, "a"), PART_A, dist.shard_shapes, d)
    col = off["q"]
    for piece in (dq, dk, dv, dgp_ssd, dgp_sb, jnp.concatenate([ddt, jnp.zeros((s, DT_PAD - dm.H), BF16)], axis=1)):
        dproj = lax.dynamic_update_slice(dproj, piece, (0, col))
        col += piece.shape[1]
    dw_all = matmul(n1, dproj, ta=True, name="d_w_all")
    if dist is None:
        gr["w_in"] = jnp.concatenate(
            [dw_all[:, :c0], dw_all[:, off["dt"]:off["dt"] + dm.H], dw_all[:, c0:c0 + 3 * dm.SBW]], axis=1)
        gr["w_gate"] = dw_all[:, off["gate"]:off["gate"] + 2 * d]
        dn1 = matmul(dproj, w_all, tb=True, name="d_n1")
    else:
        pair_sum_b, rider_b = reduce_start(slabs_b(dw_all, dm), "b")
        dn1, from_chips_b = matmul(dproj, w_all, tb=True, name="d_n1", rider=rider_b)
        reduced_b = reduce_finish(pair_sum_b, from_chips_b, "b")
        n_in = dist.shard_shapes["w_in"][1]
        reduced.update(w_in=reduced_b[:, :n_in], w_gate=reduced_b[:, n_in:])
    (grad_x,), (gr["norm_mix_pre"],) = row_bwd("norm1_bwd", lambda u, w: (_rms(u, w), u), [x], [sm["norm_mix_pre"]],
                                               [[dn1], [dx_a]], [F32])
    return loss, grad_x, gr, reduced


def kernel(x, p, norm_mix_pre, w_in, conv_w, conv_b, dt_bias, a_log, d_skip, ssd_norm, w_ssd_branch, w_sb_branch, w_gate, b_gate, w_out, norm_mix_post, norm_ffn_pre, w_ff1, w_ff2, norm_ffn_post, w_ple, w_ple_gate, norm_ple_post, loss_target, m_norm_mix_pre, m_w_in, m_conv_w, m_conv_b, m_dt_bias, m_a_log, m_d_skip, m_ssd_norm, m_w_ssd_branch, m_w_sb_branch, m_w_gate, m_b_gate, m_w_out, m_norm_mix_post, m_norm_ffn_pre, m_w_ff1, m_w_ff2, m_norm_ffn_post, m_w_ple, m_w_ple_gate, m_norm_ple_post, v_norm_mix_pre, v_w_in, v_conv_w, v_conv_b, v_dt_bias, v_a_log, v_d_skip, v_ssd_norm, v_w_ssd_branch, v_w_sb_branch, v_w_gate, v_b_gate, v_w_out, v_norm_mix_post, v_norm_ffn_pre, v_w_ff1, v_w_ff2, v_norm_ffn_post, v_w_ple, v_w_ple_gate, v_norm_ple_post):
    loc = dict(locals())
    unbatch = lambda a: a[0] if a.ndim == 3 else a
    w = {n: unbatch(loc[n]) for n in ALL_WEIGHTS}
    m = {n: unbatch(loc["m_" + n]) for n in ALL_WEIGHTS}
    v = {n: unbatch(loc["v_" + n]) for n in ALL_WEIGHTS}
    xs, ps, tgt = x[0], p[0, 0], loss_target[0]
    s, d = xs.shape
    di = w["w_ssd_branch"].shape[0] * N_CHIPS
    cd = w["conv_b"].shape[1]
    dm = Dims(S=s, D=d, DI=di, H=w["dt_bias"].shape[1], G=(cd - di) // (2 * D_STATE), CD=cd,
              SBW=w["w_sb_branch"].shape[0] * N_CHIPS, DFF=w["w_ff2"].shape[0] * N_CHIPS, PLE=ps.shape[1])
    ix, iy, ic = lax.axis_index("x"), lax.axis_index("y"), lax.axis_index("c")
    chip_idx = jnp.reshape(2 * ix + iy, (1,)).astype(jnp.int32)
    core_idx = jnp.reshape(ic, (1,)).astype(jnp.int32)
    dev_idx = jnp.reshape(4 * ix + 2 * iy + ic, (1,)).astype(jnp.int32)

    shard_shapes = {n: w[n].shape for n in BIG_WEIGHTS}
    packed_b = pack_b(w["w_in"], w["w_gate"]).astype(BF16)
    wf = {"w_all": w_all_from_slabs(own_slab(gather_weights(packed_b), packed_b, chip_idx), dm)}
    sm = {n: w[n] for n in SMALL_WEIGHTS}
    dist = Dist(packed_early=pack_gather(w, GATHER_EARLY, d), packed_late=pack_gather(w, GATHER_LATE, d),
                shard_shapes=shard_shapes, core_idx=core_idx, chip_idx=chip_idx)

    loss_part, grad_x, gr, grads = local_step(xs, ps, tgt, wf, sm, dm, dist)
    loss = lax.psum(loss_part, ("x", "y", "c"))

    lay = _small_layout({n: w[n].shape[1] for n in SMALL_WEIGHTS}, d)
    gs_red = sum_small(exchange_small(pack_small(gr, lay, d)), dev_idx, "sum_small")
    grads.update(unpack_small(gs_red, lay))
    delta, new_m, new_v = {}, {}, {}
    for n in BIG_WEIGHTS:
        delta[n], new_m[n], new_v[n] = adamw(w[n], grads[n], m[n], v[n], "adamw_" + n)
    d_sm, nm_sm, nv_sm = adamw(pack_small(w, lay, d), gs_red, pack_small(m, lay, d), pack_small(v, lay, d), "adamw_small")
    for out, packed in ((delta, d_sm), (new_m, nm_sm), (new_v, nv_sm)):
        out.update(unpack_small(packed, lay))

    def leaves(vals):
        return [vals[n][None] if n in BIG_WEIGHTS else vals[n] for n in ALL_WEIGHTS]

    return (loss, grad_x[None], *leaves(grads), *leaves(delta), *leaves(new_m), *leaves(new_v))
```

```python
import functools
import math
from typing import NamedTuple

import jax
import jax.numpy as jnp
from jax import lax
from jax.experimental import pallas as pl
from jax.experimental.pallas import tpu as pltpu

F32 = jnp.float32
BF16 = jnp.bfloat16
SDS = jax.ShapeDtypeStruct

HEAD_DIM = 64
GROUP_HEADS = 4
D_STATE = 128
CHUNK = 128
ATT_TILE = 128
ATT_GROUP = 3
ATT_DEAD = -160.0
LOG2E = 1.4426950408889634
CONV_K = 4
CONV_COLS = 128
RMS_EPS = 1e-6
LANES = 128
DT_PAD = 512
N_CHIPS = 4
N_DEV = 8
SMALL_ROWS = 16
ROW_TILE = 512
VMEM_LIMIT = 48 * 1024 * 1024
MAX_TK = 3072

ADAM_LR = 0.001
ADAM_B1 = 0.9
ADAM_B2 = 0.999
ADAM_EPS = 1e-08
ADAM_WD = 0.01
ADAM_STEP = 10

MESH_ID = pl.DeviceIdType.MESH
HBM_SPEC = pl.BlockSpec(memory_space=pltpu.HBM)

BIG_WEIGHTS = ("w_in", "conv_w", "w_ssd_branch", "w_sb_branch", "w_gate", "w_out", "w_ff1", "w_ff2", "w_ple", "w_ple_gate")
SHARD_AXIS = {"w_in": 1, "conv_w": 1, "w_ssd_branch": 0, "w_sb_branch": 0, "w_gate": 1, "w_out": 0, "w_ff1": 1,
              "w_ff2": 0, "w_ple": 1, "w_ple_gate": 0}
SMALL_WEIGHTS = ("norm_mix_pre", "conv_b", "dt_bias", "a_log", "d_skip", "ssd_norm", "b_gate", "norm_mix_post",
                 "norm_ffn_pre", "norm_ffn_post", "norm_ple_post")
ALL_WEIGHTS = ("norm_mix_pre", "w_in", "conv_w", "conv_b", "dt_bias", "a_log", "d_skip", "ssd_norm", "w_ssd_branch",
               "w_sb_branch", "w_gate", "b_gate", "w_out", "norm_mix_post", "norm_ffn_pre", "w_ff1", "w_ff2",
               "norm_ffn_post", "w_ple", "w_ple_gate", "norm_ple_post")


class Dims(NamedTuple):
    S: int
    D: int
    DI: int
    H: int
    G: int
    CD: int
    SBW: int
    DFF: int
    PLE: int

    @property
    def NA(self):
        return self.DI + self.CD + 3 * self.SBW + 2 * self.D + DT_PAD

    @property
    def off(self):
        o = {}
        o["z"] = 0
        o["xbc"] = self.DI
        o["q"] = self.DI + self.CD
        o["k"] = o["q"] + self.SBW
        o["v"] = o["k"] + self.SBW
        o["gate"] = o["v"] + self.SBW
        o["dt"] = o["gate"] + 2 * self.D
        return o


def _cparams(sem):
    return pltpu.CompilerParams(dimension_semantics=sem, vmem_limit_bytes=VMEM_LIMIT)


def _pick(n, cands):
    for c in cands:
        if n % c == 0:
            return c
    raise ValueError(f"no tile for {n}")


def _grid_call(body, *, grid, in_specs, out_specs, out_shape, scratch, operands, name, rider=None):
    if rider is None:
        sem = ("parallel",) + ("arbitrary",) * (len(grid) - 1)
        return pl.pallas_call(body, grid=grid, in_specs=in_specs, out_specs=out_specs, out_shape=out_shape,
                              scratch_shapes=scratch, compiler_params=_cparams(sem), name=name)(*operands)
    return pl.pallas_call(
        _with_rider(body, rider, grid, len(in_specs), len(out_specs)), grid=grid,
        in_specs=list(in_specs) + [HBM_SPEC] * len(rider.operands),
        out_specs=list(out_specs) + [HBM_SPEC] * len(rider.out_shape),
        out_shape=list(out_shape) + list(rider.out_shape), scratch_shapes=list(scratch) + list(rider.scratch),
        compiler_params=_cparams(("arbitrary",) * len(grid)), name=name)(*operands, *rider.operands)


class SlabOut(NamedTuple):
    buf: object
    row0: int
    rows: int
    axis: int


def matmul(a, b, *, ta=False, tb=False, out_dtype=F32, name, rider=None, slab=None, finish=None, sides=()):
    m, k = (a.shape[1], a.shape[0]) if ta else a.shape
    n, kb = b.shape if tb else (b.shape[1], b.shape[0])
    assert k == kb, (a.shape, b.shape, ta, tb)
    tm = _pick(m, (1024, 512, 256, 128))
    tn = _pick(n, (512, 256, 128))
    if slab is not None:
        tm = _pick(math.gcd(slab.rows, slab.row0), (1024, 512, 256, 128, 64, 32, 16, 8))
        tn = _pick(slab.buf.shape[2], (512, 256, 128))
    tk = max(t for t in range(LANES, min(k, MAX_TK) + 1, LANES) if k % t == 0)
    nk = k // tk
    dims = (((0 if ta else 1,), (1 if tb else 0,)), ((), ()))

    def body(a_ref, b_ref, o_ref, acc_ref):
        part = lax.dot_general(a_ref[...].astype(BF16), b_ref[...].astype(BF16), dims, preferred_element_type=F32)
        if nk == 1:
            o_ref[...] = part.astype(o_ref.dtype)
        else:
            kk = pl.program_id(2)

            @pl.when(kk == 0)
            def _():
                acc_ref[...] = part

            @pl.when(kk > 0)
            def _():
                acc_ref[...] += part

            @pl.when(kk == nk - 1)
            def _():
                o_ref[...] = acc_ref[...].astype(o_ref.dtype)

    a_spec = pl.BlockSpec((tk, tm), lambda i, j, kk: (kk, i)) if ta else pl.BlockSpec((tm, tk), lambda i, j, kk: (i, kk))
    b_spec = pl.BlockSpec((tn, tk), lambda i, j, kk: (j, kk)) if tb else pl.BlockSpec((tk, tn), lambda i, j, kk: (kk, j))
    if slab is not None:
        width = slab.buf.shape[2]
        rb0, per_shard = slab.row0 // tm, slab.rows // tm
        if slab.axis == 0:
            where = lambda i, j, kk: (i // per_shard, rb0 + i % per_shard, j)
        else:
            where = lambda i, j, kk: (j // (width // tn), rb0 + i, j % (width // tn))
        return pl.pallas_call(
            lambda a_ref, b_ref, buf_ref, o_ref, acc_ref: body(a_ref, b_ref, o_ref, acc_ref),
            grid=(m // tm, n // tn, nk), in_specs=[a_spec, b_spec, pl.BlockSpec(memory_space=pl.ANY)],
            out_specs=pl.BlockSpec((None, tm, tn), where), out_shape=SDS(slab.buf.shape, slab.buf.dtype),
            scratch_shapes=[pltpu.VMEM((tm, tn), F32)], input_output_aliases={2: 0},
            compiler_params=_cparams(("parallel", "parallel", "arbitrary")), name=name)(a, b, slab.buf)
    tile_spec = pl.BlockSpec((tm, tn), lambda i, j, kk: (i, j))
    if finish is not None:
        def fused(*refs):
            a_ref, b_ref, side_refs = refs[0], refs[1], refs[2:2 + len(sides)]
            o_refs, acc_ref = refs[2 + len(sides):-1], refs[-1]

            def emit(acc):
                for o_ref, val in zip(o_refs, finish(acc, *[r[...] for r in side_refs])):
                    o_ref[...] = val.astype(o_ref.dtype)

            part = lax.dot_general(a_ref[...].astype(BF16), b_ref[...].astype(BF16), dims, preferred_element_type=F32)
            if nk == 1:
                emit(part)
            else:
                kk = pl.program_id(2)

                @pl.when(kk == 0)
                def _():
                    acc_ref[...] = part

                @pl.when(kk > 0)
                def _():
                    acc_ref[...] += part

                @pl.when(kk == nk - 1)
                def _():
                    emit(acc_ref[...])

        return pl.pallas_call(
            fused, grid=(m // tm, n // tn, nk), in_specs=[a_spec, b_spec] + [tile_spec] * len(sides),
            out_specs=[tile_spec] * len(out_dtype), out_shape=[SDS((m, n), dt) for dt in out_dtype],
            scratch_shapes=[pltpu.VMEM((tm, tn), F32)],
            compiler_params=_cparams(("parallel", "parallel", "arbitrary")), name=name)(a, b, *sides)
    res = _grid_call(body, grid=(m // tm, n // tn, nk), in_specs=[a_spec, b_spec],
                     out_specs=[tile_spec], out_shape=[SDS((m, n), out_dtype)],
                     scratch=[pltpu.VMEM((tm, tn), F32)], operands=(a, b), name=name, rider=rider)
    return res[0] if rider is None else res


def _row_spec(entry, tile):
    arr, width, cb = entry if isinstance(entry, tuple) else (entry, entry.shape[1], 0)
    return arr, pl.BlockSpec((tile, width), lambda i, cb=cb: (i, cb))


def _par_spec(p):
    return pl.BlockSpec(p.shape, lambda i: (0, 0))


def row_fwd(name, fn, rows, params, outs, tile=ROW_TILE):
    arrs, specs = zip(*[_row_spec(e, tile) for e in rows])
    s = arrs[0].shape[0]
    nr, npar = len(rows), len(params)

    def body(*refs):
        r = [x[...].astype(F32) for x in refs[:nr]]
        p = [x[...] for x in refs[nr:nr + npar]]
        res = fn(*r, *p)
        for o_ref, val in zip(refs[nr + npar:], res):
            o_ref[...] = val.astype(o_ref.dtype)

    return pl.pallas_call(
        body, grid=(s // tile,), in_specs=list(specs) + [_par_spec(p) for p in params],
        out_specs=[pl.BlockSpec((tile, w), lambda i: (i, 0)) for w, _ in outs],
        out_shape=[SDS((s, w), dt) for w, dt in outs],
        compiler_params=_cparams(("parallel",)), name=name)(*arrs, *params)


def row_bwd(name, fn, rows, params, cots, row_grads, tile=ROW_TILE, primal_sum=False):
    arrs, specs = zip(*[_row_spec(e, tile) for e in rows])
    s = arrs[0].shape[0]
    nr, npar = len(rows), len(params)
    cot_entries = [e for c in cots if c is not None for e in c]
    carrs, cspecs = zip(*[_row_spec(e, tile) for e in cot_entries]) if cot_entries else ((), ())
    nc = len(cot_entries)
    want = [i for i, d in enumerate(row_grads) if d is not None]

    def body(*refs):
        r = [x[...].astype(F32) for x in refs[:nr]]
        p = [x[...] for x in refs[nr:nr + npar]]
        cvals = [x[...].astype(F32) for x in refs[nr + npar:nr + npar + nc]]
        outs = refs[nr + npar + nc:]
        prim, vjp = jax.vjp(fn, *r, *p)
        ct, pos = [], 0
        for c, pr in zip(cots, prim):
            if c is None:
                ct.append(jnp.ones_like(pr))
            else:
                acc = cvals[pos]
                for extra in cvals[pos + 1:pos + len(c)]:
                    acc = acc + extra
                pos += len(c)
                ct.append(acc)
        grads = vjp(tuple(ct))
        for o_ref, i in zip(outs[:len(want)], want):
            o_ref[...] = grads[i].astype(o_ref.dtype)
        acc_refs = outs[len(want):]
        vals = [grads[nr + j] for j in range(npar)]
        if primal_sum:
            vals.append(jnp.sum(prim[0], axis=0, keepdims=True))
        first = pl.program_id(0) == 0

        @pl.when(first)
        def _():
            for a_ref, v in zip(acc_refs, vals):
                a_ref[...] = v

        @pl.when(jnp.logical_not(first))
        def _():
            for a_ref, v in zip(acc_refs, vals):
                a_ref[...] += v

    widths = [(e[1] if isinstance(e, tuple) else e.shape[1]) for e in rows]
    out_specs = [pl.BlockSpec((tile, widths[i]), lambda i_: (i_, 0)) for i in want]
    out_shape = [SDS((s, widths[i]), row_grads[i]) for i in want]
    pshapes = [p.shape for p in params]
    if primal_sum:
        pshapes.append((1, widths[0]))
    out_specs += [pl.BlockSpec(sh, lambda i_: (0, 0)) for sh in pshapes]
    out_shape += [SDS(sh, F32) for sh in pshapes]
    res = pl.pallas_call(
        body, grid=(s // tile,), in_specs=list(specs) + [_par_spec(p) for p in params] + list(cspecs),
        out_specs=out_specs, out_shape=out_shape,
        compiler_params=_cparams(("arbitrary",)), name=name)(*arrs, *params, *carrs)
    return res[:len(want)], res[len(want):]


def _rms(x, w):
    return x * lax.rsqrt(jnp.mean(x * x, axis=-1, keepdims=True) + RMS_EPS) * w


def _sigmoid(x):
    return jax.nn.sigmoid(x)


def _softplus(x):
    return jnp.maximum(x, 0.0) + jnp.log1p(jnp.exp(-jnp.abs(x)))


def f_norm1(x, w):
    return (_rms(x, w),)


def f_dt(raw, bias):
    return (_softplus(raw + bias),)


def f_merge(gp_ssd, gp_sb, yb_ssd, yb_sb, b_ssd, b_sb):
    return (_sigmoid(gp_ssd + b_ssd) * yb_ssd + _sigmoid(gp_sb + b_sb) * yb_sb,)


def f_mix_out(x, mo, w_post, w_pre):
    h1 = x + _rms(mo, w_post)
    return h1, _rms(h1, w_pre)


def f_relu2(a1):
    return (jnp.square(jnp.maximum(a1, 0.0)),)


def f_ffn_out(h1, ff, w):
    return (h1 + _rms(ff, w),)


def f_ple_loss(h2, pg, pe, tgt, w):
    h3 = h2 + _rms(_sigmoid(pg) * pe, w)
    return (0.5 * jnp.square(h3 - tgt) * (1.0 / h2.shape[-1]),)


def _shift_down(u, d, rows):
    return u if d == 0 else jnp.where(rows >= d, pltpu.roll(u, d, 0), 0.0)


def _shift_up(u, d, rows):
    s = u.shape[0]
    return u if d == 0 else jnp.where(rows < s - d, pltpu.roll(u, s - d, 0), 0.0)


def conv_fwd(proj, col0, cd, conv_w, conv_b, name):
    s = proj.shape[0]
    cb0 = col0 // CONV_COLS

    def body(u_ref, w_ref, b_ref, o_ref):
        u = u_ref[...]
        rows = lax.broadcasted_iota(jnp.int32, u.shape, 0)
        y = jnp.broadcast_to(b_ref[...], u.shape)
        for k in range(CONV_K):
            y = y + w_ref[k:k + 1, :] * _shift_down(u, CONV_K - 1 - k, rows)
        o_ref[...] = y * _sigmoid(y)

    return pl.pallas_call(
        body, grid=(cd // CONV_COLS,),
        in_specs=[pl.BlockSpec((s, CONV_COLS), lambda i: (0, cb0 + i)),
                  pl.BlockSpec((CONV_K, CONV_COLS), lambda i: (0, i)),
                  pl.BlockSpec((1, CONV_COLS), lambda i: (0, i))],
        out_specs=pl.BlockSpec((s, CONV_COLS), lambda i: (0, i)),
        out_shape=SDS((s, cd), F32), compiler_params=_cparams(("parallel",)), name=name)(proj, conv_w, conv_b)


def conv_bwd(proj, col0, dout, ch0, conv_w, conv_b, dproj, name):
    s = proj.shape[0]
    ncb = dout.shape[1] // CONV_COLS
    cb0 = (col0 + ch0) // CONV_COLS
    wb0 = ch0 // CONV_COLS

    def body(u_ref, g_ref, w_ref, b_ref, _, du_ref, dw_ref, db_ref):
        u = u_ref[...]
        rows = lax.broadcasted_iota(jnp.int32, u.shape, 0)
        y = jnp.broadcast_to(b_ref[...], u.shape)
        for k in range(CONV_K):
            y = y + w_ref[k:k + 1, :] * _shift_down(u, CONV_K - 1 - k, rows)
        sg = _sigmoid(y)
        dy = g_ref[...] * (sg * (1.0 + y * (1.0 - sg)))
        du = jnp.zeros_like(u)
        for k in range(CONV_K):
            d = CONV_K - 1 - k
            du = du + w_ref[k:k + 1, :] * _shift_up(dy, d, rows)
            dw_ref[k:k + 1, :] = jnp.sum(dy * _shift_down(u, d, rows), axis=0, keepdims=True)
        du_ref[...] = du.astype(du_ref.dtype)
        db_ref[...] = jnp.sum(dy, axis=0, keepdims=True)

    return pl.pallas_call(
        body, grid=(ncb,),
        in_specs=[pl.BlockSpec((s, CONV_COLS), lambda i: (0, cb0 + i)),
                  pl.BlockSpec((s, CONV_COLS), lambda i: (0, i)),
                  pl.BlockSpec((CONV_K, CONV_COLS), lambda i: (0, wb0 + i)),
                  pl.BlockSpec((1, CONV_COLS), lambda i: (0, wb0 + i)),
                  pl.BlockSpec(memory_space=pl.ANY)],
        out_specs=[pl.BlockSpec((s, CONV_COLS), lambda i: (0, cb0 + i)),
                   pl.BlockSpec((CONV_K, CONV_COLS), lambda i: (0, i)),
                   pl.BlockSpec((1, CONV_COLS), lambda i: (0, i))],
        out_shape=[SDS(dproj.shape, dproj.dtype), SDS((CONV_K, ncb * CONV_COLS), F32), SDS((1, ncb * CONV_COLS), F32)],
        input_output_aliases={4: 0},
        compiler_params=_cparams(("parallel",)), name=name)(proj, dout, conv_w, conv_b, dproj)


def _dot(a, b, dims):
    return lax.dot_general(a.astype(BF16), b.astype(BF16), (dims, ((), ())), preferred_element_type=F32)


NN = ((1,), (0,))
NT = ((1,), (1,))
TN = ((0,), (0,))


def ssd_chunk(xs, zs, nw, dtc, dtw, alogs, dsks, bm, cm, prev):
    ln = bm.shape[0]
    gw = GROUP_HEADS * HEAD_DIM
    row = lax.broadcasted_iota(jnp.int32, (ln, ln), 0)
    col = lax.broadcasted_iota(jnp.int32, (ln, ln), 1)
    causal = row >= col
    tri = causal.astype(F32)
    tri_t = (row <= col).astype(F32)
    lane_head = lax.broadcasted_iota(jnp.int32, (1, gw), 1) // HEAD_DIM
    sub_head = lax.broadcasted_iota(jnp.int32, (gw, 1), 0) // HEAD_DIM
    on_lanes = [(lane_head == r).astype(F32) for r in range(GROUP_HEADS)]
    on_rows = [(sub_head == r).astype(F32) for r in range(GROUP_HEADS)]
    cb = _dot(cm, bm, NT)
    decays, dt_full, acs_full, end_full, dsk_full, end_rows = [], 0.0, 0.0, 0.0, 0.0, 0.0
    for r in range(GROUP_HEADS):
        a = -jnp.exp(alogs[r])
        da_c = dtc[r] * a
        da_w = dtw[r] * a
        acs_c = jnp.sum(tri * da_w, axis=1, keepdims=True)
        acs_w = jnp.sum(tri_t * da_c, axis=0, keepdims=True)
        alast = jnp.sum(da_w, axis=1, keepdims=True)
        decays.append(jnp.exp(jnp.where(causal, acs_c - acs_w, -jnp.inf)))
        dt_full = dt_full + dtc[r] * on_lanes[r]
        acs_full = acs_full + acs_c * on_lanes[r]
        end_full = end_full + alast * on_lanes[r]
        dsk_full = dsk_full + dsks[r] * on_lanes[r]
        end_rows = end_rows + alast * on_rows[r]
    xd = xs * dt_full
    y = xs * dsk_full + _dot(cm, prev, NT) * jnp.exp(acs_full)
    for r in range(GROUP_HEADS):
        y = y + _dot(cb * decays[r], xd * on_lanes[r], NN)
    new_prev = prev * jnp.exp(end_rows) + _dot(xd * jnp.exp(end_full - acs_full), bm, TN)
    yg = y * (zs * _sigmoid(zs))
    rstd = lax.rsqrt(jnp.mean(yg * yg, axis=-1, keepdims=True) + RMS_EPS)
    return yg * rstd * nw, new_prev


SSD_STEP_GROUPS = 1


def _ssd_specs(dm, cidx):
    u = SSD_STEP_GROUPS
    gw = GROUP_HEADS * HEAD_DIM
    nb0 = dm.DI // D_STATE
    assert nb0 % u == 0 and dm.G % u == 0
    par = pl.BlockSpec((u, 1, GROUP_HEADS), lambda g, c: (g, 0, 0))
    return dict(
        z=pl.BlockSpec((CHUNK, u * gw), lambda g, c: (cidx(c), g)),
        xs=pl.BlockSpec((CHUNK, u * gw), lambda g, c: (cidx(c), g)),
        b=pl.BlockSpec((CHUNK, u * D_STATE), lambda g, c: (cidx(c), nb0 // u + g)),
        c=pl.BlockSpec((CHUNK, u * D_STATE), lambda g, c: (cidx(c), (nb0 + dm.G) // u + g)),
        dtc=pl.BlockSpec((u, CHUNK, GROUP_HEADS), lambda g, c: (g, cidx(c), 0)),
        dtw=pl.BlockSpec((u, GROUP_HEADS, CHUNK), lambda g, c: (g, 0, cidx(c))),
        par=par,
        nw=pl.BlockSpec((1, u * gw), lambda g, c: (0, g)),
        st=pl.BlockSpec((u, None, gw, D_STATE), lambda g, c: (g, cidx(c), 0, 0)),
    )


def _ssd_load(k, z_ref, xs_ref, b_ref, c_ref, dtc_ref, dtw_ref, alog_ref, dsk_ref, nw_ref):
    gw = GROUP_HEADS * HEAD_DIM
    wide, narrow = slice(k * gw, (k + 1) * gw), slice(k * D_STATE, (k + 1) * D_STATE)
    dtc = tuple(dtc_ref[k, :, r:r + 1] for r in range(GROUP_HEADS))
    dtw = tuple(dtw_ref[k, r:r + 1, :] for r in range(GROUP_HEADS))
    alogs = tuple(alog_ref[k, :, r:r + 1] for r in range(GROUP_HEADS))
    dsks = tuple(dsk_ref[k, :, r:r + 1] for r in range(GROUP_HEADS))
    return xs_ref[:, wide], z_ref[:, wide], nw_ref[:, wide], dtc, dtw, alogs, dsks, b_ref[:, narrow], c_ref[:, narrow]


def ssd_fwd(proj, xbc, dtc, dtw, alog, dsk, nw, dm, name, rider=None):
    nc = dm.S // CHUNK
    u = SSD_STEP_GROUPS
    gw = GROUP_HEADS * HEAD_DIM
    sp = _ssd_specs(dm, lambda c: c)

    def body(z_ref, xs_ref, b_ref, c_ref, dtc_ref, dtw_ref, alog_ref, dsk_ref, nw_ref, y_ref, st_ref, prev):
        @pl.when(pl.program_id(1) == 0)
        def _():
            prev[...] = jnp.zeros_like(prev)

        st_ref[...] = prev[...]
        for k in range(u):
            args = _ssd_load(k, z_ref, xs_ref, b_ref, c_ref, dtc_ref, dtw_ref, alog_ref, dsk_ref, nw_ref)
            out, new = ssd_chunk(*args, prev[k])
            y_ref[:, k * gw:(k + 1) * gw] = out.astype(y_ref.dtype)
            prev[k] = new

    return _grid_call(
        body, grid=(dm.G // u, nc),
        in_specs=[sp["z"], sp["xs"], sp["b"], sp["c"], sp["dtc"], sp["dtw"], sp["par"], sp["par"], sp["nw"]],
        out_specs=[sp["xs"], sp["st"]],
        out_shape=[SDS((dm.S, dm.DI), BF16), SDS((dm.G, nc, gw, D_STATE), F32)],
        scratch=[pltpu.VMEM((u, gw, D_STATE), F32)],
        operands=(proj, xbc, xbc, xbc, dtc, dtw, alog, dsk, nw), name=name, rider=rider)


def ssd_bwd(proj, xbc, dtc, dtw, alog, dsk, nw, states, dy, dm, name):
    nc = dm.S // CHUNK
    u = SSD_STEP_GROUPS
    sp = _ssd_specs(dm, lambda c: nc - 1 - c)
    gw = GROUP_HEADS * HEAD_DIM
    bc_spec = pl.BlockSpec((CHUNK, u * D_STATE), lambda g, c: (nc - 1 - c, g))

    def body(z_ref, xs_ref, b_ref, c_ref, dtc_ref, dtw_ref, alog_ref, dsk_ref, nw_ref, st_ref, dy_ref,
             dz_ref, dxs_ref, db_ref, dc_ref, ddtc_ref, ddtw_ref, dalog_ref, ddsk_ref, dnw_ref, dprev):
        first = pl.program_id(1) == 0

        @pl.when(first)
        def _():
            dprev[...] = jnp.zeros_like(dprev)

        param_grads = []
        for k in range(u):
            wide, narrow = slice(k * gw, (k + 1) * gw), slice(k * D_STATE, (k + 1) * D_STATE)
            args = _ssd_load(k, z_ref, xs_ref, b_ref, c_ref, dtc_ref, dtw_ref, alog_ref, dsk_ref, nw_ref)
            _, vjp = jax.vjp(ssd_chunk, *args, st_ref[k])
            gxs, gzs, gnw, gdtc, gdtw, galogs, gdsks, gb, gc, gprev = vjp((dy_ref[:, wide], dprev[k]))
            dxs_ref[:, wide] = gxs
            dz_ref[:, wide] = gzs.astype(dz_ref.dtype)
            db_ref[:, narrow] = gb
            dc_ref[:, narrow] = gc
            dprev[k] = gprev
            for r in range(GROUP_HEADS):
                ddtc_ref[k, :, r:r + 1] = gdtc[r]
                ddtw_ref[k, r:r + 1, :] = gdtw[r]
            param_grads.append((wide, gnw, galogs, gdsks))

        @pl.when(first)
        def _():
            for k, (wide, gnw, galogs, gdsks) in enumerate(param_grads):
                dnw_ref[:, wide] = gnw
                for r in range(GROUP_HEADS):
                    dalog_ref[k, :, r:r + 1] = galogs[r]
                    ddsk_ref[k, :, r:r + 1] = gdsks[r]

        @pl.when(jnp.logical_not(first))
        def _():
            for k, (wide, gnw, galogs, gdsks) in enumerate(param_grads):
                dnw_ref[:, wide] += gnw
                for r in range(GROUP_HEADS):
                    dalog_ref[k, :, r:r + 1] += galogs[r]
                    ddsk_ref[k, :, r:r + 1] += gdsks[r]

    xs_out = pl.BlockSpec((CHUNK, u * gw), lambda g, c: (nc - 1 - c, g))
    return pl.pallas_call(
        body, grid=(dm.G // u, nc),
        in_specs=[sp["z"], sp["xs"], sp["b"], sp["c"], sp["dtc"], sp["dtw"], sp["par"], sp["par"], sp["nw"],
                  sp["st"], xs_out],
        out_specs=[xs_out, xs_out, bc_spec, bc_spec, sp["dtc"], sp["dtw"], sp["par"], sp["par"], sp["nw"]],
        out_shape=[SDS((dm.S, dm.NA), BF16), SDS((dm.S, dm.DI), F32), SDS((dm.S, dm.G * D_STATE), F32),
                   SDS((dm.S, dm.G * D_STATE), F32), SDS((dm.G, dm.S, GROUP_HEADS), F32), SDS((dm.G, GROUP_HEADS, dm.S), F32),
                   SDS((dm.G, 1, GROUP_HEADS), F32), SDS((dm.G, 1, GROUP_HEADS), F32), SDS((1, dm.DI), F32)],
        scratch_shapes=[pltpu.VMEM((u, gw, D_STATE), F32)],
        compiler_params=_cparams(("parallel", "arbitrary")), name=name)(
            proj, xbc, xbc, xbc, dtc, dtw, alog, dsk, nw, states, dy)


def _split_bf16(v):
    hi = v.astype(BF16)
    return hi, (v - hi.astype(F32)).astype(BF16)


def _tri(v, mat):
    hi, lo = _split_bf16(v)
    return jnp.dot(hi, mat, preferred_element_type=F32) + jnp.dot(lo, mat, preferred_element_type=F32)


def _blocks(v):
    return [v[:, b * ATT_TILE:(b + 1) * ATT_TILE] for b in range(v.shape[1] // ATT_TILE)]


def _sb_group(z, mask, run, after_mat):
    sp = jnp.maximum(z, 0.0) + jnp.log2(1.0 + jnp.exp2(-jnp.abs(z)))
    lk = -sp if mask is None else jnp.where(mask, -sp, 0.0)
    cums = [_tri(v, after_mat) for v in _blocks(lk)]
    sums = [jnp.sum(v, axis=1, keepdims=True) for v in _blocks(lk)]
    later = [None] * len(cums)
    for b in reversed(range(len(cums))):
        later[b] = run + cums[b]
        run = run + sums[b]
    ls = z - sp
    w = jnp.exp2(ls + jnp.concatenate(later, axis=1))
    if mask is not None:
        w = jnp.where(mask, w, 0.0)
    return ls, w, run


def _alive(run_a, run_b):
    return (jnp.max(jnp.maximum(run_a, run_b)) > ATT_DEAD).astype(jnp.int32)


def _window(i, jj, t):
    gw = ATT_GROUP * t
    end = (i + 1 - ATT_GROUP * jj) * t
    r0 = pl.multiple_of(jnp.maximum(end - gw, 0), t)
    rows = i * t + lax.broadcasted_iota(jnp.int32, (t, gw), 0)
    cols = r0 + lax.broadcasted_iota(jnp.int32, (t, gw), 1)
    return r0, jnp.logical_and(cols < rows, cols < end)


def _att_specs(dm, s):
    t = ATT_TILE
    qb, kb, vb = dm.off["q"] // LANES, dm.off["k"] // LANES, dm.off["v"] // LANES
    return (pl.BlockSpec((t, LANES), lambda p, i: (i, qb + p)),
            pl.BlockSpec((s, LANES), lambda p, i: (0, kb + p)),
            pl.BlockSpec((s, LANES), lambda p, i: (0, vb + p)))


def attn_fwd(proj, dm, name, rider=None):
    s, t = dm.S, ATT_TILE
    scale = HEAD_DIM ** -0.5
    hsl = [slice(h * HEAD_DIM, (h + 1) * HEAD_DIM) for h in range(2)]

    gw = ATT_GROUP * t

    def body(q_ref, k_ref, v_ref, o_ref):
        i = pl.program_id(1)
        gd = i // ATT_GROUP
        r_io = lax.broadcasted_iota(jnp.int32, (t, t), 0)
        c_io = lax.broadcasted_iota(jnp.int32, (t, t), 1)
        after_mat = (r_io > c_io).astype(BF16)
        qs = [q_ref[:, sl].astype(BF16) for sl in hsl]

        def group(jj, carry):
            r0, mask = _window(i, jj, t)
            zs = [_dot(qs[h], k_ref[pl.ds(r0, gw), hsl[h]], NT) * (scale * LOG2E) for h in range(2)]
            res = [_sb_group(zs[h], mask, carry[h][0], after_mat) for h in range(2)]
            return tuple((res[h][2], carry[h][1] + _dot(res[h][1], v_ref[pl.ds(r0, gw), hsl[h]], NN)) for h in range(2))

        zero = (jnp.zeros((t, 1), F32), jnp.zeros((t, HEAD_DIM), F32))
        carry = group(0, (zero, zero))

        def step(st):
            jj, _, c = st
            c = group(jj, c)
            return jj + 1, _alive(c[0][0], c[1][0]), c

        _, _, carry = lax.while_loop(lambda st: jnp.logical_and(st[0] <= gd, st[1] > 0), step,
                                     (jnp.int32(1), _alive(carry[0][0], carry[1][0]), carry))
        for h in range(2):
            o_ref[:, hsl[h]] = carry[h][1]

    qs_, ks_, vs_ = _att_specs(dm, s)
    res = _grid_call(body, grid=(dm.SBW // LANES, s // t), in_specs=[qs_, ks_, vs_],
                     out_specs=[pl.BlockSpec((t, LANES), lambda p, i: (i, p))], out_shape=[SDS((s, dm.SBW), F32)],
                     scratch=[], operands=(proj, proj, proj), name=name, rider=rider)
    return res[0] if rider is None else res


def attn_bwd(proj, do, dm, name, rider=None):
    s, t = dm.S, ATT_TILE
    nq = s // t
    gw = ATT_GROUP * t
    n_win = (nq - 1) // ATT_GROUP + 1
    scale = HEAD_DIM ** -0.5
    hsl = [slice(h * HEAD_DIM, (h + 1) * HEAD_DIM) for h in range(2)]

    def body(q_ref, k_ref, v_ref, do_ref, dq_ref, dk_ref, dv_ref, dk_acc, dv_acc, g_scr, s_scr):
        i = pl.program_id(1)

        @pl.when(i == 0)
        def _():
            dk_acc[...] = jnp.zeros_like(dk_acc)
            dv_acc[...] = jnp.zeros_like(dv_acc)

        gd = i // ATT_GROUP
        r_io = lax.broadcasted_iota(jnp.int32, (t, t), 0)
        c_io = lax.broadcasted_iota(jnp.int32, (t, t), 1)
        after_mat = (r_io > c_io).astype(BF16)
        before_mat = (r_io < c_io).astype(BF16)
        qs = [q_ref[:, sl].astype(BF16) for sl in hsl]
        dos = [do_ref[:, sl].astype(BF16) for sl in hsl]
        q_t = q_ref[...].T.astype(BF16)
        do_t = do_ref[...].T.astype(BF16)

        def pass1(jj, runs):
            r0, mask = _window(i, jj, t)
            zs = [_dot(qs[h], k_ref[pl.ds(r0, gw), hsl[h]], NT) * (scale * LOG2E) for h in range(2)]
            dws = [_dot(dos[h], v_ref[pl.ds(r0, gw), hsl[h]], NT) for h in range(2)]
            out = []
            for h in range(2):
                ls, w, run = _sb_group(zs[h], mask, runs[h], after_mat)
                g_scr[h, jj] = dws[h] * w
                s_scr[h, jj] = jnp.exp2(ls)
                dv_acc[hsl[h], pl.ds(r0, gw)] += _dot(do_t[hsl[h]], w, NN)
                out.append(run)
            return tuple(out)

        zero_col = jnp.zeros((t, 1), F32)
        runs = pass1(0, (zero_col, zero_col))

        def step1(st):
            jj, _, r = st
            r = pass1(jj, r)
            return jj + 1, _alive(r[0], r[1]), r

        walked, _, _ = lax.while_loop(lambda st: jnp.logical_and(st[0] <= gd, st[1] > 0), step1,
                                      (jnp.int32(1), _alive(runs[0], runs[1]), runs))

        def pass2(jj, carry):
            r0, mask = _window(i, jj, t)
            out = []
            for h in range(2):
                pre, dq = carry[h]
                gg = g_scr[h, jj]
                sig = s_scr[h, jj]
                before = []
                for v in _blocks(gg):
                    before.append(pre + _tri(v, before_mat))
                    pre = pre + jnp.sum(v, axis=1, keepdims=True)
                dz = jnp.where(mask, gg * (1.0 - sig) - jnp.concatenate(before, axis=1) * sig, 0.0)
                dz = (dz * scale).astype(BF16)
                dk_acc[hsl[h], pl.ds(r0, gw)] += _dot(q_t[hsl[h]], dz, NN)
                out.append((pre, dq + _dot(dz, k_ref[pl.ds(r0, gw), hsl[h]], NN)))
            return tuple(out)

        zero = (zero_col, jnp.zeros((t, HEAD_DIM), F32))
        carry = lax.fori_loop(1, walked, lambda n, c: pass2(walked - n, c), (zero, zero))
        carry = pass2(0, carry)
        for h in range(2):
            dq_ref[:, hsl[h]] = carry[h][1].astype(dq_ref.dtype)

        @pl.when(i == nq - 1)
        def _():
            for b in range(nq):
                rows = slice(b * t, (b + 1) * t)
                dk_ref[rows, :] = dk_acc[:, rows].T.astype(dk_ref.dtype)
                dv_ref[rows, :] = dv_acc[:, rows].T.astype(dv_ref.dtype)

    qs_, ks_, vs_ = _att_specs(dm, s)
    tile_spec = pl.BlockSpec((t, LANES), lambda p, i: (i, p))
    full_spec = pl.BlockSpec((s, LANES), lambda p, i: (0, p))
    return _grid_call(
        body, grid=(dm.SBW // LANES, nq), in_specs=[qs_, ks_, vs_, tile_spec],
        out_specs=[tile_spec, full_spec, full_spec], out_shape=[SDS((s, dm.SBW), BF16)] * 3,
        scratch=[pltpu.VMEM((LANES, s), F32), pltpu.VMEM((LANES, s), F32),
                 pltpu.VMEM((2, n_win, t, gw), F32), pltpu.VMEM((2, n_win, t, gw), F32)],
        operands=(proj, proj, proj, do), name=name, rider=rider)


def adamw(w, g, m, v, name):
    rows, width = w.shape
    tile = _pick(rows, (256, 64, 16, 8, 4))
    c1 = 1.0 / (1.0 - ADAM_B1 ** ADAM_STEP)
    c2 = 1.0 / (1.0 - ADAM_B2 ** ADAM_STEP)

    def body(w_ref, g_ref, m_ref, v_ref, d_ref, nm_ref, nv_ref):
        gg = g_ref[...]
        nm = ADAM_B1 * m_ref[...] + (1.0 - ADAM_B1) * gg
        nv = ADAM_B2 * v_ref[...] + (1.0 - ADAM_B2) * (gg * gg)
        d_ref[...] = -ADAM_LR * ((nm * c1) / (jnp.sqrt(nv * c2) + ADAM_EPS) + ADAM_WD * w_ref[...])
        nm_ref[...] = nm
        nv_ref[...] = nv

    spec = pl.BlockSpec((tile, width), lambda i: (i, 0))
    return pl.pallas_call(
        body, grid=(rows // tile,), in_specs=[spec] * 4, out_specs=[spec] * 3,
        out_shape=[SDS((rows, width), F32)] * 3, compiler_params=_cparams(("parallel",)), name=name)(w, g, m, v)


def _me():
    return lax.axis_index("x"), lax.axis_index("y"), lax.axis_index("c")


def _other_chips(x, y):
    return [(1 - x, y), (x, 1 - y), (1 - x, 1 - y)]


def gather_weights(wp):
    rows, width = wp.shape
    half = rows // 2

    def body(w_ref, out_ref, send_sems, recv_sems):
        x, y, c = _me()
        sibling = (x, y, 1 - c)
        chips = _other_chips(x, y)

        def part(cx, cy, hf):
            return out_ref.at[2 * cx + cy, hf]

        def copy(k, src, dst, to):
            return pltpu.make_async_remote_copy(src_ref=src, dst_ref=dst, send_sem=send_sems.at[k], recv_sem=recv_sems.at[k],
                                                device_id=to, device_id_type=MESH_ID)

        first = [copy(j, w_ref.at[c], part(x, y, c), (cx, cy, c)) for j, (cx, cy) in enumerate(chips)]
        for cp in first:
            cp.start()
        passed = [copy(3 + j, part(cx, cy, c), part(cx, cy, c), sibling) for j, (cx, cy) in enumerate(chips)]
        for j, (cx, cy) in enumerate(chips):
            copy(j, part(cx, cy, c), part(cx, cy, c), (x, y, c)).wait_recv()
            passed[j].start()
        for j, (cx, cy) in enumerate(chips):
            copy(3 + j, part(cx, cy, 1 - c), part(cx, cy, 1 - c), (x, y, c)).wait_recv()
        for cp in first + passed:
            cp.wait_send()

    return pl.pallas_call(
        body, out_shape=SDS((N_CHIPS, 2, half, width), wp.dtype), in_specs=[HBM_SPEC], out_specs=HBM_SPEC,
        scratch_shapes=[pltpu.SemaphoreType.DMA((6,)), pltpu.SemaphoreType.DMA((6,))],
        name="gather_weights")(wp.reshape(2, half, width)).reshape(N_CHIPS, rows, width)


class Rider(NamedTuple):
    operands: tuple
    out_shape: tuple
    scratch: tuple
    start: object
    wait: object


def _with_rider(body, rider, grid, n_in, n_out):
    if rider is None:
        return body
    r_in, r_out = len(rider.operands), len(rider.out_shape)

    def full(*refs):
        ins, refs = refs[:n_in], refs[n_in:]
        rins, refs = refs[:r_in], refs[r_in:]
        outs, refs = refs[:n_out], refs[n_out:]
        routs, refs = refs[:r_out], refs[r_out:]
        scr, rscr = refs[:len(refs) - len(rider.scratch)], refs[len(refs) - len(rider.scratch):]
        ids = [pl.program_id(k) for k in range(len(grid))]
        first = functools.reduce(jnp.logical_and, [i == 0 for i in ids])
        last = functools.reduce(jnp.logical_and, [i == g - 1 for i, g in zip(ids, grid)])

        @pl.when(first)
        def _():
            rider.start(rins, routs, rscr)

        body(*ins, *outs, *scr)

        @pl.when(last)
        def _():
            rider.wait(rins, routs, rscr)

    return full


def gather_rider(wp):
    def copies(ins, outs, scr, sending):
        (w_ref,), (o_ref,), (send_sems, recv_sems) = ins, outs, scr
        x, y, c = _me()
        return [pltpu.make_async_remote_copy(src_ref=w_ref, dst_ref=o_ref.at[2 * x + y if sending else 2 * cx + cy],
                                             send_sem=send_sems.at[j], recv_sem=recv_sems.at[j], device_id=(cx, cy, c),
                                             device_id_type=MESH_ID)
                for j, (cx, cy) in enumerate(_other_chips(x, y))]

    def start(ins, outs, scr):
        for cp in copies(ins, outs, scr, True):
            cp.start()

    def wait(ins, outs, scr):
        for cp in copies(ins, outs, scr, False):
            cp.wait()

    return Rider((wp,), (SDS((N_CHIPS,) + wp.shape, wp.dtype),),
                 (pltpu.SemaphoreType.DMA((3,)), pltpu.SemaphoreType.DMA((3,))), start, wait)


def own_slab(gathered, wp, chip_idx):
    return lax.dynamic_update_slice(gathered, wp[None], (chip_idx[0], 0, 0))


def exchange_rider(sh):
    def copies(ins, outs, scr):
        (s_ref,), (b_ref,), (send_sems, recv_sems) = ins, outs, scr
        x, y, c = _me()
        return [pltpu.make_async_remote_copy(src_ref=s_ref.at[2 * cx + cy], dst_ref=b_ref.at[j], send_sem=send_sems.at[j],
                                             recv_sem=recv_sems.at[j], device_id=(cx, cy, c), device_id_type=MESH_ID)
                for j, (cx, cy) in enumerate(_other_chips(x, y))]

    def start(ins, outs, scr):
        for cp in copies(ins, outs, scr):
            cp.start()

    def wait(ins, outs, scr):
        for cp in copies(ins, outs, scr):
            cp.wait()

    return Rider((sh,), (SDS((3,) + sh.shape[1:], sh.dtype),),
                 (pltpu.SemaphoreType.DMA((3,)), pltpu.SemaphoreType.DMA((3,))), start, wait)


def swap_halves(g, name):
    n, rows, width = g.shape
    half = rows // 2

    def body(g_ref, a_ref, send_sem, recv_sem):
        x, y, c = _me()
        cp = pltpu.make_async_remote_copy(src_ref=g_ref.at[:, pl.ds((1 - c) * half, half), :], dst_ref=a_ref,
                                          send_sem=send_sem, recv_sem=recv_sem, device_id=(x, y, 1 - c), device_id_type=MESH_ID)
        cp.start()
        cp.wait()

    return pl.pallas_call(
        body, out_shape=SDS((n, half, width), g.dtype), in_specs=[HBM_SPEC], out_specs=HBM_SPEC,
        scratch_shapes=[pltpu.SemaphoreType.DMA, pltpu.SemaphoreType.DMA], name=name)(g)


def add_half(g, a, c_idx, name):
    n, rows, width = g.shape
    half = rows // 2
    tile = half // 2
    nt = half // tile

    def body(c_ref, g_ref, a_ref, o_ref, ob_ref):
        v = g_ref[...] + a_ref[...]
        o_ref[...] = v
        ob_ref[...] = v.astype(ob_ref.dtype)

    out_spec = pl.BlockSpec((None, tile, width), lambda s, i, c_ref: (s, i, 0))
    gs = pltpu.PrefetchScalarGridSpec(
        num_scalar_prefetch=1, grid=(n, nt),
        in_specs=[pl.BlockSpec((None, tile, width), lambda s, i, c_ref: (s, c_ref[0] * nt + i, 0)), out_spec],
        out_specs=[out_spec, out_spec])
    return pl.pallas_call(body, grid_spec=gs, out_shape=[SDS((n, half, width), F32), SDS((n, half, width), BF16)],
                          compiler_params=_cparams(("parallel", "parallel")), name=name)(c_idx, g, a)


def exchange_small(small):
    def body(sm_ref, all_ref, send_sems, recv_sems, local_sem):
        x, y, c = _me()
        mine = pltpu.make_async_copy(sm_ref, all_ref.at[0], local_sem)
        mine.start()
        copies = []
        for m in range(1, N_DEV):
            peer = (x ^ ((m >> 2) & 1), y ^ ((m >> 1) & 1), c ^ (m & 1))
            copies.append(pltpu.make_async_remote_copy(
                src_ref=sm_ref, dst_ref=all_ref.at[m], send_sem=send_sems.at[m - 1], recv_sem=recv_sems.at[m - 1],
                device_id=peer, device_id_type=MESH_ID))
        for cp in copies:
            cp.start()
        for cp in copies:
            cp.wait()
        mine.wait()

    return pl.pallas_call(
        body, out_shape=SDS((N_DEV,) + small.shape, small.dtype), in_specs=[HBM_SPEC], out_specs=HBM_SPEC,
        scratch_shapes=[pltpu.SemaphoreType.DMA((N_DEV - 1,)), pltpu.SemaphoreType.DMA((N_DEV - 1,)), pltpu.SemaphoreType.DMA],
        name="exchange_small")(small)


def add_chips(sh, b, k_idx, name):
    n, hf, width = sh.shape
    tile = hf // 2

    def body(k_ref, s_ref, b0, b1, b2, o_ref):
        o_ref[...] = ((s_ref[...] + b0[...].astype(F32)) + b1[...].astype(F32)) + b2[...].astype(F32)

    def bspec(j):
        return pl.BlockSpec((None, tile, width), lambda i, k_ref, j=j: (j, i, 0))

    gs = pltpu.PrefetchScalarGridSpec(
        num_scalar_prefetch=1, grid=(hf // tile,),
        in_specs=[pl.BlockSpec((None, tile, width), lambda i, k_ref: (k_ref[0], i, 0)), bspec(0), bspec(1), bspec(2)],
        out_specs=pl.BlockSpec((tile, width), lambda i, k_ref: (i, 0)))
    return pl.pallas_call(body, grid_spec=gs, out_shape=SDS((hf, width), sh.dtype),
                          compiler_params=_cparams(("parallel",)), name=name)(k_idx, sh, b, b, b)


def sum_small(allsm, me_idx, name):
    _, rows, width = allsm.shape

    def body(me_ref, a_ref, o_ref):
        me = me_ref[0]
        acc = a_ref[me]
        for dev in range(1, N_DEV):
            acc = acc + a_ref[jnp.bitwise_xor(me, dev)]
        o_ref[...] = acc

    gs = pltpu.PrefetchScalarGridSpec(
        num_scalar_prefetch=1, grid=(1,),
        in_specs=[pl.BlockSpec((N_DEV, rows, width), lambda i, me_ref: (0, 0, 0))],
        out_specs=pl.BlockSpec((rows, width), lambda i, me_ref: (0, 0)))
    return pl.pallas_call(body, grid_spec=gs, out_shape=SDS((rows, width), allsm.dtype),
                          compiler_params=_cparams(("arbitrary",)), name=name)(me_idx, allsm)


def join_halves(t, core_idx, name):
    hf, width = t.shape

    def body(t_ref, o_ref, send_sem, recv_sem):
        x, y, c = _me()
        cp = pltpu.make_async_remote_copy(src_ref=t_ref, dst_ref=o_ref, send_sem=send_sem, recv_sem=recv_sem,
                                          device_id=(x, y, 1 - c), device_id_type=MESH_ID)
        cp.start()
        cp.wait()

    theirs = pl.pallas_call(
        body, out_shape=SDS((hf, width), t.dtype), in_specs=[HBM_SPEC], out_specs=HBM_SPEC,
        scratch_shapes=[pltpu.SemaphoreType.DMA, pltpu.SemaphoreType.DMA], name=name)(t)
    return jnp.where(core_idx[0] == 0, jnp.concatenate([t, theirs], axis=0), jnp.concatenate([theirs, t], axis=0))


ROW_PAD = 64


def _pad_rows(rows):
    return -(-rows // ROW_PAD) * ROW_PAD


def _pack_rows(names, shard_shapes, width):
    return _pad_rows(sum((shard_shapes[n][0] * shard_shapes[n][1]) // width for n in names))


def unpack_local(packed, names, shard_shapes, width):
    out, r0 = {}, 0
    for n in names:
        a, b = shard_shapes[n]
        nr = (a * b) // width
        out[n] = packed[r0:r0 + nr].reshape(a, b)
        r0 += nr
    return out


EXACT_IN_GATHER = ("conv_w",)
EXACT_TERMS = 3


def pack_gather(shards, names, width):
    parts = []
    for n in names:
        if n in EXACT_IN_GATHER:
            rest = shards[n].astype(F32)
            for _ in range(EXACT_TERMS):
                term = rest.astype(BF16)
                parts.append(term.reshape(-1, width))
                rest = rest - term.astype(F32)
        else:
            parts.append(shards[n].reshape(-1, width).astype(BF16))
    used = sum(p.shape[0] for p in parts)
    parts.append(jnp.zeros((_pad_rows(used) - used, width), BF16))
    return jnp.concatenate(parts, axis=0)


def unpack_full(gathered, names, shard_shapes, width):
    out, r0 = {}, 0
    for n in names:
        a, b = shard_shapes[n]
        terms = EXACT_TERMS if n in EXACT_IN_GATHER else 1
        nr = (a * b) // width
        pieces = []
        for j in range(N_CHIPS):
            blk = gathered[j, r0:r0 + nr].reshape(a, b)
            for t in range(1, terms):
                blk = blk.astype(F32) + gathered[j, r0 + t * nr:r0 + (t + 1) * nr].reshape(a, b).astype(F32)
            pieces.append(blk)
        out[n] = jnp.concatenate(pieces, axis=SHARD_AXIS[n])
        r0 += terms * nr
    return out


def pack_full(grads, names, shard_shapes, width, pad=True):
    total_rows = _pack_rows(names, shard_shapes, width) if pad else sum(
        (shard_shapes[n][0] * shard_shapes[n][1]) // width for n in names)
    slabs = []
    for j in range(N_CHIPS):
        parts = []
        for n in names:
            a, b = shard_shapes[n]
            ax = SHARD_AXIS[n]
            sz = (a, b)[ax]
            piece = lax.slice_in_dim(grads[n], j * sz, (j + 1) * sz, axis=ax)
            parts.append(piece.reshape(-1, width))
        used = sum(p.shape[0] for p in parts)
        parts.append(jnp.zeros((total_rows - used, width), F32))
        slabs.append(jnp.concatenate(parts, axis=0))
    return jnp.stack(slabs, axis=0)


def _small_layout(sizes, width):
    lay, r = {}, 0
    for n in SMALL_WEIGHTS:
        nr = -(-sizes[n] // width)
        lay[n] = (r, nr, sizes[n])
        r += nr
    assert r <= SMALL_ROWS
    return lay


def pack_small(vals, lay, width):
    rows = []
    for n in SMALL_WEIGHTS:
        r, nr, sz = lay[n]
        v = vals[n].reshape(-1).astype(F32)
        rows.append(jnp.pad(v, (0, nr * width - sz)).reshape(nr, width))
    used = sum(r.shape[0] for r in rows)
    rows.append(jnp.zeros((SMALL_ROWS - used, width), F32))
    return jnp.concatenate(rows, axis=0)


def unpack_small(packed, lay):
    return {n: packed[r:r + nr].reshape(-1)[:sz].reshape(1, sz) for n, (r, nr, sz) in lay.items()}


PART_B = ("w_in", "w_gate")
PART_A = ("w_ff1", "w_ff2", "w_ssd_branch", "w_sb_branch", "w_out", "w_ple_gate", "w_ple", "conv_w")
DIRECT_A = PART_A[:6]


GATHER_EARLY = ("conv_w", "w_ssd_branch", "w_sb_branch", "w_out")
GATHER_LATE = ("w_ff1", "w_ff2", "w_ple_gate", "w_ple")


class Dist(NamedTuple):
    packed_early: object
    packed_late: object
    shard_shapes: dict
    core_idx: object
    chip_idx: object


def build_w_all(w_in, w_gate, dm):
    c0 = dm.DI + dm.CD
    return jnp.concatenate(
        [w_in[:, :c0], w_in[:, c0 + dm.H:], w_gate, w_in[:, c0:c0 + dm.H],
         jnp.zeros((dm.D, DT_PAD - dm.H), w_in.dtype)], axis=1).astype(BF16)


def pack_b(w_in_shard, w_gate_shard):
    return jnp.concatenate([w_in_shard, w_gate_shard], axis=1)


def w_all_from_slabs(slabs, dm):
    n_in = (dm.NA - DT_PAD - 2 * dm.D + dm.H) // N_CHIPS
    c0 = dm.DI + dm.CD

    def w_in_cols(a, b):
        return [slabs[j, :, max(a, j * n_in) - j * n_in:min(b, (j + 1) * n_in) - j * n_in]
                for j in range(N_CHIPS) if max(a, j * n_in) < min(b, (j + 1) * n_in)]

    parts = w_in_cols(0, c0) + w_in_cols(c0 + dm.H, N_CHIPS * n_in) + [slabs[j, :, n_in:] for j in range(N_CHIPS)]
    parts += w_in_cols(c0, c0 + dm.H) + [jnp.zeros((dm.D, DT_PAD - dm.H), slabs.dtype)]
    return jnp.concatenate(parts, axis=1).astype(BF16)


def _w_in_columns(lo, hi, dm):
    c0 = dm.DI + dm.CD
    segments = [(0, c0, 0), (c0, c0 + dm.H, dm.off["dt"]), (c0 + dm.H, c0 + dm.H + 3 * dm.SBW, c0)]
    return [(w0 + max(lo, a) - a, w0 + min(hi, b) - a) for a, b, w0 in segments if max(lo, a) < min(hi, b)]


def slabs_b(dw_all, dm):
    n_in = (dm.NA - DT_PAD - 2 * dm.D + dm.H) // N_CHIPS
    n_gate = 2 * dm.D // N_CHIPS
    slabs = []
    for j in range(N_CHIPS):
        cols = _w_in_columns(j * n_in, (j + 1) * n_in, dm) + [(dm.off["gate"] + j * n_gate, dm.off["gate"] + (j + 1) * n_gate)]
        slabs.append(jnp.concatenate([dw_all[:, a:b] for a, b in cols], axis=1))
    return jnp.stack(slabs, axis=0)


def local_step(x, p, tgt, wf, sm, dm, dist=None):
    s, d = dm.S, dm.D
    off = dm.off
    c0 = dm.DI + dm.CD
    w_all = wf["w_all"] if "w_all" in wf else build_w_all(wf["w_in"], wf["w_gate"], dm)
    g = dm.G
    per_group = lambda v: v.reshape(g, 1, GROUP_HEADS)
    alog, dsk = per_group(sm["a_log"]), per_group(sm["d_skip"])
    b_gate = sm["b_gate"]
    b_ssd, b_sb = b_gate[:, :d], b_gate[:, d:]
    gcol = off["gate"] // d

    (n1,) = row_fwd("norm1", f_norm1, [x], [sm["norm_mix_pre"]], [(d, BF16)])
    proj = matmul(n1, w_all, name="in_proj")
    if dist is None:
        y_sb = attn_fwd(proj, dm, "attn_fwd")
    else:
        y_sb, gathered = attn_fwd(proj, dm, "attn_fwd", rider=gather_rider(dist.packed_early))
        gathered = own_slab(gathered, dist.packed_early, dist.chip_idx)
        wf = {**wf, **unpack_full(gathered, GATHER_EARLY, dist.shard_shapes, d)}
    xbc = conv_fwd(proj, off["xbc"], dm.CD, wf["conv_w"].astype(F32), sm["conv_b"], "conv_fwd")
    dt_raw = proj[:, off["dt"]:off["dt"] + dm.H]
    (dt,) = row_fwd("dt", f_dt, [dt_raw], [sm["dt_bias"]], [(dm.H, F32)])
    dtc = dt.reshape(s, g, GROUP_HEADS).transpose(1, 0, 2)
    dtw = dt.reshape(s, g, GROUP_HEADS).transpose(1, 2, 0)
    if dist is None:
        y_ssd, states = ssd_fwd(proj, xbc, dtc, dtw, alog, dsk, sm["ssd_norm"], dm, "ssd_fwd")
    else:
        y_ssd, states, gathered = ssd_fwd(proj, xbc, dtc, dtw, alog, dsk, sm["ssd_norm"], dm, "ssd_fwd",
                                          rider=gather_rider(dist.packed_late))
        gathered = own_slab(gathered, dist.packed_late, dist.chip_idx)
        wf = {**wf, **unpack_full(gathered, GATHER_LATE, dist.shard_shapes, d)}
    yb_ssd = matmul(y_ssd, wf["w_ssd_branch"], name="ssd_branch")
    yb_sb = matmul(y_sb, wf["w_sb_branch"], name="sb_branch")
    merge_rows = [(proj, d, gcol), (proj, d, gcol + 1), yb_ssd, yb_sb]
    (merged,) = row_fwd("merge", f_merge, merge_rows, [b_ssd, b_sb], [(d, BF16)])
    mo = matmul(merged, wf["w_out"], name="w_out")
    h1, n2 = row_fwd("mix_out", f_mix_out, [x, mo], [sm["norm_mix_post"], sm["norm_ffn_pre"]], [(d, F32), (d, BF16)])
    a1, act = matmul(n2, wf["w_ff1"], name="ff1", finish=lambda v: (v,) + f_relu2(v), out_dtype=(F32, BF16))
    ff = matmul(act, wf["w_ff2"], name="ff2")
    (h2,) = row_fwd("ffn_out", f_ffn_out, [h1, ff], [sm["norm_ffn_post"]], [(d, F32)])
    pg = matmul(h2, wf["w_ple_gate"], name="ple_gate")
    pe = matmul(p, wf["w_ple"], name="ple_emb")

    def reduce_start(gbig, tag):
        from_sibling = swap_halves(gbig, "swap_halves_" + tag)
        pair_sum, pair_sum_bf16 = add_half(gbig, from_sibling, dist.core_idx, "add_half_" + tag)
        return pair_sum, exchange_rider(pair_sum_bf16)

    def reduce_finish(pair_sum, from_chips, tag):
        my_half = add_chips(pair_sum, from_chips, dist.chip_idx, "add_chips_" + tag)
        return join_halves(my_half, dist.core_idx, "join_halves_" + tag)

    gr, reduced = {}, None
    packed_grads = [None if dist is None else jnp.zeros((N_CHIPS, _pack_rows(PART_A, dist.shard_shapes, d), d), F32)]

    def weight_grad(wname, a, dy):
        if dist is None:
            gr[wname] = matmul(a, dy, ta=True, name="d_" + wname)
            return
        row0 = 0
        for n in PART_A[:PART_A.index(wname)]:
            row0 += (dist.shard_shapes[n][0] * dist.shard_shapes[n][1]) // d
        packed_grads[0] = matmul(a, dy, ta=True, name="d_" + wname,
                                 slab=SlabOut(packed_grads[0], row0, dist.shard_shapes[wname][0], SHARD_AXIS[wname]))

    (dh2_a, dpg, dpe), (gr["norm_ple_post"], loss_cols) = row_bwd(
        "ple_loss", f_ple_loss, [h2, pg, pe, tgt], [sm["norm_ple_post"]], [None], [F32, BF16, BF16, None], primal_sum=True)
    loss = jnp.sum(loss_cols)
    gr["w_ple"] = matmul(p, dpe, ta=True, name="d_w_ple")
    weight_grad("w_ple_gate", h2, dpg)
    dh2_b = matmul(dpg, wf["w_ple_gate"], tb=True, name="d_h2")
    (dh1_a, dff), (gr["norm_ffn_post"],) = row_bwd(
        "ffn_out_bwd", f_ffn_out, [h1, ff], [sm["norm_ffn_post"]], [[dh2_a, dh2_b]], [F32, BF16])
    weight_grad("w_ff2", act, dff)
    (da1,) = matmul(dff, wf["w_ff2"], tb=True, name="d_act", sides=(a1,),
                    finish=lambda dact, a1_tile: (dact * (2.0 * jnp.maximum(a1_tile, 0.0)),), out_dtype=(BF16,))
    weight_grad("w_ff1", n2, da1)
    dn2 = matmul(da1, wf["w_ff1"], tb=True, name="d_n2")
    (dx_a, dmo), (gr["norm_mix_post"], gr["norm_ffn_pre"]) = row_bwd(
        "mix_out_bwd", f_mix_out, [x, mo], [sm["norm_mix_post"], sm["norm_ffn_pre"]], [[dh1_a], [dn2]], [F32, BF16])
    weight_grad("w_out", merged, dmo)
    dmerged = matmul(dmo, wf["w_out"], tb=True, name="d_merged")
    (dgp_ssd, dgp_sb, dyb_ssd, dyb_sb), (db_ssd, db_sb) = row_bwd(
        "merge_bwd", f_merge, merge_rows, [b_ssd, b_sb], [[dmerged]], [BF16, BF16, BF16, BF16])
    gr["b_gate"] = jnp.concatenate([db_ssd, db_sb], axis=1)
    weight_grad("w_ssd_branch", y_ssd, dyb_ssd)
    weight_grad("w_sb_branch", y_sb, dyb_sb)
    dy_ssd = matmul(dyb_ssd, wf["w_ssd_branch"], tb=True, name="d_y_ssd")
    dy_sb = matmul(dyb_sb, wf["w_sb_branch"], tb=True, name="d_y_sb")
    dproj, dxs, dbm, dcm, ddtc, ddtw, dalog, ddsk, gr["ssd_norm"] = ssd_bwd(
        proj, xbc, dtc, dtw, alog, dsk, sm["ssd_norm"], states, dy_ssd, dm, "ssd_bwd")
    gr["a_log"], gr["d_skip"] = (v.reshape(1, dm.H) for v in (dalog, ddsk))
    ddt_post = (ddtc.transpose(1, 0, 2) + ddtw.transpose(2, 0, 1)).reshape(s, dm.H)
    (ddt,), (gr["dt_bias"],) = row_bwd("dt_bwd", f_dt, [dt_raw], [sm["dt_bias"]], [[ddt_post]], [BF16])
    conv_w32 = wf["conv_w"].astype(F32)
    dproj, dw_x, dcb_x = conv_bwd(proj, off["xbc"], dxs, 0, conv_w32, sm["conv_b"], dproj, "conv_bwd_x")
    dproj, dw_b, dcb_b = conv_bwd(proj, off["xbc"], dbm, dm.DI, conv_w32, sm["conv_b"], dproj, "conv_bwd_b")
    dproj, dw_c, dcb_c = conv_bwd(proj, off["xbc"], dcm, dm.DI + g * D_STATE, conv_w32, sm["conv_b"], dproj, "conv_bwd_c")
    gr["conv_w"] = jnp.concatenate([dw_x, dw_b, dw_c], axis=1)
    gr["conv_b"] = jnp.concatenate([dcb_x, dcb_b, dcb_c], axis=1)
    if dist is None:
        dq, dk, dv = attn_bwd(proj, dy_sb, dm, "attn_bwd")
    else:
        rest = tuple(n for n in PART_A if n not in DIRECT_A)
        row0 = sum((dist.shard_shapes[n][0] * dist.shard_shapes[n][1]) // d for n in DIRECT_A)
        small_rows = pack_full(gr, rest, dist.shard_shapes, d, pad=False)
        gbig_a = lax.dynamic_update_slice(packed_grads[0], small_rows, (0, row0, 0))
        pair_sum_a, rider_a = reduce_start(gbig_a, "a")
        dq, dk, dv, from_chips_a = attn_bwd(proj, dy_sb, dm, "attn_bwd", rider=rider_a)
        reduced = unpack_local(reduce_finish(pair_sum_a, from_chips_a, "a"), PART_A, dist.shard_shapes, d)
    col = off["q"]
    for piece in (dq, dk, dv, dgp_ssd, dgp_sb, jnp.concatenate([ddt, jnp.zeros((s, DT_PAD - dm.H), BF16)], axis=1)):
        dproj = lax.dynamic_update_slice(dproj, piece, (0, col))
        col += piece.shape[1]
    dw_all = matmul(n1.T, dproj, name="d_w_all")
    if dist is None:
        gr["w_in"] = jnp.concatenate(
            [dw_all[:, :c0], dw_all[:, off["dt"]:off["dt"] + dm.H], dw_all[:, c0:c0 + 3 * dm.SBW]], axis=1)
        gr["w_gate"] = dw_all[:, off["gate"]:off["gate"] + 2 * d]
        dn1 = matmul(dproj, w_all, tb=True, name="d_n1")
    else:
        pair_sum_b, rider_b = reduce_start(slabs_b(dw_all, dm), "b")
        dn1, from_chips_b = matmul(dproj, w_all, tb=True, name="d_n1", rider=rider_b)
        reduced_b = reduce_finish(pair_sum_b, from_chips_b, "b")
        n_in = dist.shard_shapes["w_in"][1]
        reduced.update(w_in=reduced_b[:, :n_in], w_gate=reduced_b[:, n_in:])
    (grad_x,), (gr["norm_mix_pre"],) = row_bwd("norm1_bwd", lambda u, w: (_rms(u, w), u), [x], [sm["norm_mix_pre"]],
                                               [[dn1], [dx_a]], [F32])
    return loss, grad_x, gr, reduced


def kernel(x, p, norm_mix_pre, w_in, conv_w, conv_b, dt_bias, a_log, d_skip, ssd_norm, w_ssd_branch, w_sb_branch, w_gate, b_gate, w_out, norm_mix_post, norm_ffn_pre, w_ff1, w_ff2, norm_ffn_post, w_ple, w_ple_gate, norm_ple_post, loss_target, m_norm_mix_pre, m_w_in, m_conv_w, m_conv_b, m_dt_bias, m_a_log, m_d_skip, m_ssd_norm, m_w_ssd_branch, m_w_sb_branch, m_w_gate, m_b_gate, m_w_out, m_norm_mix_post, m_norm_ffn_pre, m_w_ff1, m_w_ff2, m_norm_ffn_post, m_w_ple, m_w_ple_gate, m_norm_ple_post, v_norm_mix_pre, v_w_in, v_conv_w, v_conv_b, v_dt_bias, v_a_log, v_d_skip, v_ssd_norm, v_w_ssd_branch, v_w_sb_branch, v_w_gate, v_b_gate, v_w_out, v_norm_mix_post, v_norm_ffn_pre, v_w_ff1, v_w_ff2, v_norm_ffn_post, v_w_ple, v_w_ple_gate, v_norm_ple_post):
    loc = dict(locals())
    unbatch = lambda a: a[0] if a.ndim == 3 else a
    w = {n: unbatch(loc[n]) for n in ALL_WEIGHTS}
    m = {n: unbatch(loc["m_" + n]) for n in ALL_WEIGHTS}
    v = {n: unbatch(loc["v_" + n]) for n in ALL_WEIGHTS}
    xs, ps, tgt = x[0], p[0, 0], loss_target[0]
    s, d = xs.shape
    di = w["w_ssd_branch"].shape[0] * N_CHIPS
    cd = w["conv_b"].shape[1]
    dm = Dims(S=s, D=d, DI=di, H=w["dt_bias"].shape[1], G=(cd - di) // (2 * D_STATE), CD=cd,
              SBW=w["w_sb_branch"].shape[0] * N_CHIPS, DFF=w["w_ff2"].shape[0] * N_CHIPS, PLE=ps.shape[1])
    ix, iy, ic = lax.axis_index("x"), lax.axis_index("y"), lax.axis_index("c")
    chip_idx = jnp.reshape(2 * ix + iy, (1,)).astype(jnp.int32)
    core_idx = jnp.reshape(ic, (1,)).astype(jnp.int32)
    dev_idx = jnp.reshape(4 * ix + 2 * iy + ic, (1,)).astype(jnp.int32)

    shard_shapes = {n: w[n].shape for n in BIG_WEIGHTS}
    packed_b = pack_b(w["w_in"], w["w_gate"]).astype(BF16)
    wf = {"w_all": w_all_from_slabs(own_slab(gather_weights(packed_b), packed_b, chip_idx), dm)}
    sm = {n: w[n] for n in SMALL_WEIGHTS}
    dist = Dist(packed_early=pack_gather(w, GATHER_EARLY, d), packed_late=pack_gather(w, GATHER_LATE, d),
                shard_shapes=shard_shapes, core_idx=core_idx, chip_idx=chip_idx)

    loss_part, grad_x, gr, grads = local_step(xs, ps, tgt, wf, sm, dm, dist)
    loss = lax.psum(loss_part, ("x", "y", "c"))

    lay = _small_layout({n: w[n].shape[1] for n in SMALL_WEIGHTS}, d)
    gs_red = sum_small(exchange_small(pack_small(gr, lay, d)), dev_idx, "sum_small")
    grads.update(unpack_small(gs_red, lay))
    delta, new_m, new_v = {}, {}, {}
    for n in BIG_WEIGHTS:
        delta[n], new_m[n], new_v[n] = adamw(w[n], grads[n], m[n], v[n], "adamw_" + n)
    d_sm, nm_sm, nv_sm = adamw(pack_small(w, lay, d), gs_red, pack_small(m, lay, d), pack_small(v, lay, d), "adamw_small")
    for out, packed in ((delta, d_sm), (new_m, nm_sm), (new_v, nv_sm)):
        out.update(unpack_small(packed, lay))

    def leaves(vals):
        return [vals[n][None] if n in BIG_WEIGHTS else vals[n] for n in ALL_WEIGHTS]

    return (loss, grad_x[None], *leaves(grads), *leaves(delta), *leaves(new_m), *leaves(new_v))
```

```python
import functools
import math
from typing import NamedTuple

import jax
import jax.numpy as jnp
from jax import lax
from jax.experimental import pallas as pl
from jax.experimental.pallas import tpu as pltpu

F32 = jnp.float32
BF16 = jnp.bfloat16
SDS = jax.ShapeDtypeStruct

HEAD_DIM = 64
GROUP_HEADS = 4
D_STATE = 128
CHUNK = 128
ATT_TILE = 128
ATT_GROUP = 3
ATT_DEAD = -160.0
LOG2E = 1.4426950408889634
CONV_K = 4
CONV_COLS = 128
RMS_EPS = 1e-6
LANES = 128
DT_PAD = 512
N_CHIPS = 4
N_DEV = 8
SMALL_ROWS = 16
VMEM_LIMIT = 48 * 1024 * 1024
MAX_TK = 3072
TILE_BUDGET = 40 * 1024 * 1024

ADAM_LR = 0.001
ADAM_B1 = 0.9
ADAM_B2 = 0.999
ADAM_EPS = 1e-08
ADAM_WD = 0.01
ADAM_STEP = 10

MESH_ID = pl.DeviceIdType.MESH
HBM_SPEC = pl.BlockSpec(memory_space=pltpu.HBM)

BIG_WEIGHTS = ("w_in", "conv_w", "w_ssd_branch", "w_sb_branch", "w_gate", "w_out", "w_ff1", "w_ff2", "w_ple", "w_ple_gate")
SHARD_AXIS = {"w_in": 1, "conv_w": 1, "w_ssd_branch": 0, "w_sb_branch": 0, "w_gate": 1, "w_out": 0, "w_ff1": 1,
              "w_ff2": 0, "w_ple": 1, "w_ple_gate": 0}
SMALL_WEIGHTS = ("norm_mix_pre", "conv_b", "dt_bias", "a_log", "d_skip", "ssd_norm", "b_gate", "norm_mix_post",
                 "norm_ffn_pre", "norm_ffn_post", "norm_ple_post")
ALL_WEIGHTS = ("norm_mix_pre", "w_in", "conv_w", "conv_b", "dt_bias", "a_log", "d_skip", "ssd_norm", "w_ssd_branch",
               "w_sb_branch", "w_gate", "b_gate", "w_out", "norm_mix_post", "norm_ffn_pre", "w_ff1", "w_ff2",
               "norm_ffn_post", "w_ple", "w_ple_gate", "norm_ple_post")


class Dims(NamedTuple):
    S: int
    D: int
    DI: int
    H: int
    G: int
    CD: int
    SBW: int
    DFF: int
    PLE: int

    @property
    def NA(self):
        return self.DI + self.CD + 3 * self.SBW + 2 * self.D + DT_PAD

    @property
    def off(self):
        o = {}
        o["z"] = 0
        o["xbc"] = self.DI
        o["q"] = self.DI + self.CD
        o["k"] = o["q"] + self.SBW
        o["v"] = o["k"] + self.SBW
        o["gate"] = o["v"] + self.SBW
        o["dt"] = o["gate"] + 2 * self.D
        return o


def _cparams(sem):
    return pltpu.CompilerParams(dimension_semantics=sem, vmem_limit_bytes=VMEM_LIMIT)


def _pick(n, cands):
    for c in cands:
        if n % c == 0:
            return c
    raise ValueError(f"no tile for {n}")


def _grid_call(body, *, grid, in_specs, out_specs, out_shape, scratch, operands, name, rider=None):
    if rider is None:
        sem = ("parallel",) + ("arbitrary",) * (len(grid) - 1)
        return pl.pallas_call(body, grid=grid, in_specs=in_specs, out_specs=out_specs, out_shape=out_shape,
                              scratch_shapes=scratch, compiler_params=_cparams(sem), name=name)(*operands)
    return pl.pallas_call(
        _with_rider(body, rider, grid, len(in_specs), len(out_specs)), grid=grid,
        in_specs=list(in_specs) + [HBM_SPEC] * len(rider.operands),
        out_specs=list(out_specs) + [HBM_SPEC] * len(rider.out_shape),
        out_shape=list(out_shape) + list(rider.out_shape), scratch_shapes=list(scratch) + list(rider.scratch),
        compiler_params=_cparams(("arbitrary",) * len(grid)), name=name)(*operands, *rider.operands)


class SlabOut(NamedTuple):
    buf: object
    row0: int
    rows: int
    axis: int


def matmul(a, b, *, ta=False, tb=False, out_dtype=F32, name, rider=None, slab=None, finish=None, sides=()):
    m, k = (a.shape[1], a.shape[0]) if ta else a.shape
    n, kb = b.shape if tb else (b.shape[1], b.shape[0])
    assert k == kb, (a.shape, b.shape, ta, tb)
    tm = _pick(m, (1024, 512, 256, 128))
    tn = _pick(n, (512, 256, 128))
    if slab is not None:
        tm = _pick(math.gcd(slab.rows, slab.row0), (1024, 512, 256, 128, 64, 32, 16, 8))
        tn = _pick(slab.buf.shape[2], (512, 256, 128))
    tk = max(t for t in range(LANES, min(k, MAX_TK) + 1, LANES) if k % t == 0)
    nk = k // tk
    if slab is None and finish is None and m % (2 * tm) == 0:
        tall = 2 * tm
        need = 2 * (tall * tk * a.dtype.itemsize + tk * tn * b.dtype.itemsize) + tall * tn * (4 + 2 * jnp.dtype(out_dtype).itemsize)
        if need <= TILE_BUDGET:
            tm = tall
    dims = (((0 if ta else 1,), (1 if tb else 0,)), ((), ()))

    def body(a_ref, b_ref, o_ref, acc_ref):
        part = lax.dot_general(a_ref[...].astype(BF16), b_ref[...].astype(BF16), dims, preferred_element_type=F32)
        if nk == 1:
            o_ref[...] = part.astype(o_ref.dtype)
        else:
            kk = pl.program_id(2)

            @pl.when(kk == 0)
            def _():
                acc_ref[...] = part

            @pl.when(kk > 0)
            def _():
                acc_ref[...] += part

            @pl.when(kk == nk - 1)
            def _():
                o_ref[...] = acc_ref[...].astype(o_ref.dtype)

    a_spec = pl.BlockSpec((tk, tm), lambda i, j, kk: (kk, i)) if ta else pl.BlockSpec((tm, tk), lambda i, j, kk: (i, kk))
    b_spec = pl.BlockSpec((tn, tk), lambda i, j, kk: (j, kk)) if tb else pl.BlockSpec((tk, tn), lambda i, j, kk: (kk, j))
    if slab is not None:
        width = slab.buf.shape[2]
        rb0, per_shard = slab.row0 // tm, slab.rows // tm
        if slab.axis == 0:
            where = lambda i, j, kk: (i // per_shard, rb0 + i % per_shard, j)
        else:
            where = lambda i, j, kk: (j // (width // tn), rb0 + i, j % (width // tn))
        return pl.pallas_call(
            lambda a_ref, b_ref, buf_ref, o_ref, acc_ref: body(a_ref, b_ref, o_ref, acc_ref),
            grid=(m // tm, n // tn, nk), in_specs=[a_spec, b_spec, pl.BlockSpec(memory_space=pl.ANY)],
            out_specs=pl.BlockSpec((None, tm, tn), where), out_shape=SDS(slab.buf.shape, slab.buf.dtype),
            scratch_shapes=[pltpu.VMEM((tm, tn), F32)], input_output_aliases={2: 0},
            compiler_params=_cparams(("parallel", "parallel", "arbitrary")), name=name)(a, b, slab.buf)
    tile_spec = pl.BlockSpec((tm, tn), lambda i, j, kk: (i, j))
    if finish is not None:
        def fused(*refs):
            a_ref, b_ref, side_refs = refs[0], refs[1], refs[2:2 + len(sides)]
            o_refs, acc_ref = refs[2 + len(sides):-1], refs[-1]

            def emit(acc):
                for o_ref, val in zip(o_refs, finish(acc, *[r[...] for r in side_refs])):
                    o_ref[...] = val.astype(o_ref.dtype)

            part = lax.dot_general(a_ref[...].astype(BF16), b_ref[...].astype(BF16), dims, preferred_element_type=F32)
            if nk == 1:
                emit(part)
            else:
                kk = pl.program_id(2)

                @pl.when(kk == 0)
                def _():
                    acc_ref[...] = part

                @pl.when(kk > 0)
                def _():
                    acc_ref[...] += part

                @pl.when(kk == nk - 1)
                def _():
                    emit(acc_ref[...])

        return pl.pallas_call(
            fused, grid=(m // tm, n // tn, nk), in_specs=[a_spec, b_spec] + [tile_spec] * len(sides),
            out_specs=[tile_spec] * len(out_dtype), out_shape=[SDS((m, n), dt) for dt in out_dtype],
            scratch_shapes=[pltpu.VMEM((tm, tn), F32)],
            compiler_params=_cparams(("parallel", "parallel", "arbitrary")), name=name)(a, b, *sides)
    res = _grid_call(body, grid=(m // tm, n // tn, nk), in_specs=[a_spec, b_spec],
                     out_specs=[tile_spec], out_shape=[SDS((m, n), out_dtype)],
                     scratch=[pltpu.VMEM((tm, tn), F32)], operands=(a, b), name=name, rider=rider)
    return res[0] if rider is None else res


def _row_spec(entry, tile):
    arr, width, cb = entry if isinstance(entry, tuple) else (entry, entry.shape[1], 0)
    return arr, pl.BlockSpec((tile, width), lambda i, cb=cb: (i, cb))


def _par_spec(p):
    return pl.BlockSpec(p.shape, lambda i: (0, 0))


def row_fwd(name, fn, rows, params, outs, tile=256):
    arrs, specs = zip(*[_row_spec(e, tile) for e in rows])
    s = arrs[0].shape[0]
    nr, npar = len(rows), len(params)

    def body(*refs):
        r = [x[...].astype(F32) for x in refs[:nr]]
        p = [x[...] for x in refs[nr:nr + npar]]
        res = fn(*r, *p)
        for o_ref, val in zip(refs[nr + npar:], res):
            o_ref[...] = val.astype(o_ref.dtype)

    return pl.pallas_call(
        body, grid=(s // tile,), in_specs=list(specs) + [_par_spec(p) for p in params],
        out_specs=[pl.BlockSpec((tile, w), lambda i: (i, 0)) for w, _ in outs],
        out_shape=[SDS((s, w), dt) for w, dt in outs],
        compiler_params=_cparams(("parallel",)), name=name)(*arrs, *params)


def row_bwd(name, fn, rows, params, cots, row_grads, tile=256, primal_sum=False):
    arrs, specs = zip(*[_row_spec(e, tile) for e in rows])
    s = arrs[0].shape[0]
    nr, npar = len(rows), len(params)
    cot_entries = [e for c in cots if c is not None for e in c]
    carrs, cspecs = zip(*[_row_spec(e, tile) for e in cot_entries]) if cot_entries else ((), ())
    nc = len(cot_entries)
    want = [i for i, d in enumerate(row_grads) if d is not None]

    def body(*refs):
        r = [x[...].astype(F32) for x in refs[:nr]]
        p = [x[...] for x in refs[nr:nr + npar]]
        cvals = [x[...].astype(F32) for x in refs[nr + npar:nr + npar + nc]]
        outs = refs[nr + npar + nc:]
        prim, vjp = jax.vjp(fn, *r, *p)
        ct, pos = [], 0
        for c, pr in zip(cots, prim):
            if c is None:
                ct.append(jnp.ones_like(pr))
            else:
                acc = cvals[pos]
                for extra in cvals[pos + 1:pos + len(c)]:
                    acc = acc + extra
                pos += len(c)
                ct.append(acc)
        grads = vjp(tuple(ct))
        for o_ref, i in zip(outs[:len(want)], want):
            o_ref[...] = grads[i].astype(o_ref.dtype)
        acc_refs = outs[len(want):]
        vals = [grads[nr + j] for j in range(npar)]
        if primal_sum:
            vals.append(jnp.sum(prim[0], axis=0, keepdims=True))
        first = pl.program_id(0) == 0

        @pl.when(first)
        def _():
            for a_ref, v in zip(acc_refs, vals):
                a_ref[...] = v

        @pl.when(jnp.logical_not(first))
        def _():
            for a_ref, v in zip(acc_refs, vals):
                a_ref[...] += v

    widths = [(e[1] if isinstance(e, tuple) else e.shape[1]) for e in rows]
    out_specs = [pl.BlockSpec((tile, widths[i]), lambda i_: (i_, 0)) for i in want]
    out_shape = [SDS((s, widths[i]), row_grads[i]) for i in want]
    pshapes = [p.shape for p in params]
    if primal_sum:
        pshapes.append((1, widths[0]))
    out_specs += [pl.BlockSpec(sh, lambda i_: (0, 0)) for sh in pshapes]
    out_shape += [SDS(sh, F32) for sh in pshapes]
    res = pl.pallas_call(
        body, grid=(s // tile,), in_specs=list(specs) + [_par_spec(p) for p in params] + list(cspecs),
        out_specs=out_specs, out_shape=out_shape,
        compiler_params=_cparams(("arbitrary",)), name=name)(*arrs, *params, *carrs)
    return res[:len(want)], res[len(want):]


def _rms(x, w):
    return x * lax.rsqrt(jnp.mean(x * x, axis=-1, keepdims=True) + RMS_EPS) * w


def _sigmoid(x):
    return jax.nn.sigmoid(x)


def _softplus(x):
    return jnp.maximum(x, 0.0) + jnp.log1p(jnp.exp(-jnp.abs(x)))


def f_norm1(x, w):
    return (_rms(x, w),)


def f_dt(raw, bias):
    return (_softplus(raw + bias),)


def f_merge(gp_ssd, gp_sb, yb_ssd, yb_sb, b_ssd, b_sb):
    return (_sigmoid(gp_ssd + b_ssd) * yb_ssd + _sigmoid(gp_sb + b_sb) * yb_sb,)


def f_mix_out(x, mo, w_post, w_pre):
    h1 = x + _rms(mo, w_post)
    return h1, _rms(h1, w_pre)


def f_relu2(a1):
    return (jnp.square(jnp.maximum(a1, 0.0)),)


def f_ffn_out(h1, ff, w):
    return (h1 + _rms(ff, w),)


def f_ple_loss(h2, pg, pe, tgt, w):
    h3 = h2 + _rms(_sigmoid(pg) * pe, w)
    return (0.5 * jnp.square(h3 - tgt) * (1.0 / h2.shape[-1]),)


def _shift_down(u, d, rows):
    return u if d == 0 else jnp.where(rows >= d, pltpu.roll(u, d, 0), 0.0)


def _shift_up(u, d, rows):
    s = u.shape[0]
    return u if d == 0 else jnp.where(rows < s - d, pltpu.roll(u, s - d, 0), 0.0)


def conv_fwd(proj, col0, cd, conv_w, conv_b, name):
    s = proj.shape[0]
    cb0 = col0 // CONV_COLS

    def body(u_ref, w_ref, b_ref, o_ref):
        u = u_ref[...]
        rows = lax.broadcasted_iota(jnp.int32, u.shape, 0)
        y = jnp.broadcast_to(b_ref[...], u.shape)
        for k in range(CONV_K):
            y = y + w_ref[k:k + 1, :] * _shift_down(u, CONV_K - 1 - k, rows)
        o_ref[...] = y * _sigmoid(y)

    return pl.pallas_call(
        body, grid=(cd // CONV_COLS,),
        in_specs=[pl.BlockSpec((s, CONV_COLS), lambda i: (0, cb0 + i)),
                  pl.BlockSpec((CONV_K, CONV_COLS), lambda i: (0, i)),
                  pl.BlockSpec((1, CONV_COLS), lambda i: (0, i))],
        out_specs=pl.BlockSpec((s, CONV_COLS), lambda i: (0, i)),
        out_shape=SDS((s, cd), F32), compiler_params=_cparams(("parallel",)), name=name)(proj, conv_w, conv_b)


def conv_bwd(proj, col0, dout, ch0, conv_w, conv_b, dproj, name):
    s = proj.shape[0]
    ncb = dout.shape[1] // CONV_COLS
    cb0 = (col0 + ch0) // CONV_COLS
    wb0 = ch0 // CONV_COLS

    def body(u_ref, g_ref, w_ref, b_ref, _, du_ref, dw_ref, db_ref):
        u = u_ref[...]
        rows = lax.broadcasted_iota(jnp.int32, u.shape, 0)
        y = jnp.broadcast_to(b_ref[...], u.shape)
        for k in range(CONV_K):
            y = y + w_ref[k:k + 1, :] * _shift_down(u, CONV_K - 1 - k, rows)
        sg = _sigmoid(y)
        dy = g_ref[...] * (sg * (1.0 + y * (1.0 - sg)))
        du = jnp.zeros_like(u)
        for k in range(CONV_K):
            d = CONV_K - 1 - k
            du = du + w_ref[k:k + 1, :] * _shift_up(dy, d, rows)
            dw_ref[k:k + 1, :] = jnp.sum(dy * _shift_down(u, d, rows), axis=0, keepdims=True)
        du_ref[...] = du.astype(du_ref.dtype)
        db_ref[...] = jnp.sum(dy, axis=0, keepdims=True)

    return pl.pallas_call(
        body, grid=(ncb,),
        in_specs=[pl.BlockSpec((s, CONV_COLS), lambda i: (0, cb0 + i)),
                  pl.BlockSpec((s, CONV_COLS), lambda i: (0, i)),
                  pl.BlockSpec((CONV_K, CONV_COLS), lambda i: (0, wb0 + i)),
                  pl.BlockSpec((1, CONV_COLS), lambda i: (0, wb0 + i)),
                  pl.BlockSpec(memory_space=pl.ANY)],
        out_specs=[pl.BlockSpec((s, CONV_COLS), lambda i: (0, cb0 + i)),
                   pl.BlockSpec((CONV_K, CONV_COLS), lambda i: (0, i)),
                   pl.BlockSpec((1, CONV_COLS), lambda i: (0, i))],
        out_shape=[SDS(dproj.shape, dproj.dtype), SDS((CONV_K, ncb * CONV_COLS), F32), SDS((1, ncb * CONV_COLS), F32)],
        input_output_aliases={4: 0},
        compiler_params=_cparams(("parallel",)), name=name)(proj, dout, conv_w, conv_b, dproj)


def _dot(a, b, dims):
    return lax.dot_general(a.astype(BF16), b.astype(BF16), (dims, ((), ())), preferred_element_type=F32)


NN = ((1,), (0,))
NT = ((1,), (1,))
TN = ((0,), (0,))


def ssd_chunk(xs, zs, nw, dtc, dtw, alogs, dsks, bm, cm, prev):
    ln = bm.shape[0]
    gw = GROUP_HEADS * HEAD_DIM
    row = lax.broadcasted_iota(jnp.int32, (ln, ln), 0)
    col = lax.broadcasted_iota(jnp.int32, (ln, ln), 1)
    causal = row >= col
    tri = causal.astype(F32)
    tri_t = (row <= col).astype(F32)
    lane_head = lax.broadcasted_iota(jnp.int32, (1, gw), 1) // HEAD_DIM
    sub_head = lax.broadcasted_iota(jnp.int32, (gw, 1), 0) // HEAD_DIM
    on_lanes = [(lane_head == r).astype(F32) for r in range(GROUP_HEADS)]
    on_rows = [(sub_head == r).astype(F32) for r in range(GROUP_HEADS)]
    cb = _dot(cm, bm, NT)
    decays, dt_full, acs_full, end_full, dsk_full, end_rows = [], 0.0, 0.0, 0.0, 0.0, 0.0
    for r in range(GROUP_HEADS):
        a = -jnp.exp(alogs[r])
        da_c = dtc[r] * a
        da_w = dtw[r] * a
        acs_c = jnp.sum(tri * da_w, axis=1, keepdims=True)
        acs_w = jnp.sum(tri_t * da_c, axis=0, keepdims=True)
        alast = jnp.sum(da_w, axis=1, keepdims=True)
        decays.append(jnp.exp(jnp.where(causal, acs_c - acs_w, -jnp.inf)))
        dt_full = dt_full + dtc[r] * on_lanes[r]
        acs_full = acs_full + acs_c * on_lanes[r]
        end_full = end_full + alast * on_lanes[r]
        dsk_full = dsk_full + dsks[r] * on_lanes[r]
        end_rows = end_rows + alast * on_rows[r]
    xd = xs * dt_full
    y = xs * dsk_full + _dot(cm, prev, NT) * jnp.exp(acs_full)
    for r in range(GROUP_HEADS):
        y = y + _dot(cb * decays[r], xd * on_lanes[r], NN)
    new_prev = prev * jnp.exp(end_rows) + _dot(xd * jnp.exp(end_full - acs_full), bm, TN)
    yg = y * (zs * _sigmoid(zs))
    rstd = lax.rsqrt(jnp.mean(yg * yg, axis=-1, keepdims=True) + RMS_EPS)
    return yg * rstd * nw, new_prev


SSD_STEP_GROUPS = 1


def _ssd_specs(dm, cidx):
    u = SSD_STEP_GROUPS
    gw = GROUP_HEADS * HEAD_DIM
    nb0 = dm.DI // D_STATE
    assert nb0 % u == 0 and dm.G % u == 0
    par = pl.BlockSpec((u, 1, GROUP_HEADS), lambda g, c: (g, 0, 0))
    return dict(
        z=pl.BlockSpec((CHUNK, u * gw), lambda g, c: (cidx(c), g)),
        xs=pl.BlockSpec((CHUNK, u * gw), lambda g, c: (cidx(c), g)),
        b=pl.BlockSpec((CHUNK, u * D_STATE), lambda g, c: (cidx(c), nb0 // u + g)),
        c=pl.BlockSpec((CHUNK, u * D_STATE), lambda g, c: (cidx(c), (nb0 + dm.G) // u + g)),
        dtc=pl.BlockSpec((u, CHUNK, GROUP_HEADS), lambda g, c: (g, cidx(c), 0)),
        dtw=pl.BlockSpec((u, GROUP_HEADS, CHUNK), lambda g, c: (g, 0, cidx(c))),
        par=par,
        nw=pl.BlockSpec((1, u * gw), lambda g, c: (0, g)),
        st=pl.BlockSpec((u, None, gw, D_STATE), lambda g, c: (g, cidx(c), 0, 0)),
    )


def _ssd_load(k, z_ref, xs_ref, b_ref, c_ref, dtc_ref, dtw_ref, alog_ref, dsk_ref, nw_ref):
    gw = GROUP_HEADS * HEAD_DIM
    wide, narrow = slice(k * gw, (k + 1) * gw), slice(k * D_STATE, (k + 1) * D_STATE)
    dtc = tuple(dtc_ref[k, :, r:r + 1] for r in range(GROUP_HEADS))
    dtw = tuple(dtw_ref[k, r:r + 1, :] for r in range(GROUP_HEADS))
    alogs = tuple(alog_ref[k, :, r:r + 1] for r in range(GROUP_HEADS))
    dsks = tuple(dsk_ref[k, :, r:r + 1] for r in range(GROUP_HEADS))
    return xs_ref[:, wide], z_ref[:, wide], nw_ref[:, wide], dtc, dtw, alogs, dsks, b_ref[:, narrow], c_ref[:, narrow]


def ssd_fwd(proj, xbc, dtc, dtw, alog, dsk, nw, dm, name, rider=None):
    nc = dm.S // CHUNK
    u = SSD_STEP_GROUPS
    gw = GROUP_HEADS * HEAD_DIM
    sp = _ssd_specs(dm, lambda c: c)

    def body(z_ref, xs_ref, b_ref, c_ref, dtc_ref, dtw_ref, alog_ref, dsk_ref, nw_ref, y_ref, st_ref, prev):
        @pl.when(pl.program_id(1) == 0)
        def _():
            prev[...] = jnp.zeros_like(prev)

        st_ref[...] = prev[...]
        for k in range(u):
            args = _ssd_load(k, z_ref, xs_ref, b_ref, c_ref, dtc_ref, dtw_ref, alog_ref, dsk_ref, nw_ref)
            out, new = ssd_chunk(*args, prev[k])
            y_ref[:, k * gw:(k + 1) * gw] = out.astype(y_ref.dtype)
            prev[k] = new

    return _grid_call(
        body, grid=(dm.G // u, nc),
        in_specs=[sp["z"], sp["xs"], sp["b"], sp["c"], sp["dtc"], sp["dtw"], sp["par"], sp["par"], sp["nw"]],
        out_specs=[sp["xs"], sp["st"]],
        out_shape=[SDS((dm.S, dm.DI), BF16), SDS((dm.G, nc, gw, D_STATE), F32)],
        scratch=[pltpu.VMEM((u, gw, D_STATE), F32)],
        operands=(proj, xbc, xbc, xbc, dtc, dtw, alog, dsk, nw), name=name, rider=rider)


def ssd_bwd(proj, xbc, dtc, dtw, alog, dsk, nw, states, dy, dm, name):
    nc = dm.S // CHUNK
    u = SSD_STEP_GROUPS
    sp = _ssd_specs(dm, lambda c: nc - 1 - c)
    gw = GROUP_HEADS * HEAD_DIM
    bc_spec = pl.BlockSpec((CHUNK, u * D_STATE), lambda g, c: (nc - 1 - c, g))

    def body(z_ref, xs_ref, b_ref, c_ref, dtc_ref, dtw_ref, alog_ref, dsk_ref, nw_ref, st_ref, dy_ref,
             dz_ref, dxs_ref, db_ref, dc_ref, ddtc_ref, ddtw_ref, dalog_ref, ddsk_ref, dnw_ref, dprev):
        first = pl.program_id(1) == 0

        @pl.when(first)
        def _():
            dprev[...] = jnp.zeros_like(dprev)

        param_grads = []
        for k in range(u):
            wide, narrow = slice(k * gw, (k + 1) * gw), slice(k * D_STATE, (k + 1) * D_STATE)
            args = _ssd_load(k, z_ref, xs_ref, b_ref, c_ref, dtc_ref, dtw_ref, alog_ref, dsk_ref, nw_ref)
            _, vjp = jax.vjp(ssd_chunk, *args, st_ref[k])
            gxs, gzs, gnw, gdtc, gdtw, galogs, gdsks, gb, gc, gprev = vjp((dy_ref[:, wide], dprev[k]))
            dxs_ref[:, wide] = gxs
            dz_ref[:, wide] = gzs.astype(dz_ref.dtype)
            db_ref[:, narrow] = gb
            dc_ref[:, narrow] = gc
            dprev[k] = gprev
            for r in range(GROUP_HEADS):
                ddtc_ref[k, :, r:r + 1] = gdtc[r]
                ddtw_ref[k, r:r + 1, :] = gdtw[r]
            param_grads.append((wide, gnw, galogs, gdsks))

        @pl.when(first)
        def _():
            for k, (wide, gnw, galogs, gdsks) in enumerate(param_grads):
                dnw_ref[:, wide] = gnw
                for r in range(GROUP_HEADS):
                    dalog_ref[k, :, r:r + 1] = galogs[r]
                    ddsk_ref[k, :, r:r + 1] = gdsks[r]

        @pl.when(jnp.logical_not(first))
        def _():
            for k, (wide, gnw, galogs, gdsks) in enumerate(param_grads):
                dnw_ref[:, wide] += gnw
                for r in range(GROUP_HEADS):
                    dalog_ref[k, :, r:r + 1] += galogs[r]
                    ddsk_ref[k, :, r:r + 1] += gdsks[r]

    xs_out = pl.BlockSpec((CHUNK, u * gw), lambda g, c: (nc - 1 - c, g))
    return pl.pallas_call(
        body, grid=(dm.G // u, nc),
        in_specs=[sp["z"], sp["xs"], sp["b"], sp["c"], sp["dtc"], sp["dtw"], sp["par"], sp["par"], sp["nw"],
                  sp["st"], xs_out],
        out_specs=[xs_out, xs_out, bc_spec, bc_spec, sp["dtc"], sp["dtw"], sp["par"], sp["par"], sp["nw"]],
        out_shape=[SDS((dm.S, dm.NA), BF16), SDS((dm.S, dm.DI), F32), SDS((dm.S, dm.G * D_STATE), F32),
                   SDS((dm.S, dm.G * D_STATE), F32), SDS((dm.G, dm.S, GROUP_HEADS), F32), SDS((dm.G, GROUP_HEADS, dm.S), F32),
                   SDS((dm.G, 1, GROUP_HEADS), F32), SDS((dm.G, 1, GROUP_HEADS), F32), SDS((1, dm.DI), F32)],
        scratch_shapes=[pltpu.VMEM((u, gw, D_STATE), F32)],
        compiler_params=_cparams(("parallel", "arbitrary")), name=name)(
            proj, xbc, xbc, xbc, dtc, dtw, alog, dsk, nw, states, dy)


def _split_bf16(v):
    hi = v.astype(BF16)
    return hi, (v - hi.astype(F32)).astype(BF16)


def _tri(v, mat):
    hi, lo = _split_bf16(v)
    return jnp.dot(hi, mat, preferred_element_type=F32) + jnp.dot(lo, mat, preferred_element_type=F32)


def _blocks(v):
    return [v[:, b * ATT_TILE:(b + 1) * ATT_TILE] for b in range(v.shape[1] // ATT_TILE)]


def _sb_group(z, mask, run, after_mat):
    sp = jnp.maximum(z, 0.0) + jnp.log2(1.0 + jnp.exp2(-jnp.abs(z)))
    lk = -sp if mask is None else jnp.where(mask, -sp, 0.0)
    cums = [_tri(v, after_mat) for v in _blocks(lk)]
    sums = [jnp.sum(v, axis=1, keepdims=True) for v in _blocks(lk)]
    later = [None] * len(cums)
    for b in reversed(range(len(cums))):
        later[b] = run + cums[b]
        run = run + sums[b]
    ls = z - sp
    w = jnp.exp2(ls + jnp.concatenate(later, axis=1))
    if mask is not None:
        w = jnp.where(mask, w, 0.0)
    return ls, w, run


def _alive(run_a, run_b):
    return (jnp.max(jnp.maximum(run_a, run_b)) > ATT_DEAD).astype(jnp.int32)


def _window(i, jj, t):
    gw = ATT_GROUP * t
    end = (i + 1 - ATT_GROUP * jj) * t
    r0 = pl.multiple_of(jnp.maximum(end - gw, 0), t)
    rows = i * t + lax.broadcasted_iota(jnp.int32, (t, gw), 0)
    cols = r0 + lax.broadcasted_iota(jnp.int32, (t, gw), 1)
    return r0, jnp.logical_and(cols < rows, cols < end)


def _att_specs(dm, s):
    t = ATT_TILE
    qb, kb, vb = dm.off["q"] // LANES, dm.off["k"] // LANES, dm.off["v"] // LANES
    return (pl.BlockSpec((t, LANES), lambda p, i: (i, qb + p)),
            pl.BlockSpec((s, LANES), lambda p, i: (0, kb + p)),
            pl.BlockSpec((s, LANES), lambda p, i: (0, vb + p)))


def attn_fwd(proj, dm, name, rider=None):
    s, t = dm.S, ATT_TILE
    scale = HEAD_DIM ** -0.5
    hsl = [slice(h * HEAD_DIM, (h + 1) * HEAD_DIM) for h in range(2)]

    gw = ATT_GROUP * t

    def body(q_ref, k_ref, v_ref, o_ref):
        i = pl.program_id(1)
        gd = i // ATT_GROUP
        r_io = lax.broadcasted_iota(jnp.int32, (t, t), 0)
        c_io = lax.broadcasted_iota(jnp.int32, (t, t), 1)
        after_mat = (r_io > c_io).astype(BF16)
        qs = [q_ref[:, sl].astype(BF16) for sl in hsl]

        def group(jj, carry):
            r0, mask = _window(i, jj, t)
            zs = [_dot(qs[h], k_ref[pl.ds(r0, gw), hsl[h]], NT) * (scale * LOG2E) for h in range(2)]
            res = [_sb_group(zs[h], mask, carry[h][0], after_mat) for h in range(2)]
            return tuple((res[h][2], carry[h][1] + _dot(res[h][1], v_ref[pl.ds(r0, gw), hsl[h]], NN)) for h in range(2))

        zero = (jnp.zeros((t, 1), F32), jnp.zeros((t, HEAD_DIM), F32))
        carry = group(0, (zero, zero))

        def step(st):
            jj, _, c = st
            c = group(jj, c)
            return jj + 1, _alive(c[0][0], c[1][0]), c

        _, _, carry = lax.while_loop(lambda st: jnp.logical_and(st[0] <= gd, st[1] > 0), step,
                                     (jnp.int32(1), _alive(carry[0][0], carry[1][0]), carry))
        for h in range(2):
            o_ref[:, hsl[h]] = carry[h][1]

    qs_, ks_, vs_ = _att_specs(dm, s)
    res = _grid_call(body, grid=(dm.SBW // LANES, s // t), in_specs=[qs_, ks_, vs_],
                     out_specs=[pl.BlockSpec((t, LANES), lambda p, i: (i, p))], out_shape=[SDS((s, dm.SBW), F32)],
                     scratch=[], operands=(proj, proj, proj), name=name, rider=rider)
    return res[0] if rider is None else res


def attn_bwd(proj, do, dm, name, rider=None):
    s, t = dm.S, ATT_TILE
    nq = s // t
    gw = ATT_GROUP * t
    n_win = (nq - 1) // ATT_GROUP + 1
    scale = HEAD_DIM ** -0.5
    hsl = [slice(h * HEAD_DIM, (h + 1) * HEAD_DIM) for h in range(2)]

    def body(q_ref, k_ref, v_ref, do_ref, dq_ref, dk_ref, dv_ref, dk_acc, dv_acc, g_scr, s_scr):
        i = pl.program_id(1)

        @pl.when(i == 0)
        def _():
            dk_acc[...] = jnp.zeros_like(dk_acc)
            dv_acc[...] = jnp.zeros_like(dv_acc)

        gd = i // ATT_GROUP
        r_io = lax.broadcasted_iota(jnp.int32, (t, t), 0)
        c_io = lax.broadcasted_iota(jnp.int32, (t, t), 1)
        after_mat = (r_io > c_io).astype(BF16)
        before_mat = (r_io < c_io).astype(BF16)
        qs = [q_ref[:, sl].astype(BF16) for sl in hsl]
        dos = [do_ref[:, sl].astype(BF16) for sl in hsl]
        q_t = q_ref[...].T.astype(BF16)
        do_t = do_ref[...].T.astype(BF16)

        def pass1(jj, runs):
            r0, mask = _window(i, jj, t)
            zs = [_dot(qs[h], k_ref[pl.ds(r0, gw), hsl[h]], NT) * (scale * LOG2E) for h in range(2)]
            dws = [_dot(dos[h], v_ref[pl.ds(r0, gw), hsl[h]], NT) for h in range(2)]
            out = []
            for h in range(2):
                ls, w, run = _sb_group(zs[h], mask, runs[h], after_mat)
                g_scr[h, jj] = dws[h] * w
                s_scr[h, jj] = jnp.exp2(ls)
                dv_acc[hsl[h], pl.ds(r0, gw)] += _dot(do_t[hsl[h]], w, NN)
                out.append(run)
            return tuple(out)

        zero_col = jnp.zeros((t, 1), F32)
        runs = pass1(0, (zero_col, zero_col))

        def step1(st):
            jj, _, r = st
            r = pass1(jj, r)
            return jj + 1, _alive(r[0], r[1]), r

        walked, _, _ = lax.while_loop(lambda st: jnp.logical_and(st[0] <= gd, st[1] > 0), step1,
                                      (jnp.int32(1), _alive(runs[0], runs[1]), runs))

        def pass2(jj, carry):
            r0, mask = _window(i, jj, t)
            out = []
            for h in range(2):
                pre, dq = carry[h]
                gg = g_scr[h, jj]
                sig = s_scr[h, jj]
                before = []
                for v in _blocks(gg):
                    before.append(pre + _tri(v, before_mat))
                    pre = pre + jnp.sum(v, axis=1, keepdims=True)
                dz = jnp.where(mask, gg * (1.0 - sig) - jnp.concatenate(before, axis=1) * sig, 0.0)
                dz = (dz * scale).astype(BF16)
                dk_acc[hsl[h], pl.ds(r0, gw)] += _dot(q_t[hsl[h]], dz, NN)
                out.append((pre, dq + _dot(dz, k_ref[pl.ds(r0, gw), hsl[h]], NN)))
            return tuple(out)

        zero = (zero_col, jnp.zeros((t, HEAD_DIM), F32))
        carry = lax.fori_loop(1, walked, lambda n, c: pass2(walked - n, c), (zero, zero))
        carry = pass2(0, carry)
        for h in range(2):
            dq_ref[:, hsl[h]] = carry[h][1].astype(dq_ref.dtype)

        @pl.when(i == nq - 1)
        def _():
            for b in range(nq):
                rows = slice(b * t, (b + 1) * t)
                dk_ref[rows, :] = dk_acc[:, rows].T.astype(dk_ref.dtype)
                dv_ref[rows, :] = dv_acc[:, rows].T.astype(dv_ref.dtype)

    qs_, ks_, vs_ = _att_specs(dm, s)
    tile_spec = pl.BlockSpec((t, LANES), lambda p, i: (i, p))
    full_spec = pl.BlockSpec((s, LANES), lambda p, i: (0, p))
    return _grid_call(
        body, grid=(dm.SBW // LANES, nq), in_specs=[qs_, ks_, vs_, tile_spec],
        out_specs=[tile_spec, full_spec, full_spec], out_shape=[SDS((s, dm.SBW), BF16)] * 3,
        scratch=[pltpu.VMEM((LANES, s), F32), pltpu.VMEM((LANES, s), F32),
                 pltpu.VMEM((2, n_win, t, gw), F32), pltpu.VMEM((2, n_win, t, gw), F32)],
        operands=(proj, proj, proj, do), name=name, rider=rider)


def adamw(w, g, m, v, name):
    rows, width = w.shape
    tile = _pick(rows, (256, 64, 16, 8, 4))
    c1 = 1.0 / (1.0 - ADAM_B1 ** ADAM_STEP)
    c2 = 1.0 / (1.0 - ADAM_B2 ** ADAM_STEP)

    def body(w_ref, g_ref, m_ref, v_ref, d_ref, nm_ref, nv_ref):
        gg = g_ref[...]
        nm = ADAM_B1 * m_ref[...] + (1.0 - ADAM_B1) * gg
        nv = ADAM_B2 * v_ref[...] + (1.0 - ADAM_B2) * (gg * gg)
        d_ref[...] = -ADAM_LR * ((nm * c1) / (jnp.sqrt(nv * c2) + ADAM_EPS) + ADAM_WD * w_ref[...])
        nm_ref[...] = nm
        nv_ref[...] = nv

    spec = pl.BlockSpec((tile, width), lambda i: (i, 0))
    return pl.pallas_call(
        body, grid=(rows // tile,), in_specs=[spec] * 4, out_specs=[spec] * 3,
        out_shape=[SDS((rows, width), F32)] * 3, compiler_params=_cparams(("parallel",)), name=name)(w, g, m, v)


def _me():
    return lax.axis_index("x"), lax.axis_index("y"), lax.axis_index("c")


def _other_chips(x, y):
    return [(1 - x, y), (x, 1 - y), (1 - x, 1 - y)]


def gather_weights(wp):
    rows, width = wp.shape
    half = rows // 2

    def body(w_ref, out_ref, send_sems, recv_sems):
        x, y, c = _me()
        sibling = (x, y, 1 - c)
        chips = _other_chips(x, y)

        def part(cx, cy, hf):
            return out_ref.at[2 * cx + cy, hf]

        def copy(k, src, dst, to):
            return pltpu.make_async_remote_copy(src_ref=src, dst_ref=dst, send_sem=send_sems.at[k], recv_sem=recv_sems.at[k],
                                                device_id=to, device_id_type=MESH_ID)

        first = [copy(j, w_ref.at[c], part(x, y, c), (cx, cy, c)) for j, (cx, cy) in enumerate(chips)]
        for cp in first:
            cp.start()
        passed = [copy(3 + j, part(cx, cy, c), part(cx, cy, c), sibling) for j, (cx, cy) in enumerate(chips)]
        for j, (cx, cy) in enumerate(chips):
            copy(j, part(cx, cy, c), part(cx, cy, c), (x, y, c)).wait_recv()
            passed[j].start()
        for j, (cx, cy) in enumerate(chips):
            copy(3 + j, part(cx, cy, 1 - c), part(cx, cy, 1 - c), (x, y, c)).wait_recv()
        for cp in first + passed:
            cp.wait_send()

    return pl.pallas_call(
        body, out_shape=SDS((N_CHIPS, 2, half, width), wp.dtype), in_specs=[HBM_SPEC], out_specs=HBM_SPEC,
        scratch_shapes=[pltpu.SemaphoreType.DMA((6,)), pltpu.SemaphoreType.DMA((6,))],
        name="gather_weights")(wp.reshape(2, half, width)).reshape(N_CHIPS, rows, width)


class Rider(NamedTuple):
    operands: tuple
    out_shape: tuple
    scratch: tuple
    start: object
    wait: object


def _with_rider(body, rider, grid, n_in, n_out):
    if rider is None:
        return body
    r_in, r_out = len(rider.operands), len(rider.out_shape)

    def full(*refs):
        ins, refs = refs[:n_in], refs[n_in:]
        rins, refs = refs[:r_in], refs[r_in:]
        outs, refs = refs[:n_out], refs[n_out:]
        routs, refs = refs[:r_out], refs[r_out:]
        scr, rscr = refs[:len(refs) - len(rider.scratch)], refs[len(refs) - len(rider.scratch):]
        ids = [pl.program_id(k) for k in range(len(grid))]
        first = functools.reduce(jnp.logical_and, [i == 0 for i in ids])
        last = functools.reduce(jnp.logical_and, [i == g - 1 for i, g in zip(ids, grid)])

        @pl.when(first)
        def _():
            rider.start(rins, routs, rscr)

        body(*ins, *outs, *scr)

        @pl.when(last)
        def _():
            rider.wait(rins, routs, rscr)

    return full


def gather_rider(wp):
    def copies(ins, outs, scr, sending):
        (w_ref,), (o_ref,), (send_sems, recv_sems) = ins, outs, scr
        x, y, c = _me()
        return [pltpu.make_async_remote_copy(src_ref=w_ref, dst_ref=o_ref.at[2 * x + y if sending else 2 * cx + cy],
                                             send_sem=send_sems.at[j], recv_sem=recv_sems.at[j], device_id=(cx, cy, c),
                                             device_id_type=MESH_ID)
                for j, (cx, cy) in enumerate(_other_chips(x, y))]

    def start(ins, outs, scr):
        for cp in copies(ins, outs, scr, True):
            cp.start()

    def wait(ins, outs, scr):
        for cp in copies(ins, outs, scr, False):
            cp.wait()

    return Rider((wp,), (SDS((N_CHIPS,) + wp.shape, wp.dtype),),
                 (pltpu.SemaphoreType.DMA((3,)), pltpu.SemaphoreType.DMA((3,))), start, wait)


def own_slab(gathered, wp, chip_idx):
    return lax.dynamic_update_slice(gathered, wp[None], (chip_idx[0], 0, 0))


def exchange_rider(sh):
    def copies(ins, outs, scr):
        (s_ref,), (b_ref,), (send_sems, recv_sems) = ins, outs, scr
        x, y, c = _me()
        return [pltpu.make_async_remote_copy(src_ref=s_ref.at[2 * cx + cy], dst_ref=b_ref.at[j], send_sem=send_sems.at[j],
                                             recv_sem=recv_sems.at[j], device_id=(cx, cy, c), device_id_type=MESH_ID)
                for j, (cx, cy) in enumerate(_other_chips(x, y))]

    def start(ins, outs, scr):
        for cp in copies(ins, outs, scr):
            cp.start()

    def wait(ins, outs, scr):
        for cp in copies(ins, outs, scr):
            cp.wait()

    return Rider((sh,), (SDS((3,) + sh.shape[1:], sh.dtype),),
                 (pltpu.SemaphoreType.DMA((3,)), pltpu.SemaphoreType.DMA((3,))), start, wait)


def swap_halves(g, name):
    n, rows, width = g.shape
    half = rows // 2

    def body(g_ref, a_ref, send_sem, recv_sem):
        x, y, c = _me()
        cp = pltpu.make_async_remote_copy(src_ref=g_ref.at[:, pl.ds((1 - c) * half, half), :], dst_ref=a_ref,
                                          send_sem=send_sem, recv_sem=recv_sem, device_id=(x, y, 1 - c), device_id_type=MESH_ID)
        cp.start()
        cp.wait()

    return pl.pallas_call(
        body, out_shape=SDS((n, half, width), g.dtype), in_specs=[HBM_SPEC], out_specs=HBM_SPEC,
        scratch_shapes=[pltpu.SemaphoreType.DMA, pltpu.SemaphoreType.DMA], name=name)(g)


def add_half(g, a, c_idx, name):
    n, rows, width = g.shape
    half = rows // 2
    tile = half // 2
    nt = half // tile

    def body(c_ref, g_ref, a_ref, o_ref, ob_ref):
        v = g_ref[...] + a_ref[...]
        o_ref[...] = v
        ob_ref[...] = v.astype(ob_ref.dtype)

    out_spec = pl.BlockSpec((None, tile, width), lambda s, i, c_ref: (s, i, 0))
    gs = pltpu.PrefetchScalarGridSpec(
        num_scalar_prefetch=1, grid=(n, nt),
        in_specs=[pl.BlockSpec((None, tile, width), lambda s, i, c_ref: (s, c_ref[0] * nt + i, 0)), out_spec],
        out_specs=[out_spec, out_spec])
    return pl.pallas_call(body, grid_spec=gs, out_shape=[SDS((n, half, width), F32), SDS((n, half, width), BF16)],
                          compiler_params=_cparams(("parallel", "parallel")), name=name)(c_idx, g, a)


def exchange_small(small):
    def body(sm_ref, all_ref, send_sems, recv_sems, local_sem):
        x, y, c = _me()
        mine = pltpu.make_async_copy(sm_ref, all_ref.at[0], local_sem)
        mine.start()
        copies = []
        for m in range(1, N_DEV):
            peer = (x ^ ((m >> 2) & 1), y ^ ((m >> 1) & 1), c ^ (m & 1))
            copies.append(pltpu.make_async_remote_copy(
                src_ref=sm_ref, dst_ref=all_ref.at[m], send_sem=send_sems.at[m - 1], recv_sem=recv_sems.at[m - 1],
                device_id=peer, device_id_type=MESH_ID))
        for cp in copies:
            cp.start()
        for cp in copies:
            cp.wait()
        mine.wait()

    return pl.pallas_call(
        body, out_shape=SDS((N_DEV,) + small.shape, small.dtype), in_specs=[HBM_SPEC], out_specs=HBM_SPEC,
        scratch_shapes=[pltpu.SemaphoreType.DMA((N_DEV - 1,)), pltpu.SemaphoreType.DMA((N_DEV - 1,)), pltpu.SemaphoreType.DMA],
        name="exchange_small")(small)


def add_chips(sh, b, k_idx, name):
    n, hf, width = sh.shape
    tile = hf // 2

    def body(k_ref, s_ref, b0, b1, b2, o_ref):
        o_ref[...] = ((s_ref[...] + b0[...].astype(F32)) + b1[...].astype(F32)) + b2[...].astype(F32)

    def bspec(j):
        return pl.BlockSpec((None, tile, width), lambda i, k_ref, j=j: (j, i, 0))

    gs = pltpu.PrefetchScalarGridSpec(
        num_scalar_prefetch=1, grid=(hf // tile,),
        in_specs=[pl.BlockSpec((None, tile, width), lambda i, k_ref: (k_ref[0], i, 0)), bspec(0), bspec(1), bspec(2)],
        out_specs=pl.BlockSpec((tile, width), lambda i, k_ref: (i, 0)))
    return pl.pallas_call(body, grid_spec=gs, out_shape=SDS((hf, width), sh.dtype),
                          compiler_params=_cparams(("parallel",)), name=name)(k_idx, sh, b, b, b)


def sum_small(allsm, me_idx, name):
    _, rows, width = allsm.shape

    def body(me_ref, a_ref, o_ref):
        me = me_ref[0]
        acc = a_ref[me]
        for dev in range(1, N_DEV):
            acc = acc + a_ref[jnp.bitwise_xor(me, dev)]
        o_ref[...] = acc

    gs = pltpu.PrefetchScalarGridSpec(
        num_scalar_prefetch=1, grid=(1,),
        in_specs=[pl.BlockSpec((N_DEV, rows, width), lambda i, me_ref: (0, 0, 0))],
        out_specs=pl.BlockSpec((rows, width), lambda i, me_ref: (0, 0)))
    return pl.pallas_call(body, grid_spec=gs, out_shape=SDS((rows, width), allsm.dtype),
                          compiler_params=_cparams(("arbitrary",)), name=name)(me_idx, allsm)


def join_halves(t, core_idx, name):
    hf, width = t.shape

    def body(t_ref, o_ref, send_sem, recv_sem):
        x, y, c = _me()
        cp = pltpu.make_async_remote_copy(src_ref=t_ref, dst_ref=o_ref, send_sem=send_sem, recv_sem=recv_sem,
                                          device_id=(x, y, 1 - c), device_id_type=MESH_ID)
        cp.start()
        cp.wait()

    theirs = pl.pallas_call(
        body, out_shape=SDS((hf, width), t.dtype), in_specs=[HBM_SPEC], out_specs=HBM_SPEC,
        scratch_shapes=[pltpu.SemaphoreType.DMA, pltpu.SemaphoreType.DMA], name=name)(t)
    return jnp.where(core_idx[0] == 0, jnp.concatenate([t, theirs], axis=0), jnp.concatenate([theirs, t], axis=0))


ROW_PAD = 64


def _pad_rows(rows):
    return -(-rows // ROW_PAD) * ROW_PAD


def _pack_rows(names, shard_shapes, width):
    return _pad_rows(sum((shard_shapes[n][0] * shard_shapes[n][1]) // width for n in names))


def unpack_local(packed, names, shard_shapes, width):
    out, r0 = {}, 0
    for n in names:
        a, b = shard_shapes[n]
        nr = (a * b) // width
        out[n] = packed[r0:r0 + nr].reshape(a, b)
        r0 += nr
    return out


EXACT_IN_GATHER = ("conv_w",)
EXACT_TERMS = 3


def pack_gather(shards, names, width):
    parts = []
    for n in names:
        if n in EXACT_IN_GATHER:
            rest = shards[n].astype(F32)
            for _ in range(EXACT_TERMS):
                term = rest.astype(BF16)
                parts.append(term.reshape(-1, width))
                rest = rest - term.astype(F32)
        else:
            parts.append(shards[n].reshape(-1, width).astype(BF16))
    used = sum(p.shape[0] for p in parts)
    parts.append(jnp.zeros((_pad_rows(used) - used, width), BF16))
    return jnp.concatenate(parts, axis=0)


def unpack_full(gathered, names, shard_shapes, width):
    out, r0 = {}, 0
    for n in names:
        a, b = shard_shapes[n]
        terms = EXACT_TERMS if n in EXACT_IN_GATHER else 1
        nr = (a * b) // width
        pieces = []
        for j in range(N_CHIPS):
            blk = gathered[j, r0:r0 + nr].reshape(a, b)
            for t in range(1, terms):
                blk = blk.astype(F32) + gathered[j, r0 + t * nr:r0 + (t + 1) * nr].reshape(a, b).astype(F32)
            pieces.append(blk)
        out[n] = jnp.concatenate(pieces, axis=SHARD_AXIS[n])
        r0 += terms * nr
    return out


def pack_full(grads, names, shard_shapes, width, pad=True):
    total_rows = _pack_rows(names, shard_shapes, width) if pad else sum(
        (shard_shapes[n][0] * shard_shapes[n][1]) // width for n in names)
    slabs = []
    for j in range(N_CHIPS):
        parts = []
        for n in names:
            a, b = shard_shapes[n]
            ax = SHARD_AXIS[n]
            sz = (a, b)[ax]
            piece = lax.slice_in_dim(grads[n], j * sz, (j + 1) * sz, axis=ax)
            parts.append(piece.reshape(-1, width))
        used = sum(p.shape[0] for p in parts)
        parts.append(jnp.zeros((total_rows - used, width), F32))
        slabs.append(jnp.concatenate(parts, axis=0))
    return jnp.stack(slabs, axis=0)


def _small_layout(sizes, width):
    lay, r = {}, 0
    for n in SMALL_WEIGHTS:
        nr = -(-sizes[n] // width)
        lay[n] = (r, nr, sizes[n])
        r += nr
    assert r <= SMALL_ROWS
    return lay


def pack_small(vals, lay, width):
    rows = []
    for n in SMALL_WEIGHTS:
        r, nr, sz = lay[n]
        v = vals[n].reshape(-1).astype(F32)
        rows.append(jnp.pad(v, (0, nr * width - sz)).reshape(nr, width))
    used = sum(r.shape[0] for r in rows)
    rows.append(jnp.zeros((SMALL_ROWS - used, width), F32))
    return jnp.concatenate(rows, axis=0)


def unpack_small(packed, lay):
    return {n: packed[r:r + nr].reshape(-1)[:sz].reshape(1, sz) for n, (r, nr, sz) in lay.items()}


PART_B = ("w_in", "w_gate")
PART_A = ("w_ff1", "w_ff2", "w_ssd_branch", "w_sb_branch", "w_out", "w_ple_gate", "w_ple", "conv_w")
DIRECT_A = PART_A[:6]


GATHER_EARLY = ("conv_w", "w_ssd_branch", "w_sb_branch", "w_out")
GATHER_LATE = ("w_ff1", "w_ff2", "w_ple_gate", "w_ple")


class Dist(NamedTuple):
    packed_early: object
    packed_late: object
    shard_shapes: dict
    core_idx: object
    chip_idx: object


def build_w_all(w_in, w_gate, dm):
    c0 = dm.DI + dm.CD
    return jnp.concatenate(
        [w_in[:, :c0], w_in[:, c0 + dm.H:], w_gate, w_in[:, c0:c0 + dm.H],
         jnp.zeros((dm.D, DT_PAD - dm.H), w_in.dtype)], axis=1).astype(BF16)


def pack_b(w_in_shard, w_gate_shard):
    return jnp.concatenate([w_in_shard, w_gate_shard], axis=1)


def w_all_from_slabs(slabs, dm):
    n_in = (dm.NA - DT_PAD - 2 * dm.D + dm.H) // N_CHIPS
    c0 = dm.DI + dm.CD

    def w_in_cols(a, b):
        return [slabs[j, :, max(a, j * n_in) - j * n_in:min(b, (j + 1) * n_in) - j * n_in]
                for j in range(N_CHIPS) if max(a, j * n_in) < min(b, (j + 1) * n_in)]

    parts = w_in_cols(0, c0) + w_in_cols(c0 + dm.H, N_CHIPS * n_in) + [slabs[j, :, n_in:] for j in range(N_CHIPS)]
    parts += w_in_cols(c0, c0 + dm.H) + [jnp.zeros((dm.D, DT_PAD - dm.H), slabs.dtype)]
    return jnp.concatenate(parts, axis=1).astype(BF16)


def _w_in_columns(lo, hi, dm):
    c0 = dm.DI + dm.CD
    segments = [(0, c0, 0), (c0, c0 + dm.H, dm.off["dt"]), (c0 + dm.H, c0 + dm.H + 3 * dm.SBW, c0)]
    return [(w0 + max(lo, a) - a, w0 + min(hi, b) - a) for a, b, w0 in segments if max(lo, a) < min(hi, b)]


def slabs_b(dw_all, dm):
    n_in = (dm.NA - DT_PAD - 2 * dm.D + dm.H) // N_CHIPS
    n_gate = 2 * dm.D // N_CHIPS
    slabs = []
    for j in range(N_CHIPS):
        cols = _w_in_columns(j * n_in, (j + 1) * n_in, dm) + [(dm.off["gate"] + j * n_gate, dm.off["gate"] + (j + 1) * n_gate)]
        slabs.append(jnp.concatenate([dw_all[:, a:b] for a, b in cols], axis=1))
    return jnp.stack(slabs, axis=0)


def local_step(x, p, tgt, wf, sm, dm, dist=None):
    s, d = dm.S, dm.D
    off = dm.off
    c0 = dm.DI + dm.CD
    w_all = wf["w_all"] if "w_all" in wf else build_w_all(wf["w_in"], wf["w_gate"], dm)
    g = dm.G
    per_group = lambda v: v.reshape(g, 1, GROUP_HEADS)
    alog, dsk = per_group(sm["a_log"]), per_group(sm["d_skip"])
    b_gate = sm["b_gate"]
    b_ssd, b_sb = b_gate[:, :d], b_gate[:, d:]
    gcol = off["gate"] // d

    (n1,) = row_fwd("norm1", f_norm1, [x], [sm["norm_mix_pre"]], [(d, BF16)])
    proj = matmul(n1, w_all, name="in_proj")
    if dist is None:
        y_sb = attn_fwd(proj, dm, "attn_fwd")
    else:
        y_sb, gathered = attn_fwd(proj, dm, "attn_fwd", rider=gather_rider(dist.packed_early))
        gathered = own_slab(gathered, dist.packed_early, dist.chip_idx)
        wf = {**wf, **unpack_full(gathered, GATHER_EARLY, dist.shard_shapes, d)}
    xbc = conv_fwd(proj, off["xbc"], dm.CD, wf["conv_w"].astype(F32), sm["conv_b"], "conv_fwd")
    dt_raw = proj[:, off["dt"]:off["dt"] + dm.H]
    (dt,) = row_fwd("dt", f_dt, [dt_raw], [sm["dt_bias"]], [(dm.H, F32)])
    dtc = dt.reshape(s, g, GROUP_HEADS).transpose(1, 0, 2)
    dtw = dt.reshape(s, g, GROUP_HEADS).transpose(1, 2, 0)
    if dist is None:
        y_ssd, states = ssd_fwd(proj, xbc, dtc, dtw, alog, dsk, sm["ssd_norm"], dm, "ssd_fwd")
    else:
        y_ssd, states, gathered = ssd_fwd(proj, xbc, dtc, dtw, alog, dsk, sm["ssd_norm"], dm, "ssd_fwd",
                                          rider=gather_rider(dist.packed_late))
        gathered = own_slab(gathered, dist.packed_late, dist.chip_idx)
        wf = {**wf, **unpack_full(gathered, GATHER_LATE, dist.shard_shapes, d)}
    yb_ssd = matmul(y_ssd, wf["w_ssd_branch"], name="ssd_branch")
    yb_sb = matmul(y_sb, wf["w_sb_branch"], name="sb_branch")
    merge_rows = [(proj, d, gcol), (proj, d, gcol + 1), yb_ssd, yb_sb]
    (merged,) = row_fwd("merge", f_merge, merge_rows, [b_ssd, b_sb], [(d, BF16)])
    mo = matmul(merged, wf["w_out"], name="w_out")
    h1, n2 = row_fwd("mix_out", f_mix_out, [x, mo], [sm["norm_mix_post"], sm["norm_ffn_pre"]], [(d, F32), (d, BF16)])
    a1, act = matmul(n2, wf["w_ff1"], name="ff1", finish=lambda v: (v,) + f_relu2(v), out_dtype=(F32, BF16))
    ff = matmul(act, wf["w_ff2"], name="ff2")
    (h2,) = row_fwd("ffn_out", f_ffn_out, [h1, ff], [sm["norm_ffn_post"]], [(d, F32)])
    pg = matmul(h2, wf["w_ple_gate"], name="ple_gate")
    pe = matmul(p, wf["w_ple"], name="ple_emb")

    def reduce_start(gbig, tag):
        from_sibling = swap_halves(gbig, "swap_halves_" + tag)
        pair_sum, pair_sum_bf16 = add_half(gbig, from_sibling, dist.core_idx, "add_half_" + tag)
        return pair_sum, exchange_rider(pair_sum_bf16)

    def reduce_finish(pair_sum, from_chips, tag):
        my_half = add_chips(pair_sum, from_chips, dist.chip_idx, "add_chips_" + tag)
        return join_halves(my_half, dist.core_idx, "join_halves_" + tag)

    gr, reduced = {}, None
    packed_grads = [None if dist is None else jnp.zeros((N_CHIPS, _pack_rows(PART_A, dist.shard_shapes, d), d), F32)]

    def weight_grad(wname, a, dy):
        if dist is None:
            gr[wname] = matmul(a, dy, ta=True, name="d_" + wname)
            return
        row0 = 0
        for n in PART_A[:PART_A.index(wname)]:
            row0 += (dist.shard_shapes[n][0] * dist.shard_shapes[n][1]) // d
        packed_grads[0] = matmul(a, dy, ta=True, name="d_" + wname,
                                 slab=SlabOut(packed_grads[0], row0, dist.shard_shapes[wname][0], SHARD_AXIS[wname]))

    (dh2_a, dpg, dpe), (gr["norm_ple_post"], loss_cols) = row_bwd(
        "ple_loss", f_ple_loss, [h2, pg, pe, tgt], [sm["norm_ple_post"]], [None], [F32, BF16, BF16, None], primal_sum=True)
    loss = jnp.sum(loss_cols)
    gr["w_ple"] = matmul(p, dpe, ta=True, name="d_w_ple")
    weight_grad("w_ple_gate", h2, dpg)
    dh2_b = matmul(dpg, wf["w_ple_gate"], tb=True, name="d_h2")
    (dh1_a, dff), (gr["norm_ffn_post"],) = row_bwd(
        "ffn_out_bwd", f_ffn_out, [h1, ff], [sm["norm_ffn_post"]], [[dh2_a, dh2_b]], [F32, BF16])
    weight_grad("w_ff2", act, dff)
    (da1,) = matmul(dff, wf["w_ff2"], tb=True, name="d_act", sides=(a1,),
                    finish=lambda dact, a1_tile: (dact * (2.0 * jnp.maximum(a1_tile, 0.0)),), out_dtype=(BF16,))
    weight_grad("w_ff1", n2, da1)
    dn2 = matmul(da1, wf["w_ff1"], tb=True, name="d_n2")
    (dx_a, dmo), (gr["norm_mix_post"], gr["norm_ffn_pre"]) = row_bwd(
        "mix_out_bwd", f_mix_out, [x, mo], [sm["norm_mix_post"], sm["norm_ffn_pre"]], [[dh1_a], [dn2]], [F32, BF16])
    weight_grad("w_out", merged, dmo)
    dmerged = matmul(dmo, wf["w_out"], tb=True, name="d_merged")
    (dgp_ssd, dgp_sb, dyb_ssd, dyb_sb), (db_ssd, db_sb) = row_bwd(
        "merge_bwd", f_merge, merge_rows, [b_ssd, b_sb], [[dmerged]], [BF16, BF16, BF16, BF16])
    gr["b_gate"] = jnp.concatenate([db_ssd, db_sb], axis=1)
    weight_grad("w_ssd_branch", y_ssd, dyb_ssd)
    weight_grad("w_sb_branch", y_sb, dyb_sb)
    dy_ssd = matmul(dyb_ssd, wf["w_ssd_branch"], tb=True, name="d_y_ssd")
    dy_sb = matmul(dyb_sb, wf["w_sb_branch"], tb=True, name="d_y_sb")
    dproj, dxs, dbm, dcm, ddtc, ddtw, dalog, ddsk, gr["ssd_norm"] = ssd_bwd(
        proj, xbc, dtc, dtw, alog, dsk, sm["ssd_norm"], states, dy_ssd, dm, "ssd_bwd")
    gr["a_log"], gr["d_skip"] = (v.reshape(1, dm.H) for v in (dalog, ddsk))
    ddt_post = (ddtc.transpose(1, 0, 2) + ddtw.transpose(2, 0, 1)).reshape(s, dm.H)
    (ddt,), (gr["dt_bias"],) = row_bwd("dt_bwd", f_dt, [dt_raw], [sm["dt_bias"]], [[ddt_post]], [BF16])
    conv_w32 = wf["conv_w"].astype(F32)
    dproj, dw_x, dcb_x = conv_bwd(proj, off["xbc"], dxs, 0, conv_w32, sm["conv_b"], dproj, "conv_bwd_x")
    dproj, dw_b, dcb_b = conv_bwd(proj, off["xbc"], dbm, dm.DI, conv_w32, sm["conv_b"], dproj, "conv_bwd_b")
    dproj, dw_c, dcb_c = conv_bwd(proj, off["xbc"], dcm, dm.DI + g * D_STATE, conv_w32, sm["conv_b"], dproj, "conv_bwd_c")
    gr["conv_w"] = jnp.concatenate([dw_x, dw_b, dw_c], axis=1)
    gr["conv_b"] = jnp.concatenate([dcb_x, dcb_b, dcb_c], axis=1)
    if dist is None:
        dq, dk, dv = attn_bwd(proj, dy_sb, dm, "attn_bwd")
    else:
        rest = tuple(n for n in PART_A if n not in DIRECT_A)
        row0 = sum((dist.shard_shapes[n][0] * dist.shard_shapes[n][1]) // d for n in DIRECT_A)
        small_rows = pack_full(gr, rest, dist.shard_shapes, d, pad=False)
        gbig_a = lax.dynamic_update_slice(packed_grads[0], small_rows, (0, row0, 0))
        pair_sum_a, rider_a = reduce_start(gbig_a, "a")
        dq, dk, dv, from_chips_a = attn_bwd(proj, dy_sb, dm, "attn_bwd", rider=rider_a)
        reduced = unpack_local(reduce_finish(pair_sum_a, from_chips_a, "a"), PART_A, dist.shard_shapes, d)
    col = off["q"]
    for piece in (dq, dk, dv, dgp_ssd, dgp_sb, jnp.concatenate([ddt, jnp.zeros((s, DT_PAD - dm.H), BF16)], axis=1)):
        dproj = lax.dynamic_update_slice(dproj, piece, (0, col))
        col += piece.shape[1]
    dw_all = matmul(n1, dproj, ta=True, name="d_w_all")
    if dist is None:
        gr["w_in"] = jnp.concatenate(
            [dw_all[:, :c0], dw_all[:, off["dt"]:off["dt"] + dm.H], dw_all[:, c0:c0 + 3 * dm.SBW]], axis=1)
        gr["w_gate"] = dw_all[:, off["gate"]:off["gate"] + 2 * d]
        dn1 = matmul(dproj, w_all, tb=True, name="d_n1")
    else:
        pair_sum_b, rider_b = reduce_start(slabs_b(dw_all, dm), "b")
        dn1, from_chips_b = matmul(dproj, w_all, tb=True, name="d_n1", rider=rider_b)
        reduced_b = reduce_finish(pair_sum_b, from_chips_b, "b")
        n_in = dist.shard_shapes["w_in"][1]
        reduced.update(w_in=reduced_b[:, :n_in], w_gate=reduced_b[:, n_in:])
    (grad_x,), (gr["norm_mix_pre"],) = row_bwd("norm1_bwd", lambda u, w: (_rms(u, w), u), [x], [sm["norm_mix_pre"]],
                                               [[dn1], [dx_a]], [F32])
    return loss, grad_x, gr, reduced


def kernel(x, p, norm_mix_pre, w_in, conv_w, conv_b, dt_bias, a_log, d_skip, ssd_norm, w_ssd_branch, w_sb_branch, w_gate, b_gate, w_out, norm_mix_post, norm_ffn_pre, w_ff1, w_ff2, norm_ffn_post, w_ple, w_ple_gate, norm_ple_post, loss_target, m_norm_mix_pre, m_w_in, m_conv_w, m_conv_b, m_dt_bias, m_a_log, m_d_skip, m_ssd_norm, m_w_ssd_branch, m_w_sb_branch, m_w_gate, m_b_gate, m_w_out, m_norm_mix_post, m_norm_ffn_pre, m_w_ff1, m_w_ff2, m_norm_ffn_post, m_w_ple, m_w_ple_gate, m_norm_ple_post, v_norm_mix_pre, v_w_in, v_conv_w, v_conv_b, v_dt_bias, v_a_log, v_d_skip, v_ssd_norm, v_w_ssd_branch, v_w_sb_branch, v_w_gate, v_b_gate, v_w_out, v_norm_mix_post, v_norm_ffn_pre, v_w_ff1, v_w_ff2, v_norm_ffn_post, v_w_ple, v_w_ple_gate, v_norm_ple_post):
    loc = dict(locals())
    unbatch = lambda a: a[0] if a.ndim == 3 else a
    w = {n: unbatch(loc[n]) for n in ALL_WEIGHTS}
    m = {n: unbatch(loc["m_" + n]) for n in ALL_WEIGHTS}
    v = {n: unbatch(loc["v_" + n]) for n in ALL_WEIGHTS}
    xs, ps, tgt = x[0], p[0, 0], loss_target[0]
    s, d = xs.shape
    di = w["w_ssd_branch"].shape[0] * N_CHIPS
    cd = w["conv_b"].shape[1]
    dm = Dims(S=s, D=d, DI=di, H=w["dt_bias"].shape[1], G=(cd - di) // (2 * D_STATE), CD=cd,
              SBW=w["w_sb_branch"].shape[0] * N_CHIPS, DFF=w["w_ff2"].shape[0] * N_CHIPS, PLE=ps.shape[1])
    ix, iy, ic = lax.axis_index("x"), lax.axis_index("y"), lax.axis_index("c")
    chip_idx = jnp.reshape(2 * ix + iy, (1,)).astype(jnp.int32)
    core_idx = jnp.reshape(ic, (1,)).astype(jnp.int32)
    dev_idx = jnp.reshape(4 * ix + 2 * iy + ic, (1,)).astype(jnp.int32)

    shard_shapes = {n: w[n].shape for n in BIG_WEIGHTS}
    packed_b = pack_b(w["w_in"], w["w_gate"]).astype(BF16)
    wf = {"w_all": w_all_from_slabs(own_slab(gather_weights(packed_b), packed_b, chip_idx), dm)}
    sm = {n: w[n] for n in SMALL_WEIGHTS}
    dist = Dist(packed_early=pack_gather(w, GATHER_EARLY, d), packed_late=pack_gather(w, GATHER_LATE, d),
                shard_shapes=shard_shapes, core_idx=core_idx, chip_idx=chip_idx)

    loss_part, grad_x, gr, grads = local_step(xs, ps, tgt, wf, sm, dm, dist)
    loss = lax.psum(loss_part, ("x", "y", "c"))

    lay = _small_layout({n: w[n].shape[1] for n in SMALL_WEIGHTS}, d)
    gs_red = sum_small(exchange_small(pack_small(gr, lay, d)), dev_idx, "sum_small")
    grads.update(unpack_small(gs_red, lay))
    delta, new_m, new_v = {}, {}, {}
    for n in BIG_WEIGHTS:
        delta[n], new_m[n], new_v[n] = adamw(w[n], grads[n], m[n], v[n], "adamw_" + n)
    d_sm, nm_sm, nv_sm = adamw(pack_small(w, lay, d), gs_red, pack_small(m, lay, d), pack_small(v, lay, d), "adamw_small")
    for out, packed in ((delta, d_sm), (new_m, nm_sm), (new_v, nv_sm)):
        out.update(unpack_small(packed, lay))

    def leaves(vals):
        return [vals[n][None] if n in BIG_WEIGHTS else vals[n] for n in ALL_WEIGHTS]

    return (loss, grad_x[None], *leaves(grads), *leaves(delta), *leaves(new_m), *leaves(new_v))
```

```python
import functools
import math
from typing import NamedTuple

import jax
import jax.numpy as jnp
from jax import lax
from jax.experimental import pallas as pl
from jax.experimental.pallas import tpu as pltpu

F32 = jnp.float32
BF16 = jnp.bfloat16
SDS = jax.ShapeDtypeStruct

HEAD_DIM = 64
GROUP_HEADS = 4
D_STATE = 128
CHUNK = 128
ATT_TILE = 128
ATT_GROUP = 3
ATT_DEAD = -160.0
LOG2E = 1.4426950408889634
CONV_K = 4
CONV_COLS = 128
RMS_EPS = 1e-6
LANES = 128
DT_PAD = 512
N_CHIPS = 4
N_DEV = 8
SMALL_ROWS = 16
VMEM_LIMIT = 48 * 1024 * 1024
MAX_TK = 3072
TILE_BUDGET = 40 * 1024 * 1024

ADAM_LR = 0.001
ADAM_B1 = 0.9
ADAM_B2 = 0.999
ADAM_EPS = 1e-08
ADAM_WD = 0.01
ADAM_STEP = 10

MESH_ID = pl.DeviceIdType.MESH
HBM_SPEC = pl.BlockSpec(memory_space=pltpu.HBM)

BIG_WEIGHTS = ("w_in", "conv_w", "w_ssd_branch", "w_sb_branch", "w_gate", "w_out", "w_ff1", "w_ff2", "w_ple", "w_ple_gate")
SHARD_AXIS = {"w_in": 1, "conv_w": 1, "w_ssd_branch": 0, "w_sb_branch": 0, "w_gate": 1, "w_out": 0, "w_ff1": 1,
              "w_ff2": 0, "w_ple": 1, "w_ple_gate": 0}
SMALL_WEIGHTS = ("norm_mix_pre", "conv_b", "dt_bias", "a_log", "d_skip", "ssd_norm", "b_gate", "norm_mix_post",
                 "norm_ffn_pre", "norm_ffn_post", "norm_ple_post")
ALL_WEIGHTS = ("norm_mix_pre", "w_in", "conv_w", "conv_b", "dt_bias", "a_log", "d_skip", "ssd_norm", "w_ssd_branch",
               "w_sb_branch", "w_gate", "b_gate", "w_out", "norm_mix_post", "norm_ffn_pre", "w_ff1", "w_ff2",
               "norm_ffn_post", "w_ple", "w_ple_gate", "norm_ple_post")


class Dims(NamedTuple):
    S: int
    D: int
    DI: int
    H: int
    G: int
    CD: int
    SBW: int
    DFF: int
    PLE: int

    @property
    def NA(self):
        return self.DI + self.CD + 3 * self.SBW + 2 * self.D + DT_PAD

    @property
    def off(self):
        o = {}
        o["z"] = 0
        o["xbc"] = self.DI
        o["q"] = self.DI + self.CD
        o["k"] = o["q"] + self.SBW
        o["v"] = o["k"] + self.SBW
        o["gate"] = o["v"] + self.SBW
        o["dt"] = o["gate"] + 2 * self.D
        return o


def _cparams(sem):
    return pltpu.CompilerParams(dimension_semantics=sem, vmem_limit_bytes=VMEM_LIMIT)


def _pick(n, cands):
    for c in cands:
        if n % c == 0:
            return c
    raise ValueError(f"no tile for {n}")


def _grid_call(body, *, grid, in_specs, out_specs, out_shape, scratch, operands, name, rider=None):
    if rider is None:
        sem = ("parallel",) + ("arbitrary",) * (len(grid) - 1)
        return pl.pallas_call(body, grid=grid, in_specs=in_specs, out_specs=out_specs, out_shape=out_shape,
                              scratch_shapes=scratch, compiler_params=_cparams(sem), name=name)(*operands)
    return pl.pallas_call(
        _with_rider(body, rider, grid, len(in_specs), len(out_specs)), grid=grid,
        in_specs=list(in_specs) + [HBM_SPEC] * len(rider.operands),
        out_specs=list(out_specs) + [HBM_SPEC] * len(rider.out_shape),
        out_shape=list(out_shape) + list(rider.out_shape), scratch_shapes=list(scratch) + list(rider.scratch),
        compiler_params=_cparams(("arbitrary",) * len(grid)), name=name)(*operands, *rider.operands)


class SlabOut(NamedTuple):
    buf: object
    row0: int
    rows: int
    axis: int


def matmul(a, b, *, ta=False, tb=False, out_dtype=F32, name, rider=None, slab=None, finish=None, sides=()):
    m, k = (a.shape[1], a.shape[0]) if ta else a.shape
    n, kb = b.shape if tb else (b.shape[1], b.shape[0])
    assert k == kb, (a.shape, b.shape, ta, tb)
    tm = _pick(m, (1024, 512, 256, 128))
    tn = _pick(n, (512, 256, 128))
    if slab is not None:
        tm = _pick(math.gcd(slab.rows, slab.row0), (1024, 512, 256, 128, 64, 32, 16, 8))
        tn = _pick(slab.buf.shape[2], (512, 256, 128))
    tk = max(t for t in range(LANES, min(k, MAX_TK) + 1, LANES) if k % t == 0)
    nk = k // tk
    if slab is None and m % (2 * tm) == 0:
        tall = 2 * tm
        out_bytes = sum(jnp.dtype(dt).itemsize for dt in (out_dtype if finish is not None else (out_dtype,)))
        side_bytes = sum(sd.dtype.itemsize for sd in sides)
        need = (2 * (tall * tk * a.dtype.itemsize + tk * tn * b.dtype.itemsize)
                + tall * tn * (4 + 2 * out_bytes + 2 * side_bytes))
        if need <= TILE_BUDGET:
            tm = tall
    dims = (((0 if ta else 1,), (1 if tb else 0,)), ((), ()))

    def body(a_ref, b_ref, o_ref, acc_ref):
        part = lax.dot_general(a_ref[...].astype(BF16), b_ref[...].astype(BF16), dims, preferred_element_type=F32)
        if nk == 1:
            o_ref[...] = part.astype(o_ref.dtype)
        else:
            kk = pl.program_id(2)

            @pl.when(kk == 0)
            def _():
                acc_ref[...] = part

            @pl.when(kk > 0)
            def _():
                acc_ref[...] += part

            @pl.when(kk == nk - 1)
            def _():
                o_ref[...] = acc_ref[...].astype(o_ref.dtype)

    a_spec = pl.BlockSpec((tk, tm), lambda i, j, kk: (kk, i)) if ta else pl.BlockSpec((tm, tk), lambda i, j, kk: (i, kk))
    b_spec = pl.BlockSpec((tn, tk), lambda i, j, kk: (j, kk)) if tb else pl.BlockSpec((tk, tn), lambda i, j, kk: (kk, j))
    if slab is not None:
        width = slab.buf.shape[2]
        rb0, per_shard = slab.row0 // tm, slab.rows // tm
        if slab.axis == 0:
            where = lambda i, j, kk: (i // per_shard, rb0 + i % per_shard, j)
        else:
            where = lambda i, j, kk: (j // (width // tn), rb0 + i, j % (width // tn))
        return pl.pallas_call(
            lambda a_ref, b_ref, buf_ref, o_ref, acc_ref: body(a_ref, b_ref, o_ref, acc_ref),
            grid=(m // tm, n // tn, nk), in_specs=[a_spec, b_spec, pl.BlockSpec(memory_space=pl.ANY)],
            out_specs=pl.BlockSpec((None, tm, tn), where), out_shape=SDS(slab.buf.shape, slab.buf.dtype),
            scratch_shapes=[pltpu.VMEM((tm, tn), F32)], input_output_aliases={2: 0},
            compiler_params=_cparams(("parallel", "parallel", "arbitrary")), name=name)(a, b, slab.buf)
    tile_spec = pl.BlockSpec((tm, tn), lambda i, j, kk: (i, j))
    if finish is not None:
        def fused(*refs):
            a_ref, b_ref, side_refs = refs[0], refs[1], refs[2:2 + len(sides)]
            o_refs, acc_ref = refs[2 + len(sides):-1], refs[-1]

            def emit(acc):
                for o_ref, val in zip(o_refs, finish(acc, *[r[...] for r in side_refs])):
                    o_ref[...] = val.astype(o_ref.dtype)

            part = lax.dot_general(a_ref[...].astype(BF16), b_ref[...].astype(BF16), dims, preferred_element_type=F32)
            if nk == 1:
                emit(part)
            else:
                kk = pl.program_id(2)

                @pl.when(kk == 0)
                def _():
                    acc_ref[...] = part

                @pl.when(kk > 0)
                def _():
                    acc_ref[...] += part

                @pl.when(kk == nk - 1)
                def _():
                    emit(acc_ref[...])

        return pl.pallas_call(
            fused, grid=(m // tm, n // tn, nk), in_specs=[a_spec, b_spec] + [tile_spec] * len(sides),
            out_specs=[tile_spec] * len(out_dtype), out_shape=[SDS((m, n), dt) for dt in out_dtype],
            scratch_shapes=[pltpu.VMEM((tm, tn), F32)],
            compiler_params=_cparams(("parallel", "parallel", "arbitrary")), name=name)(a, b, *sides)
    res = _grid_call(body, grid=(m // tm, n // tn, nk), in_specs=[a_spec, b_spec],
                     out_specs=[tile_spec], out_shape=[SDS((m, n), out_dtype)],
                     scratch=[pltpu.VMEM((tm, tn), F32)], operands=(a, b), name=name, rider=rider)
    return res[0] if rider is None else res


def _row_spec(entry, tile):
    arr, width, cb = entry if isinstance(entry, tuple) else (entry, entry.shape[1], 0)
    return arr, pl.BlockSpec((tile, width), lambda i, cb=cb: (i, cb))


def _par_spec(p):
    return pl.BlockSpec(p.shape, lambda i: (0, 0))


def row_fwd(name, fn, rows, params, outs, tile=256):
    arrs, specs = zip(*[_row_spec(e, tile) for e in rows])
    s = arrs[0].shape[0]
    nr, npar = len(rows), len(params)

    def body(*refs):
        r = [x[...].astype(F32) for x in refs[:nr]]
        p = [x[...] for x in refs[nr:nr + npar]]
        res = fn(*r, *p)
        for o_ref, val in zip(refs[nr + npar:], res):
            o_ref[...] = val.astype(o_ref.dtype)

    return pl.pallas_call(
        body, grid=(s // tile,), in_specs=list(specs) + [_par_spec(p) for p in params],
        out_specs=[pl.BlockSpec((tile, w), lambda i: (i, 0)) for w, _ in outs],
        out_shape=[SDS((s, w), dt) for w, dt in outs],
        compiler_params=_cparams(("parallel",)), name=name)(*arrs, *params)


def row_bwd(name, fn, rows, params, cots, row_grads, tile=256, primal_sum=False):
    arrs, specs = zip(*[_row_spec(e, tile) for e in rows])
    s = arrs[0].shape[0]
    nr, npar = len(rows), len(params)
    cot_entries = [e for c in cots if c is not None for e in c]
    carrs, cspecs = zip(*[_row_spec(e, tile) for e in cot_entries]) if cot_entries else ((), ())
    nc = len(cot_entries)
    want = [i for i, d in enumerate(row_grads) if d is not None]

    def body(*refs):
        r = [x[...].astype(F32) for x in refs[:nr]]
        p = [x[...] for x in refs[nr:nr + npar]]
        cvals = [x[...].astype(F32) for x in refs[nr + npar:nr + npar + nc]]
        outs = refs[nr + npar + nc:]
        prim, vjp = jax.vjp(fn, *r, *p)
        ct, pos = [], 0
        for c, pr in zip(cots, prim):
            if c is None:
                ct.append(jnp.ones_like(pr))
            else:
                acc = cvals[pos]
                for extra in cvals[pos + 1:pos + len(c)]:
                    acc = acc + extra
                pos += len(c)
                ct.append(acc)
        grads = vjp(tuple(ct))
        for o_ref, i in zip(outs[:len(want)], want):
            o_ref[...] = grads[i].astype(o_ref.dtype)
        acc_refs = outs[len(want):]
        vals = [grads[nr + j] for j in range(npar)]
        if primal_sum:
            vals.append(jnp.sum(prim[0], axis=0, keepdims=True))
        first = pl.program_id(0) == 0

        @pl.when(first)
        def _():
            for a_ref, v in zip(acc_refs, vals):
                a_ref[...] = v

        @pl.when(jnp.logical_not(first))
        def _():
            for a_ref, v in zip(acc_refs, vals):
                a_ref[...] += v

    widths = [(e[1] if isinstance(e, tuple) else e.shape[1]) for e in rows]
    out_specs = [pl.BlockSpec((tile, widths[i]), lambda i_: (i_, 0)) for i in want]
    out_shape = [SDS((s, widths[i]), row_grads[i]) for i in want]
    pshapes = [p.shape for p in params]
    if primal_sum:
        pshapes.append((1, widths[0]))
    out_specs += [pl.BlockSpec(sh, lambda i_: (0, 0)) for sh in pshapes]
    out_shape += [SDS(sh, F32) for sh in pshapes]
    res = pl.pallas_call(
        body, grid=(s // tile,), in_specs=list(specs) + [_par_spec(p) for p in params] + list(cspecs),
        out_specs=out_specs, out_shape=out_shape,
        compiler_params=_cparams(("arbitrary",)), name=name)(*arrs, *params, *carrs)
    return res[:len(want)], res[len(want):]


def _rms(x, w):
    return x * lax.rsqrt(jnp.mean(x * x, axis=-1, keepdims=True) + RMS_EPS) * w


def _sigmoid(x):
    return jax.nn.sigmoid(x)


def _softplus(x):
    return jnp.maximum(x, 0.0) + jnp.log1p(jnp.exp(-jnp.abs(x)))


def f_norm1(x, w):
    return (_rms(x, w),)


def f_dt(raw, bias):
    return (_softplus(raw + bias),)


def f_merge(gp_ssd, gp_sb, yb_ssd, yb_sb, b_ssd, b_sb):
    return (_sigmoid(gp_ssd + b_ssd) * yb_ssd + _sigmoid(gp_sb + b_sb) * yb_sb,)


def f_mix_out(x, mo, w_post, w_pre):
    h1 = x + _rms(mo, w_post)
    return h1, _rms(h1, w_pre)


def f_relu2(a1):
    return (jnp.square(jnp.maximum(a1, 0.0)),)


def f_ffn_out(h1, ff, w):
    return (h1 + _rms(ff, w),)


def f_ple_loss(h2, pg, pe, tgt, w):
    h3 = h2 + _rms(_sigmoid(pg) * pe, w)
    return (0.5 * jnp.square(h3 - tgt) * (1.0 / h2.shape[-1]),)


def _shift_down(u, d, rows):
    return u if d == 0 else jnp.where(rows >= d, pltpu.roll(u, d, 0), 0.0)


def _shift_up(u, d, rows):
    s = u.shape[0]
    return u if d == 0 else jnp.where(rows < s - d, pltpu.roll(u, s - d, 0), 0.0)


def conv_fwd(proj, col0, cd, conv_w, conv_b, name):
    s = proj.shape[0]
    cb0 = col0 // CONV_COLS

    def body(u_ref, w_ref, b_ref, o_ref):
        u = u_ref[...]
        rows = lax.broadcasted_iota(jnp.int32, u.shape, 0)
        y = jnp.broadcast_to(b_ref[...], u.shape)
        for k in range(CONV_K):
            y = y + w_ref[k:k + 1, :] * _shift_down(u, CONV_K - 1 - k, rows)
        o_ref[...] = y * _sigmoid(y)

    return pl.pallas_call(
        body, grid=(cd // CONV_COLS,),
        in_specs=[pl.BlockSpec((s, CONV_COLS), lambda i: (0, cb0 + i)),
                  pl.BlockSpec((CONV_K, CONV_COLS), lambda i: (0, i)),
                  pl.BlockSpec((1, CONV_COLS), lambda i: (0, i))],
        out_specs=pl.BlockSpec((s, CONV_COLS), lambda i: (0, i)),
        out_shape=SDS((s, cd), F32), compiler_params=_cparams(("parallel",)), name=name)(proj, conv_w, conv_b)


def conv_bwd(proj, col0, dout, ch0, conv_w, conv_b, dproj, name):
    s = proj.shape[0]
    ncb = dout.shape[1] // CONV_COLS
    cb0 = (col0 + ch0) // CONV_COLS
    wb0 = ch0 // CONV_COLS

    def body(u_ref, g_ref, w_ref, b_ref, _, du_ref, dw_ref, db_ref):
        u = u_ref[...]
        rows = lax.broadcasted_iota(jnp.int32, u.shape, 0)
        y = jnp.broadcast_to(b_ref[...], u.shape)
        for k in range(CONV_K):
            y = y + w_ref[k:k + 1, :] * _shift_down(u, CONV_K - 1 - k, rows)
        sg = _sigmoid(y)
        dy = g_ref[...] * (sg * (1.0 + y * (1.0 - sg)))
        du = jnp.zeros_like(u)
        for k in range(CONV_K):
            d = CONV_K - 1 - k
            du = du + w_ref[k:k + 1, :] * _shift_up(dy, d, rows)
            dw_ref[k:k + 1, :] = jnp.sum(dy * _shift_down(u, d, rows), axis=0, keepdims=True)
        du_ref[...] = du.astype(du_ref.dtype)
        db_ref[...] = jnp.sum(dy, axis=0, keepdims=True)

    return pl.pallas_call(
        body, grid=(ncb,),
        in_specs=[pl.BlockSpec((s, CONV_COLS), lambda i: (0, cb0 + i)),
                  pl.BlockSpec((s, CONV_COLS), lambda i: (0, i)),
                  pl.BlockSpec((CONV_K, CONV_COLS), lambda i: (0, wb0 + i)),
                  pl.BlockSpec((1, CONV_COLS), lambda i: (0, wb0 + i)),
                  pl.BlockSpec(memory_space=pl.ANY)],
        out_specs=[pl.BlockSpec((s, CONV_COLS), lambda i: (0, cb0 + i)),
                   pl.BlockSpec((CONV_K, CONV_COLS), lambda i: (0, i)),
                   pl.BlockSpec((1, CONV_COLS), lambda i: (0, i))],
        out_shape=[SDS(dproj.shape, dproj.dtype), SDS((CONV_K, ncb * CONV_COLS), F32), SDS((1, ncb * CONV_COLS), F32)],
        input_output_aliases={4: 0},
        compiler_params=_cparams(("parallel",)), name=name)(proj, dout, conv_w, conv_b, dproj)


def _dot(a, b, dims):
    return lax.dot_general(a.astype(BF16), b.astype(BF16), (dims, ((), ())), preferred_element_type=F32)


NN = ((1,), (0,))
NT = ((1,), (1,))
TN = ((0,), (0,))


def ssd_chunk(xs, zs, nw, dtc, dtw, alogs, dsks, bm, cm, prev):
    ln = bm.shape[0]
    gw = GROUP_HEADS * HEAD_DIM
    row = lax.broadcasted_iota(jnp.int32, (ln, ln), 0)
    col = lax.broadcasted_iota(jnp.int32, (ln, ln), 1)
    causal = row >= col
    tri = causal.astype(F32)
    tri_t = (row <= col).astype(F32)
    lane_head = lax.broadcasted_iota(jnp.int32, (1, gw), 1) // HEAD_DIM
    sub_head = lax.broadcasted_iota(jnp.int32, (gw, 1), 0) // HEAD_DIM
    on_lanes = [(lane_head == r).astype(F32) for r in range(GROUP_HEADS)]
    on_rows = [(sub_head == r).astype(F32) for r in range(GROUP_HEADS)]
    cb = _dot(cm, bm, NT)
    decays, dt_full, acs_full, end_full, dsk_full, end_rows = [], 0.0, 0.0, 0.0, 0.0, 0.0
    for r in range(GROUP_HEADS):
        a = -jnp.exp(alogs[r])
        da_c = dtc[r] * a
        da_w = dtw[r] * a
        acs_c = jnp.sum(tri * da_w, axis=1, keepdims=True)
        acs_w = jnp.sum(tri_t * da_c, axis=0, keepdims=True)
        alast = jnp.sum(da_w, axis=1, keepdims=True)
        decays.append(jnp.exp(jnp.where(causal, acs_c - acs_w, -jnp.inf)))
        dt_full = dt_full + dtc[r] * on_lanes[r]
        acs_full = acs_full + acs_c * on_lanes[r]
        end_full = end_full + alast * on_lanes[r]
        dsk_full = dsk_full + dsks[r] * on_lanes[r]
        end_rows = end_rows + alast * on_rows[r]
    xd = xs * dt_full
    y = xs * dsk_full + _dot(cm, prev, NT) * jnp.exp(acs_full)
    for r in range(GROUP_HEADS):
        y = y + _dot(cb * decays[r], xd * on_lanes[r], NN)
    new_prev = prev * jnp.exp(end_rows) + _dot(xd * jnp.exp(end_full - acs_full), bm, TN)
    yg = y * (zs * _sigmoid(zs))
    rstd = lax.rsqrt(jnp.mean(yg * yg, axis=-1, keepdims=True) + RMS_EPS)
    return yg * rstd * nw, new_prev


SSD_STEP_GROUPS = 1


def _ssd_specs(dm, cidx):
    u = SSD_STEP_GROUPS
    gw = GROUP_HEADS * HEAD_DIM
    nb0 = dm.DI // D_STATE
    assert nb0 % u == 0 and dm.G % u == 0
    par = pl.BlockSpec((u, 1, GROUP_HEADS), lambda g, c: (g, 0, 0))
    return dict(
        z=pl.BlockSpec((CHUNK, u * gw), lambda g, c: (cidx(c), g)),
        xs=pl.BlockSpec((CHUNK, u * gw), lambda g, c: (cidx(c), g)),
        b=pl.BlockSpec((CHUNK, u * D_STATE), lambda g, c: (cidx(c), nb0 // u + g)),
        c=pl.BlockSpec((CHUNK, u * D_STATE), lambda g, c: (cidx(c), (nb0 + dm.G) // u + g)),
        dtc=pl.BlockSpec((u, CHUNK, GROUP_HEADS), lambda g, c: (g, cidx(c), 0)),
        dtw=pl.BlockSpec((u, GROUP_HEADS, CHUNK), lambda g, c: (g, 0, cidx(c))),
        par=par,
        nw=pl.BlockSpec((1, u * gw), lambda g, c: (0, g)),
        st=pl.BlockSpec((u, None, gw, D_STATE), lambda g, c: (g, cidx(c), 0, 0)),
    )


def _ssd_load(k, z_ref, xs_ref, b_ref, c_ref, dtc_ref, dtw_ref, alog_ref, dsk_ref, nw_ref):
    gw = GROUP_HEADS * HEAD_DIM
    wide, narrow = slice(k * gw, (k + 1) * gw), slice(k * D_STATE, (k + 1) * D_STATE)
    dtc = tuple(dtc_ref[k, :, r:r + 1] for r in range(GROUP_HEADS))
    dtw = tuple(dtw_ref[k, r:r + 1, :] for r in range(GROUP_HEADS))
    alogs = tuple(alog_ref[k, :, r:r + 1] for r in range(GROUP_HEADS))
    dsks = tuple(dsk_ref[k, :, r:r + 1] for r in range(GROUP_HEADS))
    return xs_ref[:, wide], z_ref[:, wide], nw_ref[:, wide], dtc, dtw, alogs, dsks, b_ref[:, narrow], c_ref[:, narrow]


def ssd_fwd(proj, xbc, dtc, dtw, alog, dsk, nw, dm, name, rider=None):
    nc = dm.S // CHUNK
    u = SSD_STEP_GROUPS
    gw = GROUP_HEADS * HEAD_DIM
    sp = _ssd_specs(dm, lambda c: c)

    def body(z_ref, xs_ref, b_ref, c_ref, dtc_ref, dtw_ref, alog_ref, dsk_ref, nw_ref, y_ref, st_ref, prev):
        @pl.when(pl.program_id(1) == 0)
        def _():
            prev[...] = jnp.zeros_like(prev)

        st_ref[...] = prev[...]
        for k in range(u):
            args = _ssd_load(k, z_ref, xs_ref, b_ref, c_ref, dtc_ref, dtw_ref, alog_ref, dsk_ref, nw_ref)
            out, new = ssd_chunk(*args, prev[k])
            y_ref[:, k * gw:(k + 1) * gw] = out.astype(y_ref.dtype)
            prev[k] = new

    return _grid_call(
        body, grid=(dm.G // u, nc),
        in_specs=[sp["z"], sp["xs"], sp["b"], sp["c"], sp["dtc"], sp["dtw"], sp["par"], sp["par"], sp["nw"]],
        out_specs=[sp["xs"], sp["st"]],
        out_shape=[SDS((dm.S, dm.DI), BF16), SDS((dm.G, nc, gw, D_STATE), F32)],
        scratch=[pltpu.VMEM((u, gw, D_STATE), F32)],
        operands=(proj, xbc, xbc, xbc, dtc, dtw, alog, dsk, nw), name=name, rider=rider)


def ssd_bwd(proj, xbc, dtc, dtw, alog, dsk, nw, states, dy, dm, name):
    nc = dm.S // CHUNK
    u = SSD_STEP_GROUPS
    sp = _ssd_specs(dm, lambda c: nc - 1 - c)
    gw = GROUP_HEADS * HEAD_DIM
    bc_spec = pl.BlockSpec((CHUNK, u * D_STATE), lambda g, c: (nc - 1 - c, g))

    def body(z_ref, xs_ref, b_ref, c_ref, dtc_ref, dtw_ref, alog_ref, dsk_ref, nw_ref, st_ref, dy_ref,
             dz_ref, dxs_ref, db_ref, dc_ref, ddtc_ref, ddtw_ref, dalog_ref, ddsk_ref, dnw_ref, dprev):
        first = pl.program_id(1) == 0

        @pl.when(first)
        def _():
            dprev[...] = jnp.zeros_like(dprev)

        param_grads = []
        for k in range(u):
            wide, narrow = slice(k * gw, (k + 1) * gw), slice(k * D_STATE, (k + 1) * D_STATE)
            args = _ssd_load(k, z_ref, xs_ref, b_ref, c_ref, dtc_ref, dtw_ref, alog_ref, dsk_ref, nw_ref)
            _, vjp = jax.vjp(ssd_chunk, *args, st_ref[k])
            gxs, gzs, gnw, gdtc, gdtw, galogs, gdsks, gb, gc, gprev = vjp((dy_ref[:, wide], dprev[k]))
            dxs_ref[:, wide] = gxs
            dz_ref[:, wide] = gzs.astype(dz_ref.dtype)
            db_ref[:, narrow] = gb
            dc_ref[:, narrow] = gc
            dprev[k] = gprev
            for r in range(GROUP_HEADS):
                ddtc_ref[k, :, r:r + 1] = gdtc[r]
                ddtw_ref[k, r:r + 1, :] = gdtw[r]
            param_grads.append((wide, gnw, galogs, gdsks))

        @pl.when(first)
        def _():
            for k, (wide, gnw, galogs, gdsks) in enumerate(param_grads):
                dnw_ref[:, wide] = gnw
                for r in range(GROUP_HEADS):
                    dalog_ref[k, :, r:r + 1] = galogs[r]
                    ddsk_ref[k, :, r:r + 1] = gdsks[r]

        @pl.when(jnp.logical_not(first))
        def _():
            for k, (wide, gnw, galogs, gdsks) in enumerate(param_grads):
                dnw_ref[:, wide] += gnw
                for r in range(GROUP_HEADS):
                    dalog_ref[k, :, r:r + 1] += galogs[r]
                    ddsk_ref[k, :, r:r + 1] += gdsks[r]

    xs_out = pl.BlockSpec((CHUNK, u * gw), lambda g, c: (nc - 1 - c, g))
    return pl.pallas_call(
        body, grid=(dm.G // u, nc),
        in_specs=[sp["z"], sp["xs"], sp["b"], sp["c"], sp["dtc"], sp["dtw"], sp["par"], sp["par"], sp["nw"],
                  sp["st"], xs_out],
        out_specs=[xs_out, xs_out, bc_spec, bc_spec, sp["dtc"], sp["dtw"], sp["par"], sp["par"], sp["nw"]],
        out_shape=[SDS((dm.S, dm.NA), BF16), SDS((dm.S, dm.DI), F32), SDS((dm.S, dm.G * D_STATE), F32),
                   SDS((dm.S, dm.G * D_STATE), F32), SDS((dm.G, dm.S, GROUP_HEADS), F32), SDS((dm.G, GROUP_HEADS, dm.S), F32),
                   SDS((dm.G, 1, GROUP_HEADS), F32), SDS((dm.G, 1, GROUP_HEADS), F32), SDS((1, dm.DI), F32)],
        scratch_shapes=[pltpu.VMEM((u, gw, D_STATE), F32)],
        compiler_params=_cparams(("parallel", "arbitrary")), name=name)(
            proj, xbc, xbc, xbc, dtc, dtw, alog, dsk, nw, states, dy)


def _split_bf16(v):
    hi = v.astype(BF16)
    return hi, (v - hi.astype(F32)).astype(BF16)


def _tri(v, mat):
    hi, lo = _split_bf16(v)
    return jnp.dot(hi, mat, preferred_element_type=F32) + jnp.dot(lo, mat, preferred_element_type=F32)


def _blocks(v):
    return [v[:, b * ATT_TILE:(b + 1) * ATT_TILE] for b in range(v.shape[1] // ATT_TILE)]


def _sb_group(z, mask, run, after_mat):
    sp = jnp.maximum(z, 0.0) + jnp.log2(1.0 + jnp.exp2(-jnp.abs(z)))
    lk = -sp if mask is None else jnp.where(mask, -sp, 0.0)
    cums = [_tri(v, after_mat) for v in _blocks(lk)]
    sums = [jnp.sum(v, axis=1, keepdims=True) for v in _blocks(lk)]
    later = [None] * len(cums)
    for b in reversed(range(len(cums))):
        later[b] = run + cums[b]
        run = run + sums[b]
    ls = z - sp
    w = jnp.exp2(ls + jnp.concatenate(later, axis=1))
    if mask is not None:
        w = jnp.where(mask, w, 0.0)
    return ls, w, run


def _alive(run_a, run_b):
    return (jnp.max(jnp.maximum(run_a, run_b)) > ATT_DEAD).astype(jnp.int32)


def _window(i, jj, t):
    gw = ATT_GROUP * t
    end = (i + 1 - ATT_GROUP * jj) * t
    r0 = pl.multiple_of(jnp.maximum(end - gw, 0), t)
    rows = i * t + lax.broadcasted_iota(jnp.int32, (t, gw), 0)
    cols = r0 + lax.broadcasted_iota(jnp.int32, (t, gw), 1)
    return r0, jnp.logical_and(cols < rows, cols < end)


def _att_specs(dm, s):
    t = ATT_TILE
    qb, kb, vb = dm.off["q"] // LANES, dm.off["k"] // LANES, dm.off["v"] // LANES
    return (pl.BlockSpec((t, LANES), lambda p, i: (i, qb + p)),
            pl.BlockSpec((s, LANES), lambda p, i: (0, kb + p)),
            pl.BlockSpec((s, LANES), lambda p, i: (0, vb + p)))


def attn_fwd(proj, dm, name, rider=None):
    s, t = dm.S, ATT_TILE
    scale = HEAD_DIM ** -0.5
    hsl = [slice(h * HEAD_DIM, (h + 1) * HEAD_DIM) for h in range(2)]

    gw = ATT_GROUP * t

    def body(q_ref, k_ref, v_ref, o_ref):
        i = pl.program_id(1)
        gd = i // ATT_GROUP
        r_io = lax.broadcasted_iota(jnp.int32, (t, t), 0)
        c_io = lax.broadcasted_iota(jnp.int32, (t, t), 1)
        after_mat = (r_io > c_io).astype(BF16)
        qs = [q_ref[:, sl].astype(BF16) for sl in hsl]

        def group(jj, carry):
            r0, mask = _window(i, jj, t)
            zs = [_dot(qs[h], k_ref[pl.ds(r0, gw), hsl[h]], NT) * (scale * LOG2E) for h in range(2)]
            res = [_sb_group(zs[h], mask, carry[h][0], after_mat) for h in range(2)]
            return tuple((res[h][2], carry[h][1] + _dot(res[h][1], v_ref[pl.ds(r0, gw), hsl[h]], NN)) for h in range(2))

        zero = (jnp.zeros((t, 1), F32), jnp.zeros((t, HEAD_DIM), F32))
        carry = group(0, (zero, zero))

        def step(st):
            jj, _, c = st
            c = group(jj, c)
            return jj + 1, _alive(c[0][0], c[1][0]), c

        _, _, carry = lax.while_loop(lambda st: jnp.logical_and(st[0] <= gd, st[1] > 0), step,
                                     (jnp.int32(1), _alive(carry[0][0], carry[1][0]), carry))
        for h in range(2):
            o_ref[:, hsl[h]] = carry[h][1]

    qs_, ks_, vs_ = _att_specs(dm, s)
    res = _grid_call(body, grid=(dm.SBW // LANES, s // t), in_specs=[qs_, ks_, vs_],
                     out_specs=[pl.BlockSpec((t, LANES), lambda p, i: (i, p))], out_shape=[SDS((s, dm.SBW), F32)],
                     scratch=[], operands=(proj, proj, proj), name=name, rider=rider)
    return res[0] if rider is None else res


def attn_bwd(proj, do, dm, name, rider=None):
    s, t = dm.S, ATT_TILE
    nq = s // t
    gw = ATT_GROUP * t
    n_win = (nq - 1) // ATT_GROUP + 1
    scale = HEAD_DIM ** -0.5
    hsl = [slice(h * HEAD_DIM, (h + 1) * HEAD_DIM) for h in range(2)]

    def body(q_ref, k_ref, v_ref, do_ref, dq_ref, dk_ref, dv_ref, dk_acc, dv_acc, g_scr, s_scr):
        i = pl.program_id(1)

        @pl.when(i == 0)
        def _():
            dk_acc[...] = jnp.zeros_like(dk_acc)
            dv_acc[...] = jnp.zeros_like(dv_acc)

        gd = i // ATT_GROUP
        r_io = lax.broadcasted_iota(jnp.int32, (t, t), 0)
        c_io = lax.broadcasted_iota(jnp.int32, (t, t), 1)
        after_mat = (r_io > c_io).astype(BF16)
        before_mat = (r_io < c_io).astype(BF16)
        qs = [q_ref[:, sl].astype(BF16) for sl in hsl]
        dos = [do_ref[:, sl].astype(BF16) for sl in hsl]
        q_t = q_ref[...].T.astype(BF16)
        do_t = do_ref[...].T.astype(BF16)

        def pass1(jj, runs):
            r0, mask = _window(i, jj, t)
            zs = [_dot(qs[h], k_ref[pl.ds(r0, gw), hsl[h]], NT) * (scale * LOG2E) for h in range(2)]
            dws = [_dot(dos[h], v_ref[pl.ds(r0, gw), hsl[h]], NT) for h in range(2)]
            out = []
            for h in range(2):
                ls, w, run = _sb_group(zs[h], mask, runs[h], after_mat)
                g_scr[h, jj] = dws[h] * w
                s_scr[h, jj] = jnp.exp2(ls)
                dv_acc[hsl[h], pl.ds(r0, gw)] += _dot(do_t[hsl[h]], w, NN)
                out.append(run)
            return tuple(out)

        zero_col = jnp.zeros((t, 1), F32)
        runs = pass1(0, (zero_col, zero_col))

        def step1(st):
            jj, _, r = st
            r = pass1(jj, r)
            return jj + 1, _alive(r[0], r[1]), r

        walked, _, _ = lax.while_loop(lambda st: jnp.logical_and(st[0] <= gd, st[1] > 0), step1,
                                      (jnp.int32(1), _alive(runs[0], runs[1]), runs))

        def pass2(jj, carry):
            r0, mask = _window(i, jj, t)
            out = []
            for h in range(2):
                pre, dq = carry[h]
                gg = g_scr[h, jj]
                sig = s_scr[h, jj]
                before = []
                for v in _blocks(gg):
                    before.append(pre + _tri(v, before_mat))
                    pre = pre + jnp.sum(v, axis=1, keepdims=True)
                dz = jnp.where(mask, gg * (1.0 - sig) - jnp.concatenate(before, axis=1) * sig, 0.0)
                dz = (dz * scale).astype(BF16)
                dk_acc[hsl[h], pl.ds(r0, gw)] += _dot(q_t[hsl[h]], dz, NN)
                out.append((pre, dq + _dot(dz, k_ref[pl.ds(r0, gw), hsl[h]], NN)))
            return tuple(out)

        zero = (zero_col, jnp.zeros((t, HEAD_DIM), F32))
        carry = lax.fori_loop(1, walked, lambda n, c: pass2(walked - n, c), (zero, zero))
        carry = pass2(0, carry)
        for h in range(2):
            dq_ref[:, hsl[h]] = carry[h][1].astype(dq_ref.dtype)

        @pl.when(i == nq - 1)
        def _():
            for b in range(nq):
                rows = slice(b * t, (b + 1) * t)
                dk_ref[rows, :] = dk_acc[:, rows].T.astype(dk_ref.dtype)
                dv_ref[rows, :] = dv_acc[:, rows].T.astype(dv_ref.dtype)

    qs_, ks_, vs_ = _att_specs(dm, s)
    tile_spec = pl.BlockSpec((t, LANES), lambda p, i: (i, p))
    full_spec = pl.BlockSpec((s, LANES), lambda p, i: (0, p))
    return _grid_call(
        body, grid=(dm.SBW // LANES, nq), in_specs=[qs_, ks_, vs_, tile_spec],
        out_specs=[tile_spec, full_spec, full_spec], out_shape=[SDS((s, dm.SBW), BF16)] * 3,
        scratch=[pltpu.VMEM((LANES, s), F32), pltpu.VMEM((LANES, s), F32),
                 pltpu.VMEM((2, n_win, t, gw), F32), pltpu.VMEM((2, n_win, t, gw), F32)],
        operands=(proj, proj, proj, do), name=name, rider=rider)


def adamw(w, g, m, v, name):
    rows, width = w.shape
    tile = _pick(rows, (256, 64, 16, 8, 4))
    c1 = 1.0 / (1.0 - ADAM_B1 ** ADAM_STEP)
    c2 = 1.0 / (1.0 - ADAM_B2 ** ADAM_STEP)

    def body(w_ref, g_ref, m_ref, v_ref, d_ref, nm_ref, nv_ref):
        gg = g_ref[...]
        nm = ADAM_B1 * m_ref[...] + (1.0 - ADAM_B1) * gg
        nv = ADAM_B2 * v_ref[...] + (1.0 - ADAM_B2) * (gg * gg)
        d_ref[...] = -ADAM_LR * ((nm * c1) / (jnp.sqrt(nv * c2) + ADAM_EPS) + ADAM_WD * w_ref[...])
        nm_ref[...] = nm
        nv_ref[...] = nv

    spec = pl.BlockSpec((tile, width), lambda i: (i, 0))
    return pl.pallas_call(
        body, grid=(rows // tile,), in_specs=[spec] * 4, out_specs=[spec] * 3,
        out_shape=[SDS((rows, width), F32)] * 3, compiler_params=_cparams(("parallel",)), name=name)(w, g, m, v)


def _me():
    return lax.axis_index("x"), lax.axis_index("y"), lax.axis_index("c")


def _other_chips(x, y):
    return [(1 - x, y), (x, 1 - y), (1 - x, 1 - y)]


def gather_weights(wp):
    rows, width = wp.shape
    half = rows // 2

    def body(w_ref, out_ref, send_sems, recv_sems):
        x, y, c = _me()
        sibling = (x, y, 1 - c)
        chips = _other_chips(x, y)

        def part(cx, cy, hf):
            return out_ref.at[2 * cx + cy, hf]

        def copy(k, src, dst, to):
            return pltpu.make_async_remote_copy(src_ref=src, dst_ref=dst, send_sem=send_sems.at[k], recv_sem=recv_sems.at[k],
                                                device_id=to, device_id_type=MESH_ID)

        first = [copy(j, w_ref.at[c], part(x, y, c), (cx, cy, c)) for j, (cx, cy) in enumerate(chips)]
        for cp in first:
            cp.start()
        passed = [copy(3 + j, part(cx, cy, c), part(cx, cy, c), sibling) for j, (cx, cy) in enumerate(chips)]
        for j, (cx, cy) in enumerate(chips):
            copy(j, part(cx, cy, c), part(cx, cy, c), (x, y, c)).wait_recv()
            passed[j].start()
        for j, (cx, cy) in enumerate(chips):
            copy(3 + j, part(cx, cy, 1 - c), part(cx, cy, 1 - c), (x, y, c)).wait_recv()
        for cp in first + passed:
            cp.wait_send()

    return pl.pallas_call(
        body, out_shape=SDS((N_CHIPS, 2, half, width), wp.dtype), in_specs=[HBM_SPEC], out_specs=HBM_SPEC,
        scratch_shapes=[pltpu.SemaphoreType.DMA((6,)), pltpu.SemaphoreType.DMA((6,))],
        name="gather_weights")(wp.reshape(2, half, width)).reshape(N_CHIPS, rows, width)


class Rider(NamedTuple):
    operands: tuple
    out_shape: tuple
    scratch: tuple
    start: object
    wait: object


def _with_rider(body, rider, grid, n_in, n_out):
    if rider is None:
        return body
    r_in, r_out = len(rider.operands), len(rider.out_shape)

    def full(*refs):
        ins, refs = refs[:n_in], refs[n_in:]
        rins, refs = refs[:r_in], refs[r_in:]
        outs, refs = refs[:n_out], refs[n_out:]
        routs, refs = refs[:r_out], refs[r_out:]
        scr, rscr = refs[:len(refs) - len(rider.scratch)], refs[len(refs) - len(rider.scratch):]
        ids = [pl.program_id(k) for k in range(len(grid))]
        first = functools.reduce(jnp.logical_and, [i == 0 for i in ids])
        last = functools.reduce(jnp.logical_and, [i == g - 1 for i, g in zip(ids, grid)])

        @pl.when(first)
        def _():
            rider.start(rins, routs, rscr)

        body(*ins, *outs, *scr)

        @pl.when(last)
        def _():
            rider.wait(rins, routs, rscr)

    return full


def gather_rider(wp):
    def copies(ins, outs, scr, sending):
        (w_ref,), (o_ref,), (send_sems, recv_sems) = ins, outs, scr
        x, y, c = _me()
        return [pltpu.make_async_remote_copy(src_ref=w_ref, dst_ref=o_ref.at[2 * x + y if sending else 2 * cx + cy],
                                             send_sem=send_sems.at[j], recv_sem=recv_sems.at[j], device_id=(cx, cy, c),
                                             device_id_type=MESH_ID)
                for j, (cx, cy) in enumerate(_other_chips(x, y))]

    def start(ins, outs, scr):
        for cp in copies(ins, outs, scr, True):
            cp.start()

    def wait(ins, outs, scr):
        for cp in copies(ins, outs, scr, False):
            cp.wait()

    return Rider((wp,), (SDS((N_CHIPS,) + wp.shape, wp.dtype),),
                 (pltpu.SemaphoreType.DMA((3,)), pltpu.SemaphoreType.DMA((3,))), start, wait)


def own_slab(gathered, wp, chip_idx):
    return lax.dynamic_update_slice(gathered, wp[None], (chip_idx[0], 0, 0))


def exchange_rider(sh):
    def copies(ins, outs, scr):
        (s_ref,), (b_ref,), (send_sems, recv_sems) = ins, outs, scr
        x, y, c = _me()
        return [pltpu.make_async_remote_copy(src_ref=s_ref.at[2 * cx + cy], dst_ref=b_ref.at[j], send_sem=send_sems.at[j],
                                             recv_sem=recv_sems.at[j], device_id=(cx, cy, c), device_id_type=MESH_ID)
                for j, (cx, cy) in enumerate(_other_chips(x, y))]

    def start(ins, outs, scr):
        for cp in copies(ins, outs, scr):
            cp.start()

    def wait(ins, outs, scr):
        for cp in copies(ins, outs, scr):
            cp.wait()

    return Rider((sh,), (SDS((3,) + sh.shape[1:], sh.dtype),),
                 (pltpu.SemaphoreType.DMA((3,)), pltpu.SemaphoreType.DMA((3,))), start, wait)


def swap_halves(g, name):
    n, rows, width = g.shape
    half = rows // 2

    def body(g_ref, a_ref, send_sem, recv_sem):
        x, y, c = _me()
        cp = pltpu.make_async_remote_copy(src_ref=g_ref.at[:, pl.ds((1 - c) * half, half), :], dst_ref=a_ref,
                                          send_sem=send_sem, recv_sem=recv_sem, device_id=(x, y, 1 - c), device_id_type=MESH_ID)
        cp.start()
        cp.wait()

    return pl.pallas_call(
        body, out_shape=SDS((n, half, width), g.dtype), in_specs=[HBM_SPEC], out_specs=HBM_SPEC,
        scratch_shapes=[pltpu.SemaphoreType.DMA, pltpu.SemaphoreType.DMA], name=name)(g)


def add_half(g, a, c_idx, name):
    n, rows, width = g.shape
    half = rows // 2
    tile = half // 2
    nt = half // tile

    def body(c_ref, g_ref, a_ref, o_ref, ob_ref):
        v = g_ref[...] + a_ref[...]
        o_ref[...] = v
        ob_ref[...] = v.astype(ob_ref.dtype)

    out_spec = pl.BlockSpec((None, tile, width), lambda s, i, c_ref: (s, i, 0))
    gs = pltpu.PrefetchScalarGridSpec(
        num_scalar_prefetch=1, grid=(n, nt),
        in_specs=[pl.BlockSpec((None, tile, width), lambda s, i, c_ref: (s, c_ref[0] * nt + i, 0)), out_spec],
        out_specs=[out_spec, out_spec])
    return pl.pallas_call(body, grid_spec=gs, out_shape=[SDS((n, half, width), F32), SDS((n, half, width), BF16)],
                          compiler_params=_cparams(("parallel", "parallel")), name=name)(c_idx, g, a)


def exchange_small(small):
    def body(sm_ref, all_ref, send_sems, recv_sems, local_sem):
        x, y, c = _me()
        mine = pltpu.make_async_copy(sm_ref, all_ref.at[0], local_sem)
        mine.start()
        copies = []
        for m in range(1, N_DEV):
            peer = (x ^ ((m >> 2) & 1), y ^ ((m >> 1) & 1), c ^ (m & 1))
            copies.append(pltpu.make_async_remote_copy(
                src_ref=sm_ref, dst_ref=all_ref.at[m], send_sem=send_sems.at[m - 1], recv_sem=recv_sems.at[m - 1],
                device_id=peer, device_id_type=MESH_ID))
        for cp in copies:
            cp.start()
        for cp in copies:
            cp.wait()
        mine.wait()

    return pl.pallas_call(
        body, out_shape=SDS((N_DEV,) + small.shape, small.dtype), in_specs=[HBM_SPEC], out_specs=HBM_SPEC,
        scratch_shapes=[pltpu.SemaphoreType.DMA((N_DEV - 1,)), pltpu.SemaphoreType.DMA((N_DEV - 1,)), pltpu.SemaphoreType.DMA],
        name="exchange_small")(small)


def add_chips(sh, b, k_idx, name):
    n, hf, width = sh.shape
    tile = hf // 2

    def body(k_ref, s_ref, b0, b1, b2, o_ref):
        o_ref[...] = ((s_ref[...] + b0[...].astype(F32)) + b1[...].astype(F32)) + b2[...].astype(F32)

    def bspec(j):
        return pl.BlockSpec((None, tile, width), lambda i, k_ref, j=j: (j, i, 0))

    gs = pltpu.PrefetchScalarGridSpec(
        num_scalar_prefetch=1, grid=(hf // tile,),
        in_specs=[pl.BlockSpec((None, tile, width), lambda i, k_ref: (k_ref[0], i, 0)), bspec(0), bspec(1), bspec(2)],
        out_specs=pl.BlockSpec((tile, width), lambda i, k_ref: (i, 0)))
    return pl.pallas_call(body, grid_spec=gs, out_shape=SDS((hf, width), sh.dtype),
                          compiler_params=_cparams(("parallel",)), name=name)(k_idx, sh, b, b, b)


def sum_small(allsm, me_idx, name):
    _, rows, width = allsm.shape

    def body(me_ref, a_ref, o_ref):
        me = me_ref[0]
        acc = a_ref[me]
        for dev in range(1, N_DEV):
            acc = acc + a_ref[jnp.bitwise_xor(me, dev)]
        o_ref[...] = acc

    gs = pltpu.PrefetchScalarGridSpec(
        num_scalar_prefetch=1, grid=(1,),
        in_specs=[pl.BlockSpec((N_DEV, rows, width), lambda i, me_ref: (0, 0, 0))],
        out_specs=pl.BlockSpec((rows, width), lambda i, me_ref: (0, 0)))
    return pl.pallas_call(body, grid_spec=gs, out_shape=SDS((rows, width), allsm.dtype),
                          compiler_params=_cparams(("arbitrary",)), name=name)(me_idx, allsm)


def join_halves(t, core_idx, name):
    hf, width = t.shape

    def body(t_ref, o_ref, send_sem, recv_sem):
        x, y, c = _me()
        cp = pltpu.make_async_remote_copy(src_ref=t_ref, dst_ref=o_ref, send_sem=send_sem, recv_sem=recv_sem,
                                          device_id=(x, y, 1 - c), device_id_type=MESH_ID)
        cp.start()
        cp.wait()

    theirs = pl.pallas_call(
        body, out_shape=SDS((hf, width), t.dtype), in_specs=[HBM_SPEC], out_specs=HBM_SPEC,
        scratch_shapes=[pltpu.SemaphoreType.DMA, pltpu.SemaphoreType.DMA], name=name)(t)
    return jnp.where(core_idx[0] == 0, jnp.concatenate([t, theirs], axis=0), jnp.concatenate([theirs, t], axis=0))


ROW_PAD = 64


def _pad_rows(rows):
    return -(-rows // ROW_PAD) * ROW_PAD


def _pack_rows(names, shard_shapes, width):
    return _pad_rows(sum((shard_shapes[n][0] * shard_shapes[n][1]) // width for n in names))


def unpack_local(packed, names, shard_shapes, width):
    out, r0 = {}, 0
    for n in names:
        a, b = shard_shapes[n]
        nr = (a * b) // width
        out[n] = packed[r0:r0 + nr].reshape(a, b)
        r0 += nr
    return out


EXACT_IN_GATHER = ("conv_w",)
EXACT_TERMS = 3


def pack_gather(shards, names, width):
    parts = []
    for n in names:
        if n in EXACT_IN_GATHER:
            rest = shards[n].astype(F32)
            for _ in range(EXACT_TERMS):
                term = rest.astype(BF16)
                parts.append(term.reshape(-1, width))
                rest = rest - term.astype(F32)
        else:
            parts.append(shards[n].reshape(-1, width).astype(BF16))
    used = sum(p.shape[0] for p in parts)
    parts.append(jnp.zeros((_pad_rows(used) - used, width), BF16))
    return jnp.concatenate(parts, axis=0)


def unpack_full(gathered, names, shard_shapes, width):
    out, r0 = {}, 0
    for n in names:
        a, b = shard_shapes[n]
        terms = EXACT_TERMS if n in EXACT_IN_GATHER else 1
        nr = (a * b) // width
        pieces = []
        for j in range(N_CHIPS):
            blk = gathered[j, r0:r0 + nr].reshape(a, b)
            for t in range(1, terms):
                blk = blk.astype(F32) + gathered[j, r0 + t * nr:r0 + (t + 1) * nr].reshape(a, b).astype(F32)
            pieces.append(blk)
        out[n] = jnp.concatenate(pieces, axis=SHARD_AXIS[n])
        r0 += terms * nr
    return out


def pack_full(grads, names, shard_shapes, width, pad=True):
    total_rows = _pack_rows(names, shard_shapes, width) if pad else sum(
        (shard_shapes[n][0] * shard_shapes[n][1]) // width for n in names)
    slabs = []
    for j in range(N_CHIPS):
        parts = []
        for n in names:
            a, b = shard_shapes[n]
            ax = SHARD_AXIS[n]
            sz = (a, b)[ax]
            piece = lax.slice_in_dim(grads[n], j * sz, (j + 1) * sz, axis=ax)
            parts.append(piece.reshape(-1, width))
        used = sum(p.shape[0] for p in parts)
        parts.append(jnp.zeros((total_rows - used, width), F32))
        slabs.append(jnp.concatenate(parts, axis=0))
    return jnp.stack(slabs, axis=0)


def _small_layout(sizes, width):
    lay, r = {}, 0
    for n in SMALL_WEIGHTS:
        nr = -(-sizes[n] // width)
        lay[n] = (r, nr, sizes[n])
        r += nr
    assert r <= SMALL_ROWS
    return lay


def pack_small(vals, lay, width):
    rows = []
    for n in SMALL_WEIGHTS:
        r, nr, sz = lay[n]
        v = vals[n].reshape(-1).astype(F32)
        rows.append(jnp.pad(v, (0, nr * width - sz)).reshape(nr, width))
    used = sum(r.shape[0] for r in rows)
    rows.append(jnp.zeros((SMALL_ROWS - used, width), F32))
    return jnp.concatenate(rows, axis=0)


def unpack_small(packed, lay):
    return {n: packed[r:r + nr].reshape(-1)[:sz].reshape(1, sz) for n, (r, nr, sz) in lay.items()}


PART_B = ("w_in", "w_gate")
PART_A = ("w_ff1", "w_ff2", "w_ssd_branch", "w_sb_branch", "w_out", "w_ple_gate", "w_ple", "conv_w")
DIRECT_A = PART_A[:6]


GATHER_EARLY = ("conv_w", "w_ssd_branch", "w_sb_branch", "w_out")
GATHER_LATE = ("w_ff1", "w_ff2", "w_ple_gate", "w_ple")


class Dist(NamedTuple):
    packed_early: object
    packed_late: object
    shard_shapes: dict
    core_idx: object
    chip_idx: object


def build_w_all(w_in, w_gate, dm):
    c0 = dm.DI + dm.CD
    return jnp.concatenate(
        [w_in[:, :c0], w_in[:, c0 + dm.H:], w_gate, w_in[:, c0:c0 + dm.H],
         jnp.zeros((dm.D, DT_PAD - dm.H), w_in.dtype)], axis=1).astype(BF16)


def pack_b(w_in_shard, w_gate_shard):
    return jnp.concatenate([w_in_shard, w_gate_shard], axis=1)


def w_all_from_slabs(slabs, dm):
    n_in = (dm.NA - DT_PAD - 2 * dm.D + dm.H) // N_CHIPS
    c0 = dm.DI + dm.CD

    def w_in_cols(a, b):
        return [slabs[j, :, max(a, j * n_in) - j * n_in:min(b, (j + 1) * n_in) - j * n_in]
                for j in range(N_CHIPS) if max(a, j * n_in) < min(b, (j + 1) * n_in)]

    parts = w_in_cols(0, c0) + w_in_cols(c0 + dm.H, N_CHIPS * n_in) + [slabs[j, :, n_in:] for j in range(N_CHIPS)]
    parts += w_in_cols(c0, c0 + dm.H) + [jnp.zeros((dm.D, DT_PAD - dm.H), slabs.dtype)]
    return jnp.concatenate(parts, axis=1).astype(BF16)


def _w_in_columns(lo, hi, dm):
    c0 = dm.DI + dm.CD
    segments = [(0, c0, 0), (c0, c0 + dm.H, dm.off["dt"]), (c0 + dm.H, c0 + dm.H + 3 * dm.SBW, c0)]
    return [(w0 + max(lo, a) - a, w0 + min(hi, b) - a) for a, b, w0 in segments if max(lo, a) < min(hi, b)]


def slabs_b(dw_all, dm):
    n_in = (dm.NA - DT_PAD - 2 * dm.D + dm.H) // N_CHIPS
    n_gate = 2 * dm.D // N_CHIPS
    slabs = []
    for j in range(N_CHIPS):
        cols = _w_in_columns(j * n_in, (j + 1) * n_in, dm) + [(dm.off["gate"] + j * n_gate, dm.off["gate"] + (j + 1) * n_gate)]
        slabs.append(jnp.concatenate([dw_all[:, a:b] for a, b in cols], axis=1))
    return jnp.stack(slabs, axis=0)


def local_step(x, p, tgt, wf, sm, dm, dist=None):
    s, d = dm.S, dm.D
    off = dm.off
    c0 = dm.DI + dm.CD
    w_all = wf["w_all"] if "w_all" in wf else build_w_all(wf["w_in"], wf["w_gate"], dm)
    g = dm.G
    per_group = lambda v: v.reshape(g, 1, GROUP_HEADS)
    alog, dsk = per_group(sm["a_log"]), per_group(sm["d_skip"])
    b_gate = sm["b_gate"]
    b_ssd, b_sb = b_gate[:, :d], b_gate[:, d:]
    gcol = off["gate"] // d

    (n1,) = row_fwd("norm1", f_norm1, [x], [sm["norm_mix_pre"]], [(d, BF16)])
    proj = matmul(n1, w_all, name="in_proj")
    if dist is None:
        y_sb = attn_fwd(proj, dm, "attn_fwd")
    else:
        y_sb, gathered = attn_fwd(proj, dm, "attn_fwd", rider=gather_rider(dist.packed_early))
        gathered = own_slab(gathered, dist.packed_early, dist.chip_idx)
        wf = {**wf, **unpack_full(gathered, GATHER_EARLY, dist.shard_shapes, d)}
    xbc = conv_fwd(proj, off["xbc"], dm.CD, wf["conv_w"].astype(F32), sm["conv_b"], "conv_fwd")
    dt_raw = proj[:, off["dt"]:off["dt"] + dm.H]
    (dt,) = row_fwd("dt", f_dt, [dt_raw], [sm["dt_bias"]], [(dm.H, F32)])
    dtc = dt.reshape(s, g, GROUP_HEADS).transpose(1, 0, 2)
    dtw = dt.reshape(s, g, GROUP_HEADS).transpose(1, 2, 0)
    if dist is None:
        y_ssd, states = ssd_fwd(proj, xbc, dtc, dtw, alog, dsk, sm["ssd_norm"], dm, "ssd_fwd")
    else:
        y_ssd, states, gathered = ssd_fwd(proj, xbc, dtc, dtw, alog, dsk, sm["ssd_norm"], dm, "ssd_fwd",
                                          rider=gather_rider(dist.packed_late))
        gathered = own_slab(gathered, dist.packed_late, dist.chip_idx)
        wf = {**wf, **unpack_full(gathered, GATHER_LATE, dist.shard_shapes, d)}
    yb_ssd = matmul(y_ssd, wf["w_ssd_branch"], name="ssd_branch")
    yb_sb = matmul(y_sb, wf["w_sb_branch"], name="sb_branch")
    merge_rows = [(proj, d, gcol), (proj, d, gcol + 1), yb_ssd, yb_sb]
    (merged,) = row_fwd("merge", f_merge, merge_rows, [b_ssd, b_sb], [(d, BF16)])
    mo = matmul(merged, wf["w_out"], name="w_out")
    h1, n2 = row_fwd("mix_out", f_mix_out, [x, mo], [sm["norm_mix_post"], sm["norm_ffn_pre"]], [(d, F32), (d, BF16)])
    a1, act = matmul(n2, wf["w_ff1"], name="ff1", finish=lambda v: (v,) + f_relu2(v), out_dtype=(F32, BF16))
    ff = matmul(act, wf["w_ff2"], name="ff2")
    (h2,) = row_fwd("ffn_out", f_ffn_out, [h1, ff], [sm["norm_ffn_post"]], [(d, F32)])
    pg = matmul(h2, wf["w_ple_gate"], name="ple_gate")
    pe = matmul(p, wf["w_ple"], name="ple_emb")

    def reduce_start(gbig, tag):
        from_sibling = swap_halves(gbig, "swap_halves_" + tag)
        pair_sum, pair_sum_bf16 = add_half(gbig, from_sibling, dist.core_idx, "add_half_" + tag)
        return pair_sum, exchange_rider(pair_sum_bf16)

    def reduce_finish(pair_sum, from_chips, tag):
        my_half = add_chips(pair_sum, from_chips, dist.chip_idx, "add_chips_" + tag)
        return join_halves(my_half, dist.core_idx, "join_halves_" + tag)

    gr, reduced = {}, None
    packed_grads = [None if dist is None else jnp.zeros((N_CHIPS, _pack_rows(PART_A, dist.shard_shapes, d), d), F32)]

    def weight_grad(wname, a, dy):
        if dist is None:
            gr[wname] = matmul(a, dy, ta=True, name="d_" + wname)
            return
        row0 = 0
        for n in PART_A[:PART_A.index(wname)]:
            row0 += (dist.shard_shapes[n][0] * dist.shard_shapes[n][1]) // d
        packed_grads[0] = matmul(a, dy, ta=True, name="d_" + wname,
                                 slab=SlabOut(packed_grads[0], row0, dist.shard_shapes[wname][0], SHARD_AXIS[wname]))

    (dh2_a, dpg, dpe), (gr["norm_ple_post"], loss_cols) = row_bwd(
        "ple_loss", f_ple_loss, [h2, pg, pe, tgt], [sm["norm_ple_post"]], [None], [F32, BF16, BF16, None], primal_sum=True)
    loss = jnp.sum(loss_cols)
    gr["w_ple"] = matmul(p, dpe, ta=True, name="d_w_ple")
    weight_grad("w_ple_gate", h2, dpg)
    dh2_b = matmul(dpg, wf["w_ple_gate"], tb=True, name="d_h2")
    (dh1_a, dff), (gr["norm_ffn_post"],) = row_bwd(
        "ffn_out_bwd", f_ffn_out, [h1, ff], [sm["norm_ffn_post"]], [[dh2_a, dh2_b]], [F32, BF16])
    weight_grad("w_ff2", act, dff)
    (da1,) = matmul(dff, wf["w_ff2"], tb=True, name="d_act", sides=(a1,),
                    finish=lambda dact, a1_tile: (dact * (2.0 * jnp.maximum(a1_tile, 0.0)),), out_dtype=(BF16,))
    weight_grad("w_ff1", n2, da1)
    dn2 = matmul(da1, wf["w_ff1"], tb=True, name="d_n2")
    (dx_a, dmo), (gr["norm_mix_post"], gr["norm_ffn_pre"]) = row_bwd(
        "mix_out_bwd", f_mix_out, [x, mo], [sm["norm_mix_post"], sm["norm_ffn_pre"]], [[dh1_a], [dn2]], [F32, BF16])
    weight_grad("w_out", merged, dmo)
    dmerged = matmul(dmo, wf["w_out"], tb=True, name="d_merged")
    (dgp_ssd, dgp_sb, dyb_ssd, dyb_sb), (db_ssd, db_sb) = row_bwd(
        "merge_bwd", f_merge, merge_rows, [b_ssd, b_sb], [[dmerged]], [BF16, BF16, BF16, BF16])
    gr["b_gate"] = jnp.concatenate([db_ssd, db_sb], axis=1)
    weight_grad("w_ssd_branch", y_ssd, dyb_ssd)
    weight_grad("w_sb_branch", y_sb, dyb_sb)
    dy_ssd = matmul(dyb_ssd, wf["w_ssd_branch"], tb=True, name="d_y_ssd")
    dy_sb = matmul(dyb_sb, wf["w_sb_branch"], tb=True, name="d_y_sb")
    dproj, dxs, dbm, dcm, ddtc, ddtw, dalog, ddsk, gr["ssd_norm"] = ssd_bwd(
        proj, xbc, dtc, dtw, alog, dsk, sm["ssd_norm"], states, dy_ssd, dm, "ssd_bwd")
    gr["a_log"], gr["d_skip"] = (v.reshape(1, dm.H) for v in (dalog, ddsk))
    ddt_post = (ddtc.transpose(1, 0, 2) + ddtw.transpose(2, 0, 1)).reshape(s, dm.H)
    (ddt,), (gr["dt_bias"],) = row_bwd("dt_bwd", f_dt, [dt_raw], [sm["dt_bias"]], [[ddt_post]], [BF16])
    conv_w32 = wf["conv_w"].astype(F32)
    dproj, dw_x, dcb_x = conv_bwd(proj, off["xbc"], dxs, 0, conv_w32, sm["conv_b"], dproj, "conv_bwd_x")
    dproj, dw_b, dcb_b = conv_bwd(proj, off["xbc"], dbm, dm.DI, conv_w32, sm["conv_b"], dproj, "conv_bwd_b")
    dproj, dw_c, dcb_c = conv_bwd(proj, off["xbc"], dcm, dm.DI + g * D_STATE, conv_w32, sm["conv_b"], dproj, "conv_bwd_c")
    gr["conv_w"] = jnp.concatenate([dw_x, dw_b, dw_c], axis=1)
    gr["conv_b"] = jnp.concatenate([dcb_x, dcb_b, dcb_c], axis=1)
    if dist is None:
        dq, dk, dv = attn_bwd(proj, dy_sb, dm, "attn_bwd")
    else:
        rest = tuple(n for n in PART_A if n not in DIRECT_A)
        row0 = sum((dist.shard_shapes[n][0] * dist.shard_shapes[n][1]) // d for n in DIRECT_A)
        small_rows = pack_full(gr, rest, dist.shard_shapes, d, pad=False)
        gbig_a = lax.dynamic_update_slice(packed_grads[0], small_rows, (0, row0, 0))
        pair_sum_a, rider_a = reduce_start(gbig_a, "a")
        dq, dk, dv, from_chips_a = attn_bwd(proj, dy_sb, dm, "attn_bwd", rider=rider_a)
        reduced = unpack_local(reduce_finish(pair_sum_a, from_chips_a, "a"), PART_A, dist.shard_shapes, d)
    col = off["q"]
    for piece in (dq, dk, dv, dgp_ssd, dgp_sb, jnp.concatenate([ddt, jnp.zeros((s, DT_PAD - dm.H), BF16)], axis=1)):
        dproj = lax.dynamic_update_slice(dproj, piece, (0, col))
        col += piece.shape[1]
    dw_all = matmul(n1, dproj, ta=True, name="d_w_all")
    if dist is None:
        gr["w_in"] = jnp.concatenate(
            [dw_all[:, :c0], dw_all[:, off["dt"]:off["dt"] + dm.H], dw_all[:, c0:c0 + 3 * dm.SBW]], axis=1)
        gr["w_gate"] = dw_all[:, off["gate"]:off["gate"] + 2 * d]
        dn1 = matmul(dproj, w_all, tb=True, name="d_n1")
    else:
        pair_sum_b, rider_b = reduce_start(slabs_b(dw_all, dm), "b")
        dn1, from_chips_b = matmul(dproj, w_all, tb=True, name="d_n1", rider=rider_b)
        reduced_b = reduce_finish(pair_sum_b, from_chips_b, "b")
        n_in = dist.shard_shapes["w_in"][1]
        reduced.update(w_in=reduced_b[:, :n_in], w_gate=reduced_b[:, n_in:])
    (grad_x,), (gr["norm_mix_pre"],) = row_bwd("norm1_bwd", lambda u, w: (_rms(u, w), u), [x], [sm["norm_mix_pre"]],
                                               [[dn1], [dx_a]], [F32])
    return loss, grad_x, gr, reduced


def kernel(x, p, norm_mix_pre, w_in, conv_w, conv_b, dt_bias, a_log, d_skip, ssd_norm, w_ssd_branch, w_sb_branch, w_gate, b_gate, w_out, norm_mix_post, norm_ffn_pre, w_ff1, w_ff2, norm_ffn_post, w_ple, w_ple_gate, norm_ple_post, loss_target, m_norm_mix_pre, m_w_in, m_conv_w, m_conv_b, m_dt_bias, m_a_log, m_d_skip, m_ssd_norm, m_w_ssd_branch, m_w_sb_branch, m_w_gate, m_b_gate, m_w_out, m_norm_mix_post, m_norm_ffn_pre, m_w_ff1, m_w_ff2, m_norm_ffn_post, m_w_ple, m_w_ple_gate, m_norm_ple_post, v_norm_mix_pre, v_w_in, v_conv_w, v_conv_b, v_dt_bias, v_a_log, v_d_skip, v_ssd_norm, v_w_ssd_branch, v_w_sb_branch, v_w_gate, v_b_gate, v_w_out, v_norm_mix_post, v_norm_ffn_pre, v_w_ff1, v_w_ff2, v_norm_ffn_post, v_w_ple, v_w_ple_gate, v_norm_ple_post):
    loc = dict(locals())
    unbatch = lambda a: a[0] if a.ndim == 3 else a
    w = {n: unbatch(loc[n]) for n in ALL_WEIGHTS}
    m = {n: unbatch(loc["m_" + n]) for n in ALL_WEIGHTS}
    v = {n: unbatch(loc["v_" + n]) for n in ALL_WEIGHTS}
    xs, ps, tgt = x[0], p[0, 0], loss_target[0]
    s, d = xs.shape
    di = w["w_ssd_branch"].shape[0] * N_CHIPS
    cd = w["conv_b"].shape[1]
    dm = Dims(S=s, D=d, DI=di, H=w["dt_bias"].shape[1], G=(cd - di) // (2 * D_STATE), CD=cd,
              SBW=w["w_sb_branch"].shape[0] * N_CHIPS, DFF=w["w_ff2"].shape[0] * N_CHIPS, PLE=ps.shape[1])
    ix, iy, ic = lax.axis_index("x"), lax.axis_index("y"), lax.axis_index("c")
    chip_idx = jnp.reshape(2 * ix + iy, (1,)).astype(jnp.int32)
    core_idx = jnp.reshape(ic, (1,)).astype(jnp.int32)
    dev_idx = jnp.reshape(4 * ix + 2 * iy + ic, (1,)).astype(jnp.int32)

    shard_shapes = {n: w[n].shape for n in BIG_WEIGHTS}
    packed_b = pack_b(w["w_in"], w["w_gate"]).astype(BF16)
    wf = {"w_all": w_all_from_slabs(own_slab(gather_weights(packed_b), packed_b, chip_idx), dm)}
    sm = {n: w[n] for n in SMALL_WEIGHTS}
    dist = Dist(packed_early=pack_gather(w, GATHER_EARLY, d), packed_late=pack_gather(w, GATHER_LATE, d),
                shard_shapes=shard_shapes, core_idx=core_idx, chip_idx=chip_idx)

    loss_part, grad_x, gr, grads = local_step(xs, ps, tgt, wf, sm, dm, dist)
    loss = lax.psum(loss_part, ("x", "y", "c"))

    lay = _small_layout({n: w[n].shape[1] for n in SMALL_WEIGHTS}, d)
    gs_red = sum_small(exchange_small(pack_small(gr, lay, d)), dev_idx, "sum_small")
    grads.update(unpack_small(gs_red, lay))
    delta, new_m, new_v = {}, {}, {}
    for n in BIG_WEIGHTS:
        delta[n], new_m[n], new_v[n] = adamw(w[n], grads[n], m[n], v[n], "adamw_" + n)
    d_sm, nm_sm, nv_sm = adamw(pack_small(w, lay, d), gs_red, pack_small(m, lay, d), pack_small(v, lay, d), "adamw_small")
    for out, packed in ((delta, d_sm), (new_m, nm_sm), (new_v, nv_sm)):
        out.update(unpack_small(packed, lay))

    def leaves(vals):
        return [vals[n][None] if n in BIG_WEIGHTS else vals[n] for n in ALL_WEIGHTS]

    return (loss, grad_x[None], *leaves(grads), *leaves(delta), *leaves(new_m), *leaves(new_v))
```
